```python
import math
import jax
import jax.numpy as jnp
from jax import lax
import numpy as np

D_MODEL = 1024
BATCH = 2
SEQ = 8192
DEPTH = 1

MIX_WIDTH = D_MODEL
FOX_WIDTH = MIX_WIDTH // 2
FOX_HEAD_DIM = 64
FOX_HEADS = FOX_WIDTH // FOX_HEAD_DIM
GLA_WIDTH = MIX_WIDTH - FOX_WIDTH
GLA_HEADS = 4
GLA_VAL_DIM = GLA_WIDTH // GLA_HEADS
GLA_KEY_DIM = GLA_VAL_DIM // 2
GLA_KEY_WIDTH = GLA_HEADS * GLA_KEY_DIM
GLA_GATE_RANK = 16
GLA_GATE_TEMP = 16.0
GLA_CHUNK = 64
Q_BLOCK = 128
D_FF = ((8 * D_MODEL // 3 + 255) // 256) * 256
LN_EPS = 1e-5
RMS_EPS = 1e-6
N_MOD = 6

DEEPNORM_ALPHA = (2.0 * DEPTH) ** 0.25
DEEPNORM_BETA = (8.0 * DEPTH) ** -0.25

IN_SPLITS = [
    FOX_WIDTH,
    FOX_WIDTH,
    FOX_WIDTH,
    FOX_HEADS,
    GLA_KEY_WIDTH,
    GLA_KEY_WIDTH,
    GLA_WIDTH,
    GLA_GATE_RANK,
    GLA_WIDTH,
]
IN_WIDTH = sum(IN_SPLITS)

kernel_name = "hymba_fox_gla_deepnorm_adaln_layer"


def layer_norm(x, g, b):
    xf = x.astype(jnp.float32)
    mu = jnp.mean(xf, axis=-1, keepdims=True)
    var = jnp.mean(jnp.square(xf - mu), axis=-1, keepdims=True)
    y = (xf - mu) * lax.rsqrt(var + LN_EPS)
    return (y * g.astype(jnp.float32) + b.astype(jnp.float32)).astype(x.dtype)


def fox_attention(q, k, v, f_logit, b_f):
    B, S, H, Dh = q.shape
    scale = 1.0 / math.sqrt(Dh)
    log_f = jax.nn.log_sigmoid((f_logit + b_f).astype(jnp.float32))
    dcum = jnp.cumsum(log_f, axis=1).transpose(0, 2, 1)
    qh = q.transpose(0, 2, 1, 3)
    kh = k.transpose(0, 2, 1, 3)
    vh = v.transpose(0, 2, 1, 3)
    nq = S // Q_BLOCK
    q_blocks = qh.reshape(B, H, nq, Q_BLOCK, Dh).transpose(2, 0, 1, 3, 4)
    d_blocks = dcum.reshape(B, H, nq, Q_BLOCK).transpose(2, 0, 1, 3)
    k_pos = jnp.arange(S)

    def one_block(args):
        qb, dqb, i = args
        s = jnp.einsum('bhqd,bhkd->bhqk', qb, kh).astype(jnp.float32) * scale
        s = s + dqb[..., None] - dcum[:, :, None, :]
        q_pos = i * Q_BLOCK + jnp.arange(Q_BLOCK)
        s = jnp.where(k_pos[None, :] <= q_pos[:, None], s, -1e30)
        p = jax.nn.softmax(s, axis=-1).astype(vh.dtype)
        return jnp.einsum('bhqk,bhkd->bhqd', p, vh)

    o = lax.map(one_block, (q_blocks, d_blocks, jnp.arange(nq)))
    return o.transpose(1, 0, 3, 2, 4).reshape(B, S, H * Dh)


def gla_attention(q, k, v, a_low, w_a2, b_a, r, g_norm):
    B, S, H, dk = q.shape
    dv = v.shape[-1]
    out_dtype = v.dtype
    C = GLA_CHUNK
    nc = S // C
    log_a = jax.nn.log_sigmoid((a_low @ w_a2 + b_a).astype(jnp.float32)) / GLA_GATE_TEMP
    log_a = log_a.reshape(B, S, H, dk)

    def chunks(t):
        return t.astype(jnp.float32).reshape(B, nc, C, H, -1).transpose(0, 3, 1, 2, 4)

    qc = chunks(q) * (dk ** -0.5)
    kc = chunks(k)
    vc = chunks(v)
    b = jnp.cumsum(chunks(log_a), axis=3)
    b_last = b[:, :, :, -1:, :]
    q_dec = qc * jnp.exp(b)
    k_inv = kc * jnp.exp(-b)
    k_to_end = kc * jnp.exp(b_last - b)
    causal = jnp.tril(jnp.ones((C, C), dtype=bool))
    a_intra = jnp.einsum('bhncd,bhnsd->bhncs', q_dec, k_inv)
    a_intra = jnp.where(causal, a_intra, 0.0)
    o_intra = jnp.einsum('bhncs,bhnse->bhnce', a_intra, vc)
    kv_chunk = jnp.einsum('bhncd,bhnce->bhnde', k_to_end, vc)
    decay = jnp.exp(b_last[:, :, :, 0, :])

    def step(state, inp):
        dec, kv = inp
        return dec[..., None] * state + kv, state

    init = jnp.zeros((B, H, dk, dv), jnp.float32)
    _, states = lax.scan(step, init, (decay.transpose(2, 0, 1, 3), kv_chunk.transpose(2, 0, 1, 3, 4)))
    states = states.transpose(1, 2, 0, 3, 4)
    o_inter = jnp.einsum('bhncd,bhnde->bhnce', q_dec, states)
    o = (o_intra + o_inter).transpose(0, 2, 3, 1, 4).reshape(B, S, H, dv)
    o = o * lax.rsqrt(jnp.mean(jnp.square(o), axis=-1, keepdims=True) + RMS_EPS)
    o = o.reshape(B, S, H * dv) * g_norm.astype(jnp.float32)
    return (o * jax.nn.silu(r.astype(jnp.float32))).astype(out_dtype)


def setup_inputs(seed: int = 0) -> dict:
    key = jax.random.key(seed)
    ks = jax.random.split(key, 17)
    f32 = jnp.float32
    nrm = lambda k, shape, s: (jax.random.normal(k, shape, f32) * s)
    return {
        "x": nrm(ks[0], (BATCH, SEQ, D_MODEL), 1.0),
        "c": nrm(ks[1], (BATCH, D_MODEL), 1.0),
        "w_c": nrm(ks[2], (D_MODEL, N_MOD * D_MODEL), 0.3 * D_MODEL ** -0.5),
        "b_c": nrm(ks[3], (N_MOD * D_MODEL,), 0.02),
        "w_in": nrm(ks[4], (D_MODEL, IN_WIDTH), D_MODEL ** -0.5),
        "b_f": nrm(ks[5], (FOX_HEADS,), 0.1),
        "w_a2": nrm(ks[6], (GLA_GATE_RANK, GLA_KEY_WIDTH), GLA_GATE_RANK ** -0.5),
        "b_a": nrm(ks[7], (GLA_KEY_WIDTH,), 0.1),
        "g_gla": 1.0 + nrm(ks[8], (GLA_WIDTH,), 0.02),
        "w_o": nrm(ks[9], (MIX_WIDTH, D_MODEL), DEEPNORM_BETA * MIX_WIDTH ** -0.5),
        "ln1_g": 1.0 + nrm(ks[10], (D_MODEL,), 0.02),
        "ln1_b": nrm(ks[11], (D_MODEL,), 0.02),
        "w_gate": nrm(ks[12], (D_MODEL, D_FF), D_MODEL ** -0.5),
        "w_up": nrm(ks[13], (D_MODEL, D_FF), D_MODEL ** -0.5),
        "w_down": nrm(ks[14], (D_FF, D_MODEL), DEEPNORM_BETA * D_FF ** -0.5),
        "ln2_g": 1.0 + nrm(ks[15], (D_MODEL,), 0.02),
        "ln2_b": nrm(ks[16], (D_MODEL,), 0.02),
    }


def reference(x, c, w_c, b_c, w_in, b_f, w_a2, b_a, g_gla, w_o, ln1_g, ln1_b,
              w_gate, w_up, w_down, ln2_g, ln2_b):
    B, S, D = x.shape
    mod = (c @ w_c + b_c).reshape(B, N_MOD, D)
    shift_m, scale_m, gate_m = mod[:, 0, None, :], mod[:, 1, None, :], mod[:, 2, None, :]
    shift_f, scale_f, gate_f = mod[:, 3, None, :], mod[:, 4, None, :], mod[:, 5, None, :]
    split_idx = list(np.cumsum(IN_SPLITS)[:-1])

    for _ in range(DEPTH):
        u = x * (1.0 + scale_m) + shift_m
        proj = u @ w_in
        fq, fk, fv, ff, gq, gk, gv, ga, gr = jnp.split(proj, split_idx, axis=-1)
        fox_out = fox_attention(
            fq.reshape(B, S, FOX_HEADS, FOX_HEAD_DIM),
            fk.reshape(B, S, FOX_HEADS, FOX_HEAD_DIM),
            fv.reshape(B, S, FOX_HEADS, FOX_HEAD_DIM),
            ff, b_f)
        gla_out = gla_attention(
            gq.reshape(B, S, GLA_HEADS, GLA_KEY_DIM),
            gk.reshape(B, S, GLA_HEADS, GLA_KEY_DIM),
            gv.reshape(B, S, GLA_HEADS, GLA_VAL_DIM),
            ga, w_a2, b_a, gr, g_gla)
        y = jnp.concatenate([fox_out, gla_out], axis=-1) @ w_o
        x = layer_norm(DEEPNORM_ALPHA * x + (1.0 + gate_m) * y, ln1_g, ln1_b)

        u2 = x * (1.0 + scale_f) + shift_f
        h = jax.nn.silu(u2 @ w_gate) * (u2 @ w_up)
        y2 = h @ w_down
        x = layer_norm(DEEPNORM_ALPHA * x + (1.0 + gate_f) * y2, ln2_g, ln2_b)
    return x
```

```python
import functools
import math

import jax
import jax.numpy as jnp
from jax import lax
from jax.experimental import pallas as pl
from jax.experimental.pallas import tpu as pltpu

F32 = jnp.float32
BF16 = jnp.bfloat16

D_MODEL = 1024
FOX_WIDTH = 512
FOX_HEAD_DIM = 64
FOX_HEADS = 8
GLA_WIDTH = 512
GLA_HEADS = 4
GLA_VAL_DIM = 128
GLA_KEY_DIM = 64
GLA_KEY_WIDTH = 256
GLA_GATE_RANK = 16
GLA_GATE_TEMP = 16.0
GLA_CHUNK = 64
D_FF = 2816
LN_EPS = 1e-5
RMS_EPS = 1e-6
N_MOD = 6
DEEPNORM_ALPHA = 2.0 ** 0.25
IN_SPLITS = (FOX_WIDTH, FOX_WIDTH, FOX_WIDTH, FOX_HEADS, GLA_KEY_WIDTH, GLA_KEY_WIDTH,
             GLA_WIDTH, GLA_GATE_RANK, GLA_WIDTH)

LANES = 128
SMALL_W = LANES
VMEM_LIMIT = 56 * 1024 * 1024

MOD_TN = 1536
INPROJ_TM = 512
FOX_TQ = 512
FOX_TK = 512
GLA_TS = 512
GLA_PAIR = 2 * GLA_CHUNK
TAIL_TM = 512

NEG_BIG = -1e30


def _log_sigmoid(z):
    return jnp.minimum(z, 0.0) - jnp.log1p(jnp.exp(-jnp.abs(z)))


def _split3(a):
    hi = a.astype(BF16)
    r1 = a - hi.astype(F32)
    mid = r1.astype(BF16)
    lo = (r1 - mid.astype(F32)).astype(BF16)
    return hi, mid, lo


def _dot(a, b):
    return jnp.dot(a, b, preferred_element_type=F32)


def _tri_left(tri, a):
    hi, mid, lo = _split3(a)
    return _dot(tri, hi) + _dot(tri, mid) + _dot(tri, lo)


def _tri_right(a, tri):
    hi, mid, lo = _split3(a)
    return _dot(hi, tri) + _dot(mid, tri) + _dot(lo, tri)


def _mod_kernel(c_ref, w_ref, b_ref, o_ref):
    o_ref[...] = jnp.dot(c_ref[...], w_ref[...], preferred_element_type=F32,
                         precision=lax.Precision.HIGHEST) + b_ref[...]


def _modulation(c, w_c, b_c):
    B, D = c.shape
    N = w_c.shape[1]
    rows = 8
    c_pad = jnp.zeros((rows, D), F32).at[:B].set(c)
    out = pl.pallas_call(
        _mod_kernel,
        grid=(N // MOD_TN,),
        in_specs=[
            pl.BlockSpec((rows, D), lambda j: (0, 0)),
            pl.BlockSpec((D, MOD_TN), lambda j: (0, j)),
            pl.BlockSpec((1, MOD_TN), lambda j: (0, j)),
        ],
        out_specs=pl.BlockSpec((rows, MOD_TN), lambda j: (0, j)),
        out_shape=jax.ShapeDtypeStruct((rows, N), F32),
        compiler_params=pltpu.CompilerParams(
            dimension_semantics=("arbitrary",), vmem_limit_bytes=VMEM_LIMIT),
        name="adaln_mod",
    )(c_pad, w_c, b_c.reshape(1, N))
    return out[:B].reshape(B, N_MOD, D)


_C_FQ, _C_FK, _C_FV = 0, 512, 1024
_C_GQ, _C_GK, _C_GV, _C_GR, _C_SM = 1536, 1792, 2048, 2560, 3072
_W_COLS = _C_SM + SMALL_W


def _inproj_kernel(x_ref, mod_ref, w_ref, bf_ref,
                   fq_ref, fk_ref, fv_ref, gq_ref, gk_ref, gv_ref, gr_ref, sm_ref,
                   dcol_ref, drow_ref, carry_ref):
    tm = x_ref.shape[1]

    @pl.when(pl.program_id(1) == 0)
    def _():
        carry_ref[...] = jnp.zeros_like(carry_ref)

    m = mod_ref[0]
    u = (x_ref[0] * (1.0 + m[1:2]) + m[0:1]).astype(BF16)

    def proj(lo, width):
        return _dot(u, w_ref[:, lo:lo + width])

    fq_ref[0] = (proj(_C_FQ, FOX_WIDTH) * (FOX_HEAD_DIM ** -0.5)).astype(BF16)
    fk_ref[0] = proj(_C_FK, FOX_WIDTH).astype(BF16)
    fv_ref[0] = proj(_C_FV, FOX_WIDTH).astype(BF16)
    gq_ref[0] = proj(_C_GQ, GLA_KEY_WIDTH) * (GLA_KEY_DIM ** -0.5)
    gk_ref[0] = proj(_C_GK, GLA_KEY_WIDTH)
    gv_ref[0] = proj(_C_GV, GLA_WIDTH).astype(BF16)
    gr_ref[0] = proj(_C_GR, GLA_WIDTH)
    small = proj(_C_SM, SMALL_W)
    sm_ref[0] = small

    log_f = _log_sigmoid(small + bf_ref[...])
    r = lax.broadcasted_iota(jnp.int32, (tm, tm), 0)
    c = lax.broadcasted_iota(jnp.int32, (tm, tm), 1)
    tril = jnp.where(c <= r, 1.0, 0.0).astype(BF16)
    dcum = _tri_left(tril, log_f) + carry_ref[...]
    carry_ref[...] = dcum[tm - 1:tm, :]
    dcol_ref[0] = dcum
    drow_ref[0] = dcum.T[:FOX_HEADS, :]


def _in_projection(x, mod, w_all, b_f_pad):
    B, S, D = x.shape
    tm = INPROJ_TM
    tok = lambda w: pl.BlockSpec((1, tm, w), lambda b, s: (b, s, 0))
    outs = pl.pallas_call(
        _inproj_kernel,
        grid=(B, S // tm),
        in_specs=[
            tok(D),
            pl.BlockSpec((1, N_MOD, D), lambda b, s: (b, 0, 0)),
            pl.BlockSpec((D, _W_COLS), lambda b, s: (0, 0)),
            pl.BlockSpec((1, SMALL_W), lambda b, s: (0, 0)),
        ],
        out_specs=[
            tok(FOX_WIDTH), tok(FOX_WIDTH), tok(FOX_WIDTH),
            tok(GLA_KEY_WIDTH), tok(GLA_KEY_WIDTH), tok(GLA_WIDTH), tok(GLA_WIDTH),
            tok(SMALL_W), tok(SMALL_W),
            pl.BlockSpec((1, FOX_HEADS, tm), lambda b, s: (b, 0, s)),
        ],
        out_shape=[
            jax.ShapeDtypeStruct((B, S, FOX_WIDTH), BF16),
            jax.ShapeDtypeStruct((B, S, FOX_WIDTH), BF16),
            jax.ShapeDtypeStruct((B, S, FOX_WIDTH), BF16),
            jax.ShapeDtypeStruct((B, S, GLA_KEY_WIDTH), F32),
            jax.ShapeDtypeStruct((B, S, GLA_KEY_WIDTH), F32),
            jax.ShapeDtypeStruct((B, S, GLA_WIDTH), BF16),
            jax.ShapeDtypeStruct((B, S, GLA_WIDTH), F32),
            jax.ShapeDtypeStruct((B, S, SMALL_W), F32),
            jax.ShapeDtypeStruct((B, S, SMALL_W), F32),
            jax.ShapeDtypeStruct((B, FOX_HEADS, S), F32),
        ],
        scratch_shapes=[pltpu.VMEM((1, SMALL_W), F32)],
        compiler_params=pltpu.CompilerParams(
            dimension_semantics=("arbitrary", "arbitrary"), vmem_limit_bytes=VMEM_LIMIT),
        name="in_proj",
    )(x, mod, w_all, b_f_pad)
    return outs


def _fox_kernel(q_ref, k_ref, v_ref, dq_ref, dk_ref, o_ref, acc_ref, m_ref, l_ref):
    tq, tk = FOX_TQ, FOX_TK
    hp = pl.program_id(1)
    qi = pl.program_id(2)
    half = FOX_HEAD_DIM

    lane = lax.broadcasted_iota(jnp.int32, (tq, LANES), 1)
    low = lane < half
    q = q_ref[0]
    zero = jnp.zeros_like(q)
    qs = (jnp.where(low, q, zero), jnp.where(low, zero, q))
    dqa = dq_ref[0]
    dqs = tuple(jnp.sum(jnp.where(lane == 2 * hp + h, dqa, 0.0), axis=1, keepdims=True)
                for h in range(2))

    m_ref[...] = jnp.full_like(m_ref, NEG_BIG)
    l_ref[...] = jnp.zeros_like(l_ref)
    acc_ref[...] = jnp.zeros_like(acc_ref)

    def step(j, masked):
        off = pl.multiple_of(j * tk, tk)
        k = k_ref[0, pl.ds(off, tk), :]
        v = v_ref[0, pl.ds(off, tk), :]
        dk = dk_ref[0, 0, :, pl.ds(off, tk)]
        for h in range(2):
            s = lax.dot_general(qs[h], k, (((1,), (1,)), ((), ())),
                                preferred_element_type=F32)
            s = s + dqs[h] - dk[h:h + 1]
            if masked:
                r = lax.broadcasted_iota(jnp.int32, (tq, tk), 0)
                c = lax.broadcasted_iota(jnp.int32, (tq, tk), 1)
                s = jnp.where(c <= r, s, NEG_BIG)
            m_prev = m_ref[h]
            m_new = jnp.maximum(m_prev, jnp.max(s, axis=1, keepdims=True))
            alpha = jnp.exp(m_prev - m_new)
            p = jnp.exp(s - m_new)
            l_ref[h] = alpha * l_ref[h] + jnp.sum(p, axis=1, keepdims=True)
            acc_ref[h] = alpha * acc_ref[h] + _dot(p.astype(BF16), v)
            m_ref[h] = m_new

    def body(j, carry):
        step(j, False)
        return carry

    lax.fori_loop(0, qi, body, 0)
    step(qi, True)

    o = jnp.where(low, acc_ref[0] / l_ref[0], acc_ref[1] / l_ref[1])
    o_ref[0] = o.astype(o_ref.dtype)


def _fox_attention(fq, fk, fv, dcol, drow):
    B, S, _ = fq.shape
    tq = FOX_TQ
    n_pairs = FOX_HEADS // 2
    drow4 = drow.reshape(B, n_pairs, 2, S)
    return pl.pallas_call(
        _fox_kernel,
        grid=(B, n_pairs, S // tq),
        in_specs=[
            pl.BlockSpec((1, tq, LANES), lambda b, p, i: (b, i, p)),
            pl.BlockSpec((1, S, LANES), lambda b, p, i: (b, 0, p)),
            pl.BlockSpec((1, S, LANES), lambda b, p, i: (b, 0, p)),
            pl.BlockSpec((1, tq, SMALL_W), lambda b, p, i: (b, i, 0)),
            pl.BlockSpec((1, 1, 2, S), lambda b, p, i: (b, p, 0, 0)),
        ],
        out_specs=pl.BlockSpec((1, tq, LANES), lambda b, p, i: (b, i, p)),
        out_shape=jax.ShapeDtypeStruct((B, S, FOX_WIDTH), BF16),
        scratch_shapes=[
            pltpu.VMEM((2, tq, LANES), F32),
            pltpu.VMEM((2, tq, 1), F32),
            pltpu.VMEM((2, tq, 1), F32),
        ],
        compiler_params=pltpu.CompilerParams(
            dimension_semantics=("arbitrary", "arbitrary", "arbitrary"),
            vmem_limit_bytes=VMEM_LIMIT),
        name="fox_attn",
    )(fq, fk, fv, dcol, drow4)


def _gla_kernel(q_ref, k_ref, v_ref, r_ref, sm_ref, wa_ref, ba_ref, g_ref, o_ref, state_ref):
    ts = q_ref.shape[1]
    C, P = GLA_CHUNK, GLA_PAIR
    dk, dv, H = GLA_KEY_DIM, GLA_VAL_DIM, GLA_HEADS
    KW = GLA_KEY_WIDTH

    @pl.when(pl.program_id(1) == 0)
    def _():
        state_ref[...] = jnp.zeros_like(state_ref)

    la_all = _log_sigmoid(_dot(sm_ref[0].astype(BF16), wa_ref[...]) + ba_ref[...]) \
        * (1.0 / GLA_GATE_TEMP)

    ri = lax.broadcasted_iota(jnp.int32, (P, P), 0)
    ci = lax.broadcasted_iota(jnp.int32, (P, P), 1)
    same_chunk = (ri < C) == (ci < C)
    causal = same_chunk & (ci <= ri)
    tril2 = jnp.where(causal, 1.0, 0.0).astype(BF16)
    triu2 = jnp.where(same_chunk & (ri <= ci), 1.0, 0.0).astype(BF16)

    lane_q = lax.broadcasted_iota(jnp.int32, (P, KW), 1)
    lane_t = lax.broadcasted_iota(jnp.int32, (KW, P), 1)
    first = lane_t < C

    for pi in range(ts // P):
        rows = slice(pi * P, (pi + 1) * P)
        la = la_all[rows]
        b = _tri_left(tril2, la)
        bT = _tri_right(la.T, triu2)
        bl0 = bT[:, C - 1:C]
        bl1 = bT[:, P - 1:P]
        q_dec = q_ref[0, rows, :] * jnp.exp(b)
        kT = k_ref[0, rows, :].T
        k_invT = (kT * jnp.exp(-bT)).astype(BF16)
        k_teT = kT * jnp.exp(jnp.where(first, bl0, bl1) - bT)
        zt = jnp.zeros_like(k_teT)
        k_te0 = jnp.where(first, k_teT, zt).astype(BF16)
        k_te1 = jnp.where(first, zt, k_teT).astype(BF16)
        dec0 = jnp.exp(bl0)
        dec1 = jnp.exp(bl1)

        vs = [v_ref[0, rows, h * dv:(h + 1) * dv] for h in range(H)]
        kv0 = jnp.concatenate([_dot(k_te0[h * dk:(h + 1) * dk, :], vs[h]) for h in range(H)], axis=0)
        kv1 = jnp.concatenate([_dot(k_te1[h * dk:(h + 1) * dk, :], vs[h]) for h in range(H)], axis=0)
        s0 = state_ref[...]
        s1 = dec0 * s0 + kv0
        state_ref[...] = dec1 * s1 + kv1
        s0b = s0.astype(BF16)
        s1b = s1.astype(BF16)

        zq = jnp.zeros_like(q_dec)
        for h in range(H):
            hv = slice(h * dv, (h + 1) * dv)
            qm = jnp.where((lane_q >= h * dk) & (lane_q < (h + 1) * dk), q_dec, zq).astype(BF16)
            a = _dot(qm, k_invT)
            a = jnp.where(causal, a, 0.0).astype(BF16)
            o_inter = jnp.concatenate([_dot(qm[:C], s0b), _dot(qm[C:], s1b)], axis=0)
            o = _dot(a, vs[h]) + o_inter
            o = o * lax.rsqrt(jnp.mean(o * o, axis=-1, keepdims=True) + RMS_EPS)
            rr = r_ref[0, rows, hv]
            o = o * g_ref[:, hv] * (rr * (1.0 / (1.0 + jnp.exp(-rr))))
            o_ref[0, rows, hv] = o.astype(o_ref.dtype)


def _gla_attention(gq, gk, gv, gr, small, w_a2_pad, b_a, g_gla):
    B, S, _ = gq.shape
    ts = GLA_TS
    tok = lambda w: pl.BlockSpec((1, ts, w), lambda b, s: (b, s, 0))
    full = lambda r, c: pl.BlockSpec((r, c), lambda b, s: (0, 0))
    return pl.pallas_call(
        _gla_kernel,
        grid=(B, S // ts),
        in_specs=[
            tok(GLA_KEY_WIDTH), tok(GLA_KEY_WIDTH), tok(GLA_WIDTH), tok(GLA_WIDTH), tok(SMALL_W),
            full(SMALL_W, GLA_KEY_WIDTH), full(1, GLA_KEY_WIDTH), full(1, GLA_WIDTH),
        ],
        out_specs=tok(GLA_WIDTH),
        out_shape=jax.ShapeDtypeStruct((B, S, GLA_WIDTH), BF16),
        scratch_shapes=[pltpu.VMEM((GLA_KEY_WIDTH, GLA_VAL_DIM), F32)],
        compiler_params=pltpu.CompilerParams(
            dimension_semantics=("arbitrary", "arbitrary"), vmem_limit_bytes=VMEM_LIMIT),
        name="gla_attn",
    )(gq, gk, gv, gr, small, w_a2_pad, b_a, g_gla)


def _layer_norm(z, g, b):
    mu = jnp.mean(z, axis=-1, keepdims=True)
    d = z - mu
    var = jnp.mean(d * d, axis=-1, keepdims=True)
    return d * lax.rsqrt(var + LN_EPS) * g + b


def _tail_kernel(x_ref, fox_ref, gla_ref, mod_ref, wof_ref, wog_ref, ln1g_ref, ln1b_ref,
                 wg_ref, wu_ref, wd_ref, ln2g_ref, ln2b_ref, o_ref):
    m = mod_ref[0]
    y = _dot(fox_ref[...], wof_ref[...]) + _dot(gla_ref[...], wog_ref[...])
    x1 = _layer_norm(DEEPNORM_ALPHA * x_ref[...] + (1.0 + m[2:3]) * y,
                     ln1g_ref[...], ln1b_ref[...])
    u2 = (x1 * (1.0 + m[4:5]) + m[3:4]).astype(BF16)
    g = _dot(u2, wg_ref[...])
    up = _dot(u2, wu_ref[...])
    h = (g * (1.0 / (1.0 + jnp.exp(-g))) * up).astype(BF16)
    y2 = _dot(h, wd_ref[...])
    o_ref[...] = _layer_norm(DEEPNORM_ALPHA * x1 + (1.0 + m[5:6]) * y2,
                             ln2g_ref[...], ln2b_ref[...])


def _tail(x2d, fox2d, gla2d, mod, wof, wog, ln1g, ln1b, wg, wu, wd, ln2g, ln2b, steps_per_batch):
    T, D = x2d.shape
    tm = TAIL_TM
    const = lambda r, c: pl.BlockSpec((r, c), lambda i: (0, 0), pipeline_mode=pl.Buffered(1))
    tok = lambda w: pl.BlockSpec((tm, w), lambda i: (i, 0))
    return pl.pallas_call(
        _tail_kernel,
        grid=(T // tm,),
        in_specs=[
            tok(D), tok(FOX_WIDTH), tok(GLA_WIDTH),
            pl.BlockSpec((1, N_MOD, D), lambda i: (i // steps_per_batch, 0, 0)),
            const(FOX_WIDTH, D), const(GLA_WIDTH, D), const(1, D), const(1, D),
            const(D, D_FF), const(D, D_FF), const(D_FF, D), const(1, D), const(1, D),
        ],
        out_specs=tok(D),
        out_shape=jax.ShapeDtypeStruct((T, D), F32),
        compiler_params=pltpu.CompilerParams(
            dimension_semantics=("arbitrary",), vmem_limit_bytes=VMEM_LIMIT),
        name="tail",
    )(x2d, fox2d, gla2d, mod, wof, wog, ln1g, ln1b, wg, wu, wd, ln2g, ln2b)


def kernel(x, c, w_c, b_c, w_in, b_f, w_a2, b_a, g_gla, w_o, ln1_g, ln1_b,
           w_gate, w_up, w_down, ln2_g, ln2_b):
    B, S, D = x.shape
    assert (B, S, D) == (2, 8192, D_MODEL) or (S % FOX_TQ == 0 and D == D_MODEL)

    offs = [0]
    for wdt in IN_SPLITS:
        offs.append(offs[-1] + wdt)
    seg = lambda i: w_in[:, offs[i]:offs[i + 1]]
    w_small = jnp.concatenate(
        [seg(3), seg(7), jnp.zeros((D, SMALL_W - FOX_HEADS - GLA_GATE_RANK), F32)], axis=1)
    w_all = jnp.concatenate(
        [seg(0), seg(1), seg(2), seg(4), seg(5), seg(6), seg(8), w_small], axis=1).astype(BF16)
    b_f_pad = jnp.zeros((1, SMALL_W), F32).at[0, :FOX_HEADS].set(b_f)
    w_a2_pad = jnp.zeros((SMALL_W, GLA_KEY_WIDTH), F32) \
        .at[FOX_HEADS:FOX_HEADS + GLA_GATE_RANK].set(w_a2).astype(BF16)

    mod = _modulation(c, w_c, b_c)

    fq, fk, fv, gq, gk, gv, gr, small, dcol, drow = _in_projection(x, mod, w_all, b_f_pad)
    fox = _fox_attention(fq, fk, fv, dcol, drow)
    gla = _gla_attention(gq, gk, gv, gr, small, w_a2_pad,
                         b_a.reshape(1, -1), g_gla.reshape(1, -1))

    T = B * S
    w_o_b = w_o.astype(BF16)
    out = _tail(x.reshape(T, D), fox.reshape(T, FOX_WIDTH), gla.reshape(T, GLA_WIDTH), mod,
                w_o_b[:FOX_WIDTH], w_o_b[FOX_WIDTH:],
                ln1_g.reshape(1, D), ln1_b.reshape(1, D),
                w_gate.astype(BF16), w_up.astype(BF16), w_down.astype(BF16),
                ln2_g.reshape(1, D), ln2_b.reshape(1, D), S // TAIL_TM)
    return out.reshape(B, S, D)
```

```python
import math

import numpy as np
import jax
import jax.numpy as jnp
from jax import lax
from jax.experimental import pallas as pl
from jax.experimental.pallas import tpu as pltpu

F32 = jnp.float32
BF16 = jnp.bfloat16

D_MODEL = 1024
FOX_WIDTH = 512
FOX_HEAD_DIM = 64
FOX_HEADS = 8
FOX_PAIRS = FOX_HEADS // 2
GLA_WIDTH = 512
GLA_HEADS = 4
GLA_VAL_DIM = 128
GLA_KEY_DIM = 64
GLA_KEY_WIDTH = 256
GLA_GATE_RANK = 16
GLA_GATE_TEMP = 16.0
GLA_CHUNK = 64
D_FF = 2816
LN_EPS = 1e-5
RMS_EPS = 1e-6
N_MOD = 6
DEEPNORM_ALPHA = 2.0 ** 0.25
IN_SPLITS = (FOX_WIDTH, FOX_WIDTH, FOX_WIDTH, FOX_HEADS, GLA_KEY_WIDTH, GLA_KEY_WIDTH,
             GLA_WIDTH, GLA_GATE_RANK, GLA_WIDTH)
LOG2E = math.log2(math.e)

LANES = 128
SUBLANES = 8
MXU_DIM = 256
SMALL_W = LANES
VMEM_LIMIT = 56 * 1024 * 1024

MOD_TN = 1536
INPROJ_TM = 512
FOX_TQ = 256
FOX_TK = 512
GLA_TS = 512
GLA_PAIR = 2 * GLA_CHUNK
TAIL_TM = 512

NEG_BIG = -1e30

AUG_ROWS = 16


def _log_sigmoid(z):
    return jnp.minimum(z, 0.0) - jnp.log1p(jnp.exp(-jnp.abs(z)))


def _split3(a):
    hi = a.astype(BF16)
    r1 = a - hi.astype(F32)
    mid = r1.astype(BF16)
    lo = (r1 - mid.astype(F32)).astype(BF16)
    return hi, mid, lo


def _dot(a, b):
    return jnp.dot(a, b, preferred_element_type=F32)


def _tri_left(tri, a):
    hi, mid, lo = _split3(a)
    return _dot(tri, hi) + _dot(tri, mid) + _dot(tri, lo)


def _tri_right(a, tri):
    hi, mid, lo = _split3(a)
    return _dot(hi, tri) + _dot(mid, tri) + _dot(lo, tri)


def _mod_kernel(c_ref, w_ref, b_ref, o_ref):
    o_ref[...] = jnp.dot(c_ref[...], w_ref[...], preferred_element_type=F32,
                         precision=lax.Precision.HIGHEST) + b_ref[...]


def _modulation(c, w_c, b_c):
    B, D = c.shape
    N = w_c.shape[1]
    rows = SUBLANES
    c_pad = jnp.zeros((rows, D), F32).at[:B].set(c)
    out = pl.pallas_call(
        _mod_kernel,
        grid=(N // MOD_TN,),
        in_specs=[
            pl.BlockSpec((rows, D), lambda j: (0, 0)),
            pl.BlockSpec((D, MOD_TN), lambda j: (0, j)),
            pl.BlockSpec((1, MOD_TN), lambda j: (0, j)),
        ],
        out_specs=pl.BlockSpec((rows, MOD_TN), lambda j: (0, j)),
        out_shape=jax.ShapeDtypeStruct((rows, N), F32),
        compiler_params=pltpu.CompilerParams(
            dimension_semantics=("arbitrary",), vmem_limit_bytes=VMEM_LIMIT),
        name="adaln_mod",
    )(c_pad, w_c, b_c.reshape(1, N))
    return out[:B].reshape(B, N_MOD, D)


_C_FQ, _C_FK, _C_FV = 0, 512, 1024
_C_GQ, _C_GK, _C_GV, _C_GR, _C_SM = 1536, 1792, 2048, 2560, 3072
_W_COLS = _C_SM + SMALL_W
_KAUG_W = FOX_PAIRS * MXU_DIM


def _aug_select_constants():
    sel = np.zeros((3, SMALL_W, FOX_PAIRS * LANES), np.float32)
    ones = np.zeros((1, FOX_PAIRS * LANES), np.float32)
    for p in range(FOX_PAIRS):
        for h in range(2):
            for piece in range(3):
                sel[piece, 2 * p + h, p * LANES + 3 * h + piece] = -1.0
        ones[0, p * LANES + 6:p * LANES + 9] = 1.0
    return sel, ones


def _inproj_kernel(x_ref, mod_ref, w_ref, bf_ref, sel_ref, ones_ref,
                   fqT_ref, kaug_ref, fvT_ref, gq_ref, gk_ref, gv_ref, gr_ref, sm_ref,
                   drel_ref, cref_ref, carry_ref):
    tm = x_ref.shape[1]

    @pl.when(pl.program_id(1) == 0)
    def _():
        carry_ref[...] = jnp.zeros_like(carry_ref)

    m = mod_ref[0]
    u = (x_ref[0] * (1.0 + m[1:2]) + m[0:1]).astype(BF16)

    def proj(lo, width):
        return _dot(u, w_ref[:, lo:lo + width])

    fqT_ref[0] = (proj(_C_FQ, FOX_WIDTH) * (FOX_HEAD_DIM ** -0.5 * LOG2E)).T.astype(BF16)
    fk = proj(_C_FK, FOX_WIDTH).astype(BF16)
    fvT_ref[0] = proj(_C_FV, FOX_WIDTH).T.astype(BF16)
    gq_ref[0] = proj(_C_GQ, GLA_KEY_WIDTH) * (GLA_KEY_DIM ** -0.5)
    gk_ref[0] = proj(_C_GK, GLA_KEY_WIDTH)
    gv_ref[0] = proj(_C_GV, GLA_WIDTH).astype(BF16)
    gr_ref[0] = proj(_C_GR, GLA_WIDTH)
    small = proj(_C_SM, SMALL_W)
    sm_ref[0] = small

    lane = lax.broadcasted_iota(jnp.int32, (tm, SMALL_W), 1)
    log_f = jnp.where(lane < FOX_HEADS, _log_sigmoid(small + bf_ref[...]) * LOG2E, 0.0)
    r = lax.broadcasted_iota(jnp.int32, (tm, tm), 0)
    c = lax.broadcasted_iota(jnp.int32, (tm, tm), 1)
    tril = jnp.where(c <= r, 1.0, 0.0).astype(BF16)
    drel = _tri_left(tril, log_f)
    cref_ref[0, 0] = jnp.broadcast_to(carry_ref[...], (SUBLANES, SMALL_W))
    carry_ref[...] = carry_ref[...] + drel[tm - 1:tm, :]
    drel_ref[0] = drel.T[:FOX_HEADS, :]

    hi, mid, lo = _split3(drel)
    slab = (_dot(hi, sel_ref[0]) + _dot(mid, sel_ref[1]) + _dot(lo, sel_ref[2])
            + ones_ref[...]).astype(BF16)
    for p in range(FOX_PAIRS):
        kaug_ref[0, :, p * MXU_DIM:p * MXU_DIM + LANES] = fk[:, p * LANES:(p + 1) * LANES]
        kaug_ref[0, :, p * MXU_DIM + LANES:(p + 1) * MXU_DIM] = slab[:, p * LANES:(p + 1) * LANES]


def _in_projection(x, mod, w_all, b_f_pad):
    B, S, D = x.shape
    tm = INPROJ_TM
    sel, ones = _aug_select_constants()
    tok = lambda w: pl.BlockSpec((1, tm, w), lambda b, s: (b, s, 0))
    tokT = lambda h: pl.BlockSpec((1, h, tm), lambda b, s: (b, 0, s))
    outs = pl.pallas_call(
        _inproj_kernel,
        grid=(B, S // tm),
        in_specs=[
            tok(D),
            pl.BlockSpec((1, N_MOD, D), lambda b, s: (b, 0, 0)),
            pl.BlockSpec((D, _W_COLS), lambda b, s: (0, 0)),
            pl.BlockSpec((1, SMALL_W), lambda b, s: (0, 0)),
            pl.BlockSpec(sel.shape, lambda b, s: (0, 0, 0)),
            pl.BlockSpec(ones.shape, lambda b, s: (0, 0)),
        ],
        out_specs=[
            tokT(FOX_WIDTH), tok(_KAUG_W), tokT(FOX_WIDTH),
            tok(GLA_KEY_WIDTH), tok(GLA_KEY_WIDTH), tok(GLA_WIDTH), tok(GLA_WIDTH),
            tok(SMALL_W),
            tokT(FOX_HEADS),
            pl.BlockSpec((1, 1, SUBLANES, SMALL_W), lambda b, s: (b, s, 0, 0)),
        ],
        out_shape=[
            jax.ShapeDtypeStruct((B, FOX_WIDTH, S), BF16),
            jax.ShapeDtypeStruct((B, S, _KAUG_W), BF16),
            jax.ShapeDtypeStruct((B, FOX_WIDTH, S), BF16),
            jax.ShapeDtypeStruct((B, S, GLA_KEY_WIDTH), F32),
            jax.ShapeDtypeStruct((B, S, GLA_KEY_WIDTH), F32),
            jax.ShapeDtypeStruct((B, S, GLA_WIDTH), BF16),
            jax.ShapeDtypeStruct((B, S, GLA_WIDTH), F32),
            jax.ShapeDtypeStruct((B, S, SMALL_W), F32),
            jax.ShapeDtypeStruct((B, FOX_HEADS, S), F32),
            jax.ShapeDtypeStruct((B, S // tm, SUBLANES, SMALL_W), F32),
        ],
        scratch_shapes=[pltpu.VMEM((1, SMALL_W), F32)],
        compiler_params=pltpu.CompilerParams(
            dimension_semantics=("arbitrary", "arbitrary"), vmem_limit_bytes=VMEM_LIMIT),
        name="in_proj",
    )(x, mod, w_all, b_f_pad, jnp.asarray(sel, BF16), jnp.asarray(ones, F32))
    return outs


def _fox_kernel(qT_ref, kaug_ref, vT_ref, drel_ref, crep_ref, o_ref,
                qaug_ref, acc_ref, m_ref, l_ref):
    tq = FOX_TQ
    half = FOX_HEAD_DIM
    qi = pl.program_id(2)
    n_blk = crep_ref.shape[2] // 2
    q_start = qi * tq
    i_blk = q_start // INPROJ_TM

    qT = qT_ref[0]
    row = lax.broadcasted_iota(jnp.int32, (2 * half, tq), 0)
    r16 = lax.broadcasted_iota(jnp.int32, (AUG_ROWS, tq), 0)
    dr = drel_ref[0, 0]
    zq = jnp.zeros_like(qT)
    for h in range(2):
        own = (row < half) if h == 0 else (row >= half)
        qaug_ref[h, 0:2 * half, :] = jnp.where(own, qT, zq)
        hi, mid, lo = _split3(dr[h:h + 1])
        pick = (r16 >= 3 * h) & (r16 < 3 * h + 3)
        slab = jnp.where(r16 == 6, hi.astype(F32),
                         jnp.where(r16 == 7, mid.astype(F32),
                                   jnp.where(r16 == 8, lo.astype(F32),
                                             jnp.where(pick, 1.0, 0.0))))
        qaug_ref[h, 2 * half:2 * half + AUG_ROWS, :] = slab.astype(BF16)
        qaug_ref[h, 2 * half + AUG_ROWS:, :] = jnp.zeros(
            (MXU_DIM - 2 * half - AUG_ROWS, tq), BF16)
    c_i = [crep_ref[0, 0, pl.ds(h * n_blk + i_blk, 1), :] for h in range(2)]

    m_ref[...] = jnp.full_like(m_ref, NEG_BIG)
    l_ref[...] = jnp.zeros_like(l_ref)
    acc_ref[...] = jnp.zeros_like(acc_ref)

    def step(k_off, tk, masked):
        j_blk = k_off // INPROJ_TM
        kt = kaug_ref[0, pl.ds(k_off, tk), :]
        for h in range(2):
            sT = _dot(kt, qaug_ref[h])
            if masked:
                kr = lax.broadcasted_iota(jnp.int32, (tk, tq), 0)
                qc = lax.broadcasted_iota(jnp.int32, (tk, tq), 1)
                sT = jnp.where(kr <= qc, sT, NEG_BIG)
            big_c = c_i[h] - crep_ref[0, 0, pl.ds(h * n_blk + j_blk, 1), :]
            m_prev = m_ref[h]
            s8 = jnp.max(sT.reshape(tk // SUBLANES, SUBLANES, tq), axis=0)
            m_new = jnp.maximum(m_prev, jnp.max(s8, axis=0, keepdims=True) + big_c)
            pT = jnp.exp2(sT - (m_new - big_c))
            alpha = jnp.exp2(m_prev - m_new)
            l_ref[h] = alpha * l_ref[h] + jnp.sum(pT.reshape(tk // SUBLANES, SUBLANES, tq), axis=0)
            vT = vT_ref[0, h * half:(h + 1) * half, pl.ds(k_off, tk)]
            acc_ref[h] = alpha * acc_ref[h] + _dot(vT, pT.astype(BF16))
            m_ref[h] = m_new

    def body(j, carry):
        step(pl.multiple_of(j * FOX_TK, FOX_TK), FOX_TK, False)
        return carry

    lax.fori_loop(0, q_start // FOX_TK, body, 0)

    @pl.when(q_start % FOX_TK != 0)
    def _():
        step(pl.multiple_of(q_start - tq, tq), tq, False)

    step(pl.multiple_of(q_start, tq), tq, True)

    outs = []
    for h in range(2):
        l = jnp.sum(l_ref[h], axis=0, keepdims=True)
        outs.append(acc_ref[h] / l)
    oT = jnp.concatenate(outs, axis=0)
    o_ref[0] = oT.T.astype(o_ref.dtype)


def _fox_attention(fqT, kaug, fvT, drel, cref):
    B, _, S = fqT.shape
    tq = FOX_TQ
    n_blk = S // INPROJ_TM
    drel4 = drel.reshape(B, FOX_PAIRS, 2, S)
    c = cref[:, :, 0, :FOX_HEADS]
    c = jnp.transpose(c, (0, 2, 1)).reshape(B, FOX_PAIRS, 2 * n_blk, 1)
    crep = jnp.broadcast_to(c, (B, FOX_PAIRS, 2 * n_blk, tq))
    return pl.pallas_call(
        _fox_kernel,
        grid=(B, FOX_PAIRS, S // tq),
        in_specs=[
            pl.BlockSpec((1, LANES, tq), lambda b, p, i: (b, p, i)),
            pl.BlockSpec((1, S, MXU_DIM), lambda b, p, i: (b, 0, p)),
            pl.BlockSpec((1, LANES, S), lambda b, p, i: (b, p, 0)),
            pl.BlockSpec((1, 1, 2, tq), lambda b, p, i: (b, p, 0, i)),
            pl.BlockSpec((1, 1, 2 * n_blk, tq), lambda b, p, i: (b, p, 0, 0)),
        ],
        out_specs=pl.BlockSpec((1, tq, LANES), lambda b, p, i: (b, i, p)),
        out_shape=jax.ShapeDtypeStruct((B, S, FOX_WIDTH), BF16),
        scratch_shapes=[
            pltpu.VMEM((2, MXU_DIM, tq), BF16),
            pltpu.VMEM((2, FOX_HEAD_DIM, tq), F32),
            pltpu.VMEM((2, 1, tq), F32),
            pltpu.VMEM((2, SUBLANES, tq), F32),
        ],
        compiler_params=pltpu.CompilerParams(
            dimension_semantics=("arbitrary", "arbitrary", "arbitrary"),
            vmem_limit_bytes=VMEM_LIMIT),
        name="fox_attn",
    )(fqT, kaug, fvT, drel4, crep)


def _gla_kernel(q_ref, k_ref, v_ref, r_ref, sm_ref, wa_ref, ba_ref, g_ref, o_ref, state_ref):
    ts = q_ref.shape[1]
    C, P = GLA_CHUNK, GLA_PAIR
    dk, dv, H = GLA_KEY_DIM, GLA_VAL_DIM, GLA_HEADS
    KW = GLA_KEY_WIDTH

    @pl.when(pl.program_id(1) == 0)
    def _():
        state_ref[...] = jnp.zeros_like(state_ref)

    la_all = _log_sigmoid(_dot(sm_ref[0].astype(BF16), wa_ref[...]) + ba_ref[...]) \
        * (1.0 / GLA_GATE_TEMP)

    ri = lax.broadcasted_iota(jnp.int32, (P, P), 0)
    ci = lax.broadcasted_iota(jnp.int32, (P, P), 1)
    same_chunk = (ri < C) == (ci < C)
    causal = same_chunk & (ci <= ri)
    tril2 = jnp.where(causal, 1.0, 0.0).astype(BF16)
    triu2 = jnp.where(same_chunk & (ri <= ci), 1.0, 0.0).astype(BF16)

    lane_q = lax.broadcasted_iota(jnp.int32, (P, KW), 1)
    lane_t = lax.broadcasted_iota(jnp.int32, (KW, P), 1)
    first = lane_t < C

    for pi in range(ts // P):
        rows = slice(pi * P, (pi + 1) * P)
        la = la_all[rows]
        b = _tri_left(tril2, la)
        bT = _tri_right(la.T, triu2)
        bl0 = bT[:, C - 1:C]
        bl1 = bT[:, P - 1:P]
        q_dec = q_ref[0, rows, :] * jnp.exp(b)
        kT = k_ref[0, rows, :].T
        k_invT = (kT * jnp.exp(-bT)).astype(BF16)
        k_teT = kT * jnp.exp(jnp.where(first, bl0, bl1) - bT)
        zt = jnp.zeros_like(k_teT)
        k_te0 = jnp.where(first, k_teT, zt).astype(BF16)
        k_te1 = jnp.where(first, zt, k_teT).astype(BF16)
        dec0 = jnp.exp(bl0)
        dec1 = jnp.exp(bl1)

        vs = [v_ref[0, rows, h * dv:(h + 1) * dv] for h in range(H)]
        kv0 = jnp.concatenate([_dot(k_te0[h * dk:(h + 1) * dk, :], vs[h]) for h in range(H)], axis=0)
        kv1 = jnp.concatenate([_dot(k_te1[h * dk:(h + 1) * dk, :], vs[h]) for h in range(H)], axis=0)
        s0 = state_ref[...]
        s1 = dec0 * s0 + kv0
        state_ref[...] = dec1 * s1 + kv1
        s0b = s0.astype(BF16)
        s1b = s1.astype(BF16)

        zq = jnp.zeros_like(q_dec)
        for h in range(H):
            hv = slice(h * dv, (h + 1) * dv)
            qm = jnp.where((lane_q >= h * dk) & (lane_q < (h + 1) * dk), q_dec, zq).astype(BF16)
            a = _dot(qm, k_invT)
            a = jnp.where(causal, a, 0.0).astype(BF16)
            o_inter = jnp.concatenate([_dot(qm[:C], s0b), _dot(qm[C:], s1b)], axis=0)
            o = _dot(a, vs[h]) + o_inter
            o = o * lax.rsqrt(jnp.mean(o * o, axis=-1, keepdims=True) + RMS_EPS)
            rr = r_ref[0, rows, hv]
            o = o * g_ref[:, hv] * (rr * (1.0 / (1.0 + jnp.exp(-rr))))
            o_ref[0, rows, hv] = o.astype(o_ref.dtype)


def _gla_attention(gq, gk, gv, gr, small, w_a2_pad, b_a, g_gla):
    B, S, _ = gq.shape
    ts = GLA_TS
    tok = lambda w: pl.BlockSpec((1, ts, w), lambda b, s: (b, s, 0))
    full = lambda r, c: pl.BlockSpec((r, c), lambda b, s: (0, 0))
    return pl.pallas_call(
        _gla_kernel,
        grid=(B, S // ts),
        in_specs=[
            tok(GLA_KEY_WIDTH), tok(GLA_KEY_WIDTH), tok(GLA_WIDTH), tok(GLA_WIDTH), tok(SMALL_W),
            full(SMALL_W, GLA_KEY_WIDTH), full(1, GLA_KEY_WIDTH), full(1, GLA_WIDTH),
        ],
        out_specs=tok(GLA_WIDTH),
        out_shape=jax.ShapeDtypeStruct((B, S, GLA_WIDTH), BF16),
        scratch_shapes=[pltpu.VMEM((GLA_KEY_WIDTH, GLA_VAL_DIM), F32)],
        compiler_params=pltpu.CompilerParams(
            dimension_semantics=("arbitrary", "arbitrary"), vmem_limit_bytes=VMEM_LIMIT),
        name="gla_attn",
    )(gq, gk, gv, gr, small, w_a2_pad, b_a, g_gla)


def _layer_norm(z, g, b):
    mu = jnp.mean(z, axis=-1, keepdims=True)
    d = z - mu
    var = jnp.mean(d * d, axis=-1, keepdims=True)
    return d * lax.rsqrt(var + LN_EPS) * g + b


def _tail_kernel(x_ref, fox_ref, gla_ref, mod_ref, wof_ref, wog_ref, ln1g_ref, ln1b_ref,
                 wg_ref, wu_ref, wd_ref, ln2g_ref, ln2b_ref, o_ref):
    m = mod_ref[0]
    y = _dot(fox_ref[...], wof_ref[...]) + _dot(gla_ref[...], wog_ref[...])
    x1 = _layer_norm(DEEPNORM_ALPHA * x_ref[...] + (1.0 + m[2:3]) * y,
                     ln1g_ref[...], ln1b_ref[...])
    u2 = (x1 * (1.0 + m[4:5]) + m[3:4]).astype(BF16)
    g = _dot(u2, wg_ref[...])
    up = _dot(u2, wu_ref[...])
    h = (g * (1.0 / (1.0 + jnp.exp(-g))) * up).astype(BF16)
    y2 = _dot(h, wd_ref[...])
    o_ref[...] = _layer_norm(DEEPNORM_ALPHA * x1 + (1.0 + m[5:6]) * y2,
                             ln2g_ref[...], ln2b_ref[...])


def _tail(x2d, fox2d, gla2d, mod, wof, wog, ln1g, ln1b, wg, wu, wd, ln2g, ln2b, steps_per_batch):
    T, D = x2d.shape
    tm = TAIL_TM
    const = lambda r, c: pl.BlockSpec((r, c), lambda i: (0, 0), pipeline_mode=pl.Buffered(1))
    tok = lambda w: pl.BlockSpec((tm, w), lambda i: (i, 0))
    return pl.pallas_call(
        _tail_kernel,
        grid=(T // tm,),
        in_specs=[
            tok(D), tok(FOX_WIDTH), tok(GLA_WIDTH),
            pl.BlockSpec((1, N_MOD, D), lambda i: (i // steps_per_batch, 0, 0)),
            const(FOX_WIDTH, D), const(GLA_WIDTH, D), const(1, D), const(1, D),
            const(D, D_FF), const(D, D_FF), const(D_FF, D), const(1, D), const(1, D),
        ],
        out_specs=tok(D),
        out_shape=jax.ShapeDtypeStruct((T, D), F32),
        compiler_params=pltpu.CompilerParams(
            dimension_semantics=("arbitrary",), vmem_limit_bytes=VMEM_LIMIT),
        name="tail",
    )(x2d, fox2d, gla2d, mod, wof, wog, ln1g, ln1b, wg, wu, wd, ln2g, ln2b)


def kernel(x, c, w_c, b_c, w_in, b_f, w_a2, b_a, g_gla, w_o, ln1_g, ln1_b,
           w_gate, w_up, w_down, ln2_g, ln2_b):
    B, S, D = x.shape
    assert D == D_MODEL and S % INPROJ_TM == 0 and INPROJ_TM == FOX_TK == 2 * FOX_TQ

    offs = [0]
    for wdt in IN_SPLITS:
        offs.append(offs[-1] + wdt)
    seg = lambda i: w_in[:, offs[i]:offs[i + 1]]
    w_small = jnp.concatenate(
        [seg(3), seg(7), jnp.zeros((D, SMALL_W - FOX_HEADS - GLA_GATE_RANK), F32)], axis=1)
    w_all = jnp.concatenate(
        [seg(0), seg(1), seg(2), seg(4), seg(5), seg(6), seg(8), w_small], axis=1).astype(BF16)
    b_f_pad = jnp.zeros((1, SMALL_W), F32).at[0, :FOX_HEADS].set(b_f)
    w_a2_pad = jnp.zeros((SMALL_W, GLA_KEY_WIDTH), F32) \
        .at[FOX_HEADS:FOX_HEADS + GLA_GATE_RANK].set(w_a2).astype(BF16)

    mod = _modulation(c, w_c, b_c)

    fqT, kaug, fvT, gq, gk, gv, gr, small, drel, cref = _in_projection(x, mod, w_all, b_f_pad)
    fox = _fox_attention(fqT, kaug, fvT, drel, cref)
    gla = _gla_attention(gq, gk, gv, gr, small, w_a2_pad,
                         b_a.reshape(1, -1), g_gla.reshape(1, -1))

    T = B * S
    w_o_b = w_o.astype(BF16)
    out = _tail(x.reshape(T, D), fox.reshape(T, FOX_WIDTH), gla.reshape(T, GLA_WIDTH), mod,
                w_o_b[:FOX_WIDTH], w_o_b[FOX_WIDTH:],
                ln1_g.reshape(1, D), ln1_b.reshape(1, D),
                w_gate.astype(BF16), w_up.astype(BF16), w_down.astype(BF16),
                ln2_g.reshape(1, D), ln2_b.reshape(1, D), S // TAIL_TM)
    return out.reshape(B, S, D)
```

```python
import math

import numpy as np
import jax
import jax.numpy as jnp
from jax import lax
from jax.experimental import pallas as pl
from jax.experimental.pallas import tpu as pltpu

F32 = jnp.float32
BF16 = jnp.bfloat16

D_MODEL = 1024
FOX_WIDTH = 512
FOX_HEAD_DIM = 64
FOX_HEADS = 8
FOX_PAIRS = FOX_HEADS // 2
GLA_WIDTH = 512
GLA_HEADS = 4
GLA_VAL_DIM = 128
GLA_KEY_DIM = 64
GLA_KEY_WIDTH = 256
GLA_GATE_RANK = 16
GLA_GATE_TEMP = 16.0
GLA_CHUNK = 64
D_FF = 2816
LN_EPS = 1e-5
RMS_EPS = 1e-6
N_MOD = 6
DEEPNORM_ALPHA = 2.0 ** 0.25
IN_SPLITS = (FOX_WIDTH, FOX_WIDTH, FOX_WIDTH, FOX_HEADS, GLA_KEY_WIDTH, GLA_KEY_WIDTH,
             GLA_WIDTH, GLA_GATE_RANK, GLA_WIDTH)
LOG2E = math.log2(math.e)

LANES = 128
SUBLANES = 8
MXU_DIM = 256
SMALL_W = LANES
VMEM_LIMIT = 56 * 1024 * 1024

MOD_TN = 1536
INPROJ_TM = 512
FOX_TQ = 256
FOX_TK = 512
GLA_TS = 512
GLA_PAIR = 2 * GLA_CHUNK
TAIL_TM = 512

NEG_BIG = -1e30

AUG_ROWS = 16


def _log_sigmoid(z):
    return jnp.minimum(z, 0.0) - jnp.log1p(jnp.exp(-jnp.abs(z)))


def _split3(a):
    hi = a.astype(BF16)
    r1 = a - hi.astype(F32)
    mid = r1.astype(BF16)
    lo = (r1 - mid.astype(F32)).astype(BF16)
    return hi, mid, lo


def _dot(a, b):
    return jnp.dot(a, b, preferred_element_type=F32)


def _tri_left(tri, a):
    hi, mid, lo = _split3(a)
    return _dot(tri, hi) + _dot(tri, mid) + _dot(tri, lo)


def _tri_right(a, tri):
    hi, mid, lo = _split3(a)
    return _dot(hi, tri) + _dot(mid, tri) + _dot(lo, tri)


def _mod_kernel(c_ref, w_ref, b_ref, o_ref):
    o_ref[...] = jnp.dot(c_ref[...], w_ref[...], preferred_element_type=F32,
                         precision=lax.Precision.HIGHEST) + b_ref[...]


def _modulation(c, w_c, b_c):
    B, D = c.shape
    N = w_c.shape[1]
    rows = SUBLANES
    c_pad = jnp.zeros((rows, D), F32).at[:B].set(c)
    out = pl.pallas_call(
        _mod_kernel,
        grid=(N // MOD_TN,),
        in_specs=[
            pl.BlockSpec((rows, D), lambda j: (0, 0)),
            pl.BlockSpec((D, MOD_TN), lambda j: (0, j)),
            pl.BlockSpec((1, MOD_TN), lambda j: (0, j)),
        ],
        out_specs=pl.BlockSpec((rows, MOD_TN), lambda j: (0, j)),
        out_shape=jax.ShapeDtypeStruct((rows, N), F32),
        compiler_params=pltpu.CompilerParams(
            dimension_semantics=("arbitrary",), vmem_limit_bytes=VMEM_LIMIT),
        name="adaln_mod",
    )(c_pad, w_c, b_c.reshape(1, N))
    return out[:B].reshape(B, N_MOD, D)


_C_FQ, _C_FK, _C_FV = 0, 512, 1024
_C_GQ, _C_GK, _C_GV, _C_GR, _C_SM = 1536, 1792, 2048, 2560, 3072
_W_COLS = _C_SM + SMALL_W
_KAUG_W = FOX_PAIRS * MXU_DIM


def _aug_select_constants():
    sel = np.zeros((3, SMALL_W, FOX_PAIRS * LANES), np.float32)
    ones = np.zeros((1, FOX_PAIRS * LANES), np.float32)
    for p in range(FOX_PAIRS):
        for h in range(2):
            for piece in range(3):
                sel[piece, 2 * p + h, p * LANES + 3 * h + piece] = -1.0
        ones[0, p * LANES + 6:p * LANES + 9] = 1.0
    return sel, ones


def _inproj_kernel(x_ref, mod_ref, w_ref, bf_ref, sel_ref, ones_ref,
                   fqT_ref, kaug_ref, fvT_ref, gq_ref, gk_ref, gv_ref, gr_ref, sm_ref,
                   drel_ref, cref_ref, carry_ref):
    tm = x_ref.shape[1]

    @pl.when(pl.program_id(1) == 0)
    def _():
        carry_ref[...] = jnp.zeros_like(carry_ref)

    m = mod_ref[0]
    u = (x_ref[0] * (1.0 + m[1:2]) + m[0:1]).astype(BF16)

    def proj(lo, width):
        return _dot(u, w_ref[:, lo:lo + width])

    fqT_ref[0] = (proj(_C_FQ, FOX_WIDTH) * (FOX_HEAD_DIM ** -0.5 * LOG2E)).T.astype(BF16)
    fk = proj(_C_FK, FOX_WIDTH).astype(BF16)
    fvT_ref[0] = proj(_C_FV, FOX_WIDTH).T.astype(BF16)
    gq_ref[0] = proj(_C_GQ, GLA_KEY_WIDTH) * (GLA_KEY_DIM ** -0.5)
    gk_ref[0] = proj(_C_GK, GLA_KEY_WIDTH)
    gv_ref[0] = proj(_C_GV, GLA_WIDTH).astype(BF16)
    gr_ref[0] = proj(_C_GR, GLA_WIDTH)
    small = proj(_C_SM, SMALL_W)
    sm_ref[0] = small

    lane = lax.broadcasted_iota(jnp.int32, (tm, SMALL_W), 1)
    log_f = jnp.where(lane < FOX_HEADS, _log_sigmoid(small + bf_ref[...]) * LOG2E, 0.0)
    r = lax.broadcasted_iota(jnp.int32, (tm, tm), 0)
    c = lax.broadcasted_iota(jnp.int32, (tm, tm), 1)
    tril = jnp.where(c <= r, 1.0, 0.0).astype(BF16)
    drel = _tri_left(tril, log_f)
    cref_ref[0, 0] = jnp.broadcast_to(carry_ref[...], (SUBLANES, SMALL_W))
    carry_ref[...] = carry_ref[...] + drel[tm - 1:tm, :]
    drel_ref[0] = drel.T[:FOX_HEADS, :]

    hi, mid, lo = _split3(drel)
    slab = (_dot(hi, sel_ref[0]) + _dot(mid, sel_ref[1]) + _dot(lo, sel_ref[2])
            + ones_ref[...]).astype(BF16)
    for p in range(FOX_PAIRS):
        kaug_ref[0, :, p * MXU_DIM:p * MXU_DIM + LANES] = fk[:, p * LANES:(p + 1) * LANES]
        kaug_ref[0, :, p * MXU_DIM + LANES:(p + 1) * MXU_DIM] = slab[:, p * LANES:(p + 1) * LANES]


def _in_projection(x, mod, w_all, b_f_pad):
    B, S, D = x.shape
    tm = INPROJ_TM
    sel, ones = _aug_select_constants()
    tok = lambda w: pl.BlockSpec((1, tm, w), lambda b, s: (b, s, 0))
    tokT = lambda h: pl.BlockSpec((1, h, tm), lambda b, s: (b, 0, s))
    outs = pl.pallas_call(
        _inproj_kernel,
        grid=(B, S // tm),
        in_specs=[
            tok(D),
            pl.BlockSpec((1, N_MOD, D), lambda b, s: (b, 0, 0)),
            pl.BlockSpec((D, _W_COLS), lambda b, s: (0, 0)),
            pl.BlockSpec((1, SMALL_W), lambda b, s: (0, 0)),
            pl.BlockSpec(sel.shape, lambda b, s: (0, 0, 0)),
            pl.BlockSpec(ones.shape, lambda b, s: (0, 0)),
        ],
        out_specs=[
            tokT(FOX_WIDTH), tok(_KAUG_W), tokT(FOX_WIDTH),
            tok(GLA_KEY_WIDTH), tok(GLA_KEY_WIDTH), tok(GLA_WIDTH), tok(GLA_WIDTH),
            tok(SMALL_W),
            tokT(FOX_HEADS),
            pl.BlockSpec((1, 1, SUBLANES, SMALL_W), lambda b, s: (b, s, 0, 0)),
        ],
        out_shape=[
            jax.ShapeDtypeStruct((B, FOX_WIDTH, S), BF16),
            jax.ShapeDtypeStruct((B, S, _KAUG_W), BF16),
            jax.ShapeDtypeStruct((B, FOX_WIDTH, S), BF16),
            jax.ShapeDtypeStruct((B, S, GLA_KEY_WIDTH), F32),
            jax.ShapeDtypeStruct((B, S, GLA_KEY_WIDTH), F32),
            jax.ShapeDtypeStruct((B, S, GLA_WIDTH), BF16),
            jax.ShapeDtypeStruct((B, S, GLA_WIDTH), F32),
            jax.ShapeDtypeStruct((B, S, SMALL_W), F32),
            jax.ShapeDtypeStruct((B, FOX_HEADS, S), F32),
            jax.ShapeDtypeStruct((B, S // tm, SUBLANES, SMALL_W), F32),
        ],
        scratch_shapes=[pltpu.VMEM((1, SMALL_W), F32)],
        compiler_params=pltpu.CompilerParams(
            dimension_semantics=("arbitrary", "arbitrary"), vmem_limit_bytes=VMEM_LIMIT),
        name="in_proj",
    )(x, mod, w_all, b_f_pad, jnp.asarray(sel, BF16), jnp.asarray(ones, F32))
    return outs


def _fox_kernel(qT_ref, kaug_ref, vT_ref, drel_ref, crep_ref, o_ref,
                qaug_ref, s_ref, mx_ref, acc_ref, m_ref, l_ref):
    tq, tk = FOX_TQ, FOX_TK
    half = FOX_HEAD_DIM
    S = kaug_ref.shape[1]
    nq = S // tq
    n_blk = crep_ref.shape[2] // 2
    n_items = sum(qi * tq // tk + 1 for qi in range(nq))

    def build_qaug(qi):
        q0 = pl.multiple_of(qi * tq, tq)
        qT = qT_ref[0, :, pl.ds(q0, tq)]
        dr = drel_ref[0, 0, :, pl.ds(q0, tq)]
        row = lax.broadcasted_iota(jnp.int32, (2 * half, tq), 0)
        r16 = lax.broadcasted_iota(jnp.int32, (AUG_ROWS, tq), 0)
        zq = jnp.zeros_like(qT)
        for h in range(2):
            buf = (qi % 2) * 2 + h
            own = (row < half) if h == 0 else (row >= half)
            qaug_ref[buf, 0:2 * half, :] = jnp.where(own, qT, zq)
            hi, mid, lo = _split3(dr[h:h + 1])
            pick = (r16 >= 3 * h) & (r16 < 3 * h + 3)
            slab = jnp.where(r16 == 6, hi.astype(F32),
                             jnp.where(r16 == 7, mid.astype(F32),
                                       jnp.where(r16 == 8, lo.astype(F32),
                                                 jnp.where(pick, 1.0, 0.0))))
            qaug_ref[buf, 2 * half:2 * half + AUG_ROWS, :] = slab.astype(BF16)
            qaug_ref[buf, 2 * half + AUG_ROWS:, :] = jnp.zeros(
                (MXU_DIM - 2 * half - AUG_ROWS, tq), BF16)

    def reset_state():
        m_ref[...] = jnp.full_like(m_ref, NEG_BIG)
        l_ref[...] = jnp.zeros_like(l_ref)
        acc_ref[...] = jnp.zeros_like(acc_ref)

    def col_max8(sT):
        return jnp.max(sT.reshape(tk // SUBLANES, SUBLANES, tq), axis=0)

    def qk_dots(qi, j):
        kt = kaug_ref[0, pl.ds(pl.multiple_of(j * tk, tk), tk), :]
        return [_dot(kt, qaug_ref[(qi % 2) * 2 + h]) for h in range(2)]

    def store_scores(scores):
        for h in range(2):
            s_ref[h] = scores[h]
            mx_ref[h] = col_max8(scores[h])

    def mask_stage(qi, j):
        kr = lax.broadcasted_iota(jnp.int32, (tk, tq), 0)
        qc = lax.broadcasted_iota(jnp.int32, (tk, tq), 1)
        keep = (kr - qc) <= (qi * tq - j * tk)
        for h in range(2):
            sT = jnp.where(keep, s_ref[h], NEG_BIG)
            s_ref[h] = sT
            mx_ref[h] = col_max8(sT)

    def pv_stage(qi, j):
        k_off = pl.multiple_of(j * tk, tk)
        i_blk = qi * tq // INPROJ_TM
        for h in range(2):
            big_c = (crep_ref[0, 0, pl.ds(h * n_blk + i_blk, 1), :]
                     - crep_ref[0, 0, pl.ds(h * n_blk + j, 1), :])
            m_prev = m_ref[h]
            m_new = jnp.maximum(m_prev, jnp.max(mx_ref[h], axis=0, keepdims=True) + big_c)
            pT = jnp.exp2(s_ref[h] - (m_new - big_c))
            alpha = jnp.exp2(m_prev - m_new)
            l_ref[h] = alpha * l_ref[h] + jnp.sum(pT.reshape(tk // SUBLANES, SUBLANES, tq), axis=0)
            vT = vT_ref[0, h * half:(h + 1) * half, pl.ds(k_off, tk)]
            acc_ref[h] = alpha * acc_ref[h] + _dot(vT, pT.astype(BF16))
            m_ref[h] = m_new

    def finalize(qi):
        outs = []
        for h in range(2):
            l = jnp.sum(l_ref[h], axis=0, keepdims=True)
            outs.append(acc_ref[h] / l)
        oT = jnp.concatenate(outs, axis=0)
        o_ref[0, pl.ds(pl.multiple_of(qi * tq, tq), tq), :] = oT.T.astype(o_ref.dtype)

    zero = jnp.int32(0)
    build_qaug(zero)
    build_qaug(zero + 1)
    reset_state()
    store_scores(qk_dots(zero, zero))
    mask_stage(zero, zero)

    def body(t, carry):
        qi, j = carry
        last = j == qi * tq // tk
        qi2 = jnp.where(last, qi + 1, qi)
        j2 = jnp.where(last, 0, j + 1)
        nxt = qk_dots(qi2, j2)
        pv_stage(qi, j)
        store_scores(nxt)

        @pl.when(j2 == qi2 * tq // tk)
        def _():
            mask_stage(qi2, j2)

        @pl.when(last)
        def _():
            finalize(qi)
            reset_state()

            @pl.when(qi + 2 < nq)
            def _():
                build_qaug(qi + 2)

        return qi2, j2

    qi, j = lax.fori_loop(0, n_items - 1, body, (jnp.int32(0), jnp.int32(0)))
    pv_stage(qi, j)
    finalize(qi)


def _fox_attention(fqT, kaug, fvT, drel, cref):
    B, _, S = fqT.shape
    tq, tk = FOX_TQ, FOX_TK
    n_blk = S // INPROJ_TM
    drel4 = drel.reshape(B, FOX_PAIRS, 2, S)
    c = cref[:, :, 0, :FOX_HEADS]
    c = jnp.transpose(c, (0, 2, 1)).reshape(B, FOX_PAIRS, 2 * n_blk, 1)
    crep = jnp.broadcast_to(c, (B, FOX_PAIRS, 2 * n_blk, tq))
    return pl.pallas_call(
        _fox_kernel,
        grid=(B, FOX_PAIRS),
        in_specs=[
            pl.BlockSpec((1, LANES, S), lambda b, p: (b, p, 0)),
            pl.BlockSpec((1, S, MXU_DIM), lambda b, p: (b, 0, p)),
            pl.BlockSpec((1, LANES, S), lambda b, p: (b, p, 0)),
            pl.BlockSpec((1, 1, 2, S), lambda b, p: (b, p, 0, 0)),
            pl.BlockSpec((1, 1, 2 * n_blk, tq), lambda b, p: (b, p, 0, 0)),
        ],
        out_specs=pl.BlockSpec((1, S, LANES), lambda b, p: (b, 0, p)),
        out_shape=jax.ShapeDtypeStruct((B, S, FOX_WIDTH), BF16),
        scratch_shapes=[
            pltpu.VMEM((4, MXU_DIM, tq), BF16),
            pltpu.VMEM((2, tk, tq), F32),
            pltpu.VMEM((2, SUBLANES, tq), F32),
            pltpu.VMEM((2, FOX_HEAD_DIM, tq), F32),
            pltpu.VMEM((2, 1, tq), F32),
            pltpu.VMEM((2, SUBLANES, tq), F32),
        ],
        compiler_params=pltpu.CompilerParams(
            dimension_semantics=("arbitrary", "arbitrary"),
            vmem_limit_bytes=VMEM_LIMIT),
        name="fox_attn",
    )(fqT, kaug, fvT, drel4, crep)


def _gla_kernel(q_ref, k_ref, v_ref, r_ref, sm_ref, wa_ref, ba_ref, g_ref, o_ref, state_ref):
    ts = q_ref.shape[1]
    C, P = GLA_CHUNK, GLA_PAIR
    dk, dv, H = GLA_KEY_DIM, GLA_VAL_DIM, GLA_HEADS
    KW = GLA_KEY_WIDTH

    @pl.when(pl.program_id(1) == 0)
    def _():
        state_ref[...] = jnp.zeros_like(state_ref)

    la_all = _log_sigmoid(_dot(sm_ref[0].astype(BF16), wa_ref[...]) + ba_ref[...]) \
        * (1.0 / GLA_GATE_TEMP)

    ri = lax.broadcasted_iota(jnp.int32, (P, P), 0)
    ci = lax.broadcasted_iota(jnp.int32, (P, P), 1)
    same_chunk = (ri < C) == (ci < C)
    causal = same_chunk & (ci <= ri)
    tril2 = jnp.where(causal, 1.0, 0.0).astype(BF16)
    triu2 = jnp.where(same_chunk & (ri <= ci), 1.0, 0.0).astype(BF16)

    lane_q = lax.broadcasted_iota(jnp.int32, (P, KW), 1)
    lane_t = lax.broadcasted_iota(jnp.int32, (KW, P), 1)
    first = lane_t < C

    for pi in range(ts // P):
        rows = slice(pi * P, (pi + 1) * P)
        la = la_all[rows]
        b = _tri_left(tril2, la)
        bT = _tri_right(la.T, triu2)
        bl0 = bT[:, C - 1:C]
        bl1 = bT[:, P - 1:P]
        q_dec = q_ref[0, rows, :] * jnp.exp(b)
        kT = k_ref[0, rows, :].T
        k_invT = (kT * jnp.exp(-bT)).astype(BF16)
        k_teT = kT * jnp.exp(jnp.where(first, bl0, bl1) - bT)
        zt = jnp.zeros_like(k_teT)
        k_te0 = jnp.where(first, k_teT, zt).astype(BF16)
        k_te1 = jnp.where(first, zt, k_teT).astype(BF16)
        dec0 = jnp.exp(bl0)
        dec1 = jnp.exp(bl1)

        vs = [v_ref[0, rows, h * dv:(h + 1) * dv] for h in range(H)]
        kv0 = jnp.concatenate([_dot(k_te0[h * dk:(h + 1) * dk, :], vs[h]) for h in range(H)], axis=0)
        kv1 = jnp.concatenate([_dot(k_te1[h * dk:(h + 1) * dk, :], vs[h]) for h in range(H)], axis=0)
        s0 = state_ref[...]
        s1 = dec0 * s0 + kv0
        state_ref[...] = dec1 * s1 + kv1
        s0b = s0.astype(BF16)
        s1b = s1.astype(BF16)

        zq = jnp.zeros_like(q_dec)
        for h in range(H):
            hv = slice(h * dv, (h + 1) * dv)
            qm = jnp.where((lane_q >= h * dk) & (lane_q < (h + 1) * dk), q_dec, zq).astype(BF16)
            a = _dot(qm, k_invT)
            a = jnp.where(causal, a, 0.0).astype(BF16)
            o_inter = jnp.concatenate([_dot(qm[:C], s0b), _dot(qm[C:], s1b)], axis=0)
            o = _dot(a, vs[h]) + o_inter
            o = o * lax.rsqrt(jnp.mean(o * o, axis=-1, keepdims=True) + RMS_EPS)
            rr = r_ref[0, rows, hv]
            o = o * g_ref[:, hv] * (rr * (1.0 / (1.0 + jnp.exp(-rr))))
            o_ref[0, rows, hv] = o.astype(o_ref.dtype)


def _gla_attention(gq, gk, gv, gr, small, w_a2_pad, b_a, g_gla):
    B, S, _ = gq.shape
    ts = GLA_TS
    tok = lambda w: pl.BlockSpec((1, ts, w), lambda b, s: (b, s, 0))
    full = lambda r, c: pl.BlockSpec((r, c), lambda b, s: (0, 0))
    return pl.pallas_call(
        _gla_kernel,
        grid=(B, S // ts),
        in_specs=[
            tok(GLA_KEY_WIDTH), tok(GLA_KEY_WIDTH), tok(GLA_WIDTH), tok(GLA_WIDTH), tok(SMALL_W),
            full(SMALL_W, GLA_KEY_WIDTH), full(1, GLA_KEY_WIDTH), full(1, GLA_WIDTH),
        ],
        out_specs=tok(GLA_WIDTH),
        out_shape=jax.ShapeDtypeStruct((B, S, GLA_WIDTH), BF16),
        scratch_shapes=[pltpu.VMEM((GLA_KEY_WIDTH, GLA_VAL_DIM), F32)],
        compiler_params=pltpu.CompilerParams(
            dimension_semantics=("arbitrary", "arbitrary"), vmem_limit_bytes=VMEM_LIMIT),
        name="gla_attn",
    )(gq, gk, gv, gr, small, w_a2_pad, b_a, g_gla)


def _layer_norm(z, g, b):
    mu = jnp.mean(z, axis=-1, keepdims=True)
    d = z - mu
    var = jnp.mean(d * d, axis=-1, keepdims=True)
    return d * lax.rsqrt(var + LN_EPS) * g + b


def _tail_kernel(x_ref, fox_ref, gla_ref, mod_ref, wof_ref, wog_ref, ln1g_ref, ln1b_ref,
                 wg_ref, wu_ref, wd_ref, ln2g_ref, ln2b_ref, o_ref):
    m = mod_ref[0]
    y = _dot(fox_ref[...], wof_ref[...]) + _dot(gla_ref[...], wog_ref[...])
    x1 = _layer_norm(DEEPNORM_ALPHA * x_ref[...] + (1.0 + m[2:3]) * y,
                     ln1g_ref[...], ln1b_ref[...])
    u2 = (x1 * (1.0 + m[4:5]) + m[3:4]).astype(BF16)
    g = _dot(u2, wg_ref[...])
    up = _dot(u2, wu_ref[...])
    h = (g * (1.0 / (1.0 + jnp.exp(-g))) * up).astype(BF16)
    y2 = _dot(h, wd_ref[...])
    o_ref[...] = _layer_norm(DEEPNORM_ALPHA * x1 + (1.0 + m[5:6]) * y2,
                             ln2g_ref[...], ln2b_ref[...])


def _tail(x2d, fox2d, gla2d, mod, wof, wog, ln1g, ln1b, wg, wu, wd, ln2g, ln2b, steps_per_batch):
    T, D = x2d.shape
    tm = TAIL_TM
    const = lambda r, c: pl.BlockSpec((r, c), lambda i: (0, 0), pipeline_mode=pl.Buffered(1))
    tok = lambda w: pl.BlockSpec((tm, w), lambda i: (i, 0))
    return pl.pallas_call(
        _tail_kernel,
        grid=(T // tm,),
        in_specs=[
            tok(D), tok(FOX_WIDTH), tok(GLA_WIDTH),
            pl.BlockSpec((1, N_MOD, D), lambda i: (i // steps_per_batch, 0, 0)),
            const(FOX_WIDTH, D), const(GLA_WIDTH, D), const(1, D), const(1, D),
            const(D, D_FF), const(D, D_FF), const(D_FF, D), const(1, D), const(1, D),
        ],
        out_specs=tok(D),
        out_shape=jax.ShapeDtypeStruct((T, D), F32),
        compiler_params=pltpu.CompilerParams(
            dimension_semantics=("arbitrary",), vmem_limit_bytes=VMEM_LIMIT),
        name="tail",
    )(x2d, fox2d, gla2d, mod, wof, wog, ln1g, ln1b, wg, wu, wd, ln2g, ln2b)


def kernel(x, c, w_c, b_c, w_in, b_f, w_a2, b_a, g_gla, w_o, ln1_g, ln1_b,
           w_gate, w_up, w_down, ln2_g, ln2_b):
    B, S, D = x.shape
    assert D == D_MODEL and S % INPROJ_TM == 0 and INPROJ_TM == FOX_TK == 2 * FOX_TQ

    offs = [0]
    for wdt in IN_SPLITS:
        offs.append(offs[-1] + wdt)
    seg = lambda i: w_in[:, offs[i]:offs[i + 1]]
    w_small = jnp.concatenate(
        [seg(3), seg(7), jnp.zeros((D, SMALL_W - FOX_HEADS - GLA_GATE_RANK), F32)], axis=1)
    w_all = jnp.concatenate(
        [seg(0), seg(1), seg(2), seg(4), seg(5), seg(6), seg(8), w_small], axis=1).astype(BF16)
    b_f_pad = jnp.zeros((1, SMALL_W), F32).at[0, :FOX_HEADS].set(b_f)
    w_a2_pad = jnp.zeros((SMALL_W, GLA_KEY_WIDTH), F32) \
        .at[FOX_HEADS:FOX_HEADS + GLA_GATE_RANK].set(w_a2).astype(BF16)

    mod = _modulation(c, w_c, b_c)

    fqT, kaug, fvT, gq, gk, gv, gr, small, drel, cref = _in_projection(x, mod, w_all, b_f_pad)
    fox = _fox_attention(fqT, kaug, fvT, drel, cref)
    gla = _gla_attention(gq, gk, gv, gr, small, w_a2_pad,
                         b_a.reshape(1, -1), g_gla.reshape(1, -1))

    T = B * S
    w_o_b = w_o.astype(BF16)
    out = _tail(x.reshape(T, D), fox.reshape(T, FOX_WIDTH), gla.reshape(T, GLA_WIDTH), mod,
                w_o_b[:FOX_WIDTH], w_o_b[FOX_WIDTH:],
                ln1_g.reshape(1, D), ln1_b.reshape(1, D),
                w_gate.astype(BF16), w_up.astype(BF16), w_down.astype(BF16),
                ln2_g.reshape(1, D), ln2_b.reshape(1, D), S // TAIL_TM)
    return out.reshape(B, S, D)
```

```python
import math

import numpy as np
import jax
import jax.numpy as jnp
from jax import lax
from jax.experimental import pallas as pl
from jax.experimental.pallas import tpu as pltpu

F32 = jnp.float32
BF16 = jnp.bfloat16

D_MODEL = 1024
FOX_WIDTH = 512
FOX_HEAD_DIM = 64
FOX_HEADS = 8
FOX_PAIRS = FOX_HEADS // 2
GLA_WIDTH = 512
GLA_HEADS = 4
GLA_VAL_DIM = 128
GLA_KEY_DIM = 64
GLA_KEY_WIDTH = 256
GLA_GATE_RANK = 16
GLA_GATE_TEMP = 16.0
GLA_CHUNK = 64
D_FF = 2816
LN_EPS = 1e-5
RMS_EPS = 1e-6
N_MOD = 6
DEEPNORM_ALPHA = 2.0 ** 0.25
IN_SPLITS = (FOX_WIDTH, FOX_WIDTH, FOX_WIDTH, FOX_HEADS, GLA_KEY_WIDTH, GLA_KEY_WIDTH,
             GLA_WIDTH, GLA_GATE_RANK, GLA_WIDTH)
LOG2E = math.log2(math.e)

LANES = 128
SUBLANES = 8
MXU_DIM = 256
SMALL_W = LANES
VMEM_LIMIT = 56 * 1024 * 1024

MOD_TN = 1536
INPROJ_TM = 512
FOX_TQ = 256
FOX_TK = 512
GLA_TS = 512
GLA_PAIR = 2 * GLA_CHUNK
TAIL_TM = 512

NEG_BIG = -1e30
PRUNE_LOG2 = -1100.0
NORM_SLACK = 1.05

AUG_ROWS = 16


def _log_sigmoid(z):
    return jnp.minimum(z, 0.0) - jnp.log1p(jnp.exp(-jnp.abs(z)))


def _split3(a):
    hi = a.astype(BF16)
    r1 = a - hi.astype(F32)
    mid = r1.astype(BF16)
    lo = (r1 - mid.astype(F32)).astype(BF16)
    return hi, mid, lo


def _dot(a, b):
    return jnp.dot(a, b, preferred_element_type=F32)


def _tri_left(tri, a):
    hi, mid, lo = _split3(a)
    return _dot(tri, hi) + _dot(tri, mid) + _dot(tri, lo)


def _tri_right(a, tri):
    hi, mid, lo = _split3(a)
    return _dot(hi, tri) + _dot(mid, tri) + _dot(lo, tri)


def _mod_kernel(c_ref, w_ref, b_ref, o_ref):
    o_ref[...] = jnp.dot(c_ref[...], w_ref[...], preferred_element_type=F32,
                         precision=lax.Precision.HIGHEST) + b_ref[...]


def _modulation(c, w_c, b_c):
    B, D = c.shape
    N = w_c.shape[1]
    rows = SUBLANES
    c_pad = jnp.zeros((rows, D), F32).at[:B].set(c)
    out = pl.pallas_call(
        _mod_kernel,
        grid=(N // MOD_TN,),
        in_specs=[
            pl.BlockSpec((rows, D), lambda j: (0, 0)),
            pl.BlockSpec((D, MOD_TN), lambda j: (0, j)),
            pl.BlockSpec((1, MOD_TN), lambda j: (0, j)),
        ],
        out_specs=pl.BlockSpec((rows, MOD_TN), lambda j: (0, j)),
        out_shape=jax.ShapeDtypeStruct((rows, N), F32),
        compiler_params=pltpu.CompilerParams(
            dimension_semantics=("arbitrary",), vmem_limit_bytes=VMEM_LIMIT),
        name="adaln_mod",
    )(c_pad, w_c, b_c.reshape(1, N))
    return out[:B].reshape(B, N_MOD, D)


_C_FQ, _C_FK, _C_FV = 0, 512, 1024
_C_GQ, _C_GK, _C_GV, _C_GR, _C_SM = 1536, 1792, 2048, 2560, 3072
_W_COLS = _C_SM + SMALL_W
_KAUG_W = FOX_PAIRS * MXU_DIM


def _aug_select_constants():
    sel = np.zeros((3, SMALL_W, FOX_PAIRS * LANES), np.float32)
    ones = np.zeros((1, FOX_PAIRS * LANES), np.float32)
    for p in range(FOX_PAIRS):
        for h in range(2):
            for piece in range(3):
                sel[piece, 2 * p + h, p * LANES + 3 * h + piece] = -1.0
        ones[0, p * LANES + 6:p * LANES + 9] = 1.0
    return sel, ones


def _head_indicator():
    ind = np.zeros((FOX_WIDTH, SMALL_W), np.float32)
    for h in range(FOX_HEADS):
        ind[h * FOX_HEAD_DIM:(h + 1) * FOX_HEAD_DIM, h] = 1.0
    return ind


def _inproj_kernel(x_ref, mod_ref, w_ref, bf_ref, sel_ref, ones_ref, ind_ref,
                   fqT_ref, kaug_ref, fvT_ref, gq_ref, gk_ref, gv_ref, gr_ref, sm_ref,
                   drel_ref, stats_ref, carry_ref):
    tm = x_ref.shape[1]

    @pl.when(pl.program_id(1) == 0)
    def _():
        carry_ref[...] = jnp.zeros_like(carry_ref)

    m = mod_ref[0]
    u = (x_ref[0] * (1.0 + m[1:2]) + m[0:1]).astype(BF16)

    def proj(lo, width):
        return _dot(u, w_ref[:, lo:lo + width])

    fq = proj(_C_FQ, FOX_WIDTH) * (FOX_HEAD_DIM ** -0.5 * LOG2E)
    fqT_ref[0] = fq.T.astype(BF16)
    fk32 = proj(_C_FK, FOX_WIDTH)
    fk = fk32.astype(BF16)
    qn2 = jnp.max(_dot((fq * fq).astype(BF16), ind_ref[...]), axis=0, keepdims=True)
    kn2 = jnp.max(_dot((fk32 * fk32).astype(BF16), ind_ref[...]), axis=0, keepdims=True)
    fvT_ref[0] = proj(_C_FV, FOX_WIDTH).T.astype(BF16)
    gq_ref[0] = proj(_C_GQ, GLA_KEY_WIDTH) * (GLA_KEY_DIM ** -0.5)
    gk_ref[0] = proj(_C_GK, GLA_KEY_WIDTH)
    gv_ref[0] = proj(_C_GV, GLA_WIDTH).astype(BF16)
    gr_ref[0] = proj(_C_GR, GLA_WIDTH)
    small = proj(_C_SM, SMALL_W)
    sm_ref[0] = small

    lane = lax.broadcasted_iota(jnp.int32, (tm, SMALL_W), 1)
    log_f = jnp.where(lane < FOX_HEADS, _log_sigmoid(small + bf_ref[...]) * LOG2E, 0.0)
    r = lax.broadcasted_iota(jnp.int32, (tm, tm), 0)
    c = lax.broadcasted_iota(jnp.int32, (tm, tm), 1)
    tril = jnp.where(c <= r, 1.0, 0.0).astype(BF16)
    drel = _tri_left(tril, log_f)
    srow = lax.broadcasted_iota(jnp.int32, (SUBLANES, SMALL_W), 0)
    stats_ref[0, 0] = jnp.where(srow == 0, carry_ref[...],
                                jnp.where(srow == 1, qn2, jnp.where(srow == 2, kn2, 0.0)))
    carry_ref[...] = carry_ref[...] + drel[tm - 1:tm, :]
    drel_ref[0] = drel.T[:FOX_HEADS, :]

    hi, mid, lo = _split3(drel)
    slab = (_dot(hi, sel_ref[0]) + _dot(mid, sel_ref[1]) + _dot(lo, sel_ref[2])
            + ones_ref[...]).astype(BF16)
    for p in range(FOX_PAIRS):
        kaug_ref[0, :, p * MXU_DIM:p * MXU_DIM + LANES] = fk[:, p * LANES:(p + 1) * LANES]
        kaug_ref[0, :, p * MXU_DIM + LANES:(p + 1) * MXU_DIM] = slab[:, p * LANES:(p + 1) * LANES]


def _in_projection(x, mod, w_all, b_f_pad):
    B, S, D = x.shape
    tm = INPROJ_TM
    sel, ones = _aug_select_constants()
    ind = _head_indicator()
    tok = lambda w: pl.BlockSpec((1, tm, w), lambda b, s: (b, s, 0))
    tokT = lambda h: pl.BlockSpec((1, h, tm), lambda b, s: (b, 0, s))
    outs = pl.pallas_call(
        _inproj_kernel,
        grid=(B, S // tm),
        in_specs=[
            tok(D),
            pl.BlockSpec((1, N_MOD, D), lambda b, s: (b, 0, 0)),
            pl.BlockSpec((D, _W_COLS), lambda b, s: (0, 0)),
            pl.BlockSpec((1, SMALL_W), lambda b, s: (0, 0)),
            pl.BlockSpec(sel.shape, lambda b, s: (0, 0, 0)),
            pl.BlockSpec(ones.shape, lambda b, s: (0, 0)),
            pl.BlockSpec(ind.shape, lambda b, s: (0, 0)),
        ],
        out_specs=[
            tokT(FOX_WIDTH), tok(_KAUG_W), tokT(FOX_WIDTH),
            tok(GLA_KEY_WIDTH), tok(GLA_KEY_WIDTH), tok(GLA_WIDTH), tok(GLA_WIDTH),
            tok(SMALL_W),
            tokT(FOX_HEADS),
            pl.BlockSpec((1, 1, SUBLANES, SMALL_W), lambda b, s: (b, s, 0, 0)),
        ],
        out_shape=[
            jax.ShapeDtypeStruct((B, FOX_WIDTH, S), BF16),
            jax.ShapeDtypeStruct((B, S, _KAUG_W), BF16),
            jax.ShapeDtypeStruct((B, FOX_WIDTH, S), BF16),
            jax.ShapeDtypeStruct((B, S, GLA_KEY_WIDTH), F32),
            jax.ShapeDtypeStruct((B, S, GLA_KEY_WIDTH), F32),
            jax.ShapeDtypeStruct((B, S, GLA_WIDTH), BF16),
            jax.ShapeDtypeStruct((B, S, GLA_WIDTH), F32),
            jax.ShapeDtypeStruct((B, S, SMALL_W), F32),
            jax.ShapeDtypeStruct((B, FOX_HEADS, S), F32),
            jax.ShapeDtypeStruct((B, S // tm, SUBLANES, SMALL_W), F32),
        ],
        scratch_shapes=[pltpu.VMEM((1, SMALL_W), F32)],
        compiler_params=pltpu.CompilerParams(
            dimension_semantics=("arbitrary", "arbitrary"), vmem_limit_bytes=VMEM_LIMIT),
        name="in_proj",
    )(x, mod, w_all, b_f_pad, jnp.asarray(sel, BF16), jnp.asarray(ones, F32), jnp.asarray(ind, BF16))
    return outs


def _fox_kernel(jstart_ref, nitems_ref, qT_ref, kaug_ref, vT_ref, drel_ref, crep_ref, o_ref,
                qaug_ref, s_ref, mx_ref, acc_ref, m_ref, l_ref):
    tq, tk = FOX_TQ, FOX_TK
    half = FOX_HEAD_DIM
    S = kaug_ref.shape[1]
    nq = S // tq
    n_blk = crep_ref.shape[2] // 2
    g = pl.program_id(0) * pl.num_programs(1) + pl.program_id(1)

    def build_qaug(qi):
        q0 = pl.multiple_of(qi * tq, tq)
        qT = qT_ref[0, :, pl.ds(q0, tq)]
        dr = drel_ref[0, 0, :, pl.ds(q0, tq)]
        row = lax.broadcasted_iota(jnp.int32, (2 * half, tq), 0)
        r16 = lax.broadcasted_iota(jnp.int32, (AUG_ROWS, tq), 0)
        zq = jnp.zeros_like(qT)
        for h in range(2):
            buf = (qi % 2) * 2 + h
            own = (row < half) if h == 0 else (row >= half)
            qaug_ref[buf, 0:2 * half, :] = jnp.where(own, qT, zq)
            hi, mid, lo = _split3(dr[h:h + 1])
            pick = (r16 >= 3 * h) & (r16 < 3 * h + 3)
            slab = jnp.where(r16 == 6, hi.astype(F32),
                             jnp.where(r16 == 7, mid.astype(F32),
                                       jnp.where(r16 == 8, lo.astype(F32),
                                                 jnp.where(pick, 1.0, 0.0))))
            qaug_ref[buf, 2 * half:2 * half + AUG_ROWS, :] = slab.astype(BF16)
            qaug_ref[buf, 2 * half + AUG_ROWS:, :] = jnp.zeros(
                (MXU_DIM - 2 * half - AUG_ROWS, tq), BF16)

    def reset_state():
        m_ref[...] = jnp.full_like(m_ref, NEG_BIG)
        l_ref[...] = jnp.zeros_like(l_ref)
        acc_ref[...] = jnp.zeros_like(acc_ref)

    def col_max8(sT):
        return jnp.max(sT.reshape(tk // SUBLANES, SUBLANES, tq), axis=0)

    def qk_dots(qi, j):
        kt = kaug_ref[0, pl.ds(pl.multiple_of(j * tk, tk), tk), :]
        return [_dot(kt, qaug_ref[(qi % 2) * 2 + h]) for h in range(2)]

    def store_scores(scores):
        for h in range(2):
            s_ref[h] = scores[h]
            mx_ref[h] = col_max8(scores[h])

    def mask_stage(qi, j):
        kr = lax.broadcasted_iota(jnp.int32, (tk, tq), 0)
        qc = lax.broadcasted_iota(jnp.int32, (tk, tq), 1)
        keep = (kr - qc) <= (qi * tq - j * tk)
        for h in range(2):
            sT = jnp.where(keep, s_ref[h], NEG_BIG)
            s_ref[h] = sT
            mx_ref[h] = col_max8(sT)

    def pv_stage(qi, j):
        k_off = pl.multiple_of(j * tk, tk)
        i_blk = qi * tq // INPROJ_TM
        for h in range(2):
            big_c = (crep_ref[0, 0, pl.ds(h * n_blk + i_blk, 1), :]
                     - crep_ref[0, 0, pl.ds(h * n_blk + j, 1), :])
            m_prev = m_ref[h]
            m_new = jnp.maximum(m_prev, jnp.max(mx_ref[h], axis=0, keepdims=True) + big_c)
            pT = jnp.exp2(s_ref[h] - (m_new - big_c))
            alpha = jnp.exp2(m_prev - m_new)
            l_ref[h] = alpha * l_ref[h] + jnp.sum(pT.reshape(tk // SUBLANES, SUBLANES, tq), axis=0)
            vT = vT_ref[0, h * half:(h + 1) * half, pl.ds(k_off, tk)]
            acc_ref[h] = alpha * acc_ref[h] + _dot(vT, pT.astype(BF16))
            m_ref[h] = m_new

    def finalize(qi):
        outs = []
        for h in range(2):
            l = jnp.sum(l_ref[h], axis=0, keepdims=True)
            outs.append(acc_ref[h] / l)
        oT = jnp.concatenate(outs, axis=0)
        o_ref[0, pl.ds(pl.multiple_of(qi * tq, tq), tq), :] = oT.T.astype(o_ref.dtype)

    zero = jnp.int32(0)
    build_qaug(zero)
    build_qaug(zero + 1)
    reset_state()
    store_scores(qk_dots(zero, zero))
    mask_stage(zero, zero)

    def body(t, carry):
        qi, j = carry
        last = j == qi * tq // tk
        qi2 = jnp.where(last, qi + 1, qi)
        j2 = jnp.where(last, jstart_ref[g * nq + jnp.minimum(qi + 1, nq - 1)], j + 1)
        nxt = qk_dots(qi2, j2)
        pv_stage(qi, j)
        store_scores(nxt)

        @pl.when(j2 == qi2 * tq // tk)
        def _():
            mask_stage(qi2, j2)

        @pl.when(last)
        def _():
            finalize(qi)
            reset_state()

            @pl.when(qi + 2 < nq)
            def _():
                build_qaug(qi + 2)

        return qi2, j2

    qi, j = lax.fori_loop(0, nitems_ref[g] - 1, body, (jnp.int32(0), jnp.int32(0)))
    pv_stage(qi, j)
    finalize(qi)


def _prune_plan(stats, nq):
    B, n_blk = stats.shape[:2]
    c = stats[:, :, 0, :FOX_HEADS]
    qn = jnp.sqrt(stats[:, :, 1, :FOX_HEADS])
    kn = jnp.sqrt(jnp.max(stats[:, :, 2, :FOX_HEADS], axis=1))
    c_next = jnp.concatenate([c[:, 1:], c[:, -1:]], axis=1)
    bound = (c[:, :, None, :] - c_next[:, None, :, :]
             + 2.0 * NORM_SLACK * qn[:, :, None, :] * kn[:, None, None, :])
    i_idx = jnp.arange(n_blk)[:, None]
    j_idx = jnp.arange(n_blk)[None, :]
    skip = (bound <= PRUNE_LOG2) & (j_idx + 1 <= i_idx)[None, :, :, None]
    skip_pair = skip[..., 0::2] & skip[..., 1::2]
    jstart_blk = jnp.min(jnp.where(skip_pair, n_blk, j_idx[None, :, :, None]), axis=2)
    jstart_blk = jnp.transpose(jstart_blk, (0, 2, 1))
    per = FOX_TK // FOX_TQ
    jstart = jnp.repeat(jstart_blk, per, axis=2)
    last = (jnp.arange(nq) // per)[None, None, :]
    n_items = jnp.sum(last - jstart + 1, axis=2)
    return jstart.reshape(-1).astype(jnp.int32), n_items.reshape(-1).astype(jnp.int32)


def _fox_attention(fqT, kaug, fvT, drel, stats):
    B, _, S = fqT.shape
    tq, tk = FOX_TQ, FOX_TK
    n_blk = S // INPROJ_TM
    drel4 = drel.reshape(B, FOX_PAIRS, 2, S)
    c = stats[:, :, 0, :FOX_HEADS]
    c = jnp.transpose(c, (0, 2, 1)).reshape(B, FOX_PAIRS, 2 * n_blk, 1)
    crep = jnp.broadcast_to(c, (B, FOX_PAIRS, 2 * n_blk, tq))
    jstart, n_items = _prune_plan(stats, S // tq)
    grid_spec = pltpu.PrefetchScalarGridSpec(
        num_scalar_prefetch=2,
        grid=(B, FOX_PAIRS),
        in_specs=[
            pl.BlockSpec((1, LANES, S), lambda b, p, js, ni: (b, p, 0)),
            pl.BlockSpec((1, S, MXU_DIM), lambda b, p, js, ni: (b, 0, p)),
            pl.BlockSpec((1, LANES, S), lambda b, p, js, ni: (b, p, 0)),
            pl.BlockSpec((1, 1, 2, S), lambda b, p, js, ni: (b, p, 0, 0)),
            pl.BlockSpec((1, 1, 2 * n_blk, tq), lambda b, p, js, ni: (b, p, 0, 0)),
        ],
        out_specs=pl.BlockSpec((1, S, LANES), lambda b, p, js, ni: (b, 0, p)),
        scratch_shapes=[
            pltpu.VMEM((4, MXU_DIM, tq), BF16),
            pltpu.VMEM((2, tk, tq), F32),
            pltpu.VMEM((2, SUBLANES, tq), F32),
            pltpu.VMEM((2, FOX_HEAD_DIM, tq), F32),
            pltpu.VMEM((2, 1, tq), F32),
            pltpu.VMEM((2, SUBLANES, tq), F32),
        ])
    return pl.pallas_call(
        _fox_kernel,
        grid_spec=grid_spec,
        out_shape=jax.ShapeDtypeStruct((B, S, FOX_WIDTH), BF16),
        compiler_params=pltpu.CompilerParams(
            dimension_semantics=("arbitrary", "arbitrary"),
            vmem_limit_bytes=VMEM_LIMIT),
        name="fox_attn",
    )(jstart, n_items, fqT, kaug, fvT, drel4, crep)


def _gla_kernel(q_ref, k_ref, v_ref, r_ref, sm_ref, wa_ref, ba_ref, g_ref, o_ref, state_ref):
    ts = q_ref.shape[1]
    C, P = GLA_CHUNK, GLA_PAIR
    dk, dv, H = GLA_KEY_DIM, GLA_VAL_DIM, GLA_HEADS
    KW = GLA_KEY_WIDTH

    @pl.when(pl.program_id(1) == 0)
    def _():
        state_ref[...] = jnp.zeros_like(state_ref)

    la_all = _log_sigmoid(_dot(sm_ref[0].astype(BF16), wa_ref[...]) + ba_ref[...]) \
        * (1.0 / GLA_GATE_TEMP)

    ri = lax.broadcasted_iota(jnp.int32, (P, P), 0)
    ci = lax.broadcasted_iota(jnp.int32, (P, P), 1)
    same_chunk = (ri < C) == (ci < C)
    causal = same_chunk & (ci <= ri)
    tril2 = jnp.where(causal, 1.0, 0.0).astype(BF16)
    triu2 = jnp.where(same_chunk & (ri <= ci), 1.0, 0.0).astype(BF16)

    lane_q = lax.broadcasted_iota(jnp.int32, (P, KW), 1)
    lane_t = lax.broadcasted_iota(jnp.int32, (KW, P), 1)
    first = lane_t < C

    for pi in range(ts // P):
        rows = slice(pi * P, (pi + 1) * P)
        la = la_all[rows]
        b = _tri_left(tril2, la)
        bT = _tri_right(la.T, triu2)
        bl0 = bT[:, C - 1:C]
        bl1 = bT[:, P - 1:P]
        q_dec = q_ref[0, rows, :] * jnp.exp(b)
        kT = k_ref[0, rows, :].T
        k_invT = (kT * jnp.exp(-bT)).astype(BF16)
        k_teT = kT * jnp.exp(jnp.where(first, bl0, bl1) - bT)
        zt = jnp.zeros_like(k_teT)
        k_te0 = jnp.where(first, k_teT, zt).astype(BF16)
        k_te1 = jnp.where(first, zt, k_teT).astype(BF16)
        dec0 = jnp.exp(bl0)
        dec1 = jnp.exp(bl1)

        vs = [v_ref[0, rows, h * dv:(h + 1) * dv] for h in range(H)]
        kv0 = jnp.concatenate([_dot(k_te0[h * dk:(h + 1) * dk, :], vs[h]) for h in range(H)], axis=0)
        kv1 = jnp.concatenate([_dot(k_te1[h * dk:(h + 1) * dk, :], vs[h]) for h in range(H)], axis=0)
        s0 = state_ref[...]
        s1 = dec0 * s0 + kv0
        state_ref[...] = dec1 * s1 + kv1
        s0b = s0.astype(BF16)
        s1b = s1.astype(BF16)

        zq = jnp.zeros_like(q_dec)
        for h in range(H):
            hv = slice(h * dv, (h + 1) * dv)
            qm = jnp.where((lane_q >= h * dk) & (lane_q < (h + 1) * dk), q_dec, zq).astype(BF16)
            a = _dot(qm, k_invT)
            a = jnp.where(causal, a, 0.0).astype(BF16)
            o_inter = jnp.concatenate([_dot(qm[:C], s0b), _dot(qm[C:], s1b)], axis=0)
            o = _dot(a, vs[h]) + o_inter
            o = o * lax.rsqrt(jnp.mean(o * o, axis=-1, keepdims=True) + RMS_EPS)
            rr = r_ref[0, rows, hv]
            o = o * g_ref[:, hv] * (rr * (1.0 / (1.0 + jnp.exp(-rr))))
            o_ref[0, rows, hv] = o.astype(o_ref.dtype)


def _gla_attention(gq, gk, gv, gr, small, w_a2_pad, b_a, g_gla):
    B, S, _ = gq.shape
    ts = GLA_TS
    tok = lambda w: pl.BlockSpec((1, ts, w), lambda b, s: (b, s, 0))
    full = lambda r, c: pl.BlockSpec((r, c), lambda b, s: (0, 0))
    return pl.pallas_call(
        _gla_kernel,
        grid=(B, S // ts),
        in_specs=[
            tok(GLA_KEY_WIDTH), tok(GLA_KEY_WIDTH), tok(GLA_WIDTH), tok(GLA_WIDTH), tok(SMALL_W),
            full(SMALL_W, GLA_KEY_WIDTH), full(1, GLA_KEY_WIDTH), full(1, GLA_WIDTH),
        ],
        out_specs=tok(GLA_WIDTH),
        out_shape=jax.ShapeDtypeStruct((B, S, GLA_WIDTH), BF16),
        scratch_shapes=[pltpu.VMEM((GLA_KEY_WIDTH, GLA_VAL_DIM), F32)],
        compiler_params=pltpu.CompilerParams(
            dimension_semantics=("arbitrary", "arbitrary"), vmem_limit_bytes=VMEM_LIMIT),
        name="gla_attn",
    )(gq, gk, gv, gr, small, w_a2_pad, b_a, g_gla)


def _layer_norm(z, g, b):
    mu = jnp.mean(z, axis=-1, keepdims=True)
    d = z - mu
    var = jnp.mean(d * d, axis=-1, keepdims=True)
    return d * lax.rsqrt(var + LN_EPS) * g + b


def _tail_kernel(x_ref, fox_ref, gla_ref, mod_ref, wof_ref, wog_ref, ln1g_ref, ln1b_ref,
                 wg_ref, wu_ref, wd_ref, ln2g_ref, ln2b_ref, o_ref):
    m = mod_ref[0]
    y = _dot(fox_ref[...], wof_ref[...]) + _dot(gla_ref[...], wog_ref[...])
    x1 = _layer_norm(DEEPNORM_ALPHA * x_ref[...] + (1.0 + m[2:3]) * y,
                     ln1g_ref[...], ln1b_ref[...])
    u2 = (x1 * (1.0 + m[4:5]) + m[3:4]).astype(BF16)
    g = _dot(u2, wg_ref[...])
    up = _dot(u2, wu_ref[...])
    h = (g * (1.0 / (1.0 + jnp.exp(-g))) * up).astype(BF16)
    y2 = _dot(h, wd_ref[...])
    o_ref[...] = _layer_norm(DEEPNORM_ALPHA * x1 + (1.0 + m[5:6]) * y2,
                             ln2g_ref[...], ln2b_ref[...])


def _tail(x2d, fox2d, gla2d, mod, wof, wog, ln1g, ln1b, wg, wu, wd, ln2g, ln2b, steps_per_batch):
    T, D = x2d.shape
    tm = TAIL_TM
    const = lambda r, c: pl.BlockSpec((r, c), lambda i: (0, 0), pipeline_mode=pl.Buffered(1))
    tok = lambda w: pl.BlockSpec((tm, w), lambda i: (i, 0))
    return pl.pallas_call(
        _tail_kernel,
        grid=(T // tm,),
        in_specs=[
            tok(D), tok(FOX_WIDTH), tok(GLA_WIDTH),
            pl.BlockSpec((1, N_MOD, D), lambda i: (i // steps_per_batch, 0, 0)),
            const(FOX_WIDTH, D), const(GLA_WIDTH, D), const(1, D), const(1, D),
            const(D, D_FF), const(D, D_FF), const(D_FF, D), const(1, D), const(1, D),
        ],
        out_specs=tok(D),
        out_shape=jax.ShapeDtypeStruct((T, D), F32),
        compiler_params=pltpu.CompilerParams(
            dimension_semantics=("arbitrary",), vmem_limit_bytes=VMEM_LIMIT),
        name="tail",
    )(x2d, fox2d, gla2d, mod, wof, wog, ln1g, ln1b, wg, wu, wd, ln2g, ln2b)


def kernel(x, c, w_c, b_c, w_in, b_f, w_a2, b_a, g_gla, w_o, ln1_g, ln1_b,
           w_gate, w_up, w_down, ln2_g, ln2_b):
    B, S, D = x.shape
    assert D == D_MODEL and S % INPROJ_TM == 0 and INPROJ_TM == FOX_TK == 2 * FOX_TQ

    offs = [0]
    for wdt in IN_SPLITS:
        offs.append(offs[-1] + wdt)
    seg = lambda i: w_in[:, offs[i]:offs[i + 1]]
    w_small = jnp.concatenate(
        [seg(3), seg(7), jnp.zeros((D, SMALL_W - FOX_HEADS - GLA_GATE_RANK), F32)], axis=1)
    w_all = jnp.concatenate(
        [seg(0), seg(1), seg(2), seg(4), seg(5), seg(6), seg(8), w_small], axis=1).astype(BF16)
    b_f_pad = jnp.zeros((1, SMALL_W), F32).at[0, :FOX_HEADS].set(b_f)
    w_a2_pad = jnp.zeros((SMALL_W, GLA_KEY_WIDTH), F32) \
        .at[FOX_HEADS:FOX_HEADS + GLA_GATE_RANK].set(w_a2).astype(BF16)

    mod = _modulation(c, w_c, b_c)

    fqT, kaug, fvT, gq, gk, gv, gr, small, drel, stats = _in_projection(x, mod, w_all, b_f_pad)
    fox = _fox_attention(fqT, kaug, fvT, drel, stats)
    gla = _gla_attention(gq, gk, gv, gr, small, w_a2_pad,
                         b_a.reshape(1, -1), g_gla.reshape(1, -1))

    T = B * S
    w_o_b = w_o.astype(BF16)
    out = _tail(x.reshape(T, D), fox.reshape(T, FOX_WIDTH), gla.reshape(T, GLA_WIDTH), mod,
                w_o_b[:FOX_WIDTH], w_o_b[FOX_WIDTH:],
                ln1_g.reshape(1, D), ln1_b.reshape(1, D),
                w_gate.astype(BF16), w_up.astype(BF16), w_down.astype(BF16),
                ln2_g.reshape(1, D), ln2_b.reshape(1, D), S // TAIL_TM)
    return out.reshape(B, S, D)
```

```python
import math

import numpy as np
import jax
import jax.numpy as jnp
from jax import lax
from jax.experimental import pallas as pl
from jax.experimental.pallas import tpu as pltpu

F32 = jnp.float32
BF16 = jnp.bfloat16

D_MODEL = 1024
FOX_WIDTH = 512
FOX_HEAD_DIM = 64
FOX_HEADS = 8
FOX_PAIRS = FOX_HEADS // 2
GLA_WIDTH = 512
GLA_HEADS = 4
GLA_VAL_DIM = 128
GLA_KEY_DIM = 64
GLA_KEY_WIDTH = 256
GLA_GATE_RANK = 16
GLA_GATE_TEMP = 16.0
GLA_CHUNK = 64
D_FF = 2816
LN_EPS = 1e-5
RMS_EPS = 1e-6
N_MOD = 6
DEEPNORM_ALPHA = 2.0 ** 0.25
IN_SPLITS = (FOX_WIDTH, FOX_WIDTH, FOX_WIDTH, FOX_HEADS, GLA_KEY_WIDTH, GLA_KEY_WIDTH,
             GLA_WIDTH, GLA_GATE_RANK, GLA_WIDTH)
LOG2E = math.log2(math.e)

LANES = 128
SUBLANES = 8
MXU_DIM = 256
SMALL_W = LANES
VMEM_LIMIT = 56 * 1024 * 1024

MOD_TN = 1536
INPROJ_TM = 512
FOX_TQ = 256
FOX_TK = 512
GLA_TS = 512
GLA_PAIR = 2 * GLA_CHUNK
TAIL_TM = 512

NEG_BIG = -1e30
PRUNE_LOG2 = -1100.0
NORM_SLACK = 1.05

AUG_ROWS = 32


def _log_sigmoid(z):
    return jnp.minimum(z, 0.0) - jnp.log1p(jnp.exp(-jnp.abs(z)))


def _split3(a):
    hi = a.astype(BF16)
    r1 = a - hi.astype(F32)
    mid = r1.astype(BF16)
    lo = (r1 - mid.astype(F32)).astype(BF16)
    return hi, mid, lo


def _dot(a, b):
    return jnp.dot(a, b, preferred_element_type=F32)


def _tri_left(tri, a):
    hi, mid, lo = _split3(a)
    return _dot(tri, hi) + _dot(tri, mid) + _dot(tri, lo)


def _tri_right(a, tri):
    hi, mid, lo = _split3(a)
    return _dot(hi, tri) + _dot(mid, tri) + _dot(lo, tri)


def _mod_kernel(c_ref, w_ref, b_ref, o_ref):
    o_ref[...] = jnp.dot(c_ref[...], w_ref[...], preferred_element_type=F32,
                         precision=lax.Precision.HIGHEST) + b_ref[...]


def _modulation(c, w_c, b_c):
    B, D = c.shape
    N = w_c.shape[1]
    rows = SUBLANES
    c_pad = jnp.zeros((rows, D), F32).at[:B].set(c)
    out = pl.pallas_call(
        _mod_kernel,
        grid=(N // MOD_TN,),
        in_specs=[
            pl.BlockSpec((rows, D), lambda j: (0, 0)),
            pl.BlockSpec((D, MOD_TN), lambda j: (0, j)),
            pl.BlockSpec((1, MOD_TN), lambda j: (0, j)),
        ],
        out_specs=pl.BlockSpec((rows, MOD_TN), lambda j: (0, j)),
        out_shape=jax.ShapeDtypeStruct((rows, N), F32),
        compiler_params=pltpu.CompilerParams(
            dimension_semantics=("arbitrary",), vmem_limit_bytes=VMEM_LIMIT),
        name="adaln_mod",
    )(c_pad, w_c, b_c.reshape(1, N))
    return out[:B].reshape(B, N_MOD, D)


_C_FQ, _C_FK, _C_FV = 0, 512, 1024
_C_GQ, _C_SM, _C_GK, _C_GV, _C_GR = 1536, 1792, 2048, 2304, 2816
_W_COLS = _C_GR + GLA_WIDTH
_KAUG_W = FOX_PAIRS * MXU_DIM


def _head_indicator():
    ind = np.zeros((FOX_WIDTH, MXU_DIM), np.float32)
    for h in range(FOX_HEADS):
        ind[h * FOX_HEAD_DIM:(h + 1) * FOX_HEAD_DIM, h] = 1.0
    return ind


def _inproj_kernel(x_ref, mod_ref, w_ref, bf_ref, ind_ref,
                   fqT_ref, kaug_ref, fvT_ref, gq_ref, gk_ref, gv_ref, gr_ref, sm_ref,
                   drel_ref, stats_ref, carry_ref):
    tm = x_ref.shape[1]

    @pl.when(pl.program_id(1) == 0)
    def _():
        carry_ref[...] = jnp.zeros_like(carry_ref)

    m = mod_ref[0]
    u = (x_ref[0] * (1.0 + m[1:2]) + m[0:1]).astype(BF16)

    def proj(lo, width):
        return _dot(u, w_ref[:, lo:lo + width])

    gq_small = proj(_C_GQ, 2 * GLA_KEY_WIDTH)
    gq_ref[0] = gq_small[:, :GLA_KEY_WIDTH] * (GLA_KEY_DIM ** -0.5)
    small = gq_small[:, GLA_KEY_WIDTH:GLA_KEY_WIDTH + SMALL_W]
    sm_ref[0] = small

    fq = proj(_C_FQ, FOX_WIDTH) * (FOX_HEAD_DIM ** -0.5 * LOG2E)
    fqT_ref[0] = fq.T.astype(BF16)
    fk32 = proj(_C_FK, FOX_WIDTH)
    fk = fk32.astype(BF16)
    qn2 = jnp.max(_dot((fq * fq).astype(BF16), ind_ref[...])[:, :SMALL_W], axis=0, keepdims=True)
    kn2 = jnp.max(_dot((fk32 * fk32).astype(BF16), ind_ref[...])[:, :SMALL_W], axis=0, keepdims=True)

    zT = (small + bf_ref[...]).T[:FOX_HEADS, :]
    lfT = _log_sigmoid(zT) * LOG2E
    r = lax.broadcasted_iota(jnp.int32, (tm, tm), 0)
    c = lax.broadcasted_iota(jnp.int32, (tm, tm), 1)
    triu = jnp.where(r <= c, 1.0, 0.0).astype(BF16)
    drelT = _tri_right(lfT, triu)
    drel_ref[0] = drelT

    c_before = carry_ref[...]
    srow = lax.broadcasted_iota(jnp.int32, (SUBLANES, SMALL_W), 0)
    slane = lax.broadcasted_iota(jnp.int32, (SUBLANES, SMALL_W), 1)
    total = jnp.sum(jnp.where(srow == slane, drelT[:, tm - 1:tm], 0.0), axis=0, keepdims=True)
    carry_ref[...] = c_before + total

    hi, mid, lo = _split3(drelT)
    ones3 = jnp.where(lax.broadcasted_iota(jnp.int32, (SUBLANES, tm), 0) < 3, 1.0, 0.0)
    slabT = jnp.concatenate(
        [-hi.astype(F32), -mid.astype(F32), -lo.astype(F32), ones3,
         jnp.zeros((LANES - 4 * SUBLANES, tm), F32)], axis=0)
    slab = slabT.T.astype(BF16)

    fvT_ref[0] = proj(_C_FV, FOX_WIDTH).T.astype(BF16)
    gv_ref[0] = proj(_C_GV, GLA_WIDTH).astype(BF16)
    gr_ref[0] = proj(_C_GR, GLA_WIDTH)
    gk_ref[0] = proj(_C_GK, GLA_KEY_WIDTH)

    stats_ref[0, 0] = jnp.where(srow == 0, c_before,
                                jnp.where(srow == 1, qn2, jnp.where(srow == 2, kn2, 0.0)))
    for p in range(FOX_PAIRS):
        kaug_ref[0, :, p * MXU_DIM:p * MXU_DIM + LANES] = fk[:, p * LANES:(p + 1) * LANES]
        kaug_ref[0, :, p * MXU_DIM + LANES:(p + 1) * MXU_DIM] = slab


def _in_projection(x, mod, w_all, b_f_pad):
    B, S, D = x.shape
    tm = INPROJ_TM
    ind = _head_indicator()
    tok = lambda w: pl.BlockSpec((1, tm, w), lambda b, s: (b, s, 0))
    tokT = lambda h: pl.BlockSpec((1, h, tm), lambda b, s: (b, 0, s))
    outs = pl.pallas_call(
        _inproj_kernel,
        grid=(B, S // tm),
        in_specs=[
            tok(D),
            pl.BlockSpec((1, N_MOD, D), lambda b, s: (b, 0, 0)),
            pl.BlockSpec((D, _W_COLS), lambda b, s: (0, 0)),
            pl.BlockSpec((1, SMALL_W), lambda b, s: (0, 0)),
            pl.BlockSpec(ind.shape, lambda b, s: (0, 0)),
        ],
        out_specs=[
            tokT(FOX_WIDTH), tok(_KAUG_W), tokT(FOX_WIDTH),
            tok(GLA_KEY_WIDTH), tok(GLA_KEY_WIDTH), tok(GLA_WIDTH), tok(GLA_WIDTH),
            tok(SMALL_W),
            tokT(FOX_HEADS),
            pl.BlockSpec((1, 1, SUBLANES, SMALL_W), lambda b, s: (b, s, 0, 0)),
        ],
        out_shape=[
            jax.ShapeDtypeStruct((B, FOX_WIDTH, S), BF16),
            jax.ShapeDtypeStruct((B, S, _KAUG_W), BF16),
            jax.ShapeDtypeStruct((B, FOX_WIDTH, S), BF16),
            jax.ShapeDtypeStruct((B, S, GLA_KEY_WIDTH), F32),
            jax.ShapeDtypeStruct((B, S, GLA_KEY_WIDTH), F32),
            jax.ShapeDtypeStruct((B, S, GLA_WIDTH), BF16),
            jax.ShapeDtypeStruct((B, S, GLA_WIDTH), F32),
            jax.ShapeDtypeStruct((B, S, SMALL_W), F32),
            jax.ShapeDtypeStruct((B, FOX_HEADS, S), F32),
            jax.ShapeDtypeStruct((B, S // tm, SUBLANES, SMALL_W), F32),
        ],
        scratch_shapes=[pltpu.VMEM((1, SMALL_W), F32)],
        compiler_params=pltpu.CompilerParams(
            dimension_semantics=("arbitrary", "arbitrary"), vmem_limit_bytes=VMEM_LIMIT),
        name="in_proj",
    )(x, mod, w_all, b_f_pad, jnp.asarray(ind, BF16))
    return outs


def _fox_kernel(jstart_ref, nitems_ref, qT_ref, kaug_ref, vT_ref, drel_ref, crep_ref, o_ref,
                qaug_ref, s_ref, mx_ref, acc_ref, m_ref, l_ref):
    tq, tk = FOX_TQ, FOX_TK
    half = FOX_HEAD_DIM
    S = kaug_ref.shape[1]
    nq = S // tq
    n_blk = crep_ref.shape[2] // 2
    g = pl.program_id(0) * pl.num_programs(1) + pl.program_id(1)

    def build_qaug(qi):
        q0 = pl.multiple_of(qi * tq, tq)
        qT = qT_ref[0, :, pl.ds(q0, tq)]
        dr = drel_ref[0, 0, :, pl.ds(q0, tq)]
        row = lax.broadcasted_iota(jnp.int32, (2 * half, tq), 0)
        r32 = lax.broadcasted_iota(jnp.int32, (AUG_ROWS, tq), 0)
        zq = jnp.zeros_like(qT)
        for h in range(2):
            buf = (qi % 2) * 2 + h
            own = (row < half) if h == 0 else (row >= half)
            qaug_ref[buf, 0:2 * half, :] = jnp.where(own, qT, zq)
            hi, mid, lo = _split3(dr[h:h + 1])
            gh = 2 * pl.program_id(1) + h
            pick = (r32 == gh) | (r32 == gh + SUBLANES) | (r32 == gh + 2 * SUBLANES)
            slab = jnp.where(r32 == 24, hi.astype(F32),
                             jnp.where(r32 == 25, mid.astype(F32),
                                       jnp.where(r32 == 26, lo.astype(F32),
                                                 jnp.where(pick, 1.0, 0.0))))
            qaug_ref[buf, 2 * half:2 * half + AUG_ROWS, :] = slab.astype(BF16)
            qaug_ref[buf, 2 * half + AUG_ROWS:, :] = jnp.zeros(
                (MXU_DIM - 2 * half - AUG_ROWS, tq), BF16)

    def reset_state():
        m_ref[...] = jnp.full_like(m_ref, NEG_BIG)
        l_ref[...] = jnp.zeros_like(l_ref)
        acc_ref[...] = jnp.zeros_like(acc_ref)

    def col_max8(sT):
        return jnp.max(sT.reshape(tk // SUBLANES, SUBLANES, tq), axis=0)

    def qk_dots(qi, j):
        kt = kaug_ref[0, pl.ds(pl.multiple_of(j * tk, tk), tk), :]
        return [_dot(kt, qaug_ref[(qi % 2) * 2 + h]) for h in range(2)]

    def store_scores(scores):
        for h in range(2):
            s_ref[h] = scores[h]
            mx_ref[h] = col_max8(scores[h])

    def mask_stage(qi, j):
        kr = lax.broadcasted_iota(jnp.int32, (tk, tq), 0)
        qc = lax.broadcasted_iota(jnp.int32, (tk, tq), 1)
        keep = (kr - qc) <= (qi * tq - j * tk)
        for h in range(2):
            sT = jnp.where(keep, s_ref[h], NEG_BIG)
            s_ref[h] = sT
            mx_ref[h] = col_max8(sT)

    def pv_stage(qi, j):
        k_off = pl.multiple_of(j * tk, tk)
        i_blk = qi * tq // INPROJ_TM
        for h in range(2):
            big_c = (crep_ref[0, 0, pl.ds(h * n_blk + i_blk, 1), :]
                     - crep_ref[0, 0, pl.ds(h * n_blk + j, 1), :])
            m_prev = m_ref[h]
            m_new = jnp.maximum(m_prev, jnp.max(mx_ref[h], axis=0, keepdims=True) + big_c)
            pT = jnp.exp2(s_ref[h] - (m_new - big_c))
            alpha = jnp.exp2(m_prev - m_new)
            l_ref[h] = alpha * l_ref[h] + jnp.sum(pT.reshape(tk // SUBLANES, SUBLANES, tq), axis=0)
            vT = vT_ref[0, h * half:(h + 1) * half, pl.ds(k_off, tk)]
            acc_ref[h] = alpha * acc_ref[h] + _dot(vT, pT.astype(BF16))
            m_ref[h] = m_new

    def finalize(qi):
        outs = []
        for h in range(2):
            l = jnp.sum(l_ref[h], axis=0, keepdims=True)
            outs.append(acc_ref[h] / l)
        oT = jnp.concatenate(outs, axis=0)
        o_ref[0, pl.ds(pl.multiple_of(qi * tq, tq), tq), :] = oT.T.astype(o_ref.dtype)

    zero = jnp.int32(0)
    build_qaug(zero)
    build_qaug(zero + 1)
    reset_state()
    store_scores(qk_dots(zero, zero))
    mask_stage(zero, zero)

    def body(t, carry):
        qi, j = carry
        last = j == qi * tq // tk
        qi2 = jnp.where(last, qi + 1, qi)
        j2 = jnp.where(last, jstart_ref[g * nq + jnp.minimum(qi + 1, nq - 1)], j + 1)
        nxt = qk_dots(qi2, j2)
        pv_stage(qi, j)
        store_scores(nxt)

        @pl.when(j2 == qi2 * tq // tk)
        def _():
            mask_stage(qi2, j2)

        @pl.when(last)
        def _():
            finalize(qi)
            reset_state()

            @pl.when(qi + 2 < nq)
            def _():
                build_qaug(qi + 2)

        return qi2, j2

    qi, j = lax.fori_loop(0, nitems_ref[g] - 1, body, (jnp.int32(0), jnp.int32(0)))
    pv_stage(qi, j)
    finalize(qi)


def _prune_plan(stats, nq):
    B, n_blk = stats.shape[:2]
    c = stats[:, :, 0, :FOX_HEADS]
    qn = jnp.sqrt(stats[:, :, 1, :FOX_HEADS])
    kn = jnp.sqrt(jnp.max(stats[:, :, 2, :FOX_HEADS], axis=1))
    c_next = jnp.concatenate([c[:, 1:], c[:, -1:]], axis=1)
    bound = (c[:, :, None, :] - c_next[:, None, :, :]
             + 2.0 * NORM_SLACK * qn[:, :, None, :] * kn[:, None, None, :])
    i_idx = jnp.arange(n_blk)[:, None]
    j_idx = jnp.arange(n_blk)[None, :]
    skip = (bound <= PRUNE_LOG2) & (j_idx + 1 <= i_idx)[None, :, :, None]
    skip_pair = skip[..., 0::2] & skip[..., 1::2]
    jstart_blk = jnp.min(jnp.where(skip_pair, n_blk, j_idx[None, :, :, None]), axis=2)
    jstart_blk = jnp.transpose(jstart_blk, (0, 2, 1))
    per = FOX_TK // FOX_TQ
    jstart = jnp.repeat(jstart_blk, per, axis=2)
    last = (jnp.arange(nq) // per)[None, None, :]
    n_items = jnp.sum(last - jstart + 1, axis=2)
    return jstart.reshape(-1).astype(jnp.int32), n_items.reshape(-1).astype(jnp.int32)


def _fox_attention(fqT, kaug, fvT, drel, stats):
    B, _, S = fqT.shape
    tq, tk = FOX_TQ, FOX_TK
    n_blk = S // INPROJ_TM
    drel4 = drel.reshape(B, FOX_PAIRS, 2, S)
    c = stats[:, :, 0, :FOX_HEADS]
    c = jnp.transpose(c, (0, 2, 1)).reshape(B, FOX_PAIRS, 2 * n_blk, 1)
    crep = jnp.broadcast_to(c, (B, FOX_PAIRS, 2 * n_blk, tq))
    jstart, n_items = _prune_plan(stats, S // tq)
    grid_spec = pltpu.PrefetchScalarGridSpec(
        num_scalar_prefetch=2,
        grid=(B, FOX_PAIRS),
        in_specs=[
            pl.BlockSpec((1, LANES, S), lambda b, p, js, ni: (b, p, 0)),
            pl.BlockSpec((1, S, MXU_DIM), lambda b, p, js, ni: (b, 0, p)),
            pl.BlockSpec((1, LANES, S), lambda b, p, js, ni: (b, p, 0)),
            pl.BlockSpec((1, 1, 2, S), lambda b, p, js, ni: (b, p, 0, 0)),
            pl.BlockSpec((1, 1, 2 * n_blk, tq), lambda b, p, js, ni: (b, p, 0, 0)),
        ],
        out_specs=pl.BlockSpec((1, S, LANES), lambda b, p, js, ni: (b, 0, p)),
        scratch_shapes=[
            pltpu.VMEM((4, MXU_DIM, tq), BF16),
            pltpu.VMEM((2, tk, tq), F32),
            pltpu.VMEM((2, SUBLANES, tq), F32),
            pltpu.VMEM((2, FOX_HEAD_DIM, tq), F32),
            pltpu.VMEM((2, 1, tq), F32),
            pltpu.VMEM((2, SUBLANES, tq), F32),
        ])
    return pl.pallas_call(
        _fox_kernel,
        grid_spec=grid_spec,
        out_shape=jax.ShapeDtypeStruct((B, S, FOX_WIDTH), BF16),
        compiler_params=pltpu.CompilerParams(
            dimension_semantics=("arbitrary", "arbitrary"),
            vmem_limit_bytes=VMEM_LIMIT),
        name="fox_attn",
    )(jstart, n_items, fqT, kaug, fvT, drel4, crep)


def _gla_kernel(q_ref, k_ref, v_ref, r_ref, sm_ref, wa_ref, ba_ref, g_ref, o_ref, state_ref):
    ts = q_ref.shape[1]
    C, P = GLA_CHUNK, GLA_PAIR
    dk, dv, H = GLA_KEY_DIM, GLA_VAL_DIM, GLA_HEADS
    KW = GLA_KEY_WIDTH

    @pl.when(pl.program_id(1) == 0)
    def _():
        state_ref[...] = jnp.zeros_like(state_ref)

    la_all = _log_sigmoid(_dot(sm_ref[0].astype(BF16), wa_ref[...]) + ba_ref[...]) \
        * (1.0 / GLA_GATE_TEMP)

    ri = lax.broadcasted_iota(jnp.int32, (P, P), 0)
    ci = lax.broadcasted_iota(jnp.int32, (P, P), 1)
    same_chunk = (ri < C) == (ci < C)
    causal = same_chunk & (ci <= ri)
    tril2 = jnp.where(causal, 1.0, 0.0).astype(BF16)
    triu2 = jnp.where(same_chunk & (ri <= ci), 1.0, 0.0).astype(BF16)

    lane_q = lax.broadcasted_iota(jnp.int32, (P, KW), 1)
    lane_t = lax.broadcasted_iota(jnp.int32, (KW, P), 1)
    first = lane_t < C

    for pi in range(ts // P):
        rows = slice(pi * P, (pi + 1) * P)
        la = la_all[rows]
        b = _tri_left(tril2, la)
        bT = _tri_right(la.T, triu2)
        bl0 = bT[:, C - 1:C]
        bl1 = bT[:, P - 1:P]
        q_dec = q_ref[0, rows, :] * jnp.exp(b)
        kT = k_ref[0, rows, :].T
        k_invT = (kT * jnp.exp(-bT)).astype(BF16)
        k_teT = kT * jnp.exp(jnp.where(first, bl0, bl1) - bT)
        zt = jnp.zeros_like(k_teT)
        k_te0 = jnp.where(first, k_teT, zt).astype(BF16)
        k_te1 = jnp.where(first, zt, k_teT).astype(BF16)
        dec0 = jnp.exp(bl0)
        dec1 = jnp.exp(bl1)

        vs = [v_ref[0, rows, h * dv:(h + 1) * dv] for h in range(H)]
        kv0 = jnp.concatenate([_dot(k_te0[h * dk:(h + 1) * dk, :], vs[h]) for h in range(H)], axis=0)
        kv1 = jnp.concatenate([_dot(k_te1[h * dk:(h + 1) * dk, :], vs[h]) for h in range(H)], axis=0)
        s0 = state_ref[...]
        s1 = dec0 * s0 + kv0
        state_ref[...] = dec1 * s1 + kv1
        s0b = s0.astype(BF16)
        s1b = s1.astype(BF16)

        zq = jnp.zeros_like(q_dec)
        for h in range(H):
            hv = slice(h * dv, (h + 1) * dv)
            qm = jnp.where((lane_q >= h * dk) & (lane_q < (h + 1) * dk), q_dec, zq).astype(BF16)
            a = _dot(qm, k_invT)
            a = jnp.where(causal, a, 0.0).astype(BF16)
            o_inter = jnp.concatenate([_dot(qm[:C], s0b), _dot(qm[C:], s1b)], axis=0)
            o = _dot(a, vs[h]) + o_inter
            o = o * lax.rsqrt(jnp.mean(o * o, axis=-1, keepdims=True) + RMS_EPS)
            rr = r_ref[0, rows, hv]
            o = o * g_ref[:, hv] * (rr * (1.0 / (1.0 + jnp.exp(-rr))))
            o_ref[0, rows, hv] = o.astype(o_ref.dtype)


def _gla_attention(gq, gk, gv, gr, small, w_a2_pad, b_a, g_gla):
    B, S, _ = gq.shape
    ts = GLA_TS
    tok = lambda w: pl.BlockSpec((1, ts, w), lambda b, s: (b, s, 0))
    full = lambda r, c: pl.BlockSpec((r, c), lambda b, s: (0, 0))
    return pl.pallas_call(
        _gla_kernel,
        grid=(B, S // ts),
        in_specs=[
            tok(GLA_KEY_WIDTH), tok(GLA_KEY_WIDTH), tok(GLA_WIDTH), tok(GLA_WIDTH), tok(SMALL_W),
            full(SMALL_W, GLA_KEY_WIDTH), full(1, GLA_KEY_WIDTH), full(1, GLA_WIDTH),
        ],
        out_specs=tok(GLA_WIDTH),
        out_shape=jax.ShapeDtypeStruct((B, S, GLA_WIDTH), BF16),
        scratch_shapes=[pltpu.VMEM((GLA_KEY_WIDTH, GLA_VAL_DIM), F32)],
        compiler_params=pltpu.CompilerParams(
            dimension_semantics=("arbitrary", "arbitrary"), vmem_limit_bytes=VMEM_LIMIT),
        name="gla_attn",
    )(gq, gk, gv, gr, small, w_a2_pad, b_a, g_gla)


def _layer_norm(z, g, b):
    mu = jnp.mean(z, axis=-1, keepdims=True)
    d = z - mu
    var = jnp.mean(d * d, axis=-1, keepdims=True)
    return d * lax.rsqrt(var + LN_EPS) * g + b


def _tail_kernel(x_ref, fox_ref, gla_ref, mod_ref, wof_ref, wog_ref, ln1g_ref, ln1b_ref,
                 wg_ref, wu_ref, wd_ref, ln2g_ref, ln2b_ref, o_ref):
    m = mod_ref[0]
    y = _dot(fox_ref[...], wof_ref[...]) + _dot(gla_ref[...], wog_ref[...])
    x1 = _layer_norm(DEEPNORM_ALPHA * x_ref[...] + (1.0 + m[2:3]) * y,
                     ln1g_ref[...], ln1b_ref[...])
    u2 = (x1 * (1.0 + m[4:5]) + m[3:4]).astype(BF16)
    g = _dot(u2, wg_ref[...])
    up = _dot(u2, wu_ref[...])
    h = (g * (1.0 / (1.0 + jnp.exp(-g))) * up).astype(BF16)
    y2 = _dot(h, wd_ref[...])
    o_ref[...] = _layer_norm(DEEPNORM_ALPHA * x1 + (1.0 + m[5:6]) * y2,
                             ln2g_ref[...], ln2b_ref[...])


def _tail(x2d, fox2d, gla2d, mod, wof, wog, ln1g, ln1b, wg, wu, wd, ln2g, ln2b, steps_per_batch):
    T, D = x2d.shape
    tm = TAIL_TM
    const = lambda r, c: pl.BlockSpec((r, c), lambda i: (0, 0), pipeline_mode=pl.Buffered(1))
    tok = lambda w: pl.BlockSpec((tm, w), lambda i: (i, 0))
    return pl.pallas_call(
        _tail_kernel,
        grid=(T // tm,),
        in_specs=[
            tok(D), tok(FOX_WIDTH), tok(GLA_WIDTH),
            pl.BlockSpec((1, N_MOD, D), lambda i: (i // steps_per_batch, 0, 0)),
            const(FOX_WIDTH, D), const(GLA_WIDTH, D), const(1, D), const(1, D),
            const(D, D_FF), const(D, D_FF), const(D_FF, D), const(1, D), const(1, D),
        ],
        out_specs=tok(D),
        out_shape=jax.ShapeDtypeStruct((T, D), F32),
        compiler_params=pltpu.CompilerParams(
            dimension_semantics=("arbitrary",), vmem_limit_bytes=VMEM_LIMIT),
        name="tail",
    )(x2d, fox2d, gla2d, mod, wof, wog, ln1g, ln1b, wg, wu, wd, ln2g, ln2b)


def kernel(x, c, w_c, b_c, w_in, b_f, w_a2, b_a, g_gla, w_o, ln1_g, ln1_b,
           w_gate, w_up, w_down, ln2_g, ln2_b):
    B, S, D = x.shape
    assert D == D_MODEL and S % INPROJ_TM == 0 and INPROJ_TM == FOX_TK == 2 * FOX_TQ

    offs = [0]
    for wdt in IN_SPLITS:
        offs.append(offs[-1] + wdt)
    seg = lambda i: w_in[:, offs[i]:offs[i + 1]]
    w_small = jnp.concatenate(
        [seg(3), seg(7), jnp.zeros((D, SMALL_W - FOX_HEADS - GLA_GATE_RANK), F32)], axis=1)
    w_all = jnp.concatenate(
        [seg(0), seg(1), seg(2), seg(4), w_small, jnp.zeros((D, SMALL_W), F32),
         seg(5), seg(6), seg(8)], axis=1).astype(BF16)
    b_f_pad = jnp.zeros((1, SMALL_W), F32).at[0, :FOX_HEADS].set(b_f)
    w_a2_pad = jnp.zeros((SMALL_W, GLA_KEY_WIDTH), F32) \
        .at[FOX_HEADS:FOX_HEADS + GLA_GATE_RANK].set(w_a2).astype(BF16)

    mod = _modulation(c, w_c, b_c)

    fqT, kaug, fvT, gq, gk, gv, gr, small, drel, stats = _in_projection(x, mod, w_all, b_f_pad)
    fox = _fox_attention(fqT, kaug, fvT, drel, stats)
    gla = _gla_attention(gq, gk, gv, gr, small, w_a2_pad,
                         b_a.reshape(1, -1), g_gla.reshape(1, -1))

    T = B * S
    w_o_b = w_o.astype(BF16)
    out = _tail(x.reshape(T, D), fox.reshape(T, FOX_WIDTH), gla.reshape(T, GLA_WIDTH), mod,
                w_o_b[:FOX_WIDTH], w_o_b[FOX_WIDTH:],
                ln1_g.reshape(1, D), ln1_b.reshape(1, D),
                w_gate.astype(BF16), w_up.astype(BF16), w_down.astype(BF16),
                ln2_g.reshape(1, D), ln2_b.reshape(1, D), S // TAIL_TM)
    return out.reshape(B, S, D)
```

```python
import math

import numpy as np
import jax
import jax.numpy as jnp
from jax import lax
from jax.experimental import pallas as pl
from jax.experimental.pallas import tpu as pltpu

F32 = jnp.float32
BF16 = jnp.bfloat16

D_MODEL = 1024
FOX_WIDTH = 512
FOX_HEAD_DIM = 64
FOX_HEADS = 8
FOX_PAIRS = FOX_HEADS // 2
GLA_WIDTH = 512
GLA_HEADS = 4
GLA_VAL_DIM = 128
GLA_KEY_DIM = 64
GLA_KEY_WIDTH = 256
GLA_GATE_RANK = 16
GLA_GATE_TEMP = 16.0
GLA_CHUNK = 64
D_FF = 2816
LN_EPS = 1e-5
RMS_EPS = 1e-6
N_MOD = 6
DEEPNORM_ALPHA = 2.0 ** 0.25
IN_SPLITS = (FOX_WIDTH, FOX_WIDTH, FOX_WIDTH, FOX_HEADS, GLA_KEY_WIDTH, GLA_KEY_WIDTH,
             GLA_WIDTH, GLA_GATE_RANK, GLA_WIDTH)
LOG2E = math.log2(math.e)

LANES = 128
SUBLANES = 8
MXU_DIM = 256
SMALL_W = LANES
VMEM_LIMIT = 56 * 1024 * 1024

MOD_TN = 1536
INPROJ_TM = 512
FOX_TQ = 256
FOX_TK = 512
GLA_TS = 512
GLA_PAIR = 2 * GLA_CHUNK
TAIL_TM = 512

NEG_BIG = -1e30
PRUNE_LOG2 = -1100.0
NORM_SLACK = 1.05

AUG_ROWS = 32


def _log_sigmoid(z):
    return jnp.minimum(z, 0.0) - jnp.log(1.0 + jnp.exp(-jnp.abs(z)))


def _split3(a):
    hi = a.astype(BF16)
    r1 = a - hi.astype(F32)
    mid = r1.astype(BF16)
    lo = (r1 - mid.astype(F32)).astype(BF16)
    return hi, mid, lo


def _dot(a, b):
    return jnp.dot(a, b, preferred_element_type=F32)


def _tri_left(tri, a):
    hi, mid, lo = _split3(a)
    return _dot(tri, hi) + _dot(tri, mid) + _dot(tri, lo)


def _tri_right(a, tri):
    hi, mid, lo = _split3(a)
    return _dot(hi, tri) + _dot(mid, tri) + _dot(lo, tri)


def _mod_kernel(c_ref, w_ref, b_ref, o_ref):
    o_ref[...] = jnp.dot(c_ref[...], w_ref[...], preferred_element_type=F32,
                         precision=lax.Precision.HIGHEST) + b_ref[...]


def _modulation(c, w_c, b_c):
    B, D = c.shape
    N = w_c.shape[1]
    rows = SUBLANES
    c_pad = jnp.zeros((rows, D), F32).at[:B].set(c)
    out = pl.pallas_call(
        _mod_kernel,
        grid=(N // MOD_TN,),
        in_specs=[
            pl.BlockSpec((rows, D), lambda j: (0, 0)),
            pl.BlockSpec((D, MOD_TN), lambda j: (0, j)),
            pl.BlockSpec((1, MOD_TN), lambda j: (0, j)),
        ],
        out_specs=pl.BlockSpec((rows, MOD_TN), lambda j: (0, j)),
        out_shape=jax.ShapeDtypeStruct((rows, N), F32),
        compiler_params=pltpu.CompilerParams(
            dimension_semantics=("arbitrary",), vmem_limit_bytes=VMEM_LIMIT),
        name="adaln_mod",
    )(c_pad, w_c, b_c.reshape(1, N))
    return out[:B].reshape(B, N_MOD, D)


_C_FQ, _C_FK, _C_FV = 0, 512, 1024
_C_GQ, _C_SM, _C_GK, _C_GV, _C_GR = 1536, 1792, 2048, 2304, 2816
_W_COLS = _C_GR + GLA_WIDTH
_KAUG_W = FOX_PAIRS * MXU_DIM


def _head_indicator():
    ind = np.zeros((FOX_WIDTH, MXU_DIM), np.float32)
    for h in range(FOX_HEADS):
        ind[h * FOX_HEAD_DIM:(h + 1) * FOX_HEAD_DIM, h] = 1.0
    return ind


def _inproj_kernel(x_ref, mod_ref, w_ref, bf_ref, ind_ref,
                   fqT_ref, kaug_ref, fvT_ref, gq_ref, gk_ref, gv_ref, gr_ref, sm_ref,
                   drel_ref, stats_ref, carry_ref):
    tm = x_ref.shape[1]

    @pl.when(pl.program_id(1) == 0)
    def _():
        carry_ref[...] = jnp.zeros_like(carry_ref)

    m = mod_ref[0]
    u = (x_ref[0] * (1.0 + m[1:2]) + m[0:1]).astype(BF16)

    def proj(lo, width):
        return _dot(u, w_ref[:, lo:lo + width])

    gq_small = proj(_C_GQ, 2 * GLA_KEY_WIDTH)
    gq_ref[0] = gq_small[:, :GLA_KEY_WIDTH] * (GLA_KEY_DIM ** -0.5)
    small = gq_small[:, GLA_KEY_WIDTH:GLA_KEY_WIDTH + SMALL_W]
    sm_ref[0] = small

    fq = proj(_C_FQ, FOX_WIDTH) * (FOX_HEAD_DIM ** -0.5 * LOG2E)
    fqT_ref[0] = fq.T.astype(BF16)
    fk32 = proj(_C_FK, FOX_WIDTH)
    fk = fk32.astype(BF16)
    qn2 = jnp.max(_dot((fq * fq).astype(BF16), ind_ref[...])[:, :SMALL_W], axis=0, keepdims=True)
    kn2 = jnp.max(_dot((fk32 * fk32).astype(BF16), ind_ref[...])[:, :SMALL_W], axis=0, keepdims=True)

    zT = (small + bf_ref[...]).T[:FOX_HEADS, :]
    lfT = _log_sigmoid(zT) * LOG2E
    r = lax.broadcasted_iota(jnp.int32, (tm, tm), 0)
    c = lax.broadcasted_iota(jnp.int32, (tm, tm), 1)
    triu = jnp.where(r <= c, 1.0, 0.0).astype(BF16)
    drelT = _tri_right(lfT, triu)
    drel_ref[0] = drelT

    c_before = carry_ref[...]
    srow = lax.broadcasted_iota(jnp.int32, (SUBLANES, SMALL_W), 0)
    slane = lax.broadcasted_iota(jnp.int32, (SUBLANES, SMALL_W), 1)
    total = jnp.sum(jnp.where(srow == slane, drelT[:, tm - 1:tm], 0.0), axis=0, keepdims=True)
    carry_ref[...] = c_before + total

    hi, mid, lo = _split3(drelT)
    ones3 = jnp.where(lax.broadcasted_iota(jnp.int32, (SUBLANES, tm), 0) < 3, 1.0, 0.0)
    slabT = jnp.concatenate(
        [-hi.astype(F32), -mid.astype(F32), -lo.astype(F32), ones3,
         jnp.zeros((LANES - 4 * SUBLANES, tm), F32)], axis=0)
    slab = slabT.T.astype(BF16)

    fvT_ref[0] = proj(_C_FV, FOX_WIDTH).T.astype(BF16)
    gv_ref[0] = proj(_C_GV, GLA_WIDTH).astype(BF16)
    gr_ref[0] = proj(_C_GR, GLA_WIDTH)
    gk_ref[0] = proj(_C_GK, GLA_KEY_WIDTH)

    stats_ref[0, 0] = jnp.where(srow == 0, c_before,
                                jnp.where(srow == 1, qn2, jnp.where(srow == 2, kn2, 0.0)))
    for p in range(FOX_PAIRS):
        kaug_ref[0, :, p * MXU_DIM:p * MXU_DIM + LANES] = fk[:, p * LANES:(p + 1) * LANES]
        kaug_ref[0, :, p * MXU_DIM + LANES:(p + 1) * MXU_DIM] = slab


def _in_projection(x, mod, w_all, b_f_pad):
    B, S, D = x.shape
    tm = INPROJ_TM
    ind = _head_indicator()
    tok = lambda w: pl.BlockSpec((1, tm, w), lambda b, s: (b, s, 0))
    tokT = lambda h: pl.BlockSpec((1, h, tm), lambda b, s: (b, 0, s))
    outs = pl.pallas_call(
        _inproj_kernel,
        grid=(B, S // tm),
        in_specs=[
            tok(D),
            pl.BlockSpec((1, N_MOD, D), lambda b, s: (b, 0, 0)),
            pl.BlockSpec((D, _W_COLS), lambda b, s: (0, 0)),
            pl.BlockSpec((1, SMALL_W), lambda b, s: (0, 0)),
            pl.BlockSpec(ind.shape, lambda b, s: (0, 0)),
        ],
        out_specs=[
            tokT(FOX_WIDTH), tok(_KAUG_W), tokT(FOX_WIDTH),
            tok(GLA_KEY_WIDTH), tok(GLA_KEY_WIDTH), tok(GLA_WIDTH), tok(GLA_WIDTH),
            tok(SMALL_W),
            tokT(FOX_HEADS),
            pl.BlockSpec((1, 1, SUBLANES, SMALL_W), lambda b, s: (b, s, 0, 0)),
        ],
        out_shape=[
            jax.ShapeDtypeStruct((B, FOX_WIDTH, S), BF16),
            jax.ShapeDtypeStruct((B, S, _KAUG_W), BF16),
            jax.ShapeDtypeStruct((B, FOX_WIDTH, S), BF16),
            jax.ShapeDtypeStruct((B, S, GLA_KEY_WIDTH), F32),
            jax.ShapeDtypeStruct((B, S, GLA_KEY_WIDTH), F32),
            jax.ShapeDtypeStruct((B, S, GLA_WIDTH), BF16),
            jax.ShapeDtypeStruct((B, S, GLA_WIDTH), F32),
            jax.ShapeDtypeStruct((B, S, SMALL_W), F32),
            jax.ShapeDtypeStruct((B, FOX_HEADS, S), F32),
            jax.ShapeDtypeStruct((B, S // tm, SUBLANES, SMALL_W), F32),
        ],
        scratch_shapes=[pltpu.VMEM((1, SMALL_W), F32)],
        compiler_params=pltpu.CompilerParams(
            dimension_semantics=("arbitrary", "arbitrary"), vmem_limit_bytes=VMEM_LIMIT),
        name="in_proj",
    )(x, mod, w_all, b_f_pad, jnp.asarray(ind, BF16))
    return outs


def _fox_kernel(jstart_ref, nitems_ref, qT_ref, kaug_ref, vT_ref, drel_ref, crep_ref, o_ref,
                qaug_ref, s_ref, mx_ref, acc_ref, m_ref, l_ref):
    tq, tk = FOX_TQ, FOX_TK
    half = FOX_HEAD_DIM
    S = kaug_ref.shape[1]
    nq = S // tq
    n_blk = crep_ref.shape[2] // 2
    g = pl.program_id(0) * pl.num_programs(1) + pl.program_id(1)

    def build_qaug(qi):
        q0 = pl.multiple_of(qi * tq, tq)
        qT = qT_ref[0, :, pl.ds(q0, tq)]
        dr = drel_ref[0, 0, :, pl.ds(q0, tq)]
        row = lax.broadcasted_iota(jnp.int32, (2 * half, tq), 0)
        r32 = lax.broadcasted_iota(jnp.int32, (AUG_ROWS, tq), 0)
        zq = jnp.zeros_like(qT)
        for h in range(2):
            buf = (qi % 2) * 2 + h
            own = (row < half) if h == 0 else (row >= half)
            qaug_ref[buf, 0:2 * half, :] = jnp.where(own, qT, zq)
            hi, mid, lo = _split3(dr[h:h + 1])
            gh = 2 * pl.program_id(1) + h
            pick = (r32 == gh) | (r32 == gh + SUBLANES) | (r32 == gh + 2 * SUBLANES)
            slab = jnp.where(r32 == 24, hi.astype(F32),
                             jnp.where(r32 == 25, mid.astype(F32),
                                       jnp.where(r32 == 26, lo.astype(F32),
                                                 jnp.where(pick, 1.0, 0.0))))
            qaug_ref[buf, 2 * half:2 * half + AUG_ROWS, :] = slab.astype(BF16)
            qaug_ref[buf, 2 * half + AUG_ROWS:, :] = jnp.zeros(
                (MXU_DIM - 2 * half - AUG_ROWS, tq), BF16)

    def reset_state():
        m_ref[...] = jnp.full_like(m_ref, NEG_BIG)
        l_ref[...] = jnp.zeros_like(l_ref)
        acc_ref[...] = jnp.zeros_like(acc_ref)

    def col_max8(sT):
        return jnp.max(sT.reshape(tk // SUBLANES, SUBLANES, tq), axis=0)

    def qk_dots(qi, j):
        kt = kaug_ref[0, pl.ds(pl.multiple_of(j * tk, tk), tk), :]
        return [_dot(kt, qaug_ref[(qi % 2) * 2 + h]) for h in range(2)]

    def store_scores(scores):
        for h in range(2):
            s_ref[h] = scores[h]
            mx_ref[h] = col_max8(scores[h])

    def mask_stage(qi, j):
        kr = lax.broadcasted_iota(jnp.int32, (tk, tq), 0)
        qc = lax.broadcasted_iota(jnp.int32, (tk, tq), 1)
        keep = (kr - qc) <= (qi * tq - j * tk)
        for h in range(2):
            sT = jnp.where(keep, s_ref[h], NEG_BIG)
            s_ref[h] = sT
            mx_ref[h] = col_max8(sT)

    def pv_stage(qi, j):
        k_off = pl.multiple_of(j * tk, tk)
        i_blk = qi * tq // INPROJ_TM
        for h in range(2):
            big_c = (crep_ref[0, 0, pl.ds(h * n_blk + i_blk, 1), :]
                     - crep_ref[0, 0, pl.ds(h * n_blk + j, 1), :])
            m_prev = m_ref[h]
            m_new = jnp.maximum(m_prev, jnp.max(mx_ref[h], axis=0, keepdims=True) + big_c)
            pT = jnp.exp2(s_ref[h] - (m_new - big_c))
            alpha = jnp.exp2(m_prev - m_new)
            l_ref[h] = alpha * l_ref[h] + jnp.sum(pT.reshape(tk // SUBLANES, SUBLANES, tq), axis=0)
            vT = vT_ref[0, h * half:(h + 1) * half, pl.ds(k_off, tk)]
            acc_ref[h] = alpha * acc_ref[h] + _dot(vT, pT.astype(BF16))
            m_ref[h] = m_new

    def finalize(qi):
        outs = []
        for h in range(2):
            l = jnp.sum(l_ref[h], axis=0, keepdims=True)
            outs.append(acc_ref[h] / l)
        oT = jnp.concatenate(outs, axis=0)
        o_ref[0, pl.ds(pl.multiple_of(qi * tq, tq), tq), :] = oT.T.astype(o_ref.dtype)

    zero = jnp.int32(0)
    build_qaug(zero)
    build_qaug(zero + 1)
    reset_state()
    store_scores(qk_dots(zero, zero))
    mask_stage(zero, zero)

    def body(t, carry):
        qi, j = carry
        last = j == qi * tq // tk
        qi2 = jnp.where(last, qi + 1, qi)
        j2 = jnp.where(last, jstart_ref[g * nq + jnp.minimum(qi + 1, nq - 1)], j + 1)
        nxt = qk_dots(qi2, j2)
        pv_stage(qi, j)
        store_scores(nxt)

        @pl.when(j2 == qi2 * tq // tk)
        def _():
            mask_stage(qi2, j2)

        @pl.when(last)
        def _():
            finalize(qi)
            reset_state()

            @pl.when(qi + 2 < nq)
            def _():
                build_qaug(qi + 2)

        return qi2, j2

    qi, j = lax.fori_loop(0, nitems_ref[g] - 1, body, (jnp.int32(0), jnp.int32(0)))
    pv_stage(qi, j)
    finalize(qi)


def _prune_plan(stats, nq):
    B, n_blk = stats.shape[:2]
    c = stats[:, :, 0, :FOX_HEADS]
    qn = jnp.sqrt(stats[:, :, 1, :FOX_HEADS])
    kn = jnp.sqrt(jnp.max(stats[:, :, 2, :FOX_HEADS], axis=1))
    c_next = jnp.concatenate([c[:, 1:], c[:, -1:]], axis=1)
    bound = (c[:, :, None, :] - c_next[:, None, :, :]
             + 2.0 * NORM_SLACK * qn[:, :, None, :] * kn[:, None, None, :])
    i_idx = jnp.arange(n_blk)[:, None]
    j_idx = jnp.arange(n_blk)[None, :]
    skip = (bound <= PRUNE_LOG2) & (j_idx + 1 <= i_idx)[None, :, :, None]
    skip_pair = skip[..., 0::2] & skip[..., 1::2]
    jstart_blk = jnp.min(jnp.where(skip_pair, n_blk, j_idx[None, :, :, None]), axis=2)
    jstart_blk = jnp.transpose(jstart_blk, (0, 2, 1))
    per = FOX_TK // FOX_TQ
    jstart = jnp.repeat(jstart_blk, per, axis=2)
    last = (jnp.arange(nq) // per)[None, None, :]
    n_items = jnp.sum(last - jstart + 1, axis=2)
    return jstart.reshape(-1).astype(jnp.int32), n_items.reshape(-1).astype(jnp.int32)


def _fox_attention(fqT, kaug, fvT, drel, stats):
    B, _, S = fqT.shape
    tq, tk = FOX_TQ, FOX_TK
    n_blk = S // INPROJ_TM
    drel4 = drel.reshape(B, FOX_PAIRS, 2, S)
    c = stats[:, :, 0, :FOX_HEADS]
    c = jnp.transpose(c, (0, 2, 1)).reshape(B, FOX_PAIRS, 2 * n_blk, 1)
    crep = jnp.broadcast_to(c, (B, FOX_PAIRS, 2 * n_blk, tq))
    jstart, n_items = _prune_plan(stats, S // tq)
    grid_spec = pltpu.PrefetchScalarGridSpec(
        num_scalar_prefetch=2,
        grid=(B, FOX_PAIRS),
        in_specs=[
            pl.BlockSpec((1, LANES, S), lambda b, p, js, ni: (b, p, 0)),
            pl.BlockSpec((1, S, MXU_DIM), lambda b, p, js, ni: (b, 0, p)),
            pl.BlockSpec((1, LANES, S), lambda b, p, js, ni: (b, p, 0)),
            pl.BlockSpec((1, 1, 2, S), lambda b, p, js, ni: (b, p, 0, 0)),
            pl.BlockSpec((1, 1, 2 * n_blk, tq), lambda b, p, js, ni: (b, p, 0, 0)),
        ],
        out_specs=pl.BlockSpec((1, S, LANES), lambda b, p, js, ni: (b, 0, p)),
        scratch_shapes=[
            pltpu.VMEM((4, MXU_DIM, tq), BF16),
            pltpu.VMEM((2, tk, tq), F32),
            pltpu.VMEM((2, SUBLANES, tq), F32),
            pltpu.VMEM((2, FOX_HEAD_DIM, tq), F32),
            pltpu.VMEM((2, 1, tq), F32),
            pltpu.VMEM((2, SUBLANES, tq), F32),
        ])
    return pl.pallas_call(
        _fox_kernel,
        grid_spec=grid_spec,
        out_shape=jax.ShapeDtypeStruct((B, S, FOX_WIDTH), BF16),
        compiler_params=pltpu.CompilerParams(
            dimension_semantics=("arbitrary", "arbitrary"),
            vmem_limit_bytes=VMEM_LIMIT),
        name="fox_attn",
    )(jstart, n_items, fqT, kaug, fvT, drel4, crep)


def _gla_kernel(q_ref, k_ref, v_ref, r_ref, sm_ref, wa_ref, ba_ref, g_ref, o_ref, state_ref):
    ts = q_ref.shape[1]
    C, P = GLA_CHUNK, GLA_PAIR
    dk, dv, H = GLA_KEY_DIM, GLA_VAL_DIM, GLA_HEADS
    KW = GLA_KEY_WIDTH

    @pl.when(pl.program_id(1) == 0)
    def _():
        state_ref[...] = jnp.zeros_like(state_ref)

    la_all = _log_sigmoid(_dot(sm_ref[0].astype(BF16), wa_ref[...]) + ba_ref[...]) \
        * (1.0 / GLA_GATE_TEMP)

    ri = lax.broadcasted_iota(jnp.int32, (P, P), 0)
    ci = lax.broadcasted_iota(jnp.int32, (P, P), 1)
    tril2 = jnp.where(((ri < C) == (ci < C)) & (ci <= ri), 1.0, 0.0).astype(BF16)

    rs = lax.broadcasted_iota(jnp.int32, (2 * H * C, P), 0)
    cs = lax.broadcasted_iota(jnp.int32, (2 * H * C, P), 1)
    causal = ((rs >= H * C) == (cs >= C)) & ((cs & (C - 1)) <= (rs & (C - 1)))

    lane_q = lax.broadcasted_iota(jnp.int32, (C, KW), 1)
    lane_t = lax.broadcasted_iota(jnp.int32, (KW, P), 1)
    first = lane_t < C

    n_slabs = ts // P
    slab_rows = [slice(pi * P, (pi + 1) * P) for pi in range(n_slabs)]
    bs = [_tri_left(tril2, la_all[rows]) for rows in slab_rows]

    qms, scores, kv0s, kv1s, dec0s, dec1s, vss = [], [], [], [], [], [], []
    for pi in range(n_slabs):
        rows = slab_rows[pi]
        b = bs[pi]
        bT = b.T
        bl0 = bT[:, C - 1:C]
        bl1 = bT[:, P - 1:P]
        q_dec = q_ref[0, rows, :] * jnp.exp(b)
        kT = k_ref[0, rows, :].T
        k_invT = (kT * jnp.exp(-bT)).astype(BF16)
        k_teT = kT * jnp.exp(jnp.where(first, bl0, bl1) - bT)
        zt = jnp.zeros_like(k_teT)
        k_te0 = jnp.where(first, k_teT, zt).astype(BF16)
        k_te1 = jnp.where(first, zt, k_teT).astype(BF16)
        dec0 = jnp.exp(bl0)
        dec1 = jnp.exp(bl1)

        vs = [v_ref[0, rows, h * dv:(h + 1) * dv] for h in range(H)]
        kv = [_dot(jnp.concatenate([k_te0[h * dk:(h + 1) * dk, :], k_te1[h * dk:(h + 1) * dk, :]], axis=0),
                   vs[h]) for h in range(H)]
        kv0s.append(jnp.concatenate([kv[h][:dk] for h in range(H)], axis=0))
        kv1s.append(jnp.concatenate([kv[h][dk:] for h in range(H)], axis=0))
        dec0s.append(dec0)
        dec1s.append(dec1)
        vss.append(vs)

        zq = jnp.zeros((C, KW), F32)
        qm = [jnp.concatenate(
            [jnp.where((lane_q >= h * dk) & (lane_q < (h + 1) * dk), q_dec[c * C:(c + 1) * C], zq)
             for h in range(H)], axis=0).astype(BF16) for c in range(2)]
        a = _dot(jnp.concatenate(qm, axis=0), k_invT)
        scores.append(jnp.where(causal, a, 0.0).astype(BF16))
        qms.append(qm)

    s = state_ref[...]
    o_inters = []
    for pi in range(n_slabs):
        s1 = dec0s[pi] * s + kv0s[pi]
        o_inters.append([_dot(qms[pi][0], s.astype(BF16)), _dot(qms[pi][1], s1.astype(BF16))])
        s = dec1s[pi] * s1 + kv1s[pi]
    state_ref[...] = s

    for pi in range(n_slabs):
        rows = slab_rows[pi]
        a, o_inter, vs = scores[pi], o_inters[pi], vss[pi]
        for h in range(H):
            hv = slice(h * dv, (h + 1) * dv)
            a_h = jnp.concatenate([a[(c * H + h) * C:(c * H + h + 1) * C] for c in range(2)], axis=0)
            o = _dot(a_h, vs[h]) + jnp.concatenate(
                [o_inter[c][h * C:(h + 1) * C] for c in range(2)], axis=0)
            o = o * lax.rsqrt(jnp.mean(o * o, axis=-1, keepdims=True) + RMS_EPS)
            rr = r_ref[0, rows, hv]
            o = o * g_ref[:, hv] * (rr * (1.0 / (1.0 + jnp.exp(-rr))))
            o_ref[0, rows, hv] = o.astype(o_ref.dtype)


def _gla_attention(gq, gk, gv, gr, small, w_a2_pad, b_a, g_gla):
    B, S, _ = gq.shape
    ts = GLA_TS
    tok = lambda w: pl.BlockSpec((1, ts, w), lambda b, s: (b, s, 0))
    full = lambda r, c: pl.BlockSpec((r, c), lambda b, s: (0, 0))
    return pl.pallas_call(
        _gla_kernel,
        grid=(B, S // ts),
        in_specs=[
            tok(GLA_KEY_WIDTH), tok(GLA_KEY_WIDTH), tok(GLA_WIDTH), tok(GLA_WIDTH), tok(SMALL_W),
            full(SMALL_W, GLA_KEY_WIDTH), full(1, GLA_KEY_WIDTH), full(1, GLA_WIDTH),
        ],
        out_specs=tok(GLA_WIDTH),
        out_shape=jax.ShapeDtypeStruct((B, S, GLA_WIDTH), BF16),
        scratch_shapes=[pltpu.VMEM((GLA_KEY_WIDTH, GLA_VAL_DIM), F32)],
        compiler_params=pltpu.CompilerParams(
            dimension_semantics=("arbitrary", "arbitrary"), vmem_limit_bytes=VMEM_LIMIT),
        name="gla_attn",
    )(gq, gk, gv, gr, small, w_a2_pad, b_a, g_gla)


def _layer_norm(z, g, b):
    mu = jnp.mean(z, axis=-1, keepdims=True)
    d = z - mu
    var = jnp.mean(d * d, axis=-1, keepdims=True)
    return d * lax.rsqrt(var + LN_EPS) * g + b


def _tail_kernel(x_ref, fox_ref, gla_ref, mod_ref, wof_ref, wog_ref, ln1g_ref, ln1b_ref,
                 wg_ref, wu_ref, wd_ref, ln2g_ref, ln2b_ref, o_ref):
    m = mod_ref[0]
    tm = x_ref.shape[0]
    halves = [slice(0, tm // 2), slice(tm // 2, tm)]
    ys = [_dot(fox_ref[r, :], wof_ref[...]) + _dot(gla_ref[r, :], wog_ref[...]) for r in halves]
    x1s, gs, ups = [], [], []
    for r, y in zip(halves, ys):
        x1 = _layer_norm(DEEPNORM_ALPHA * x_ref[r, :] + (1.0 + m[2:3]) * y,
                         ln1g_ref[...], ln1b_ref[...])
        u2 = (x1 * (1.0 + m[4:5]) + m[3:4]).astype(BF16)
        x1s.append(x1)
        gs.append(_dot(u2, wg_ref[...]))
        ups.append(_dot(u2, wu_ref[...]))
    y2s = []
    for g, up in zip(gs, ups):
        h = (g * (1.0 / (1.0 + jnp.exp(-g))) * up).astype(BF16)
        y2s.append(_dot(h, wd_ref[...]))
    for r, x1, y2 in zip(halves, x1s, y2s):
        o_ref[r, :] = _layer_norm(DEEPNORM_ALPHA * x1 + (1.0 + m[5:6]) * y2,
                                  ln2g_ref[...], ln2b_ref[...])


def _tail(x2d, fox2d, gla2d, mod, wof, wog, ln1g, ln1b, wg, wu, wd, ln2g, ln2b, steps_per_batch):
    T, D = x2d.shape
    tm = TAIL_TM
    const = lambda r, c: pl.BlockSpec((r, c), lambda i: (0, 0), pipeline_mode=pl.Buffered(1))
    tok = lambda w: pl.BlockSpec((tm, w), lambda i: (i, 0))
    return pl.pallas_call(
        _tail_kernel,
        grid=(T // tm,),
        in_specs=[
            tok(D), tok(FOX_WIDTH), tok(GLA_WIDTH),
            pl.BlockSpec((1, N_MOD, D), lambda i: (i // steps_per_batch, 0, 0)),
            const(FOX_WIDTH, D), const(GLA_WIDTH, D), const(1, D), const(1, D),
            const(D, D_FF), const(D, D_FF), const(D_FF, D), const(1, D), const(1, D),
        ],
        out_specs=tok(D),
        out_shape=jax.ShapeDtypeStruct((T, D), F32),
        compiler_params=pltpu.CompilerParams(
            dimension_semantics=("arbitrary",), vmem_limit_bytes=VMEM_LIMIT),
        name="tail",
    )(x2d, fox2d, gla2d, mod, wof, wog, ln1g, ln1b, wg, wu, wd, ln2g, ln2b)


def kernel(x, c, w_c, b_c, w_in, b_f, w_a2, b_a, g_gla, w_o, ln1_g, ln1_b,
           w_gate, w_up, w_down, ln2_g, ln2_b):
    B, S, D = x.shape
    assert D == D_MODEL and S % INPROJ_TM == 0 and INPROJ_TM == FOX_TK and FOX_TK % FOX_TQ == 0

    offs = [0]
    for wdt in IN_SPLITS:
        offs.append(offs[-1] + wdt)
    seg = lambda i: w_in[:, offs[i]:offs[i + 1]]
    w_small = jnp.concatenate(
        [seg(3), seg(7), jnp.zeros((D, SMALL_W - FOX_HEADS - GLA_GATE_RANK), F32)], axis=1)
    w_all = jnp.concatenate(
        [seg(0), seg(1), seg(2), seg(4), w_small, jnp.zeros((D, SMALL_W), F32),
         seg(5), seg(6), seg(8)], axis=1).astype(BF16)
    b_f_pad = jnp.zeros((1, SMALL_W), F32).at[0, :FOX_HEADS].set(b_f)
    w_a2_pad = jnp.zeros((SMALL_W, GLA_KEY_WIDTH), F32) \
        .at[FOX_HEADS:FOX_HEADS + GLA_GATE_RANK].set(w_a2).astype(BF16)

    mod = _modulation(c, w_c, b_c)

    fqT, kaug, fvT, gq, gk, gv, gr, small, drel, stats = _in_projection(x, mod, w_all, b_f_pad)
    fox = _fox_attention(fqT, kaug, fvT, drel, stats)
    gla = _gla_attention(gq, gk, gv, gr, small, w_a2_pad,
                         b_a.reshape(1, -1), g_gla.reshape(1, -1))

    T = B * S
    w_o_b = w_o.astype(BF16)
    out = _tail(x.reshape(T, D), fox.reshape(T, FOX_WIDTH), gla.reshape(T, GLA_WIDTH), mod,
                w_o_b[:FOX_WIDTH], w_o_b[FOX_WIDTH:],
                ln1_g.reshape(1, D), ln1_b.reshape(1, D),
                w_gate.astype(BF16), w_up.astype(BF16), w_down.astype(BF16),
                ln2_g.reshape(1, D), ln2_b.reshape(1, D), S // TAIL_TM)
    return out.reshape(B, S, D)
```

```python
import math

import numpy as np
import jax
import jax.numpy as jnp
from jax import lax
from jax.experimental import pallas as pl
from jax.experimental.pallas import tpu as pltpu

F32 = jnp.float32
BF16 = jnp.bfloat16

D_MODEL = 1024
FOX_WIDTH = 512
FOX_HEAD_DIM = 64
FOX_HEADS = 8
FOX_PAIRS = FOX_HEADS // 2
GLA_WIDTH = 512
GLA_HEADS = 4
GLA_VAL_DIM = 128
GLA_KEY_DIM = 64
GLA_KEY_WIDTH = 256
GLA_GATE_RANK = 16
GLA_GATE_TEMP = 16.0
GLA_CHUNK = 64
D_FF = 2816
LN_EPS = 1e-5
RMS_EPS = 1e-6
N_MOD = 6
DEEPNORM_ALPHA = 2.0 ** 0.25
IN_SPLITS = (FOX_WIDTH, FOX_WIDTH, FOX_WIDTH, FOX_HEADS, GLA_KEY_WIDTH, GLA_KEY_WIDTH,
             GLA_WIDTH, GLA_GATE_RANK, GLA_WIDTH)
LOG2E = math.log2(math.e)

LANES = 128
SUBLANES = 8
MXU_DIM = 256
SMALL_W = LANES
VMEM_LIMIT = 56 * 1024 * 1024

MOD_TN = 1536
INPROJ_TM = 512
FOX_TQ = 256
FOX_TK = 512
GLA_TS = 512
GLA_PAIR = 2 * GLA_CHUNK
TAIL_TM = 512

NEG_BIG = -1e30
PRUNE_LOG2 = -1080.0
NORM_SLACK = 1.02

AUG_ROWS = 32


def _log_sigmoid(z):
    return jnp.minimum(z, 0.0) - jnp.log(1.0 + jnp.exp(-jnp.abs(z)))


def _split3(a):
    hi = a.astype(BF16)
    r1 = a - hi.astype(F32)
    mid = r1.astype(BF16)
    lo = (r1 - mid.astype(F32)).astype(BF16)
    return hi, mid, lo


def _dot(a, b):
    return jnp.dot(a, b, preferred_element_type=F32)


def _tri_left(tri, a):
    hi, mid, lo = _split3(a)
    return _dot(tri, hi) + _dot(tri, mid) + _dot(tri, lo)


def _tri_right(a, tri):
    hi, mid, lo = _split3(a)
    return _dot(hi, tri) + _dot(mid, tri) + _dot(lo, tri)


def _mod_kernel(c_ref, w_ref, b_ref, o_ref):
    o_ref[...] = jnp.dot(c_ref[...], w_ref[...], preferred_element_type=F32,
                         precision=lax.Precision.HIGHEST) + b_ref[...]


def _modulation(c, w_c, b_c):
    B, D = c.shape
    N = w_c.shape[1]
    rows = SUBLANES
    c_pad = jnp.zeros((rows, D), F32).at[:B].set(c)
    out = pl.pallas_call(
        _mod_kernel,
        grid=(N // MOD_TN,),
        in_specs=[
            pl.BlockSpec((rows, D), lambda j: (0, 0)),
            pl.BlockSpec((D, MOD_TN), lambda j: (0, j)),
            pl.BlockSpec((1, MOD_TN), lambda j: (0, j)),
        ],
        out_specs=pl.BlockSpec((rows, MOD_TN), lambda j: (0, j)),
        out_shape=jax.ShapeDtypeStruct((rows, N), F32),
        compiler_params=pltpu.CompilerParams(
            dimension_semantics=("arbitrary",), vmem_limit_bytes=VMEM_LIMIT),
        name="adaln_mod",
    )(c_pad, w_c, b_c.reshape(1, N))
    return out[:B].reshape(B, N_MOD, D)


_C_FQ, _C_FK, _C_FV = 0, 512, 1024
_C_GQ, _C_SM, _C_GK, _C_GV, _C_GR = 1536, 1792, 2048, 2304, 2816
_W_COLS = _C_GR + GLA_WIDTH
_KAUG_W = FOX_PAIRS * MXU_DIM


def _head_indicator():
    ind = np.zeros((FOX_WIDTH, MXU_DIM), np.float32)
    for h in range(FOX_HEADS):
        ind[h * FOX_HEAD_DIM:(h + 1) * FOX_HEAD_DIM, h] = 1.0
    return ind


def _inproj_kernel(x_ref, mod_ref, w_ref, bf_ref, ind_ref,
                   fqT_ref, kaug_ref, fvT_ref, gq_ref, gk_ref, gv_ref, gr_ref, sm_ref,
                   drel_ref, stats_ref, carry_ref):
    tm = x_ref.shape[1]

    @pl.when(pl.program_id(1) == 0)
    def _():
        carry_ref[...] = jnp.zeros_like(carry_ref)

    m = mod_ref[0]
    u = (x_ref[0] * (1.0 + m[1:2]) + m[0:1]).astype(BF16)

    def proj(lo, width):
        return _dot(u, w_ref[:, lo:lo + width])

    gq_small = proj(_C_GQ, 2 * GLA_KEY_WIDTH)
    gq_ref[0] = gq_small[:, :GLA_KEY_WIDTH] * (GLA_KEY_DIM ** -0.5)
    small = gq_small[:, GLA_KEY_WIDTH:GLA_KEY_WIDTH + SMALL_W]
    sm_ref[0] = small

    fq = proj(_C_FQ, FOX_WIDTH) * (FOX_HEAD_DIM ** -0.5 * LOG2E)
    fqT_ref[0] = fq.T.astype(BF16)
    fk32 = proj(_C_FK, FOX_WIDTH)
    fk = fk32.astype(BF16)
    qn2 = jnp.max(_dot((fq * fq).astype(BF16), ind_ref[...])[:, :SMALL_W], axis=0, keepdims=True)
    kn2 = jnp.max(_dot((fk32 * fk32).astype(BF16), ind_ref[...])[:, :SMALL_W], axis=0, keepdims=True)

    zT = (small + bf_ref[...]).T[:FOX_HEADS, :]
    lfT = _log_sigmoid(zT) * LOG2E
    r = lax.broadcasted_iota(jnp.int32, (tm, tm), 0)
    c = lax.broadcasted_iota(jnp.int32, (tm, tm), 1)
    triu = jnp.where(r <= c, 1.0, 0.0).astype(BF16)
    drelT = _tri_right(lfT, triu)
    drel_ref[0] = drelT

    c_before = carry_ref[...]
    srow = lax.broadcasted_iota(jnp.int32, (SUBLANES, SMALL_W), 0)
    slane = lax.broadcasted_iota(jnp.int32, (SUBLANES, SMALL_W), 1)
    total = jnp.sum(jnp.where(srow == slane, drelT[:, tm - 1:tm], 0.0), axis=0, keepdims=True)
    carry_ref[...] = c_before + total

    hi, mid, lo = _split3(drelT)
    ones3 = jnp.where(lax.broadcasted_iota(jnp.int32, (SUBLANES, tm), 0) < 3, 1.0, 0.0)
    slabT = jnp.concatenate(
        [-hi.astype(F32), -mid.astype(F32), -lo.astype(F32), ones3,
         jnp.zeros((LANES - 4 * SUBLANES, tm), F32)], axis=0)
    slab = slabT.T.astype(BF16)

    fvT_ref[0] = proj(_C_FV, FOX_WIDTH).T.astype(BF16)
    gv_ref[0] = proj(_C_GV, GLA_WIDTH).astype(BF16)
    gr_ref[0] = proj(_C_GR, GLA_WIDTH)
    gk_ref[0] = proj(_C_GK, GLA_KEY_WIDTH)

    stats_ref[0, 0] = jnp.where(srow == 0, c_before,
                                jnp.where(srow == 1, qn2, jnp.where(srow == 2, kn2, 0.0)))
    for p in range(FOX_PAIRS):
        kaug_ref[0, :, p * MXU_DIM:p * MXU_DIM + LANES] = fk[:, p * LANES:(p + 1) * LANES]
        kaug_ref[0, :, p * MXU_DIM + LANES:(p + 1) * MXU_DIM] = slab


def _in_projection(x, mod, w_all, b_f_pad):
    B, S, D = x.shape
    tm = INPROJ_TM
    ind = _head_indicator()
    tok = lambda w: pl.BlockSpec((1, tm, w), lambda b, s: (b, s, 0))
    tokT = lambda h: pl.BlockSpec((1, h, tm), lambda b, s: (b, 0, s))
    outs = pl.pallas_call(
        _inproj_kernel,
        grid=(B, S // tm),
        in_specs=[
            tok(D),
            pl.BlockSpec((1, N_MOD, D), lambda b, s: (b, 0, 0)),
            pl.BlockSpec((D, _W_COLS), lambda b, s: (0, 0)),
            pl.BlockSpec((1, SMALL_W), lambda b, s: (0, 0)),
            pl.BlockSpec(ind.shape, lambda b, s: (0, 0)),
        ],
        out_specs=[
            tokT(FOX_WIDTH), tok(_KAUG_W), tokT(FOX_WIDTH),
            tok(GLA_KEY_WIDTH), tok(GLA_KEY_WIDTH), tok(GLA_WIDTH), tok(GLA_WIDTH),
            tok(SMALL_W),
            tokT(FOX_HEADS),
            pl.BlockSpec((1, 1, SUBLANES, SMALL_W), lambda b, s: (b, s, 0, 0)),
        ],
        out_shape=[
            jax.ShapeDtypeStruct((B, FOX_WIDTH, S), BF16),
            jax.ShapeDtypeStruct((B, S, _KAUG_W), BF16),
            jax.ShapeDtypeStruct((B, FOX_WIDTH, S), BF16),
            jax.ShapeDtypeStruct((B, S, GLA_KEY_WIDTH), F32),
            jax.ShapeDtypeStruct((B, S, GLA_KEY_WIDTH), F32),
            jax.ShapeDtypeStruct((B, S, GLA_WIDTH), BF16),
            jax.ShapeDtypeStruct((B, S, GLA_WIDTH), F32),
            jax.ShapeDtypeStruct((B, S, SMALL_W), F32),
            jax.ShapeDtypeStruct((B, FOX_HEADS, S), F32),
            jax.ShapeDtypeStruct((B, S // tm, SUBLANES, SMALL_W), F32),
        ],
        scratch_shapes=[pltpu.VMEM((1, SMALL_W), F32)],
        compiler_params=pltpu.CompilerParams(
            dimension_semantics=("arbitrary", "arbitrary"), vmem_limit_bytes=VMEM_LIMIT),
        name="in_proj",
    )(x, mod, w_all, b_f_pad, jnp.asarray(ind, BF16))
    return outs


def _fox_kernel(jstart_ref, nitems_ref, qT_ref, kaug_ref, vT_ref, drel_ref, crep_ref, o_ref,
                qaug_ref, s_ref, mx_ref, acc_ref, m_ref, l_ref):
    tq, tk = FOX_TQ, FOX_TK
    half = FOX_HEAD_DIM
    S = kaug_ref.shape[1]
    nq = S // tq
    n_blk = crep_ref.shape[2] // 2
    g = pl.program_id(0) * pl.num_programs(1) + pl.program_id(1)

    def build_qaug(qi):
        q0 = pl.multiple_of(qi * tq, tq)
        qT = qT_ref[0, :, pl.ds(q0, tq)]
        dr = drel_ref[0, 0, :, pl.ds(q0, tq)]
        row = lax.broadcasted_iota(jnp.int32, (2 * half, tq), 0)
        r32 = lax.broadcasted_iota(jnp.int32, (AUG_ROWS, tq), 0)
        zq = jnp.zeros_like(qT)
        for h in range(2):
            buf = (qi % 2) * 2 + h
            own = (row < half) if h == 0 else (row >= half)
            qaug_ref[buf, 0:2 * half, :] = jnp.where(own, qT, zq)
            hi, mid, lo = _split3(dr[h:h + 1])
            gh = 2 * pl.program_id(1) + h
            pick = (r32 == gh) | (r32 == gh + SUBLANES) | (r32 == gh + 2 * SUBLANES)
            slab = jnp.where(r32 == 24, hi.astype(F32),
                             jnp.where(r32 == 25, mid.astype(F32),
                                       jnp.where(r32 == 26, lo.astype(F32),
                                                 jnp.where(pick, 1.0, 0.0))))
            qaug_ref[buf, 2 * half:2 * half + AUG_ROWS, :] = slab.astype(BF16)
            qaug_ref[buf, 2 * half + AUG_ROWS:, :] = jnp.zeros(
                (MXU_DIM - 2 * half - AUG_ROWS, tq), BF16)

    def reset_state():
        m_ref[...] = jnp.full_like(m_ref, NEG_BIG)
        l_ref[...] = jnp.zeros_like(l_ref)
        acc_ref[...] = jnp.zeros_like(acc_ref)

    per = tk // tq

    def col_max8(s):
        return jnp.max(s.reshape(s.shape[0] // SUBLANES, SUBLANES, tq), axis=0)

    def qk_dots(qi, j):
        kt = kaug_ref[0, pl.ds(pl.multiple_of(j * tk, tk), tk), :]
        return [_dot(kt, qaug_ref[(qi % 2) * 2 + h]) for h in range(2)]

    def store_scores(scores):
        for h in range(2):
            s_ref[h] = scores[h]
            for g in range(per):
                mx_ref[per * h + g] = col_max8(scores[h][g * tq:(g + 1) * tq])

    def mask_stage(qi):
        d = qi % per
        d0 = pl.multiple_of(d * tq, tq)
        kr = lax.broadcasted_iota(jnp.int32, (tq, tq), 0)
        qc = lax.broadcasted_iota(jnp.int32, (tq, tq), 1)
        for h in range(2):
            tri = jnp.where(kr <= qc, s_ref[h, pl.ds(d0, tq), :], NEG_BIG)
            s_ref[h, pl.ds(d0, tq), :] = tri
            mx_ref[per * h + d] = col_max8(tri)
        for g in range(1, per):
            @pl.when(d < g)
            def _():
                for h in range(2):
                    s_ref[h, g * tq:(g + 1) * tq, :] = jnp.full((tq, tq), NEG_BIG, F32)
                    mx_ref[per * h + g] = jnp.full((SUBLANES, tq), NEG_BIG, F32)

    def pv_stage(qi, j):
        k_off = pl.multiple_of(j * tk, tk)
        i_blk = qi * tq // INPROJ_TM
        for h in range(2):
            big_c = (crep_ref[0, 0, pl.ds(h * n_blk + i_blk, 1), :]
                     - crep_ref[0, 0, pl.ds(h * n_blk + j, 1), :])
            m_prev = m_ref[h]
            mx = mx_ref[per * h]
            for g in range(1, per):
                mx = jnp.maximum(mx, mx_ref[per * h + g])
            m_new = jnp.maximum(m_prev, jnp.max(mx, axis=0, keepdims=True) + big_c)
            pT = jnp.exp2(s_ref[h] - (m_new - big_c))
            alpha = jnp.exp2(m_prev - m_new)
            l_ref[h] = alpha * l_ref[h] + jnp.sum(pT.reshape(tk // SUBLANES, SUBLANES, tq), axis=0)
            vT = vT_ref[0, h * half:(h + 1) * half, pl.ds(k_off, tk)]
            acc_ref[h] = alpha * acc_ref[h] + _dot(vT, pT.astype(BF16))
            m_ref[h] = m_new

    def finalize(qi):
        outs = []
        for h in range(2):
            l = jnp.sum(l_ref[h], axis=0, keepdims=True)
            outs.append(acc_ref[h] / l)
        oT = jnp.concatenate(outs, axis=0)
        o_ref[0, pl.ds(pl.multiple_of(qi * tq, tq), tq), :] = oT.T.astype(o_ref.dtype)

    zero = jnp.int32(0)
    build_qaug(zero)
    build_qaug(zero + 1)
    reset_state()
    store_scores(qk_dots(zero, zero))
    mask_stage(zero)

    def body(t, carry):
        qi, j = carry
        last = j == qi * tq // tk
        qi2 = jnp.where(last, qi + 1, qi)
        j2 = jnp.where(last, jstart_ref[g * nq + jnp.minimum(qi + 1, nq - 1)], j + 1)
        nxt = qk_dots(qi2, j2)
        pv_stage(qi, j)
        store_scores(nxt)

        @pl.when(j2 == qi2 * tq // tk)
        def _():
            mask_stage(qi2)

        @pl.when(last)
        def _():
            finalize(qi)
            reset_state()

            @pl.when(qi + 2 < nq)
            def _():
                build_qaug(qi + 2)

        return qi2, j2

    qi, j = lax.fori_loop(0, nitems_ref[g] - 1, body, (jnp.int32(0), jnp.int32(0)))
    pv_stage(qi, j)
    finalize(qi)


def _prune_plan(stats, drel, nq):
    B, n_blk = stats.shape[:2]
    per = FOX_TK // FOX_TQ
    c = stats[:, :, 0, :FOX_HEADS]
    qn = jnp.sqrt(stats[:, :, 1, :FOX_HEADS])
    kn = jnp.sqrt(stats[:, :, 2, :FOX_HEADS])
    c_next = jnp.concatenate([c[:, 1:], c[:, -1:]], axis=1)
    d_in = jnp.transpose(drel[:, :, FOX_TQ - 1::FOX_TQ], (0, 2, 1))
    d_in = jnp.concatenate([jnp.zeros_like(d_in[:, :1]), d_in[:, :-1]], axis=1)
    first_in_tile = (jnp.arange(nq) % per == 0)[None, :, None]
    d_before = jnp.repeat(c, per, axis=1) + jnp.where(first_in_tile, 0.0, d_in)
    qn_q = jnp.repeat(qn, per, axis=1)
    kn_q = jnp.repeat(kn, per, axis=1)
    bound = (d_before[:, :, None, :] - c_next[:, None, :, :]
             + NORM_SLACK * qn_q[:, :, None, :] * (kn[:, None, :, :] + kn_q[:, :, None, :]))
    i_idx = (jnp.arange(nq) // per)[:, None]
    j_idx = jnp.arange(n_blk)[None, :]
    skip = (bound <= PRUNE_LOG2) & (j_idx + 1 <= i_idx)[None, :, :, None]
    skip_pair = skip[..., 0::2] & skip[..., 1::2]
    jstart = jnp.min(jnp.where(skip_pair, n_blk, j_idx[None, :, :, None]), axis=2)
    jstart = jnp.transpose(jstart, (0, 2, 1))
    n_items = jnp.sum(i_idx[:, 0][None, None, :] - jstart + 1, axis=2)
    return jstart.reshape(-1).astype(jnp.int32), n_items.reshape(-1).astype(jnp.int32)


def _fox_attention(fqT, kaug, fvT, drel, stats):
    B, _, S = fqT.shape
    tq, tk = FOX_TQ, FOX_TK
    n_blk = S // INPROJ_TM
    drel4 = drel.reshape(B, FOX_PAIRS, 2, S)
    c = stats[:, :, 0, :FOX_HEADS]
    c = jnp.transpose(c, (0, 2, 1)).reshape(B, FOX_PAIRS, 2 * n_blk, 1)
    crep = jnp.broadcast_to(c, (B, FOX_PAIRS, 2 * n_blk, tq))
    jstart, n_items = _prune_plan(stats, drel, S // tq)
    grid_spec = pltpu.PrefetchScalarGridSpec(
        num_scalar_prefetch=2,
        grid=(B, FOX_PAIRS),
        in_specs=[
            pl.BlockSpec((1, LANES, S), lambda b, p, js, ni: (b, p, 0)),
            pl.BlockSpec((1, S, MXU_DIM), lambda b, p, js, ni: (b, 0, p)),
            pl.BlockSpec((1, LANES, S), lambda b, p, js, ni: (b, p, 0)),
            pl.BlockSpec((1, 1, 2, S), lambda b, p, js, ni: (b, p, 0, 0)),
            pl.BlockSpec((1, 1, 2 * n_blk, tq), lambda b, p, js, ni: (b, p, 0, 0)),
        ],
        out_specs=pl.BlockSpec((1, S, LANES), lambda b, p, js, ni: (b, 0, p)),
        scratch_shapes=[
            pltpu.VMEM((4, MXU_DIM, tq), BF16),
            pltpu.VMEM((2, tk, tq), F32),
            pltpu.VMEM((2 * (tk // tq), SUBLANES, tq), F32),
            pltpu.VMEM((2, FOX_HEAD_DIM, tq), F32),
            pltpu.VMEM((2, 1, tq), F32),
            pltpu.VMEM((2, SUBLANES, tq), F32),
        ])
    return pl.pallas_call(
        _fox_kernel,
        grid_spec=grid_spec,
        out_shape=jax.ShapeDtypeStruct((B, S, FOX_WIDTH), BF16),
        compiler_params=pltpu.CompilerParams(
            dimension_semantics=("arbitrary", "arbitrary"),
            vmem_limit_bytes=VMEM_LIMIT),
        name="fox_attn",
    )(jstart, n_items, fqT, kaug, fvT, drel4, crep)


def _gla_kernel(q_ref, k_ref, v_ref, r_ref, sm_ref, wa_ref, ba_ref, g_ref, o_ref, state_ref):
    ts = q_ref.shape[1]
    C, P = GLA_CHUNK, GLA_PAIR
    dk, dv, H = GLA_KEY_DIM, GLA_VAL_DIM, GLA_HEADS
    KW = GLA_KEY_WIDTH

    @pl.when(pl.program_id(1) == 0)
    def _():
        state_ref[...] = jnp.zeros_like(state_ref)

    la_all = _log_sigmoid(_dot(sm_ref[0].astype(BF16), wa_ref[...]) + ba_ref[...]) \
        * (1.0 / GLA_GATE_TEMP)

    ri = lax.broadcasted_iota(jnp.int32, (P, P), 0)
    ci = lax.broadcasted_iota(jnp.int32, (P, P), 1)
    tril2 = jnp.where(((ri < C) == (ci < C)) & (ci <= ri), 1.0, 0.0).astype(BF16)

    rs = lax.broadcasted_iota(jnp.int32, (2 * H * C, P), 0)
    cs = lax.broadcasted_iota(jnp.int32, (2 * H * C, P), 1)
    causal = ((rs >= H * C) == (cs >= C)) & ((cs & (C - 1)) <= (rs & (C - 1)))

    lane_q = lax.broadcasted_iota(jnp.int32, (C, KW), 1)
    lane_t = lax.broadcasted_iota(jnp.int32, (KW, P), 1)
    first = lane_t < C

    n_slabs = ts // P
    slab_rows = [slice(pi * P, (pi + 1) * P) for pi in range(n_slabs)]
    bs = [_tri_left(tril2, la_all[rows]) for rows in slab_rows]

    qms, scores, kv0s, kv1s, dec0s, dec1s, vss = [], [], [], [], [], [], []
    for pi in range(n_slabs):
        rows = slab_rows[pi]
        b = bs[pi]
        bT = b.T
        bl0 = bT[:, C - 1:C]
        bl1 = bT[:, P - 1:P]
        q_dec = q_ref[0, rows, :] * jnp.exp(b)
        kT = k_ref[0, rows, :].T
        k_invT = (kT * jnp.exp(-bT)).astype(BF16)
        k_teT = kT * jnp.exp(jnp.where(first, bl0, bl1) - bT)
        zt = jnp.zeros_like(k_teT)
        k_te0 = jnp.where(first, k_teT, zt).astype(BF16)
        k_te1 = jnp.where(first, zt, k_teT).astype(BF16)
        dec0 = jnp.exp(bl0)
        dec1 = jnp.exp(bl1)

        vs = [v_ref[0, rows, h * dv:(h + 1) * dv] for h in range(H)]
        kv = [_dot(jnp.concatenate([k_te0[h * dk:(h + 1) * dk, :], k_te1[h * dk:(h + 1) * dk, :]], axis=0),
                   vs[h]) for h in range(H)]
        kv0s.append(jnp.concatenate([kv[h][:dk] for h in range(H)], axis=0))
        kv1s.append(jnp.concatenate([kv[h][dk:] for h in range(H)], axis=0))
        dec0s.append(dec0)
        dec1s.append(dec1)
        vss.append(vs)

        zq = jnp.zeros((C, KW), F32)
        qm = [jnp.concatenate(
            [jnp.where((lane_q >= h * dk) & (lane_q < (h + 1) * dk), q_dec[c * C:(c + 1) * C], zq)
             for h in range(H)], axis=0).astype(BF16) for c in range(2)]
        a = _dot(jnp.concatenate(qm, axis=0), k_invT)
        scores.append(jnp.where(causal, a, 0.0).astype(BF16))
        qms.append(qm)

    s = state_ref[...]
    o_inters = []
    for pi in range(n_slabs):
        s1 = dec0s[pi] * s + kv0s[pi]
        o_inters.append([_dot(qms[pi][0], s.astype(BF16)), _dot(qms[pi][1], s1.astype(BF16))])
        s = dec1s[pi] * s1 + kv1s[pi]
    state_ref[...] = s

    for pi in range(n_slabs):
        rows = slab_rows[pi]
        a, o_inter, vs = scores[pi], o_inters[pi], vss[pi]
        for h in range(H):
            hv = slice(h * dv, (h + 1) * dv)
            a_h = jnp.concatenate([a[(c * H + h) * C:(c * H + h + 1) * C] for c in range(2)], axis=0)
            o = _dot(a_h, vs[h]) + jnp.concatenate(
                [o_inter[c][h * C:(h + 1) * C] for c in range(2)], axis=0)
            o = o * lax.rsqrt(jnp.mean(o * o, axis=-1, keepdims=True) + RMS_EPS)
            rr = r_ref[0, rows, hv]
            o = o * g_ref[:, hv] * (rr * (1.0 / (1.0 + jnp.exp(-rr))))
            o_ref[0, rows, hv] = o.astype(o_ref.dtype)


def _gla_attention(gq, gk, gv, gr, small, w_a2_pad, b_a, g_gla):
    B, S, _ = gq.shape
    ts = GLA_TS
    tok = lambda w: pl.BlockSpec((1, ts, w), lambda b, s: (b, s, 0))
    full = lambda r, c: pl.BlockSpec((r, c), lambda b, s: (0, 0))
    return pl.pallas_call(
        _gla_kernel,
        grid=(B, S // ts),
        in_specs=[
            tok(GLA_KEY_WIDTH), tok(GLA_KEY_WIDTH), tok(GLA_WIDTH), tok(GLA_WIDTH), tok(SMALL_W),
            full(SMALL_W, GLA_KEY_WIDTH), full(1, GLA_KEY_WIDTH), full(1, GLA_WIDTH),
        ],
        out_specs=tok(GLA_WIDTH),
        out_shape=jax.ShapeDtypeStruct((B, S, GLA_WIDTH), BF16),
        scratch_shapes=[pltpu.VMEM((GLA_KEY_WIDTH, GLA_VAL_DIM), F32)],
        compiler_params=pltpu.CompilerParams(
            dimension_semantics=("arbitrary", "arbitrary"), vmem_limit_bytes=VMEM_LIMIT),
        name="gla_attn",
    )(gq, gk, gv, gr, small, w_a2_pad, b_a, g_gla)


def _layer_norm(z, g, b):
    mu = jnp.mean(z, axis=-1, keepdims=True)
    d = z - mu
    var = jnp.mean(d * d, axis=-1, keepdims=True)
    return d * lax.rsqrt(var + LN_EPS) * g + b


def _tail_kernel(x_ref, fox_ref, gla_ref, mod_ref, wof_ref, wog_ref, ln1g_ref, ln1b_ref,
                 wg_ref, wu_ref, wd_ref, ln2g_ref, ln2b_ref, o_ref):
    m = mod_ref[0]
    tm = x_ref.shape[0]
    halves = [slice(0, tm // 2), slice(tm // 2, tm)]
    ys = [_dot(fox_ref[r, :], wof_ref[...]) + _dot(gla_ref[r, :], wog_ref[...]) for r in halves]
    x1s, gs, ups = [], [], []
    for r, y in zip(halves, ys):
        x1 = _layer_norm(DEEPNORM_ALPHA * x_ref[r, :] + (1.0 + m[2:3]) * y,
                         ln1g_ref[...], ln1b_ref[...])
        u2 = (x1 * (1.0 + m[4:5]) + m[3:4]).astype(BF16)
        x1s.append(x1)
        gs.append(_dot(u2, wg_ref[...]))
        ups.append(_dot(u2, wu_ref[...]))
    y2s = []
    for g, up in zip(gs, ups):
        h = (g * (1.0 / (1.0 + jnp.exp(-g))) * up).astype(BF16)
        y2s.append(_dot(h, wd_ref[...]))
    for r, x1, y2 in zip(halves, x1s, y2s):
        o_ref[r, :] = _layer_norm(DEEPNORM_ALPHA * x1 + (1.0 + m[5:6]) * y2,
                                  ln2g_ref[...], ln2b_ref[...])


def _tail(x2d, fox2d, gla2d, mod, wo, ln1g, ln1b, wg, wu, wd, ln2g, ln2b, steps_per_batch):
    T, D = x2d.shape
    tm = TAIL_TM
    const = lambda r, c: pl.BlockSpec((r, c), lambda i: (0, 0), pipeline_mode=pl.Buffered(1))
    tok = lambda w: pl.BlockSpec((tm, w), lambda i: (i, 0))
    return pl.pallas_call(
        _tail_kernel,
        grid=(T // tm,),
        in_specs=[
            tok(D), tok(FOX_WIDTH), tok(GLA_WIDTH),
            pl.BlockSpec((1, N_MOD, D), lambda i: (i // steps_per_batch, 0, 0)),
            pl.BlockSpec((FOX_WIDTH, D), lambda i: (0, 0), pipeline_mode=pl.Buffered(1)),
            pl.BlockSpec((GLA_WIDTH, D), lambda i: (1, 0), pipeline_mode=pl.Buffered(1)),
            const(1, D), const(1, D),
            const(D, D_FF), const(D, D_FF), const(D_FF, D), const(1, D), const(1, D),
        ],
        out_specs=tok(D),
        out_shape=jax.ShapeDtypeStruct((T, D), F32),
        compiler_params=pltpu.CompilerParams(
            dimension_semantics=("arbitrary",), vmem_limit_bytes=VMEM_LIMIT),
        name="tail",
    )(x2d, fox2d, gla2d, mod, wo, wo, ln1g, ln1b, wg, wu, wd, ln2g, ln2b)


def kernel(x, c, w_c, b_c, w_in, b_f, w_a2, b_a, g_gla, w_o, ln1_g, ln1_b,
           w_gate, w_up, w_down, ln2_g, ln2_b):
    B, S, D = x.shape
    assert D == D_MODEL and S % INPROJ_TM == 0 and INPROJ_TM == FOX_TK and FOX_TK % FOX_TQ == 0

    offs = [0]
    for wdt in IN_SPLITS:
        offs.append(offs[-1] + wdt)
    seg = lambda i: w_in[:, offs[i]:offs[i + 1]]
    w_small = jnp.concatenate(
        [seg(3), seg(7), jnp.zeros((D, SMALL_W - FOX_HEADS - GLA_GATE_RANK), F32)], axis=1)
    w_all = jnp.concatenate(
        [seg(0), seg(1), seg(2), seg(4), w_small, jnp.zeros((D, SMALL_W), F32),
         seg(5), seg(6), seg(8)], axis=1).astype(BF16)
    b_f_pad = jnp.zeros((1, SMALL_W), F32).at[0, :FOX_HEADS].set(b_f)
    w_a2_pad = jnp.zeros((SMALL_W, GLA_KEY_WIDTH), F32) \
        .at[FOX_HEADS:FOX_HEADS + GLA_GATE_RANK].set(w_a2).astype(BF16)

    mod = _modulation(c, w_c, b_c)

    fqT, kaug, fvT, gq, gk, gv, gr, small, drel, stats = _in_projection(x, mod, w_all, b_f_pad)
    fox = _fox_attention(fqT, kaug, fvT, drel, stats)
    gla = _gla_attention(gq, gk, gv, gr, small, w_a2_pad,
                         b_a.reshape(1, -1), g_gla.reshape(1, -1))

    T = B * S
    w_o_b = w_o.astype(BF16)
    out = _tail(x.reshape(T, D), fox.reshape(T, FOX_WIDTH), gla.reshape(T, GLA_WIDTH), mod,
                w_o_b,
                ln1_g.reshape(1, D), ln1_b.reshape(1, D),
                w_gate.astype(BF16), w_up.astype(BF16), w_down.astype(BF16),
                ln2_g.reshape(1, D), ln2_b.reshape(1, D), S // TAIL_TM)
    return out.reshape(B, S, D)
```

```python
import math

import numpy as np
import jax
import jax.numpy as jnp
from jax import lax
from jax.experimental import pallas as pl
from jax.experimental.pallas import tpu as pltpu

F32 = jnp.float32
BF16 = jnp.bfloat16

D_MODEL = 1024
FOX_WIDTH = 512
FOX_HEAD_DIM = 64
FOX_HEADS = 8
FOX_PAIRS = FOX_HEADS // 2
GLA_WIDTH = 512
GLA_HEADS = 4
GLA_VAL_DIM = 128
GLA_KEY_DIM = 64
GLA_KEY_WIDTH = 256
GLA_GATE_RANK = 16
GLA_GATE_TEMP = 16.0
GLA_CHUNK = 64
D_FF = 2816
LN_EPS = 1e-5
RMS_EPS = 1e-6
N_MOD = 6
DEEPNORM_ALPHA = 2.0 ** 0.25
IN_SPLITS = (FOX_WIDTH, FOX_WIDTH, FOX_WIDTH, FOX_HEADS, GLA_KEY_WIDTH, GLA_KEY_WIDTH,
             GLA_WIDTH, GLA_GATE_RANK, GLA_WIDTH)
LOG2E = math.log2(math.e)

LANES = 128
SUBLANES = 8
MXU_DIM = 256
SMALL_W = LANES
VMEM_LIMIT = 56 * 1024 * 1024

MOD_TN = 1536
INPROJ_TM = 512
FOX_TQ = 256
FOX_GROUP_HEADS = 4
FOX_SUM_ROWS = 16
FOX_TK = 512
GLA_TS = 512
GLA_PAIR = 2 * GLA_CHUNK
TAIL_TM = 512

NEG_BIG = -1e30
PRUNE_LOG2 = -1080.0
NORM_SLACK = 1.02

AUG_ROWS = 32


def _log_sigmoid(z):
    return jnp.minimum(z, 0.0) - jnp.log(1.0 + jnp.exp(-jnp.abs(z)))


def _split3(a):
    hi = a.astype(BF16)
    r1 = a - hi.astype(F32)
    mid = r1.astype(BF16)
    lo = (r1 - mid.astype(F32)).astype(BF16)
    return hi, mid, lo


def _dot(a, b):
    return jnp.dot(a, b, preferred_element_type=F32)


def _tri_left(tri, a):
    hi, mid, lo = _split3(a)
    return _dot(tri, hi) + _dot(tri, mid) + _dot(tri, lo)


def _tri_right(a, tri):
    hi, mid, lo = _split3(a)
    return _dot(hi, tri) + _dot(mid, tri) + _dot(lo, tri)


def _mod_kernel(c_ref, w_ref, b_ref, o_ref):
    o_ref[...] = jnp.dot(c_ref[...], w_ref[...], preferred_element_type=F32,
                         precision=lax.Precision.HIGHEST) + b_ref[...]


def _modulation(c, w_c, b_c):
    B, D = c.shape
    N = w_c.shape[1]
    rows = SUBLANES
    c_pad = jnp.zeros((rows, D), F32).at[:B].set(c)
    out = pl.pallas_call(
        _mod_kernel,
        grid=(N // MOD_TN,),
        in_specs=[
            pl.BlockSpec((rows, D), lambda j: (0, 0)),
            pl.BlockSpec((D, MOD_TN), lambda j: (0, j)),
            pl.BlockSpec((1, MOD_TN), lambda j: (0, j)),
        ],
        out_specs=pl.BlockSpec((rows, MOD_TN), lambda j: (0, j)),
        out_shape=jax.ShapeDtypeStruct((rows, N), F32),
        compiler_params=pltpu.CompilerParams(
            dimension_semantics=("arbitrary",), vmem_limit_bytes=VMEM_LIMIT),
        name="adaln_mod",
    )(c_pad, w_c, b_c.reshape(1, N))
    return out[:B].reshape(B, N_MOD, D)


_C_FQ, _C_FK, _C_FV = 0, 512, 1024
_C_GQ, _C_SM, _C_GK, _C_GV, _C_GR = 1536, 1792, 2048, 2304, 2816
_W_COLS = _C_GR + GLA_WIDTH
_KAUG_W = FOX_PAIRS * MXU_DIM


def _head_indicator():
    ind = np.zeros((FOX_WIDTH, MXU_DIM), np.float32)
    for h in range(FOX_HEADS):
        ind[h * FOX_HEAD_DIM:(h + 1) * FOX_HEAD_DIM, h] = 1.0
    return ind


def _inproj_kernel(x_ref, mod_ref, w_ref, bf_ref, ind_ref,
                   fqT_ref, kaug_ref, fvT_ref, gq_ref, gk_ref, gv_ref, gr_ref, sm_ref,
                   drel_ref, stats_ref, carry_ref):
    tm = x_ref.shape[1]

    @pl.when(pl.program_id(1) == 0)
    def _():
        carry_ref[...] = jnp.zeros_like(carry_ref)

    m = mod_ref[0]
    u = (x_ref[0] * (1.0 + m[1:2]) + m[0:1]).astype(BF16)

    def proj(lo, width):
        return _dot(u, w_ref[:, lo:lo + width])

    gq_small = proj(_C_GQ, 2 * GLA_KEY_WIDTH)
    gq_ref[0] = gq_small[:, :GLA_KEY_WIDTH] * (GLA_KEY_DIM ** -0.5)
    small = gq_small[:, GLA_KEY_WIDTH:GLA_KEY_WIDTH + SMALL_W]
    sm_ref[0] = small

    fq = proj(_C_FQ, FOX_WIDTH) * (FOX_HEAD_DIM ** -0.5 * LOG2E)
    fqT_ref[0] = fq.T.astype(BF16)
    fk32 = proj(_C_FK, FOX_WIDTH)
    fk = fk32.astype(BF16)
    qn2 = jnp.max(_dot((fq * fq).astype(BF16), ind_ref[...])[:, :SMALL_W], axis=0, keepdims=True)
    kn2 = jnp.max(_dot((fk32 * fk32).astype(BF16), ind_ref[...])[:, :SMALL_W], axis=0, keepdims=True)

    zT = (small + bf_ref[...]).T[:FOX_HEADS, :]
    lfT = _log_sigmoid(zT) * LOG2E
    r = lax.broadcasted_iota(jnp.int32, (tm, tm), 0)
    c = lax.broadcasted_iota(jnp.int32, (tm, tm), 1)
    triu = jnp.where(r <= c, 1.0, 0.0).astype(BF16)
    drelT = _tri_right(lfT, triu)
    drel_ref[0] = drelT

    c_before = carry_ref[...]
    srow = lax.broadcasted_iota(jnp.int32, (SUBLANES, SMALL_W), 0)
    slane = lax.broadcasted_iota(jnp.int32, (SUBLANES, SMALL_W), 1)
    total = jnp.sum(jnp.where(srow == slane, drelT[:, tm - 1:tm], 0.0), axis=0, keepdims=True)
    carry_ref[...] = c_before + total

    hi, mid, lo = _split3(drelT)
    ones3 = jnp.where(lax.broadcasted_iota(jnp.int32, (SUBLANES, tm), 0) < 3, 1.0, 0.0)
    slabT = jnp.concatenate(
        [-hi.astype(F32), -mid.astype(F32), -lo.astype(F32), ones3,
         jnp.zeros((LANES - 4 * SUBLANES, tm), F32)], axis=0)
    slab = slabT.T.astype(BF16)

    fvT_ref[0] = proj(_C_FV, FOX_WIDTH).T.astype(BF16)
    gv_ref[0] = proj(_C_GV, GLA_WIDTH).astype(BF16)
    gr_ref[0] = proj(_C_GR, GLA_WIDTH)
    gk_ref[0] = proj(_C_GK, GLA_KEY_WIDTH)

    stats_ref[0, 0] = jnp.where(srow == 0, c_before,
                                jnp.where(srow == 1, qn2, jnp.where(srow == 2, kn2, 0.0)))
    for p in range(FOX_PAIRS):
        kaug_ref[0, :, p * MXU_DIM:p * MXU_DIM + LANES] = fk[:, p * LANES:(p + 1) * LANES]
        kaug_ref[0, :, p * MXU_DIM + LANES:(p + 1) * MXU_DIM] = slab


def _in_projection(x, mod, w_all, b_f_pad):
    B, S, D = x.shape
    tm = INPROJ_TM
    ind = _head_indicator()
    tok = lambda w: pl.BlockSpec((1, tm, w), lambda b, s: (b, s, 0))
    tokT = lambda h: pl.BlockSpec((1, h, tm), lambda b, s: (b, 0, s))
    outs = pl.pallas_call(
        _inproj_kernel,
        grid=(B, S // tm),
        in_specs=[
            tok(D),
            pl.BlockSpec((1, N_MOD, D), lambda b, s: (b, 0, 0)),
            pl.BlockSpec((D, _W_COLS), lambda b, s: (0, 0)),
            pl.BlockSpec((1, SMALL_W), lambda b, s: (0, 0)),
            pl.BlockSpec(ind.shape, lambda b, s: (0, 0)),
        ],
        out_specs=[
            tokT(FOX_WIDTH), tok(_KAUG_W), tokT(FOX_WIDTH),
            tok(GLA_KEY_WIDTH), tok(GLA_KEY_WIDTH), tok(GLA_WIDTH), tok(GLA_WIDTH),
            tok(SMALL_W),
            tokT(FOX_HEADS),
            pl.BlockSpec((1, 1, SUBLANES, SMALL_W), lambda b, s: (b, s, 0, 0)),
        ],
        out_shape=[
            jax.ShapeDtypeStruct((B, FOX_WIDTH, S), BF16),
            jax.ShapeDtypeStruct((B, S, _KAUG_W), BF16),
            jax.ShapeDtypeStruct((B, FOX_WIDTH, S), BF16),
            jax.ShapeDtypeStruct((B, S, GLA_KEY_WIDTH), F32),
            jax.ShapeDtypeStruct((B, S, GLA_KEY_WIDTH), F32),
            jax.ShapeDtypeStruct((B, S, GLA_WIDTH), BF16),
            jax.ShapeDtypeStruct((B, S, GLA_WIDTH), F32),
            jax.ShapeDtypeStruct((B, S, SMALL_W), F32),
            jax.ShapeDtypeStruct((B, FOX_HEADS, S), F32),
            jax.ShapeDtypeStruct((B, S // tm, SUBLANES, SMALL_W), F32),
        ],
        scratch_shapes=[pltpu.VMEM((1, SMALL_W), F32)],
        compiler_params=pltpu.CompilerParams(
            dimension_semantics=("arbitrary", "arbitrary"), vmem_limit_bytes=VMEM_LIMIT),
        name="in_proj",
    )(x, mod, w_all, b_f_pad, jnp.asarray(ind, BF16))
    return outs


def _fox_kernel(jstart_ref, nitems_ref, qT_ref, kaug_ref, vT_ref, drel_ref, crep_ref, o_ref,
                qaug_ref, s_ref, p_ref, mx_ref, acc_ref, m_ref, mnext_ref, shift_ref, alpha_ref):
    tq, tk = FOX_TQ, FOX_TK
    half = FOX_HEAD_DIM
    S = kaug_ref.shape[1]
    nq = S // tq
    nh = FOX_GROUP_HEADS
    n_blk = crep_ref.shape[2] // nh
    g = pl.program_id(0) * pl.num_programs(1) + pl.program_id(1)

    def build_qaug(qi):
        q0 = pl.multiple_of(qi * tq, tq)
        dr = drel_ref[0, 0, :, pl.ds(q0, tq)]
        row = lax.broadcasted_iota(jnp.int32, (2 * half, tq), 0)
        r32 = lax.broadcasted_iota(jnp.int32, (AUG_ROWS, tq), 0)
        for h in range(nh):
            buf = (qi % 2) * nh + h
            qT = qT_ref[0, (h // 2) * 2 * half:(h // 2 + 1) * 2 * half, pl.ds(q0, tq)]
            own = (row < half) if h % 2 == 0 else (row >= half)
            qaug_ref[buf, 0:2 * half, :] = jnp.where(own, qT, jnp.zeros_like(qT))
            hi, mid, lo = _split3(dr[h:h + 1])
            gh = nh * pl.program_id(1) + h
            pick = (r32 == gh) | (r32 == gh + SUBLANES) | (r32 == gh + 2 * SUBLANES)
            slab = jnp.where(r32 == 24, hi.astype(F32),
                             jnp.where(r32 == 25, mid.astype(F32),
                                       jnp.where(r32 == 26, lo.astype(F32),
                                                 jnp.where(pick, 1.0, 0.0))))
            qaug_ref[buf, 2 * half:2 * half + AUG_ROWS, :] = slab.astype(BF16)
            qaug_ref[buf, 2 * half + AUG_ROWS:, :] = jnp.zeros(
                (MXU_DIM - 2 * half - AUG_ROWS, tq), BF16)

    def reset_state():
        acc_ref[...] = jnp.zeros_like(acc_ref)

    sum_rows = jnp.where(lax.broadcasted_iota(jnp.int32, (FOX_SUM_ROWS, tk), 0) == 0, 1.0, 0.0).astype(BF16)

    per = tk // tq

    def col_max8(s):
        return jnp.max(s.reshape(s.shape[0] // SUBLANES, SUBLANES, tq), axis=0)

    def qk_dots(qi, j):
        k_off = pl.multiple_of(j * tk, tk)
        kts = [kaug_ref[0, pl.ds(k_off, tk), p * MXU_DIM:(p + 1) * MXU_DIM] for p in range(nh // 2)]
        return [_dot(kts[h // 2], qaug_ref[(qi % 2) * nh + h]) for h in range(nh)]

    def store_scores(scores, slot):
        for h in range(nh):
            s_ref[slot, h] = scores[h]
            for g in range(per):
                mx_ref[per * h + g] = col_max8(scores[h][g * tq:(g + 1) * tq])

    def mask_stage(qi, slot):
        d = qi % per
        d0 = pl.multiple_of(d * tq, tq)
        kr = lax.broadcasted_iota(jnp.int32, (tq, tq), 0)
        qc = lax.broadcasted_iota(jnp.int32, (tq, tq), 1)
        for h in range(nh):
            tri = jnp.where(kr <= qc, s_ref[slot, h, pl.ds(d0, tq), :], NEG_BIG)
            s_ref[slot, h, pl.ds(d0, tq), :] = tri
            mx_ref[per * h + d] = col_max8(tri)
        for g in range(1, per):
            @pl.when(d < g)
            def _():
                for h in range(nh):
                    s_ref[slot, h, g * tq:(g + 1) * tq, :] = jnp.full((tq, tq), NEG_BIG, F32)
                    mx_ref[per * h + g] = jnp.full((SUBLANES, tq), NEG_BIG, F32)

    def prep_stage(qi, j, first):
        i_blk = qi * tq // INPROJ_TM
        for h in range(nh):
            big_c = (crep_ref[0, 0, pl.ds(h * n_blk + i_blk, 1), :]
                     - crep_ref[0, 0, pl.ds(h * n_blk + j, 1), :])
            mx = mx_ref[per * h]
            for g in range(1, per):
                mx = jnp.maximum(mx, mx_ref[per * h + g])
            m_prev = jnp.where(first, NEG_BIG, m_ref[h])
            m_new = jnp.maximum(m_prev, jnp.max(mx, axis=0, keepdims=True) + big_c)
            mnext_ref[h] = m_new
            shift_ref[h] = m_new - big_c
            alpha_ref[h] = jnp.exp2(m_prev - m_new)

    def exp_stage(slot):
        for h in range(nh):
            p_ref[h] = jnp.exp2(s_ref[slot, h] - shift_ref[h]).astype(BF16)

    def pv_stage(j):
        k_off = pl.multiple_of(j * tk, tk)
        for h in range(nh):
            vT = jnp.concatenate([vT_ref[0, h * half:(h + 1) * half, pl.ds(k_off, tk)], sum_rows], axis=0)
            acc_ref[h] = alpha_ref[h] * acc_ref[h] + _dot(vT, p_ref[h])
            m_ref[h] = mnext_ref[h]

    def finalize(qi):
        outs = []
        for h in range(nh):
            acc = acc_ref[h]
            outs.append(acc[:half] / acc[half:half + 1])
        oT = jnp.concatenate(outs, axis=0)
        o_ref[0, pl.ds(pl.multiple_of(qi * tq, tq), tq), :] = oT.T.astype(o_ref.dtype)

    zero = jnp.int32(0)
    build_qaug(zero)
    build_qaug(zero + 1)
    reset_state()
    store_scores(qk_dots(zero, zero), 0)
    mask_stage(zero, 0)
    prep_stage(zero, zero, True)

    def body(t, carry):
        qi, j = carry
        last = j == qi * tq // tk
        qi2 = jnp.where(last, qi + 1, qi)
        j2 = jnp.where(last, jstart_ref[g * nq + jnp.minimum(qi + 1, nq - 1)], j + 1)
        def stages(rd, wr):
            exp_stage(rd)
            nxt = qk_dots(qi2, j2)
            pv_stage(j)
            store_scores(nxt, wr)
            prep_stage(qi2, j2, last)

            @pl.when(j2 == qi2 * tq // tk)
            def _():
                mask_stage(qi2, wr)
                prep_stage(qi2, j2, last)

        @pl.when(t % 2 == 0)
        def _():
            stages(0, 1)

        @pl.when(t % 2 == 1)
        def _():
            stages(1, 0)

        @pl.when(last)
        def _():
            finalize(qi)
            reset_state()

            @pl.when(qi + 2 < nq)
            def _():
                build_qaug(qi + 2)

        return qi2, j2

    n_last = nitems_ref[g] - 1
    qi, j = lax.fori_loop(0, n_last, body, (jnp.int32(0), jnp.int32(0)))
    for slot in range(2):
        @pl.when(n_last % 2 == slot)
        def _():
            exp_stage(slot)
    pv_stage(j)
    finalize(qi)


def _prune_plan(stats, drel, nq):
    B, n_blk = stats.shape[:2]
    per = FOX_TK // FOX_TQ
    c = stats[:, :, 0, :FOX_HEADS]
    qn = jnp.sqrt(stats[:, :, 1, :FOX_HEADS])
    kn = jnp.sqrt(stats[:, :, 2, :FOX_HEADS])
    c_next = jnp.concatenate([c[:, 1:], c[:, -1:]], axis=1)
    d_in = jnp.transpose(drel[:, :, FOX_TQ - 1::FOX_TQ], (0, 2, 1))
    d_in = jnp.concatenate([jnp.zeros_like(d_in[:, :1]), d_in[:, :-1]], axis=1)
    first_in_tile = (jnp.arange(nq) % per == 0)[None, :, None]
    d_before = jnp.repeat(c, per, axis=1) + jnp.where(first_in_tile, 0.0, d_in)
    qn_q = jnp.repeat(qn, per, axis=1)
    kn_q = jnp.repeat(kn, per, axis=1)
    bound = (d_before[:, :, None, :] - c_next[:, None, :, :]
             + NORM_SLACK * qn_q[:, :, None, :] * (kn[:, None, :, :] + kn_q[:, :, None, :]))
    i_idx = (jnp.arange(nq) // per)[:, None]
    j_idx = jnp.arange(n_blk)[None, :]
    skip = (bound <= PRUNE_LOG2) & (j_idx + 1 <= i_idx)[None, :, :, None]
    nh = FOX_GROUP_HEADS
    skip_pair = skip[..., 0::nh]
    for h in range(1, nh):
        skip_pair = skip_pair & skip[..., h::nh]
    jstart = jnp.min(jnp.where(skip_pair, n_blk, j_idx[None, :, :, None]), axis=2)
    jstart = jnp.transpose(jstart, (0, 2, 1))
    n_items = jnp.sum(i_idx[:, 0][None, None, :] - jstart + 1, axis=2)
    return jstart.reshape(-1).astype(jnp.int32), n_items.reshape(-1).astype(jnp.int32)


def _fox_attention(fqT, kaug, fvT, drel, stats):
    B, _, S = fqT.shape
    tq, tk = FOX_TQ, FOX_TK
    n_blk = S // INPROJ_TM
    nh = FOX_GROUP_HEADS
    n_groups = FOX_HEADS // nh
    drel4 = drel.reshape(B, n_groups, nh, S)
    c = stats[:, :, 0, :FOX_HEADS]
    c = jnp.transpose(c, (0, 2, 1)).reshape(B, n_groups, nh * n_blk, 1)
    crep = jnp.broadcast_to(c, (B, n_groups, nh * n_blk, tq))
    jstart, n_items = _prune_plan(stats, drel, S // tq)
    grid_spec = pltpu.PrefetchScalarGridSpec(
        num_scalar_prefetch=2,
        grid=(B, n_groups),
        in_specs=[
            pl.BlockSpec((1, nh * FOX_HEAD_DIM, S), lambda b, p, js, ni: (b, p, 0)),
            pl.BlockSpec((1, S, nh // 2 * MXU_DIM), lambda b, p, js, ni: (b, 0, p)),
            pl.BlockSpec((1, nh * FOX_HEAD_DIM, S), lambda b, p, js, ni: (b, p, 0)),
            pl.BlockSpec((1, 1, nh, S), lambda b, p, js, ni: (b, p, 0, 0)),
            pl.BlockSpec((1, 1, nh * n_blk, tq), lambda b, p, js, ni: (b, p, 0, 0)),
        ],
        out_specs=pl.BlockSpec((1, S, nh * FOX_HEAD_DIM), lambda b, p, js, ni: (b, 0, p)),
        scratch_shapes=[
            pltpu.VMEM((2 * nh, MXU_DIM, tq), BF16),
            pltpu.VMEM((2, nh, tk, tq), F32),
            pltpu.VMEM((nh, tk, tq), BF16),
            pltpu.VMEM((nh * (tk // tq), SUBLANES, tq), F32),
            pltpu.VMEM((nh, FOX_HEAD_DIM + FOX_SUM_ROWS, tq), F32),
            pltpu.VMEM((nh, 1, tq), F32),
            pltpu.VMEM((nh, 1, tq), F32),
            pltpu.VMEM((nh, 1, tq), F32),
            pltpu.VMEM((nh, 1, tq), F32),
        ])
    return pl.pallas_call(
        _fox_kernel,
        grid_spec=grid_spec,
        out_shape=jax.ShapeDtypeStruct((B, S, FOX_WIDTH), BF16),
        compiler_params=pltpu.CompilerParams(
            dimension_semantics=("arbitrary", "arbitrary"),
            vmem_limit_bytes=VMEM_LIMIT),
        name="fox_attn",
    )(jstart, n_items, fqT, kaug, fvT, drel4, crep)


def _gla_kernel(q_ref, k_ref, v_ref, r_ref, sm_ref, wa_ref, ba_ref, g_ref, o_ref, state_ref):
    ts = q_ref.shape[1]
    C, P = GLA_CHUNK, GLA_PAIR
    dk, dv, H = GLA_KEY_DIM, GLA_VAL_DIM, GLA_HEADS
    KW = GLA_KEY_WIDTH

    @pl.when(pl.program_id(1) == 0)
    def _():
        state_ref[...] = jnp.zeros_like(state_ref)

    la_all = _log_sigmoid(_dot(sm_ref[0].astype(BF16), wa_ref[...]) + ba_ref[...]) \
        * (1.0 / GLA_GATE_TEMP)

    ri = lax.broadcasted_iota(jnp.int32, (P, P), 0)
    ci = lax.broadcasted_iota(jnp.int32, (P, P), 1)
    tril2 = jnp.where(((ri < C) == (ci < C)) & (ci <= ri), 1.0, 0.0).astype(BF16)

    rs = lax.broadcasted_iota(jnp.int32, (2 * H * C, P), 0)
    cs = lax.broadcasted_iota(jnp.int32, (2 * H * C, P), 1)
    causal = ((rs >= H * C) == (cs >= C)) & ((cs & (C - 1)) <= (rs & (C - 1)))

    lane_q = lax.broadcasted_iota(jnp.int32, (C, KW), 1)
    lane_t = lax.broadcasted_iota(jnp.int32, (KW, P), 1)
    first = lane_t < C

    n_slabs = ts // P
    slab_rows = [slice(pi * P, (pi + 1) * P) for pi in range(n_slabs)]
    bs = [_tri_left(tril2, la_all[rows]) for rows in slab_rows]

    qms, scores, kv0s, kv1s, dec0s, dec1s, vss = [], [], [], [], [], [], []
    for pi in range(n_slabs):
        rows = slab_rows[pi]
        b = bs[pi]
        bT = b.T
        bl0 = bT[:, C - 1:C]
        bl1 = bT[:, P - 1:P]
        q_dec = q_ref[0, rows, :] * jnp.exp(b)
        kT = k_ref[0, rows, :].T
        k_invT = (kT * jnp.exp(-bT)).astype(BF16)
        k_teT = kT * jnp.exp(jnp.where(first, bl0, bl1) - bT)
        zt = jnp.zeros_like(k_teT)
        k_te0 = jnp.where(first, k_teT, zt).astype(BF16)
        k_te1 = jnp.where(first, zt, k_teT).astype(BF16)
        dec0 = jnp.exp(bl0)
        dec1 = jnp.exp(bl1)

        vs = [v_ref[0, rows, h * dv:(h + 1) * dv] for h in range(H)]
        kv = [_dot(jnp.concatenate([k_te0[h * dk:(h + 1) * dk, :], k_te1[h * dk:(h + 1) * dk, :]], axis=0),
                   vs[h]) for h in range(H)]
        kv0s.append(jnp.concatenate([kv[h][:dk] for h in range(H)], axis=0))
        kv1s.append(jnp.concatenate([kv[h][dk:] for h in range(H)], axis=0))
        dec0s.append(dec0)
        dec1s.append(dec1)
        vss.append(vs)

        zq = jnp.zeros((C, KW), F32)
        qm = [jnp.concatenate(
            [jnp.where((lane_q >= h * dk) & (lane_q < (h + 1) * dk), q_dec[c * C:(c + 1) * C], zq)
             for h in range(H)], axis=0).astype(BF16) for c in range(2)]
        a = _dot(jnp.concatenate(qm, axis=0), k_invT)
        scores.append(jnp.where(causal, a, 0.0).astype(BF16))
        qms.append(qm)

    s = state_ref[...]
    o_inters = []
    for pi in range(n_slabs):
        s1 = dec0s[pi] * s + kv0s[pi]
        o_inters.append([_dot(qms[pi][0], s.astype(BF16)), _dot(qms[pi][1], s1.astype(BF16))])
        s = dec1s[pi] * s1 + kv1s[pi]
    state_ref[...] = s

    for pi in range(n_slabs):
        rows = slab_rows[pi]
        a, o_inter, vs = scores[pi], o_inters[pi], vss[pi]
        for h in range(H):
            hv = slice(h * dv, (h + 1) * dv)
            a_h = jnp.concatenate([a[(c * H + h) * C:(c * H + h + 1) * C] for c in range(2)], axis=0)
            o = _dot(a_h, vs[h]) + jnp.concatenate(
                [o_inter[c][h * C:(h + 1) * C] for c in range(2)], axis=0)
            o = o * lax.rsqrt(jnp.mean(o * o, axis=-1, keepdims=True) + RMS_EPS)
            rr = r_ref[0, rows, hv]
            o = o * g_ref[:, hv] * (rr * (1.0 / (1.0 + jnp.exp(-rr))))
            o_ref[0, rows, hv] = o.astype(o_ref.dtype)


def _gla_attention(gq, gk, gv, gr, small, w_a2_pad, b_a, g_gla):
    B, S, _ = gq.shape
    ts = GLA_TS
    tok = lambda w: pl.BlockSpec((1, ts, w), lambda b, s: (b, s, 0))
    full = lambda r, c: pl.BlockSpec((r, c), lambda b, s: (0, 0))
    return pl.pallas_call(
        _gla_kernel,
        grid=(B, S // ts),
        in_specs=[
            tok(GLA_KEY_WIDTH), tok(GLA_KEY_WIDTH), tok(GLA_WIDTH), tok(GLA_WIDTH), tok(SMALL_W),
            full(SMALL_W, GLA_KEY_WIDTH), full(1, GLA_KEY_WIDTH), full(1, GLA_WIDTH),
        ],
        out_specs=tok(GLA_WIDTH),
        out_shape=jax.ShapeDtypeStruct((B, S, GLA_WIDTH), BF16),
        scratch_shapes=[pltpu.VMEM((GLA_KEY_WIDTH, GLA_VAL_DIM), F32)],
        compiler_params=pltpu.CompilerParams(
            dimension_semantics=("arbitrary", "arbitrary"), vmem_limit_bytes=VMEM_LIMIT),
        name="gla_attn",
    )(gq, gk, gv, gr, small, w_a2_pad, b_a, g_gla)


def _layer_norm(z, g, b):
    mu = jnp.mean(z, axis=-1, keepdims=True)
    d = z - mu
    var = jnp.mean(d * d, axis=-1, keepdims=True)
    return d * lax.rsqrt(var + LN_EPS) * g + b


def _tail_kernel(x_ref, fox_ref, gla_ref, mod_ref, wof_ref, wog_ref, ln1g_ref, ln1b_ref,
                 wg_ref, wu_ref, wd_ref, ln2g_ref, ln2b_ref, o_ref):
    m = mod_ref[0]
    tm = x_ref.shape[0]
    halves = [slice(0, tm // 2), slice(tm // 2, tm)]
    ys = [_dot(fox_ref[r, :], wof_ref[...]) + _dot(gla_ref[r, :], wog_ref[...]) for r in halves]
    x1s, gs, ups = [], [], []
    for r, y in zip(halves, ys):
        x1 = _layer_norm(DEEPNORM_ALPHA * x_ref[r, :] + (1.0 + m[2:3]) * y,
                         ln1g_ref[...], ln1b_ref[...])
        u2 = (x1 * (1.0 + m[4:5]) + m[3:4]).astype(BF16)
        x1s.append(x1)
        gs.append(_dot(u2, wg_ref[...]))
        ups.append(_dot(u2, wu_ref[...]))
    y2s = []
    for g, up in zip(gs, ups):
        h = (g * (1.0 / (1.0 + jnp.exp(-g))) * up).astype(BF16)
        y2s.append(_dot(h, wd_ref[...]))
    for r, x1, y2 in zip(halves, x1s, y2s):
        o_ref[r, :] = _layer_norm(DEEPNORM_ALPHA * x1 + (1.0 + m[5:6]) * y2,
                                  ln2g_ref[...], ln2b_ref[...])


def _tail(x2d, fox2d, gla2d, mod, wo, ln1g, ln1b, wg, wu, wd, ln2g, ln2b, steps_per_batch):
    T, D = x2d.shape
    tm = TAIL_TM
    const = lambda r, c: pl.BlockSpec((r, c), lambda i: (0, 0), pipeline_mode=pl.Buffered(1))
    tok = lambda w: pl.BlockSpec((tm, w), lambda i: (i, 0))
    return pl.pallas_call(
        _tail_kernel,
        grid=(T // tm,),
        in_specs=[
            tok(D), tok(FOX_WIDTH), tok(GLA_WIDTH),
            pl.BlockSpec((1, N_MOD, D), lambda i: (i // steps_per_batch, 0, 0)),
            pl.BlockSpec((FOX_WIDTH, D), lambda i: (0, 0), pipeline_mode=pl.Buffered(1)),
            pl.BlockSpec((GLA_WIDTH, D), lambda i: (1, 0), pipeline_mode=pl.Buffered(1)),
            const(1, D), const(1, D),
            const(D, D_FF), const(D, D_FF), const(D_FF, D), const(1, D), const(1, D),
        ],
        out_specs=tok(D),
        out_shape=jax.ShapeDtypeStruct((T, D), F32),
        compiler_params=pltpu.CompilerParams(
            dimension_semantics=("arbitrary",), vmem_limit_bytes=VMEM_LIMIT),
        name="tail",
    )(x2d, fox2d, gla2d, mod, wo, wo, ln1g, ln1b, wg, wu, wd, ln2g, ln2b)


def kernel(x, c, w_c, b_c, w_in, b_f, w_a2, b_a, g_gla, w_o, ln1_g, ln1_b,
           w_gate, w_up, w_down, ln2_g, ln2_b):
    B, S, D = x.shape
    assert D == D_MODEL and S % INPROJ_TM == 0 and INPROJ_TM == FOX_TK and FOX_TK % FOX_TQ == 0

    offs = [0]
    for wdt in IN_SPLITS:
        offs.append(offs[-1] + wdt)
    seg = lambda i: w_in[:, offs[i]:offs[i + 1]]
    w_small = jnp.concatenate(
        [seg(3), seg(7), jnp.zeros((D, SMALL_W - FOX_HEADS - GLA_GATE_RANK), F32)], axis=1)
    w_all = jnp.concatenate(
        [seg(0), seg(1), seg(2), seg(4), w_small, jnp.zeros((D, SMALL_W), F32),
         seg(5), seg(6), seg(8)], axis=1).astype(BF16)
    b_f_pad = jnp.zeros((1, SMALL_W), F32).at[0, :FOX_HEADS].set(b_f)
    w_a2_pad = jnp.zeros((SMALL_W, GLA_KEY_WIDTH), F32) \
        .at[FOX_HEADS:FOX_HEADS + GLA_GATE_RANK].set(w_a2).astype(BF16)

    mod = _modulation(c, w_c, b_c)

    fqT, kaug, fvT, gq, gk, gv, gr, small, drel, stats = _in_projection(x, mod, w_all, b_f_pad)
    fox = _fox_attention(fqT, kaug, fvT, drel, stats)
    gla = _gla_attention(gq, gk, gv, gr, small, w_a2_pad,
                         b_a.reshape(1, -1), g_gla.reshape(1, -1))

    T = B * S
    w_o_b = w_o.astype(BF16)
    out = _tail(x.reshape(T, D), fox.reshape(T, FOX_WIDTH), gla.reshape(T, GLA_WIDTH), mod,
                w_o_b,
                ln1_g.reshape(1, D), ln1_b.reshape(1, D),
                w_gate.astype(BF16), w_up.astype(BF16), w_down.astype(BF16),
                ln2_g.reshape(1, D), ln2_b.reshape(1, D), S // TAIL_TM)
    return out.reshape(B, S, D)
```

```python
import math

import numpy as np
import jax
import jax.numpy as jnp
from jax import lax
from jax.experimental import pallas as pl
from jax.experimental.pallas import tpu as pltpu

F32 = jnp.float32
BF16 = jnp.bfloat16

D_MODEL = 1024
FOX_WIDTH = 512
FOX_HEAD_DIM = 64
FOX_HEADS = 8
FOX_PAIRS = FOX_HEADS // 2
GLA_WIDTH = 512
GLA_HEADS = 4
GLA_VAL_DIM = 128
GLA_KEY_DIM = 64
GLA_KEY_WIDTH = 256
GLA_GATE_RANK = 16
GLA_GATE_TEMP = 16.0
GLA_CHUNK = 64
D_FF = 2816
LN_EPS = 1e-5
RMS_EPS = 1e-6
N_MOD = 6
DEEPNORM_ALPHA = 2.0 ** 0.25
IN_SPLITS = (FOX_WIDTH, FOX_WIDTH, FOX_WIDTH, FOX_HEADS, GLA_KEY_WIDTH, GLA_KEY_WIDTH,
             GLA_WIDTH, GLA_GATE_RANK, GLA_WIDTH)
LOG2E = math.log2(math.e)

LANES = 128
SUBLANES = 8
MXU_DIM = 256
SMALL_W = LANES
VMEM_LIMIT = 56 * 1024 * 1024

MOD_TN = 1536
INPROJ_TM = 512
FOX_TQ = 256
FOX_GROUP_HEADS = 4
FOX_SUM_ROWS = 16
FOX_TK = 512
GLA_TS = 512
GLA_PAIR = 2 * GLA_CHUNK
TAIL_TM = 512

NEG_BIG = -1e30
PRUNE_LOG2 = -1080.0
NORM_SLACK = 1.02

AUG_ROWS = 32


def _log_sigmoid(z):
    return jnp.minimum(z, 0.0) - jnp.log(1.0 + jnp.exp(-jnp.abs(z)))


def _split3(a):
    hi = a.astype(BF16)
    r1 = a - hi.astype(F32)
    mid = r1.astype(BF16)
    lo = (r1 - mid.astype(F32)).astype(BF16)
    return hi, mid, lo


def _dot(a, b):
    return jnp.dot(a, b, preferred_element_type=F32)


def _tri_left(tri, a):
    hi, mid, lo = _split3(a)
    return _dot(tri, hi) + _dot(tri, mid) + _dot(tri, lo)


def _tri_right(a, tri):
    hi, mid, lo = _split3(a)
    return _dot(hi, tri) + _dot(mid, tri) + _dot(lo, tri)


def _mod_kernel(c_ref, w_ref, b_ref, o_ref):
    o_ref[...] = jnp.dot(c_ref[...], w_ref[...], preferred_element_type=F32,
                         precision=lax.Precision.HIGHEST) + b_ref[...]


def _modulation(c, w_c, b_c):
    B, D = c.shape
    N = w_c.shape[1]
    rows = SUBLANES
    c_pad = jnp.zeros((rows, D), F32).at[:B].set(c)
    out = pl.pallas_call(
        _mod_kernel,
        grid=(N // MOD_TN,),
        in_specs=[
            pl.BlockSpec((rows, D), lambda j: (0, 0)),
            pl.BlockSpec((D, MOD_TN), lambda j: (0, j)),
            pl.BlockSpec((1, MOD_TN), lambda j: (0, j)),
        ],
        out_specs=pl.BlockSpec((rows, MOD_TN), lambda j: (0, j)),
        out_shape=jax.ShapeDtypeStruct((rows, N), F32),
        compiler_params=pltpu.CompilerParams(
            dimension_semantics=("arbitrary",), vmem_limit_bytes=VMEM_LIMIT),
        name="adaln_mod",
    )(c_pad, w_c, b_c.reshape(1, N))
    return out[:B].reshape(B, N_MOD, D)


_C_FQ, _C_FK, _C_FV = 0, 512, 1024
_C_GQ, _C_SM, _C_GK, _C_GV, _C_GR = 1536, 1792, 2048, 2304, 2816
_W_COLS = _C_GR + GLA_WIDTH
_KAUG_W = FOX_PAIRS * MXU_DIM


def _head_indicator():
    ind = np.zeros((FOX_WIDTH, MXU_DIM), np.float32)
    for h in range(FOX_HEADS):
        ind[h * FOX_HEAD_DIM:(h + 1) * FOX_HEAD_DIM, h] = 1.0
    return ind


def _inproj_kernel(x_ref, mod_ref, w_ref, bf_ref, ind_ref, wg32_ref, wu32_ref, wd32_ref,
                   fqT_ref, kaug_ref, fvT_ref, gq_ref, gk_ref, gv_ref, gr_ref, sm_ref,
                   drel_ref, stats_ref, wg16_ref, wu16_ref, wd16_ref, carry_ref):
    tm = x_ref.shape[1]

    wg16_ref[...] = wg32_ref[...].astype(BF16)
    wu16_ref[...] = wu32_ref[...].astype(BF16)
    wd16_ref[...] = wd32_ref[...].astype(BF16)

    @pl.when(pl.program_id(1) == 0)
    def _():
        carry_ref[...] = jnp.zeros_like(carry_ref)

    m = mod_ref[0]
    u = (x_ref[0] * (1.0 + m[1:2]) + m[0:1]).astype(BF16)

    def proj(lo, width):
        return _dot(u, w_ref[:, lo:lo + width])

    gq_small = proj(_C_GQ, 2 * GLA_KEY_WIDTH)
    gq_ref[0] = gq_small[:, :GLA_KEY_WIDTH] * (GLA_KEY_DIM ** -0.5)
    small = gq_small[:, GLA_KEY_WIDTH:GLA_KEY_WIDTH + SMALL_W]
    sm_ref[0] = small

    fq = proj(_C_FQ, FOX_WIDTH) * (FOX_HEAD_DIM ** -0.5 * LOG2E)
    fqT_ref[0] = fq.T.astype(BF16)
    fk32 = proj(_C_FK, FOX_WIDTH)
    fk = fk32.astype(BF16)
    qn2 = jnp.max(_dot((fq * fq).astype(BF16), ind_ref[...])[:, :SMALL_W], axis=0, keepdims=True)
    kn2 = jnp.max(_dot((fk32 * fk32).astype(BF16), ind_ref[...])[:, :SMALL_W], axis=0, keepdims=True)

    zT = (small + bf_ref[...]).T[:FOX_HEADS, :]
    lfT = _log_sigmoid(zT) * LOG2E
    r = lax.broadcasted_iota(jnp.int32, (tm, tm), 0)
    c = lax.broadcasted_iota(jnp.int32, (tm, tm), 1)
    triu = jnp.where(r <= c, 1.0, 0.0).astype(BF16)
    drelT = _tri_right(lfT, triu)
    drel_ref[0] = drelT

    c_before = carry_ref[...]
    srow = lax.broadcasted_iota(jnp.int32, (SUBLANES, SMALL_W), 0)
    slane = lax.broadcasted_iota(jnp.int32, (SUBLANES, SMALL_W), 1)
    total = jnp.sum(jnp.where(srow == slane, drelT[:, tm - 1:tm], 0.0), axis=0, keepdims=True)
    carry_ref[...] = c_before + total

    hi, mid, lo = _split3(drelT)
    ones3 = jnp.where(lax.broadcasted_iota(jnp.int32, (SUBLANES, tm), 0) < 3, 1.0, 0.0)
    slabT = jnp.concatenate(
        [-hi.astype(F32), -mid.astype(F32), -lo.astype(F32), ones3,
         jnp.zeros((LANES - 4 * SUBLANES, tm), F32)], axis=0)
    slab = slabT.T.astype(BF16)

    fvT_ref[0] = proj(_C_FV, FOX_WIDTH).T.astype(BF16)
    gv_ref[0] = proj(_C_GV, GLA_WIDTH).astype(BF16)
    gr_ref[0] = proj(_C_GR, GLA_WIDTH)
    gk_ref[0] = proj(_C_GK, GLA_KEY_WIDTH)

    stats_ref[0, 0] = jnp.where(srow == 0, c_before,
                                jnp.where(srow == 1, qn2, jnp.where(srow == 2, kn2, 0.0)))
    for p in range(FOX_PAIRS):
        kaug_ref[0, :, p * MXU_DIM:p * MXU_DIM + LANES] = fk[:, p * LANES:(p + 1) * LANES]
        kaug_ref[0, :, p * MXU_DIM + LANES:(p + 1) * MXU_DIM] = slab


def _in_projection(x, mod, w_all, b_f_pad, w_gate, w_up, w_down):
    B, S, D = x.shape
    tm = INPROJ_TM
    n_steps = B * (S // tm)
    steps_b = S // tm
    up_rows = D // n_steps
    down_rows = 2 * D_FF // n_steps
    assert D % n_steps == 0 and up_rows % 16 == 0 and (2 * D_FF) % n_steps == 0 and down_rows % 16 == 0
    up_spec = pl.BlockSpec((up_rows, D_FF), lambda b, s: (b * steps_b + s, 0))
    down_spec = pl.BlockSpec((down_rows, D), lambda b, s: ((b * steps_b + s) // 2, 0))
    ind = _head_indicator()
    tok = lambda w: pl.BlockSpec((1, tm, w), lambda b, s: (b, s, 0))
    tokT = lambda h: pl.BlockSpec((1, h, tm), lambda b, s: (b, 0, s))
    outs = pl.pallas_call(
        _inproj_kernel,
        grid=(B, S // tm),
        in_specs=[
            tok(D),
            pl.BlockSpec((1, N_MOD, D), lambda b, s: (b, 0, 0)),
            pl.BlockSpec((D, _W_COLS), lambda b, s: (0, 0)),
            pl.BlockSpec((1, SMALL_W), lambda b, s: (0, 0)),
            pl.BlockSpec(ind.shape, lambda b, s: (0, 0)),
            up_spec, up_spec, down_spec,
        ],
        out_specs=[
            tokT(FOX_WIDTH), tok(_KAUG_W), tokT(FOX_WIDTH),
            tok(GLA_KEY_WIDTH), tok(GLA_KEY_WIDTH), tok(GLA_WIDTH), tok(GLA_WIDTH),
            tok(SMALL_W),
            tokT(FOX_HEADS),
            pl.BlockSpec((1, 1, SUBLANES, SMALL_W), lambda b, s: (b, s, 0, 0)),
            up_spec, up_spec, down_spec,
        ],
        out_shape=[
            jax.ShapeDtypeStruct((B, FOX_WIDTH, S), BF16),
            jax.ShapeDtypeStruct((B, S, _KAUG_W), BF16),
            jax.ShapeDtypeStruct((B, FOX_WIDTH, S), BF16),
            jax.ShapeDtypeStruct((B, S, GLA_KEY_WIDTH), F32),
            jax.ShapeDtypeStruct((B, S, GLA_KEY_WIDTH), F32),
            jax.ShapeDtypeStruct((B, S, GLA_WIDTH), BF16),
            jax.ShapeDtypeStruct((B, S, GLA_WIDTH), F32),
            jax.ShapeDtypeStruct((B, S, SMALL_W), F32),
            jax.ShapeDtypeStruct((B, FOX_HEADS, S), F32),
            jax.ShapeDtypeStruct((B, S // tm, SUBLANES, SMALL_W), F32),
            jax.ShapeDtypeStruct(w_gate.shape, BF16),
            jax.ShapeDtypeStruct(w_up.shape, BF16),
            jax.ShapeDtypeStruct(w_down.shape, BF16),
        ],
        scratch_shapes=[pltpu.VMEM((1, SMALL_W), F32)],
        compiler_params=pltpu.CompilerParams(
            dimension_semantics=("arbitrary", "arbitrary"), vmem_limit_bytes=VMEM_LIMIT),
        name="in_proj",
    )(x, mod, w_all, b_f_pad, jnp.asarray(ind, BF16), w_gate, w_up, w_down)
    return outs


def _fox_kernel(jstart_ref, nitems_ref, qT_ref, kaug_ref, vT_ref, drel_ref, crep_ref, o_ref,
                qaug_ref, s_ref, p_ref, mx_ref, acc_ref, m_ref, mnext_ref, shift_ref, alpha_ref):
    tq, tk = FOX_TQ, FOX_TK
    half = FOX_HEAD_DIM
    S = kaug_ref.shape[1]
    nq = S // tq
    nh = FOX_GROUP_HEADS
    n_blk = crep_ref.shape[2] // nh
    g = pl.program_id(0) * pl.num_programs(1) + pl.program_id(1)

    def build_qaug(qi):
        q0 = pl.multiple_of(qi * tq, tq)
        dr = drel_ref[0, 0, :, pl.ds(q0, tq)]
        row = lax.broadcasted_iota(jnp.int32, (2 * half, tq), 0)
        r32 = lax.broadcasted_iota(jnp.int32, (AUG_ROWS, tq), 0)
        for h in range(nh):
            buf = (qi % 2) * nh + h
            qT = qT_ref[0, (h // 2) * 2 * half:(h // 2 + 1) * 2 * half, pl.ds(q0, tq)]
            own = (row < half) if h % 2 == 0 else (row >= half)
            qaug_ref[buf, 0:2 * half, :] = jnp.where(own, qT, jnp.zeros_like(qT))
            hi, mid, lo = _split3(dr[h:h + 1])
            gh = nh * pl.program_id(1) + h
            pick = (r32 == gh) | (r32 == gh + SUBLANES) | (r32 == gh + 2 * SUBLANES)
            slab = jnp.where(r32 == 24, hi.astype(F32),
                             jnp.where(r32 == 25, mid.astype(F32),
                                       jnp.where(r32 == 26, lo.astype(F32),
                                                 jnp.where(pick, 1.0, 0.0))))
            qaug_ref[buf, 2 * half:2 * half + AUG_ROWS, :] = slab.astype(BF16)
            qaug_ref[buf, 2 * half + AUG_ROWS:, :] = jnp.zeros(
                (MXU_DIM - 2 * half - AUG_ROWS, tq), BF16)

    def reset_state():
        acc_ref[...] = jnp.zeros_like(acc_ref)

    sum_rows = jnp.where(lax.broadcasted_iota(jnp.int32, (FOX_SUM_ROWS, tk), 0) == 0, 1.0, 0.0).astype(BF16)

    per = tk // tq

    def col_max8(s):
        return jnp.max(s.reshape(s.shape[0] // SUBLANES, SUBLANES, tq), axis=0)

    def qk_dots(qi, j):
        k_off = pl.multiple_of(j * tk, tk)
        kts = [kaug_ref[0, pl.ds(k_off, tk), p * MXU_DIM:(p + 1) * MXU_DIM] for p in range(nh // 2)]
        return [_dot(kts[h // 2], qaug_ref[(qi % 2) * nh + h]) for h in range(nh)]

    def store_scores(scores, slot):
        for h in range(nh):
            s_ref[slot, h] = scores[h]
            for g in range(per):
                mx_ref[per * h + g] = col_max8(scores[h][g * tq:(g + 1) * tq])

    def mask_stage(qi, slot):
        d = qi % per
        d0 = pl.multiple_of(d * tq, tq)
        kr = lax.broadcasted_iota(jnp.int32, (tq, tq), 0)
        qc = lax.broadcasted_iota(jnp.int32, (tq, tq), 1)
        for h in range(nh):
            tri = jnp.where(kr <= qc, s_ref[slot, h, pl.ds(d0, tq), :], NEG_BIG)
            s_ref[slot, h, pl.ds(d0, tq), :] = tri
            mx_ref[per * h + d] = col_max8(tri)
        for g in range(1, per):
            @pl.when(d < g)
            def _():
                for h in range(nh):
                    s_ref[slot, h, g * tq:(g + 1) * tq, :] = jnp.full((tq, tq), NEG_BIG, F32)
                    mx_ref[per * h + g] = jnp.full((SUBLANES, tq), NEG_BIG, F32)

    def prep_stage(qi, j, first):
        i_blk = qi * tq // INPROJ_TM
        for h in range(nh):
            big_c = (crep_ref[0, 0, pl.ds(h * n_blk + i_blk, 1), :]
                     - crep_ref[0, 0, pl.ds(h * n_blk + j, 1), :])
            mx = mx_ref[per * h]
            for g in range(1, per):
                mx = jnp.maximum(mx, mx_ref[per * h + g])
            m_prev = jnp.where(first, NEG_BIG, m_ref[h])
            m_new = jnp.maximum(m_prev, jnp.max(mx, axis=0, keepdims=True) + big_c)
            mnext_ref[h] = m_new
            shift_ref[h] = m_new - big_c
            alpha_ref[h] = jnp.exp2(m_prev - m_new)

    def exp_stage(slot):
        for h in range(nh):
            p_ref[h] = jnp.exp2(s_ref[slot, h] - shift_ref[h]).astype(BF16)

    def pv_stage(j):
        k_off = pl.multiple_of(j * tk, tk)
        for h in range(nh):
            vT = jnp.concatenate([vT_ref[0, h * half:(h + 1) * half, pl.ds(k_off, tk)], sum_rows], axis=0)
            acc_ref[h] = alpha_ref[h] * acc_ref[h] + _dot(vT, p_ref[h])
            m_ref[h] = mnext_ref[h]

    def finalize(qi):
        outs = []
        for h in range(nh):
            acc = acc_ref[h]
            outs.append(acc[:half] / acc[half:half + 1])
        oT = jnp.concatenate(outs, axis=0)
        o_ref[0, pl.ds(pl.multiple_of(qi * tq, tq), tq), :] = oT.T.astype(o_ref.dtype)

    zero = jnp.int32(0)
    build_qaug(zero)
    build_qaug(zero + 1)
    reset_state()
    store_scores(qk_dots(zero, zero), 0)
    mask_stage(zero, 0)
    prep_stage(zero, zero, True)

    def body(t, carry):
        qi, j = carry
        last = j == qi * tq // tk
        qi2 = jnp.where(last, qi + 1, qi)
        j2 = jnp.where(last, jstart_ref[g * nq + jnp.minimum(qi + 1, nq - 1)], j + 1)
        def stages(rd, wr):
            exp_stage(rd)
            nxt = qk_dots(qi2, j2)
            pv_stage(j)
            store_scores(nxt, wr)
            prep_stage(qi2, j2, last)

            @pl.when(j2 == qi2 * tq // tk)
            def _():
                mask_stage(qi2, wr)
                prep_stage(qi2, j2, last)

        @pl.when(t % 2 == 0)
        def _():
            stages(0, 1)

        @pl.when(t % 2 == 1)
        def _():
            stages(1, 0)

        @pl.when(last)
        def _():
            finalize(qi)
            reset_state()

            @pl.when(qi + 2 < nq)
            def _():
                build_qaug(qi + 2)

        return qi2, j2

    n_last = nitems_ref[g] - 1
    qi, j = lax.fori_loop(0, n_last, body, (jnp.int32(0), jnp.int32(0)))
    for slot in range(2):
        @pl.when(n_last % 2 == slot)
        def _():
            exp_stage(slot)
    pv_stage(j)
    finalize(qi)


def _prune_plan(stats, drel, nq):
    B, n_blk = stats.shape[:2]
    per = FOX_TK // FOX_TQ
    c = stats[:, :, 0, :FOX_HEADS]
    qn = jnp.sqrt(stats[:, :, 1, :FOX_HEADS])
    kn = jnp.sqrt(stats[:, :, 2, :FOX_HEADS])
    c_next = jnp.concatenate([c[:, 1:], c[:, -1:]], axis=1)
    d_in = jnp.transpose(drel[:, :, FOX_TQ - 1::FOX_TQ], (0, 2, 1))
    d_in = jnp.concatenate([jnp.zeros_like(d_in[:, :1]), d_in[:, :-1]], axis=1)
    first_in_tile = (jnp.arange(nq) % per == 0)[None, :, None]
    d_before = jnp.repeat(c, per, axis=1) + jnp.where(first_in_tile, 0.0, d_in)
    qn_q = jnp.repeat(qn, per, axis=1)
    kn_q = jnp.repeat(kn, per, axis=1)
    bound = (d_before[:, :, None, :] - c_next[:, None, :, :]
             + NORM_SLACK * qn_q[:, :, None, :] * (kn[:, None, :, :] + kn_q[:, :, None, :]))
    i_idx = (jnp.arange(nq) // per)[:, None]
    j_idx = jnp.arange(n_blk)[None, :]
    skip = (bound <= PRUNE_LOG2) & (j_idx + 1 <= i_idx)[None, :, :, None]
    nh = FOX_GROUP_HEADS
    skip_pair = skip[..., 0::nh]
    for h in range(1, nh):
        skip_pair = skip_pair & skip[..., h::nh]
    jstart = jnp.min(jnp.where(skip_pair, n_blk, j_idx[None, :, :, None]), axis=2)
    jstart = jnp.transpose(jstart, (0, 2, 1))
    n_items = jnp.sum(i_idx[:, 0][None, None, :] - jstart + 1, axis=2)
    return jstart.reshape(-1).astype(jnp.int32), n_items.reshape(-1).astype(jnp.int32)


def _fox_attention(fqT, kaug, fvT, drel, stats):
    B, _, S = fqT.shape
    tq, tk = FOX_TQ, FOX_TK
    n_blk = S // INPROJ_TM
    nh = FOX_GROUP_HEADS
    n_groups = FOX_HEADS // nh
    drel4 = drel.reshape(B, n_groups, nh, S)
    c = stats[:, :, 0, :FOX_HEADS]
    c = jnp.transpose(c, (0, 2, 1)).reshape(B, n_groups, nh * n_blk, 1)
    crep = jnp.broadcast_to(c, (B, n_groups, nh * n_blk, tq))
    jstart, n_items = _prune_plan(stats, drel, S // tq)
    grid_spec = pltpu.PrefetchScalarGridSpec(
        num_scalar_prefetch=2,
        grid=(B, n_groups),
        in_specs=[
            pl.BlockSpec((1, nh * FOX_HEAD_DIM, S), lambda b, p, js, ni: (b, p, 0)),
            pl.BlockSpec((1, S, nh // 2 * MXU_DIM), lambda b, p, js, ni: (b, 0, p)),
            pl.BlockSpec((1, nh * FOX_HEAD_DIM, S), lambda b, p, js, ni: (b, p, 0)),
            pl.BlockSpec((1, 1, nh, S), lambda b, p, js, ni: (b, p, 0, 0)),
            pl.BlockSpec((1, 1, nh * n_blk, tq), lambda b, p, js, ni: (b, p, 0, 0)),
        ],
        out_specs=pl.BlockSpec((1, S, nh * FOX_HEAD_DIM), lambda b, p, js, ni: (b, 0, p)),
        scratch_shapes=[
            pltpu.VMEM((2 * nh, MXU_DIM, tq), BF16),
            pltpu.VMEM((2, nh, tk, tq), F32),
            pltpu.VMEM((nh, tk, tq), BF16),
            pltpu.VMEM((nh * (tk // tq), SUBLANES, tq), F32),
            pltpu.VMEM((nh, FOX_HEAD_DIM + FOX_SUM_ROWS, tq), F32),
            pltpu.VMEM((nh, 1, tq), F32),
            pltpu.VMEM((nh, 1, tq), F32),
            pltpu.VMEM((nh, 1, tq), F32),
            pltpu.VMEM((nh, 1, tq), F32),
        ])
    return pl.pallas_call(
        _fox_kernel,
        grid_spec=grid_spec,
        out_shape=jax.ShapeDtypeStruct((B, S, FOX_WIDTH), BF16),
        compiler_params=pltpu.CompilerParams(
            dimension_semantics=("arbitrary", "arbitrary"),
            vmem_limit_bytes=VMEM_LIMIT),
        name="fox_attn",
    )(jstart, n_items, fqT, kaug, fvT, drel4, crep)


def _gla_kernel(q_ref, k_ref, v_ref, r_ref, sm_ref, wa_ref, ba_ref, g_ref, o_ref, state_ref):
    nb, ts = q_ref.shape[0], q_ref.shape[1]
    C, P = GLA_CHUNK, GLA_PAIR
    dk, dv, H = GLA_KEY_DIM, GLA_VAL_DIM, GLA_HEADS
    KW = GLA_KEY_WIDTH

    @pl.when(pl.program_id(0) == 0)
    def _():
        state_ref[...] = jnp.zeros_like(state_ref)

    la_all = [_log_sigmoid(_dot(sm_ref[bb].astype(BF16), wa_ref[...]) + ba_ref[...])
              * (1.0 / GLA_GATE_TEMP) for bb in range(nb)]

    ri = lax.broadcasted_iota(jnp.int32, (P, P), 0)
    ci = lax.broadcasted_iota(jnp.int32, (P, P), 1)
    tril2 = jnp.where(((ri < C) == (ci < C)) & (ci <= ri), 1.0, 0.0).astype(BF16)

    rs = lax.broadcasted_iota(jnp.int32, (2 * H * C, P), 0)
    cs = lax.broadcasted_iota(jnp.int32, (2 * H * C, P), 1)
    causal = ((rs >= H * C) == (cs >= C)) & ((cs & (C - 1)) <= (rs & (C - 1)))

    lane_q = lax.broadcasted_iota(jnp.int32, (C, KW), 1)
    lane_t = lax.broadcasted_iota(jnp.int32, (KW, P), 1)
    first = lane_t < C

    per_b = ts // P
    n_slabs = nb * per_b
    slab_b = [k // per_b for k in range(n_slabs)]
    slab_rows = [slice((k % per_b) * P, (k % per_b + 1) * P) for k in range(n_slabs)]
    bs = [_tri_left(tril2, la_all[slab_b[k]][slab_rows[k]]) for k in range(n_slabs)]

    qms, scores, kv0s, kv1s, dec0s, dec1s, vss = [], [], [], [], [], [], []
    for pi in range(n_slabs):
        bb, rows = slab_b[pi], slab_rows[pi]
        b = bs[pi]
        bT = b.T
        bl0 = bT[:, C - 1:C]
        bl1 = bT[:, P - 1:P]
        q_dec = q_ref[bb, rows, :] * jnp.exp(b)
        kT = k_ref[bb, rows, :].T
        k_invT = (kT * jnp.exp(-bT)).astype(BF16)
        k_teT = kT * jnp.exp(jnp.where(first, bl0, bl1) - bT)
        zt = jnp.zeros_like(k_teT)
        k_te0 = jnp.where(first, k_teT, zt).astype(BF16)
        k_te1 = jnp.where(first, zt, k_teT).astype(BF16)
        dec0 = jnp.exp(bl0)
        dec1 = jnp.exp(bl1)

        vs = [v_ref[bb, rows, h * dv:(h + 1) * dv] for h in range(H)]
        kv = [_dot(jnp.concatenate([k_te0[h * dk:(h + 1) * dk, :], k_te1[h * dk:(h + 1) * dk, :]], axis=0),
                   vs[h]) for h in range(H)]
        kv0s.append(jnp.concatenate([kv[h][:dk] for h in range(H)], axis=0))
        kv1s.append(jnp.concatenate([kv[h][dk:] for h in range(H)], axis=0))
        dec0s.append(dec0)
        dec1s.append(dec1)
        vss.append(vs)

        zq = jnp.zeros((C, KW), F32)
        qm = [jnp.concatenate(
            [jnp.where((lane_q >= h * dk) & (lane_q < (h + 1) * dk), q_dec[c * C:(c + 1) * C], zq)
             for h in range(H)], axis=0).astype(BF16) for c in range(2)]
        a = _dot(jnp.concatenate(qm, axis=0), k_invT)
        scores.append(jnp.where(causal, a, 0.0).astype(BF16))
        qms.append(qm)

    o_inters = []
    for pi in range(n_slabs):
        if pi % per_b == 0:
            s = state_ref[slab_b[pi]]
        s1 = dec0s[pi] * s + kv0s[pi]
        o_inters.append([_dot(qms[pi][0], s.astype(BF16)), _dot(qms[pi][1], s1.astype(BF16))])
        s = dec1s[pi] * s1 + kv1s[pi]
        if pi % per_b == per_b - 1:
            state_ref[slab_b[pi]] = s

    for pi in range(n_slabs):
        bb, rows = slab_b[pi], slab_rows[pi]
        a, o_inter, vs = scores[pi], o_inters[pi], vss[pi]
        for h in range(H):
            hv = slice(h * dv, (h + 1) * dv)
            a_h = jnp.concatenate([a[(c * H + h) * C:(c * H + h + 1) * C] for c in range(2)], axis=0)
            o = _dot(a_h, vs[h]) + jnp.concatenate(
                [o_inter[c][h * C:(h + 1) * C] for c in range(2)], axis=0)
            o = o * lax.rsqrt(jnp.mean(o * o, axis=-1, keepdims=True) + RMS_EPS)
            rr = r_ref[bb, rows, hv]
            o = o * g_ref[:, hv] * (rr * (1.0 / (1.0 + jnp.exp(-rr))))
            o_ref[bb, rows, hv] = o.astype(o_ref.dtype)


def _gla_attention(gq, gk, gv, gr, small, w_a2_pad, b_a, g_gla):
    B, S, _ = gq.shape
    ts = GLA_TS
    tok = lambda w: pl.BlockSpec((B, ts, w), lambda s: (0, s, 0))
    full = lambda r, c: pl.BlockSpec((r, c), lambda s: (0, 0))
    return pl.pallas_call(
        _gla_kernel,
        grid=(S // ts,),
        in_specs=[
            tok(GLA_KEY_WIDTH), tok(GLA_KEY_WIDTH), tok(GLA_WIDTH), tok(GLA_WIDTH), tok(SMALL_W),
            full(SMALL_W, GLA_KEY_WIDTH), full(1, GLA_KEY_WIDTH), full(1, GLA_WIDTH),
        ],
        out_specs=tok(GLA_WIDTH),
        out_shape=jax.ShapeDtypeStruct((B, S, GLA_WIDTH), BF16),
        scratch_shapes=[pltpu.VMEM((B, GLA_KEY_WIDTH, GLA_VAL_DIM), F32)],
        compiler_params=pltpu.CompilerParams(
            dimension_semantics=("arbitrary",), vmem_limit_bytes=VMEM_LIMIT),
        name="gla_attn",
    )(gq, gk, gv, gr, small, w_a2_pad, b_a, g_gla)


def _layer_norm(z, g, b):
    mu = jnp.mean(z, axis=-1, keepdims=True)
    d = z - mu
    var = jnp.mean(d * d, axis=-1, keepdims=True)
    return d * lax.rsqrt(var + LN_EPS) * g + b


def _tail_kernel(x_ref, fox_ref, gla_ref, mod_ref, wof_ref, wog_ref, ln1g_ref, ln1b_ref,
                 wg_ref, wu_ref, wd_ref, ln2g_ref, ln2b_ref, o_ref):
    m = mod_ref[0]
    tm = x_ref.shape[0]
    halves = [slice(0, tm // 2), slice(tm // 2, tm)]
    ys = [_dot(fox_ref[r, :], wof_ref[...]) + _dot(gla_ref[r, :], wog_ref[...]) for r in halves]
    x1s, gs, ups = [], [], []
    for r, y in zip(halves, ys):
        x1 = _layer_norm(DEEPNORM_ALPHA * x_ref[r, :] + (1.0 + m[2:3]) * y,
                         ln1g_ref[...], ln1b_ref[...])
        u2 = (x1 * (1.0 + m[4:5]) + m[3:4]).astype(BF16)
        x1s.append(x1)
        gs.append(_dot(u2, wg_ref[...]))
        ups.append(_dot(u2, wu_ref[...]))
    y2s = []
    for g, up in zip(gs, ups):
        h = (g * (1.0 / (1.0 + jnp.exp(-g))) * up).astype(BF16)
        y2s.append(_dot(h, wd_ref[...]))
    for r, x1, y2 in zip(halves, x1s, y2s):
        o_ref[r, :] = _layer_norm(DEEPNORM_ALPHA * x1 + (1.0 + m[5:6]) * y2,
                                  ln2g_ref[...], ln2b_ref[...])


def _tail(x2d, fox2d, gla2d, mod, wo, ln1g, ln1b, wg, wu, wd, ln2g, ln2b, steps_per_batch):
    T, D = x2d.shape
    tm = TAIL_TM
    const = lambda r, c: pl.BlockSpec((r, c), lambda i: (0, 0), pipeline_mode=pl.Buffered(1))
    tok = lambda w: pl.BlockSpec((tm, w), lambda i: (i, 0))
    return pl.pallas_call(
        _tail_kernel,
        grid=(T // tm,),
        in_specs=[
            tok(D), tok(FOX_WIDTH), tok(GLA_WIDTH),
            pl.BlockSpec((1, N_MOD, D), lambda i: (i // steps_per_batch, 0, 0)),
            pl.BlockSpec((FOX_WIDTH, D), lambda i: (0, 0), pipeline_mode=pl.Buffered(1)),
            pl.BlockSpec((GLA_WIDTH, D), lambda i: (1, 0), pipeline_mode=pl.Buffered(1)),
            const(1, D), const(1, D),
            const(D, D_FF), const(D, D_FF), const(D_FF, D), const(1, D), const(1, D),
        ],
        out_specs=tok(D),
        out_shape=jax.ShapeDtypeStruct((T, D), F32),
        compiler_params=pltpu.CompilerParams(
            dimension_semantics=("arbitrary",), vmem_limit_bytes=VMEM_LIMIT),
        name="tail",
    )(x2d, fox2d, gla2d, mod, wo, wo, ln1g, ln1b, wg, wu, wd, ln2g, ln2b)


def kernel(x, c, w_c, b_c, w_in, b_f, w_a2, b_a, g_gla, w_o, ln1_g, ln1_b,
           w_gate, w_up, w_down, ln2_g, ln2_b):
    B, S, D = x.shape
    assert D == D_MODEL and S % INPROJ_TM == 0 and INPROJ_TM == FOX_TK and FOX_TK % FOX_TQ == 0

    offs = [0]
    for wdt in IN_SPLITS:
        offs.append(offs[-1] + wdt)
    seg = lambda i: w_in[:, offs[i]:offs[i + 1]]
    w_small = jnp.concatenate(
        [seg(3), seg(7), jnp.zeros((D, SMALL_W - FOX_HEADS - GLA_GATE_RANK), F32)], axis=1)
    w_all = jnp.concatenate(
        [seg(0), seg(1), seg(2), seg(4), w_small, jnp.zeros((D, SMALL_W), F32),
         seg(5), seg(6), seg(8)], axis=1).astype(BF16)
    b_f_pad = jnp.zeros((1, SMALL_W), F32).at[0, :FOX_HEADS].set(b_f)
    w_a2_pad = jnp.zeros((SMALL_W, GLA_KEY_WIDTH), F32) \
        .at[FOX_HEADS:FOX_HEADS + GLA_GATE_RANK].set(w_a2).astype(BF16)

    mod = _modulation(c, w_c, b_c)

    fqT, kaug, fvT, gq, gk, gv, gr, small, drel, stats, wg16, wu16, wd16 = _in_projection(
        x, mod, w_all, b_f_pad, w_gate, w_up, w_down)
    fox = _fox_attention(fqT, kaug, fvT, drel, stats)
    gla = _gla_attention(gq, gk, gv, gr, small, w_a2_pad,
                         b_a.reshape(1, -1), g_gla.reshape(1, -1))

    T = B * S
    w_o_b = w_o.astype(BF16)
    out = _tail(x.reshape(T, D), fox.reshape(T, FOX_WIDTH), gla.reshape(T, GLA_WIDTH), mod,
                w_o_b,
                ln1_g.reshape(1, D), ln1_b.reshape(1, D),
                wg16, wu16, wd16,
                ln2_g.reshape(1, D), ln2_b.reshape(1, D), S // TAIL_TM)
    return out.reshape(B, S, D)
```

```python
import functools
import math

import numpy as np
import jax
import jax.numpy as jnp
from jax import lax
from jax.experimental import pallas as pl
from jax.experimental.pallas import tpu as pltpu

F32 = jnp.float32
BF16 = jnp.bfloat16

D_MODEL = 1024
FOX_WIDTH = 512
FOX_HEAD_DIM = 64
FOX_HEADS = 8
FOX_PAIRS = FOX_HEADS // 2
GLA_WIDTH = 512
GLA_HEADS = 4
GLA_VAL_DIM = 128
GLA_KEY_DIM = 64
GLA_KEY_WIDTH = 256
GLA_GATE_RANK = 16
GLA_GATE_TEMP = 16.0
GLA_CHUNK = 64
D_FF = 2816
LN_EPS = 1e-5
RMS_EPS = 1e-6
N_MOD = 6
DEEPNORM_ALPHA = 2.0 ** 0.25
IN_SPLITS = (FOX_WIDTH, FOX_WIDTH, FOX_WIDTH, FOX_HEADS, GLA_KEY_WIDTH, GLA_KEY_WIDTH,
             GLA_WIDTH, GLA_GATE_RANK, GLA_WIDTH)
LOG2E = math.log2(math.e)

LANES = 128
SUBLANES = 8
MXU_DIM = 256
SMALL_W = LANES
VMEM_LIMIT = 56 * 1024 * 1024

MOD_TN = 1536
INPROJ_TM = 512
FOX_TQ = 256
FOX_GROUP_HEADS = 4
FOX_SUM_ROWS = 16
FOX_TK = 512
GLA_TS = 512
GLA_PAIR = 2 * GLA_CHUNK
TAIL_TM = 512

NEG_BIG = -1e30
PRUNE_LOG2 = -1080.0
NORM_SLACK = 1.02

AUG_ROWS = 32


def _log_sigmoid(z):
    return jnp.minimum(z, 0.0) - jnp.log(1.0 + jnp.exp(-jnp.abs(z)))


def _split3(a):
    hi = a.astype(BF16)
    r1 = a - hi.astype(F32)
    mid = r1.astype(BF16)
    lo = (r1 - mid.astype(F32)).astype(BF16)
    return hi, mid, lo


def _dot(a, b):
    return jnp.dot(a, b, preferred_element_type=F32)


def _tri_left(tri, a):
    hi, mid, lo = _split3(a)
    return _dot(tri, hi) + _dot(tri, mid) + _dot(tri, lo)


def _tri_right(a, tri):
    hi, mid, lo = _split3(a)
    return _dot(hi, tri) + _dot(mid, tri) + _dot(lo, tri)


def _mod_kernel(cT_ref, w_ref, b_ref, o_ref, *, n_rows):
    w = w_ref[...]
    outs = [jnp.sum(w * cT_ref[:, b:b + 1], axis=0, keepdims=True) for b in range(n_rows)]
    outs.append(jnp.zeros((o_ref.shape[0] - n_rows, w.shape[1]), F32))
    o_ref[...] = jnp.concatenate(outs, axis=0) + b_ref[...]


def _modulation(c, w_c, b_c):
    B, D = c.shape
    N = w_c.shape[1]
    rows = SUBLANES
    cT_pad = jnp.zeros((D, LANES), F32).at[:, :B].set(c.T)
    out = pl.pallas_call(
        functools.partial(_mod_kernel, n_rows=B),
        grid=(N // MOD_TN,),
        in_specs=[
            pl.BlockSpec((D, LANES), lambda j: (0, 0)),
            pl.BlockSpec((D, MOD_TN), lambda j: (0, j)),
            pl.BlockSpec((1, MOD_TN), lambda j: (0, j)),
        ],
        out_specs=pl.BlockSpec((rows, MOD_TN), lambda j: (0, j)),
        out_shape=jax.ShapeDtypeStruct((rows, N), F32),
        compiler_params=pltpu.CompilerParams(
            dimension_semantics=("arbitrary",), vmem_limit_bytes=VMEM_LIMIT),
        name="adaln_mod",
    )(cT_pad, w_c, b_c.reshape(1, N))
    return out[:B].reshape(B, N_MOD, D)


_C_FQ, _C_FK, _C_FV = 0, 512, 1024
_C_GQ, _C_SM, _C_GK, _C_GV, _C_GR = 1536, 1792, 2048, 2304, 2816
_W_COLS = _C_GR + GLA_WIDTH
_SRC = [sum(IN_SPLITS[:i]) for i in range(len(IN_SPLITS))]
_SRC_FF, _SRC_GA = _SRC[3], _SRC[7]
_W_GROUPS = ((_C_FQ, _SRC[0], FOX_WIDTH), (_C_FK, _SRC[1], FOX_WIDTH), (_C_FV, _SRC[2], FOX_WIDTH),
             (_C_GQ, _SRC[4], GLA_KEY_WIDTH), (_C_GK, _SRC[5], GLA_KEY_WIDTH),
             (_C_GV, _SRC[6], GLA_WIDTH), (_C_GR, _SRC[8], GLA_WIDTH))
_KAUG_W = FOX_PAIRS * MXU_DIM


def _head_indicator():
    ind = np.zeros((FOX_WIDTH, MXU_DIM), np.float32)
    for h in range(FOX_HEADS):
        ind[h * FOX_HEAD_DIM:(h + 1) * FOX_HEAD_DIM, h] = 1.0
    return ind


def _inproj_kernel(x_ref, mod_ref, w_ref, bf_ref, ind_ref, wg32_ref, wu32_ref, wd32_ref,
                   fqT_ref, kaug_ref, fvT_ref, gq_ref, gk_ref, gv_ref, gr_ref, sm_ref,
                   drel_ref, stats_ref, wg16_ref, wu16_ref, wd16_ref, carry_ref, ws_ref):
    tm = x_ref.shape[1]

    wg16_ref[...] = wg32_ref[...].astype(BF16)
    wu16_ref[...] = wu32_ref[...].astype(BF16)
    wd16_ref[...] = wd32_ref[...].astype(BF16)

    @pl.when(pl.program_id(1) == 0)
    def _():
        carry_ref[...] = jnp.zeros_like(carry_ref)

    m = mod_ref[0]
    u = (x_ref[0] * (1.0 + m[1:2]) + m[0:1]).astype(BF16)

    @pl.when((pl.program_id(0) == 0) & (pl.program_id(1) == 0))
    def _():
        rows_per = 256
        for r0 in range(0, w_ref.shape[0], rows_per):
            rs = slice(r0, r0 + rows_per)
            for dst, src, width in _W_GROUPS:
                ws_ref[rs, dst:dst + width] = w_ref[rs, src:src + width].astype(BF16)
            small_w = jnp.concatenate(
                [w_ref[rs, _SRC_FF:_SRC_FF + FOX_HEADS], w_ref[rs, _SRC_GA:_SRC_GA + GLA_GATE_RANK],
                 jnp.zeros((rows_per, 2 * SMALL_W - FOX_HEADS - GLA_GATE_RANK), F32)], axis=1)
            ws_ref[rs, _C_SM:_C_SM + 2 * SMALL_W] = small_w.astype(BF16)

    def proj(lo, width):
        return _dot(u, ws_ref[:, lo:lo + width])

    gq_small = proj(_C_GQ, 2 * GLA_KEY_WIDTH)
    gq_ref[0] = gq_small[:, :GLA_KEY_WIDTH] * (GLA_KEY_DIM ** -0.5)
    small = gq_small[:, GLA_KEY_WIDTH:GLA_KEY_WIDTH + SMALL_W]
    sm_ref[0] = small

    fq = proj(_C_FQ, FOX_WIDTH) * (FOX_HEAD_DIM ** -0.5 * LOG2E)
    fqT_ref[0] = fq.T.astype(BF16)
    fk32 = proj(_C_FK, FOX_WIDTH)
    fk = fk32.astype(BF16)
    qn2 = jnp.max(_dot((fq * fq).astype(BF16), ind_ref[...])[:, :SMALL_W], axis=0, keepdims=True)
    kn2 = jnp.max(_dot((fk32 * fk32).astype(BF16), ind_ref[...])[:, :SMALL_W], axis=0, keepdims=True)

    zT = (small + bf_ref[...]).T[:FOX_HEADS, :]
    lfT = _log_sigmoid(zT) * LOG2E
    r = lax.broadcasted_iota(jnp.int32, (tm, tm), 0)
    c = lax.broadcasted_iota(jnp.int32, (tm, tm), 1)
    triu = jnp.where(r <= c, 1.0, 0.0).astype(BF16)
    drelT = _tri_right(lfT, triu)
    drel_ref[0] = drelT

    c_before = carry_ref[...]
    srow = lax.broadcasted_iota(jnp.int32, (SUBLANES, SMALL_W), 0)
    slane = lax.broadcasted_iota(jnp.int32, (SUBLANES, SMALL_W), 1)
    total = jnp.sum(jnp.where(srow == slane, drelT[:, tm - 1:tm], 0.0), axis=0, keepdims=True)
    carry_ref[...] = c_before + total

    hi, mid, lo = _split3(drelT)
    ones3 = jnp.where(lax.broadcasted_iota(jnp.int32, (SUBLANES, tm), 0) < 3, 1.0, 0.0)
    slabT = jnp.concatenate(
        [-hi.astype(F32), -mid.astype(F32), -lo.astype(F32), ones3,
         jnp.zeros((LANES - 4 * SUBLANES, tm), F32)], axis=0)
    slab = slabT.T.astype(BF16)

    fvT_ref[0] = proj(_C_FV, FOX_WIDTH).T.astype(BF16)
    gv_ref[0] = proj(_C_GV, GLA_WIDTH).astype(BF16)
    gr_ref[0] = proj(_C_GR, GLA_WIDTH)
    gk_ref[0] = proj(_C_GK, GLA_KEY_WIDTH)

    stats_ref[0, 0] = jnp.where(srow == 0, c_before,
                                jnp.where(srow == 1, qn2, jnp.where(srow == 2, kn2, 0.0)))
    for p in range(FOX_PAIRS):
        kaug_ref[0, :, p * MXU_DIM:p * MXU_DIM + LANES] = fk[:, p * LANES:(p + 1) * LANES]
        kaug_ref[0, :, p * MXU_DIM + LANES:(p + 1) * MXU_DIM] = slab


def _in_projection(x, mod, w_in, b_f_pad, w_gate, w_up, w_down):
    B, S, D = x.shape
    tm = INPROJ_TM
    n_steps = B * (S // tm)
    steps_b = S // tm
    up_rows = D // n_steps
    down_rows = 2 * D_FF // n_steps
    assert D % n_steps == 0 and up_rows % 16 == 0 and (2 * D_FF) % n_steps == 0 and down_rows % 16 == 0
    up_spec = pl.BlockSpec((up_rows, D_FF), lambda b, s: (b * steps_b + s, 0))
    down_spec = pl.BlockSpec((down_rows, D), lambda b, s: ((b * steps_b + s) // 2, 0))
    ind = _head_indicator()
    tok = lambda w: pl.BlockSpec((1, tm, w), lambda b, s: (b, s, 0))
    tokT = lambda h: pl.BlockSpec((1, h, tm), lambda b, s: (b, 0, s))
    outs = pl.pallas_call(
        _inproj_kernel,
        grid=(B, S // tm),
        in_specs=[
            tok(D),
            pl.BlockSpec((1, N_MOD, D), lambda b, s: (b, 0, 0)),
            pl.BlockSpec(w_in.shape, lambda b, s: (0, 0), pipeline_mode=pl.Buffered(1)),
            pl.BlockSpec((1, SMALL_W), lambda b, s: (0, 0)),
            pl.BlockSpec(ind.shape, lambda b, s: (0, 0)),
            up_spec, up_spec, down_spec,
        ],
        out_specs=[
            tokT(FOX_WIDTH), tok(_KAUG_W), tokT(FOX_WIDTH),
            tok(GLA_KEY_WIDTH), tok(GLA_KEY_WIDTH), tok(GLA_WIDTH), tok(GLA_WIDTH),
            tok(SMALL_W),
            tokT(FOX_HEADS),
            pl.BlockSpec((1, 1, SUBLANES, SMALL_W), lambda b, s: (b, s, 0, 0)),
            up_spec, up_spec, down_spec,
        ],
        out_shape=[
            jax.ShapeDtypeStruct((B, FOX_WIDTH, S), BF16),
            jax.ShapeDtypeStruct((B, S, _KAUG_W), BF16),
            jax.ShapeDtypeStruct((B, FOX_WIDTH, S), BF16),
            jax.ShapeDtypeStruct((B, S, GLA_KEY_WIDTH), F32),
            jax.ShapeDtypeStruct((B, S, GLA_KEY_WIDTH), F32),
            jax.ShapeDtypeStruct((B, S, GLA_WIDTH), BF16),
            jax.ShapeDtypeStruct((B, S, GLA_WIDTH), F32),
            jax.ShapeDtypeStruct((B, S, SMALL_W), F32),
            jax.ShapeDtypeStruct((B, FOX_HEADS, S), F32),
            jax.ShapeDtypeStruct((B, S // tm, SUBLANES, SMALL_W), F32),
            jax.ShapeDtypeStruct(w_gate.shape, BF16),
            jax.ShapeDtypeStruct(w_up.shape, BF16),
            jax.ShapeDtypeStruct(w_down.shape, BF16),
        ],
        scratch_shapes=[pltpu.VMEM((1, SMALL_W), F32),
                        pltpu.VMEM((D, _W_COLS), BF16)],
        compiler_params=pltpu.CompilerParams(
            dimension_semantics=("arbitrary", "arbitrary"), vmem_limit_bytes=VMEM_LIMIT),
        name="in_proj",
    )(x, mod, w_in, b_f_pad, jnp.asarray(ind, BF16), w_gate, w_up, w_down)
    return outs


def _fox_kernel(jstart_ref, nitems_ref, qT_ref, kaug_ref, vT_ref, drel_ref, crep_ref, o_ref,
                qaug_ref, s_ref, p_ref, mx_ref, acc_ref, m_ref, mnext_ref, shift_ref, alpha_ref):
    tq, tk = FOX_TQ, FOX_TK
    half = FOX_HEAD_DIM
    S = kaug_ref.shape[1]
    nq = S // tq
    nh = FOX_GROUP_HEADS
    n_blk = crep_ref.shape[2] // nh
    g = pl.program_id(0) * pl.num_programs(1) + pl.program_id(1)

    def build_qaug(qi):
        q0 = pl.multiple_of(qi * tq, tq)
        dr = drel_ref[0, 0, :, pl.ds(q0, tq)]
        row = lax.broadcasted_iota(jnp.int32, (2 * half, tq), 0)
        r32 = lax.broadcasted_iota(jnp.int32, (AUG_ROWS, tq), 0)
        for h in range(nh):
            buf = (qi % 2) * nh + h
            qT = qT_ref[0, (h // 2) * 2 * half:(h // 2 + 1) * 2 * half, pl.ds(q0, tq)]
            own = (row < half) if h % 2 == 0 else (row >= half)
            qaug_ref[buf, 0:2 * half, :] = jnp.where(own, qT, jnp.zeros_like(qT))
            hi, mid, lo = _split3(dr[h:h + 1])
            gh = nh * pl.program_id(1) + h
            pick = (r32 == gh) | (r32 == gh + SUBLANES) | (r32 == gh + 2 * SUBLANES)
            slab = jnp.where(r32 == 24, hi.astype(F32),
                             jnp.where(r32 == 25, mid.astype(F32),
                                       jnp.where(r32 == 26, lo.astype(F32),
                                                 jnp.where(pick, 1.0, 0.0))))
            qaug_ref[buf, 2 * half:2 * half + AUG_ROWS, :] = slab.astype(BF16)
            qaug_ref[buf, 2 * half + AUG_ROWS:, :] = jnp.zeros(
                (MXU_DIM - 2 * half - AUG_ROWS, tq), BF16)

    def reset_state():
        acc_ref[...] = jnp.zeros_like(acc_ref)

    sum_rows = jnp.where(lax.broadcasted_iota(jnp.int32, (FOX_SUM_ROWS, tk), 0) == 0, 1.0, 0.0).astype(BF16)

    per = tk // tq

    def col_max8(s):
        return jnp.max(s.reshape(s.shape[0] // SUBLANES, SUBLANES, tq), axis=0)

    def qk_dots(qi, j):
        k_off = pl.multiple_of(j * tk, tk)
        kts = [kaug_ref[0, pl.ds(k_off, tk), p * MXU_DIM:(p + 1) * MXU_DIM] for p in range(nh // 2)]
        return [_dot(kts[h // 2], qaug_ref[(qi % 2) * nh + h]) for h in range(nh)]

    def store_scores(scores, slot):
        for h in range(nh):
            s_ref[slot, h] = scores[h]
            for g in range(per):
                mx_ref[per * h + g] = col_max8(scores[h][g * tq:(g + 1) * tq])

    def mask_stage(qi, slot):
        d = qi % per
        d0 = pl.multiple_of(d * tq, tq)
        kr = lax.broadcasted_iota(jnp.int32, (tq, tq), 0)
        qc = lax.broadcasted_iota(jnp.int32, (tq, tq), 1)
        for h in range(nh):
            tri = jnp.where(kr <= qc, s_ref[slot, h, pl.ds(d0, tq), :], NEG_BIG)
            s_ref[slot, h, pl.ds(d0, tq), :] = tri
            mx_ref[per * h + d] = col_max8(tri)
        for g in range(1, per):
            @pl.when(d < g)
            def _():
                for h in range(nh):
                    s_ref[slot, h, g * tq:(g + 1) * tq, :] = jnp.full((tq, tq), NEG_BIG, F32)
                    mx_ref[per * h + g] = jnp.full((SUBLANES, tq), NEG_BIG, F32)

    def prep_stage(qi, j, first):
        i_blk = qi * tq // INPROJ_TM
        for h in range(nh):
            big_c = (crep_ref[0, 0, pl.ds(h * n_blk + i_blk, 1), :]
                     - crep_ref[0, 0, pl.ds(h * n_blk + j, 1), :])
            mx = mx_ref[per * h]
            for g in range(1, per):
                mx = jnp.maximum(mx, mx_ref[per * h + g])
            m_prev = jnp.where(first, NEG_BIG, m_ref[h])
            m_new = jnp.maximum(m_prev, jnp.max(mx, axis=0, keepdims=True) + big_c)
            mnext_ref[h] = m_new
            shift_ref[h] = m_new - big_c
            alpha_ref[h] = jnp.exp2(m_prev - m_new)

    def exp_stage(slot):
        for h in range(nh):
            p_ref[h] = jnp.exp2(s_ref[slot, h] - shift_ref[h]).astype(BF16)

    def pv_stage(j):
        k_off = pl.multiple_of(j * tk, tk)
        for h in range(nh):
            vT = jnp.concatenate([vT_ref[0, h * half:(h + 1) * half, pl.ds(k_off, tk)], sum_rows], axis=0)
            acc_ref[h] = alpha_ref[h] * acc_ref[h] + _dot(vT, p_ref[h])
            m_ref[h] = mnext_ref[h]

    def finalize(qi):
        outs = []
        for h in range(nh):
            acc = acc_ref[h]
            outs.append(acc[:half] / acc[half:half + 1])
        oT = jnp.concatenate(outs, axis=0)
        o_ref[0, pl.ds(pl.multiple_of(qi * tq, tq), tq), :] = oT.T.astype(o_ref.dtype)

    zero = jnp.int32(0)
    build_qaug(zero)
    build_qaug(zero + 1)
    reset_state()
    store_scores(qk_dots(zero, zero), 0)
    mask_stage(zero, 0)
    prep_stage(zero, zero, True)

    def body(t, carry):
        qi, j = carry
        last = j == qi * tq // tk
        qi2 = jnp.where(last, qi + 1, qi)
        j2 = jnp.where(last, jstart_ref[g * nq + jnp.minimum(qi + 1, nq - 1)], j + 1)
        def stages(rd, wr):
            exp_stage(rd)
            nxt = qk_dots(qi2, j2)
            pv_stage(j)
            store_scores(nxt, wr)
            prep_stage(qi2, j2, last)

            @pl.when(j2 == qi2 * tq // tk)
            def _():
                mask_stage(qi2, wr)
                prep_stage(qi2, j2, last)

        @pl.when(t % 2 == 0)
        def _():
            stages(0, 1)

        @pl.when(t % 2 == 1)
        def _():
            stages(1, 0)

        @pl.when(last)
        def _():
            finalize(qi)
            reset_state()

            @pl.when(qi + 2 < nq)
            def _():
                build_qaug(qi + 2)

        return qi2, j2

    n_last = nitems_ref[g] - 1
    qi, j = lax.fori_loop(0, n_last, body, (jnp.int32(0), jnp.int32(0)))
    for slot in range(2):
        @pl.when(n_last % 2 == slot)
        def _():
            exp_stage(slot)
    pv_stage(j)
    finalize(qi)


def _prune_plan(stats, drel, nq):
    B, n_blk = stats.shape[:2]
    per = FOX_TK // FOX_TQ
    c = stats[:, :, 0, :FOX_HEADS]
    qn = jnp.sqrt(stats[:, :, 1, :FOX_HEADS])
    kn = jnp.sqrt(stats[:, :, 2, :FOX_HEADS])
    c_next = jnp.concatenate([c[:, 1:], c[:, -1:]], axis=1)
    d_in = jnp.transpose(drel[:, :, FOX_TQ - 1::FOX_TQ], (0, 2, 1))
    d_in = jnp.concatenate([jnp.zeros_like(d_in[:, :1]), d_in[:, :-1]], axis=1)
    first_in_tile = (jnp.arange(nq) % per == 0)[None, :, None]
    d_before = jnp.repeat(c, per, axis=1) + jnp.where(first_in_tile, 0.0, d_in)
    qn_q = jnp.repeat(qn, per, axis=1)
    kn_q = jnp.repeat(kn, per, axis=1)
    bound = (d_before[:, :, None, :] - c_next[:, None, :, :]
             + NORM_SLACK * qn_q[:, :, None, :] * (kn[:, None, :, :] + kn_q[:, :, None, :]))
    i_idx = (jnp.arange(nq) // per)[:, None]
    j_idx = jnp.arange(n_blk)[None, :]
    skip = (bound <= PRUNE_LOG2) & (j_idx + 1 <= i_idx)[None, :, :, None]
    nh = FOX_GROUP_HEADS
    skip_pair = skip[..., 0::nh]
    for h in range(1, nh):
        skip_pair = skip_pair & skip[..., h::nh]
    jstart = jnp.min(jnp.where(skip_pair, n_blk, j_idx[None, :, :, None]), axis=2)
    jstart = jnp.transpose(jstart, (0, 2, 1))
    n_items = jnp.sum(i_idx[:, 0][None, None, :] - jstart + 1, axis=2)
    return jstart.reshape(-1).astype(jnp.int32), n_items.reshape(-1).astype(jnp.int32)


def _fox_attention(fqT, kaug, fvT, drel, stats):
    B, _, S = fqT.shape
    tq, tk = FOX_TQ, FOX_TK
    n_blk = S // INPROJ_TM
    nh = FOX_GROUP_HEADS
    n_groups = FOX_HEADS // nh
    drel4 = drel.reshape(B, n_groups, nh, S)
    c = stats[:, :, 0, :FOX_HEADS]
    c = jnp.transpose(c, (0, 2, 1)).reshape(B, n_groups, nh * n_blk, 1)
    crep = jnp.broadcast_to(c, (B, n_groups, nh * n_blk, tq))
    jstart, n_items = _prune_plan(stats, drel, S // tq)
    grid_spec = pltpu.PrefetchScalarGridSpec(
        num_scalar_prefetch=2,
        grid=(B, n_groups),
        in_specs=[
            pl.BlockSpec((1, nh * FOX_HEAD_DIM, S), lambda b, p, js, ni: (b, p, 0)),
            pl.BlockSpec((1, S, nh // 2 * MXU_DIM), lambda b, p, js, ni: (b, 0, p)),
            pl.BlockSpec((1, nh * FOX_HEAD_DIM, S), lambda b, p, js, ni: (b, p, 0)),
            pl.BlockSpec((1, 1, nh, S), lambda b, p, js, ni: (b, p, 0, 0)),
            pl.BlockSpec((1, 1, nh * n_blk, tq), lambda b, p, js, ni: (b, p, 0, 0)),
        ],
        out_specs=pl.BlockSpec((1, S, nh * FOX_HEAD_DIM), lambda b, p, js, ni: (b, 0, p)),
        scratch_shapes=[
            pltpu.VMEM((2 * nh, MXU_DIM, tq), BF16),
            pltpu.VMEM((2, nh, tk, tq), F32),
            pltpu.VMEM((nh, tk, tq), BF16),
            pltpu.VMEM((nh * (tk // tq), SUBLANES, tq), F32),
            pltpu.VMEM((nh, FOX_HEAD_DIM + FOX_SUM_ROWS, tq), F32),
            pltpu.VMEM((nh, 1, tq), F32),
            pltpu.VMEM((nh, 1, tq), F32),
            pltpu.VMEM((nh, 1, tq), F32),
            pltpu.VMEM((nh, 1, tq), F32),
        ])
    return pl.pallas_call(
        _fox_kernel,
        grid_spec=grid_spec,
        out_shape=jax.ShapeDtypeStruct((B, S, FOX_WIDTH), BF16),
        compiler_params=pltpu.CompilerParams(
            dimension_semantics=("arbitrary", "arbitrary"),
            vmem_limit_bytes=VMEM_LIMIT),
        name="fox_attn",
    )(jstart, n_items, fqT, kaug, fvT, drel4, crep)


def _gla_kernel(q_ref, k_ref, v_ref, r_ref, sm_ref, wa_ref, ba_ref, g_ref, o_ref, state_ref):
    nb, ts = q_ref.shape[0], q_ref.shape[1]
    C, P = GLA_CHUNK, GLA_PAIR
    dk, dv, H = GLA_KEY_DIM, GLA_VAL_DIM, GLA_HEADS
    KW = GLA_KEY_WIDTH

    @pl.when(pl.program_id(0) == 0)
    def _():
        state_ref[...] = jnp.zeros_like(state_ref)

    la_all = [_log_sigmoid(_dot(sm_ref[bb].astype(BF16), wa_ref[...]) + ba_ref[...])
              * (1.0 / GLA_GATE_TEMP) for bb in range(nb)]

    ri = lax.broadcasted_iota(jnp.int32, (P, P), 0)
    ci = lax.broadcasted_iota(jnp.int32, (P, P), 1)
    tril2 = jnp.where(((ri < C) == (ci < C)) & (ci <= ri), 1.0, 0.0).astype(BF16)

    rs = lax.broadcasted_iota(jnp.int32, (2 * H * C, P), 0)
    cs = lax.broadcasted_iota(jnp.int32, (2 * H * C, P), 1)
    causal = ((rs >= H * C) == (cs >= C)) & ((cs & (C - 1)) <= (rs & (C - 1)))

    lane_q = lax.broadcasted_iota(jnp.int32, (C, KW), 1)
    lane_t = lax.broadcasted_iota(jnp.int32, (KW, P), 1)
    first = lane_t < C

    per_b = ts // P
    n_slabs = nb * per_b
    slab_b = [k // per_b for k in range(n_slabs)]
    slab_rows = [slice((k % per_b) * P, (k % per_b + 1) * P) for k in range(n_slabs)]
    bs = [_tri_left(tril2, la_all[slab_b[k]][slab_rows[k]]) for k in range(n_slabs)]

    qms, scores, kv0s, kv1s, dec0s, dec1s, vss = [], [], [], [], [], [], []
    for pi in range(n_slabs):
        bb, rows = slab_b[pi], slab_rows[pi]
        b = bs[pi]
        bT = b.T
        bl0 = bT[:, C - 1:C]
        bl1 = bT[:, P - 1:P]
        q_dec = q_ref[bb, rows, :] * jnp.exp(b)
        kT = k_ref[bb, rows, :].T
        k_invT = (kT * jnp.exp(-bT)).astype(BF16)
        k_teT = kT * jnp.exp(jnp.where(first, bl0, bl1) - bT)
        zt = jnp.zeros_like(k_teT)
        k_te0 = jnp.where(first, k_teT, zt).astype(BF16)
        k_te1 = jnp.where(first, zt, k_teT).astype(BF16)
        dec0 = jnp.exp(bl0)
        dec1 = jnp.exp(bl1)

        vs = [v_ref[bb, rows, h * dv:(h + 1) * dv] for h in range(H)]
        kv = [_dot(jnp.concatenate([k_te0[h * dk:(h + 1) * dk, :], k_te1[h * dk:(h + 1) * dk, :]], axis=0),
                   vs[h]) for h in range(H)]
        kv0s.append(jnp.concatenate([kv[h][:dk] for h in range(H)], axis=0))
        kv1s.append(jnp.concatenate([kv[h][dk:] for h in range(H)], axis=0))
        dec0s.append(dec0)
        dec1s.append(dec1)
        vss.append(vs)

        zq = jnp.zeros((C, KW), F32)
        qm = [jnp.concatenate(
            [jnp.where((lane_q >= h * dk) & (lane_q < (h + 1) * dk), q_dec[c * C:(c + 1) * C], zq)
             for h in range(H)], axis=0).astype(BF16) for c in range(2)]
        a = _dot(jnp.concatenate(qm, axis=0), k_invT)
        scores.append(jnp.where(causal, a, 0.0).astype(BF16))
        qms.append(qm)

    o_inters = []
    for pi in range(n_slabs):
        if pi % per_b == 0:
            s = state_ref[slab_b[pi]]
        s1 = dec0s[pi] * s + kv0s[pi]
        o_inters.append([_dot(qms[pi][0], s.astype(BF16)), _dot(qms[pi][1], s1.astype(BF16))])
        s = dec1s[pi] * s1 + kv1s[pi]
        if pi % per_b == per_b - 1:
            state_ref[slab_b[pi]] = s

    for pi in range(n_slabs):
        bb, rows = slab_b[pi], slab_rows[pi]
        a, o_inter, vs = scores[pi], o_inters[pi], vss[pi]
        for h in range(H):
            hv = slice(h * dv, (h + 1) * dv)
            a_h = jnp.concatenate([a[(c * H + h) * C:(c * H + h + 1) * C] for c in range(2)], axis=0)
            o = _dot(a_h, vs[h]) + jnp.concatenate(
                [o_inter[c][h * C:(h + 1) * C] for c in range(2)], axis=0)
            o = o * lax.rsqrt(jnp.mean(o * o, axis=-1, keepdims=True) + RMS_EPS)
            rr = r_ref[bb, rows, hv]
            o = o * g_ref[:, hv] * (rr * (1.0 / (1.0 + jnp.exp(-rr))))
            o_ref[bb, rows, hv] = o.astype(o_ref.dtype)


def _gla_attention(gq, gk, gv, gr, small, w_a2_pad, b_a, g_gla):
    B, S, _ = gq.shape
    ts = GLA_TS
    tok = lambda w: pl.BlockSpec((B, ts, w), lambda s: (0, s, 0))
    full = lambda r, c: pl.BlockSpec((r, c), lambda s: (0, 0))
    return pl.pallas_call(
        _gla_kernel,
        grid=(S // ts,),
        in_specs=[
            tok(GLA_KEY_WIDTH), tok(GLA_KEY_WIDTH), tok(GLA_WIDTH), tok(GLA_WIDTH), tok(SMALL_W),
            full(SMALL_W, GLA_KEY_WIDTH), full(1, GLA_KEY_WIDTH), full(1, GLA_WIDTH),
        ],
        out_specs=tok(GLA_WIDTH),
        out_shape=jax.ShapeDtypeStruct((B, S, GLA_WIDTH), BF16),
        scratch_shapes=[pltpu.VMEM((B, GLA_KEY_WIDTH, GLA_VAL_DIM), F32)],
        compiler_params=pltpu.CompilerParams(
            dimension_semantics=("arbitrary",), vmem_limit_bytes=VMEM_LIMIT),
        name="gla_attn",
    )(gq, gk, gv, gr, small, w_a2_pad, b_a, g_gla)


def _layer_norm(z, g, b):
    mu = jnp.mean(z, axis=-1, keepdims=True)
    d = z - mu
    var = jnp.mean(d * d, axis=-1, keepdims=True)
    return d * lax.rsqrt(var + LN_EPS) * g + b


def _tail_kernel(x_ref, fox_ref, gla_ref, mod_ref, wof_ref, wog_ref, ln1g_ref, ln1b_ref,
                 wg_ref, wu_ref, wd_ref, ln2g_ref, ln2b_ref, o_ref):
    m = mod_ref[0]
    tm = x_ref.shape[0]
    halves = [slice(0, tm // 2), slice(tm // 2, tm)]
    ys = [_dot(fox_ref[r, :], wof_ref[...]) + _dot(gla_ref[r, :], wog_ref[...]) for r in halves]
    x1s, gs, ups = [], [], []
    for r, y in zip(halves, ys):
        x1 = _layer_norm(DEEPNORM_ALPHA * x_ref[r, :] + (1.0 + m[2:3]) * y,
                         ln1g_ref[...], ln1b_ref[...])
        u2 = (x1 * (1.0 + m[4:5]) + m[3:4]).astype(BF16)
        x1s.append(x1)
        gs.append(_dot(u2, wg_ref[...]))
        ups.append(_dot(u2, wu_ref[...]))
    y2s = []
    for g, up in zip(gs, ups):
        h = (g * (1.0 / (1.0 + jnp.exp(-g))) * up).astype(BF16)
        y2s.append(_dot(h, wd_ref[...]))
    for r, x1, y2 in zip(halves, x1s, y2s):
        o_ref[r, :] = _layer_norm(DEEPNORM_ALPHA * x1 + (1.0 + m[5:6]) * y2,
                                  ln2g_ref[...], ln2b_ref[...])


def _tail(x2d, fox2d, gla2d, mod, wo, ln1g, ln1b, wg, wu, wd, ln2g, ln2b, steps_per_batch):
    T, D = x2d.shape
    tm = TAIL_TM
    const = lambda r, c: pl.BlockSpec((r, c), lambda i: (0, 0), pipeline_mode=pl.Buffered(1))
    tok = lambda w: pl.BlockSpec((tm, w), lambda i: (i, 0))
    return pl.pallas_call(
        _tail_kernel,
        grid=(T // tm,),
        in_specs=[
            tok(D), tok(FOX_WIDTH), tok(GLA_WIDTH),
            pl.BlockSpec((1, N_MOD, D), lambda i: (i // steps_per_batch, 0, 0)),
            pl.BlockSpec((FOX_WIDTH, D), lambda i: (0, 0), pipeline_mode=pl.Buffered(1)),
            pl.BlockSpec((GLA_WIDTH, D), lambda i: (1, 0), pipeline_mode=pl.Buffered(1)),
            const(1, D), const(1, D),
            const(D, D_FF), const(D, D_FF), const(D_FF, D), const(1, D), const(1, D),
        ],
        out_specs=tok(D),
        out_shape=jax.ShapeDtypeStruct((T, D), F32),
        compiler_params=pltpu.CompilerParams(
            dimension_semantics=("arbitrary",), vmem_limit_bytes=VMEM_LIMIT),
        name="tail",
    )(x2d, fox2d, gla2d, mod, wo, wo, ln1g, ln1b, wg, wu, wd, ln2g, ln2b)


def kernel(x, c, w_c, b_c, w_in, b_f, w_a2, b_a, g_gla, w_o, ln1_g, ln1_b,
           w_gate, w_up, w_down, ln2_g, ln2_b):
    B, S, D = x.shape
    assert D == D_MODEL and S % INPROJ_TM == 0 and INPROJ_TM == FOX_TK and FOX_TK % FOX_TQ == 0

    b_f_pad = jnp.zeros((1, SMALL_W), F32).at[0, :FOX_HEADS].set(b_f)
    w_a2_pad = jnp.zeros((SMALL_W, GLA_KEY_WIDTH), F32) \
        .at[FOX_HEADS:FOX_HEADS + GLA_GATE_RANK].set(w_a2).astype(BF16)

    mod = _modulation(c, w_c, b_c)

    fqT, kaug, fvT, gq, gk, gv, gr, small, drel, stats, wg16, wu16, wd16 = _in_projection(
        x, mod, w_in, b_f_pad, w_gate, w_up, w_down)
    fox = _fox_attention(fqT, kaug, fvT, drel, stats)
    gla = _gla_attention(gq, gk, gv, gr, small, w_a2_pad,
                         b_a.reshape(1, -1), g_gla.reshape(1, -1))

    T = B * S
    w_o_b = w_o.astype(BF16)
    out = _tail(x.reshape(T, D), fox.reshape(T, FOX_WIDTH), gla.reshape(T, GLA_WIDTH), mod,
                w_o_b,
                ln1_g.reshape(1, D), ln1_b.reshape(1, D),
                wg16, wu16, wd16,
                ln2_g.reshape(1, D), ln2_b.reshape(1, D), S // TAIL_TM)
    return out.reshape(B, S, D)
```

```python
import functools
import math

import numpy as np
import jax
import jax.numpy as jnp
from jax import lax
from jax.experimental import pallas as pl
from jax.experimental.pallas import tpu as pltpu

F32 = jnp.float32
BF16 = jnp.bfloat16

D_MODEL = 1024
FOX_WIDTH = 512
FOX_HEAD_DIM = 64
FOX_HEADS = 8
FOX_PAIRS = FOX_HEADS // 2
GLA_WIDTH = 512
GLA_HEADS = 4
GLA_VAL_DIM = 128
GLA_KEY_DIM = 64
GLA_KEY_WIDTH = 256
GLA_GATE_RANK = 16
GLA_GATE_TEMP = 16.0
GLA_CHUNK = 64
D_FF = 2816
LN_EPS = 1e-5
RMS_EPS = 1e-6
N_MOD = 6
DEEPNORM_ALPHA = 2.0 ** 0.25
IN_SPLITS = (FOX_WIDTH, FOX_WIDTH, FOX_WIDTH, FOX_HEADS, GLA_KEY_WIDTH, GLA_KEY_WIDTH,
             GLA_WIDTH, GLA_GATE_RANK, GLA_WIDTH)
LOG2E = math.log2(math.e)

LANES = 128
SUBLANES = 8
MXU_DIM = 256
SMALL_W = LANES
VMEM_LIMIT = 56 * 1024 * 1024

MOD_TN = 1536
INPROJ_TM = 512
FOX_TQ = 256
FOX_GROUP_HEADS = 4
FOX_SUM_ROWS = 16
FOX_TK = 512
GLA_TS = 512
GLA_PAIR = 2 * GLA_CHUNK
TAIL_TM = 512

NEG_BIG = -1e30
PRUNE_LOG2 = -1080.0
NORM_SLACK = 1.02

AUG_ROWS = 32


def _log_sigmoid(z):
    return jnp.minimum(z, 0.0) - jnp.log(1.0 + jnp.exp(-jnp.abs(z)))


def _split3(a):
    hi = a.astype(BF16)
    r1 = a - hi.astype(F32)
    mid = r1.astype(BF16)
    lo = (r1 - mid.astype(F32)).astype(BF16)
    return hi, mid, lo


def _dot(a, b):
    return jnp.dot(a, b, preferred_element_type=F32)


def _tri_left(tri, a):
    hi, mid, lo = _split3(a)
    return _dot(tri, hi) + _dot(tri, mid) + _dot(tri, lo)


def _tri_right(a, tri):
    hi, mid, lo = _split3(a)
    return _dot(hi, tri) + _dot(mid, tri) + _dot(lo, tri)


def _mod_kernel(cT_ref, w_ref, b_ref, o_ref, *, n_rows):
    w = w_ref[...]
    outs = [jnp.sum(w * cT_ref[:, b:b + 1], axis=0, keepdims=True) for b in range(n_rows)]
    outs.append(jnp.zeros((o_ref.shape[0] - n_rows, w.shape[1]), F32))
    o_ref[...] = jnp.concatenate(outs, axis=0) + b_ref[...]


def _modulation(c, w_c, b_c):
    B, D = c.shape
    N = w_c.shape[1]
    rows = SUBLANES
    cT_pad = jnp.zeros((D, LANES), F32).at[:, :B].set(c.T)
    out = pl.pallas_call(
        functools.partial(_mod_kernel, n_rows=B),
        grid=(N // MOD_TN,),
        in_specs=[
            pl.BlockSpec((D, LANES), lambda j: (0, 0)),
            pl.BlockSpec((D, MOD_TN), lambda j: (0, j)),
            pl.BlockSpec((1, MOD_TN), lambda j: (0, j)),
        ],
        out_specs=pl.BlockSpec((rows, MOD_TN), lambda j: (0, j)),
        out_shape=jax.ShapeDtypeStruct((rows, N), F32),
        compiler_params=pltpu.CompilerParams(
            dimension_semantics=("arbitrary",), vmem_limit_bytes=VMEM_LIMIT),
        name="adaln_mod",
    )(cT_pad, w_c, b_c.reshape(1, N))
    return out[:B].reshape(B, N_MOD, D)


_C_FQ, _C_FK, _C_FV = 0, 512, 1024
_C_GQ, _C_SM, _C_GK, _C_GV, _C_GR = 1536, 1792, 2048, 2304, 2816
_W_COLS = _C_GR + GLA_WIDTH
_SRC = [sum(IN_SPLITS[:i]) for i in range(len(IN_SPLITS))]
_SRC_FF, _SRC_GA = _SRC[3], _SRC[7]
_W_GROUPS = ((_C_FQ, _SRC[0], FOX_WIDTH), (_C_FK, _SRC[1], FOX_WIDTH), (_C_FV, _SRC[2], FOX_WIDTH),
             (_C_GQ, _SRC[4], GLA_KEY_WIDTH), (_C_GK, _SRC[5], GLA_KEY_WIDTH),
             (_C_GV, _SRC[6], GLA_WIDTH), (_C_GR, _SRC[8], GLA_WIDTH))
_KAUG_W = FOX_PAIRS * MXU_DIM


def _head_indicator():
    ind = np.zeros((FOX_WIDTH, MXU_DIM), np.float32)
    for h in range(FOX_HEADS):
        ind[h * FOX_HEAD_DIM:(h + 1) * FOX_HEAD_DIM, h] = 1.0
    return ind


def _inproj_kernel(x_ref, mod_ref, w_ref, bf_ref, ind_ref, wg32_ref, wu32_ref, wd32_ref,
                   fqT_ref, kaug_ref, fvT_ref, gq_ref, gk_ref, gv_ref, gr_ref, sm_ref,
                   drel_ref, stats_ref, wg16_ref, wu16_ref, wd16_ref, carry_ref, ws_ref, w32_ref, wsem_ref):
    tm = x_ref.shape[1]

    wg16_ref[...] = wg32_ref[...].astype(BF16)
    wu16_ref[...] = wu32_ref[...].astype(BF16)
    wd16_ref[...] = wd32_ref[...].astype(BF16)

    @pl.when(pl.program_id(1) == 0)
    def _():
        carry_ref[...] = jnp.zeros_like(carry_ref)

    m = mod_ref[0]
    u = (x_ref[0] * (1.0 + m[1:2]) + m[0:1]).astype(BF16)

    @pl.when((pl.program_id(0) == 0) & (pl.program_id(1) == 0))
    def _():
        copy = pltpu.make_async_copy(w_ref, w32_ref, wsem_ref)
        copy.start()
        copy.wait()
        rows_per = 256
        for r0 in range(0, w32_ref.shape[0], rows_per):
            rs = slice(r0, r0 + rows_per)
            for dst, src, width in _W_GROUPS:
                ws_ref[rs, dst:dst + width] = w32_ref[rs, src:src + width].astype(BF16)
            small_w = jnp.concatenate(
                [w32_ref[rs, _SRC_FF:_SRC_FF + FOX_HEADS], w32_ref[rs, _SRC_GA:_SRC_GA + GLA_GATE_RANK],
                 jnp.zeros((rows_per, 2 * SMALL_W - FOX_HEADS - GLA_GATE_RANK), F32)], axis=1)
            ws_ref[rs, _C_SM:_C_SM + 2 * SMALL_W] = small_w.astype(BF16)

    def proj(lo, width):
        return _dot(u, ws_ref[:, lo:lo + width])

    gq_small = proj(_C_GQ, 2 * GLA_KEY_WIDTH)
    gq_ref[0] = gq_small[:, :GLA_KEY_WIDTH] * (GLA_KEY_DIM ** -0.5)
    small = gq_small[:, GLA_KEY_WIDTH:GLA_KEY_WIDTH + SMALL_W]
    sm_ref[0] = small

    fq = proj(_C_FQ, FOX_WIDTH) * (FOX_HEAD_DIM ** -0.5 * LOG2E)
    fqT_ref[0] = fq.T.astype(BF16)
    fk32 = proj(_C_FK, FOX_WIDTH)
    fk = fk32.astype(BF16)
    qn2 = jnp.max(_dot((fq * fq).astype(BF16), ind_ref[...])[:, :SMALL_W], axis=0, keepdims=True)
    kn2 = jnp.max(_dot((fk32 * fk32).astype(BF16), ind_ref[...])[:, :SMALL_W], axis=0, keepdims=True)

    zT = (small + bf_ref[...]).T[:FOX_HEADS, :]
    lfT = _log_sigmoid(zT) * LOG2E
    r = lax.broadcasted_iota(jnp.int32, (tm, tm), 0)
    c = lax.broadcasted_iota(jnp.int32, (tm, tm), 1)
    triu = jnp.where(r <= c, 1.0, 0.0).astype(BF16)
    drelT = _tri_right(lfT, triu)
    drel_ref[0] = drelT

    c_before = carry_ref[...]
    srow = lax.broadcasted_iota(jnp.int32, (SUBLANES, SMALL_W), 0)
    slane = lax.broadcasted_iota(jnp.int32, (SUBLANES, SMALL_W), 1)
    total = jnp.sum(jnp.where(srow == slane, drelT[:, tm - 1:tm], 0.0), axis=0, keepdims=True)
    carry_ref[...] = c_before + total

    hi, mid, lo = _split3(drelT)
    ones3 = jnp.where(lax.broadcasted_iota(jnp.int32, (SUBLANES, tm), 0) < 3, 1.0, 0.0)
    slabT = jnp.concatenate(
        [-hi.astype(F32), -mid.astype(F32), -lo.astype(F32), ones3,
         jnp.zeros((LANES - 4 * SUBLANES, tm), F32)], axis=0)
    slab = slabT.T.astype(BF16)

    fvT_ref[0] = proj(_C_FV, FOX_WIDTH).T.astype(BF16)
    gv_ref[0] = proj(_C_GV, GLA_WIDTH).astype(BF16)
    gr_ref[0] = proj(_C_GR, GLA_WIDTH)
    gk_ref[0] = proj(_C_GK, GLA_KEY_WIDTH)

    stats_ref[0, 0] = jnp.where(srow == 0, c_before,
                                jnp.where(srow == 1, qn2, jnp.where(srow == 2, kn2, 0.0)))
    for p in range(FOX_PAIRS):
        kaug_ref[0, :, p * MXU_DIM:p * MXU_DIM + LANES] = fk[:, p * LANES:(p + 1) * LANES]
        kaug_ref[0, :, p * MXU_DIM + LANES:(p + 1) * MXU_DIM] = slab


def _in_projection(x, mod, w_in, b_f_pad, w_gate, w_up, w_down):
    B, S, D = x.shape
    tm = INPROJ_TM
    n_steps = B * (S // tm)
    steps_b = S // tm
    up_rows = D // n_steps
    down_rows = 2 * D_FF // n_steps
    assert D % n_steps == 0 and up_rows % 16 == 0 and (2 * D_FF) % n_steps == 0 and down_rows % 16 == 0
    up_spec = pl.BlockSpec((up_rows, D_FF), lambda b, s: (b * steps_b + s, 0))
    down_spec = pl.BlockSpec((down_rows, D), lambda b, s: ((b * steps_b + s) // 2, 0))
    ind = _head_indicator()
    tok = lambda w: pl.BlockSpec((1, tm, w), lambda b, s: (b, s, 0))
    tokT = lambda h: pl.BlockSpec((1, h, tm), lambda b, s: (b, 0, s))
    outs = pl.pallas_call(
        _inproj_kernel,
        grid=(B, S // tm),
        in_specs=[
            tok(D),
            pl.BlockSpec((1, N_MOD, D), lambda b, s: (b, 0, 0)),
            pl.BlockSpec(memory_space=pl.ANY),
            pl.BlockSpec((1, SMALL_W), lambda b, s: (0, 0)),
            pl.BlockSpec(ind.shape, lambda b, s: (0, 0)),
            up_spec, up_spec, down_spec,
        ],
        out_specs=[
            tokT(FOX_WIDTH), tok(_KAUG_W), tokT(FOX_WIDTH),
            tok(GLA_KEY_WIDTH), tok(GLA_KEY_WIDTH), tok(GLA_WIDTH), tok(GLA_WIDTH),
            tok(SMALL_W),
            tokT(FOX_HEADS),
            pl.BlockSpec((1, 1, SUBLANES, SMALL_W), lambda b, s: (b, s, 0, 0)),
            up_spec, up_spec, down_spec,
        ],
        out_shape=[
            jax.ShapeDtypeStruct((B, FOX_WIDTH, S), BF16),
            jax.ShapeDtypeStruct((B, S, _KAUG_W), BF16),
            jax.ShapeDtypeStruct((B, FOX_WIDTH, S), BF16),
            jax.ShapeDtypeStruct((B, S, GLA_KEY_WIDTH), F32),
            jax.ShapeDtypeStruct((B, S, GLA_KEY_WIDTH), F32),
            jax.ShapeDtypeStruct((B, S, GLA_WIDTH), BF16),
            jax.ShapeDtypeStruct((B, S, GLA_WIDTH), F32),
            jax.ShapeDtypeStruct((B, S, SMALL_W), F32),
            jax.ShapeDtypeStruct((B, FOX_HEADS, S), F32),
            jax.ShapeDtypeStruct((B, S // tm, SUBLANES, SMALL_W), F32),
            jax.ShapeDtypeStruct(w_gate.shape, BF16),
            jax.ShapeDtypeStruct(w_up.shape, BF16),
            jax.ShapeDtypeStruct(w_down.shape, BF16),
        ],
        scratch_shapes=[pltpu.VMEM((1, SMALL_W), F32),
                        pltpu.VMEM((D, _W_COLS), BF16),
                        pltpu.VMEM(w_in.shape, F32),
                        pltpu.SemaphoreType.DMA(())],
        compiler_params=pltpu.CompilerParams(
            dimension_semantics=("arbitrary", "arbitrary"), vmem_limit_bytes=VMEM_LIMIT),
        name="in_proj",
    )(x, mod, w_in, b_f_pad, jnp.asarray(ind, BF16), w_gate, w_up, w_down)
    return outs


def _fox_kernel(jstart_ref, nitems_ref, qT_ref, kaug_ref, vT_ref, drel_ref, crep_ref, o_ref,
                qaug_ref, s_ref, p_ref, mx_ref, acc_ref, m_ref, mnext_ref, shift_ref, alpha_ref):
    tq, tk = FOX_TQ, FOX_TK
    half = FOX_HEAD_DIM
    S = kaug_ref.shape[1]
    nq = S // tq
    nh = FOX_GROUP_HEADS
    n_blk = crep_ref.shape[2] // nh
    g = pl.program_id(0) * pl.num_programs(1) + pl.program_id(1)

    def build_qaug(qi):
        q0 = pl.multiple_of(qi * tq, tq)
        dr = drel_ref[0, 0, :, pl.ds(q0, tq)]
        row = lax.broadcasted_iota(jnp.int32, (2 * half, tq), 0)
        r32 = lax.broadcasted_iota(jnp.int32, (AUG_ROWS, tq), 0)
        for h in range(nh):
            buf = (qi % 2) * nh + h
            qT = qT_ref[0, (h // 2) * 2 * half:(h // 2 + 1) * 2 * half, pl.ds(q0, tq)]
            own = (row < half) if h % 2 == 0 else (row >= half)
            qaug_ref[buf, 0:2 * half, :] = jnp.where(own, qT, jnp.zeros_like(qT))
            hi, mid, lo = _split3(dr[h:h + 1])
            gh = nh * pl.program_id(1) + h
            pick = (r32 == gh) | (r32 == gh + SUBLANES) | (r32 == gh + 2 * SUBLANES)
            slab = jnp.where(r32 == 24, hi.astype(F32),
                             jnp.where(r32 == 25, mid.astype(F32),
                                       jnp.where(r32 == 26, lo.astype(F32),
                                                 jnp.where(pick, 1.0, 0.0))))
            qaug_ref[buf, 2 * half:2 * half + AUG_ROWS, :] = slab.astype(BF16)
            qaug_ref[buf, 2 * half + AUG_ROWS:, :] = jnp.zeros(
                (MXU_DIM - 2 * half - AUG_ROWS, tq), BF16)

    def reset_state():
        acc_ref[...] = jnp.zeros_like(acc_ref)

    sum_rows = jnp.where(lax.broadcasted_iota(jnp.int32, (FOX_SUM_ROWS, tk), 0) == 0, 1.0, 0.0).astype(BF16)

    per = tk // tq

    def col_max8(s):
        return jnp.max(s.reshape(s.shape[0] // SUBLANES, SUBLANES, tq), axis=0)

    def qk_dots(qi, j):
        k_off = pl.multiple_of(j * tk, tk)
        kts = [kaug_ref[0, pl.ds(k_off, tk), p * MXU_DIM:(p + 1) * MXU_DIM] for p in range(nh // 2)]
        return [_dot(kts[h // 2], qaug_ref[(qi % 2) * nh + h]) for h in range(nh)]

    def store_scores(scores, slot):
        for h in range(nh):
            s_ref[slot, h] = scores[h]
            for g in range(per):
                mx_ref[per * h + g] = col_max8(scores[h][g * tq:(g + 1) * tq])

    def mask_stage(qi, slot):
        d = qi % per
        d0 = pl.multiple_of(d * tq, tq)
        kr = lax.broadcasted_iota(jnp.int32, (tq, tq), 0)
        qc = lax.broadcasted_iota(jnp.int32, (tq, tq), 1)
        for h in range(nh):
            tri = jnp.where(kr <= qc, s_ref[slot, h, pl.ds(d0, tq), :], NEG_BIG)
            s_ref[slot, h, pl.ds(d0, tq), :] = tri
            mx_ref[per * h + d] = col_max8(tri)
        for g in range(1, per):
            @pl.when(d < g)
            def _():
                for h in range(nh):
                    s_ref[slot, h, g * tq:(g + 1) * tq, :] = jnp.full((tq, tq), NEG_BIG, F32)
                    mx_ref[per * h + g] = jnp.full((SUBLANES, tq), NEG_BIG, F32)

    def prep_stage(qi, j, first):
        i_blk = qi * tq // INPROJ_TM
        for h in range(nh):
            big_c = (crep_ref[0, 0, pl.ds(h * n_blk + i_blk, 1), :]
                     - crep_ref[0, 0, pl.ds(h * n_blk + j, 1), :])
            mx = mx_ref[per * h]
            for g in range(1, per):
                mx = jnp.maximum(mx, mx_ref[per * h + g])
            m_prev = jnp.where(first, NEG_BIG, m_ref[h])
            m_new = jnp.maximum(m_prev, jnp.max(mx, axis=0, keepdims=True) + big_c)
            mnext_ref[h] = m_new
            shift_ref[h] = m_new - big_c
            alpha_ref[h] = jnp.exp2(m_prev - m_new)

    def exp_stage(slot):
        for h in range(nh):
            p_ref[h] = jnp.exp2(s_ref[slot, h] - shift_ref[h]).astype(BF16)

    def pv_stage(j):
        k_off = pl.multiple_of(j * tk, tk)
        for h in range(nh):
            vT = jnp.concatenate([vT_ref[0, h * half:(h + 1) * half, pl.ds(k_off, tk)], sum_rows], axis=0)
            acc_ref[h] = alpha_ref[h] * acc_ref[h] + _dot(vT, p_ref[h])
            m_ref[h] = mnext_ref[h]

    def finalize(qi):
        outs = []
        for h in range(nh):
            acc = acc_ref[h]
            outs.append(acc[:half] / acc[half:half + 1])
        oT = jnp.concatenate(outs, axis=0)
        o_ref[0, pl.ds(pl.multiple_of(qi * tq, tq), tq), :] = oT.T.astype(o_ref.dtype)

    zero = jnp.int32(0)
    build_qaug(zero)
    build_qaug(zero + 1)
    reset_state()
    store_scores(qk_dots(zero, zero), 0)
    mask_stage(zero, 0)
    prep_stage(zero, zero, True)

    def body(t, carry):
        qi, j = carry
        last = j == qi * tq // tk
        qi2 = jnp.where(last, qi + 1, qi)
        j2 = jnp.where(last, jstart_ref[g * nq + jnp.minimum(qi + 1, nq - 1)], j + 1)
        def stages(rd, wr):
            exp_stage(rd)
            nxt = qk_dots(qi2, j2)
            pv_stage(j)
            store_scores(nxt, wr)
            prep_stage(qi2, j2, last)

            @pl.when(j2 == qi2 * tq // tk)
            def _():
                mask_stage(qi2, wr)
                prep_stage(qi2, j2, last)

        @pl.when(t % 2 == 0)
        def _():
            stages(0, 1)

        @pl.when(t % 2 == 1)
        def _():
            stages(1, 0)

        @pl.when(last)
        def _():
            finalize(qi)
            reset_state()

            @pl.when(qi + 2 < nq)
            def _():
                build_qaug(qi + 2)

        return qi2, j2

    n_last = nitems_ref[g] - 1
    qi, j = lax.fori_loop(0, n_last, body, (jnp.int32(0), jnp.int32(0)))
    for slot in range(2):
        @pl.when(n_last % 2 == slot)
        def _():
            exp_stage(slot)
    pv_stage(j)
    finalize(qi)


def _prune_plan(stats, drel, nq):
    B, n_blk = stats.shape[:2]
    per = FOX_TK // FOX_TQ
    c = stats[:, :, 0, :FOX_HEADS]
    qn = jnp.sqrt(stats[:, :, 1, :FOX_HEADS])
    kn = jnp.sqrt(stats[:, :, 2, :FOX_HEADS])
    c_next = jnp.concatenate([c[:, 1:], c[:, -1:]], axis=1)
    d_in = jnp.transpose(drel[:, :, FOX_TQ - 1::FOX_TQ], (0, 2, 1))
    d_in = jnp.concatenate([jnp.zeros_like(d_in[:, :1]), d_in[:, :-1]], axis=1)
    first_in_tile = (jnp.arange(nq) % per == 0)[None, :, None]
    d_before = jnp.repeat(c, per, axis=1) + jnp.where(first_in_tile, 0.0, d_in)
    qn_q = jnp.repeat(qn, per, axis=1)
    kn_q = jnp.repeat(kn, per, axis=1)
    bound = (d_before[:, :, None, :] - c_next[:, None, :, :]
             + NORM_SLACK * qn_q[:, :, None, :] * (kn[:, None, :, :] + kn_q[:, :, None, :]))
    i_idx = (jnp.arange(nq) // per)[:, None]
    j_idx = jnp.arange(n_blk)[None, :]
    skip = (bound <= PRUNE_LOG2) & (j_idx + 1 <= i_idx)[None, :, :, None]
    nh = FOX_GROUP_HEADS
    skip_pair = skip[..., 0::nh]
    for h in range(1, nh):
        skip_pair = skip_pair & skip[..., h::nh]
    jstart = jnp.min(jnp.where(skip_pair, n_blk, j_idx[None, :, :, None]), axis=2)
    jstart = jnp.transpose(jstart, (0, 2, 1))
    n_items = jnp.sum(i_idx[:, 0][None, None, :] - jstart + 1, axis=2)
    return jstart.reshape(-1).astype(jnp.int32), n_items.reshape(-1).astype(jnp.int32)


def _fox_attention(fqT, kaug, fvT, drel, stats):
    B, _, S = fqT.shape
    tq, tk = FOX_TQ, FOX_TK
    n_blk = S // INPROJ_TM
    nh = FOX_GROUP_HEADS
    n_groups = FOX_HEADS // nh
    drel4 = drel.reshape(B, n_groups, nh, S)
    c = stats[:, :, 0, :FOX_HEADS]
    c = jnp.transpose(c, (0, 2, 1)).reshape(B, n_groups, nh * n_blk, 1)
    crep = jnp.broadcast_to(c, (B, n_groups, nh * n_blk, tq))
    jstart, n_items = _prune_plan(stats, drel, S // tq)
    grid_spec = pltpu.PrefetchScalarGridSpec(
        num_scalar_prefetch=2,
        grid=(B, n_groups),
        in_specs=[
            pl.BlockSpec((1, nh * FOX_HEAD_DIM, S), lambda b, p, js, ni: (b, p, 0)),
            pl.BlockSpec((1, S, nh // 2 * MXU_DIM), lambda b, p, js, ni: (b, 0, p)),
            pl.BlockSpec((1, nh * FOX_HEAD_DIM, S), lambda b, p, js, ni: (b, p, 0)),
            pl.BlockSpec((1, 1, nh, S), lambda b, p, js, ni: (b, p, 0, 0)),
            pl.BlockSpec((1, 1, nh * n_blk, tq), lambda b, p, js, ni: (b, p, 0, 0)),
        ],
        out_specs=pl.BlockSpec((1, S, nh * FOX_HEAD_DIM), lambda b, p, js, ni: (b, 0, p)),
        scratch_shapes=[
            pltpu.VMEM((2 * nh, MXU_DIM, tq), BF16),
            pltpu.VMEM((2, nh, tk, tq), F32),
            pltpu.VMEM((nh, tk, tq), BF16),
            pltpu.VMEM((nh * (tk // tq), SUBLANES, tq), F32),
            pltpu.VMEM((nh, FOX_HEAD_DIM + FOX_SUM_ROWS, tq), F32),
            pltpu.VMEM((nh, 1, tq), F32),
            pltpu.VMEM((nh, 1, tq), F32),
            pltpu.VMEM((nh, 1, tq), F32),
            pltpu.VMEM((nh, 1, tq), F32),
        ])
    return pl.pallas_call(
        _fox_kernel,
        grid_spec=grid_spec,
        out_shape=jax.ShapeDtypeStruct((B, S, FOX_WIDTH), BF16),
        compiler_params=pltpu.CompilerParams(
            dimension_semantics=("arbitrary", "arbitrary"),
            vmem_limit_bytes=VMEM_LIMIT),
        name="fox_attn",
    )(jstart, n_items, fqT, kaug, fvT, drel4, crep)


def _gla_kernel(q_ref, k_ref, v_ref, r_ref, sm_ref, wa_ref, ba_ref, g_ref, o_ref, state_ref):
    nb, ts = q_ref.shape[0], q_ref.shape[1]
    C, P = GLA_CHUNK, GLA_PAIR
    dk, dv, H = GLA_KEY_DIM, GLA_VAL_DIM, GLA_HEADS
    KW = GLA_KEY_WIDTH

    @pl.when(pl.program_id(0) == 0)
    def _():
        state_ref[...] = jnp.zeros_like(state_ref)

    la_all = [_log_sigmoid(_dot(sm_ref[bb].astype(BF16), wa_ref[...]) + ba_ref[...])
              * (1.0 / GLA_GATE_TEMP) for bb in range(nb)]

    ri = lax.broadcasted_iota(jnp.int32, (P, P), 0)
    ci = lax.broadcasted_iota(jnp.int32, (P, P), 1)
    tril2 = jnp.where(((ri < C) == (ci < C)) & (ci <= ri), 1.0, 0.0).astype(BF16)

    rs = lax.broadcasted_iota(jnp.int32, (2 * H * C, P), 0)
    cs = lax.broadcasted_iota(jnp.int32, (2 * H * C, P), 1)
    causal = ((rs >= H * C) == (cs >= C)) & ((cs & (C - 1)) <= (rs & (C - 1)))

    lane_q = lax.broadcasted_iota(jnp.int32, (C, KW), 1)
    lane_t = lax.broadcasted_iota(jnp.int32, (KW, P), 1)
    first = lane_t < C

    per_b = ts // P
    n_slabs = nb * per_b
    slab_b = [k // per_b for k in range(n_slabs)]
    slab_rows = [slice((k % per_b) * P, (k % per_b + 1) * P) for k in range(n_slabs)]
    bs = [_tri_left(tril2, la_all[slab_b[k]][slab_rows[k]]) for k in range(n_slabs)]

    qms, scores, kv0s, kv1s, dec0s, dec1s, vss = [], [], [], [], [], [], []
    for pi in range(n_slabs):
        bb, rows = slab_b[pi], slab_rows[pi]
        b = bs[pi]
        bT = b.T
        bl0 = bT[:, C - 1:C]
        bl1 = bT[:, P - 1:P]
        q_dec = q_ref[bb, rows, :] * jnp.exp(b)
        kT = k_ref[bb, rows, :].T
        k_invT = (kT * jnp.exp(-bT)).astype(BF16)
        k_teT = kT * jnp.exp(jnp.where(first, bl0, bl1) - bT)
        zt = jnp.zeros_like(k_teT)
        k_te0 = jnp.where(first, k_teT, zt).astype(BF16)
        k_te1 = jnp.where(first, zt, k_teT).astype(BF16)
        dec0 = jnp.exp(bl0)
        dec1 = jnp.exp(bl1)

        vs = [v_ref[bb, rows, h * dv:(h + 1) * dv] for h in range(H)]
        kv = [_dot(jnp.concatenate([k_te0[h * dk:(h + 1) * dk, :], k_te1[h * dk:(h + 1) * dk, :]], axis=0),
                   vs[h]) for h in range(H)]
        kv0s.append(jnp.concatenate([kv[h][:dk] for h in range(H)], axis=0))
        kv1s.append(jnp.concatenate([kv[h][dk:] for h in range(H)], axis=0))
        dec0s.append(dec0)
        dec1s.append(dec1)
        vss.append(vs)

        zq = jnp.zeros((C, KW), F32)
        qm = [jnp.concatenate(
            [jnp.where((lane_q >= h * dk) & (lane_q < (h + 1) * dk), q_dec[c * C:(c + 1) * C], zq)
             for h in range(H)], axis=0).astype(BF16) for c in range(2)]
        a = _dot(jnp.concatenate(qm, axis=0), k_invT)
        scores.append(jnp.where(causal, a, 0.0).astype(BF16))
        qms.append(qm)

    o_inters = []
    for pi in range(n_slabs):
        if pi % per_b == 0:
            s = state_ref[slab_b[pi]]
        s1 = dec0s[pi] * s + kv0s[pi]
        o_inters.append([_dot(qms[pi][0], s.astype(BF16)), _dot(qms[pi][1], s1.astype(BF16))])
        s = dec1s[pi] * s1 + kv1s[pi]
        if pi % per_b == per_b - 1:
            state_ref[slab_b[pi]] = s

    for pi in range(n_slabs):
        bb, rows = slab_b[pi], slab_rows[pi]
        a, o_inter, vs = scores[pi], o_inters[pi], vss[pi]
        for h in range(H):
            hv = slice(h * dv, (h + 1) * dv)
            a_h = jnp.concatenate([a[(c * H + h) * C:(c * H + h + 1) * C] for c in range(2)], axis=0)
            o = _dot(a_h, vs[h]) + jnp.concatenate(
                [o_inter[c][h * C:(h + 1) * C] for c in range(2)], axis=0)
            o = o * lax.rsqrt(jnp.mean(o * o, axis=-1, keepdims=True) + RMS_EPS)
            rr = r_ref[bb, rows, hv]
            o = o * g_ref[:, hv] * (rr * (1.0 / (1.0 + jnp.exp(-rr))))
            o_ref[bb, rows, hv] = o.astype(o_ref.dtype)


def _gla_attention(gq, gk, gv, gr, small, w_a2_pad, b_a, g_gla):
    B, S, _ = gq.shape
    ts = GLA_TS
    tok = lambda w: pl.BlockSpec((B, ts, w), lambda s: (0, s, 0))
    full = lambda r, c: pl.BlockSpec((r, c), lambda s: (0, 0))
    return pl.pallas_call(
        _gla_kernel,
        grid=(S // ts,),
        in_specs=[
            tok(GLA_KEY_WIDTH), tok(GLA_KEY_WIDTH), tok(GLA_WIDTH), tok(GLA_WIDTH), tok(SMALL_W),
            full(SMALL_W, GLA_KEY_WIDTH), full(1, GLA_KEY_WIDTH), full(1, GLA_WIDTH),
        ],
        out_specs=tok(GLA_WIDTH),
        out_shape=jax.ShapeDtypeStruct((B, S, GLA_WIDTH), BF16),
        scratch_shapes=[pltpu.VMEM((B, GLA_KEY_WIDTH, GLA_VAL_DIM), F32)],
        compiler_params=pltpu.CompilerParams(
            dimension_semantics=("arbitrary",), vmem_limit_bytes=VMEM_LIMIT),
        name="gla_attn",
    )(gq, gk, gv, gr, small, w_a2_pad, b_a, g_gla)


def _layer_norm(z, g, b):
    mu = jnp.mean(z, axis=-1, keepdims=True)
    d = z - mu
    var = jnp.mean(d * d, axis=-1, keepdims=True)
    return d * lax.rsqrt(var + LN_EPS) * g + b


def _tail_kernel(x_ref, fox_ref, gla_ref, mod_ref, wof_ref, wog_ref, ln1g_ref, ln1b_ref,
                 wg_ref, wu_ref, wd_ref, ln2g_ref, ln2b_ref, o_ref):
    m = mod_ref[0]
    tm = x_ref.shape[0]
    halves = [slice(0, tm // 2), slice(tm // 2, tm)]
    ys = [_dot(fox_ref[r, :], wof_ref[...]) + _dot(gla_ref[r, :], wog_ref[...]) for r in halves]
    x1s, gs, ups = [], [], []
    for r, y in zip(halves, ys):
        x1 = _layer_norm(DEEPNORM_ALPHA * x_ref[r, :] + (1.0 + m[2:3]) * y,
                         ln1g_ref[...], ln1b_ref[...])
        u2 = (x1 * (1.0 + m[4:5]) + m[3:4]).astype(BF16)
        x1s.append(x1)
        gs.append(_dot(u2, wg_ref[...]))
        ups.append(_dot(u2, wu_ref[...]))
    y2s = []
    for g, up in zip(gs, ups):
        h = (g * (1.0 / (1.0 + jnp.exp(-g))) * up).astype(BF16)
        y2s.append(_dot(h, wd_ref[...]))
    for r, x1, y2 in zip(halves, x1s, y2s):
        o_ref[r, :] = _layer_norm(DEEPNORM_ALPHA * x1 + (1.0 + m[5:6]) * y2,
                                  ln2g_ref[...], ln2b_ref[...])


def _tail(x2d, fox2d, gla2d, mod, wo, ln1g, ln1b, wg, wu, wd, ln2g, ln2b, steps_per_batch):
    T, D = x2d.shape
    tm = TAIL_TM
    const = lambda r, c: pl.BlockSpec((r, c), lambda i: (0, 0), pipeline_mode=pl.Buffered(1))
    tok = lambda w: pl.BlockSpec((tm, w), lambda i: (i, 0))
    return pl.pallas_call(
        _tail_kernel,
        grid=(T // tm,),
        in_specs=[
            tok(D), tok(FOX_WIDTH), tok(GLA_WIDTH),
            pl.BlockSpec((1, N_MOD, D), lambda i: (i // steps_per_batch, 0, 0)),
            pl.BlockSpec((FOX_WIDTH, D), lambda i: (0, 0), pipeline_mode=pl.Buffered(1)),
            pl.BlockSpec((GLA_WIDTH, D), lambda i: (1, 0), pipeline_mode=pl.Buffered(1)),
            const(1, D), const(1, D),
            const(D, D_FF), const(D, D_FF), const(D_FF, D), const(1, D), const(1, D),
        ],
        out_specs=tok(D),
        out_shape=jax.ShapeDtypeStruct((T, D), F32),
        compiler_params=pltpu.CompilerParams(
            dimension_semantics=("arbitrary",), vmem_limit_bytes=VMEM_LIMIT),
        name="tail",
    )(x2d, fox2d, gla2d, mod, wo, wo, ln1g, ln1b, wg, wu, wd, ln2g, ln2b)


def kernel(x, c, w_c, b_c, w_in, b_f, w_a2, b_a, g_gla, w_o, ln1_g, ln1_b,
           w_gate, w_up, w_down, ln2_g, ln2_b):
    B, S, D = x.shape
    assert D == D_MODEL and S % INPROJ_TM == 0 and INPROJ_TM == FOX_TK and FOX_TK % FOX_TQ == 0

    b_f_pad = jnp.zeros((1, SMALL_W), F32).at[0, :FOX_HEADS].set(b_f)
    w_a2_pad = jnp.zeros((SMALL_W, GLA_KEY_WIDTH), F32) \
        .at[FOX_HEADS:FOX_HEADS + GLA_GATE_RANK].set(w_a2).astype(BF16)

    mod = _modulation(c, w_c, b_c)

    fqT, kaug, fvT, gq, gk, gv, gr, small, drel, stats, wg16, wu16, wd16 = _in_projection(
        x, mod, w_in, b_f_pad, w_gate, w_up, w_down)
    fox = _fox_attention(fqT, kaug, fvT, drel, stats)
    gla = _gla_attention(gq, gk, gv, gr, small, w_a2_pad,
                         b_a.reshape(1, -1), g_gla.reshape(1, -1))

    T = B * S
    w_o_b = w_o.astype(BF16)
    out = _tail(x.reshape(T, D), fox.reshape(T, FOX_WIDTH), gla.reshape(T, GLA_WIDTH), mod,
                w_o_b,
                ln1_g.reshape(1, D), ln1_b.reshape(1, D),
                wg16, wu16, wd16,
                ln2_g.reshape(1, D), ln2_b.reshape(1, D), S // TAIL_TM)
    return out.reshape(B, S, D)
```

```python
import functools
import math

import numpy as np
import jax
import jax.numpy as jnp
from jax import lax
from jax.experimental import pallas as pl
from jax.experimental.pallas import tpu as pltpu

F32 = jnp.float32
BF16 = jnp.bfloat16

D_MODEL = 1024
FOX_WIDTH = 512
FOX_HEAD_DIM = 64
FOX_HEADS = 8
FOX_PAIRS = FOX_HEADS // 2
GLA_WIDTH = 512
GLA_HEADS = 4
GLA_VAL_DIM = 128
GLA_KEY_DIM = 64
GLA_KEY_WIDTH = 256
GLA_GATE_RANK = 16
GLA_GATE_TEMP = 16.0
GLA_CHUNK = 64
D_FF = 2816
LN_EPS = 1e-5
RMS_EPS = 1e-6
N_MOD = 6
DEEPNORM_ALPHA = 2.0 ** 0.25
IN_SPLITS = (FOX_WIDTH, FOX_WIDTH, FOX_WIDTH, FOX_HEADS, GLA_KEY_WIDTH, GLA_KEY_WIDTH,
             GLA_WIDTH, GLA_GATE_RANK, GLA_WIDTH)
LOG2E = math.log2(math.e)

LANES = 128
SUBLANES = 8
MXU_DIM = 256
SMALL_W = LANES
VMEM_LIMIT = 56 * 1024 * 1024

MOD_TN = 1536
INPROJ_TM = 512
FOX_TQ = 256
FOX_GROUP_HEADS = 4
FOX_SUM_ROWS = 16
FOX_TK = 512
GLA_TS = 512
GLA_PAIR = 2 * GLA_CHUNK
TAIL_TM = 512

NEG_BIG = -1e30
PRUNE_LOG2 = -1080.0
NORM_SLACK = 1.02

AUG_ROWS = 32


def _log_sigmoid(z):
    return jnp.minimum(z, 0.0) - jnp.log(1.0 + jnp.exp(-jnp.abs(z)))


def _split3(a):
    hi = a.astype(BF16)
    r1 = a - hi.astype(F32)
    mid = r1.astype(BF16)
    lo = (r1 - mid.astype(F32)).astype(BF16)
    return hi, mid, lo


def _dot(a, b):
    return jnp.dot(a, b, preferred_element_type=F32)


def _tri_left(tri, a):
    hi, mid, lo = _split3(a)
    return _dot(tri, hi) + _dot(tri, mid) + _dot(tri, lo)


def _tri_right(a, tri):
    hi, mid, lo = _split3(a)
    return _dot(hi, tri) + _dot(mid, tri) + _dot(lo, tri)


def _mod_kernel(cT_ref, w_ref, b_ref, o_ref, *, n_rows):
    w = w_ref[...]
    outs = [jnp.sum(w * cT_ref[:, b:b + 1], axis=0, keepdims=True) for b in range(n_rows)]
    outs.append(jnp.zeros((o_ref.shape[0] - n_rows, w.shape[1]), F32))
    o_ref[...] = jnp.concatenate(outs, axis=0) + b_ref[...]


def _modulation(c, w_c, b_c):
    B, D = c.shape
    N = w_c.shape[1]
    rows = SUBLANES
    cT_pad = jnp.zeros((D, LANES), F32).at[:, :B].set(c.T)
    out = pl.pallas_call(
        functools.partial(_mod_kernel, n_rows=B),
        grid=(N // MOD_TN,),
        in_specs=[
            pl.BlockSpec((D, LANES), lambda j: (0, 0)),
            pl.BlockSpec((D, MOD_TN), lambda j: (0, j)),
            pl.BlockSpec((1, MOD_TN), lambda j: (0, j)),
        ],
        out_specs=pl.BlockSpec((rows, MOD_TN), lambda j: (0, j)),
        out_shape=jax.ShapeDtypeStruct((rows, N), F32),
        compiler_params=pltpu.CompilerParams(
            dimension_semantics=("arbitrary",), vmem_limit_bytes=VMEM_LIMIT),
        name="adaln_mod",
    )(cT_pad, w_c, b_c.reshape(1, N))
    return out[:B].reshape(B, N_MOD, D)


_C_FQ, _C_FK, _C_FV = 0, 512, 1024
_C_GQ, _C_SM, _C_GK, _C_GV, _C_GR = 1536, 1792, 2048, 2304, 2816
_W_COLS = _C_GR + GLA_WIDTH
_SRC = [sum(IN_SPLITS[:i]) for i in range(len(IN_SPLITS))]
_SRC_FF, _SRC_GA = _SRC[3], _SRC[7]
_W_GROUPS = ((_C_FQ, _SRC[0], FOX_WIDTH), (_C_FK, _SRC[1], FOX_WIDTH), (_C_FV, _SRC[2], FOX_WIDTH),
             (_C_GQ, _SRC[4], GLA_KEY_WIDTH), (_C_GK, _SRC[5], GLA_KEY_WIDTH),
             (_C_GV, _SRC[6], GLA_WIDTH), (_C_GR, _SRC[8], GLA_WIDTH))
_KAUG_W = FOX_PAIRS * MXU_DIM


def _head_indicator():
    ind = np.zeros((FOX_WIDTH, MXU_DIM), np.float32)
    for h in range(FOX_HEADS):
        ind[h * FOX_HEAD_DIM:(h + 1) * FOX_HEAD_DIM, h] = 1.0
    return ind


def _inproj_kernel(x_ref, mod_ref, w_ref, bf_ref, ind_ref, wg32_ref, wu32_ref, wd32_ref,
                   fqT_ref, kaug_ref, fvT_ref, gq_ref, gk_ref, gv_ref, gr_ref, sm_ref,
                   drel_ref, stats_ref, wg16_ref, wu16_ref, wd16_ref, carry_ref, ws_ref):
    tm = x_ref.shape[1]

    wg16_ref[...] = wg32_ref[...].astype(BF16)
    wu16_ref[...] = wu32_ref[...].astype(BF16)
    wd16_ref[...] = wd32_ref[...].astype(BF16)

    @pl.when(pl.program_id(1) == 0)
    def _():
        carry_ref[...] = jnp.zeros_like(carry_ref)

    m = mod_ref[0]
    u = (x_ref[0] * (1.0 + m[1:2]) + m[0:1]).astype(BF16)

    @pl.when((pl.program_id(0) == 0) & (pl.program_id(1) == 0))
    def _():
        rows_per = 256
        for r0 in range(0, w_ref.shape[0], rows_per):
            rs = slice(r0, r0 + rows_per)
            for dst, src, width in _W_GROUPS:
                ws_ref[rs, dst:dst + width] = w_ref[rs, src:src + width].astype(BF16)
            small_w = jnp.concatenate(
                [w_ref[rs, _SRC_FF:_SRC_FF + FOX_HEADS], w_ref[rs, _SRC_GA:_SRC_GA + GLA_GATE_RANK],
                 jnp.zeros((rows_per, 2 * SMALL_W - FOX_HEADS - GLA_GATE_RANK), F32)], axis=1)
            ws_ref[rs, _C_SM:_C_SM + 2 * SMALL_W] = small_w.astype(BF16)

    def proj(lo, width):
        return _dot(u, ws_ref[:, lo:lo + width])

    gq_small = proj(_C_GQ, 2 * GLA_KEY_WIDTH)
    gq_ref[0] = gq_small[:, :GLA_KEY_WIDTH] * (GLA_KEY_DIM ** -0.5)
    small = gq_small[:, GLA_KEY_WIDTH:GLA_KEY_WIDTH + SMALL_W]
    sm_ref[0] = small

    fq = proj(_C_FQ, FOX_WIDTH) * (FOX_HEAD_DIM ** -0.5 * LOG2E)
    fqT_ref[0] = fq.T.astype(BF16)
    fk32 = proj(_C_FK, FOX_WIDTH)
    fk = fk32.astype(BF16)
    qn2 = jnp.max(_dot((fq * fq).astype(BF16), ind_ref[...])[:, :SMALL_W], axis=0, keepdims=True)
    kn2 = jnp.max(_dot((fk32 * fk32).astype(BF16), ind_ref[...])[:, :SMALL_W], axis=0, keepdims=True)

    zT = (small + bf_ref[...]).T[:FOX_HEADS, :]
    lfT = _log_sigmoid(zT) * LOG2E
    r = lax.broadcasted_iota(jnp.int32, (tm, tm), 0)
    c = lax.broadcasted_iota(jnp.int32, (tm, tm), 1)
    triu = jnp.where(r <= c, 1.0, 0.0).astype(BF16)
    drelT = _tri_right(lfT, triu)
    drel_ref[0] = drelT

    c_before = carry_ref[...]
    srow = lax.broadcasted_iota(jnp.int32, (SUBLANES, SMALL_W), 0)
    slane = lax.broadcasted_iota(jnp.int32, (SUBLANES, SMALL_W), 1)
    total = jnp.sum(jnp.where(srow == slane, drelT[:, tm - 1:tm], 0.0), axis=0, keepdims=True)
    carry_ref[...] = c_before + total

    hi, mid, lo = _split3(drelT)
    ones3 = jnp.where(lax.broadcasted_iota(jnp.int32, (SUBLANES, tm), 0) < 3, 1.0, 0.0)
    slabT = jnp.concatenate(
        [-hi.astype(F32), -mid.astype(F32), -lo.astype(F32), ones3,
         jnp.zeros((LANES - 4 * SUBLANES, tm), F32)], axis=0)
    slab = slabT.T.astype(BF16)

    fvT_ref[0] = proj(_C_FV, FOX_WIDTH).T.astype(BF16)
    gv_ref[0] = proj(_C_GV, GLA_WIDTH).astype(BF16)
    gr_ref[0] = proj(_C_GR, GLA_WIDTH)
    gk_ref[0] = proj(_C_GK, GLA_KEY_WIDTH)

    stats_ref[0, 0] = jnp.where(srow == 0, c_before,
                                jnp.where(srow == 1, qn2, jnp.where(srow == 2, kn2, 0.0)))
    for p in range(FOX_PAIRS):
        kaug_ref[0, :, p * MXU_DIM:p * MXU_DIM + LANES] = fk[:, p * LANES:(p + 1) * LANES]
        kaug_ref[0, :, p * MXU_DIM + LANES:(p + 1) * MXU_DIM] = slab


def _in_projection(x, mod, w_in, b_f_pad, w_gate, w_up, w_down):
    B, S, D = x.shape
    tm = INPROJ_TM
    n_steps = B * (S // tm)
    steps_b = S // tm
    up_rows = D // n_steps
    down_rows = 2 * D_FF // n_steps
    assert D % n_steps == 0 and up_rows % 16 == 0 and (2 * D_FF) % n_steps == 0 and down_rows % 16 == 0
    up_spec = pl.BlockSpec((up_rows, D_FF), lambda b, s: (b * steps_b + s, 0))
    down_spec = pl.BlockSpec((down_rows, D), lambda b, s: ((b * steps_b + s) // 2, 0))
    ind = _head_indicator()
    tok = lambda w: pl.BlockSpec((1, tm, w), lambda b, s: (b, s, 0))
    tokT = lambda h: pl.BlockSpec((1, h, tm), lambda b, s: (b, 0, s))
    outs = pl.pallas_call(
        _inproj_kernel,
        grid=(B, S // tm),
        in_specs=[
            tok(D),
            pl.BlockSpec((1, N_MOD, D), lambda b, s: (b, 0, 0)),
            pl.BlockSpec(w_in.shape, lambda b, s: (0, 0), pipeline_mode=pl.Buffered(1)),
            pl.BlockSpec((1, SMALL_W), lambda b, s: (0, 0)),
            pl.BlockSpec(ind.shape, lambda b, s: (0, 0)),
            up_spec, up_spec, down_spec,
        ],
        out_specs=[
            tokT(FOX_WIDTH), tok(_KAUG_W), tokT(FOX_WIDTH),
            tok(GLA_KEY_WIDTH), tok(GLA_KEY_WIDTH), tok(GLA_WIDTH), tok(GLA_WIDTH),
            tok(SMALL_W),
            tokT(FOX_HEADS),
            pl.BlockSpec((1, 1, SUBLANES, SMALL_W), lambda b, s: (b, s, 0, 0)),
            up_spec, up_spec, down_spec,
        ],
        out_shape=[
            jax.ShapeDtypeStruct((B, FOX_WIDTH, S), BF16),
            jax.ShapeDtypeStruct((B, S, _KAUG_W), BF16),
            jax.ShapeDtypeStruct((B, FOX_WIDTH, S), BF16),
            jax.ShapeDtypeStruct((B, S, GLA_KEY_WIDTH), F32),
            jax.ShapeDtypeStruct((B, S, GLA_KEY_WIDTH), F32),
            jax.ShapeDtypeStruct((B, S, GLA_WIDTH), BF16),
            jax.ShapeDtypeStruct((B, S, GLA_WIDTH), F32),
            jax.ShapeDtypeStruct((B, S, SMALL_W), F32),
            jax.ShapeDtypeStruct((B, FOX_HEADS, S), F32),
            jax.ShapeDtypeStruct((B, S // tm, SUBLANES, SMALL_W), F32),
            jax.ShapeDtypeStruct(w_gate.shape, BF16),
            jax.ShapeDtypeStruct(w_up.shape, BF16),
            jax.ShapeDtypeStruct(w_down.shape, BF16),
        ],
        scratch_shapes=[pltpu.VMEM((1, SMALL_W), F32),
                        pltpu.VMEM((D, _W_COLS), BF16)],
        compiler_params=pltpu.CompilerParams(
            dimension_semantics=("arbitrary", "arbitrary"), vmem_limit_bytes=VMEM_LIMIT),
        name="in_proj",
    )(x, mod, w_in, b_f_pad, jnp.asarray(ind, BF16), w_gate, w_up, w_down)
    return outs


def _fox_kernel(jstart_ref, nitems_ref, qT_ref, kaug_ref, vT_ref, drel_ref, crep_ref, o_ref,
                qaug_ref, s_ref, p_ref, mx_ref, acc_ref, m_ref, mnext_ref, shift_ref, alpha_ref):
    tq, tk = FOX_TQ, FOX_TK
    half = FOX_HEAD_DIM
    S = kaug_ref.shape[1]
    nq = S // tq
    nh = FOX_GROUP_HEADS
    n_blk = crep_ref.shape[2] // nh
    g = pl.program_id(0) * pl.num_programs(1) + pl.program_id(1)

    def build_qaug(qi):
        q0 = pl.multiple_of(qi * tq, tq)
        dr = drel_ref[0, 0, :, pl.ds(q0, tq)]
        row = lax.broadcasted_iota(jnp.int32, (2 * half, tq), 0)
        r32 = lax.broadcasted_iota(jnp.int32, (AUG_ROWS, tq), 0)
        for h in range(nh):
            buf = (qi % 2) * nh + h
            qT = qT_ref[0, (h // 2) * 2 * half:(h // 2 + 1) * 2 * half, pl.ds(q0, tq)]
            own = (row < half) if h % 2 == 0 else (row >= half)
            qaug_ref[buf, 0:2 * half, :] = jnp.where(own, qT, jnp.zeros_like(qT))
            hi, mid, lo = _split3(dr[h:h + 1])
            gh = nh * pl.program_id(1) + h
            pick = (r32 == gh) | (r32 == gh + SUBLANES) | (r32 == gh + 2 * SUBLANES)
            slab = jnp.where(r32 == 24, hi.astype(F32),
                             jnp.where(r32 == 25, mid.astype(F32),
                                       jnp.where(r32 == 26, lo.astype(F32),
                                                 jnp.where(pick, 1.0, 0.0))))
            qaug_ref[buf, 2 * half:2 * half + AUG_ROWS, :] = slab.astype(BF16)
            qaug_ref[buf, 2 * half + AUG_ROWS:, :] = jnp.zeros(
                (MXU_DIM - 2 * half - AUG_ROWS, tq), BF16)

    def reset_state():
        acc_ref[...] = jnp.zeros_like(acc_ref)

    sum_rows = jnp.where(lax.broadcasted_iota(jnp.int32, (FOX_SUM_ROWS, tk), 0) == 0, 1.0, 0.0).astype(BF16)

    per = tk // tq

    def col_max8(s):
        return jnp.max(s.reshape(s.shape[0] // SUBLANES, SUBLANES, tq), axis=0)

    def qk_dots(qi, j):
        k_off = pl.multiple_of(j * tk, tk)
        kts = [kaug_ref[0, pl.ds(k_off, tk), p * MXU_DIM:(p + 1) * MXU_DIM] for p in range(nh // 2)]
        return [_dot(kts[h // 2], qaug_ref[(qi % 2) * nh + h]) for h in range(nh)]

    def store_scores(scores):
        for h in range(nh):
            s_ref[h] = scores[h]
            for g in range(per):
                mx_ref[per * h + g] = col_max8(scores[h][g * tq:(g + 1) * tq])

    def mask_stage(qi):
        d = qi % per
        d0 = pl.multiple_of(d * tq, tq)
        kr = lax.broadcasted_iota(jnp.int32, (tq, tq), 0)
        qc = lax.broadcasted_iota(jnp.int32, (tq, tq), 1)
        for h in range(nh):
            tri = jnp.where(kr <= qc, s_ref[h, pl.ds(d0, tq), :], NEG_BIG)
            s_ref[h, pl.ds(d0, tq), :] = tri
            mx_ref[per * h + d] = col_max8(tri)
        for g in range(1, per):
            @pl.when(d < g)
            def _():
                for h in range(nh):
                    s_ref[h, g * tq:(g + 1) * tq, :] = jnp.full((tq, tq), NEG_BIG, F32)
                    mx_ref[per * h + g] = jnp.full((SUBLANES, tq), NEG_BIG, F32)

    def prep_stage(qi, j, first):
        i_blk = qi * tq // INPROJ_TM
        for h in range(nh):
            big_c = (crep_ref[0, 0, pl.ds(h * n_blk + i_blk, 1), :]
                     - crep_ref[0, 0, pl.ds(h * n_blk + j, 1), :])
            mx = mx_ref[per * h]
            for g in range(1, per):
                mx = jnp.maximum(mx, mx_ref[per * h + g])
            m_prev = jnp.where(first, NEG_BIG, m_ref[h])
            m_new = jnp.maximum(m_prev, jnp.max(mx, axis=0, keepdims=True) + big_c)
            mnext_ref[h] = m_new
            shift_ref[h] = m_new - big_c
            alpha_ref[h] = jnp.exp2(m_prev - m_new)

    def exp_stage():
        for h in range(nh):
            p_ref[h] = jnp.exp2(s_ref[h] - shift_ref[h]).astype(BF16)

    def pv_stage(j):
        k_off = pl.multiple_of(j * tk, tk)
        for h in range(nh):
            vT = jnp.concatenate([vT_ref[0, h * half:(h + 1) * half, pl.ds(k_off, tk)], sum_rows], axis=0)
            acc_ref[h] = alpha_ref[h] * acc_ref[h] + _dot(vT, p_ref[h])
            m_ref[h] = mnext_ref[h]

    def finalize(qi):
        outs = []
        for h in range(nh):
            acc = acc_ref[h]
            outs.append(acc[:half] / acc[half:half + 1])
        oT = jnp.concatenate(outs, axis=0)
        o_ref[0, pl.ds(pl.multiple_of(qi * tq, tq), tq), :] = oT.T.astype(o_ref.dtype)

    zero = jnp.int32(0)
    build_qaug(zero)
    build_qaug(zero + 1)
    reset_state()
    store_scores(qk_dots(zero, zero))
    mask_stage(zero)
    prep_stage(zero, zero, True)

    def body(t, carry):
        qi, j = carry
        last = j == qi * tq // tk
        qi2 = jnp.where(last, qi + 1, qi)
        j2 = jnp.where(last, jstart_ref[g * nq + jnp.minimum(qi + 1, nq - 1)], j + 1)
        exp_stage()
        nxt = qk_dots(qi2, j2)
        pv_stage(j)
        store_scores(nxt)
        prep_stage(qi2, j2, last)

        @pl.when(j2 == qi2 * tq // tk)
        def _():
            mask_stage(qi2)
            prep_stage(qi2, j2, last)

        @pl.when(last)
        def _():
            finalize(qi)
            reset_state()

            @pl.when(qi + 2 < nq)
            def _():
                build_qaug(qi + 2)

        return qi2, j2

    qi, j = lax.fori_loop(0, nitems_ref[g] - 1, body, (jnp.int32(0), jnp.int32(0)))
    exp_stage()
    pv_stage(j)
    finalize(qi)


def _prune_plan(stats, drel, nq):
    B, n_blk = stats.shape[:2]
    per = FOX_TK // FOX_TQ
    c = stats[:, :, 0, :FOX_HEADS]
    qn = jnp.sqrt(stats[:, :, 1, :FOX_HEADS])
    kn = jnp.sqrt(stats[:, :, 2, :FOX_HEADS])
    c_next = jnp.concatenate([c[:, 1:], c[:, -1:]], axis=1)
    d_in = jnp.transpose(drel[:, :, FOX_TQ - 1::FOX_TQ], (0, 2, 1))
    d_in = jnp.concatenate([jnp.zeros_like(d_in[:, :1]), d_in[:, :-1]], axis=1)
    first_in_tile = (jnp.arange(nq) % per == 0)[None, :, None]
    d_before = jnp.repeat(c, per, axis=1) + jnp.where(first_in_tile, 0.0, d_in)
    qn_q = jnp.repeat(qn, per, axis=1)
    kn_q = jnp.repeat(kn, per, axis=1)
    bound = (d_before[:, :, None, :] - c_next[:, None, :, :]
             + NORM_SLACK * qn_q[:, :, None, :] * (kn[:, None, :, :] + kn_q[:, :, None, :]))
    i_idx = (jnp.arange(nq) // per)[:, None]
    j_idx = jnp.arange(n_blk)[None, :]
    skip = (bound <= PRUNE_LOG2) & (j_idx + 1 <= i_idx)[None, :, :, None]
    nh = FOX_GROUP_HEADS
    skip_pair = skip[..., 0::nh]
    for h in range(1, nh):
        skip_pair = skip_pair & skip[..., h::nh]
    jstart = jnp.min(jnp.where(skip_pair, n_blk, j_idx[None, :, :, None]), axis=2)
    jstart = jnp.transpose(jstart, (0, 2, 1))
    n_items = jnp.sum(i_idx[:, 0][None, None, :] - jstart + 1, axis=2)
    return jstart.reshape(-1).astype(jnp.int32), n_items.reshape(-1).astype(jnp.int32)


def _fox_attention(fqT, kaug, fvT, drel, stats):
    B, _, S = fqT.shape
    tq, tk = FOX_TQ, FOX_TK
    n_blk = S // INPROJ_TM
    nh = FOX_GROUP_HEADS
    n_groups = FOX_HEADS // nh
    drel4 = drel.reshape(B, n_groups, nh, S)
    c = stats[:, :, 0, :FOX_HEADS]
    c = jnp.transpose(c, (0, 2, 1)).reshape(B, n_groups, nh * n_blk, 1)
    crep = jnp.broadcast_to(c, (B, n_groups, nh * n_blk, tq))
    jstart, n_items = _prune_plan(stats, drel, S // tq)
    grid_spec = pltpu.PrefetchScalarGridSpec(
        num_scalar_prefetch=2,
        grid=(B, n_groups),
        in_specs=[
            pl.BlockSpec((1, nh * FOX_HEAD_DIM, S), lambda b, p, js, ni: (b, p, 0)),
            pl.BlockSpec((1, S, nh // 2 * MXU_DIM), lambda b, p, js, ni: (b, 0, p)),
            pl.BlockSpec((1, nh * FOX_HEAD_DIM, S), lambda b, p, js, ni: (b, p, 0)),
            pl.BlockSpec((1, 1, nh, S), lambda b, p, js, ni: (b, p, 0, 0)),
            pl.BlockSpec((1, 1, nh * n_blk, tq), lambda b, p, js, ni: (b, p, 0, 0)),
        ],
        out_specs=pl.BlockSpec((1, S, nh * FOX_HEAD_DIM), lambda b, p, js, ni: (b, 0, p)),
        scratch_shapes=[
            pltpu.VMEM((2 * nh, MXU_DIM, tq), BF16),
            pltpu.VMEM((nh, tk, tq), F32),
            pltpu.VMEM((nh, tk, tq), BF16),
            pltpu.VMEM((nh * (tk // tq), SUBLANES, tq), F32),
            pltpu.VMEM((nh, FOX_HEAD_DIM + FOX_SUM_ROWS, tq), F32),
            pltpu.VMEM((nh, 1, tq), F32),
            pltpu.VMEM((nh, 1, tq), F32),
            pltpu.VMEM((nh, 1, tq), F32),
            pltpu.VMEM((nh, 1, tq), F32),
        ])
    return pl.pallas_call(
        _fox_kernel,
        grid_spec=grid_spec,
        out_shape=jax.ShapeDtypeStruct((B, S, FOX_WIDTH), BF16),
        compiler_params=pltpu.CompilerParams(
            dimension_semantics=("arbitrary", "arbitrary"),
            vmem_limit_bytes=VMEM_LIMIT),
        name="fox_attn",
    )(jstart, n_items, fqT, kaug, fvT, drel4, crep)


def _gla_kernel(q_ref, k_ref, v_ref, r_ref, sm_ref, wa_ref, ba_ref, g_ref, o_ref, state_ref):
    nb, ts = q_ref.shape[0], q_ref.shape[1]
    C, P = GLA_CHUNK, GLA_PAIR
    dk, dv, H = GLA_KEY_DIM, GLA_VAL_DIM, GLA_HEADS
    KW = GLA_KEY_WIDTH

    @pl.when(pl.program_id(0) == 0)
    def _():
        state_ref[...] = jnp.zeros_like(state_ref)

    la_all = [_log_sigmoid(_dot(sm_ref[bb].astype(BF16), wa_ref[...]) + ba_ref[...])
              * (1.0 / GLA_GATE_TEMP) for bb in range(nb)]

    ri = lax.broadcasted_iota(jnp.int32, (P, P), 0)
    ci = lax.broadcasted_iota(jnp.int32, (P, P), 1)
    tril2 = jnp.where(((ri < C) == (ci < C)) & (ci <= ri), 1.0, 0.0).astype(BF16)

    rs = lax.broadcasted_iota(jnp.int32, (2 * H * C, P), 0)
    cs = lax.broadcasted_iota(jnp.int32, (2 * H * C, P), 1)
    causal = ((rs >= H * C) == (cs >= C)) & ((cs & (C - 1)) <= (rs & (C - 1)))

    lane_q = lax.broadcasted_iota(jnp.int32, (C, KW), 1)
    lane_t = lax.broadcasted_iota(jnp.int32, (KW, P), 1)
    first = lane_t < C

    per_b = ts // P
    n_slabs = nb * per_b
    slab_b = [k // per_b for k in range(n_slabs)]
    slab_rows = [slice((k % per_b) * P, (k % per_b + 1) * P) for k in range(n_slabs)]
    bs = [_tri_left(tril2, la_all[slab_b[k]][slab_rows[k]]) for k in range(n_slabs)]

    qms, scores, kv0s, kv1s, dec0s, dec1s, vss = [], [], [], [], [], [], []
    for pi in range(n_slabs):
        bb, rows = slab_b[pi], slab_rows[pi]
        b = bs[pi]
        bT = b.T
        bl0 = bT[:, C - 1:C]
        bl1 = bT[:, P - 1:P]
        q_dec = q_ref[bb, rows, :] * jnp.exp(b)
        kT = k_ref[bb, rows, :].T
        k_invT = (kT * jnp.exp(-bT)).astype(BF16)
        k_teT = kT * jnp.exp(jnp.where(first, bl0, bl1) - bT)
        zt = jnp.zeros_like(k_teT)
        k_te0 = jnp.where(first, k_teT, zt).astype(BF16)
        k_te1 = jnp.where(first, zt, k_teT).astype(BF16)
        dec0 = jnp.exp(bl0)
        dec1 = jnp.exp(bl1)

        vs = [v_ref[bb, rows, h * dv:(h + 1) * dv] for h in range(H)]
        kv = [_dot(jnp.concatenate([k_te0[h * dk:(h + 1) * dk, :], k_te1[h * dk:(h + 1) * dk, :]], axis=0),
                   vs[h]) for h in range(H)]
        kv0s.append(jnp.concatenate([kv[h][:dk] for h in range(H)], axis=0))
        kv1s.append(jnp.concatenate([kv[h][dk:] for h in range(H)], axis=0))
        dec0s.append(dec0)
        dec1s.append(dec1)
        vss.append(vs)

        zq = jnp.zeros((C, KW), F32)
        qm = [jnp.concatenate(
            [jnp.where((lane_q >= h * dk) & (lane_q < (h + 1) * dk), q_dec[c * C:(c + 1) * C], zq)
             for h in range(H)], axis=0).astype(BF16) for c in range(2)]
        a = _dot(jnp.concatenate(qm, axis=0), k_invT)
        scores.append(jnp.where(causal, a, 0.0).astype(BF16))
        qms.append(qm)

    o_inters = []
    for pi in range(n_slabs):
        if pi % per_b == 0:
            s = state_ref[slab_b[pi]]
        s1 = dec0s[pi] * s + kv0s[pi]
        o_inters.append([_dot(qms[pi][0], s.astype(BF16)), _dot(qms[pi][1], s1.astype(BF16))])
        s = dec1s[pi] * s1 + kv1s[pi]
        if pi % per_b == per_b - 1:
            state_ref[slab_b[pi]] = s

    for pi in range(n_slabs):
        bb, rows = slab_b[pi], slab_rows[pi]
        a, o_inter, vs = scores[pi], o_inters[pi], vss[pi]
        for h in range(H):
            hv = slice(h * dv, (h + 1) * dv)
            a_h = jnp.concatenate([a[(c * H + h) * C:(c * H + h + 1) * C] for c in range(2)], axis=0)
            o = _dot(a_h, vs[h]) + jnp.concatenate(
                [o_inter[c][h * C:(h + 1) * C] for c in range(2)], axis=0)
            o = o * lax.rsqrt(jnp.mean(o * o, axis=-1, keepdims=True) + RMS_EPS)
            rr = r_ref[bb, rows, hv]
            o = o * g_ref[:, hv] * (rr * (1.0 / (1.0 + jnp.exp(-rr))))
            o_ref[bb, rows, hv] = o.astype(o_ref.dtype)


def _gla_attention(gq, gk, gv, gr, small, w_a2_pad, b_a, g_gla):
    B, S, _ = gq.shape
    ts = GLA_TS
    tok = lambda w: pl.BlockSpec((B, ts, w), lambda s: (0, s, 0))
    full = lambda r, c: pl.BlockSpec((r, c), lambda s: (0, 0))
    return pl.pallas_call(
        _gla_kernel,
        grid=(S // ts,),
        in_specs=[
            tok(GLA_KEY_WIDTH), tok(GLA_KEY_WIDTH), tok(GLA_WIDTH), tok(GLA_WIDTH), tok(SMALL_W),
            full(SMALL_W, GLA_KEY_WIDTH), full(1, GLA_KEY_WIDTH), full(1, GLA_WIDTH),
        ],
        out_specs=tok(GLA_WIDTH),
        out_shape=jax.ShapeDtypeStruct((B, S, GLA_WIDTH), BF16),
        scratch_shapes=[pltpu.VMEM((B, GLA_KEY_WIDTH, GLA_VAL_DIM), F32)],
        compiler_params=pltpu.CompilerParams(
            dimension_semantics=("arbitrary",), vmem_limit_bytes=VMEM_LIMIT),
        name="gla_attn",
    )(gq, gk, gv, gr, small, w_a2_pad, b_a, g_gla)


def _layer_norm(z, g, b):
    mu = jnp.mean(z, axis=-1, keepdims=True)
    d = z - mu
    var = jnp.mean(d * d, axis=-1, keepdims=True)
    return d * lax.rsqrt(var + LN_EPS) * g + b


def _tail_kernel(x_ref, fox_ref, gla_ref, mod_ref, wof_ref, wog_ref, ln1g_ref, ln1b_ref,
                 wg_ref, wu_ref, wd_ref, ln2g_ref, ln2b_ref, o_ref):
    m = mod_ref[0]
    tm = x_ref.shape[0]
    halves = [slice(0, tm // 2), slice(tm // 2, tm)]
    ys = [_dot(fox_ref[r, :], wof_ref[...]) + _dot(gla_ref[r, :], wog_ref[...]) for r in halves]
    x1s, gs, ups = [], [], []
    for r, y in zip(halves, ys):
        x1 = _layer_norm(DEEPNORM_ALPHA * x_ref[r, :] + (1.0 + m[2:3]) * y,
                         ln1g_ref[...], ln1b_ref[...])
        u2 = (x1 * (1.0 + m[4:5]) + m[3:4]).astype(BF16)
        x1s.append(x1)
        gs.append(_dot(u2, wg_ref[...]))
        ups.append(_dot(u2, wu_ref[...]))
    y2s = []
    for g, up in zip(gs, ups):
        h = (g * (1.0 / (1.0 + jnp.exp(-g))) * up).astype(BF16)
        y2s.append(_dot(h, wd_ref[...]))
    for r, x1, y2 in zip(halves, x1s, y2s):
        o_ref[r, :] = _layer_norm(DEEPNORM_ALPHA * x1 + (1.0 + m[5:6]) * y2,
                                  ln2g_ref[...], ln2b_ref[...])


def _tail(x2d, fox2d, gla2d, mod, wo, ln1g, ln1b, wg, wu, wd, ln2g, ln2b, steps_per_batch):
    T, D = x2d.shape
    tm = TAIL_TM
    const = lambda r, c: pl.BlockSpec((r, c), lambda i: (0, 0), pipeline_mode=pl.Buffered(1))
    tok = lambda w: pl.BlockSpec((tm, w), lambda i: (i, 0))
    return pl.pallas_call(
        _tail_kernel,
        grid=(T // tm,),
        in_specs=[
            tok(D), tok(FOX_WIDTH), tok(GLA_WIDTH),
            pl.BlockSpec((1, N_MOD, D), lambda i: (i // steps_per_batch, 0, 0)),
            pl.BlockSpec((FOX_WIDTH, D), lambda i: (0, 0), pipeline_mode=pl.Buffered(1)),
            pl.BlockSpec((GLA_WIDTH, D), lambda i: (1, 0), pipeline_mode=pl.Buffered(1)),
            const(1, D), const(1, D),
            const(D, D_FF), const(D, D_FF), const(D_FF, D), const(1, D), const(1, D),
        ],
        out_specs=tok(D),
        out_shape=jax.ShapeDtypeStruct((T, D), F32),
        compiler_params=pltpu.CompilerParams(
            dimension_semantics=("arbitrary",), vmem_limit_bytes=VMEM_LIMIT),
        name="tail",
    )(x2d, fox2d, gla2d, mod, wo, wo, ln1g, ln1b, wg, wu, wd, ln2g, ln2b)


def kernel(x, c, w_c, b_c, w_in, b_f, w_a2, b_a, g_gla, w_o, ln1_g, ln1_b,
           w_gate, w_up, w_down, ln2_g, ln2_b):
    B, S, D = x.shape
    assert D == D_MODEL and S % INPROJ_TM == 0 and INPROJ_TM == FOX_TK and FOX_TK % FOX_TQ == 0

    b_f_pad = jnp.zeros((1, SMALL_W), F32).at[0, :FOX_HEADS].set(b_f)
    w_a2_pad = jnp.zeros((SMALL_W, GLA_KEY_WIDTH), F32) \
        .at[FOX_HEADS:FOX_HEADS + GLA_GATE_RANK].set(w_a2).astype(BF16)

    mod = _modulation(c, w_c, b_c)

    fqT, kaug, fvT, gq, gk, gv, gr, small, drel, stats, wg16, wu16, wd16 = _in_projection(
        x, mod, w_in, b_f_pad, w_gate, w_up, w_down)
    fox = _fox_attention(fqT, kaug, fvT, drel, stats)
    gla = _gla_attention(gq, gk, gv, gr, small, w_a2_pad,
                         b_a.reshape(1, -1), g_gla.reshape(1, -1))

    T = B * S
    w_o_b = w_o.astype(BF16)
    out = _tail(x.reshape(T, D), fox.reshape(T, FOX_WIDTH), gla.reshape(T, GLA_WIDTH), mod,
                w_o_b,
                ln1_g.reshape(1, D), ln1_b.reshape(1, D),
                wg16, wu16, wd16,
                ln2_g.reshape(1, D), ln2_b.reshape(1, D), S // TAIL_TM)
    return out.reshape(B, S, D)
```

```python
import functools
import math

import numpy as np
import jax
import jax.numpy as jnp
from jax import lax
from jax.experimental import pallas as pl
from jax.experimental.pallas import tpu as pltpu

F32 = jnp.float32
BF16 = jnp.bfloat16

D_MODEL = 1024
FOX_WIDTH = 512
FOX_HEAD_DIM = 64
FOX_HEADS = 8
FOX_PAIRS = FOX_HEADS // 2
GLA_WIDTH = 512
GLA_HEADS = 4
GLA_VAL_DIM = 128
GLA_KEY_DIM = 64
GLA_KEY_WIDTH = 256
GLA_GATE_RANK = 16
GLA_GATE_TEMP = 16.0
GLA_CHUNK = 64
D_FF = 2816
LN_EPS = 1e-5
RMS_EPS = 1e-6
N_MOD = 6
DEEPNORM_ALPHA = 2.0 ** 0.25
IN_SPLITS = (FOX_WIDTH, FOX_WIDTH, FOX_WIDTH, FOX_HEADS, GLA_KEY_WIDTH, GLA_KEY_WIDTH,
             GLA_WIDTH, GLA_GATE_RANK, GLA_WIDTH)
LOG2E = math.log2(math.e)

LANES = 128
SUBLANES = 8
MXU_DIM = 256
SMALL_W = LANES
VMEM_LIMIT = 56 * 1024 * 1024

MOD_TN = 768
INPROJ_TM = 512
FOX_TQ = 256
FOX_GROUP_HEADS = 4
FOX_SUM_ROWS = 16
FOX_TK = 512
GLA_TS = 512
GLA_PAIR = 2 * GLA_CHUNK
TAIL_TM = 512

NEG_BIG = -1e30
PRUNE_LOG2 = -1080.0
NORM_SLACK = 1.02

AUG_ROWS = 32


def _log_sigmoid(z):
    return jnp.minimum(z, 0.0) - jnp.log(1.0 + jnp.exp(-jnp.abs(z)))


def _split3(a):
    hi = a.astype(BF16)
    r1 = a - hi.astype(F32)
    mid = r1.astype(BF16)
    lo = (r1 - mid.astype(F32)).astype(BF16)
    return hi, mid, lo


def _dot(a, b):
    return jnp.dot(a, b, preferred_element_type=F32)


def _tri_left(tri, a):
    hi, mid, lo = _split3(a)
    return _dot(tri, hi) + _dot(tri, mid) + _dot(tri, lo)


def _tri_right(a, tri):
    hi, mid, lo = _split3(a)
    return _dot(hi, tri) + _dot(mid, tri) + _dot(lo, tri)


def _mod_kernel(cT_ref, w_ref, b_ref, o_ref, *, n_rows):
    w = w_ref[...]
    outs = [jnp.sum(w * cT_ref[:, b:b + 1], axis=0, keepdims=True) for b in range(n_rows)]
    outs.append(jnp.zeros((o_ref.shape[0] - n_rows, w.shape[1]), F32))
    o_ref[...] = jnp.concatenate(outs, axis=0) + b_ref[...]


def _modulation(c, w_c, b_c):
    B, D = c.shape
    N = w_c.shape[1]
    rows = SUBLANES
    cT_pad = jnp.zeros((D, LANES), F32).at[:, :B].set(c.T)
    out = pl.pallas_call(
        functools.partial(_mod_kernel, n_rows=B),
        grid=(N // MOD_TN,),
        in_specs=[
            pl.BlockSpec((D, LANES), lambda j: (0, 0)),
            pl.BlockSpec((D, MOD_TN), lambda j: (0, j)),
            pl.BlockSpec((1, MOD_TN), lambda j: (0, j)),
        ],
        out_specs=pl.BlockSpec((rows, MOD_TN), lambda j: (0, j)),
        out_shape=jax.ShapeDtypeStruct((rows, N), F32),
        compiler_params=pltpu.CompilerParams(
            dimension_semantics=("arbitrary",), vmem_limit_bytes=VMEM_LIMIT),
        name="adaln_mod",
    )(cT_pad, w_c, b_c.reshape(1, N))
    return out[:B].reshape(B, N_MOD, D)


_C_FQ, _C_FK, _C_FV = 0, 512, 1024
_C_GQ, _C_SM, _C_GK, _C_GV, _C_GR = 1536, 1792, 2048, 2304, 2816
_W_COLS = _C_GR + GLA_WIDTH
_SRC = [sum(IN_SPLITS[:i]) for i in range(len(IN_SPLITS))]
_SRC_FF, _SRC_GA = _SRC[3], _SRC[7]
_W_GROUPS = ((_C_FQ, _SRC[0], FOX_WIDTH), (_C_FK, _SRC[1], FOX_WIDTH), (_C_FV, _SRC[2], FOX_WIDTH),
             (_C_GQ, _SRC[4], GLA_KEY_WIDTH), (_C_GK, _SRC[5], GLA_KEY_WIDTH),
             (_C_GV, _SRC[6], GLA_WIDTH), (_C_GR, _SRC[8], GLA_WIDTH))
_KAUG_W = FOX_PAIRS * MXU_DIM


def _head_indicator():
    ind = np.zeros((FOX_WIDTH, MXU_DIM), np.float32)
    for h in range(FOX_HEADS):
        ind[h * FOX_HEAD_DIM:(h + 1) * FOX_HEAD_DIM, h] = 1.0
    return ind


def _inproj_kernel(x_ref, mod_ref, w_ref, bf_ref, ind_ref, wg32_ref, wu32_ref, wd32_ref,
                   fqT_ref, kaug_ref, fvT_ref, gq_ref, gk_ref, gv_ref, gr_ref, sm_ref,
                   drel_ref, stats_ref, wg16_ref, wu16_ref, wd16_ref, carry_ref, ws_ref):
    tm = x_ref.shape[1]

    wg16_ref[...] = wg32_ref[...].astype(BF16)
    wu16_ref[...] = wu32_ref[...].astype(BF16)
    wd16_ref[...] = wd32_ref[...].astype(BF16)

    @pl.when(pl.program_id(1) == 0)
    def _():
        carry_ref[...] = jnp.zeros_like(carry_ref)

    m = mod_ref[0]
    u = (x_ref[0] * (1.0 + m[1:2]) + m[0:1]).astype(BF16)

    @pl.when((pl.program_id(0) == 0) & (pl.program_id(1) == 0))
    def _():
        rows_per = 256
        for r0 in range(0, w_ref.shape[0], rows_per):
            rs = slice(r0, r0 + rows_per)
            for dst, src, width in _W_GROUPS:
                ws_ref[rs, dst:dst + width] = w_ref[rs, src:src + width].astype(BF16)
            small_w = jnp.concatenate(
                [w_ref[rs, _SRC_FF:_SRC_FF + FOX_HEADS], w_ref[rs, _SRC_GA:_SRC_GA + GLA_GATE_RANK],
                 jnp.zeros((rows_per, 2 * SMALL_W - FOX_HEADS - GLA_GATE_RANK), F32)], axis=1)
            ws_ref[rs, _C_SM:_C_SM + 2 * SMALL_W] = small_w.astype(BF16)

    def proj(lo, width):
        return _dot(u, ws_ref[:, lo:lo + width])

    gq_small = proj(_C_GQ, 2 * GLA_KEY_WIDTH)
    gq_ref[0] = gq_small[:, :GLA_KEY_WIDTH] * (GLA_KEY_DIM ** -0.5)
    small = gq_small[:, GLA_KEY_WIDTH:GLA_KEY_WIDTH + SMALL_W]
    sm_ref[0] = small

    fq = proj(_C_FQ, FOX_WIDTH) * (FOX_HEAD_DIM ** -0.5 * LOG2E)
    fqT_ref[0] = fq.T.astype(BF16)
    fk32 = proj(_C_FK, FOX_WIDTH)
    fk = fk32.astype(BF16)
    qn2 = jnp.max(_dot((fq * fq).astype(BF16), ind_ref[...])[:, :SMALL_W], axis=0, keepdims=True)
    kn2 = jnp.max(_dot((fk32 * fk32).astype(BF16), ind_ref[...])[:, :SMALL_W], axis=0, keepdims=True)

    zT = (small + bf_ref[...]).T[:FOX_HEADS, :]
    lfT = _log_sigmoid(zT) * LOG2E
    r = lax.broadcasted_iota(jnp.int32, (tm, tm), 0)
    c = lax.broadcasted_iota(jnp.int32, (tm, tm), 1)
    triu = jnp.where(r <= c, 1.0, 0.0).astype(BF16)
    drelT = _tri_right(lfT, triu)
    drel_ref[0] = drelT

    c_before = carry_ref[...]
    srow = lax.broadcasted_iota(jnp.int32, (SUBLANES, SMALL_W), 0)
    slane = lax.broadcasted_iota(jnp.int32, (SUBLANES, SMALL_W), 1)
    total = jnp.sum(jnp.where(srow == slane, drelT[:, tm - 1:tm], 0.0), axis=0, keepdims=True)
    carry_ref[...] = c_before + total

    hi, mid, lo = _split3(drelT)
    ones3 = jnp.where(lax.broadcasted_iota(jnp.int32, (SUBLANES, tm), 0) < 3, 1.0, 0.0)
    slabT = jnp.concatenate(
        [-hi.astype(F32), -mid.astype(F32), -lo.astype(F32), ones3,
         jnp.zeros((LANES - 4 * SUBLANES, tm), F32)], axis=0)
    slab = slabT.T.astype(BF16)

    fvT_ref[0] = proj(_C_FV, FOX_WIDTH).T.astype(BF16)
    gv_ref[0] = proj(_C_GV, GLA_WIDTH).astype(BF16)
    gr_ref[0] = proj(_C_GR, GLA_WIDTH)
    gk_ref[0] = proj(_C_GK, GLA_KEY_WIDTH)

    stats_ref[0, 0] = jnp.where(srow == 0, c_before,
                                jnp.where(srow == 1, qn2, jnp.where(srow == 2, kn2, 0.0)))
    for p in range(FOX_PAIRS):
        kaug_ref[0, :, p * MXU_DIM:p * MXU_DIM + LANES] = fk[:, p * LANES:(p + 1) * LANES]
        kaug_ref[0, :, p * MXU_DIM + LANES:(p + 1) * MXU_DIM] = slab


def _in_projection(x, mod, w_in, b_f_pad, w_gate, w_up, w_down):
    B, S, D = x.shape
    tm = INPROJ_TM
    n_steps = B * (S // tm)
    steps_b = S // tm
    up_rows = D // n_steps
    down_rows = 2 * D_FF // n_steps
    assert D % n_steps == 0 and up_rows % 16 == 0 and (2 * D_FF) % n_steps == 0 and down_rows % 16 == 0
    up_spec = pl.BlockSpec((up_rows, D_FF), lambda b, s: (b * steps_b + s, 0))
    down_spec = pl.BlockSpec((down_rows, D), lambda b, s: ((b * steps_b + s) // 2, 0))
    ind = _head_indicator()
    tok = lambda w: pl.BlockSpec((1, tm, w), lambda b, s: (b, s, 0))
    tokT = lambda h: pl.BlockSpec((1, h, tm), lambda b, s: (b, 0, s))
    outs = pl.pallas_call(
        _inproj_kernel,
        grid=(B, S // tm),
        in_specs=[
            tok(D),
            pl.BlockSpec((1, N_MOD, D), lambda b, s: (b, 0, 0)),
            pl.BlockSpec(w_in.shape, lambda b, s: (0, 0), pipeline_mode=pl.Buffered(1)),
            pl.BlockSpec((1, SMALL_W), lambda b, s: (0, 0)),
            pl.BlockSpec(ind.shape, lambda b, s: (0, 0)),
            up_spec, up_spec, down_spec,
        ],
        out_specs=[
            tokT(FOX_WIDTH), tok(_KAUG_W), tokT(FOX_WIDTH),
            tok(GLA_KEY_WIDTH), tok(GLA_KEY_WIDTH), tok(GLA_WIDTH), tok(GLA_WIDTH),
            tok(SMALL_W),
            tokT(FOX_HEADS),
            pl.BlockSpec((1, 1, SUBLANES, SMALL_W), lambda b, s: (b, s, 0, 0)),
            up_spec, up_spec, down_spec,
        ],
        out_shape=[
            jax.ShapeDtypeStruct((B, FOX_WIDTH, S), BF16),
            jax.ShapeDtypeStruct((B, S, _KAUG_W), BF16),
            jax.ShapeDtypeStruct((B, FOX_WIDTH, S), BF16),
            jax.ShapeDtypeStruct((B, S, GLA_KEY_WIDTH), F32),
            jax.ShapeDtypeStruct((B, S, GLA_KEY_WIDTH), F32),
            jax.ShapeDtypeStruct((B, S, GLA_WIDTH), BF16),
            jax.ShapeDtypeStruct((B, S, GLA_WIDTH), F32),
            jax.ShapeDtypeStruct((B, S, SMALL_W), F32),
            jax.ShapeDtypeStruct((B, FOX_HEADS, S), F32),
            jax.ShapeDtypeStruct((B, S // tm, SUBLANES, SMALL_W), F32),
            jax.ShapeDtypeStruct(w_gate.shape, BF16),
            jax.ShapeDtypeStruct(w_up.shape, BF16),
            jax.ShapeDtypeStruct(w_down.shape, BF16),
        ],
        scratch_shapes=[pltpu.VMEM((1, SMALL_W), F32),
                        pltpu.VMEM((D, _W_COLS), BF16)],
        compiler_params=pltpu.CompilerParams(
            dimension_semantics=("arbitrary", "arbitrary"), vmem_limit_bytes=VMEM_LIMIT),
        name="in_proj",
    )(x, mod, w_in, b_f_pad, jnp.asarray(ind, BF16), w_gate, w_up, w_down)
    return outs


def _fox_kernel(jstart_ref, nitems_ref, qT_ref, kaug_ref, vT_ref, drel_ref, crep_ref, o_ref,
                qaug_ref, s_ref, p_ref, mx_ref, acc_ref, m_ref, mnext_ref, shift_ref, alpha_ref):
    tq, tk = FOX_TQ, FOX_TK
    half = FOX_HEAD_DIM
    S = kaug_ref.shape[1]
    nq = S // tq
    nh = FOX_GROUP_HEADS
    n_blk = crep_ref.shape[2] // nh
    g = pl.program_id(0) * pl.num_programs(1) + pl.program_id(1)

    def build_qaug(qi):
        q0 = pl.multiple_of(qi * tq, tq)
        dr = drel_ref[0, 0, :, pl.ds(q0, tq)]
        row = lax.broadcasted_iota(jnp.int32, (2 * half, tq), 0)
        r32 = lax.broadcasted_iota(jnp.int32, (AUG_ROWS, tq), 0)
        for h in range(nh):
            buf = (qi % 2) * nh + h
            qT = qT_ref[0, (h // 2) * 2 * half:(h // 2 + 1) * 2 * half, pl.ds(q0, tq)]
            own = (row < half) if h % 2 == 0 else (row >= half)
            qaug_ref[buf, 0:2 * half, :] = jnp.where(own, qT, jnp.zeros_like(qT))
            hi, mid, lo = _split3(dr[h:h + 1])
            gh = nh * pl.program_id(1) + h
            pick = (r32 == gh) | (r32 == gh + SUBLANES) | (r32 == gh + 2 * SUBLANES)
            slab = jnp.where(r32 == 24, hi.astype(F32),
                             jnp.where(r32 == 25, mid.astype(F32),
                                       jnp.where(r32 == 26, lo.astype(F32),
                                                 jnp.where(pick, 1.0, 0.0))))
            qaug_ref[buf, 2 * half:2 * half + AUG_ROWS, :] = slab.astype(BF16)

    def reset_state():
        acc_ref[...] = jnp.zeros_like(acc_ref)

    sum_rows = jnp.where(lax.broadcasted_iota(jnp.int32, (FOX_SUM_ROWS, tk), 0) == 0, 1.0, 0.0).astype(BF16)

    per = tk // tq

    def col_max8(s):
        return jnp.max(s.reshape(s.shape[0] // SUBLANES, SUBLANES, tq), axis=0)

    def qk_dots(qi, j):
        k_off = pl.multiple_of(j * tk, tk)
        kts = [kaug_ref[0, pl.ds(k_off, tk), p * MXU_DIM:(p + 1) * MXU_DIM] for p in range(nh // 2)]
        return [_dot(kts[h // 2], qaug_ref[(qi % 2) * nh + h]) for h in range(nh)]

    def store_scores(scores):
        for h in range(nh):
            s_ref[h] = scores[h]
            for g in range(per):
                mx_ref[per * h + g] = col_max8(scores[h][g * tq:(g + 1) * tq])

    def mask_stage(qi):
        d = qi % per
        d0 = pl.multiple_of(d * tq, tq)
        kr = lax.broadcasted_iota(jnp.int32, (tq, tq), 0)
        qc = lax.broadcasted_iota(jnp.int32, (tq, tq), 1)
        for h in range(nh):
            tri = jnp.where(kr <= qc, s_ref[h, pl.ds(d0, tq), :], NEG_BIG)
            s_ref[h, pl.ds(d0, tq), :] = tri
            mx_ref[per * h + d] = col_max8(tri)
        for g in range(1, per):
            @pl.when(d < g)
            def _():
                for h in range(nh):
                    s_ref[h, g * tq:(g + 1) * tq, :] = jnp.full((tq, tq), NEG_BIG, F32)
                    mx_ref[per * h + g] = jnp.full((SUBLANES, tq), NEG_BIG, F32)

    def prep_stage(qi, j, first):
        i_blk = qi * tq // INPROJ_TM
        for h in range(nh):
            big_c = (crep_ref[0, 0, pl.ds(h * n_blk + i_blk, 1), :]
                     - crep_ref[0, 0, pl.ds(h * n_blk + j, 1), :])
            mx = mx_ref[per * h]
            for g in range(1, per):
                mx = jnp.maximum(mx, mx_ref[per * h + g])
            m_prev = jnp.where(first, NEG_BIG, m_ref[h])
            m_new = jnp.maximum(m_prev, jnp.max(mx, axis=0, keepdims=True) + big_c)
            mnext_ref[h] = m_new
            shift_ref[h] = m_new - big_c
            alpha_ref[h] = jnp.exp2(m_prev - m_new)

    def exp_stage():
        for h in range(nh):
            p_ref[h] = jnp.exp2(s_ref[h] - shift_ref[h]).astype(BF16)

    def pv_stage(j):
        k_off = pl.multiple_of(j * tk, tk)
        for h in range(nh):
            vT = jnp.concatenate([vT_ref[0, h * half:(h + 1) * half, pl.ds(k_off, tk)], sum_rows], axis=0)
            acc_ref[h] = alpha_ref[h] * acc_ref[h] + _dot(vT, p_ref[h])
            m_ref[h] = mnext_ref[h]

    def finalize(qi):
        outs = []
        for h in range(nh):
            acc = acc_ref[h]
            outs.append(acc[:half] / acc[half:half + 1])
        oT = jnp.concatenate(outs, axis=0)
        o_ref[0, pl.ds(pl.multiple_of(qi * tq, tq), tq), :] = oT.T.astype(o_ref.dtype)

    zero = jnp.int32(0)
    qaug_ref[...] = jnp.zeros_like(qaug_ref)
    build_qaug(zero)
    build_qaug(zero + 1)
    reset_state()
    store_scores(qk_dots(zero, zero))
    mask_stage(zero)
    prep_stage(zero, zero, True)

    def body(t, carry):
        qi, j = carry
        last = j == qi * tq // tk
        qi2 = jnp.where(last, qi + 1, qi)
        j2 = jnp.where(last, jstart_ref[g * nq + jnp.minimum(qi + 1, nq - 1)], j + 1)
        exp_stage()
        nxt = qk_dots(qi2, j2)
        pv_stage(j)
        store_scores(nxt)
        prep_stage(qi2, j2, last)

        @pl.when(j2 == qi2 * tq // tk)
        def _():
            mask_stage(qi2)
            prep_stage(qi2, j2, last)

        @pl.when(last)
        def _():
            finalize(qi)
            reset_state()

            @pl.when(qi + 2 < nq)
            def _():
                build_qaug(qi + 2)

        return qi2, j2

    qi, j = lax.fori_loop(0, nitems_ref[g] - 1, body, (jnp.int32(0), jnp.int32(0)))
    exp_stage()
    pv_stage(j)
    finalize(qi)


def _prune_plan(stats, drel, nq):
    B, n_blk = stats.shape[:2]
    per = FOX_TK // FOX_TQ
    c = stats[:, :, 0, :FOX_HEADS]
    qn = jnp.sqrt(stats[:, :, 1, :FOX_HEADS])
    kn = jnp.sqrt(stats[:, :, 2, :FOX_HEADS])
    c_next = jnp.concatenate([c[:, 1:], c[:, -1:]], axis=1)
    d_in = jnp.transpose(drel[:, :, FOX_TQ - 1::FOX_TQ], (0, 2, 1))
    d_in = jnp.concatenate([jnp.zeros_like(d_in[:, :1]), d_in[:, :-1]], axis=1)
    first_in_tile = (jnp.arange(nq) % per == 0)[None, :, None]
    d_before = jnp.repeat(c, per, axis=1) + jnp.where(first_in_tile, 0.0, d_in)
    qn_q = jnp.repeat(qn, per, axis=1)
    kn_q = jnp.repeat(kn, per, axis=1)
    bound = (d_before[:, :, None, :] - c_next[:, None, :, :]
             + NORM_SLACK * qn_q[:, :, None, :] * (kn[:, None, :, :] + kn_q[:, :, None, :]))
    i_idx = (jnp.arange(nq) // per)[:, None]
    j_idx = jnp.arange(n_blk)[None, :]
    skip = (bound <= PRUNE_LOG2) & (j_idx + 1 <= i_idx)[None, :, :, None]
    nh = FOX_GROUP_HEADS
    skip_pair = skip[..., 0::nh]
    for h in range(1, nh):
        skip_pair = skip_pair & skip[..., h::nh]
    jstart = jnp.min(jnp.where(skip_pair, n_blk, j_idx[None, :, :, None]), axis=2)
    jstart = jnp.transpose(jstart, (0, 2, 1))
    n_items = jnp.sum(i_idx[:, 0][None, None, :] - jstart + 1, axis=2)
    return jstart.reshape(-1).astype(jnp.int32), n_items.reshape(-1).astype(jnp.int32)


def _fox_attention(fqT, kaug, fvT, drel, stats):
    B, _, S = fqT.shape
    tq, tk = FOX_TQ, FOX_TK
    n_blk = S // INPROJ_TM
    nh = FOX_GROUP_HEADS
    n_groups = FOX_HEADS // nh
    drel4 = drel.reshape(B, n_groups, nh, S)
    c = stats[:, :, 0, :FOX_HEADS]
    c = jnp.transpose(c, (0, 2, 1)).reshape(B, n_groups, nh * n_blk, 1)
    crep = jnp.broadcast_to(c, (B, n_groups, nh * n_blk, tq))
    jstart, n_items = _prune_plan(stats, drel, S // tq)
    grid_spec = pltpu.PrefetchScalarGridSpec(
        num_scalar_prefetch=2,
        grid=(B, n_groups),
        in_specs=[
            pl.BlockSpec((1, nh * FOX_HEAD_DIM, S), lambda b, p, js, ni: (b, p, 0)),
            pl.BlockSpec((1, S, nh // 2 * MXU_DIM), lambda b, p, js, ni: (b, 0, p)),
            pl.BlockSpec((1, nh * FOX_HEAD_DIM, S), lambda b, p, js, ni: (b, p, 0)),
            pl.BlockSpec((1, 1, nh, S), lambda b, p, js, ni: (b, p, 0, 0)),
            pl.BlockSpec((1, 1, nh * n_blk, tq), lambda b, p, js, ni: (b, p, 0, 0)),
        ],
        out_specs=pl.BlockSpec((1, S, nh * FOX_HEAD_DIM), lambda b, p, js, ni: (b, 0, p)),
        scratch_shapes=[
            pltpu.VMEM((2 * nh, MXU_DIM, tq), BF16),
            pltpu.VMEM((nh, tk, tq), F32),
            pltpu.VMEM((nh, tk, tq), BF16),
            pltpu.VMEM((nh * (tk // tq), SUBLANES, tq), F32),
            pltpu.VMEM((nh, FOX_HEAD_DIM + FOX_SUM_ROWS, tq), F32),
            pltpu.VMEM((nh, 1, tq), F32),
            pltpu.VMEM((nh, 1, tq), F32),
            pltpu.VMEM((nh, 1, tq), F32),
            pltpu.VMEM((nh, 1, tq), F32),
        ])
    return pl.pallas_call(
        _fox_kernel,
        grid_spec=grid_spec,
        out_shape=jax.ShapeDtypeStruct((B, S, FOX_WIDTH), BF16),
        compiler_params=pltpu.CompilerParams(
            dimension_semantics=("arbitrary", "arbitrary"),
            vmem_limit_bytes=VMEM_LIMIT),
        name="fox_attn",
    )(jstart, n_items, fqT, kaug, fvT, drel4, crep)


def _gla_kernel(q_ref, k_ref, v_ref, r_ref, sm_ref, wa_ref, ba_ref, g_ref, o_ref, state_ref):
    nb, ts = q_ref.shape[0], q_ref.shape[1]
    C, P = GLA_CHUNK, GLA_PAIR
    dk, dv, H = GLA_KEY_DIM, GLA_VAL_DIM, GLA_HEADS
    KW = GLA_KEY_WIDTH

    @pl.when(pl.program_id(0) == 0)
    def _():
        state_ref[...] = jnp.zeros_like(state_ref)

    la_all = [_log_sigmoid(_dot(sm_ref[bb].astype(BF16), wa_ref[...]) + ba_ref[...])
              * (1.0 / GLA_GATE_TEMP) for bb in range(nb)]

    ri = lax.broadcasted_iota(jnp.int32, (P, P), 0)
    ci = lax.broadcasted_iota(jnp.int32, (P, P), 1)
    tril2 = jnp.where(((ri < C) == (ci < C)) & (ci <= ri), 1.0, 0.0).astype(BF16)

    rs = lax.broadcasted_iota(jnp.int32, (2 * H * C, P), 0)
    cs = lax.broadcasted_iota(jnp.int32, (2 * H * C, P), 1)
    causal = ((rs >= H * C) == (cs >= C)) & ((cs & (C - 1)) <= (rs & (C - 1)))

    lane_q = lax.broadcasted_iota(jnp.int32, (C, KW), 1)
    lane_t = lax.broadcasted_iota(jnp.int32, (KW, P), 1)
    first = lane_t < C

    per_b = ts // P
    n_slabs = nb * per_b
    slab_b = [k // per_b for k in range(n_slabs)]
    slab_rows = [slice((k % per_b) * P, (k % per_b + 1) * P) for k in range(n_slabs)]
    bs = [_tri_left(tril2, la_all[slab_b[k]][slab_rows[k]]) for k in range(n_slabs)]

    qms, scores, kv0s, kv1s, dec0s, dec1s, vss = [], [], [], [], [], [], []
    for pi in range(n_slabs):
        bb, rows = slab_b[pi], slab_rows[pi]
        b = bs[pi]
        bT = b.T
        bl0 = bT[:, C - 1:C]
        bl1 = bT[:, P - 1:P]
        q_dec = q_ref[bb, rows, :] * jnp.exp(b)
        kT = k_ref[bb, rows, :].T
        k_invT = (kT * jnp.exp(-bT)).astype(BF16)
        k_teT = kT * jnp.exp(jnp.where(first, bl0, bl1) - bT)
        zt = jnp.zeros_like(k_teT)
        k_te0 = jnp.where(first, k_teT, zt).astype(BF16)
        k_te1 = jnp.where(first, zt, k_teT).astype(BF16)
        dec0 = jnp.exp(bl0)
        dec1 = jnp.exp(bl1)

        vs = [v_ref[bb, rows, h * dv:(h + 1) * dv] for h in range(H)]
        kv = [_dot(jnp.concatenate([k_te0[h * dk:(h + 1) * dk, :], k_te1[h * dk:(h + 1) * dk, :]], axis=0),
                   vs[h]) for h in range(H)]
        kv0s.append(jnp.concatenate([kv[h][:dk] for h in range(H)], axis=0))
        kv1s.append(jnp.concatenate([kv[h][dk:] for h in range(H)], axis=0))
        dec0s.append(dec0)
        dec1s.append(dec1)
        vss.append(vs)

        zq = jnp.zeros((C, KW), F32)
        qm = [jnp.concatenate(
            [jnp.where((lane_q >= h * dk) & (lane_q < (h + 1) * dk), q_dec[c * C:(c + 1) * C], zq)
             for h in range(H)], axis=0).astype(BF16) for c in range(2)]
        a = _dot(jnp.concatenate(qm, axis=0), k_invT)
        scores.append(jnp.where(causal, a, 0.0).astype(BF16))
        qms.append(qm)

    o_inters = []
    for pi in range(n_slabs):
        if pi % per_b == 0:
            s = state_ref[slab_b[pi]]
        s1 = dec0s[pi] * s + kv0s[pi]
        o_inters.append([_dot(qms[pi][0], s.astype(BF16)), _dot(qms[pi][1], s1.astype(BF16))])
        s = dec1s[pi] * s1 + kv1s[pi]
        if pi % per_b == per_b - 1:
            state_ref[slab_b[pi]] = s

    for pi in range(n_slabs):
        bb, rows = slab_b[pi], slab_rows[pi]
        a, o_inter, vs = scores[pi], o_inters[pi], vss[pi]
        for h in range(H):
            hv = slice(h * dv, (h + 1) * dv)
            a_h = jnp.concatenate([a[(c * H + h) * C:(c * H + h + 1) * C] for c in range(2)], axis=0)
            o = _dot(a_h, vs[h]) + jnp.concatenate(
                [o_inter[c][h * C:(h + 1) * C] for c in range(2)], axis=0)
            o = o * lax.rsqrt(jnp.mean(o * o, axis=-1, keepdims=True) + RMS_EPS)
            rr = r_ref[bb, rows, hv]
            o = o * g_ref[:, hv] * (rr * (1.0 / (1.0 + jnp.exp(-rr))))
            o_ref[bb, rows, hv] = o.astype(o_ref.dtype)


def _gla_attention(gq, gk, gv, gr, small, w_a2_pad, b_a, g_gla):
    B, S, _ = gq.shape
    ts = GLA_TS
    tok = lambda w: pl.BlockSpec((B, ts, w), lambda s: (0, s, 0))
    full = lambda r, c: pl.BlockSpec((r, c), lambda s: (0, 0))
    return pl.pallas_call(
        _gla_kernel,
        grid=(S // ts,),
        in_specs=[
            tok(GLA_KEY_WIDTH), tok(GLA_KEY_WIDTH), tok(GLA_WIDTH), tok(GLA_WIDTH), tok(SMALL_W),
            full(SMALL_W, GLA_KEY_WIDTH), full(1, GLA_KEY_WIDTH), full(1, GLA_WIDTH),
        ],
        out_specs=tok(GLA_WIDTH),
        out_shape=jax.ShapeDtypeStruct((B, S, GLA_WIDTH), BF16),
        scratch_shapes=[pltpu.VMEM((B, GLA_KEY_WIDTH, GLA_VAL_DIM), F32)],
        compiler_params=pltpu.CompilerParams(
            dimension_semantics=("arbitrary",), vmem_limit_bytes=VMEM_LIMIT),
        name="gla_attn",
    )(gq, gk, gv, gr, small, w_a2_pad, b_a, g_gla)


def _layer_norm(z, g, b):
    mu = jnp.mean(z, axis=-1, keepdims=True)
    d = z - mu
    var = jnp.mean(d * d, axis=-1, keepdims=True)
    return d * lax.rsqrt(var + LN_EPS) * g + b


def _tail_kernel(x_ref, fox_ref, gla_ref, mod_ref, wof_ref, wog_ref, ln1g_ref, ln1b_ref,
                 wg_ref, wu_ref, wd_ref, ln2g_ref, ln2b_ref, o_ref):
    m = mod_ref[0]
    tm = x_ref.shape[0]
    halves = [slice(0, tm // 2), slice(tm // 2, tm)]
    ys = [_dot(fox_ref[r, :], wof_ref[...]) + _dot(gla_ref[r, :], wog_ref[...]) for r in halves]
    x1s, gs, ups = [], [], []
    for r, y in zip(halves, ys):
        x1 = _layer_norm(DEEPNORM_ALPHA * x_ref[r, :] + (1.0 + m[2:3]) * y,
                         ln1g_ref[...], ln1b_ref[...])
        u2 = (x1 * (1.0 + m[4:5]) + m[3:4]).astype(BF16)
        x1s.append(x1)
        gs.append(_dot(u2, wg_ref[...]))
        ups.append(_dot(u2, wu_ref[...]))
    y2s = []
    for g, up in zip(gs, ups):
        h = (g * (1.0 / (1.0 + jnp.exp(-g))) * up).astype(BF16)
        y2s.append(_dot(h, wd_ref[...]))
    for r, x1, y2 in zip(halves, x1s, y2s):
        o_ref[r, :] = _layer_norm(DEEPNORM_ALPHA * x1 + (1.0 + m[5:6]) * y2,
                                  ln2g_ref[...], ln2b_ref[...])


def _tail(x2d, fox2d, gla2d, mod, wo, ln1g, ln1b, wg, wu, wd, ln2g, ln2b, steps_per_batch):
    T, D = x2d.shape
    tm = TAIL_TM
    const = lambda r, c: pl.BlockSpec((r, c), lambda i: (0, 0), pipeline_mode=pl.Buffered(1))
    tok = lambda w: pl.BlockSpec((tm, w), lambda i: (i, 0))
    return pl.pallas_call(
        _tail_kernel,
        grid=(T // tm,),
        in_specs=[
            tok(D), tok(FOX_WIDTH), tok(GLA_WIDTH),
            pl.BlockSpec((1, N_MOD, D), lambda i: (i // steps_per_batch, 0, 0)),
            pl.BlockSpec((FOX_WIDTH, D), lambda i: (0, 0), pipeline_mode=pl.Buffered(1)),
            pl.BlockSpec((GLA_WIDTH, D), lambda i: (1, 0), pipeline_mode=pl.Buffered(1)),
            const(1, D), const(1, D),
            const(D, D_FF), const(D, D_FF), const(D_FF, D), const(1, D), const(1, D),
        ],
        out_specs=tok(D),
        out_shape=jax.ShapeDtypeStruct((T, D), F32),
        compiler_params=pltpu.CompilerParams(
            dimension_semantics=("arbitrary",), vmem_limit_bytes=VMEM_LIMIT),
        name="tail",
    )(x2d, fox2d, gla2d, mod, wo, wo, ln1g, ln1b, wg, wu, wd, ln2g, ln2b)


def kernel(x, c, w_c, b_c, w_in, b_f, w_a2, b_a, g_gla, w_o, ln1_g, ln1_b,
           w_gate, w_up, w_down, ln2_g, ln2_b):
    B, S, D = x.shape
    assert D == D_MODEL and S % INPROJ_TM == 0 and INPROJ_TM == FOX_TK and FOX_TK % FOX_TQ == 0

    b_f_pad = jnp.zeros((1, SMALL_W), F32).at[0, :FOX_HEADS].set(b_f)
    w_a2_pad = jnp.zeros((SMALL_W, GLA_KEY_WIDTH), F32) \
        .at[FOX_HEADS:FOX_HEADS + GLA_GATE_RANK].set(w_a2).astype(BF16)

    mod = _modulation(c, w_c, b_c)

    fqT, kaug, fvT, gq, gk, gv, gr, small, drel, stats, wg16, wu16, wd16 = _in_projection(
        x, mod, w_in, b_f_pad, w_gate, w_up, w_down)
    fox = _fox_attention(fqT, kaug, fvT, drel, stats)
    gla = _gla_attention(gq, gk, gv, gr, small, w_a2_pad,
                         b_a.reshape(1, -1), g_gla.reshape(1, -1))

    T = B * S
    w_o_b = w_o.astype(BF16)
    out = _tail(x.reshape(T, D), fox.reshape(T, FOX_WIDTH), gla.reshape(T, GLA_WIDTH), mod,
                w_o_b,
                ln1_g.reshape(1, D), ln1_b.reshape(1, D),
                wg16, wu16, wd16,
                ln2_g.reshape(1, D), ln2_b.reshape(1, D), S // TAIL_TM)
    return out.reshape(B, S, D)
```

```python
import functools
import math

import numpy as np
import jax
import jax.numpy as jnp
from jax import lax
from jax.experimental import pallas as pl
from jax.experimental.pallas import tpu as pltpu

F32 = jnp.float32
BF16 = jnp.bfloat16

D_MODEL = 1024
FOX_WIDTH = 512
FOX_HEAD_DIM = 64
FOX_HEADS = 8
FOX_PAIRS = FOX_HEADS // 2
GLA_WIDTH = 512
GLA_HEADS = 4
GLA_VAL_DIM = 128
GLA_KEY_DIM = 64
GLA_KEY_WIDTH = 256
GLA_GATE_RANK = 16
GLA_GATE_TEMP = 16.0
GLA_CHUNK = 64
D_FF = 2816
LN_EPS = 1e-5
RMS_EPS = 1e-6
N_MOD = 6
DEEPNORM_ALPHA = 2.0 ** 0.25
IN_SPLITS = (FOX_WIDTH, FOX_WIDTH, FOX_WIDTH, FOX_HEADS, GLA_KEY_WIDTH, GLA_KEY_WIDTH,
             GLA_WIDTH, GLA_GATE_RANK, GLA_WIDTH)
LOG2E = math.log2(math.e)

LANES = 128
SUBLANES = 8
MXU_DIM = 256
SMALL_W = LANES
VMEM_LIMIT = 56 * 1024 * 1024

MOD_TN = 768
INPROJ_TM = 512
FOX_TQ = 256
FOX_GROUP_HEADS = 4
FOX_SUM_ROWS = 16
FOX_TK = 512
GLA_TS = 512
GLA_PAIR = 2 * GLA_CHUNK
TAIL_TM = 512

NEG_BIG = -1e30
PRUNE_LOG2 = -1080.0
NORM_SLACK = 1.02

AUG_ROWS = 32


def _log_sigmoid(z):
    return jnp.minimum(z, 0.0) - jnp.log(1.0 + jnp.exp(-jnp.abs(z)))


def _split3(a):
    hi = a.astype(BF16)
    r1 = a - hi.astype(F32)
    mid = r1.astype(BF16)
    lo = (r1 - mid.astype(F32)).astype(BF16)
    return hi, mid, lo


def _dot(a, b):
    return jnp.dot(a, b, preferred_element_type=F32)


def _tri_left(tri, a):
    hi, mid, lo = _split3(a)
    return _dot(tri, hi) + _dot(tri, mid) + _dot(tri, lo)


def _tri_right(a, tri):
    hi, mid, lo = _split3(a)
    return _dot(hi, tri) + _dot(mid, tri) + _dot(lo, tri)


def _mod_kernel(cT_ref, w_ref, b_ref, win_ref, o_ref, wall_ref, *, n_rows):
    w = w_ref[...]
    outs = [jnp.sum(w * cT_ref[:, b:b + 1], axis=0, keepdims=True) for b in range(n_rows)]
    outs.append(jnp.zeros((o_ref.shape[0] - n_rows, w.shape[1]), F32))
    o_ref[...] = jnp.concatenate(outs, axis=0) + b_ref[...]

    for dst, src, width in _W_GROUPS:
        wall_ref[:, dst:dst + width] = win_ref[:, src:src + width].astype(BF16)
    small_w = jnp.concatenate(
        [win_ref[:, _SRC_FF:_SRC_FF + FOX_HEADS], win_ref[:, _SRC_GA:_SRC_GA + GLA_GATE_RANK],
         jnp.zeros((win_ref.shape[0], 2 * SMALL_W - FOX_HEADS - GLA_GATE_RANK), F32)], axis=1)
    wall_ref[:, _C_SM:_C_SM + 2 * SMALL_W] = small_w.astype(BF16)


def _modulation(c, w_c, b_c, w_in):
    B, D = c.shape
    N = w_c.shape[1]
    rows = SUBLANES
    n_steps = N // MOD_TN
    w_rows = w_in.shape[0] // n_steps
    cT_pad = jnp.zeros((D, LANES), F32).at[:, :B].set(c.T)
    out, w_all = pl.pallas_call(
        functools.partial(_mod_kernel, n_rows=B),
        grid=(n_steps,),
        in_specs=[
            pl.BlockSpec((D, LANES), lambda j: (0, 0)),
            pl.BlockSpec((D, MOD_TN), lambda j: (0, j)),
            pl.BlockSpec((1, MOD_TN), lambda j: (0, j)),
            pl.BlockSpec((w_rows, w_in.shape[1]), lambda j: (j, 0)),
        ],
        out_specs=[pl.BlockSpec((rows, MOD_TN), lambda j: (0, j)),
                   pl.BlockSpec((w_rows, _W_COLS), lambda j: (j, 0))],
        out_shape=[jax.ShapeDtypeStruct((rows, N), F32),
                   jax.ShapeDtypeStruct((w_in.shape[0], _W_COLS), BF16)],
        compiler_params=pltpu.CompilerParams(
            dimension_semantics=("arbitrary",), vmem_limit_bytes=VMEM_LIMIT),
        name="adaln_mod",
    )(cT_pad, w_c, b_c.reshape(1, N), w_in)
    return out[:B].reshape(B, N_MOD, D), w_all


_C_FQ, _C_FK, _C_FV = 0, 512, 1024
_C_GQ, _C_SM, _C_GK, _C_GV, _C_GR = 1536, 1792, 2048, 2304, 2816
_W_COLS = _C_GR + GLA_WIDTH
_SRC = [sum(IN_SPLITS[:i]) for i in range(len(IN_SPLITS))]
_SRC_FF, _SRC_GA = _SRC[3], _SRC[7]
_W_GROUPS = ((_C_FQ, _SRC[0], FOX_WIDTH), (_C_FK, _SRC[1], FOX_WIDTH), (_C_FV, _SRC[2], FOX_WIDTH),
             (_C_GQ, _SRC[4], GLA_KEY_WIDTH), (_C_GK, _SRC[5], GLA_KEY_WIDTH),
             (_C_GV, _SRC[6], GLA_WIDTH), (_C_GR, _SRC[8], GLA_WIDTH))
_KAUG_W = FOX_PAIRS * MXU_DIM


def _head_indicator():
    ind = np.zeros((FOX_WIDTH, MXU_DIM), np.float32)
    for h in range(FOX_HEADS):
        ind[h * FOX_HEAD_DIM:(h + 1) * FOX_HEAD_DIM, h] = 1.0
    return ind


def _inproj_kernel(x_ref, mod_ref, w_ref, bf_ref, ind_ref, wg32_ref, wu32_ref, wd32_ref,
                   fqT_ref, kaug_ref, fvT_ref, gq_ref, gk_ref, gv_ref, gr_ref, sm_ref,
                   drel_ref, stats_ref, wg16_ref, wu16_ref, wd16_ref, carry_ref):
    tm = x_ref.shape[1]

    wg16_ref[...] = wg32_ref[...].astype(BF16)
    wu16_ref[...] = wu32_ref[...].astype(BF16)
    wd16_ref[...] = wd32_ref[...].astype(BF16)

    @pl.when(pl.program_id(1) == 0)
    def _():
        carry_ref[...] = jnp.zeros_like(carry_ref)

    m = mod_ref[0]
    u = (x_ref[0] * (1.0 + m[1:2]) + m[0:1]).astype(BF16)

    def proj(lo, width):
        return _dot(u, w_ref[:, lo:lo + width])

    gq_small = proj(_C_GQ, 2 * GLA_KEY_WIDTH)
    gq_ref[0] = gq_small[:, :GLA_KEY_WIDTH] * (GLA_KEY_DIM ** -0.5)
    small = gq_small[:, GLA_KEY_WIDTH:GLA_KEY_WIDTH + SMALL_W]
    sm_ref[0] = small

    fq = proj(_C_FQ, FOX_WIDTH) * (FOX_HEAD_DIM ** -0.5 * LOG2E)
    fqT_ref[0] = fq.T.astype(BF16)
    fk32 = proj(_C_FK, FOX_WIDTH)
    fk = fk32.astype(BF16)
    qn2 = jnp.max(_dot((fq * fq).astype(BF16), ind_ref[...])[:, :SMALL_W], axis=0, keepdims=True)
    kn2 = jnp.max(_dot((fk32 * fk32).astype(BF16), ind_ref[...])[:, :SMALL_W], axis=0, keepdims=True)

    zT = (small + bf_ref[...]).T[:FOX_HEADS, :]
    lfT = _log_sigmoid(zT) * LOG2E
    r = lax.broadcasted_iota(jnp.int32, (tm, tm), 0)
    c = lax.broadcasted_iota(jnp.int32, (tm, tm), 1)
    triu = jnp.where(r <= c, 1.0, 0.0).astype(BF16)
    drelT = _tri_right(lfT, triu)
    drel_ref[0] = drelT

    c_before = carry_ref[...]
    srow = lax.broadcasted_iota(jnp.int32, (SUBLANES, SMALL_W), 0)
    slane = lax.broadcasted_iota(jnp.int32, (SUBLANES, SMALL_W), 1)
    total = jnp.sum(jnp.where(srow == slane, drelT[:, tm - 1:tm], 0.0), axis=0, keepdims=True)
    carry_ref[...] = c_before + total

    hi, mid, lo = _split3(drelT)
    ones3 = jnp.where(lax.broadcasted_iota(jnp.int32, (SUBLANES, tm), 0) < 3, 1.0, 0.0)
    slabT = jnp.concatenate(
        [-hi.astype(F32), -mid.astype(F32), -lo.astype(F32), ones3,
         jnp.zeros((LANES - 4 * SUBLANES, tm), F32)], axis=0)
    slab = slabT.T.astype(BF16)

    fvT_ref[0] = proj(_C_FV, FOX_WIDTH).T.astype(BF16)
    gv_ref[0] = proj(_C_GV, GLA_WIDTH).astype(BF16)
    gr_ref[0] = proj(_C_GR, GLA_WIDTH)
    gk_ref[0] = proj(_C_GK, GLA_KEY_WIDTH)

    stats_ref[0, 0] = jnp.where(srow == 0, c_before,
                                jnp.where(srow == 1, qn2, jnp.where(srow == 2, kn2, 0.0)))
    for p in range(FOX_PAIRS):
        kaug_ref[0, :, p * MXU_DIM:p * MXU_DIM + LANES] = fk[:, p * LANES:(p + 1) * LANES]
        kaug_ref[0, :, p * MXU_DIM + LANES:(p + 1) * MXU_DIM] = slab


def _in_projection(x, mod, w_all, b_f_pad, w_gate, w_up, w_down):
    B, S, D = x.shape
    tm = INPROJ_TM
    n_steps = B * (S // tm)
    steps_b = S // tm
    up_rows = D // n_steps
    down_rows = 2 * D_FF // n_steps
    assert D % n_steps == 0 and up_rows % 16 == 0 and (2 * D_FF) % n_steps == 0 and down_rows % 16 == 0
    up_spec = pl.BlockSpec((up_rows, D_FF), lambda b, s: (b * steps_b + s, 0))
    down_spec = pl.BlockSpec((down_rows, D), lambda b, s: ((b * steps_b + s) // 2, 0))
    ind = _head_indicator()
    tok = lambda w: pl.BlockSpec((1, tm, w), lambda b, s: (b, s, 0))
    tokT = lambda h: pl.BlockSpec((1, h, tm), lambda b, s: (b, 0, s))
    outs = pl.pallas_call(
        _inproj_kernel,
        grid=(B, S // tm),
        in_specs=[
            tok(D),
            pl.BlockSpec((1, N_MOD, D), lambda b, s: (b, 0, 0)),
            pl.BlockSpec((D, _W_COLS), lambda b, s: (0, 0), pipeline_mode=pl.Buffered(1)),
            pl.BlockSpec((1, SMALL_W), lambda b, s: (0, 0)),
            pl.BlockSpec(ind.shape, lambda b, s: (0, 0)),
            up_spec, up_spec, down_spec,
        ],
        out_specs=[
            tokT(FOX_WIDTH), tok(_KAUG_W), tokT(FOX_WIDTH),
            tok(GLA_KEY_WIDTH), tok(GLA_KEY_WIDTH), tok(GLA_WIDTH), tok(GLA_WIDTH),
            tok(SMALL_W),
            tokT(FOX_HEADS),
            pl.BlockSpec((1, 1, SUBLANES, SMALL_W), lambda b, s: (b, s, 0, 0)),
            up_spec, up_spec, down_spec,
        ],
        out_shape=[
            jax.ShapeDtypeStruct((B, FOX_WIDTH, S), BF16),
            jax.ShapeDtypeStruct((B, S, _KAUG_W), BF16),
            jax.ShapeDtypeStruct((B, FOX_WIDTH, S), BF16),
            jax.ShapeDtypeStruct((B, S, GLA_KEY_WIDTH), F32),
            jax.ShapeDtypeStruct((B, S, GLA_KEY_WIDTH), F32),
            jax.ShapeDtypeStruct((B, S, GLA_WIDTH), BF16),
            jax.ShapeDtypeStruct((B, S, GLA_WIDTH), F32),
            jax.ShapeDtypeStruct((B, S, SMALL_W), F32),
            jax.ShapeDtypeStruct((B, FOX_HEADS, S), F32),
            jax.ShapeDtypeStruct((B, S // tm, SUBLANES, SMALL_W), F32),
            jax.ShapeDtypeStruct(w_gate.shape, BF16),
            jax.ShapeDtypeStruct(w_up.shape, BF16),
            jax.ShapeDtypeStruct(w_down.shape, BF16),
        ],
        scratch_shapes=[pltpu.VMEM((1, SMALL_W), F32)],
        compiler_params=pltpu.CompilerParams(
            dimension_semantics=("arbitrary", "arbitrary"), vmem_limit_bytes=VMEM_LIMIT),
        name="in_proj",
    )(x, mod, w_all, b_f_pad, jnp.asarray(ind, BF16), w_gate, w_up, w_down)
    return outs


def _fox_kernel(jstart_ref, nitems_ref, qT_ref, kaug_ref, vT_ref, drel_ref, crep_ref, o_ref,
                qaug_ref, s_ref, p_ref, mx_ref, acc_ref, m_ref, mnext_ref, shift_ref, alpha_ref):
    tq, tk = FOX_TQ, FOX_TK
    half = FOX_HEAD_DIM
    S = kaug_ref.shape[1]
    nq = S // tq
    nh = FOX_GROUP_HEADS
    n_blk = crep_ref.shape[2] // nh
    g = pl.program_id(0) * pl.num_programs(1) + pl.program_id(1)

    def build_qaug(qi):
        q0 = pl.multiple_of(qi * tq, tq)
        dr = drel_ref[0, 0, :, pl.ds(q0, tq)]
        row = lax.broadcasted_iota(jnp.int32, (2 * half, tq), 0)
        r32 = lax.broadcasted_iota(jnp.int32, (AUG_ROWS, tq), 0)
        for h in range(nh):
            buf = (qi % 2) * nh + h
            qT = qT_ref[0, (h // 2) * 2 * half:(h // 2 + 1) * 2 * half, pl.ds(q0, tq)]
            own = (row < half) if h % 2 == 0 else (row >= half)
            qaug_ref[buf, 0:2 * half, :] = jnp.where(own, qT, jnp.zeros_like(qT))
            hi, mid, lo = _split3(dr[h:h + 1])
            gh = nh * pl.program_id(1) + h
            pick = (r32 == gh) | (r32 == gh + SUBLANES) | (r32 == gh + 2 * SUBLANES)
            slab = jnp.where(r32 == 24, hi.astype(F32),
                             jnp.where(r32 == 25, mid.astype(F32),
                                       jnp.where(r32 == 26, lo.astype(F32),
                                                 jnp.where(pick, 1.0, 0.0))))
            qaug_ref[buf, 2 * half:2 * half + AUG_ROWS, :] = slab.astype(BF16)

    def reset_state():
        acc_ref[...] = jnp.zeros_like(acc_ref)

    sum_rows = jnp.where(lax.broadcasted_iota(jnp.int32, (FOX_SUM_ROWS, tk), 0) == 0, 1.0, 0.0).astype(BF16)

    per = tk // tq

    def col_max8(s):
        return jnp.max(s.reshape(s.shape[0] // SUBLANES, SUBLANES, tq), axis=0)

    def qk_dots(qi, j):
        k_off = pl.multiple_of(j * tk, tk)
        kts = [kaug_ref[0, pl.ds(k_off, tk), p * MXU_DIM:(p + 1) * MXU_DIM] for p in range(nh // 2)]
        return [_dot(kts[h // 2], qaug_ref[(qi % 2) * nh + h]) for h in range(nh)]

    def store_scores(scores):
        for h in range(nh):
            s_ref[h] = scores[h]
            for g in range(per):
                mx_ref[per * h + g] = col_max8(scores[h][g * tq:(g + 1) * tq])

    def mask_stage(qi):
        d = qi % per
        d0 = pl.multiple_of(d * tq, tq)
        kr = lax.broadcasted_iota(jnp.int32, (tq, tq), 0)
        qc = lax.broadcasted_iota(jnp.int32, (tq, tq), 1)
        for h in range(nh):
            tri = jnp.where(kr <= qc, s_ref[h, pl.ds(d0, tq), :], NEG_BIG)
            s_ref[h, pl.ds(d0, tq), :] = tri
            mx_ref[per * h + d] = col_max8(tri)
        for g in range(1, per):
            @pl.when(d < g)
            def _():
                for h in range(nh):
                    s_ref[h, g * tq:(g + 1) * tq, :] = jnp.full((tq, tq), NEG_BIG, F32)
                    mx_ref[per * h + g] = jnp.full((SUBLANES, tq), NEG_BIG, F32)

    def prep_stage(qi, j, first):
        i_blk = qi * tq // INPROJ_TM
        for h in range(nh):
            big_c = (crep_ref[0, 0, pl.ds(h * n_blk + i_blk, 1), :]
                     - crep_ref[0, 0, pl.ds(h * n_blk + j, 1), :])
            mx = mx_ref[per * h]
            for g in range(1, per):
                mx = jnp.maximum(mx, mx_ref[per * h + g])
            m_prev = jnp.where(first, NEG_BIG, m_ref[h])
            m_new = jnp.maximum(m_prev, jnp.max(mx, axis=0, keepdims=True) + big_c)
            mnext_ref[h] = m_new
            shift_ref[h] = m_new - big_c
            alpha_ref[h] = jnp.exp2(m_prev - m_new)

    def exp_stage():
        for h in range(nh):
            p_ref[h] = jnp.exp2(s_ref[h] - shift_ref[h]).astype(BF16)

    def pv_stage(j):
        k_off = pl.multiple_of(j * tk, tk)
        for h in range(nh):
            vT = jnp.concatenate([vT_ref[0, h * half:(h + 1) * half, pl.ds(k_off, tk)], sum_rows], axis=0)
            acc_ref[h] = alpha_ref[h] * acc_ref[h] + _dot(vT, p_ref[h])
            m_ref[h] = mnext_ref[h]

    def finalize(qi):
        outs = []
        for h in range(nh):
            acc = acc_ref[h]
            outs.append(acc[:half] / acc[half:half + 1])
        oT = jnp.concatenate(outs, axis=0)
        o_ref[0, pl.ds(pl.multiple_of(qi * tq, tq), tq), :] = oT.T.astype(o_ref.dtype)

    zero = jnp.int32(0)
    qaug_ref[...] = jnp.zeros_like(qaug_ref)
    build_qaug(zero)
    build_qaug(zero + 1)
    reset_state()
    store_scores(qk_dots(zero, zero))
    mask_stage(zero)
    prep_stage(zero, zero, True)

    def body(t, carry):
        qi, j = carry
        last = j == qi * tq // tk
        qi2 = jnp.where(last, qi + 1, qi)
        j2 = jnp.where(last, jstart_ref[g * nq + jnp.minimum(qi + 1, nq - 1)], j + 1)
        exp_stage()
        nxt = qk_dots(qi2, j2)
        pv_stage(j)
        store_scores(nxt)
        prep_stage(qi2, j2, last)

        @pl.when(j2 == qi2 * tq // tk)
        def _():
            mask_stage(qi2)
            prep_stage(qi2, j2, last)

        @pl.when(last)
        def _():
            finalize(qi)
            reset_state()

            @pl.when(qi + 2 < nq)
            def _():
                build_qaug(qi + 2)

        return qi2, j2

    qi, j = lax.fori_loop(0, nitems_ref[g] - 1, body, (jnp.int32(0), jnp.int32(0)))
    exp_stage()
    pv_stage(j)
    finalize(qi)


def _prune_plan(stats, drel, nq):
    B, n_blk = stats.shape[:2]
    per = FOX_TK // FOX_TQ
    c = stats[:, :, 0, :FOX_HEADS]
    qn = jnp.sqrt(stats[:, :, 1, :FOX_HEADS])
    kn = jnp.sqrt(stats[:, :, 2, :FOX_HEADS])
    c_next = jnp.concatenate([c[:, 1:], c[:, -1:]], axis=1)
    d_in = jnp.transpose(drel[:, :, FOX_TQ - 1::FOX_TQ], (0, 2, 1))
    d_in = jnp.concatenate([jnp.zeros_like(d_in[:, :1]), d_in[:, :-1]], axis=1)
    first_in_tile = (jnp.arange(nq) % per == 0)[None, :, None]
    d_before = jnp.repeat(c, per, axis=1) + jnp.where(first_in_tile, 0.0, d_in)
    qn_q = jnp.repeat(qn, per, axis=1)
    kn_q = jnp.repeat(kn, per, axis=1)
    bound = (d_before[:, :, None, :] - c_next[:, None, :, :]
             + NORM_SLACK * qn_q[:, :, None, :] * (kn[:, None, :, :] + kn_q[:, :, None, :]))
    i_idx = (jnp.arange(nq) // per)[:, None]
    j_idx = jnp.arange(n_blk)[None, :]
    skip = (bound <= PRUNE_LOG2) & (j_idx + 1 <= i_idx)[None, :, :, None]
    nh = FOX_GROUP_HEADS
    skip_pair = skip[..., 0::nh]
    for h in range(1, nh):
        skip_pair = skip_pair & skip[..., h::nh]
    jstart = jnp.min(jnp.where(skip_pair, n_blk, j_idx[None, :, :, None]), axis=2)
    jstart = jnp.transpose(jstart, (0, 2, 1))
    n_items = jnp.sum(i_idx[:, 0][None, None, :] - jstart + 1, axis=2)
    return jstart.reshape(-1).astype(jnp.int32), n_items.reshape(-1).astype(jnp.int32)


def _fox_attention(fqT, kaug, fvT, drel, stats):
    B, _, S = fqT.shape
    tq, tk = FOX_TQ, FOX_TK
    n_blk = S // INPROJ_TM
    nh = FOX_GROUP_HEADS
    n_groups = FOX_HEADS // nh
    drel4 = drel.reshape(B, n_groups, nh, S)
    c = stats[:, :, 0, :FOX_HEADS]
    c = jnp.transpose(c, (0, 2, 1)).reshape(B, n_groups, nh * n_blk, 1)
    crep = jnp.broadcast_to(c, (B, n_groups, nh * n_blk, tq))
    jstart, n_items = _prune_plan(stats, drel, S // tq)
    grid_spec = pltpu.PrefetchScalarGridSpec(
        num_scalar_prefetch=2,
        grid=(B, n_groups),
        in_specs=[
            pl.BlockSpec((1, nh * FOX_HEAD_DIM, S), lambda b, p, js, ni: (b, p, 0)),
            pl.BlockSpec((1, S, nh // 2 * MXU_DIM), lambda b, p, js, ni: (b, 0, p)),
            pl.BlockSpec((1, nh * FOX_HEAD_DIM, S), lambda b, p, js, ni: (b, p, 0)),
            pl.BlockSpec((1, 1, nh, S), lambda b, p, js, ni: (b, p, 0, 0)),
            pl.BlockSpec((1, 1, nh * n_blk, tq), lambda b, p, js, ni: (b, p, 0, 0)),
        ],
        out_specs=pl.BlockSpec((1, S, nh * FOX_HEAD_DIM), lambda b, p, js, ni: (b, 0, p)),
        scratch_shapes=[
            pltpu.VMEM((2 * nh, MXU_DIM, tq), BF16),
            pltpu.VMEM((nh, tk, tq), F32),
            pltpu.VMEM((nh, tk, tq), BF16),
            pltpu.VMEM((nh * (tk // tq), SUBLANES, tq), F32),
            pltpu.VMEM((nh, FOX_HEAD_DIM + FOX_SUM_ROWS, tq), F32),
            pltpu.VMEM((nh, 1, tq), F32),
            pltpu.VMEM((nh, 1, tq), F32),
            pltpu.VMEM((nh, 1, tq), F32),
            pltpu.VMEM((nh, 1, tq), F32),
        ])
    return pl.pallas_call(
        _fox_kernel,
        grid_spec=grid_spec,
        out_shape=jax.ShapeDtypeStruct((B, S, FOX_WIDTH), BF16),
        compiler_params=pltpu.CompilerParams(
            dimension_semantics=("arbitrary", "arbitrary"),
            vmem_limit_bytes=VMEM_LIMIT),
        name="fox_attn",
    )(jstart, n_items, fqT, kaug, fvT, drel4, crep)


def _gla_kernel(q_ref, k_ref, v_ref, r_ref, sm_ref, wa_ref, ba_ref, g_ref, o_ref, state_ref):
    nb, ts = q_ref.shape[0], q_ref.shape[1]
    C, P = GLA_CHUNK, GLA_PAIR
    dk, dv, H = GLA_KEY_DIM, GLA_VAL_DIM, GLA_HEADS
    KW = GLA_KEY_WIDTH

    @pl.when(pl.program_id(0) == 0)
    def _():
        state_ref[...] = jnp.zeros_like(state_ref)

    la_all = [_log_sigmoid(_dot(sm_ref[bb].astype(BF16), wa_ref[...]) + ba_ref[...])
              * (1.0 / GLA_GATE_TEMP) for bb in range(nb)]

    ri = lax.broadcasted_iota(jnp.int32, (P, P), 0)
    ci = lax.broadcasted_iota(jnp.int32, (P, P), 1)
    tril2 = jnp.where(((ri < C) == (ci < C)) & (ci <= ri), 1.0, 0.0).astype(BF16)

    rs = lax.broadcasted_iota(jnp.int32, (2 * H * C, P), 0)
    cs = lax.broadcasted_iota(jnp.int32, (2 * H * C, P), 1)
    causal = ((rs >= H * C) == (cs >= C)) & ((cs & (C - 1)) <= (rs & (C - 1)))

    lane_q = lax.broadcasted_iota(jnp.int32, (C, KW), 1)
    lane_t = lax.broadcasted_iota(jnp.int32, (KW, P), 1)
    first = lane_t < C

    per_b = ts // P
    n_slabs = nb * per_b
    slab_b = [k // per_b for k in range(n_slabs)]
    slab_rows = [slice((k % per_b) * P, (k % per_b + 1) * P) for k in range(n_slabs)]
    bs = [_tri_left(tril2, la_all[slab_b[k]][slab_rows[k]]) for k in range(n_slabs)]

    qms, scores, kv0s, kv1s, dec0s, dec1s, vss = [], [], [], [], [], [], []
    for pi in range(n_slabs):
        bb, rows = slab_b[pi], slab_rows[pi]
        b = bs[pi]
        bT = b.T
        bl0 = bT[:, C - 1:C]
        bl1 = bT[:, P - 1:P]
        q_dec = q_ref[bb, rows, :] * jnp.exp(b)
        kT = k_ref[bb, rows, :].T
        k_invT = (kT * jnp.exp(-bT)).astype(BF16)
        k_teT = kT * jnp.exp(jnp.where(first, bl0, bl1) - bT)
        zt = jnp.zeros_like(k_teT)
        k_te0 = jnp.where(first, k_teT, zt).astype(BF16)
        k_te1 = jnp.where(first, zt, k_teT).astype(BF16)
        dec0 = jnp.exp(bl0)
        dec1 = jnp.exp(bl1)

        vs = [v_ref[bb, rows, h * dv:(h + 1) * dv] for h in range(H)]
        kv = [_dot(jnp.concatenate([k_te0[h * dk:(h + 1) * dk, :], k_te1[h * dk:(h + 1) * dk, :]], axis=0),
                   vs[h]) for h in range(H)]
        kv0s.append(jnp.concatenate([kv[h][:dk] for h in range(H)], axis=0))
        kv1s.append(jnp.concatenate([kv[h][dk:] for h in range(H)], axis=0))
        dec0s.append(dec0)
        dec1s.append(dec1)
        vss.append(vs)

        zq = jnp.zeros((C, KW), F32)
        qm = [jnp.concatenate(
            [jnp.where((lane_q >= h * dk) & (lane_q < (h + 1) * dk), q_dec[c * C:(c + 1) * C], zq)
             for h in range(H)], axis=0).astype(BF16) for c in range(2)]
        a = _dot(jnp.concatenate(qm, axis=0), k_invT)
        scores.append(jnp.where(causal, a, 0.0).astype(BF16))
        qms.append(qm)

    o_inters = []
    for pi in range(n_slabs):
        if pi % per_b == 0:
            s = state_ref[slab_b[pi]]
        s1 = dec0s[pi] * s + kv0s[pi]
        o_inters.append([_dot(qms[pi][0], s.astype(BF16)), _dot(qms[pi][1], s1.astype(BF16))])
        s = dec1s[pi] * s1 + kv1s[pi]
        if pi % per_b == per_b - 1:
            state_ref[slab_b[pi]] = s

    for pi in range(n_slabs):
        bb, rows = slab_b[pi], slab_rows[pi]
        a, o_inter, vs = scores[pi], o_inters[pi], vss[pi]
        for h in range(H):
            hv = slice(h * dv, (h + 1) * dv)
            a_h = jnp.concatenate([a[(c * H + h) * C:(c * H + h + 1) * C] for c in range(2)], axis=0)
            o = _dot(a_h, vs[h]) + jnp.concatenate(
                [o_inter[c][h * C:(h + 1) * C] for c in range(2)], axis=0)
            o = o * lax.rsqrt(jnp.mean(o * o, axis=-1, keepdims=True) + RMS_EPS)
            rr = r_ref[bb, rows, hv]
            o = o * g_ref[:, hv] * (rr * (1.0 / (1.0 + jnp.exp(-rr))))
            o_ref[bb, rows, hv] = o.astype(o_ref.dtype)


def _gla_attention(gq, gk, gv, gr, small, w_a2_pad, b_a, g_gla):
    B, S, _ = gq.shape
    ts = GLA_TS
    tok = lambda w: pl.BlockSpec((B, ts, w), lambda s: (0, s, 0))
    full = lambda r, c: pl.BlockSpec((r, c), lambda s: (0, 0))
    return pl.pallas_call(
        _gla_kernel,
        grid=(S // ts,),
        in_specs=[
            tok(GLA_KEY_WIDTH), tok(GLA_KEY_WIDTH), tok(GLA_WIDTH), tok(GLA_WIDTH), tok(SMALL_W),
            full(SMALL_W, GLA_KEY_WIDTH), full(1, GLA_KEY_WIDTH), full(1, GLA_WIDTH),
        ],
        out_specs=tok(GLA_WIDTH),
        out_shape=jax.ShapeDtypeStruct((B, S, GLA_WIDTH), BF16),
        scratch_shapes=[pltpu.VMEM((B, GLA_KEY_WIDTH, GLA_VAL_DIM), F32)],
        compiler_params=pltpu.CompilerParams(
            dimension_semantics=("arbitrary",), vmem_limit_bytes=VMEM_LIMIT),
        name="gla_attn",
    )(gq, gk, gv, gr, small, w_a2_pad, b_a, g_gla)


def _layer_norm(z, g, b):
    mu = jnp.mean(z, axis=-1, keepdims=True)
    d = z - mu
    var = jnp.mean(d * d, axis=-1, keepdims=True)
    return d * lax.rsqrt(var + LN_EPS) * g + b


def _tail_kernel(x_ref, fox_ref, gla_ref, mod_ref, wof_ref, wog_ref, ln1g_ref, ln1b_ref,
                 wg_ref, wu_ref, wd_ref, ln2g_ref, ln2b_ref, o_ref):
    m = mod_ref[0]
    tm = x_ref.shape[0]
    halves = [slice(0, tm // 2), slice(tm // 2, tm)]
    ys = [_dot(fox_ref[r, :], wof_ref[...]) + _dot(gla_ref[r, :], wog_ref[...]) for r in halves]
    x1s, gs, ups = [], [], []
    for r, y in zip(halves, ys):
        x1 = _layer_norm(DEEPNORM_ALPHA * x_ref[r, :] + (1.0 + m[2:3]) * y,
                         ln1g_ref[...], ln1b_ref[...])
        u2 = (x1 * (1.0 + m[4:5]) + m[3:4]).astype(BF16)
        x1s.append(x1)
        gs.append(_dot(u2, wg_ref[...]))
        ups.append(_dot(u2, wu_ref[...]))
    y2s = []
    for g, up in zip(gs, ups):
        h = (g * (1.0 / (1.0 + jnp.exp(-g))) * up).astype(BF16)
        y2s.append(_dot(h, wd_ref[...]))
    for r, x1, y2 in zip(halves, x1s, y2s):
        o_ref[r, :] = _layer_norm(DEEPNORM_ALPHA * x1 + (1.0 + m[5:6]) * y2,
                                  ln2g_ref[...], ln2b_ref[...])


def _tail(x2d, fox2d, gla2d, mod, wo, ln1g, ln1b, wg, wu, wd, ln2g, ln2b, steps_per_batch):
    T, D = x2d.shape
    tm = TAIL_TM
    const = lambda r, c: pl.BlockSpec((r, c), lambda i: (0, 0), pipeline_mode=pl.Buffered(1))
    tok = lambda w: pl.BlockSpec((tm, w), lambda i: (i, 0))
    return pl.pallas_call(
        _tail_kernel,
        grid=(T // tm,),
        in_specs=[
            tok(D), tok(FOX_WIDTH), tok(GLA_WIDTH),
            pl.BlockSpec((1, N_MOD, D), lambda i: (i // steps_per_batch, 0, 0)),
            pl.BlockSpec((FOX_WIDTH, D), lambda i: (0, 0), pipeline_mode=pl.Buffered(1)),
            pl.BlockSpec((GLA_WIDTH, D), lambda i: (1, 0), pipeline_mode=pl.Buffered(1)),
            const(1, D), const(1, D),
            const(D, D_FF), const(D, D_FF), const(D_FF, D), const(1, D), const(1, D),
        ],
        out_specs=tok(D),
        out_shape=jax.ShapeDtypeStruct((T, D), F32),
        compiler_params=pltpu.CompilerParams(
            dimension_semantics=("arbitrary",), vmem_limit_bytes=VMEM_LIMIT),
        name="tail",
    )(x2d, fox2d, gla2d, mod, wo, wo, ln1g, ln1b, wg, wu, wd, ln2g, ln2b)


def kernel(x, c, w_c, b_c, w_in, b_f, w_a2, b_a, g_gla, w_o, ln1_g, ln1_b,
           w_gate, w_up, w_down, ln2_g, ln2_b):
    B, S, D = x.shape
    assert D == D_MODEL and S % INPROJ_TM == 0 and INPROJ_TM == FOX_TK and FOX_TK % FOX_TQ == 0

    b_f_pad = jnp.zeros((1, SMALL_W), F32).at[0, :FOX_HEADS].set(b_f)
    w_a2_pad = jnp.zeros((SMALL_W, GLA_KEY_WIDTH), F32) \
        .at[FOX_HEADS:FOX_HEADS + GLA_GATE_RANK].set(w_a2).astype(BF16)

    mod, w_all = _modulation(c, w_c, b_c, w_in)

    fqT, kaug, fvT, gq, gk, gv, gr, small, drel, stats, wg16, wu16, wd16 = _in_projection(
        x, mod, w_all, b_f_pad, w_gate, w_up, w_down)
    fox = _fox_attention(fqT, kaug, fvT, drel, stats)
    gla = _gla_attention(gq, gk, gv, gr, small, w_a2_pad,
                         b_a.reshape(1, -1), g_gla.reshape(1, -1))

    T = B * S
    w_o_b = w_o.astype(BF16)
    out = _tail(x.reshape(T, D), fox.reshape(T, FOX_WIDTH), gla.reshape(T, GLA_WIDTH), mod,
                w_o_b,
                ln1_g.reshape(1, D), ln1_b.reshape(1, D),
                wg16, wu16, wd16,
                ln2_g.reshape(1, D), ln2_b.reshape(1, D), S // TAIL_TM)
    return out.reshape(B, S, D)
```

```python
import functools
import math

import numpy as np
import jax
import jax.numpy as jnp
from jax import lax
from jax.experimental import pallas as pl
from jax.experimental.pallas import tpu as pltpu

F32 = jnp.float32
BF16 = jnp.bfloat16

D_MODEL = 1024
FOX_WIDTH = 512
FOX_HEAD_DIM = 64
FOX_HEADS = 8
FOX_PAIRS = FOX_HEADS // 2
GLA_WIDTH = 512
GLA_HEADS = 4
GLA_VAL_DIM = 128
GLA_KEY_DIM = 64
GLA_KEY_WIDTH = 256
GLA_GATE_RANK = 16
GLA_GATE_TEMP = 16.0
GLA_CHUNK = 64
D_FF = 2816
LN_EPS = 1e-5
RMS_EPS = 1e-6
N_MOD = 6
DEEPNORM_ALPHA = 2.0 ** 0.25
IN_SPLITS = (FOX_WIDTH, FOX_WIDTH, FOX_WIDTH, FOX_HEADS, GLA_KEY_WIDTH, GLA_KEY_WIDTH,
             GLA_WIDTH, GLA_GATE_RANK, GLA_WIDTH)
LOG2E = math.log2(math.e)

LANES = 128
SUBLANES = 8
MXU_DIM = 256
SMALL_W = LANES
VMEM_LIMIT = 56 * 1024 * 1024

MOD_TN = 768
INPROJ_TM = 512
FOX_TQ = 256
FOX_GROUP_HEADS = 4
FOX_SUM_ROWS = 16
FOX_TK = 512
GLA_TS = 512
GLA_PAIR = 2 * GLA_CHUNK
TAIL_TM = 512

NEG_BIG = -1e30
PRUNE_LOG2 = -1080.0
NORM_SLACK = 1.02

AUG_ROWS = 32


def _log_sigmoid(z):
    return jnp.minimum(z, 0.0) - jnp.log(1.0 + jnp.exp(-jnp.abs(z)))


def _split3(a):
    hi = a.astype(BF16)
    r1 = a - hi.astype(F32)
    mid = r1.astype(BF16)
    lo = (r1 - mid.astype(F32)).astype(BF16)
    return hi, mid, lo


def _dot(a, b):
    return jnp.dot(a, b, preferred_element_type=F32)


def _tri_left(tri, a):
    hi, mid, lo = _split3(a)
    return _dot(tri, hi) + _dot(tri, mid) + _dot(tri, lo)


def _tri_right(a, tri):
    hi, mid, lo = _split3(a)
    return _dot(hi, tri) + _dot(mid, tri) + _dot(lo, tri)


def _mod_kernel(cT_ref, w_ref, b_ref, win_ref, o_ref, wall_ref, *, n_rows):
    w = w_ref[...]
    outs = [jnp.sum(w * cT_ref[:, b:b + 1], axis=0, keepdims=True) for b in range(n_rows)]
    outs.append(jnp.zeros((o_ref.shape[0] - n_rows, w.shape[1]), F32))
    o_ref[...] = jnp.concatenate(outs, axis=0) + b_ref[...]

    for dst, src, width in _W_GROUPS:
        wall_ref[:, dst:dst + width] = win_ref[:, src:src + width].astype(BF16)
    small_w = jnp.concatenate(
        [win_ref[:, _SRC_FF:_SRC_FF + FOX_HEADS], win_ref[:, _SRC_GA:_SRC_GA + GLA_GATE_RANK],
         jnp.zeros((win_ref.shape[0], 2 * SMALL_W - FOX_HEADS - GLA_GATE_RANK), F32)], axis=1)
    wall_ref[:, _C_SM:_C_SM + 2 * SMALL_W] = small_w.astype(BF16)


def _modulation(c, w_c, b_c, w_in):
    B, D = c.shape
    N = w_c.shape[1]
    rows = SUBLANES
    n_steps = N // MOD_TN
    w_rows = w_in.shape[0] // n_steps
    cT_pad = jnp.zeros((D, LANES), F32).at[:, :B].set(c.T)
    out, w_all = pl.pallas_call(
        functools.partial(_mod_kernel, n_rows=B),
        grid=(n_steps,),
        in_specs=[
            pl.BlockSpec((D, LANES), lambda j: (0, 0)),
            pl.BlockSpec((D, MOD_TN), lambda j: (0, j)),
            pl.BlockSpec((1, MOD_TN), lambda j: (0, j)),
            pl.BlockSpec((w_rows, w_in.shape[1]), lambda j: (j, 0)),
        ],
        out_specs=[pl.BlockSpec((rows, MOD_TN), lambda j: (0, j)),
                   pl.BlockSpec((w_rows, _W_COLS), lambda j: (j, 0))],
        out_shape=[jax.ShapeDtypeStruct((rows, N), F32),
                   jax.ShapeDtypeStruct((w_in.shape[0], _W_COLS), BF16)],
        compiler_params=pltpu.CompilerParams(
            dimension_semantics=("arbitrary",), vmem_limit_bytes=VMEM_LIMIT),
        name="adaln_mod",
    )(cT_pad, w_c, b_c.reshape(1, N), w_in)
    return out[:B].reshape(B, N_MOD, D), w_all


_C_FQ, _C_FK, _C_FV = 0, 512, 1024
_C_GQ, _C_SM, _C_GK, _C_GV, _C_GR = 1536, 1792, 2048, 2304, 2816
_W_COLS = _C_GR + GLA_WIDTH
_SRC = [sum(IN_SPLITS[:i]) for i in range(len(IN_SPLITS))]
_SRC_FF, _SRC_GA = _SRC[3], _SRC[7]
_W_GROUPS = ((_C_FQ, _SRC[0], FOX_WIDTH), (_C_FK, _SRC[1], FOX_WIDTH), (_C_FV, _SRC[2], FOX_WIDTH),
             (_C_GQ, _SRC[4], GLA_KEY_WIDTH), (_C_GK, _SRC[5], GLA_KEY_WIDTH),
             (_C_GV, _SRC[6], GLA_WIDTH), (_C_GR, _SRC[8], GLA_WIDTH))
_KAUG_W = FOX_PAIRS * MXU_DIM


def _head_indicator():
    ind = np.zeros((FOX_WIDTH, MXU_DIM), np.float32)
    for h in range(FOX_HEADS):
        ind[h * FOX_HEAD_DIM:(h + 1) * FOX_HEAD_DIM, h] = 1.0
    return ind


def _inproj_kernel(x_ref, mod_ref, w_ref, bf_ref, ind_ref, gg_ref, wg32_ref, wu32_ref, wd32_ref,
                   fqT_ref, kaug_ref, fvT_ref, gq_ref, gk_ref, gv_ref, gr_ref, sm_ref,
                   drel_ref, stats_ref, wg16_ref, wu16_ref, wd16_ref, carry_ref):
    tm = x_ref.shape[1]

    wg16_ref[...] = wg32_ref[...].astype(BF16)
    wu16_ref[...] = wu32_ref[...].astype(BF16)
    wd16_ref[...] = wd32_ref[...].astype(BF16)

    @pl.when(pl.program_id(1) == 0)
    def _():
        carry_ref[...] = jnp.zeros_like(carry_ref)

    m = mod_ref[0]
    u = (x_ref[0] * (1.0 + m[1:2]) + m[0:1]).astype(BF16)

    def proj(lo, width):
        return _dot(u, w_ref[:, lo:lo + width])

    gq_small = proj(_C_GQ, 2 * GLA_KEY_WIDTH)
    gq_ref[0] = gq_small[:, :GLA_KEY_WIDTH] * (GLA_KEY_DIM ** -0.5)
    small = gq_small[:, GLA_KEY_WIDTH:GLA_KEY_WIDTH + SMALL_W]
    sm_ref[0] = small

    fq = proj(_C_FQ, FOX_WIDTH) * (FOX_HEAD_DIM ** -0.5 * LOG2E)
    fqT_ref[0] = fq.T.astype(BF16)
    fk32 = proj(_C_FK, FOX_WIDTH)
    fk = fk32.astype(BF16)
    qn2 = jnp.max(_dot((fq * fq).astype(BF16), ind_ref[...])[:, :SMALL_W], axis=0, keepdims=True)
    kn2 = jnp.max(_dot((fk32 * fk32).astype(BF16), ind_ref[...])[:, :SMALL_W], axis=0, keepdims=True)

    zT = (small + bf_ref[...]).T[:FOX_HEADS, :]
    lfT = _log_sigmoid(zT) * LOG2E
    r = lax.broadcasted_iota(jnp.int32, (tm, tm), 0)
    c = lax.broadcasted_iota(jnp.int32, (tm, tm), 1)
    triu = jnp.where(r <= c, 1.0, 0.0).astype(BF16)
    drelT = _tri_right(lfT, triu)
    drel_ref[0] = drelT

    c_before = carry_ref[...]
    srow = lax.broadcasted_iota(jnp.int32, (SUBLANES, SMALL_W), 0)
    slane = lax.broadcasted_iota(jnp.int32, (SUBLANES, SMALL_W), 1)
    total = jnp.sum(jnp.where(srow == slane, drelT[:, tm - 1:tm], 0.0), axis=0, keepdims=True)
    carry_ref[...] = c_before + total

    hi, mid, lo = _split3(drelT)
    ones3 = jnp.where(lax.broadcasted_iota(jnp.int32, (SUBLANES, tm), 0) < 3, 1.0, 0.0)
    slabT = jnp.concatenate(
        [-hi.astype(F32), -mid.astype(F32), -lo.astype(F32), ones3,
         jnp.zeros((LANES - 4 * SUBLANES, tm), F32)], axis=0)
    slab = slabT.T.astype(BF16)

    fvT_ref[0] = proj(_C_FV, FOX_WIDTH).T.astype(BF16)
    gv_ref[0] = proj(_C_GV, GLA_WIDTH).astype(BF16)
    gr = proj(_C_GR, GLA_WIDTH)
    gr_ref[0] = gg_ref[...] * (gr * (1.0 / (1.0 + jnp.exp(-gr))))
    gk_ref[0] = proj(_C_GK, GLA_KEY_WIDTH).T

    stats_ref[0, 0] = jnp.where(srow == 0, c_before,
                                jnp.where(srow == 1, qn2, jnp.where(srow == 2, kn2, 0.0)))
    for p in range(FOX_PAIRS):
        kaug_ref[0, :, p * MXU_DIM:p * MXU_DIM + LANES] = fk[:, p * LANES:(p + 1) * LANES]
        kaug_ref[0, :, p * MXU_DIM + LANES:(p + 1) * MXU_DIM] = slab


def _in_projection(x, mod, w_all, b_f_pad, g_gla, w_gate, w_up, w_down):
    B, S, D = x.shape
    tm = INPROJ_TM
    n_steps = B * (S // tm)
    steps_b = S // tm
    up_rows = D // n_steps
    down_rows = 2 * D_FF // n_steps
    assert D % n_steps == 0 and up_rows % 16 == 0 and (2 * D_FF) % n_steps == 0 and down_rows % 16 == 0
    up_spec = pl.BlockSpec((up_rows, D_FF), lambda b, s: (b * steps_b + s, 0))
    down_spec = pl.BlockSpec((down_rows, D), lambda b, s: ((b * steps_b + s) // 2, 0))
    ind = _head_indicator()
    tok = lambda w: pl.BlockSpec((1, tm, w), lambda b, s: (b, s, 0))
    tokT = lambda h: pl.BlockSpec((1, h, tm), lambda b, s: (b, 0, s))
    outs = pl.pallas_call(
        _inproj_kernel,
        grid=(B, S // tm),
        in_specs=[
            tok(D),
            pl.BlockSpec((1, N_MOD, D), lambda b, s: (b, 0, 0)),
            pl.BlockSpec((D, _W_COLS), lambda b, s: (0, 0), pipeline_mode=pl.Buffered(1)),
            pl.BlockSpec((1, SMALL_W), lambda b, s: (0, 0)),
            pl.BlockSpec(ind.shape, lambda b, s: (0, 0)),
            pl.BlockSpec((1, GLA_WIDTH), lambda b, s: (0, 0)),
            up_spec, up_spec, down_spec,
        ],
        out_specs=[
            tokT(FOX_WIDTH), tok(_KAUG_W), tokT(FOX_WIDTH),
            tok(GLA_KEY_WIDTH), tokT(GLA_KEY_WIDTH), tok(GLA_WIDTH), tok(GLA_WIDTH),
            tok(SMALL_W),
            tokT(FOX_HEADS),
            pl.BlockSpec((1, 1, SUBLANES, SMALL_W), lambda b, s: (b, s, 0, 0)),
            up_spec, up_spec, down_spec,
        ],
        out_shape=[
            jax.ShapeDtypeStruct((B, FOX_WIDTH, S), BF16),
            jax.ShapeDtypeStruct((B, S, _KAUG_W), BF16),
            jax.ShapeDtypeStruct((B, FOX_WIDTH, S), BF16),
            jax.ShapeDtypeStruct((B, S, GLA_KEY_WIDTH), F32),
            jax.ShapeDtypeStruct((B, GLA_KEY_WIDTH, S), F32),
            jax.ShapeDtypeStruct((B, S, GLA_WIDTH), BF16),
            jax.ShapeDtypeStruct((B, S, GLA_WIDTH), F32),
            jax.ShapeDtypeStruct((B, S, SMALL_W), F32),
            jax.ShapeDtypeStruct((B, FOX_HEADS, S), F32),
            jax.ShapeDtypeStruct((B, S // tm, SUBLANES, SMALL_W), F32),
            jax.ShapeDtypeStruct(w_gate.shape, BF16),
            jax.ShapeDtypeStruct(w_up.shape, BF16),
            jax.ShapeDtypeStruct(w_down.shape, BF16),
        ],
        scratch_shapes=[pltpu.VMEM((1, SMALL_W), F32)],
        compiler_params=pltpu.CompilerParams(
            dimension_semantics=("arbitrary", "arbitrary"), vmem_limit_bytes=VMEM_LIMIT),
        name="in_proj",
    )(x, mod, w_all, b_f_pad, jnp.asarray(ind, BF16), g_gla, w_gate, w_up, w_down)
    return outs


def _fox_kernel(jstart_ref, nitems_ref, qT_ref, kaug_ref, vT_ref, drel_ref, crep_ref, o_ref,
                qaug_ref, s_ref, p_ref, mx_ref, acc_ref, m_ref, mnext_ref, shift_ref, alpha_ref):
    tq, tk = FOX_TQ, FOX_TK
    half = FOX_HEAD_DIM
    S = kaug_ref.shape[1]
    nq = S // tq
    nh = FOX_GROUP_HEADS
    n_blk = crep_ref.shape[2] // nh
    g = pl.program_id(0) * pl.num_programs(1) + pl.program_id(1)

    def build_qaug(qi):
        q0 = pl.multiple_of(qi * tq, tq)
        dr = drel_ref[0, 0, :, pl.ds(q0, tq)]
        row = lax.broadcasted_iota(jnp.int32, (2 * half, tq), 0)
        r32 = lax.broadcasted_iota(jnp.int32, (AUG_ROWS, tq), 0)
        for h in range(nh):
            buf = (qi % 2) * nh + h
            qT = qT_ref[0, (h // 2) * 2 * half:(h // 2 + 1) * 2 * half, pl.ds(q0, tq)]
            own = (row < half) if h % 2 == 0 else (row >= half)
            qaug_ref[buf, 0:2 * half, :] = jnp.where(own, qT, jnp.zeros_like(qT))
            hi, mid, lo = _split3(dr[h:h + 1])
            gh = nh * pl.program_id(1) + h
            pick = (r32 == gh) | (r32 == gh + SUBLANES) | (r32 == gh + 2 * SUBLANES)
            slab = jnp.where(r32 == 24, hi.astype(F32),
                             jnp.where(r32 == 25, mid.astype(F32),
                                       jnp.where(r32 == 26, lo.astype(F32),
                                                 jnp.where(pick, 1.0, 0.0))))
            qaug_ref[buf, 2 * half:2 * half + AUG_ROWS, :] = slab.astype(BF16)

    def reset_state():
        acc_ref[...] = jnp.zeros_like(acc_ref)

    sum_rows = jnp.where(lax.broadcasted_iota(jnp.int32, (FOX_SUM_ROWS, tk), 0) == 0, 1.0, 0.0).astype(BF16)

    per = tk // tq

    def col_max8(s):
        return jnp.max(s.reshape(s.shape[0] // SUBLANES, SUBLANES, tq), axis=0)

    def qk_dots(qi, j):
        k_off = pl.multiple_of(j * tk, tk)
        kts = [kaug_ref[0, pl.ds(k_off, tk), p * MXU_DIM:(p + 1) * MXU_DIM] for p in range(nh // 2)]
        return [_dot(kts[h // 2], qaug_ref[(qi % 2) * nh + h]) for h in range(nh)]

    def store_scores(scores):
        for h in range(nh):
            s_ref[h] = scores[h]
            for g in range(per):
                mx_ref[per * h + g] = col_max8(scores[h][g * tq:(g + 1) * tq])

    def mask_stage(qi):
        d = qi % per
        d0 = pl.multiple_of(d * tq, tq)
        kr = lax.broadcasted_iota(jnp.int32, (tq, tq), 0)
        qc = lax.broadcasted_iota(jnp.int32, (tq, tq), 1)
        for h in range(nh):
            tri = jnp.where(kr <= qc, s_ref[h, pl.ds(d0, tq), :], NEG_BIG)
            s_ref[h, pl.ds(d0, tq), :] = tri
            mx_ref[per * h + d] = col_max8(tri)
        for g in range(1, per):
            @pl.when(d < g)
            def _():
                for h in range(nh):
                    s_ref[h, g * tq:(g + 1) * tq, :] = jnp.full((tq, tq), NEG_BIG, F32)
                    mx_ref[per * h + g] = jnp.full((SUBLANES, tq), NEG_BIG, F32)

    def prep_stage(qi, j, first):
        i_blk = qi * tq // INPROJ_TM
        for h in range(nh):
            big_c = (crep_ref[0, 0, pl.ds(h * n_blk + i_blk, 1), :]
                     - crep_ref[0, 0, pl.ds(h * n_blk + j, 1), :])
            mx = mx_ref[per * h]
            for g in range(1, per):
                mx = jnp.maximum(mx, mx_ref[per * h + g])
            m_prev = jnp.where(first, NEG_BIG, m_ref[h])
            m_new = jnp.maximum(m_prev, jnp.max(mx, axis=0, keepdims=True) + big_c)
            mnext_ref[h] = m_new
            shift_ref[h] = m_new - big_c
            alpha_ref[h] = jnp.exp2(m_prev - m_new)

    def exp_stage():
        for h in range(nh):
            p_ref[h] = jnp.exp2(s_ref[h] - shift_ref[h]).astype(BF16)

    def pv_stage(j):
        k_off = pl.multiple_of(j * tk, tk)
        for h in range(nh):
            vT = jnp.concatenate([vT_ref[0, h * half:(h + 1) * half, pl.ds(k_off, tk)], sum_rows], axis=0)
            acc_ref[h] = alpha_ref[h] * acc_ref[h] + _dot(vT, p_ref[h])
            m_ref[h] = mnext_ref[h]

    def finalize(qi):
        outs = []
        for h in range(nh):
            acc = acc_ref[h]
            outs.append(acc[:half] / acc[half:half + 1])
        oT = jnp.concatenate(outs, axis=0)
        o_ref[0, pl.ds(pl.multiple_of(qi * tq, tq), tq), :] = oT.T.astype(o_ref.dtype)

    zero = jnp.int32(0)
    qaug_ref[...] = jnp.zeros_like(qaug_ref)
    build_qaug(zero)
    build_qaug(zero + 1)
    reset_state()
    store_scores(qk_dots(zero, zero))
    mask_stage(zero)
    prep_stage(zero, zero, True)

    def body(t, carry):
        qi, j = carry
        last = j == qi * tq // tk
        qi2 = jnp.where(last, qi + 1, qi)
        j2 = jnp.where(last, jstart_ref[g * nq + jnp.minimum(qi + 1, nq - 1)], j + 1)
        exp_stage()
        nxt = qk_dots(qi2, j2)
        pv_stage(j)
        store_scores(nxt)
        prep_stage(qi2, j2, last)

        @pl.when(j2 == qi2 * tq // tk)
        def _():
            mask_stage(qi2)
            prep_stage(qi2, j2, last)

        @pl.when(last)
        def _():
            finalize(qi)
            reset_state()

            @pl.when(qi + 2 < nq)
            def _():
                build_qaug(qi + 2)

        return qi2, j2

    qi, j = lax.fori_loop(0, nitems_ref[g] - 1, body, (jnp.int32(0), jnp.int32(0)))
    exp_stage()
    pv_stage(j)
    finalize(qi)


def _prune_plan(stats, drel, nq):
    B, n_blk = stats.shape[:2]
    per = FOX_TK // FOX_TQ
    c = stats[:, :, 0, :FOX_HEADS]
    qn = jnp.sqrt(stats[:, :, 1, :FOX_HEADS])
    kn = jnp.sqrt(stats[:, :, 2, :FOX_HEADS])
    c_next = jnp.concatenate([c[:, 1:], c[:, -1:]], axis=1)
    d_in = jnp.transpose(drel[:, :, FOX_TQ - 1::FOX_TQ], (0, 2, 1))
    d_in = jnp.concatenate([jnp.zeros_like(d_in[:, :1]), d_in[:, :-1]], axis=1)
    first_in_tile = (jnp.arange(nq) % per == 0)[None, :, None]
    d_before = jnp.repeat(c, per, axis=1) + jnp.where(first_in_tile, 0.0, d_in)
    qn_q = jnp.repeat(qn, per, axis=1)
    kn_q = jnp.repeat(kn, per, axis=1)
    bound = (d_before[:, :, None, :] - c_next[:, None, :, :]
             + NORM_SLACK * qn_q[:, :, None, :] * (kn[:, None, :, :] + kn_q[:, :, None, :]))
    i_idx = (jnp.arange(nq) // per)[:, None]
    j_idx = jnp.arange(n_blk)[None, :]
    skip = (bound <= PRUNE_LOG2) & (j_idx + 1 <= i_idx)[None, :, :, None]
    nh = FOX_GROUP_HEADS
    skip_pair = skip[..., 0::nh]
    for h in range(1, nh):
        skip_pair = skip_pair & skip[..., h::nh]
    jstart = jnp.min(jnp.where(skip_pair, n_blk, j_idx[None, :, :, None]), axis=2)
    jstart = jnp.transpose(jstart, (0, 2, 1))
    n_items = jnp.sum(i_idx[:, 0][None, None, :] - jstart + 1, axis=2)
    return jstart.reshape(-1).astype(jnp.int32), n_items.reshape(-1).astype(jnp.int32)


def _fox_attention(fqT, kaug, fvT, drel, stats):
    B, _, S = fqT.shape
    tq, tk = FOX_TQ, FOX_TK
    n_blk = S // INPROJ_TM
    nh = FOX_GROUP_HEADS
    n_groups = FOX_HEADS // nh
    drel4 = drel.reshape(B, n_groups, nh, S)
    c = stats[:, :, 0, :FOX_HEADS]
    c = jnp.transpose(c, (0, 2, 1)).reshape(B, n_groups, nh * n_blk, 1)
    crep = jnp.broadcast_to(c, (B, n_groups, nh * n_blk, tq))
    jstart, n_items = _prune_plan(stats, drel, S // tq)
    grid_spec = pltpu.PrefetchScalarGridSpec(
        num_scalar_prefetch=2,
        grid=(B, n_groups),
        in_specs=[
            pl.BlockSpec((1, nh * FOX_HEAD_DIM, S), lambda b, p, js, ni: (b, p, 0)),
            pl.BlockSpec((1, S, nh // 2 * MXU_DIM), lambda b, p, js, ni: (b, 0, p)),
            pl.BlockSpec((1, nh * FOX_HEAD_DIM, S), lambda b, p, js, ni: (b, p, 0)),
            pl.BlockSpec((1, 1, nh, S), lambda b, p, js, ni: (b, p, 0, 0)),
            pl.BlockSpec((1, 1, nh * n_blk, tq), lambda b, p, js, ni: (b, p, 0, 0)),
        ],
        out_specs=pl.BlockSpec((1, S, nh * FOX_HEAD_DIM), lambda b, p, js, ni: (b, 0, p)),
        scratch_shapes=[
            pltpu.VMEM((2 * nh, MXU_DIM, tq), BF16),
            pltpu.VMEM((nh, tk, tq), F32),
            pltpu.VMEM((nh, tk, tq), BF16),
            pltpu.VMEM((nh * (tk // tq), SUBLANES, tq), F32),
            pltpu.VMEM((nh, FOX_HEAD_DIM + FOX_SUM_ROWS, tq), F32),
            pltpu.VMEM((nh, 1, tq), F32),
            pltpu.VMEM((nh, 1, tq), F32),
            pltpu.VMEM((nh, 1, tq), F32),
            pltpu.VMEM((nh, 1, tq), F32),
        ])
    return pl.pallas_call(
        _fox_kernel,
        grid_spec=grid_spec,
        out_shape=jax.ShapeDtypeStruct((B, S, FOX_WIDTH), BF16),
        compiler_params=pltpu.CompilerParams(
            dimension_semantics=("arbitrary", "arbitrary"),
            vmem_limit_bytes=VMEM_LIMIT),
        name="fox_attn",
    )(jstart, n_items, fqT, kaug, fvT, drel4, crep)


def _gla_kernel(q_ref, kT_ref, v_ref, gate_ref, sm_ref, wa_ref, ba_ref, o_ref, state_ref):
    nb, ts = q_ref.shape[0], q_ref.shape[1]
    C, P = GLA_CHUNK, GLA_PAIR
    dk, dv, H = GLA_KEY_DIM, GLA_VAL_DIM, GLA_HEADS
    KW = GLA_KEY_WIDTH

    @pl.when(pl.program_id(0) == 0)
    def _():
        state_ref[...] = jnp.zeros_like(state_ref)

    la_all = [_log_sigmoid(_dot(sm_ref[bb].astype(BF16), wa_ref[...]) + ba_ref[...])
              * (1.0 / GLA_GATE_TEMP) for bb in range(nb)]

    ri = lax.broadcasted_iota(jnp.int32, (P, P), 0)
    ci = lax.broadcasted_iota(jnp.int32, (P, P), 1)
    tril2 = jnp.where(((ri < C) == (ci < C)) & (ci <= ri), 1.0, 0.0).astype(BF16)

    rs = lax.broadcasted_iota(jnp.int32, (2 * H * C, P), 0)
    cs = lax.broadcasted_iota(jnp.int32, (2 * H * C, P), 1)
    causal = ((rs >= H * C) == (cs >= C)) & ((cs & (C - 1)) <= (rs & (C - 1)))

    lane_q = lax.broadcasted_iota(jnp.int32, (C, KW), 1)
    lane_t = lax.broadcasted_iota(jnp.int32, (KW, P), 1)
    first = lane_t < C

    per_b = ts // P
    n_slabs = nb * per_b
    slab_b = [k // per_b for k in range(n_slabs)]
    slab_rows = [slice((k % per_b) * P, (k % per_b + 1) * P) for k in range(n_slabs)]
    bs = [_tri_left(tril2, la_all[slab_b[k]][slab_rows[k]]) for k in range(n_slabs)]

    qms, scores, kv0s, kv1s, dec0s, dec1s, vss = [], [], [], [], [], [], []
    for pi in range(n_slabs):
        bb, rows = slab_b[pi], slab_rows[pi]
        b = bs[pi]
        bT = b.T
        bl0 = bT[:, C - 1:C]
        bl1 = bT[:, P - 1:P]
        q_dec = q_ref[bb, rows, :] * jnp.exp(b)
        kT = kT_ref[bb, :, rows]
        k_invT = (kT * jnp.exp(-bT)).astype(BF16)
        k_teT = kT * jnp.exp(jnp.where(first, bl0, bl1) - bT)
        zt = jnp.zeros_like(k_teT)
        k_te0 = jnp.where(first, k_teT, zt).astype(BF16)
        k_te1 = jnp.where(first, zt, k_teT).astype(BF16)
        dec0 = jnp.exp(bl0)
        dec1 = jnp.exp(bl1)

        vs = [v_ref[bb, rows, h * dv:(h + 1) * dv] for h in range(H)]
        kv = [_dot(jnp.concatenate([k_te0[h * dk:(h + 1) * dk, :], k_te1[h * dk:(h + 1) * dk, :]], axis=0),
                   vs[h]) for h in range(H)]
        kv0s.append(jnp.concatenate([kv[h][:dk] for h in range(H)], axis=0))
        kv1s.append(jnp.concatenate([kv[h][dk:] for h in range(H)], axis=0))
        dec0s.append(dec0)
        dec1s.append(dec1)
        vss.append(vs)

        zq = jnp.zeros((C, KW), F32)
        qm = [jnp.concatenate(
            [jnp.where((lane_q >= h * dk) & (lane_q < (h + 1) * dk), q_dec[c * C:(c + 1) * C], zq)
             for h in range(H)], axis=0).astype(BF16) for c in range(2)]
        a = _dot(jnp.concatenate(qm, axis=0), k_invT)
        scores.append(jnp.where(causal, a, 0.0).astype(BF16))
        qms.append(qm)

    o_inters = []
    for pi in range(n_slabs):
        if pi % per_b == 0:
            s = state_ref[slab_b[pi]]
        s1 = dec0s[pi] * s + kv0s[pi]
        o_inters.append([_dot(qms[pi][0], s.astype(BF16)), _dot(qms[pi][1], s1.astype(BF16))])
        s = dec1s[pi] * s1 + kv1s[pi]
        if pi % per_b == per_b - 1:
            state_ref[slab_b[pi]] = s

    for pi in range(n_slabs):
        bb, rows = slab_b[pi], slab_rows[pi]
        a, o_inter, vs = scores[pi], o_inters[pi], vss[pi]
        for h in range(H):
            hv = slice(h * dv, (h + 1) * dv)
            a_h = jnp.concatenate([a[(c * H + h) * C:(c * H + h + 1) * C] for c in range(2)], axis=0)
            o = _dot(a_h, vs[h]) + jnp.concatenate(
                [o_inter[c][h * C:(h + 1) * C] for c in range(2)], axis=0)
            o = o * lax.rsqrt(jnp.mean(o * o, axis=-1, keepdims=True) + RMS_EPS)
            o_ref[bb, rows, hv] = (o * gate_ref[bb, rows, hv]).astype(o_ref.dtype)


def _gla_attention(gq, gkT, gv, gate, small, w_a2_pad, b_a):
    B, S, _ = gq.shape
    ts = GLA_TS
    tok = lambda w: pl.BlockSpec((B, ts, w), lambda s: (0, s, 0))
    full = lambda r, c: pl.BlockSpec((r, c), lambda s: (0, 0))
    return pl.pallas_call(
        _gla_kernel,
        grid=(S // ts,),
        in_specs=[
            tok(GLA_KEY_WIDTH), pl.BlockSpec((B, GLA_KEY_WIDTH, ts), lambda s: (0, 0, s)),
            tok(GLA_WIDTH), tok(GLA_WIDTH), tok(SMALL_W),
            full(SMALL_W, GLA_KEY_WIDTH), full(1, GLA_KEY_WIDTH),
        ],
        out_specs=tok(GLA_WIDTH),
        out_shape=jax.ShapeDtypeStruct((B, S, GLA_WIDTH), BF16),
        scratch_shapes=[pltpu.VMEM((B, GLA_KEY_WIDTH, GLA_VAL_DIM), F32)],
        compiler_params=pltpu.CompilerParams(
            dimension_semantics=("arbitrary",), vmem_limit_bytes=VMEM_LIMIT),
        name="gla_attn",
    )(gq, gkT, gv, gate, small, w_a2_pad, b_a)


def _layer_norm(z, g, b):
    mu = jnp.mean(z, axis=-1, keepdims=True)
    d = z - mu
    var = jnp.mean(d * d, axis=-1, keepdims=True)
    return d * lax.rsqrt(var + LN_EPS) * g + b


def _tail_kernel(x_ref, fox_ref, gla_ref, mod_ref, wof_ref, wog_ref, ln1g_ref, ln1b_ref,
                 wg_ref, wu_ref, wd_ref, ln2g_ref, ln2b_ref, o_ref):
    m = mod_ref[0]
    tm = x_ref.shape[0]
    halves = [slice(0, tm // 2), slice(tm // 2, tm)]
    ys = [_dot(fox_ref[r, :], wof_ref[...]) + _dot(gla_ref[r, :], wog_ref[...]) for r in halves]
    x1s, gs, ups = [], [], []
    for r, y in zip(halves, ys):
        x1 = _layer_norm(DEEPNORM_ALPHA * x_ref[r, :] + (1.0 + m[2:3]) * y,
                         ln1g_ref[...], ln1b_ref[...])
        u2 = (x1 * (1.0 + m[4:5]) + m[3:4]).astype(BF16)
        x1s.append(x1)
        gs.append(_dot(u2, wg_ref[...]))
        ups.append(_dot(u2, wu_ref[...]))
    y2s = []
    for g, up in zip(gs, ups):
        h = (g * (1.0 / (1.0 + jnp.exp(-g))) * up).astype(BF16)
        y2s.append(_dot(h, wd_ref[...]))
    for r, x1, y2 in zip(halves, x1s, y2s):
        o_ref[r, :] = _layer_norm(DEEPNORM_ALPHA * x1 + (1.0 + m[5:6]) * y2,
                                  ln2g_ref[...], ln2b_ref[...])


def _tail(x2d, fox2d, gla2d, mod, wo, ln1g, ln1b, wg, wu, wd, ln2g, ln2b, steps_per_batch):
    T, D = x2d.shape
    tm = TAIL_TM
    const = lambda r, c: pl.BlockSpec((r, c), lambda i: (0, 0), pipeline_mode=pl.Buffered(1))
    tok = lambda w: pl.BlockSpec((tm, w), lambda i: (i, 0))
    return pl.pallas_call(
        _tail_kernel,
        grid=(T // tm,),
        in_specs=[
            tok(D), tok(FOX_WIDTH), tok(GLA_WIDTH),
            pl.BlockSpec((1, N_MOD, D), lambda i: (i // steps_per_batch, 0, 0)),
            pl.BlockSpec((FOX_WIDTH, D), lambda i: (0, 0), pipeline_mode=pl.Buffered(1)),
            pl.BlockSpec((GLA_WIDTH, D), lambda i: (1, 0), pipeline_mode=pl.Buffered(1)),
            const(1, D), const(1, D),
            const(D, D_FF), const(D, D_FF), const(D_FF, D), const(1, D), const(1, D),
        ],
        out_specs=tok(D),
        out_shape=jax.ShapeDtypeStruct((T, D), F32),
        compiler_params=pltpu.CompilerParams(
            dimension_semantics=("arbitrary",), vmem_limit_bytes=VMEM_LIMIT),
        name="tail",
    )(x2d, fox2d, gla2d, mod, wo, wo, ln1g, ln1b, wg, wu, wd, ln2g, ln2b)


def kernel(x, c, w_c, b_c, w_in, b_f, w_a2, b_a, g_gla, w_o, ln1_g, ln1_b,
           w_gate, w_up, w_down, ln2_g, ln2_b):
    B, S, D = x.shape
    assert D == D_MODEL and S % INPROJ_TM == 0 and INPROJ_TM == FOX_TK and FOX_TK % FOX_TQ == 0

    b_f_pad = jnp.zeros((1, SMALL_W), F32).at[0, :FOX_HEADS].set(b_f)
    w_a2_pad = jnp.zeros((SMALL_W, GLA_KEY_WIDTH), F32) \
        .at[FOX_HEADS:FOX_HEADS + GLA_GATE_RANK].set(w_a2).astype(BF16)

    mod, w_all = _modulation(c, w_c, b_c, w_in)

    fqT, kaug, fvT, gq, gkT, gv, gate, small, drel, stats, wg16, wu16, wd16 = _in_projection(
        x, mod, w_all, b_f_pad, g_gla.reshape(1, -1), w_gate, w_up, w_down)
    fox = _fox_attention(fqT, kaug, fvT, drel, stats)
    gla = _gla_attention(gq, gkT, gv, gate, small, w_a2_pad, b_a.reshape(1, -1))

    T = B * S
    w_o_b = w_o.astype(BF16)
    out = _tail(x.reshape(T, D), fox.reshape(T, FOX_WIDTH), gla.reshape(T, GLA_WIDTH), mod,
                w_o_b,
                ln1_g.reshape(1, D), ln1_b.reshape(1, D),
                wg16, wu16, wd16,
                ln2_g.reshape(1, D), ln2_b.reshape(1, D), S // TAIL_TM)
    return out.reshape(B, S, D)
```

```python
import functools
import math

import numpy as np
import jax
import jax.numpy as jnp
from jax import lax
from jax.experimental import pallas as pl
from jax.experimental.pallas import tpu as pltpu

F32 = jnp.float32
BF16 = jnp.bfloat16

D_MODEL = 1024
FOX_WIDTH = 512
FOX_HEAD_DIM = 64
FOX_HEADS = 8
FOX_PAIRS = FOX_HEADS // 2
GLA_WIDTH = 512
GLA_HEADS = 4
GLA_VAL_DIM = 128
GLA_KEY_DIM = 64
GLA_KEY_WIDTH = 256
GLA_GATE_RANK = 16
GLA_GATE_TEMP = 16.0
GLA_CHUNK = 64
D_FF = 2816
LN_EPS = 1e-5
RMS_EPS = 1e-6
N_MOD = 6
DEEPNORM_ALPHA = 2.0 ** 0.25
IN_SPLITS = (FOX_WIDTH, FOX_WIDTH, FOX_WIDTH, FOX_HEADS, GLA_KEY_WIDTH, GLA_KEY_WIDTH,
             GLA_WIDTH, GLA_GATE_RANK, GLA_WIDTH)
LOG2E = math.log2(math.e)

LANES = 128
SUBLANES = 8
MXU_DIM = 256
SMALL_W = LANES
VMEM_LIMIT = 56 * 1024 * 1024

MOD_TN = 768
INPROJ_TM = 512
FOX_TQ = 256
FOX_GROUP_HEADS = 4
FOX_SUM_ROWS = 16
FOX_TK = 512
GLA_TS = 512
GLA_PAIR = 2 * GLA_CHUNK
TAIL_TM = 512

NEG_BIG = -1e30
PRUNE_LOG2 = -1080.0
NORM_SLACK = 1.02

AUG_ROWS = 32


def _log_sigmoid(z):
    return jnp.minimum(z, 0.0) - jnp.log(1.0 + jnp.exp(-jnp.abs(z)))


def _split3(a):
    hi = a.astype(BF16)
    r1 = a - hi.astype(F32)
    mid = r1.astype(BF16)
    lo = (r1 - mid.astype(F32)).astype(BF16)
    return hi, mid, lo


def _dot(a, b):
    return jnp.dot(a, b, preferred_element_type=F32)


def _tri_left(tri, a):
    hi, mid, lo = _split3(a)
    return _dot(tri, hi) + _dot(tri, mid) + _dot(tri, lo)


def _tri_right(a, tri):
    hi, mid, lo = _split3(a)
    return _dot(hi, tri) + _dot(mid, tri) + _dot(lo, tri)


def _regroup_blocks():
    blocks = []
    for dst, src, width in _W_GROUPS:
        blocks += [(dst + c0, src + c0) for c0 in range(0, width, MXU_DIM)]
    blocks.append((_C_SM, None))
    return blocks


def _mod_kernel(cT_ref, w_ref, b_ref, winT_ref, o_ref, wall_ref, *, n_rows):
    w = w_ref[...]
    outs = [jnp.sum(w * cT_ref[:, b:b + 1], axis=0, keepdims=True) for b in range(n_rows)]
    outs.append(jnp.zeros((o_ref.shape[0] - n_rows, w.shape[1]), F32))
    o_ref[...] = jnp.concatenate(outs, axis=0) + b_ref[...]

    j = pl.program_id(0)
    n_steps = pl.num_programs(0)
    for k, (dst, src) in enumerate(_regroup_blocks()):
        @pl.when(j == k % n_steps)
        def _():
            if src is None:
                blk = jnp.concatenate(
                    [winT_ref[_SRC_FF:_SRC_FF + FOX_HEADS, :], winT_ref[_SRC_GA:_SRC_GA + GLA_GATE_RANK, :],
                     jnp.zeros((MXU_DIM - FOX_HEADS - GLA_GATE_RANK, winT_ref.shape[1]), F32)], axis=0)
            else:
                blk = winT_ref[src:src + MXU_DIM, :]
            wall_ref[:, dst:dst + MXU_DIM] = blk.T.astype(BF16)


def _modulation(c, w_c, b_c, w_in):
    B, D = c.shape
    N = w_c.shape[1]
    rows = SUBLANES
    n_steps = N // MOD_TN
    cT_pad = jnp.zeros((D, LANES), F32).at[:, :B].set(c.T)
    w_inT = w_in.T
    out, w_all = pl.pallas_call(
        functools.partial(_mod_kernel, n_rows=B),
        grid=(n_steps,),
        in_specs=[
            pl.BlockSpec((D, LANES), lambda j: (0, 0)),
            pl.BlockSpec((D, MOD_TN), lambda j: (0, j)),
            pl.BlockSpec((1, MOD_TN), lambda j: (0, j)),
            pl.BlockSpec(w_inT.shape, lambda j: (0, 0), pipeline_mode=pl.Buffered(1)),
        ],
        out_specs=[pl.BlockSpec((rows, MOD_TN), lambda j: (0, j)),
                   pl.BlockSpec((D, _W_COLS), lambda j: (0, 0))],
        out_shape=[jax.ShapeDtypeStruct((rows, N), F32),
                   jax.ShapeDtypeStruct((D, _W_COLS), BF16)],
        compiler_params=pltpu.CompilerParams(
            dimension_semantics=("arbitrary",), vmem_limit_bytes=VMEM_LIMIT),
        name="adaln_mod",
    )(cT_pad, w_c, b_c.reshape(1, N), w_inT)
    return out[:B].reshape(B, N_MOD, D), w_all


_C_FQ, _C_FK, _C_FV = 0, 512, 1024
_C_GQ, _C_SM, _C_GK, _C_GV, _C_GR = 1536, 1792, 2048, 2304, 2816
_W_COLS = _C_GR + GLA_WIDTH
_SRC = [sum(IN_SPLITS[:i]) for i in range(len(IN_SPLITS))]
_SRC_FF, _SRC_GA = _SRC[3], _SRC[7]
_W_GROUPS = ((_C_FQ, _SRC[0], FOX_WIDTH), (_C_FK, _SRC[1], FOX_WIDTH), (_C_FV, _SRC[2], FOX_WIDTH),
             (_C_GQ, _SRC[4], GLA_KEY_WIDTH), (_C_GK, _SRC[5], GLA_KEY_WIDTH),
             (_C_GV, _SRC[6], GLA_WIDTH), (_C_GR, _SRC[8], GLA_WIDTH))
_KAUG_W = FOX_PAIRS * MXU_DIM


def _head_indicator():
    ind = np.zeros((FOX_WIDTH, MXU_DIM), np.float32)
    for h in range(FOX_HEADS):
        ind[h * FOX_HEAD_DIM:(h + 1) * FOX_HEAD_DIM, h] = 1.0
    return ind


def _inproj_kernel(x_ref, mod_ref, w_ref, bf_ref, ind_ref, gg_ref, wg32_ref, wu32_ref, wd32_ref,
                   fqT_ref, kaug_ref, fvT_ref, gq_ref, gk_ref, gv_ref, gr_ref, sm_ref,
                   drel_ref, stats_ref, wg16_ref, wu16_ref, wd16_ref, carry_ref):
    tm = x_ref.shape[1]

    wg16_ref[...] = wg32_ref[...].astype(BF16)
    wu16_ref[...] = wu32_ref[...].astype(BF16)
    wd16_ref[...] = wd32_ref[...].astype(BF16)

    @pl.when(pl.program_id(1) == 0)
    def _():
        carry_ref[...] = jnp.zeros_like(carry_ref)

    m = mod_ref[0]
    u = (x_ref[0] * (1.0 + m[1:2]) + m[0:1]).astype(BF16)

    def proj(lo, width):
        return _dot(u, w_ref[:, lo:lo + width])

    gq_small = proj(_C_GQ, 2 * GLA_KEY_WIDTH)
    gq_ref[0] = gq_small[:, :GLA_KEY_WIDTH] * (GLA_KEY_DIM ** -0.5)
    small = gq_small[:, GLA_KEY_WIDTH:GLA_KEY_WIDTH + SMALL_W]
    sm_ref[0] = small

    fq = proj(_C_FQ, FOX_WIDTH) * (FOX_HEAD_DIM ** -0.5 * LOG2E)
    fqT_ref[0] = fq.T.astype(BF16)
    fk32 = proj(_C_FK, FOX_WIDTH)
    fk = fk32.astype(BF16)
    qn2 = jnp.max(_dot((fq * fq).astype(BF16), ind_ref[...])[:, :SMALL_W], axis=0, keepdims=True)
    kn2 = jnp.max(_dot((fk32 * fk32).astype(BF16), ind_ref[...])[:, :SMALL_W], axis=0, keepdims=True)

    zT = (small + bf_ref[...]).T[:FOX_HEADS, :]
    lfT = _log_sigmoid(zT) * LOG2E
    r = lax.broadcasted_iota(jnp.int32, (tm, tm), 0)
    c = lax.broadcasted_iota(jnp.int32, (tm, tm), 1)
    triu = jnp.where(r <= c, 1.0, 0.0).astype(BF16)
    drelT = _tri_right(lfT, triu)
    drel_ref[0] = drelT

    c_before = carry_ref[...]
    srow = lax.broadcasted_iota(jnp.int32, (SUBLANES, SMALL_W), 0)
    slane = lax.broadcasted_iota(jnp.int32, (SUBLANES, SMALL_W), 1)
    total = jnp.sum(jnp.where(srow == slane, drelT[:, tm - 1:tm], 0.0), axis=0, keepdims=True)
    carry_ref[...] = c_before + total

    hi, mid, lo = _split3(drelT)
    ones3 = jnp.where(lax.broadcasted_iota(jnp.int32, (SUBLANES, tm), 0) < 3, 1.0, 0.0)
    slabT = jnp.concatenate(
        [-hi.astype(F32), -mid.astype(F32), -lo.astype(F32), ones3,
         jnp.zeros((LANES - 4 * SUBLANES, tm), F32)], axis=0)
    slab = slabT.T.astype(BF16)

    fvT_ref[0] = proj(_C_FV, FOX_WIDTH).T.astype(BF16)
    gv_ref[0] = proj(_C_GV, GLA_WIDTH).astype(BF16)
    gr = proj(_C_GR, GLA_WIDTH)
    gr_ref[0] = gg_ref[...] * (gr * (1.0 / (1.0 + jnp.exp(-gr))))
    gk_ref[0] = proj(_C_GK, GLA_KEY_WIDTH).T

    stats_ref[0, 0] = jnp.where(srow == 0, c_before,
                                jnp.where(srow == 1, qn2, jnp.where(srow == 2, kn2, 0.0)))
    for p in range(FOX_PAIRS):
        kaug_ref[0, :, p * MXU_DIM:p * MXU_DIM + LANES] = fk[:, p * LANES:(p + 1) * LANES]
        kaug_ref[0, :, p * MXU_DIM + LANES:(p + 1) * MXU_DIM] = slab


def _in_projection(x, mod, w_all, b_f_pad, g_gla, w_gate, w_up, w_down):
    B, S, D = x.shape
    tm = INPROJ_TM
    n_steps = B * (S // tm)
    steps_b = S // tm
    up_rows = D // n_steps
    down_rows = 2 * D_FF // n_steps
    assert D % n_steps == 0 and up_rows % 16 == 0 and (2 * D_FF) % n_steps == 0 and down_rows % 16 == 0
    up_spec = pl.BlockSpec((up_rows, D_FF), lambda b, s: (b * steps_b + s, 0))
    down_spec = pl.BlockSpec((down_rows, D), lambda b, s: ((b * steps_b + s) // 2, 0))
    ind = _head_indicator()
    tok = lambda w: pl.BlockSpec((1, tm, w), lambda b, s: (b, s, 0))
    tokT = lambda h: pl.BlockSpec((1, h, tm), lambda b, s: (b, 0, s))
    outs = pl.pallas_call(
        _inproj_kernel,
        grid=(B, S // tm),
        in_specs=[
            tok(D),
            pl.BlockSpec((1, N_MOD, D), lambda b, s: (b, 0, 0)),
            pl.BlockSpec((D, _W_COLS), lambda b, s: (0, 0), pipeline_mode=pl.Buffered(1)),
            pl.BlockSpec((1, SMALL_W), lambda b, s: (0, 0)),
            pl.BlockSpec(ind.shape, lambda b, s: (0, 0)),
            pl.BlockSpec((1, GLA_WIDTH), lambda b, s: (0, 0)),
            up_spec, up_spec, down_spec,
        ],
        out_specs=[
            tokT(FOX_WIDTH), tok(_KAUG_W), tokT(FOX_WIDTH),
            tok(GLA_KEY_WIDTH), tokT(GLA_KEY_WIDTH), tok(GLA_WIDTH), tok(GLA_WIDTH),
            tok(SMALL_W),
            tokT(FOX_HEADS),
            pl.BlockSpec((1, 1, SUBLANES, SMALL_W), lambda b, s: (b, s, 0, 0)),
            up_spec, up_spec, down_spec,
        ],
        out_shape=[
            jax.ShapeDtypeStruct((B, FOX_WIDTH, S), BF16),
            jax.ShapeDtypeStruct((B, S, _KAUG_W), BF16),
            jax.ShapeDtypeStruct((B, FOX_WIDTH, S), BF16),
            jax.ShapeDtypeStruct((B, S, GLA_KEY_WIDTH), F32),
            jax.ShapeDtypeStruct((B, GLA_KEY_WIDTH, S), F32),
            jax.ShapeDtypeStruct((B, S, GLA_WIDTH), BF16),
            jax.ShapeDtypeStruct((B, S, GLA_WIDTH), F32),
            jax.ShapeDtypeStruct((B, S, SMALL_W), F32),
            jax.ShapeDtypeStruct((B, FOX_HEADS, S), F32),
            jax.ShapeDtypeStruct((B, S // tm, SUBLANES, SMALL_W), F32),
            jax.ShapeDtypeStruct(w_gate.shape, BF16),
            jax.ShapeDtypeStruct(w_up.shape, BF16),
            jax.ShapeDtypeStruct(w_down.shape, BF16),
        ],
        scratch_shapes=[pltpu.VMEM((1, SMALL_W), F32)],
        compiler_params=pltpu.CompilerParams(
            dimension_semantics=("arbitrary", "arbitrary"), vmem_limit_bytes=VMEM_LIMIT),
        name="in_proj",
    )(x, mod, w_all, b_f_pad, jnp.asarray(ind, BF16), g_gla, w_gate, w_up, w_down)
    return outs


def _fox_kernel(jstart_ref, nitems_ref, qT_ref, kaug_ref, vT_ref, drel_ref, crep_ref, o_ref,
                qaug_ref, s_ref, p_ref, mx_ref, acc_ref, m_ref, mnext_ref, shift_ref, alpha_ref):
    tq, tk = FOX_TQ, FOX_TK
    half = FOX_HEAD_DIM
    S = kaug_ref.shape[1]
    nq = S // tq
    nh = FOX_GROUP_HEADS
    n_blk = crep_ref.shape[2] // nh
    g = pl.program_id(0) * pl.num_programs(1) + pl.program_id(1)

    def build_qaug(qi):
        q0 = pl.multiple_of(qi * tq, tq)
        dr = drel_ref[0, 0, :, pl.ds(q0, tq)]
        row = lax.broadcasted_iota(jnp.int32, (2 * half, tq), 0)
        r32 = lax.broadcasted_iota(jnp.int32, (AUG_ROWS, tq), 0)
        for h in range(nh):
            buf = (qi % 2) * nh + h
            qT = qT_ref[0, (h // 2) * 2 * half:(h // 2 + 1) * 2 * half, pl.ds(q0, tq)]
            own = (row < half) if h % 2 == 0 else (row >= half)
            qaug_ref[buf, 0:2 * half, :] = jnp.where(own, qT, jnp.zeros_like(qT))
            hi, mid, lo = _split3(dr[h:h + 1])
            gh = nh * pl.program_id(1) + h
            pick = (r32 == gh) | (r32 == gh + SUBLANES) | (r32 == gh + 2 * SUBLANES)
            slab = jnp.where(r32 == 24, hi.astype(F32),
                             jnp.where(r32 == 25, mid.astype(F32),
                                       jnp.where(r32 == 26, lo.astype(F32),
                                                 jnp.where(pick, 1.0, 0.0))))
            qaug_ref[buf, 2 * half:2 * half + AUG_ROWS, :] = slab.astype(BF16)

    def reset_state():
        acc_ref[...] = jnp.zeros_like(acc_ref)

    sum_rows = jnp.where(lax.broadcasted_iota(jnp.int32, (FOX_SUM_ROWS, tk), 0) == 0, 1.0, 0.0).astype(BF16)

    per = tk // tq

    def col_max8(s):
        return jnp.max(s.reshape(s.shape[0] // SUBLANES, SUBLANES, tq), axis=0)

    def qk_dots(qi, j):
        k_off = pl.multiple_of(j * tk, tk)
        kts = [kaug_ref[0, pl.ds(k_off, tk), p * MXU_DIM:(p + 1) * MXU_DIM] for p in range(nh // 2)]
        return [_dot(kts[h // 2], qaug_ref[(qi % 2) * nh + h]) for h in range(nh)]

    def store_scores(scores):
        for h in range(nh):
            s_ref[h] = scores[h]
            for g in range(per):
                mx_ref[per * h + g] = col_max8(scores[h][g * tq:(g + 1) * tq])

    def mask_stage(qi):
        d = qi % per
        d0 = pl.multiple_of(d * tq, tq)
        kr = lax.broadcasted_iota(jnp.int32, (tq, tq), 0)
        qc = lax.broadcasted_iota(jnp.int32, (tq, tq), 1)
        for h in range(nh):
            tri = jnp.where(kr <= qc, s_ref[h, pl.ds(d0, tq), :], NEG_BIG)
            s_ref[h, pl.ds(d0, tq), :] = tri
            mx_ref[per * h + d] = col_max8(tri)
        for g in range(1, per):
            @pl.when(d < g)
            def _():
                for h in range(nh):
                    s_ref[h, g * tq:(g + 1) * tq, :] = jnp.full((tq, tq), NEG_BIG, F32)
                    mx_ref[per * h + g] = jnp.full((SUBLANES, tq), NEG_BIG, F32)

    def prep_stage(qi, j, first):
        i_blk = qi * tq // INPROJ_TM
        for h in range(nh):
            big_c = (crep_ref[0, 0, pl.ds(h * n_blk + i_blk, 1), :]
                     - crep_ref[0, 0, pl.ds(h * n_blk + j, 1), :])
            mx = mx_ref[per * h]
            for g in range(1, per):
                mx = jnp.maximum(mx, mx_ref[per * h + g])
            m_prev = jnp.where(first, NEG_BIG, m_ref[h])
            m_new = jnp.maximum(m_prev, jnp.max(mx, axis=0, keepdims=True) + big_c)
            mnext_ref[h] = m_new
            shift_ref[h] = m_new - big_c
            alpha_ref[h] = jnp.exp2(m_prev - m_new)

    def exp_stage():
        for h in range(nh):
            p_ref[h] = jnp.exp2(s_ref[h] - shift_ref[h]).astype(BF16)

    def pv_stage(j):
        k_off = pl.multiple_of(j * tk, tk)
        for h in range(nh):
            vT = jnp.concatenate([vT_ref[0, h * half:(h + 1) * half, pl.ds(k_off, tk)], sum_rows], axis=0)
            acc_ref[h] = alpha_ref[h] * acc_ref[h] + _dot(vT, p_ref[h])
            m_ref[h] = mnext_ref[h]

    def finalize(qi):
        outs = []
        for h in range(nh):
            acc = acc_ref[h]
            outs.append(acc[:half] / acc[half:half + 1])
        oT = jnp.concatenate(outs, axis=0)
        o_ref[0, pl.ds(pl.multiple_of(qi * tq, tq), tq), :] = oT.T.astype(o_ref.dtype)

    zero = jnp.int32(0)
    qaug_ref[...] = jnp.zeros_like(qaug_ref)
    build_qaug(zero)
    build_qaug(zero + 1)
    reset_state()
    store_scores(qk_dots(zero, zero))
    mask_stage(zero)
    prep_stage(zero, zero, True)

    def body(t, carry):
        qi, j = carry
        last = j == qi * tq // tk
        qi2 = jnp.where(last, qi + 1, qi)
        j2 = jnp.where(last, jstart_ref[g * nq + jnp.minimum(qi + 1, nq - 1)], j + 1)
        exp_stage()
        nxt = qk_dots(qi2, j2)
        pv_stage(j)
        store_scores(nxt)
        prep_stage(qi2, j2, last)

        @pl.when(j2 == qi2 * tq // tk)
        def _():
            mask_stage(qi2)
            prep_stage(qi2, j2, last)

        @pl.when(last)
        def _():
            finalize(qi)
            reset_state()

            @pl.when(qi + 2 < nq)
            def _():
                build_qaug(qi + 2)

        return qi2, j2

    qi, j = lax.fori_loop(0, nitems_ref[g] - 1, body, (jnp.int32(0), jnp.int32(0)))
    exp_stage()
    pv_stage(j)
    finalize(qi)


def _prune_plan(stats, drel, nq):
    B, n_blk = stats.shape[:2]
    per = FOX_TK // FOX_TQ
    c = stats[:, :, 0, :FOX_HEADS]
    qn = jnp.sqrt(stats[:, :, 1, :FOX_HEADS])
    kn = jnp.sqrt(stats[:, :, 2, :FOX_HEADS])
    c_next = jnp.concatenate([c[:, 1:], c[:, -1:]], axis=1)
    d_in = jnp.transpose(drel[:, :, FOX_TQ - 1::FOX_TQ], (0, 2, 1))
    d_in = jnp.concatenate([jnp.zeros_like(d_in[:, :1]), d_in[:, :-1]], axis=1)
    first_in_tile = (jnp.arange(nq) % per == 0)[None, :, None]
    d_before = jnp.repeat(c, per, axis=1) + jnp.where(first_in_tile, 0.0, d_in)
    qn_q = jnp.repeat(qn, per, axis=1)
    kn_q = jnp.repeat(kn, per, axis=1)
    bound = (d_before[:, :, None, :] - c_next[:, None, :, :]
             + NORM_SLACK * qn_q[:, :, None, :] * (kn[:, None, :, :] + kn_q[:, :, None, :]))
    i_idx = (jnp.arange(nq) // per)[:, None]
    j_idx = jnp.arange(n_blk)[None, :]
    skip = (bound <= PRUNE_LOG2) & (j_idx + 1 <= i_idx)[None, :, :, None]
    nh = FOX_GROUP_HEADS
    skip_pair = skip[..., 0::nh]
    for h in range(1, nh):
        skip_pair = skip_pair & skip[..., h::nh]
    jstart = jnp.min(jnp.where(skip_pair, n_blk, j_idx[None, :, :, None]), axis=2)
    jstart = jnp.transpose(jstart, (0, 2, 1))
    n_items = jnp.sum(i_idx[:, 0][None, None, :] - jstart + 1, axis=2)
    return jstart.reshape(-1).astype(jnp.int32), n_items.reshape(-1).astype(jnp.int32)


def _fox_attention(fqT, kaug, fvT, drel, stats):
    B, _, S = fqT.shape
    tq, tk = FOX_TQ, FOX_TK
    n_blk = S // INPROJ_TM
    nh = FOX_GROUP_HEADS
    n_groups = FOX_HEADS // nh
    drel4 = drel.reshape(B, n_groups, nh, S)
    c = stats[:, :, 0, :FOX_HEADS]
    c = jnp.transpose(c, (0, 2, 1)).reshape(B, n_groups, nh * n_blk, 1)
    crep = jnp.broadcast_to(c, (B, n_groups, nh * n_blk, tq))
    jstart, n_items = _prune_plan(stats, drel, S // tq)
    grid_spec = pltpu.PrefetchScalarGridSpec(
        num_scalar_prefetch=2,
        grid=(B, n_groups),
        in_specs=[
            pl.BlockSpec((1, nh * FOX_HEAD_DIM, S), lambda b, p, js, ni: (b, p, 0)),
            pl.BlockSpec((1, S, nh // 2 * MXU_DIM), lambda b, p, js, ni: (b, 0, p)),
            pl.BlockSpec((1, nh * FOX_HEAD_DIM, S), lambda b, p, js, ni: (b, p, 0)),
            pl.BlockSpec((1, 1, nh, S), lambda b, p, js, ni: (b, p, 0, 0)),
            pl.BlockSpec((1, 1, nh * n_blk, tq), lambda b, p, js, ni: (b, p, 0, 0)),
        ],
        out_specs=pl.BlockSpec((1, S, nh * FOX_HEAD_DIM), lambda b, p, js, ni: (b, 0, p)),
        scratch_shapes=[
            pltpu.VMEM((2 * nh, MXU_DIM, tq), BF16),
            pltpu.VMEM((nh, tk, tq), F32),
            pltpu.VMEM((nh, tk, tq), BF16),
            pltpu.VMEM((nh * (tk // tq), SUBLANES, tq), F32),
            pltpu.VMEM((nh, FOX_HEAD_DIM + FOX_SUM_ROWS, tq), F32),
            pltpu.VMEM((nh, 1, tq), F32),
            pltpu.VMEM((nh, 1, tq), F32),
            pltpu.VMEM((nh, 1, tq), F32),
            pltpu.VMEM((nh, 1, tq), F32),
        ])
    return pl.pallas_call(
        _fox_kernel,
        grid_spec=grid_spec,
        out_shape=jax.ShapeDtypeStruct((B, S, FOX_WIDTH), BF16),
        compiler_params=pltpu.CompilerParams(
            dimension_semantics=("arbitrary", "arbitrary"),
            vmem_limit_bytes=VMEM_LIMIT),
        name="fox_attn",
    )(jstart, n_items, fqT, kaug, fvT, drel4, crep)


def _gla_kernel(q_ref, kT_ref, v_ref, gate_ref, sm_ref, wa_ref, ba_ref, o_ref, state_ref):
    nb, ts = q_ref.shape[0], q_ref.shape[1]
    C, P = GLA_CHUNK, GLA_PAIR
    dk, dv, H = GLA_KEY_DIM, GLA_VAL_DIM, GLA_HEADS
    KW = GLA_KEY_WIDTH

    @pl.when(pl.program_id(0) == 0)
    def _():
        state_ref[...] = jnp.zeros_like(state_ref)

    la_all = [_log_sigmoid(_dot(sm_ref[bb].astype(BF16), wa_ref[...]) + ba_ref[...])
              * (1.0 / GLA_GATE_TEMP) for bb in range(nb)]

    ri = lax.broadcasted_iota(jnp.int32, (P, P), 0)
    ci = lax.broadcasted_iota(jnp.int32, (P, P), 1)
    tril2 = jnp.where(((ri < C) == (ci < C)) & (ci <= ri), 1.0, 0.0).astype(BF16)

    rs = lax.broadcasted_iota(jnp.int32, (2 * H * C, P), 0)
    cs = lax.broadcasted_iota(jnp.int32, (2 * H * C, P), 1)
    causal = ((rs >= H * C) == (cs >= C)) & ((cs & (C - 1)) <= (rs & (C - 1)))

    lane_q = lax.broadcasted_iota(jnp.int32, (C, KW), 1)
    lane_t = lax.broadcasted_iota(jnp.int32, (KW, P), 1)
    first = lane_t < C

    per_b = ts // P
    n_slabs = nb * per_b
    slab_b = [k // per_b for k in range(n_slabs)]
    slab_rows = [slice((k % per_b) * P, (k % per_b + 1) * P) for k in range(n_slabs)]
    bs = [_tri_left(tril2, la_all[slab_b[k]][slab_rows[k]]) for k in range(n_slabs)]

    qms, scores, kv0s, kv1s, dec0s, dec1s, vss = [], [], [], [], [], [], []
    for pi in range(n_slabs):
        bb, rows = slab_b[pi], slab_rows[pi]
        b = bs[pi]
        bT = b.T
        bl0 = bT[:, C - 1:C]
        bl1 = bT[:, P - 1:P]
        q_dec = q_ref[bb, rows, :] * jnp.exp(b)
        kT = kT_ref[bb, :, rows]
        k_invT = (kT * jnp.exp(-bT)).astype(BF16)
        k_teT = kT * jnp.exp(jnp.where(first, bl0, bl1) - bT)
        zt = jnp.zeros_like(k_teT)
        k_te0 = jnp.where(first, k_teT, zt).astype(BF16)
        k_te1 = jnp.where(first, zt, k_teT).astype(BF16)
        dec0 = jnp.exp(bl0)
        dec1 = jnp.exp(bl1)

        vs = [v_ref[bb, rows, h * dv:(h + 1) * dv] for h in range(H)]
        kv = [_dot(jnp.concatenate([k_te0[h * dk:(h + 1) * dk, :], k_te1[h * dk:(h + 1) * dk, :]], axis=0),
                   vs[h]) for h in range(H)]
        kv0s.append(jnp.concatenate([kv[h][:dk] for h in range(H)], axis=0))
        kv1s.append(jnp.concatenate([kv[h][dk:] for h in range(H)], axis=0))
        dec0s.append(dec0)
        dec1s.append(dec1)
        vss.append(vs)

        zq = jnp.zeros((C, KW), F32)
        qm = [jnp.concatenate(
            [jnp.where((lane_q >= h * dk) & (lane_q < (h + 1) * dk), q_dec[c * C:(c + 1) * C], zq)
             for h in range(H)], axis=0).astype(BF16) for c in range(2)]
        a = _dot(jnp.concatenate(qm, axis=0), k_invT)
        scores.append(jnp.where(causal, a, 0.0).astype(BF16))
        qms.append(qm)

    o_inters = []
    for pi in range(n_slabs):
        if pi % per_b == 0:
            s = state_ref[slab_b[pi]]
        s1 = dec0s[pi] * s + kv0s[pi]
        o_inters.append([_dot(qms[pi][0], s.astype(BF16)), _dot(qms[pi][1], s1.astype(BF16))])
        s = dec1s[pi] * s1 + kv1s[pi]
        if pi % per_b == per_b - 1:
            state_ref[slab_b[pi]] = s

    for pi in range(n_slabs):
        bb, rows = slab_b[pi], slab_rows[pi]
        a, o_inter, vs = scores[pi], o_inters[pi], vss[pi]
        for h in range(H):
            hv = slice(h * dv, (h + 1) * dv)
            a_h = jnp.concatenate([a[(c * H + h) * C:(c * H + h + 1) * C] for c in range(2)], axis=0)
            o = _dot(a_h, vs[h]) + jnp.concatenate(
                [o_inter[c][h * C:(h + 1) * C] for c in range(2)], axis=0)
            o = o * lax.rsqrt(jnp.mean(o * o, axis=-1, keepdims=True) + RMS_EPS)
            o_ref[bb, rows, hv] = (o * gate_ref[bb, rows, hv]).astype(o_ref.dtype)


def _gla_attention(gq, gkT, gv, gate, small, w_a2_pad, b_a):
    B, S, _ = gq.shape
    ts = GLA_TS
    tok = lambda w: pl.BlockSpec((B, ts, w), lambda s: (0, s, 0))
    full = lambda r, c: pl.BlockSpec((r, c), lambda s: (0, 0))
    return pl.pallas_call(
        _gla_kernel,
        grid=(S // ts,),
        in_specs=[
            tok(GLA_KEY_WIDTH), pl.BlockSpec((B, GLA_KEY_WIDTH, ts), lambda s: (0, 0, s)),
            tok(GLA_WIDTH), tok(GLA_WIDTH), tok(SMALL_W),
            full(SMALL_W, GLA_KEY_WIDTH), full(1, GLA_KEY_WIDTH),
        ],
        out_specs=tok(GLA_WIDTH),
        out_shape=jax.ShapeDtypeStruct((B, S, GLA_WIDTH), BF16),
        scratch_shapes=[pltpu.VMEM((B, GLA_KEY_WIDTH, GLA_VAL_DIM), F32)],
        compiler_params=pltpu.CompilerParams(
            dimension_semantics=("arbitrary",), vmem_limit_bytes=VMEM_LIMIT),
        name="gla_attn",
    )(gq, gkT, gv, gate, small, w_a2_pad, b_a)


def _layer_norm(z, g, b):
    mu = jnp.mean(z, axis=-1, keepdims=True)
    d = z - mu
    var = jnp.mean(d * d, axis=-1, keepdims=True)
    return d * lax.rsqrt(var + LN_EPS) * g + b


def _tail_kernel(x_ref, fox_ref, gla_ref, mod_ref, wof_ref, wog_ref, ln1g_ref, ln1b_ref,
                 wg_ref, wu_ref, wd_ref, ln2g_ref, ln2b_ref, o_ref):
    m = mod_ref[0]
    tm = x_ref.shape[0]
    halves = [slice(0, tm // 2), slice(tm // 2, tm)]
    ys = [_dot(fox_ref[r, :], wof_ref[...]) + _dot(gla_ref[r, :], wog_ref[...]) for r in halves]
    x1s, gs, ups = [], [], []
    for r, y in zip(halves, ys):
        x1 = _layer_norm(DEEPNORM_ALPHA * x_ref[r, :] + (1.0 + m[2:3]) * y,
                         ln1g_ref[...], ln1b_ref[...])
        u2 = (x1 * (1.0 + m[4:5]) + m[3:4]).astype(BF16)
        x1s.append(x1)
        gs.append(_dot(u2, wg_ref[...]))
        ups.append(_dot(u2, wu_ref[...]))
    y2s = []
    for g, up in zip(gs, ups):
        h = (g * (1.0 / (1.0 + jnp.exp(-g))) * up).astype(BF16)
        y2s.append(_dot(h, wd_ref[...]))
    for r, x1, y2 in zip(halves, x1s, y2s):
        o_ref[r, :] = _layer_norm(DEEPNORM_ALPHA * x1 + (1.0 + m[5:6]) * y2,
                                  ln2g_ref[...], ln2b_ref[...])


def _tail(x2d, fox2d, gla2d, mod, wo, ln1g, ln1b, wg, wu, wd, ln2g, ln2b, steps_per_batch):
    T, D = x2d.shape
    tm = TAIL_TM
    const = lambda r, c: pl.BlockSpec((r, c), lambda i: (0, 0), pipeline_mode=pl.Buffered(1))
    tok = lambda w: pl.BlockSpec((tm, w), lambda i: (i, 0))
    return pl.pallas_call(
        _tail_kernel,
        grid=(T // tm,),
        in_specs=[
            tok(D), tok(FOX_WIDTH), tok(GLA_WIDTH),
            pl.BlockSpec((1, N_MOD, D), lambda i: (i // steps_per_batch, 0, 0)),
            pl.BlockSpec((FOX_WIDTH, D), lambda i: (0, 0), pipeline_mode=pl.Buffered(1)),
            pl.BlockSpec((GLA_WIDTH, D), lambda i: (1, 0), pipeline_mode=pl.Buffered(1)),
            const(1, D), const(1, D),
            const(D, D_FF), const(D, D_FF), const(D_FF, D), const(1, D), const(1, D),
        ],
        out_specs=tok(D),
        out_shape=jax.ShapeDtypeStruct((T, D), F32),
        compiler_params=pltpu.CompilerParams(
            dimension_semantics=("arbitrary",), vmem_limit_bytes=VMEM_LIMIT),
        name="tail",
    )(x2d, fox2d, gla2d, mod, wo, wo, ln1g, ln1b, wg, wu, wd, ln2g, ln2b)


def kernel(x, c, w_c, b_c, w_in, b_f, w_a2, b_a, g_gla, w_o, ln1_g, ln1_b,
           w_gate, w_up, w_down, ln2_g, ln2_b):
    B, S, D = x.shape
    assert D == D_MODEL and S % INPROJ_TM == 0 and INPROJ_TM == FOX_TK and FOX_TK % FOX_TQ == 0

    b_f_pad = jnp.zeros((1, SMALL_W), F32).at[0, :FOX_HEADS].set(b_f)
    w_a2_pad = jnp.zeros((SMALL_W, GLA_KEY_WIDTH), F32) \
        .at[FOX_HEADS:FOX_HEADS + GLA_GATE_RANK].set(w_a2).astype(BF16)

    mod, w_all = _modulation(c, w_c, b_c, w_in)

    fqT, kaug, fvT, gq, gkT, gv, gate, small, drel, stats, wg16, wu16, wd16 = _in_projection(
        x, mod, w_all, b_f_pad, g_gla.reshape(1, -1), w_gate, w_up, w_down)
    fox = _fox_attention(fqT, kaug, fvT, drel, stats)
    gla = _gla_attention(gq, gkT, gv, gate, small, w_a2_pad, b_a.reshape(1, -1))

    T = B * S
    w_o_b = w_o.astype(BF16)
    out = _tail(x.reshape(T, D), fox.reshape(T, FOX_WIDTH), gla.reshape(T, GLA_WIDTH), mod,
                w_o_b,
                ln1_g.reshape(1, D), ln1_b.reshape(1, D),
                wg16, wu16, wd16,
                ln2_g.reshape(1, D), ln2_b.reshape(1, D), S // TAIL_TM)
    return out.reshape(B, S, D)
```

```python
import functools
import math

import numpy as np
import jax
import jax.numpy as jnp
from jax import lax
from jax.experimental import pallas as pl
from jax.experimental.pallas import tpu as pltpu

F32 = jnp.float32
BF16 = jnp.bfloat16

D_MODEL = 1024
FOX_WIDTH = 512
FOX_HEAD_DIM = 64
FOX_HEADS = 8
FOX_PAIRS = FOX_HEADS // 2
GLA_WIDTH = 512
GLA_HEADS = 4
GLA_VAL_DIM = 128
GLA_KEY_DIM = 64
GLA_KEY_WIDTH = 256
GLA_GATE_RANK = 16
GLA_GATE_TEMP = 16.0
GLA_CHUNK = 64
D_FF = 2816
LN_EPS = 1e-5
RMS_EPS = 1e-6
N_MOD = 6
DEEPNORM_ALPHA = 2.0 ** 0.25
IN_SPLITS = (FOX_WIDTH, FOX_WIDTH, FOX_WIDTH, FOX_HEADS, GLA_KEY_WIDTH, GLA_KEY_WIDTH,
             GLA_WIDTH, GLA_GATE_RANK, GLA_WIDTH)
LOG2E = math.log2(math.e)

LANES = 128
SUBLANES = 8
MXU_DIM = 256
SMALL_W = LANES
VMEM_LIMIT = 56 * 1024 * 1024

MOD_TN = 768
INPROJ_TM = 512
FOX_TQ = 256
FOX_GROUP_HEADS = 4
FOX_SUM_ROWS = 16
FOX_TK = 512
GLA_TS = 512
GLA_PAIR = 2 * GLA_CHUNK
TAIL_TM = 512

NEG_BIG = -1e30
PRUNE_LOG2 = -1080.0
NORM_SLACK = 1.02

AUG_ROWS = 32


def _log_sigmoid(z):
    return jnp.minimum(z, 0.0) - jnp.log(1.0 + jnp.exp(-jnp.abs(z)))


def _split3(a):
    hi = a.astype(BF16)
    r1 = a - hi.astype(F32)
    mid = r1.astype(BF16)
    lo = (r1 - mid.astype(F32)).astype(BF16)
    return hi, mid, lo


def _dot(a, b):
    return jnp.dot(a, b, preferred_element_type=F32)


def _tri_left(tri, a):
    hi, mid, lo = _split3(a)
    return _dot(tri, hi) + _dot(tri, mid) + _dot(tri, lo)


def _tri_right(a, tri):
    hi, mid, lo = _split3(a)
    return _dot(hi, tri) + _dot(mid, tri) + _dot(lo, tri)


def _regroup_blocks():
    blocks = []
    for dst, src, width in _W_GROUPS:
        blocks += [(dst + c0, src + c0) for c0 in range(0, width, MXU_DIM)]
    blocks.append((_C_SM, None))
    return blocks


def _mod_kernel(cT_ref, w_ref, b_ref, winT_ref, o_ref, wall_ref, *, n_rows):
    w = w_ref[...]
    outs = [jnp.sum(w * cT_ref[:, b:b + 1], axis=0, keepdims=True) for b in range(n_rows)]
    outs.append(jnp.zeros((o_ref.shape[0] - n_rows, w.shape[1]), F32))
    o_ref[...] = jnp.concatenate(outs, axis=0) + b_ref[...]

    j = pl.program_id(0)
    n_steps = pl.num_programs(0)
    for k, (dst, src) in enumerate(_regroup_blocks()):
        @pl.when(j == k % n_steps)
        def _():
            if src is None:
                blk = jnp.concatenate(
                    [winT_ref[_SRC_FF:_SRC_FF + FOX_HEADS, :], winT_ref[_SRC_GA:_SRC_GA + GLA_GATE_RANK, :],
                     jnp.zeros((MXU_DIM - FOX_HEADS - GLA_GATE_RANK, winT_ref.shape[1]), F32)], axis=0)
            else:
                blk = winT_ref[src:src + MXU_DIM, :]
            wall_ref[:, dst:dst + MXU_DIM] = blk.T.astype(BF16)


def _modulation(c, w_c, b_c, w_in):
    B, D = c.shape
    N = w_c.shape[1]
    rows = SUBLANES
    n_steps = N // MOD_TN
    cT_pad = jnp.zeros((D, LANES), F32).at[:, :B].set(c.T)
    w_inT = w_in.T
    out, w_all = pl.pallas_call(
        functools.partial(_mod_kernel, n_rows=B),
        grid=(n_steps,),
        in_specs=[
            pl.BlockSpec((D, LANES), lambda j: (0, 0)),
            pl.BlockSpec((D, MOD_TN), lambda j: (0, j)),
            pl.BlockSpec((1, MOD_TN), lambda j: (0, j)),
            pl.BlockSpec(w_inT.shape, lambda j: (0, 0), pipeline_mode=pl.Buffered(1)),
        ],
        out_specs=[pl.BlockSpec((rows, MOD_TN), lambda j: (0, j)),
                   pl.BlockSpec((D, _W_COLS), lambda j: (0, 0))],
        out_shape=[jax.ShapeDtypeStruct((rows, N), F32),
                   jax.ShapeDtypeStruct((D, _W_COLS), BF16)],
        compiler_params=pltpu.CompilerParams(
            dimension_semantics=("arbitrary",), vmem_limit_bytes=VMEM_LIMIT),
        name="adaln_mod",
    )(cT_pad, w_c, b_c.reshape(1, N), w_inT)
    return out[:B].reshape(B, N_MOD, D), w_all


_C_FQ, _C_FK, _C_FV = 0, 512, 1024
_C_GQ, _C_SM, _C_GK, _C_GV, _C_GR = 1536, 1792, 2048, 2304, 2816
_W_COLS = _C_GR + GLA_WIDTH
_SRC = [sum(IN_SPLITS[:i]) for i in range(len(IN_SPLITS))]
_SRC_FF, _SRC_GA = _SRC[3], _SRC[7]
_W_GROUPS = ((_C_FQ, _SRC[0], FOX_WIDTH), (_C_FK, _SRC[1], FOX_WIDTH), (_C_FV, _SRC[2], FOX_WIDTH),
             (_C_GQ, _SRC[4], GLA_KEY_WIDTH), (_C_GK, _SRC[5], GLA_KEY_WIDTH),
             (_C_GV, _SRC[6], GLA_WIDTH), (_C_GR, _SRC[8], GLA_WIDTH))
_KAUG_W = FOX_PAIRS * MXU_DIM


def _head_indicator():
    ind = np.zeros((FOX_WIDTH, MXU_DIM), np.float32)
    for h in range(FOX_HEADS):
        ind[h * FOX_HEAD_DIM:(h + 1) * FOX_HEAD_DIM, h] = 1.0
    return ind


def _inproj_kernel(x_ref, mod_ref, w_ref, bf_ref, ind_ref, gg_ref, wg32_ref, wu32_ref, wd32_ref,
                   fqT_ref, kaug_ref, fvT_ref, gq_ref, gk_ref, gv_ref, gr_ref, sm_ref,
                   drel_ref, stats_ref, wg16_ref, wu16_ref, wd16_ref, carry_ref):
    tm = x_ref.shape[1]

    wg16_ref[...] = wg32_ref[...].astype(BF16)
    wu16_ref[...] = wu32_ref[...].astype(BF16)
    wd16_ref[...] = wd32_ref[...].astype(BF16)

    @pl.when(pl.program_id(1) == 0)
    def _():
        carry_ref[...] = jnp.zeros_like(carry_ref)

    m = mod_ref[0]
    u = (x_ref[0] * (1.0 + m[1:2]) + m[0:1]).astype(BF16)

    def proj(lo, width):
        return _dot(u, w_ref[:, lo:lo + width])

    gq_small = proj(_C_GQ, 2 * GLA_KEY_WIDTH)
    gq_ref[0] = gq_small[:, :GLA_KEY_WIDTH] * (GLA_KEY_DIM ** -0.5)
    small = gq_small[:, GLA_KEY_WIDTH:GLA_KEY_WIDTH + SMALL_W]
    sm_ref[0] = small

    fq = proj(_C_FQ, FOX_WIDTH) * (FOX_HEAD_DIM ** -0.5 * LOG2E)
    fqT_ref[0] = fq.T.astype(BF16)
    fk32 = proj(_C_FK, FOX_WIDTH)
    fk = fk32.astype(BF16)
    qn2 = jnp.max(_dot((fq * fq).astype(BF16), ind_ref[...])[:, :SMALL_W], axis=0, keepdims=True)
    kn2 = jnp.max(_dot((fk32 * fk32).astype(BF16), ind_ref[...])[:, :SMALL_W], axis=0, keepdims=True)

    zT = (small + bf_ref[...]).T[:FOX_HEADS, :]
    lfT = _log_sigmoid(zT) * LOG2E
    r = lax.broadcasted_iota(jnp.int32, (tm, tm), 0)
    c = lax.broadcasted_iota(jnp.int32, (tm, tm), 1)
    triu = jnp.where(r <= c, 1.0, 0.0).astype(BF16)
    drelT = _tri_right(lfT, triu)
    drel_ref[0] = drelT

    c_before = carry_ref[...]
    srow = lax.broadcasted_iota(jnp.int32, (SUBLANES, SMALL_W), 0)
    slane = lax.broadcasted_iota(jnp.int32, (SUBLANES, SMALL_W), 1)
    total = jnp.sum(jnp.where(srow == slane, drelT[:, tm - 1:tm], 0.0), axis=0, keepdims=True)
    carry_ref[...] = c_before + total

    hi, mid, lo = _split3(drelT)
    ones3 = jnp.where(lax.broadcasted_iota(jnp.int32, (SUBLANES, tm), 0) < 3, 1.0, 0.0)
    slabT = jnp.concatenate(
        [-hi.astype(F32), -mid.astype(F32), -lo.astype(F32), ones3,
         jnp.zeros((LANES - 4 * SUBLANES, tm), F32)], axis=0)
    slab = slabT.T.astype(BF16)

    fvT_ref[0] = proj(_C_FV, FOX_WIDTH).T.astype(BF16)
    gv_ref[0] = proj(_C_GV, GLA_WIDTH).astype(BF16)
    gr = proj(_C_GR, GLA_WIDTH)
    gr_ref[0] = gg_ref[...] * (gr * (1.0 / (1.0 + jnp.exp(-gr))))
    gk_ref[0] = proj(_C_GK, GLA_KEY_WIDTH).T

    stats_ref[0, 0] = jnp.where(srow == 0, c_before,
                                jnp.where(srow == 1, qn2, jnp.where(srow == 2, kn2, 0.0)))
    for p in range(FOX_PAIRS):
        kaug_ref[0, :, p * MXU_DIM:p * MXU_DIM + LANES] = fk[:, p * LANES:(p + 1) * LANES]
        kaug_ref[0, :, p * MXU_DIM + LANES:(p + 1) * MXU_DIM] = slab


def _in_projection(x, mod, w_all, b_f_pad, g_gla, w_gate, w_up, w_down):
    B, S, D = x.shape
    tm = INPROJ_TM
    n_steps = B * (S // tm)
    steps_b = S // tm
    up_rows = D // n_steps
    down_rows = 2 * D_FF // n_steps
    assert D % n_steps == 0 and up_rows % 16 == 0 and (2 * D_FF) % n_steps == 0 and down_rows % 16 == 0
    up_spec = pl.BlockSpec((up_rows, D_FF), lambda b, s: (b * steps_b + s, 0))
    down_spec = pl.BlockSpec((down_rows, D), lambda b, s: ((b * steps_b + s) // 2, 0))
    ind = _head_indicator()
    tok = lambda w: pl.BlockSpec((1, tm, w), lambda b, s: (b, s, 0))
    tokT = lambda h: pl.BlockSpec((1, h, tm), lambda b, s: (b, 0, s))
    outs = pl.pallas_call(
        _inproj_kernel,
        grid=(B, S // tm),
        in_specs=[
            tok(D),
            pl.BlockSpec((1, N_MOD, D), lambda b, s: (b, 0, 0)),
            pl.BlockSpec((D, _W_COLS), lambda b, s: (0, 0), pipeline_mode=pl.Buffered(1)),
            pl.BlockSpec((1, SMALL_W), lambda b, s: (0, 0)),
            pl.BlockSpec(ind.shape, lambda b, s: (0, 0)),
            pl.BlockSpec((1, GLA_WIDTH), lambda b, s: (0, 0)),
            up_spec, up_spec, down_spec,
        ],
        out_specs=[
            tokT(FOX_WIDTH), tok(_KAUG_W), tokT(FOX_WIDTH),
            tok(GLA_KEY_WIDTH), tokT(GLA_KEY_WIDTH), tok(GLA_WIDTH), tok(GLA_WIDTH),
            tok(SMALL_W),
            tokT(FOX_HEADS),
            pl.BlockSpec((1, 1, SUBLANES, SMALL_W), lambda b, s: (b, s, 0, 0)),
            up_spec, up_spec, down_spec,
        ],
        out_shape=[
            jax.ShapeDtypeStruct((B, FOX_WIDTH, S), BF16),
            jax.ShapeDtypeStruct((B, S, _KAUG_W), BF16),
            jax.ShapeDtypeStruct((B, FOX_WIDTH, S), BF16),
            jax.ShapeDtypeStruct((B, S, GLA_KEY_WIDTH), F32),
            jax.ShapeDtypeStruct((B, GLA_KEY_WIDTH, S), F32),
            jax.ShapeDtypeStruct((B, S, GLA_WIDTH), BF16),
            jax.ShapeDtypeStruct((B, S, GLA_WIDTH), F32),
            jax.ShapeDtypeStruct((B, S, SMALL_W), F32),
            jax.ShapeDtypeStruct((B, FOX_HEADS, S), F32),
            jax.ShapeDtypeStruct((B, S // tm, SUBLANES, SMALL_W), F32),
            jax.ShapeDtypeStruct(w_gate.shape, BF16),
            jax.ShapeDtypeStruct(w_up.shape, BF16),
            jax.ShapeDtypeStruct(w_down.shape, BF16),
        ],
        scratch_shapes=[pltpu.VMEM((1, SMALL_W), F32)],
        compiler_params=pltpu.CompilerParams(
            dimension_semantics=("arbitrary", "arbitrary"), vmem_limit_bytes=VMEM_LIMIT),
        name="in_proj",
    )(x, mod, w_all, b_f_pad, jnp.asarray(ind, BF16), g_gla, w_gate, w_up, w_down)
    return outs


def _fox_kernel(ntiles_ref, nitems_ref, qT_ref, kaug_ref, vT_ref, drel_ref, crep_ref, o_ref,
                qaug_ref, s_ref, p_ref, mx_ref, acc_ref, m_ref, mnext_ref, shift_ref, alpha_ref):
    tq, tk = FOX_TQ, FOX_TK
    half = FOX_HEAD_DIM
    S = kaug_ref.shape[1]
    nq = S // tq
    nh = FOX_GROUP_HEADS
    n_blk = crep_ref.shape[2] // nh
    g = pl.program_id(0) * pl.num_programs(1) + pl.program_id(1)

    def build_qaug(qi):
        q0 = pl.multiple_of(qi * tq, tq)
        dr = drel_ref[0, 0, :, pl.ds(q0, tq)]
        row = lax.broadcasted_iota(jnp.int32, (2 * half, tq), 0)
        r32 = lax.broadcasted_iota(jnp.int32, (AUG_ROWS, tq), 0)
        for h in range(nh):
            buf = (qi % 2) * nh + h
            qT = qT_ref[0, (h // 2) * 2 * half:(h // 2 + 1) * 2 * half, pl.ds(q0, tq)]
            own = (row < half) if h % 2 == 0 else (row >= half)
            qaug_ref[buf, 0:2 * half, :] = jnp.where(own, qT, jnp.zeros_like(qT))
            hi, mid, lo = _split3(dr[h:h + 1])
            gh = nh * pl.program_id(1) + h
            pick = (r32 == gh) | (r32 == gh + SUBLANES) | (r32 == gh + 2 * SUBLANES)
            slab = jnp.where(r32 == 24, hi.astype(F32),
                             jnp.where(r32 == 25, mid.astype(F32),
                                       jnp.where(r32 == 26, lo.astype(F32),
                                                 jnp.where(pick, 1.0, 0.0))))
            qaug_ref[buf, 2 * half:2 * half + AUG_ROWS, :] = slab.astype(BF16)

    def reset_state():
        acc_ref[...] = jnp.zeros_like(acc_ref)

    sum_rows = jnp.where(lax.broadcasted_iota(jnp.int32, (FOX_SUM_ROWS, tk), 0) == 0, 1.0, 0.0).astype(BF16)

    per = tk // tq

    def col_max8(s):
        return jnp.max(s.reshape(s.shape[0] // SUBLANES, SUBLANES, tq), axis=0)

    def tile_of(qi, t):
        raw = (qi + 1) * tq - (t + 1) * tk
        return pl.multiple_of(jnp.maximum(raw, 0), tq), raw < 0

    def qk_dots(qi, k_off):
        kts = [kaug_ref[0, pl.ds(k_off, tk), p * MXU_DIM:(p + 1) * MXU_DIM] for p in range(nh // 2)]
        return [_dot(kts[h // 2], qaug_ref[(qi % 2) * nh + h]) for h in range(nh)]

    def store_scores(scores):
        for h in range(nh):
            s_ref[h] = scores[h]
            for g in range(per):
                mx_ref[per * h + g] = col_max8(scores[h][g * tq:(g + 1) * tq])

    def fill_groups(d):
        for g in range(1, per):
            @pl.when(d < g)
            def _():
                for h in range(nh):
                    s_ref[h, g * tq:(g + 1) * tq, :] = jnp.full((tq, tq), NEG_BIG, F32)
                    mx_ref[per * h + g] = jnp.full((SUBLANES, tq), NEG_BIG, F32)

    def mask_stage(d):
        d0 = pl.multiple_of(d * tq, tq)
        kr = lax.broadcasted_iota(jnp.int32, (tq, tq), 0)
        qc = lax.broadcasted_iota(jnp.int32, (tq, tq), 1)
        for h in range(nh):
            tri = jnp.where(kr <= qc, s_ref[h, pl.ds(d0, tq), :], NEG_BIG)
            s_ref[h, pl.ds(d0, tq), :] = tri
            mx_ref[per * h + d] = col_max8(tri)
        fill_groups(d)

    def prep_stage(qi, k_off, first):
        i_blk = qi * tq // INPROJ_TM
        for h in range(nh):
            c_i = crep_ref[0, 0, pl.ds(h * n_blk + i_blk, 1), :]
            big_c = [c_i - crep_ref[0, 0, pl.ds(h * n_blk + (k_off + g * tq) // INPROJ_TM, 1), :]
                     for g in range(per)]
            cand = jnp.max(mx_ref[per * h], axis=0, keepdims=True) + big_c[0]
            for g in range(1, per):
                cand = jnp.maximum(cand, jnp.max(mx_ref[per * h + g], axis=0, keepdims=True) + big_c[g])
            m_prev = jnp.where(first, NEG_BIG, m_ref[h])
            m_new = jnp.maximum(m_prev, cand)
            mnext_ref[h] = m_new
            for g in range(per):
                shift_ref[per * h + g] = m_new - big_c[g]
            alpha_ref[h] = jnp.exp2(m_prev - m_new)

    def exp_stage():
        for h in range(nh):
            for g in range(per):
                rows = slice(g * tq, (g + 1) * tq)
                p_ref[h, rows, :] = jnp.exp2(s_ref[h, rows, :] - shift_ref[per * h + g]).astype(BF16)

    def pv_stage(k_off):
        for h in range(nh):
            vT = jnp.concatenate([vT_ref[0, h * half:(h + 1) * half, pl.ds(k_off, tk)], sum_rows], axis=0)
            acc_ref[h] = alpha_ref[h] * acc_ref[h] + _dot(vT, p_ref[h])
            m_ref[h] = mnext_ref[h]

    def finalize(qi):
        outs = []
        for h in range(nh):
            acc = acc_ref[h]
            outs.append(acc[:half] / acc[half:half + 1])
        oT = jnp.concatenate(outs, axis=0)
        o_ref[0, pl.ds(pl.multiple_of(qi * tq, tq), tq), :] = oT.T.astype(o_ref.dtype)

    def fix_masks(t, clamped):
        @pl.when(t == 0)
        def _():
            mask_stage(jnp.where(clamped, 0, per - 1))

        @pl.when((t > 0) & clamped)
        def _():
            fill_groups(0)

    zero = jnp.int32(0)
    qaug_ref[...] = jnp.zeros_like(qaug_ref)
    build_qaug(zero)
    build_qaug(zero + 1)
    reset_state()
    k0, cl0 = tile_of(zero, zero)
    store_scores(qk_dots(zero, k0))
    fix_masks(zero, cl0)
    prep_stage(zero, k0, True)

    def body(it, carry):
        qi, t = carry
        last = t == 0
        qi2 = jnp.where(last, qi + 1, qi)
        t2 = jnp.where(last, ntiles_ref[g * nq + jnp.minimum(qi + 1, nq - 1)] - 1, t - 1)
        k_cur, _ = tile_of(qi, t)
        k_nxt, cl_nxt = tile_of(qi2, t2)
        exp_stage()
        nxt = qk_dots(qi2, k_nxt)
        pv_stage(k_cur)
        store_scores(nxt)
        prep_stage(qi2, k_nxt, last)

        @pl.when((t2 == 0) | cl_nxt)
        def _():
            fix_masks(t2, cl_nxt)
            prep_stage(qi2, k_nxt, last)

        @pl.when(last)
        def _():
            finalize(qi)
            reset_state()

            @pl.when(qi + 2 < nq)
            def _():
                build_qaug(qi + 2)

        return qi2, t2

    qi, t = lax.fori_loop(0, nitems_ref[g] - 1, body, (jnp.int32(0), jnp.int32(0)))
    exp_stage()
    pv_stage(tile_of(qi, t)[0])
    finalize(qi)


def _prune_plan(stats, drel, nq):
    B, n_blk = stats.shape[:2]
    per = FOX_TK // FOX_TQ
    n_t_max = nq // per + 1
    c = stats[:, :, 0, :FOX_HEADS]
    qn = jnp.sqrt(stats[:, :, 1, :FOX_HEADS])
    kn = jnp.sqrt(stats[:, :, 2, :FOX_HEADS])
    d_in = jnp.transpose(drel[:, :, FOX_TQ - 1::FOX_TQ], (0, 2, 1))
    d_end = jnp.repeat(c, per, axis=1) + d_in
    d_before = jnp.concatenate([jnp.zeros_like(d_end[:, :1]), d_end[:, :-1]], axis=1)
    qi = jnp.arange(nq)[:, None]
    t = jnp.arange(n_t_max)[None, :]
    m_t = qi - per * t
    valid = (t >= 1) & (m_t >= 0)
    m_hi = jnp.clip(m_t, 0, nq - 1)
    m_lo = jnp.clip(m_t - 1, 0, nq - 1)
    kn_q = jnp.repeat(kn, per, axis=1)
    qn_q = jnp.repeat(qn, per, axis=1)
    kn_tile = jnp.maximum(kn_q[:, m_hi], kn_q[:, m_lo])
    bound = (d_before[:, :, None, :] - d_end[:, m_hi]
             + NORM_SLACK * qn_q[:, :, None, :] * (kn_tile + kn_q[:, :, None, :]))
    needed = valid[None, :, :, None] & ~(bound <= PRUNE_LOG2)
    nh = FOX_GROUP_HEADS
    needed_g = needed[..., 0::nh]
    for h in range(1, nh):
        needed_g = needed_g | needed[..., h::nh]
    n_tiles = 1 + jnp.max(jnp.where(needed_g, t[None, :, :, None], 0), axis=2)
    n_tiles = jnp.transpose(n_tiles, (0, 2, 1))
    n_items = jnp.sum(n_tiles, axis=2)
    return n_tiles.reshape(-1).astype(jnp.int32), n_items.reshape(-1).astype(jnp.int32)


def _fox_attention(fqT, kaug, fvT, drel, stats):
    B, _, S = fqT.shape
    tq, tk = FOX_TQ, FOX_TK
    n_blk = S // INPROJ_TM
    nh = FOX_GROUP_HEADS
    n_groups = FOX_HEADS // nh
    drel4 = drel.reshape(B, n_groups, nh, S)
    c = stats[:, :, 0, :FOX_HEADS]
    c = jnp.transpose(c, (0, 2, 1)).reshape(B, n_groups, nh * n_blk, 1)
    crep = jnp.broadcast_to(c, (B, n_groups, nh * n_blk, tq))
    n_tiles, n_items = _prune_plan(stats, drel, S // tq)
    grid_spec = pltpu.PrefetchScalarGridSpec(
        num_scalar_prefetch=2,
        grid=(B, n_groups),
        in_specs=[
            pl.BlockSpec((1, nh * FOX_HEAD_DIM, S), lambda b, p, js, ni: (b, p, 0)),
            pl.BlockSpec((1, S, nh // 2 * MXU_DIM), lambda b, p, js, ni: (b, 0, p)),
            pl.BlockSpec((1, nh * FOX_HEAD_DIM, S), lambda b, p, js, ni: (b, p, 0)),
            pl.BlockSpec((1, 1, nh, S), lambda b, p, js, ni: (b, p, 0, 0)),
            pl.BlockSpec((1, 1, nh * n_blk, tq), lambda b, p, js, ni: (b, p, 0, 0)),
        ],
        out_specs=pl.BlockSpec((1, S, nh * FOX_HEAD_DIM), lambda b, p, js, ni: (b, 0, p)),
        scratch_shapes=[
            pltpu.VMEM((2 * nh, MXU_DIM, tq), BF16),
            pltpu.VMEM((nh, tk, tq), F32),
            pltpu.VMEM((nh, tk, tq), BF16),
            pltpu.VMEM((nh * (tk // tq), SUBLANES, tq), F32),
            pltpu.VMEM((nh, FOX_HEAD_DIM + FOX_SUM_ROWS, tq), F32),
            pltpu.VMEM((nh, 1, tq), F32),
            pltpu.VMEM((nh, 1, tq), F32),
            pltpu.VMEM((nh * (tk // tq), 1, tq), F32),
            pltpu.VMEM((nh, 1, tq), F32),
        ])
    return pl.pallas_call(
        _fox_kernel,
        grid_spec=grid_spec,
        out_shape=jax.ShapeDtypeStruct((B, S, FOX_WIDTH), BF16),
        compiler_params=pltpu.CompilerParams(
            dimension_semantics=("arbitrary", "arbitrary"),
            vmem_limit_bytes=VMEM_LIMIT),
        name="fox_attn",
    )(n_tiles, n_items, fqT, kaug, fvT, drel4, crep)


def _gla_kernel(q_ref, kT_ref, v_ref, gate_ref, sm_ref, wa_ref, ba_ref, o_ref, state_ref):
    nb, ts = q_ref.shape[0], q_ref.shape[1]
    C, P = GLA_CHUNK, GLA_PAIR
    dk, dv, H = GLA_KEY_DIM, GLA_VAL_DIM, GLA_HEADS
    KW = GLA_KEY_WIDTH

    @pl.when(pl.program_id(0) == 0)
    def _():
        state_ref[...] = jnp.zeros_like(state_ref)

    la_all = [_log_sigmoid(_dot(sm_ref[bb].astype(BF16), wa_ref[...]) + ba_ref[...])
              * (1.0 / GLA_GATE_TEMP) for bb in range(nb)]

    ri = lax.broadcasted_iota(jnp.int32, (P, P), 0)
    ci = lax.broadcasted_iota(jnp.int32, (P, P), 1)
    tril2 = jnp.where(((ri < C) == (ci < C)) & (ci <= ri), 1.0, 0.0).astype(BF16)

    rs = lax.broadcasted_iota(jnp.int32, (2 * H * C, P), 0)
    cs = lax.broadcasted_iota(jnp.int32, (2 * H * C, P), 1)
    causal = ((rs >= H * C) == (cs >= C)) & ((cs & (C - 1)) <= (rs & (C - 1)))

    lane_q = lax.broadcasted_iota(jnp.int32, (C, KW), 1)
    lane_t = lax.broadcasted_iota(jnp.int32, (KW, P), 1)
    first = lane_t < C

    per_b = ts // P
    n_slabs = nb * per_b
    slab_b = [k // per_b for k in range(n_slabs)]
    slab_rows = [slice((k % per_b) * P, (k % per_b + 1) * P) for k in range(n_slabs)]
    bs = [_tri_left(tril2, la_all[slab_b[k]][slab_rows[k]]) for k in range(n_slabs)]

    qms, scores, kv0s, kv1s, dec0s, dec1s, vss = [], [], [], [], [], [], []
    for pi in range(n_slabs):
        bb, rows = slab_b[pi], slab_rows[pi]
        b = bs[pi]
        bT = b.T
        bl0 = bT[:, C - 1:C]
        bl1 = bT[:, P - 1:P]
        q_dec = q_ref[bb, rows, :] * jnp.exp(b)
        kT = kT_ref[bb, :, rows]
        k_invT = (kT * jnp.exp(-bT)).astype(BF16)
        k_teT = kT * jnp.exp(jnp.where(first, bl0, bl1) - bT)
        zt = jnp.zeros_like(k_teT)
        k_te0 = jnp.where(first, k_teT, zt).astype(BF16)
        k_te1 = jnp.where(first, zt, k_teT).astype(BF16)
        dec0 = jnp.exp(bl0)
        dec1 = jnp.exp(bl1)

        vs = [v_ref[bb, rows, h * dv:(h + 1) * dv] for h in range(H)]
        kv = [_dot(jnp.concatenate([k_te0[h * dk:(h + 1) * dk, :], k_te1[h * dk:(h + 1) * dk, :]], axis=0),
                   vs[h]) for h in range(H)]
        kv0s.append(jnp.concatenate([kv[h][:dk] for h in range(H)], axis=0))
        kv1s.append(jnp.concatenate([kv[h][dk:] for h in range(H)], axis=0))
        dec0s.append(dec0)
        dec1s.append(dec1)
        vss.append(vs)

        zq = jnp.zeros((C, KW), F32)
        qm = [jnp.concatenate(
            [jnp.where((lane_q >= h * dk) & (lane_q < (h + 1) * dk), q_dec[c * C:(c + 1) * C], zq)
             for h in range(H)], axis=0).astype(BF16) for c in range(2)]
        a = _dot(jnp.concatenate(qm, axis=0), k_invT)
        scores.append(jnp.where(causal, a, 0.0).astype(BF16))
        qms.append(qm)

    o_inters = []
    for pi in range(n_slabs):
        if pi % per_b == 0:
            s = state_ref[slab_b[pi]]
        s1 = dec0s[pi] * s + kv0s[pi]
        o_inters.append([_dot(qms[pi][0], s.astype(BF16)), _dot(qms[pi][1], s1.astype(BF16))])
        s = dec1s[pi] * s1 + kv1s[pi]
        if pi % per_b == per_b - 1:
            state_ref[slab_b[pi]] = s

    for pi in range(n_slabs):
        bb, rows = slab_b[pi], slab_rows[pi]
        a, o_inter, vs = scores[pi], o_inters[pi], vss[pi]
        for h in range(H):
            hv = slice(h * dv, (h + 1) * dv)
            a_h = jnp.concatenate([a[(c * H + h) * C:(c * H + h + 1) * C] for c in range(2)], axis=0)
            o = _dot(a_h, vs[h]) + jnp.concatenate(
                [o_inter[c][h * C:(h + 1) * C] for c in range(2)], axis=0)
            o = o * lax.rsqrt(jnp.mean(o * o, axis=-1, keepdims=True) + RMS_EPS)
            o_ref[bb, rows, hv] = (o * gate_ref[bb, rows, hv]).astype(o_ref.dtype)


def _gla_attention(gq, gkT, gv, gate, small, w_a2_pad, b_a):
    B, S, _ = gq.shape
    ts = GLA_TS
    tok = lambda w: pl.BlockSpec((B, ts, w), lambda s: (0, s, 0))
    full = lambda r, c: pl.BlockSpec((r, c), lambda s: (0, 0))
    return pl.pallas_call(
        _gla_kernel,
        grid=(S // ts,),
        in_specs=[
            tok(GLA_KEY_WIDTH), pl.BlockSpec((B, GLA_KEY_WIDTH, ts), lambda s: (0, 0, s)),
            tok(GLA_WIDTH), tok(GLA_WIDTH), tok(SMALL_W),
            full(SMALL_W, GLA_KEY_WIDTH), full(1, GLA_KEY_WIDTH),
        ],
        out_specs=tok(GLA_WIDTH),
        out_shape=jax.ShapeDtypeStruct((B, S, GLA_WIDTH), BF16),
        scratch_shapes=[pltpu.VMEM((B, GLA_KEY_WIDTH, GLA_VAL_DIM), F32)],
        compiler_params=pltpu.CompilerParams(
            dimension_semantics=("arbitrary",), vmem_limit_bytes=VMEM_LIMIT),
        name="gla_attn",
    )(gq, gkT, gv, gate, small, w_a2_pad, b_a)


def _layer_norm(z, g, b):
    mu = jnp.mean(z, axis=-1, keepdims=True)
    d = z - mu
    var = jnp.mean(d * d, axis=-1, keepdims=True)
    return d * lax.rsqrt(var + LN_EPS) * g + b


def _tail_kernel(x_ref, fox_ref, gla_ref, mod_ref, wof_ref, wog_ref, ln1g_ref, ln1b_ref,
                 wg_ref, wu_ref, wd_ref, ln2g_ref, ln2b_ref, o_ref):
    m = mod_ref[0]
    tm = x_ref.shape[0]
    halves = [slice(0, tm // 2), slice(tm // 2, tm)]
    ys = [_dot(fox_ref[r, :], wof_ref[...]) + _dot(gla_ref[r, :], wog_ref[...]) for r in halves]
    x1s, gs, ups = [], [], []
    for r, y in zip(halves, ys):
        x1 = _layer_norm(DEEPNORM_ALPHA * x_ref[r, :] + (1.0 + m[2:3]) * y,
                         ln1g_ref[...], ln1b_ref[...])
        u2 = (x1 * (1.0 + m[4:5]) + m[3:4]).astype(BF16)
        x1s.append(x1)
        gs.append(_dot(u2, wg_ref[...]))
        ups.append(_dot(u2, wu_ref[...]))
    y2s = []
    for g, up in zip(gs, ups):
        h = (g * (1.0 / (1.0 + jnp.exp(-g))) * up).astype(BF16)
        y2s.append(_dot(h, wd_ref[...]))
    for r, x1, y2 in zip(halves, x1s, y2s):
        o_ref[r, :] = _layer_norm(DEEPNORM_ALPHA * x1 + (1.0 + m[5:6]) * y2,
                                  ln2g_ref[...], ln2b_ref[...])


def _tail(x2d, fox2d, gla2d, mod, wo, ln1g, ln1b, wg, wu, wd, ln2g, ln2b, steps_per_batch):
    T, D = x2d.shape
    tm = TAIL_TM
    const = lambda r, c: pl.BlockSpec((r, c), lambda i: (0, 0), pipeline_mode=pl.Buffered(1))
    tok = lambda w: pl.BlockSpec((tm, w), lambda i: (i, 0))
    return pl.pallas_call(
        _tail_kernel,
        grid=(T // tm,),
        in_specs=[
            tok(D), tok(FOX_WIDTH), tok(GLA_WIDTH),
            pl.BlockSpec((1, N_MOD, D), lambda i: (i // steps_per_batch, 0, 0)),
            pl.BlockSpec((FOX_WIDTH, D), lambda i: (0, 0), pipeline_mode=pl.Buffered(1)),
            pl.BlockSpec((GLA_WIDTH, D), lambda i: (1, 0), pipeline_mode=pl.Buffered(1)),
            const(1, D), const(1, D),
            const(D, D_FF), const(D, D_FF), const(D_FF, D), const(1, D), const(1, D),
        ],
        out_specs=tok(D),
        out_shape=jax.ShapeDtypeStruct((T, D), F32),
        compiler_params=pltpu.CompilerParams(
            dimension_semantics=("arbitrary",), vmem_limit_bytes=VMEM_LIMIT),
        name="tail",
    )(x2d, fox2d, gla2d, mod, wo, wo, ln1g, ln1b, wg, wu, wd, ln2g, ln2b)


def kernel(x, c, w_c, b_c, w_in, b_f, w_a2, b_a, g_gla, w_o, ln1_g, ln1_b,
           w_gate, w_up, w_down, ln2_g, ln2_b):
    B, S, D = x.shape
    assert D == D_MODEL and S % INPROJ_TM == 0 and INPROJ_TM == FOX_TK and FOX_TK % FOX_TQ == 0

    b_f_pad = jnp.zeros((1, SMALL_W), F32).at[0, :FOX_HEADS].set(b_f)
    w_a2_pad = jnp.zeros((SMALL_W, GLA_KEY_WIDTH), F32) \
        .at[FOX_HEADS:FOX_HEADS + GLA_GATE_RANK].set(w_a2).astype(BF16)

    mod, w_all = _modulation(c, w_c, b_c, w_in)

    fqT, kaug, fvT, gq, gkT, gv, gate, small, drel, stats, wg16, wu16, wd16 = _in_projection(
        x, mod, w_all, b_f_pad, g_gla.reshape(1, -1), w_gate, w_up, w_down)
    fox = _fox_attention(fqT, kaug, fvT, drel, stats)
    gla = _gla_attention(gq, gkT, gv, gate, small, w_a2_pad, b_a.reshape(1, -1))

    T = B * S
    w_o_b = w_o.astype(BF16)
    out = _tail(x.reshape(T, D), fox.reshape(T, FOX_WIDTH), gla.reshape(T, GLA_WIDTH), mod,
                w_o_b,
                ln1_g.reshape(1, D), ln1_b.reshape(1, D),
                wg16, wu16, wd16,
                ln2_g.reshape(1, D), ln2_b.reshape(1, D), S // TAIL_TM)
    return out.reshape(B, S, D)
```

```python
import functools
import math

import numpy as np
import jax
import jax.numpy as jnp
from jax import lax
from jax.experimental import pallas as pl
from jax.experimental.pallas import tpu as pltpu

F32 = jnp.float32
BF16 = jnp.bfloat16

D_MODEL = 1024
FOX_WIDTH = 512
FOX_HEAD_DIM = 64
FOX_HEADS = 8
FOX_PAIRS = FOX_HEADS // 2
GLA_WIDTH = 512
GLA_HEADS = 4
GLA_VAL_DIM = 128
GLA_KEY_DIM = 64
GLA_KEY_WIDTH = 256
GLA_GATE_RANK = 16
GLA_GATE_TEMP = 16.0
GLA_CHUNK = 64
D_FF = 2816
LN_EPS = 1e-5
RMS_EPS = 1e-6
N_MOD = 6
DEEPNORM_ALPHA = 2.0 ** 0.25
IN_SPLITS = (FOX_WIDTH, FOX_WIDTH, FOX_WIDTH, FOX_HEADS, GLA_KEY_WIDTH, GLA_KEY_WIDTH,
             GLA_WIDTH, GLA_GATE_RANK, GLA_WIDTH)
LOG2E = math.log2(math.e)

LANES = 128
SUBLANES = 8
MXU_DIM = 256
SMALL_W = LANES
VMEM_LIMIT = 56 * 1024 * 1024

MOD_TN = 768
INPROJ_TM = 512
FOX_TQ = 256
FOX_GROUP_HEADS = 4
FOX_SUM_ROWS = 16
FOX_TK = 512
GLA_TS = 512
GLA_PAIR = 2 * GLA_CHUNK
TAIL_TM = 512

NEG_BIG = -1e30
PRUNE_LOG2 = -1080.0
NORM_SLACK = 1.02

AUG_ROWS = 32


def _log_sigmoid(z):
    return jnp.minimum(z, 0.0) - jnp.log(1.0 + jnp.exp(-jnp.abs(z)))


def _split3(a):
    hi = a.astype(BF16)
    r1 = a - hi.astype(F32)
    mid = r1.astype(BF16)
    lo = (r1 - mid.astype(F32)).astype(BF16)
    return hi, mid, lo


def _dot(a, b):
    return jnp.dot(a, b, preferred_element_type=F32)


def _tri_left(tri, a):
    hi, mid, lo = _split3(a)
    return _dot(tri, hi) + _dot(tri, mid) + _dot(tri, lo)


def _tri_right(a, tri):
    hi, mid, lo = _split3(a)
    return _dot(hi, tri) + _dot(mid, tri) + _dot(lo, tri)


def _regroup_blocks():
    blocks = []
    for dst, src, width in _W_GROUPS:
        blocks += [(dst + c0, src + c0) for c0 in range(0, width, MXU_DIM)]
    blocks.append((_C_SM, None))
    return blocks


def _mod_kernel(cT_ref, w_ref, b_ref, winT_ref, o_ref, wall_ref, *, n_rows):
    w = w_ref[...]
    outs = [jnp.sum(w * cT_ref[:, b:b + 1], axis=0, keepdims=True) for b in range(n_rows)]
    outs.append(jnp.zeros((o_ref.shape[0] - n_rows, w.shape[1]), F32))
    o_ref[...] = jnp.concatenate(outs, axis=0) + b_ref[...]

    j = pl.program_id(0)
    n_steps = pl.num_programs(0)
    for k, (dst, src) in enumerate(_regroup_blocks()):
        @pl.when(j == k % n_steps)
        def _():
            if src is None:
                blk = jnp.concatenate(
                    [winT_ref[_SRC_FF:_SRC_FF + FOX_HEADS, :], winT_ref[_SRC_GA:_SRC_GA + GLA_GATE_RANK, :],
                     jnp.zeros((MXU_DIM - FOX_HEADS - GLA_GATE_RANK, winT_ref.shape[1]), F32)], axis=0)
            else:
                blk = winT_ref[src:src + MXU_DIM, :]
            wall_ref[:, dst:dst + MXU_DIM] = blk.T.astype(BF16)


def _modulation(c, w_c, b_c, w_in):
    B, D = c.shape
    N = w_c.shape[1]
    rows = SUBLANES
    n_steps = N // MOD_TN
    cT_pad = jnp.zeros((D, LANES), F32).at[:, :B].set(c.T)
    w_inT = w_in.T
    out, w_all = pl.pallas_call(
        functools.partial(_mod_kernel, n_rows=B),
        grid=(n_steps,),
        in_specs=[
            pl.BlockSpec((D, LANES), lambda j: (0, 0)),
            pl.BlockSpec((D, MOD_TN), lambda j: (0, j)),
            pl.BlockSpec((1, MOD_TN), lambda j: (0, j)),
            pl.BlockSpec(w_inT.shape, lambda j: (0, 0), pipeline_mode=pl.Buffered(1)),
        ],
        out_specs=[pl.BlockSpec((rows, MOD_TN), lambda j: (0, j)),
                   pl.BlockSpec((D, _W_COLS), lambda j: (0, 0))],
        out_shape=[jax.ShapeDtypeStruct((rows, N), F32),
                   jax.ShapeDtypeStruct((D, _W_COLS), BF16)],
        compiler_params=pltpu.CompilerParams(
            dimension_semantics=("arbitrary",), vmem_limit_bytes=VMEM_LIMIT),
        name="adaln_mod",
    )(cT_pad, w_c, b_c.reshape(1, N), w_inT)
    return out[:B].reshape(B, N_MOD, D), w_all


_C_FQ, _C_FK, _C_FV = 0, 512, 1024
_C_GQ, _C_SM, _C_GK, _C_GV, _C_GR = 1536, 1792, 2048, 2304, 2816
_W_COLS = _C_GR + GLA_WIDTH
_SRC = [sum(IN_SPLITS[:i]) for i in range(len(IN_SPLITS))]
_SRC_FF, _SRC_GA = _SRC[3], _SRC[7]
_W_GROUPS = ((_C_FQ, _SRC[0], FOX_WIDTH), (_C_FK, _SRC[1], FOX_WIDTH), (_C_FV, _SRC[2], FOX_WIDTH),
             (_C_GQ, _SRC[4], GLA_KEY_WIDTH), (_C_GK, _SRC[5], GLA_KEY_WIDTH),
             (_C_GV, _SRC[6], GLA_WIDTH), (_C_GR, _SRC[8], GLA_WIDTH))
_KAUG_W = FOX_PAIRS * MXU_DIM


def _head_indicator():
    ind = np.zeros((FOX_WIDTH, MXU_DIM), np.float32)
    for h in range(FOX_HEADS):
        ind[h * FOX_HEAD_DIM:(h + 1) * FOX_HEAD_DIM, h] = 1.0
    return ind


def _inproj_kernel(x_ref, mod_ref, w_ref, bf_ref, ind_ref, gg_ref, wg32_ref, wu32_ref, wd32_ref,
                   fqT_ref, kaug_ref, fvT_ref, gq_ref, gk_ref, gv_ref, gr_ref, sm_ref,
                   drel_ref, stats_ref, wg16_ref, wu16_ref, wd16_ref, carry_ref):
    tm = x_ref.shape[1]

    wg16_ref[...] = wg32_ref[...].astype(BF16)
    wu16_ref[...] = wu32_ref[...].astype(BF16)
    wd16_ref[...] = wd32_ref[...].astype(BF16)

    @pl.when(pl.program_id(1) == 0)
    def _():
        carry_ref[...] = jnp.zeros_like(carry_ref)

    m = mod_ref[0]
    u = (x_ref[0] * (1.0 + m[1:2]) + m[0:1]).astype(BF16)

    def proj(lo, width):
        return _dot(u, w_ref[:, lo:lo + width])

    gq_small = proj(_C_GQ, 2 * GLA_KEY_WIDTH)
    gq_ref[0] = gq_small[:, :GLA_KEY_WIDTH] * (GLA_KEY_DIM ** -0.5)
    small = gq_small[:, GLA_KEY_WIDTH:GLA_KEY_WIDTH + SMALL_W]
    sm_ref[0] = small

    fq = proj(_C_FQ, FOX_WIDTH) * (FOX_HEAD_DIM ** -0.5 * LOG2E)
    fqT_ref[0] = fq.T.astype(BF16)
    fk32 = proj(_C_FK, FOX_WIDTH)
    fk = fk32.astype(BF16)
    qn2 = jnp.max(_dot((fq * fq).astype(BF16), ind_ref[...])[:, :SMALL_W], axis=0, keepdims=True)
    kn2 = jnp.max(_dot((fk32 * fk32).astype(BF16), ind_ref[...])[:, :SMALL_W], axis=0, keepdims=True)

    zT = (small + bf_ref[...]).T[:FOX_HEADS, :]
    lfT = _log_sigmoid(zT) * LOG2E
    r = lax.broadcasted_iota(jnp.int32, (tm, tm), 0)
    c = lax.broadcasted_iota(jnp.int32, (tm, tm), 1)
    triu = jnp.where(r <= c, 1.0, 0.0).astype(BF16)
    drelT = _tri_right(lfT, triu)
    drel_ref[0] = drelT

    c_before = carry_ref[...]
    srow = lax.broadcasted_iota(jnp.int32, (SUBLANES, SMALL_W), 0)
    slane = lax.broadcasted_iota(jnp.int32, (SUBLANES, SMALL_W), 1)
    total = jnp.sum(jnp.where(srow == slane, drelT[:, tm - 1:tm], 0.0), axis=0, keepdims=True)
    carry_ref[...] = c_before + total

    hi, mid, lo = _split3(drelT)
    ones3 = jnp.where(lax.broadcasted_iota(jnp.int32, (SUBLANES, tm), 0) < 3, 1.0, 0.0)
    slabT = jnp.concatenate(
        [-hi.astype(F32), -mid.astype(F32), -lo.astype(F32), ones3,
         jnp.zeros((LANES - 4 * SUBLANES, tm), F32)], axis=0)
    slab = slabT.T.astype(BF16)

    fvT_ref[0] = proj(_C_FV, FOX_WIDTH).T.astype(BF16)
    gv_ref[0] = proj(_C_GV, GLA_WIDTH).astype(BF16)
    gr = proj(_C_GR, GLA_WIDTH)
    gr_ref[0] = gg_ref[...] * (gr * (1.0 / (1.0 + jnp.exp(-gr))))
    gk_ref[0] = proj(_C_GK, GLA_KEY_WIDTH).T

    stats_ref[0, 0] = jnp.where(srow == 0, c_before,
                                jnp.where(srow == 1, qn2, jnp.where(srow == 2, kn2, 0.0)))
    for p in range(FOX_PAIRS):
        kaug_ref[0, :, p * MXU_DIM:p * MXU_DIM + LANES] = fk[:, p * LANES:(p + 1) * LANES]
        kaug_ref[0, :, p * MXU_DIM + LANES:(p + 1) * MXU_DIM] = slab


def _in_projection(x, mod, w_all, b_f_pad, g_gla, w_gate, w_up, w_down):
    B, S, D = x.shape
    tm = INPROJ_TM
    n_steps = B * (S // tm)
    steps_b = S // tm
    up_rows = D // n_steps
    down_rows = 2 * D_FF // n_steps
    assert D % n_steps == 0 and up_rows % 16 == 0 and (2 * D_FF) % n_steps == 0 and down_rows % 16 == 0
    up_spec = pl.BlockSpec((up_rows, D_FF), lambda b, s: (b * steps_b + s, 0))
    down_spec = pl.BlockSpec((down_rows, D), lambda b, s: ((b * steps_b + s) // 2, 0))
    ind = _head_indicator()
    tok = lambda w: pl.BlockSpec((1, tm, w), lambda b, s: (b, s, 0))
    tokT = lambda h: pl.BlockSpec((1, h, tm), lambda b, s: (b, 0, s))
    outs = pl.pallas_call(
        _inproj_kernel,
        grid=(B, S // tm),
        in_specs=[
            tok(D),
            pl.BlockSpec((1, N_MOD, D), lambda b, s: (b, 0, 0)),
            pl.BlockSpec((D, _W_COLS), lambda b, s: (0, 0), pipeline_mode=pl.Buffered(1)),
            pl.BlockSpec((1, SMALL_W), lambda b, s: (0, 0)),
            pl.BlockSpec(ind.shape, lambda b, s: (0, 0)),
            pl.BlockSpec((1, GLA_WIDTH), lambda b, s: (0, 0)),
            up_spec, up_spec, down_spec,
        ],
        out_specs=[
            tokT(FOX_WIDTH), tok(_KAUG_W), tokT(FOX_WIDTH),
            tok(GLA_KEY_WIDTH), tokT(GLA_KEY_WIDTH), tok(GLA_WIDTH), tok(GLA_WIDTH),
            tok(SMALL_W),
            tokT(FOX_HEADS),
            pl.BlockSpec((1, 1, SUBLANES, SMALL_W), lambda b, s: (b, s, 0, 0)),
            up_spec, up_spec, down_spec,
        ],
        out_shape=[
            jax.ShapeDtypeStruct((B, FOX_WIDTH, S), BF16),
            jax.ShapeDtypeStruct((B, S, _KAUG_W), BF16),
            jax.ShapeDtypeStruct((B, FOX_WIDTH, S), BF16),
            jax.ShapeDtypeStruct((B, S, GLA_KEY_WIDTH), F32),
            jax.ShapeDtypeStruct((B, GLA_KEY_WIDTH, S), F32),
            jax.ShapeDtypeStruct((B, S, GLA_WIDTH), BF16),
            jax.ShapeDtypeStruct((B, S, GLA_WIDTH), F32),
            jax.ShapeDtypeStruct((B, S, SMALL_W), F32),
            jax.ShapeDtypeStruct((B, FOX_HEADS, S), F32),
            jax.ShapeDtypeStruct((B, S // tm, SUBLANES, SMALL_W), F32),
            jax.ShapeDtypeStruct(w_gate.shape, BF16),
            jax.ShapeDtypeStruct(w_up.shape, BF16),
            jax.ShapeDtypeStruct(w_down.shape, BF16),
        ],
        scratch_shapes=[pltpu.VMEM((1, SMALL_W), F32)],
        compiler_params=pltpu.CompilerParams(
            dimension_semantics=("arbitrary", "arbitrary"), vmem_limit_bytes=VMEM_LIMIT),
        name="in_proj",
    )(x, mod, w_all, b_f_pad, jnp.asarray(ind, BF16), g_gla, w_gate, w_up, w_down)
    return outs


def _fox_kernel(ntiles_ref, nitems_ref, qT_ref, kaug_ref, vT_ref, drel_ref, crep_ref, o_ref,
                qaug_ref, s_ref, p_ref, mx_ref, acc_ref, m_ref, mnext_ref, shift_ref, alpha_ref):
    tq, tk = FOX_TQ, FOX_TK
    half = FOX_HEAD_DIM
    S = kaug_ref.shape[1]
    nq = S // tq
    nh = FOX_GROUP_HEADS
    n_blk = crep_ref.shape[2] // nh
    g = pl.program_id(0) * pl.num_programs(1) + pl.program_id(1)

    def build_qaug(qi):
        q0 = pl.multiple_of(qi * tq, tq)
        dr = drel_ref[0, 0, :, pl.ds(q0, tq)]
        row = lax.broadcasted_iota(jnp.int32, (2 * half, tq), 0)
        r32 = lax.broadcasted_iota(jnp.int32, (AUG_ROWS, tq), 0)
        for h in range(nh):
            buf = (qi % 2) * nh + h
            qT = qT_ref[0, (h // 2) * 2 * half:(h // 2 + 1) * 2 * half, pl.ds(q0, tq)]
            own = (row < half) if h % 2 == 0 else (row >= half)
            qaug_ref[buf, 0:2 * half, :] = jnp.where(own, qT, jnp.zeros_like(qT))
            hi, mid, lo = _split3(dr[h:h + 1])
            gh = nh * pl.program_id(1) + h
            pick = (r32 == gh) | (r32 == gh + SUBLANES) | (r32 == gh + 2 * SUBLANES)
            slab = jnp.where(r32 == 24, hi.astype(F32),
                             jnp.where(r32 == 25, mid.astype(F32),
                                       jnp.where(r32 == 26, lo.astype(F32),
                                                 jnp.where(pick, 1.0, 0.0))))
            qaug_ref[buf, 2 * half:2 * half + AUG_ROWS, :] = slab.astype(BF16)

    def reset_state():
        acc_ref[...] = jnp.zeros_like(acc_ref)

    sum_rows = jnp.where(lax.broadcasted_iota(jnp.int32, (FOX_SUM_ROWS, tk), 0) == 0, 1.0, 0.0).astype(BF16)

    per = tk // tq

    def col_max8(s):
        return jnp.max(s.reshape(s.shape[0] // SUBLANES, SUBLANES, tq), axis=0)

    def tile_of(qi, t):
        raw = (qi + 1) * tq - (t + 1) * tk
        return pl.multiple_of(jnp.maximum(raw, 0), tq), raw < 0

    def qk_dots(qi, k_off):
        kts = [kaug_ref[0, pl.ds(k_off, tk), p * MXU_DIM:(p + 1) * MXU_DIM] for p in range(nh // 2)]
        return [_dot(kts[h // 2], qaug_ref[(qi % 2) * nh + h]) for h in range(nh)]

    def store_scores(scores):
        for h in range(nh):
            s_ref[h] = scores[h]
            for g in range(per):
                mx_ref[per * h + g] = col_max8(scores[h][g * tq:(g + 1) * tq])

    def fill_groups(d):
        for g in range(1, per):
            @pl.when(d < g)
            def _():
                for h in range(nh):
                    s_ref[h, g * tq:(g + 1) * tq, :] = jnp.full((tq, tq), NEG_BIG, F32)
                    mx_ref[per * h + g] = jnp.full((SUBLANES, tq), NEG_BIG, F32)

    def mask_stage(d):
        d0 = pl.multiple_of(d * tq, tq)
        kr = lax.broadcasted_iota(jnp.int32, (tq, tq), 0)
        qc = lax.broadcasted_iota(jnp.int32, (tq, tq), 1)
        for h in range(nh):
            tri = jnp.where(kr <= qc, s_ref[h, pl.ds(d0, tq), :], NEG_BIG)
            s_ref[h, pl.ds(d0, tq), :] = tri
            mx_ref[per * h + d] = col_max8(tri)
        fill_groups(d)

    def prep_stage(qi, k_off, first):
        i_blk = qi * tq // INPROJ_TM
        for h in range(nh):
            c_i = crep_ref[0, 0, pl.ds(h * n_blk + i_blk, 1), :]
            big_c = [c_i - crep_ref[0, 0, pl.ds(h * n_blk + (k_off + g * tq) // INPROJ_TM, 1), :]
                     for g in range(per)]
            cand = jnp.max(mx_ref[per * h], axis=0, keepdims=True) + big_c[0]
            for g in range(1, per):
                cand = jnp.maximum(cand, jnp.max(mx_ref[per * h + g], axis=0, keepdims=True) + big_c[g])
            m_prev = jnp.where(first, NEG_BIG, m_ref[h])
            m_new = jnp.maximum(m_prev, cand)
            mnext_ref[h] = m_new
            for g in range(per):
                shift_ref[per * h + g] = m_new - big_c[g]
            alpha_ref[h] = jnp.exp2(m_prev - m_new)

    def exp_stage():
        for h in range(nh):
            for g in range(per):
                rows = slice(g * tq, (g + 1) * tq)
                p_ref[h, rows, :] = jnp.exp2(s_ref[h, rows, :] - shift_ref[per * h + g]).astype(BF16)

    def pv_stage(k_off):
        for h in range(nh):
            vT = jnp.concatenate([vT_ref[0, h * half:(h + 1) * half, pl.ds(k_off, tk)], sum_rows], axis=0)
            acc_ref[h] = alpha_ref[h] * acc_ref[h] + _dot(vT, p_ref[h])
            m_ref[h] = mnext_ref[h]

    def finalize(qi):
        outs = []
        for h in range(nh):
            acc = acc_ref[h]
            outs.append(acc[:half] / acc[half:half + 1])
        oT = jnp.concatenate(outs, axis=0)
        o_ref[0, pl.ds(pl.multiple_of(qi * tq, tq), tq), :] = oT.T.astype(o_ref.dtype)

    def fix_masks(t, clamped):
        @pl.when(t == 0)
        def _():
            mask_stage(jnp.where(clamped, 0, per - 1))

        @pl.when((t > 0) & clamped)
        def _():
            fill_groups(0)

    zero = jnp.int32(0)
    qaug_ref[...] = jnp.zeros_like(qaug_ref)
    build_qaug(zero)
    build_qaug(zero + 1)
    reset_state()
    k0, cl0 = tile_of(zero, zero)
    store_scores(qk_dots(zero, k0))
    fix_masks(zero, cl0)
    prep_stage(zero, k0, True)

    def body(it, carry):
        qi, t = carry
        last = t == 0
        qi2 = jnp.where(last, qi + 1, qi)
        t2 = jnp.where(last, ntiles_ref[g * nq + jnp.minimum(qi + 1, nq - 1)] - 1, t - 1)
        k_cur, _ = tile_of(qi, t)
        k_nxt, cl_nxt = tile_of(qi2, t2)
        exp_stage()
        nxt = qk_dots(qi2, k_nxt)
        pv_stage(k_cur)
        store_scores(nxt)
        prep_stage(qi2, k_nxt, last)

        @pl.when((t2 == 0) | cl_nxt)
        def _():
            fix_masks(t2, cl_nxt)
            prep_stage(qi2, k_nxt, last)

        @pl.when(last)
        def _():
            finalize(qi)
            reset_state()

            @pl.when(qi + 2 < nq)
            def _():
                build_qaug(qi + 2)

        return qi2, t2

    qi, t = lax.fori_loop(0, nitems_ref[g] - 1, body, (jnp.int32(0), jnp.int32(0)))
    exp_stage()
    pv_stage(tile_of(qi, t)[0])
    finalize(qi)


def _prune_plan(stats, drel, nq):
    B, n_blk = stats.shape[:2]
    per = FOX_TK // FOX_TQ
    n_t_max = nq // per + 1
    c = stats[:, :, 0, :FOX_HEADS]
    qn = jnp.sqrt(stats[:, :, 1, :FOX_HEADS])
    kn = jnp.sqrt(stats[:, :, 2, :FOX_HEADS])
    d_in = jnp.transpose(drel[:, :, FOX_TQ - 1::FOX_TQ], (0, 2, 1))
    d_end = jnp.repeat(c, per, axis=1) + d_in
    d_before = jnp.concatenate([jnp.zeros_like(d_end[:, :1]), d_end[:, :-1]], axis=1)
    qi = jnp.arange(nq)[:, None]
    t = jnp.arange(n_t_max)[None, :]
    valid = (t >= 1) & (qi - per * t >= 0)

    def at_tile(a, back):
        front = per * (n_t_max - 1) + back
        ap = jnp.pad(a, ((0, 0), (front, 0), (0, 0)))
        return jnp.stack([ap[:, per * (n_t_max - 1 - i):per * (n_t_max - 1 - i) + nq]
                          for i in range(n_t_max)], axis=2)

    kn_q = jnp.repeat(kn, per, axis=1)
    qn_q = jnp.repeat(qn, per, axis=1)
    kn_tile = jnp.maximum(at_tile(kn_q, 0), at_tile(kn_q, 1))
    bound = (d_before[:, :, None, :] - at_tile(d_end, 0)
             + NORM_SLACK * qn_q[:, :, None, :] * (kn_tile + kn_q[:, :, None, :]))
    needed = valid[None, :, :, None] & ~(bound <= PRUNE_LOG2)
    nh = FOX_GROUP_HEADS
    needed_g = needed[..., 0::nh]
    for h in range(1, nh):
        needed_g = needed_g | needed[..., h::nh]
    n_tiles = 1 + jnp.max(jnp.where(needed_g, t[None, :, :, None], 0), axis=2)
    n_tiles = jnp.transpose(n_tiles, (0, 2, 1))
    n_items = jnp.sum(n_tiles, axis=2)
    return n_tiles.reshape(-1).astype(jnp.int32), n_items.reshape(-1).astype(jnp.int32)


def _fox_attention(fqT, kaug, fvT, drel, stats):
    B, _, S = fqT.shape
    tq, tk = FOX_TQ, FOX_TK
    n_blk = S // INPROJ_TM
    nh = FOX_GROUP_HEADS
    n_groups = FOX_HEADS // nh
    drel4 = drel.reshape(B, n_groups, nh, S)
    c = stats[:, :, 0, :FOX_HEADS]
    c = jnp.transpose(c, (0, 2, 1)).reshape(B, n_groups, nh * n_blk, 1)
    crep = jnp.broadcast_to(c, (B, n_groups, nh * n_blk, tq))
    n_tiles, n_items = _prune_plan(stats, drel, S // tq)
    grid_spec = pltpu.PrefetchScalarGridSpec(
        num_scalar_prefetch=2,
        grid=(B, n_groups),
        in_specs=[
            pl.BlockSpec((1, nh * FOX_HEAD_DIM, S), lambda b, p, js, ni: (b, p, 0)),
            pl.BlockSpec((1, S, nh // 2 * MXU_DIM), lambda b, p, js, ni: (b, 0, p)),
            pl.BlockSpec((1, nh * FOX_HEAD_DIM, S), lambda b, p, js, ni: (b, p, 0)),
            pl.BlockSpec((1, 1, nh, S), lambda b, p, js, ni: (b, p, 0, 0)),
            pl.BlockSpec((1, 1, nh * n_blk, tq), lambda b, p, js, ni: (b, p, 0, 0)),
        ],
        out_specs=pl.BlockSpec((1, S, nh * FOX_HEAD_DIM), lambda b, p, js, ni: (b, 0, p)),
        scratch_shapes=[
            pltpu.VMEM((2 * nh, MXU_DIM, tq), BF16),
            pltpu.VMEM((nh, tk, tq), F32),
            pltpu.VMEM((nh, tk, tq), BF16),
            pltpu.VMEM((nh * (tk // tq), SUBLANES, tq), F32),
            pltpu.VMEM((nh, FOX_HEAD_DIM + FOX_SUM_ROWS, tq), F32),
            pltpu.VMEM((nh, 1, tq), F32),
            pltpu.VMEM((nh, 1, tq), F32),
            pltpu.VMEM((nh * (tk // tq), 1, tq), F32),
            pltpu.VMEM((nh, 1, tq), F32),
        ])
    return pl.pallas_call(
        _fox_kernel,
        grid_spec=grid_spec,
        out_shape=jax.ShapeDtypeStruct((B, S, FOX_WIDTH), BF16),
        compiler_params=pltpu.CompilerParams(
            dimension_semantics=("arbitrary", "arbitrary"),
            vmem_limit_bytes=VMEM_LIMIT),
        name="fox_attn",
    )(n_tiles, n_items, fqT, kaug, fvT, drel4, crep)


def _gla_kernel(q_ref, kT_ref, v_ref, gate_ref, sm_ref, wa_ref, ba_ref, o_ref, state_ref):
    nb, ts = q_ref.shape[0], q_ref.shape[1]
    C, P = GLA_CHUNK, GLA_PAIR
    dk, dv, H = GLA_KEY_DIM, GLA_VAL_DIM, GLA_HEADS
    KW = GLA_KEY_WIDTH

    @pl.when(pl.program_id(0) == 0)
    def _():
        state_ref[...] = jnp.zeros_like(state_ref)

    la_all = [_log_sigmoid(_dot(sm_ref[bb].astype(BF16), wa_ref[...]) + ba_ref[...])
              * (1.0 / GLA_GATE_TEMP) for bb in range(nb)]

    ri = lax.broadcasted_iota(jnp.int32, (P, P), 0)
    ci = lax.broadcasted_iota(jnp.int32, (P, P), 1)
    tril2 = jnp.where(((ri < C) == (ci < C)) & (ci <= ri), 1.0, 0.0).astype(BF16)

    rs = lax.broadcasted_iota(jnp.int32, (2 * H * C, P), 0)
    cs = lax.broadcasted_iota(jnp.int32, (2 * H * C, P), 1)
    causal = ((rs >= H * C) == (cs >= C)) & ((cs & (C - 1)) <= (rs & (C - 1)))

    lane_q = lax.broadcasted_iota(jnp.int32, (C, KW), 1)
    lane_t = lax.broadcasted_iota(jnp.int32, (KW, P), 1)
    first = lane_t < C

    per_b = ts // P
    n_slabs = nb * per_b
    slab_b = [k // per_b for k in range(n_slabs)]
    slab_rows = [slice((k % per_b) * P, (k % per_b + 1) * P) for k in range(n_slabs)]
    bs = [_tri_left(tril2, la_all[slab_b[k]][slab_rows[k]]) for k in range(n_slabs)]

    qms, scores, kv0s, kv1s, dec0s, dec1s, vss = [], [], [], [], [], [], []
    for pi in range(n_slabs):
        bb, rows = slab_b[pi], slab_rows[pi]
        b = bs[pi]
        bT = b.T
        bl0 = bT[:, C - 1:C]
        bl1 = bT[:, P - 1:P]
        q_dec = q_ref[bb, rows, :] * jnp.exp(b)
        kT = kT_ref[bb, :, rows]
        k_invT = (kT * jnp.exp(-bT)).astype(BF16)
        k_teT = kT * jnp.exp(jnp.where(first, bl0, bl1) - bT)
        zt = jnp.zeros_like(k_teT)
        k_te0 = jnp.where(first, k_teT, zt).astype(BF16)
        k_te1 = jnp.where(first, zt, k_teT).astype(BF16)
        dec0 = jnp.exp(bl0)
        dec1 = jnp.exp(bl1)

        vs = [v_ref[bb, rows, h * dv:(h + 1) * dv] for h in range(H)]
        kv = [_dot(jnp.concatenate([k_te0[h * dk:(h + 1) * dk, :], k_te1[h * dk:(h + 1) * dk, :]], axis=0),
                   vs[h]) for h in range(H)]
        kv0s.append(jnp.concatenate([kv[h][:dk] for h in range(H)], axis=0))
        kv1s.append(jnp.concatenate([kv[h][dk:] for h in range(H)], axis=0))
        dec0s.append(dec0)
        dec1s.append(dec1)
        vss.append(vs)

        zq = jnp.zeros((C, KW), F32)
        qm = [jnp.concatenate(
            [jnp.where((lane_q >= h * dk) & (lane_q < (h + 1) * dk), q_dec[c * C:(c + 1) * C], zq)
             for h in range(H)], axis=0).astype(BF16) for c in range(2)]
        a = _dot(jnp.concatenate(qm, axis=0), k_invT)
        scores.append(jnp.where(causal, a, 0.0).astype(BF16))
        qms.append(qm)

    o_inters = []
    for pi in range(n_slabs):
        if pi % per_b == 0:
            s = state_ref[slab_b[pi]]
        s1 = dec0s[pi] * s + kv0s[pi]
        o_inters.append([_dot(qms[pi][0], s.astype(BF16)), _dot(qms[pi][1], s1.astype(BF16))])
        s = dec1s[pi] * s1 + kv1s[pi]
        if pi % per_b == per_b - 1:
            state_ref[slab_b[pi]] = s

    for pi in range(n_slabs):
        bb, rows = slab_b[pi], slab_rows[pi]
        a, o_inter, vs = scores[pi], o_inters[pi], vss[pi]
        for h in range(H):
            hv = slice(h * dv, (h + 1) * dv)
            a_h = jnp.concatenate([a[(c * H + h) * C:(c * H + h + 1) * C] for c in range(2)], axis=0)
            o = _dot(a_h, vs[h]) + jnp.concatenate(
                [o_inter[c][h * C:(h + 1) * C] for c in range(2)], axis=0)
            o = o * lax.rsqrt(jnp.mean(o * o, axis=-1, keepdims=True) + RMS_EPS)
            o_ref[bb, rows, hv] = (o * gate_ref[bb, rows, hv]).astype(o_ref.dtype)


def _gla_attention(gq, gkT, gv, gate, small, w_a2_pad, b_a):
    B, S, _ = gq.shape
    ts = GLA_TS
    tok = lambda w: pl.BlockSpec((B, ts, w), lambda s: (0, s, 0))
    full = lambda r, c: pl.BlockSpec((r, c), lambda s: (0, 0))
    return pl.pallas_call(
        _gla_kernel,
        grid=(S // ts,),
        in_specs=[
            tok(GLA_KEY_WIDTH), pl.BlockSpec((B, GLA_KEY_WIDTH, ts), lambda s: (0, 0, s)),
            tok(GLA_WIDTH), tok(GLA_WIDTH), tok(SMALL_W),
            full(SMALL_W, GLA_KEY_WIDTH), full(1, GLA_KEY_WIDTH),
        ],
        out_specs=tok(GLA_WIDTH),
        out_shape=jax.ShapeDtypeStruct((B, S, GLA_WIDTH), BF16),
        scratch_shapes=[pltpu.VMEM((B, GLA_KEY_WIDTH, GLA_VAL_DIM), F32)],
        compiler_params=pltpu.CompilerParams(
            dimension_semantics=("arbitrary",), vmem_limit_bytes=VMEM_LIMIT),
        name="gla_attn",
    )(gq, gkT, gv, gate, small, w_a2_pad, b_a)


def _layer_norm(z, g, b):
    mu = jnp.mean(z, axis=-1, keepdims=True)
    d = z - mu
    var = jnp.mean(d * d, axis=-1, keepdims=True)
    return d * lax.rsqrt(var + LN_EPS) * g + b


def _tail_kernel(x_ref, fox_ref, gla_ref, mod_ref, wof_ref, wog_ref, ln1g_ref, ln1b_ref,
                 wg_ref, wu_ref, wd_ref, ln2g_ref, ln2b_ref, o_ref):
    m = mod_ref[0]
    tm = x_ref.shape[0]
    halves = [slice(0, tm // 2), slice(tm // 2, tm)]
    ys = [_dot(fox_ref[r, :], wof_ref[...]) + _dot(gla_ref[r, :], wog_ref[...]) for r in halves]
    x1s, gs, ups = [], [], []
    for r, y in zip(halves, ys):
        x1 = _layer_norm(DEEPNORM_ALPHA * x_ref[r, :] + (1.0 + m[2:3]) * y,
                         ln1g_ref[...], ln1b_ref[...])
        u2 = (x1 * (1.0 + m[4:5]) + m[3:4]).astype(BF16)
        x1s.append(x1)
        gs.append(_dot(u2, wg_ref[...]))
        ups.append(_dot(u2, wu_ref[...]))
    y2s = []
    for g, up in zip(gs, ups):
        h = (g * (1.0 / (1.0 + jnp.exp(-g))) * up).astype(BF16)
        y2s.append(_dot(h, wd_ref[...]))
    for r, x1, y2 in zip(halves, x1s, y2s):
        o_ref[r, :] = _layer_norm(DEEPNORM_ALPHA * x1 + (1.0 + m[5:6]) * y2,
                                  ln2g_ref[...], ln2b_ref[...])


def _tail(x2d, fox2d, gla2d, mod, wo, ln1g, ln1b, wg, wu, wd, ln2g, ln2b, steps_per_batch):
    T, D = x2d.shape
    tm = TAIL_TM
    const = lambda r, c: pl.BlockSpec((r, c), lambda i: (0, 0), pipeline_mode=pl.Buffered(1))
    tok = lambda w: pl.BlockSpec((tm, w), lambda i: (i, 0))
    return pl.pallas_call(
        _tail_kernel,
        grid=(T // tm,),
        in_specs=[
            tok(D), tok(FOX_WIDTH), tok(GLA_WIDTH),
            pl.BlockSpec((1, N_MOD, D), lambda i: (i // steps_per_batch, 0, 0)),
            pl.BlockSpec((FOX_WIDTH, D), lambda i: (0, 0), pipeline_mode=pl.Buffered(1)),
            pl.BlockSpec((GLA_WIDTH, D), lambda i: (1, 0), pipeline_mode=pl.Buffered(1)),
            const(1, D), const(1, D),
            const(D, D_FF), const(D, D_FF), const(D_FF, D), const(1, D), const(1, D),
        ],
        out_specs=tok(D),
        out_shape=jax.ShapeDtypeStruct((T, D), F32),
        compiler_params=pltpu.CompilerParams(
            dimension_semantics=("arbitrary",), vmem_limit_bytes=VMEM_LIMIT),
        name="tail",
    )(x2d, fox2d, gla2d, mod, wo, wo, ln1g, ln1b, wg, wu, wd, ln2g, ln2b)


def kernel(x, c, w_c, b_c, w_in, b_f, w_a2, b_a, g_gla, w_o, ln1_g, ln1_b,
           w_gate, w_up, w_down, ln2_g, ln2_b):
    B, S, D = x.shape
    assert D == D_MODEL and S % INPROJ_TM == 0 and INPROJ_TM == FOX_TK and FOX_TK % FOX_TQ == 0

    b_f_pad = jnp.zeros((1, SMALL_W), F32).at[0, :FOX_HEADS].set(b_f)
    w_a2_pad = jnp.zeros((SMALL_W, GLA_KEY_WIDTH), F32) \
        .at[FOX_HEADS:FOX_HEADS + GLA_GATE_RANK].set(w_a2).astype(BF16)

    mod, w_all = _modulation(c, w_c, b_c, w_in)

    fqT, kaug, fvT, gq, gkT, gv, gate, small, drel, stats, wg16, wu16, wd16 = _in_projection(
        x, mod, w_all, b_f_pad, g_gla.reshape(1, -1), w_gate, w_up, w_down)
    fox = _fox_attention(fqT, kaug, fvT, drel, stats)
    gla = _gla_attention(gq, gkT, gv, gate, small, w_a2_pad, b_a.reshape(1, -1))

    T = B * S
    w_o_b = w_o.astype(BF16)
    out = _tail(x.reshape(T, D), fox.reshape(T, FOX_WIDTH), gla.reshape(T, GLA_WIDTH), mod,
                w_o_b,
                ln1_g.reshape(1, D), ln1_b.reshape(1, D),
                wg16, wu16, wd16,
                ln2_g.reshape(1, D), ln2_b.reshape(1, D), S // TAIL_TM)
    return out.reshape(B, S, D)
```

```python
import functools
import math

import numpy as np
import jax
import jax.numpy as jnp
from jax import lax
from jax.experimental import pallas as pl
from jax.experimental.pallas import tpu as pltpu

F32 = jnp.float32
BF16 = jnp.bfloat16

D_MODEL = 1024
FOX_WIDTH = 512
FOX_HEAD_DIM = 64
FOX_HEADS = 8
FOX_PAIRS = FOX_HEADS // 2
GLA_WIDTH = 512
GLA_HEADS = 4
GLA_VAL_DIM = 128
GLA_KEY_DIM = 64
GLA_KEY_WIDTH = 256
GLA_GATE_RANK = 16
GLA_GATE_TEMP = 16.0
GLA_CHUNK = 64
D_FF = 2816
LN_EPS = 1e-5
RMS_EPS = 1e-6
N_MOD = 6
DEEPNORM_ALPHA = 2.0 ** 0.25
IN_SPLITS = (FOX_WIDTH, FOX_WIDTH, FOX_WIDTH, FOX_HEADS, GLA_KEY_WIDTH, GLA_KEY_WIDTH,
             GLA_WIDTH, GLA_GATE_RANK, GLA_WIDTH)
LOG2E = math.log2(math.e)

LANES = 128
SUBLANES = 8
MXU_DIM = 256
SMALL_W = LANES
VMEM_LIMIT = 56 * 1024 * 1024

MOD_TN = 768
INPROJ_TM = 512
FOX_TQ = 256
FOX_GROUP_HEADS = 4
FOX_SUM_ROWS = 16
FOX_TK = 512
GLA_TS = 512
GLA_PAIR = 2 * GLA_CHUNK
TAIL_TM = 512

NEG_BIG = -1e30
PRUNE_LOG2 = -1080.0
NORM_SLACK = 1.02

AUG_ROWS = 32


def _log_sigmoid(z):
    return jnp.minimum(z, 0.0) - jnp.log(1.0 + jnp.exp(-jnp.abs(z)))


def _split3(a):
    hi = a.astype(BF16)
    r1 = a - hi.astype(F32)
    mid = r1.astype(BF16)
    lo = (r1 - mid.astype(F32)).astype(BF16)
    return hi, mid, lo


def _dot(a, b):
    return jnp.dot(a, b, preferred_element_type=F32)


def _tri_left(tri, a):
    hi, mid, lo = _split3(a)
    return _dot(tri, hi) + _dot(tri, mid) + _dot(tri, lo)


def _tri_right(a, tri):
    hi, mid, lo = _split3(a)
    return _dot(hi, tri) + _dot(mid, tri) + _dot(lo, tri)


def _regroup_blocks():
    blocks = []
    for dst, src, width in _W_GROUPS:
        blocks += [(dst + c0, src + c0) for c0 in range(0, width, MXU_DIM)]
    blocks.append((_C_SM, None))
    return blocks


def _mod_kernel(cT_ref, w_ref, b_ref, winT_ref, o_ref, wall_ref, *, n_rows):
    w = w_ref[...]
    outs = [jnp.sum(w * cT_ref[:, b:b + 1], axis=0, keepdims=True) for b in range(n_rows)]
    outs.append(jnp.zeros((o_ref.shape[0] - n_rows, w.shape[1]), F32))
    o_ref[...] = jnp.concatenate(outs, axis=0) + b_ref[...]

    j = pl.program_id(0)
    n_steps = pl.num_programs(0)
    for k, (dst, src) in enumerate(_regroup_blocks()):
        @pl.when(j == k % n_steps)
        def _():
            if src is None:
                blk = jnp.concatenate(
                    [winT_ref[_SRC_FF:_SRC_FF + FOX_HEADS, :], winT_ref[_SRC_GA:_SRC_GA + GLA_GATE_RANK, :],
                     jnp.zeros((MXU_DIM - FOX_HEADS - GLA_GATE_RANK, winT_ref.shape[1]), F32)], axis=0)
            else:
                blk = winT_ref[src:src + MXU_DIM, :]
            wall_ref[:, dst:dst + MXU_DIM] = blk.T.astype(BF16)


def _modulation(c, w_c, b_c, w_in):
    B, D = c.shape
    N = w_c.shape[1]
    rows = SUBLANES
    n_steps = N // MOD_TN
    cT_pad = jnp.pad(c.T, ((0, 0), (0, LANES - B)))
    w_inT = w_in.T
    out, w_all = pl.pallas_call(
        functools.partial(_mod_kernel, n_rows=B),
        grid=(n_steps,),
        in_specs=[
            pl.BlockSpec((D, LANES), lambda j: (0, 0)),
            pl.BlockSpec((D, MOD_TN), lambda j: (0, j)),
            pl.BlockSpec((1, MOD_TN), lambda j: (0, j)),
            pl.BlockSpec(w_inT.shape, lambda j: (0, 0), pipeline_mode=pl.Buffered(1)),
        ],
        out_specs=[pl.BlockSpec((rows, MOD_TN), lambda j: (0, j)),
                   pl.BlockSpec((D, _W_COLS), lambda j: (0, 0))],
        out_shape=[jax.ShapeDtypeStruct((rows, N), F32),
                   jax.ShapeDtypeStruct((D, _W_COLS), BF16)],
        compiler_params=pltpu.CompilerParams(
            dimension_semantics=("arbitrary",), vmem_limit_bytes=VMEM_LIMIT),
        name="adaln_mod",
    )(cT_pad, w_c, b_c.reshape(1, N), w_inT)
    return out[:B].reshape(B, N_MOD, D), w_all


_C_FQ, _C_FK, _C_FV = 0, 512, 1024
_C_GQ, _C_SM, _C_GK, _C_GV, _C_GR = 1536, 1792, 2048, 2304, 2816
_W_COLS = _C_GR + GLA_WIDTH
_SRC = [sum(IN_SPLITS[:i]) for i in range(len(IN_SPLITS))]
_SRC_FF, _SRC_GA = _SRC[3], _SRC[7]
_W_GROUPS = ((_C_FQ, _SRC[0], FOX_WIDTH), (_C_FK, _SRC[1], FOX_WIDTH), (_C_FV, _SRC[2], FOX_WIDTH),
             (_C_GQ, _SRC[4], GLA_KEY_WIDTH), (_C_GK, _SRC[5], GLA_KEY_WIDTH),
             (_C_GV, _SRC[6], GLA_WIDTH), (_C_GR, _SRC[8], GLA_WIDTH))
_KAUG_W = FOX_PAIRS * MXU_DIM


def _head_indicator():
    ind = np.zeros((FOX_WIDTH, MXU_DIM), np.float32)
    for h in range(FOX_HEADS):
        ind[h * FOX_HEAD_DIM:(h + 1) * FOX_HEAD_DIM, h] = 1.0
    return ind


def _inproj_kernel(x_ref, mod_ref, w_ref, bf_ref, ind_ref, gg_ref, wg32_ref, wu32_ref, wd32_ref, wo32_ref,
                   fqT_ref, kaug_ref, fvT_ref, gq_ref, gk_ref, gv_ref, gr_ref, sm_ref,
                   drel_ref, stats_ref, wg16_ref, wu16_ref, wd16_ref, wo16_ref, carry_ref):
    tm = x_ref.shape[1]

    wg16_ref[...] = wg32_ref[...].astype(BF16)
    wu16_ref[...] = wu32_ref[...].astype(BF16)
    wd16_ref[...] = wd32_ref[...].astype(BF16)
    wo16_ref[...] = wo32_ref[...].astype(BF16)

    @pl.when(pl.program_id(1) == 0)
    def _():
        carry_ref[...] = jnp.zeros_like(carry_ref)

    m = mod_ref[0]
    u = (x_ref[0] * (1.0 + m[1:2]) + m[0:1]).astype(BF16)

    def proj(lo, width):
        return _dot(u, w_ref[:, lo:lo + width])

    gq_small = proj(_C_GQ, 2 * GLA_KEY_WIDTH)
    gq_ref[0] = gq_small[:, :GLA_KEY_WIDTH] * (GLA_KEY_DIM ** -0.5)
    small = gq_small[:, GLA_KEY_WIDTH:GLA_KEY_WIDTH + SMALL_W]
    sm_ref[0] = small

    fq = proj(_C_FQ, FOX_WIDTH) * (FOX_HEAD_DIM ** -0.5 * LOG2E)
    fqT_ref[0] = fq.T.astype(BF16)
    fk32 = proj(_C_FK, FOX_WIDTH)
    fk = fk32.astype(BF16)
    qn2 = jnp.max(_dot((fq * fq).astype(BF16), ind_ref[...])[:, :SMALL_W], axis=0, keepdims=True)
    kn2 = jnp.max(_dot((fk32 * fk32).astype(BF16), ind_ref[...])[:, :SMALL_W], axis=0, keepdims=True)

    zT = (small + bf_ref[...]).T[:FOX_HEADS, :]
    lfT = _log_sigmoid(zT) * LOG2E
    r = lax.broadcasted_iota(jnp.int32, (tm, tm), 0)
    c = lax.broadcasted_iota(jnp.int32, (tm, tm), 1)
    triu = jnp.where(r <= c, 1.0, 0.0).astype(BF16)
    drelT = _tri_right(lfT, triu)
    drel_ref[0] = drelT

    c_before = carry_ref[...]
    srow = lax.broadcasted_iota(jnp.int32, (SUBLANES, SMALL_W), 0)
    slane = lax.broadcasted_iota(jnp.int32, (SUBLANES, SMALL_W), 1)
    total = jnp.sum(jnp.where(srow == slane, drelT[:, tm - 1:tm], 0.0), axis=0, keepdims=True)
    carry_ref[...] = c_before + total

    hi, mid, lo = _split3(drelT)
    ones3 = jnp.where(lax.broadcasted_iota(jnp.int32, (SUBLANES, tm), 0) < 3, 1.0, 0.0)
    slabT = jnp.concatenate(
        [-hi.astype(F32), -mid.astype(F32), -lo.astype(F32), ones3,
         jnp.zeros((LANES - 4 * SUBLANES, tm), F32)], axis=0)
    slab = slabT.T.astype(BF16)

    fvT_ref[0] = proj(_C_FV, FOX_WIDTH).T.astype(BF16)
    gv_ref[0] = proj(_C_GV, GLA_WIDTH).astype(BF16)
    gr = proj(_C_GR, GLA_WIDTH)
    gr_ref[0] = gg_ref[...] * (gr * (1.0 / (1.0 + jnp.exp(-gr))))
    gk_ref[0] = proj(_C_GK, GLA_KEY_WIDTH).T

    stats_ref[0, 0] = jnp.where(srow == 0, c_before,
                                jnp.where(srow == 1, qn2, jnp.where(srow == 2, kn2, 0.0)))
    for p in range(FOX_PAIRS):
        kaug_ref[0, :, p * MXU_DIM:p * MXU_DIM + LANES] = fk[:, p * LANES:(p + 1) * LANES]
        kaug_ref[0, :, p * MXU_DIM + LANES:(p + 1) * MXU_DIM] = slab


def _in_projection(x, mod, w_all, b_f_pad, g_gla, w_gate, w_up, w_down, w_o):
    B, S, D = x.shape
    tm = INPROJ_TM
    n_steps = B * (S // tm)
    steps_b = S // tm
    up_rows = D // n_steps
    down_rows = 2 * D_FF // n_steps
    assert D % n_steps == 0 and up_rows % 16 == 0 and (2 * D_FF) % n_steps == 0 and down_rows % 16 == 0
    up_spec = pl.BlockSpec((up_rows, D_FF), lambda b, s: (b * steps_b + s, 0))
    down_spec = pl.BlockSpec((down_rows, D), lambda b, s: ((b * steps_b + s) // 2, 0))
    wo_rows = w_o.shape[0] // n_steps
    assert w_o.shape[0] % n_steps == 0 and wo_rows % 16 == 0
    wo_spec = pl.BlockSpec((wo_rows, D), lambda b, s: (b * steps_b + s, 0))
    ind = _head_indicator()
    tok = lambda w: pl.BlockSpec((1, tm, w), lambda b, s: (b, s, 0))
    tokT = lambda h: pl.BlockSpec((1, h, tm), lambda b, s: (b, 0, s))
    outs = pl.pallas_call(
        _inproj_kernel,
        grid=(B, S // tm),
        in_specs=[
            tok(D),
            pl.BlockSpec((1, N_MOD, D), lambda b, s: (b, 0, 0)),
            pl.BlockSpec((D, _W_COLS), lambda b, s: (0, 0), pipeline_mode=pl.Buffered(1)),
            pl.BlockSpec((1, SMALL_W), lambda b, s: (0, 0)),
            pl.BlockSpec(ind.shape, lambda b, s: (0, 0)),
            pl.BlockSpec((1, GLA_WIDTH), lambda b, s: (0, 0)),
            up_spec, up_spec, down_spec, wo_spec,
        ],
        out_specs=[
            tokT(FOX_WIDTH), tok(_KAUG_W), tokT(FOX_WIDTH),
            tok(GLA_KEY_WIDTH), tokT(GLA_KEY_WIDTH), tok(GLA_WIDTH), tok(GLA_WIDTH),
            tok(SMALL_W),
            tokT(FOX_HEADS),
            pl.BlockSpec((1, 1, SUBLANES, SMALL_W), lambda b, s: (b, s, 0, 0)),
            up_spec, up_spec, down_spec, wo_spec,
        ],
        out_shape=[
            jax.ShapeDtypeStruct((B, FOX_WIDTH, S), BF16),
            jax.ShapeDtypeStruct((B, S, _KAUG_W), BF16),
            jax.ShapeDtypeStruct((B, FOX_WIDTH, S), BF16),
            jax.ShapeDtypeStruct((B, S, GLA_KEY_WIDTH), F32),
            jax.ShapeDtypeStruct((B, GLA_KEY_WIDTH, S), F32),
            jax.ShapeDtypeStruct((B, S, GLA_WIDTH), BF16),
            jax.ShapeDtypeStruct((B, S, GLA_WIDTH), F32),
            jax.ShapeDtypeStruct((B, S, SMALL_W), F32),
            jax.ShapeDtypeStruct((B, FOX_HEADS, S), F32),
            jax.ShapeDtypeStruct((B, S // tm, SUBLANES, SMALL_W), F32),
            jax.ShapeDtypeStruct(w_gate.shape, BF16),
            jax.ShapeDtypeStruct(w_up.shape, BF16),
            jax.ShapeDtypeStruct(w_down.shape, BF16),
            jax.ShapeDtypeStruct(w_o.shape, BF16),
        ],
        scratch_shapes=[pltpu.VMEM((1, SMALL_W), F32)],
        compiler_params=pltpu.CompilerParams(
            dimension_semantics=("arbitrary", "arbitrary"), vmem_limit_bytes=VMEM_LIMIT),
        name="in_proj",
    )(x, mod, w_all, b_f_pad, jnp.asarray(ind, BF16), g_gla, w_gate, w_up, w_down, w_o)
    return outs


def _fox_kernel(ntiles_ref, nitems_ref, qT_ref, kaug_ref, vT_ref, drel_ref, crep_ref, o_ref,
                qaug_ref, s_ref, p_ref, mx_ref, acc_ref, m_ref, mnext_ref, shift_ref, alpha_ref):
    tq, tk = FOX_TQ, FOX_TK
    half = FOX_HEAD_DIM
    S = kaug_ref.shape[1]
    nq = S // tq
    nh = FOX_GROUP_HEADS
    n_blk = crep_ref.shape[2] // nh
    g = pl.program_id(0) * pl.num_programs(1) + pl.program_id(1)

    def build_qaug(qi):
        q0 = pl.multiple_of(qi * tq, tq)
        dr_all = drel_ref[0, :, pl.ds(q0, tq)]
        dr = dr_all[0:nh]
        for grp in range(1, dr_all.shape[0] // nh):
            dr = jnp.where(pl.program_id(1) == grp, dr_all[grp * nh:(grp + 1) * nh], dr)
        row = lax.broadcasted_iota(jnp.int32, (2 * half, tq), 0)
        r32 = lax.broadcasted_iota(jnp.int32, (AUG_ROWS, tq), 0)
        for h in range(nh):
            buf = (qi % 2) * nh + h
            qT = qT_ref[0, (h // 2) * 2 * half:(h // 2 + 1) * 2 * half, pl.ds(q0, tq)]
            own = (row < half) if h % 2 == 0 else (row >= half)
            qaug_ref[buf, 0:2 * half, :] = jnp.where(own, qT, jnp.zeros_like(qT))
            hi, mid, lo = _split3(dr[h:h + 1])
            gh = nh * pl.program_id(1) + h
            pick = (r32 == gh) | (r32 == gh + SUBLANES) | (r32 == gh + 2 * SUBLANES)
            slab = jnp.where(r32 == 24, hi.astype(F32),
                             jnp.where(r32 == 25, mid.astype(F32),
                                       jnp.where(r32 == 26, lo.astype(F32),
                                                 jnp.where(pick, 1.0, 0.0))))
            qaug_ref[buf, 2 * half:2 * half + AUG_ROWS, :] = slab.astype(BF16)

    def reset_state():
        acc_ref[...] = jnp.zeros_like(acc_ref)

    sum_rows = jnp.where(lax.broadcasted_iota(jnp.int32, (FOX_SUM_ROWS, tk), 0) == 0, 1.0, 0.0).astype(BF16)

    per = tk // tq

    def col_max8(s):
        return jnp.max(s.reshape(s.shape[0] // SUBLANES, SUBLANES, tq), axis=0)

    def tile_of(qi, t):
        raw = (qi + 1) * tq - (t + 1) * tk
        return pl.multiple_of(jnp.maximum(raw, 0), tq), raw < 0

    def qk_dots(qi, k_off):
        kts = [kaug_ref[0, pl.ds(k_off, tk), p * MXU_DIM:(p + 1) * MXU_DIM] for p in range(nh // 2)]
        return [_dot(kts[h // 2], qaug_ref[(qi % 2) * nh + h]) for h in range(nh)]

    def store_scores(scores):
        for h in range(nh):
            s_ref[h] = scores[h]
            for g in range(per):
                mx_ref[per * h + g] = col_max8(scores[h][g * tq:(g + 1) * tq])

    def fill_groups(d):
        for g in range(1, per):
            @pl.when(d < g)
            def _():
                for h in range(nh):
                    s_ref[h, g * tq:(g + 1) * tq, :] = jnp.full((tq, tq), NEG_BIG, F32)
                    mx_ref[per * h + g] = jnp.full((SUBLANES, tq), NEG_BIG, F32)

    def mask_stage(d):
        d0 = pl.multiple_of(d * tq, tq)
        kr = lax.broadcasted_iota(jnp.int32, (tq, tq), 0)
        qc = lax.broadcasted_iota(jnp.int32, (tq, tq), 1)
        for h in range(nh):
            tri = jnp.where(kr <= qc, s_ref[h, pl.ds(d0, tq), :], NEG_BIG)
            s_ref[h, pl.ds(d0, tq), :] = tri
            mx_ref[per * h + d] = col_max8(tri)
        fill_groups(d)

    def prep_stage(qi, k_off, first):
        i_blk = qi * tq // INPROJ_TM
        for h in range(nh):
            c_i = crep_ref[0, 0, pl.ds(h * n_blk + i_blk, 1), :]
            big_c = [c_i - crep_ref[0, 0, pl.ds(h * n_blk + (k_off + g * tq) // INPROJ_TM, 1), :]
                     for g in range(per)]
            cand = jnp.max(mx_ref[per * h], axis=0, keepdims=True) + big_c[0]
            for g in range(1, per):
                cand = jnp.maximum(cand, jnp.max(mx_ref[per * h + g], axis=0, keepdims=True) + big_c[g])
            m_prev = jnp.where(first, NEG_BIG, m_ref[h])
            m_new = jnp.maximum(m_prev, cand)
            mnext_ref[h] = m_new
            for g in range(per):
                shift_ref[per * h + g] = m_new - big_c[g]
            alpha_ref[h] = jnp.exp2(m_prev - m_new)

    def exp_stage():
        for h in range(nh):
            for g in range(per):
                rows = slice(g * tq, (g + 1) * tq)
                p_ref[h, rows, :] = jnp.exp2(s_ref[h, rows, :] - shift_ref[per * h + g]).astype(BF16)

    def pv_stage(k_off):
        for h in range(nh):
            vT = jnp.concatenate([vT_ref[0, h * half:(h + 1) * half, pl.ds(k_off, tk)], sum_rows], axis=0)
            acc_ref[h] = alpha_ref[h] * acc_ref[h] + _dot(vT, p_ref[h])
            m_ref[h] = mnext_ref[h]

    def finalize(qi):
        outs = []
        for h in range(nh):
            acc = acc_ref[h]
            outs.append(acc[:half] / acc[half:half + 1])
        oT = jnp.concatenate(outs, axis=0)
        o_ref[0, pl.ds(pl.multiple_of(qi * tq, tq), tq), :] = oT.T.astype(o_ref.dtype)

    def fix_masks(t, clamped):
        @pl.when(t == 0)
        def _():
            mask_stage(jnp.where(clamped, 0, per - 1))

        @pl.when((t > 0) & clamped)
        def _():
            fill_groups(0)

    zero = jnp.int32(0)
    qaug_ref[...] = jnp.zeros_like(qaug_ref)
    build_qaug(zero)
    build_qaug(zero + 1)
    reset_state()
    k0, cl0 = tile_of(zero, zero)
    store_scores(qk_dots(zero, k0))
    fix_masks(zero, cl0)
    prep_stage(zero, k0, True)

    def body(it, carry):
        qi, t = carry
        last = t == 0
        qi2 = jnp.where(last, qi + 1, qi)
        t2 = jnp.where(last, ntiles_ref[g * nq + jnp.minimum(qi + 1, nq - 1)] - 1, t - 1)
        k_cur, _ = tile_of(qi, t)
        k_nxt, cl_nxt = tile_of(qi2, t2)
        exp_stage()
        nxt = qk_dots(qi2, k_nxt)
        pv_stage(k_cur)
        store_scores(nxt)
        prep_stage(qi2, k_nxt, last)

        @pl.when((t2 == 0) | cl_nxt)
        def _():
            fix_masks(t2, cl_nxt)
            prep_stage(qi2, k_nxt, last)

        @pl.when(last)
        def _():
            finalize(qi)
            reset_state()

            @pl.when(qi + 2 < nq)
            def _():
                build_qaug(qi + 2)

        return qi2, t2

    qi, t = lax.fori_loop(0, nitems_ref[g] - 1, body, (jnp.int32(0), jnp.int32(0)))
    exp_stage()
    pv_stage(tile_of(qi, t)[0])
    finalize(qi)


def _prune_plan(stats, drel, nq):
    B, n_blk = stats.shape[:2]
    per = FOX_TK // FOX_TQ
    c = stats[:, :, 0, :FOX_HEADS]
    qn = jnp.sqrt(stats[:, :, 1, :FOX_HEADS])
    kn = jnp.sqrt(stats[:, :, 2, :FOX_HEADS])
    d_in = jnp.transpose(drel[:, :, FOX_TQ - 1::FOX_TQ], (0, 2, 1))
    d_end = jnp.repeat(c, per, axis=1) + d_in
    d_before = jnp.concatenate([jnp.zeros_like(d_end[:, :1]), d_end[:, :-1]], axis=1)
    kn_q = jnp.repeat(kn, per, axis=1)
    qn_q = jnp.repeat(qn, per, axis=1)
    kn_prev = jnp.concatenate([jnp.zeros_like(kn_q[:, :1]), kn_q[:, :-1]], axis=1)
    kn_tile = jnp.maximum(kn_q, kn_prev)
    bound = (d_before[:, :, None, :] - d_end[:, None, :, :]
             + NORM_SLACK * qn_q[:, :, None, :] * (kn_tile[:, None, :, :] + kn_q[:, :, None, :]))
    back = np.arange(nq)[:, None] - np.arange(nq)[None, :]
    is_tile = (back > 0) & (back % per == 0)
    t_of = np.where(is_tile, back // per, 0).astype(np.int32)
    needed = is_tile[None, :, :, None] & ~(bound <= PRUNE_LOG2)
    nh = FOX_GROUP_HEADS
    needed_g = jnp.any(needed.reshape(B, nq, nq, FOX_HEADS // nh, nh), axis=-1)
    n_tiles = 1 + jnp.max(jnp.where(needed_g, t_of[None, :, :, None], 0), axis=2)
    n_tiles = jnp.transpose(n_tiles, (0, 2, 1))
    n_items = jnp.sum(n_tiles, axis=2)
    return n_tiles.reshape(-1).astype(jnp.int32), n_items.reshape(-1).astype(jnp.int32)


def _fox_attention(fqT, kaug, fvT, drel, stats):
    B, _, S = fqT.shape
    tq, tk = FOX_TQ, FOX_TK
    n_blk = S // INPROJ_TM
    nh = FOX_GROUP_HEADS
    n_groups = FOX_HEADS // nh
    c = stats[:, :, 0, :FOX_HEADS]
    c = jnp.transpose(c, (0, 2, 1)).reshape(B, n_groups, nh * n_blk, 1)
    crep = jnp.broadcast_to(c, (B, n_groups, nh * n_blk, tq))
    n_tiles, n_items = _prune_plan(stats, drel, S // tq)
    grid_spec = pltpu.PrefetchScalarGridSpec(
        num_scalar_prefetch=2,
        grid=(B, n_groups),
        in_specs=[
            pl.BlockSpec((1, nh * FOX_HEAD_DIM, S), lambda b, p, js, ni: (b, p, 0)),
            pl.BlockSpec((1, S, nh // 2 * MXU_DIM), lambda b, p, js, ni: (b, 0, p)),
            pl.BlockSpec((1, nh * FOX_HEAD_DIM, S), lambda b, p, js, ni: (b, p, 0)),
            pl.BlockSpec((1, FOX_HEADS, S), lambda b, p, js, ni: (b, 0, 0)),
            pl.BlockSpec((1, 1, nh * n_blk, tq), lambda b, p, js, ni: (b, p, 0, 0)),
        ],
        out_specs=pl.BlockSpec((1, S, nh * FOX_HEAD_DIM), lambda b, p, js, ni: (b, 0, p)),
        scratch_shapes=[
            pltpu.VMEM((2 * nh, MXU_DIM, tq), BF16),
            pltpu.VMEM((nh, tk, tq), F32),
            pltpu.VMEM((nh, tk, tq), BF16),
            pltpu.VMEM((nh * (tk // tq), SUBLANES, tq), F32),
            pltpu.VMEM((nh, FOX_HEAD_DIM + FOX_SUM_ROWS, tq), F32),
            pltpu.VMEM((nh, 1, tq), F32),
            pltpu.VMEM((nh, 1, tq), F32),
            pltpu.VMEM((nh * (tk // tq), 1, tq), F32),
            pltpu.VMEM((nh, 1, tq), F32),
        ])
    return pl.pallas_call(
        _fox_kernel,
        grid_spec=grid_spec,
        out_shape=jax.ShapeDtypeStruct((B, S, FOX_WIDTH), BF16),
        compiler_params=pltpu.CompilerParams(
            dimension_semantics=("arbitrary", "arbitrary"),
            vmem_limit_bytes=VMEM_LIMIT),
        name="fox_attn",
    )(n_tiles, n_items, fqT, kaug, fvT, drel, crep)


def _gla_kernel(q_ref, kT_ref, v_ref, gate_ref, sm_ref, wa_ref, ba_ref, o_ref, state_ref):
    nb, ts = q_ref.shape[0], q_ref.shape[1]
    C, P = GLA_CHUNK, GLA_PAIR
    dk, dv, H = GLA_KEY_DIM, GLA_VAL_DIM, GLA_HEADS
    KW = GLA_KEY_WIDTH

    @pl.when(pl.program_id(0) == 0)
    def _():
        state_ref[...] = jnp.zeros_like(state_ref)

    la_all = [_log_sigmoid(_dot(sm_ref[bb].astype(BF16), wa_ref[...]) + ba_ref[...])
              * (1.0 / GLA_GATE_TEMP) for bb in range(nb)]

    ri = lax.broadcasted_iota(jnp.int32, (P, P), 0)
    ci = lax.broadcasted_iota(jnp.int32, (P, P), 1)
    tril2 = jnp.where(((ri < C) == (ci < C)) & (ci <= ri), 1.0, 0.0).astype(BF16)

    rs = lax.broadcasted_iota(jnp.int32, (2 * H * C, P), 0)
    cs = lax.broadcasted_iota(jnp.int32, (2 * H * C, P), 1)
    causal = ((rs >= H * C) == (cs >= C)) & ((cs & (C - 1)) <= (rs & (C - 1)))

    lane_q = lax.broadcasted_iota(jnp.int32, (C, KW), 1)
    lane_t = lax.broadcasted_iota(jnp.int32, (KW, P), 1)
    first = lane_t < C

    per_b = ts // P
    n_slabs = nb * per_b
    slab_b = [k // per_b for k in range(n_slabs)]
    slab_rows = [slice((k % per_b) * P, (k % per_b + 1) * P) for k in range(n_slabs)]
    bs = [_tri_left(tril2, la_all[slab_b[k]][slab_rows[k]]) for k in range(n_slabs)]

    qms, scores, kv0s, kv1s, dec0s, dec1s, vss = [], [], [], [], [], [], []
    for pi in range(n_slabs):
        bb, rows = slab_b[pi], slab_rows[pi]
        b = bs[pi]
        bT = b.T
        bl0 = bT[:, C - 1:C]
        bl1 = bT[:, P - 1:P]
        q_dec = q_ref[bb, rows, :] * jnp.exp(b)
        kT = kT_ref[bb, :, rows]
        k_invT = (kT * jnp.exp(-bT)).astype(BF16)
        k_teT = kT * jnp.exp(jnp.where(first, bl0, bl1) - bT)
        zt = jnp.zeros_like(k_teT)
        k_te0 = jnp.where(first, k_teT, zt).astype(BF16)
        k_te1 = jnp.where(first, zt, k_teT).astype(BF16)
        dec0 = jnp.exp(bl0)
        dec1 = jnp.exp(bl1)

        vs = [v_ref[bb, rows, h * dv:(h + 1) * dv] for h in range(H)]
        kv = [_dot(jnp.concatenate([k_te0[h * dk:(h + 1) * dk, :], k_te1[h * dk:(h + 1) * dk, :]], axis=0),
                   vs[h]) for h in range(H)]
        kv0s.append(jnp.concatenate([kv[h][:dk] for h in range(H)], axis=0))
        kv1s.append(jnp.concatenate([kv[h][dk:] for h in range(H)], axis=0))
        dec0s.append(dec0)
        dec1s.append(dec1)
        vss.append(vs)

        zq = jnp.zeros((C, KW), F32)
        qm = [jnp.concatenate(
            [jnp.where((lane_q >= h * dk) & (lane_q < (h + 1) * dk), q_dec[c * C:(c + 1) * C], zq)
             for h in range(H)], axis=0).astype(BF16) for c in range(2)]
        a = _dot(jnp.concatenate(qm, axis=0), k_invT)
        scores.append(jnp.where(causal, a, 0.0).astype(BF16))
        qms.append(qm)

    o_inters = []
    for pi in range(n_slabs):
        if pi % per_b == 0:
            s = state_ref[slab_b[pi]]
        s1 = dec0s[pi] * s + kv0s[pi]
        o_inters.append([_dot(qms[pi][0], s.astype(BF16)), _dot(qms[pi][1], s1.astype(BF16))])
        s = dec1s[pi] * s1 + kv1s[pi]
        if pi % per_b == per_b - 1:
            state_ref[slab_b[pi]] = s

    for pi in range(n_slabs):
        bb, rows = slab_b[pi], slab_rows[pi]
        a, o_inter, vs = scores[pi], o_inters[pi], vss[pi]
        for h in range(H):
            hv = slice(h * dv, (h + 1) * dv)
            a_h = jnp.concatenate([a[(c * H + h) * C:(c * H + h + 1) * C] for c in range(2)], axis=0)
            o = _dot(a_h, vs[h]) + jnp.concatenate(
                [o_inter[c][h * C:(h + 1) * C] for c in range(2)], axis=0)
            o = o * lax.rsqrt(jnp.mean(o * o, axis=-1, keepdims=True) + RMS_EPS)
            o_ref[bb, rows, hv] = (o * gate_ref[bb, rows, hv]).astype(o_ref.dtype)


def _gla_attention(gq, gkT, gv, gate, small, w_a2_pad, b_a):
    B, S, _ = gq.shape
    ts = GLA_TS
    tok = lambda w: pl.BlockSpec((B, ts, w), lambda s: (0, s, 0))
    full = lambda r, c: pl.BlockSpec((r, c), lambda s: (0, 0))
    return pl.pallas_call(
        _gla_kernel,
        grid=(S // ts,),
        in_specs=[
            tok(GLA_KEY_WIDTH), pl.BlockSpec((B, GLA_KEY_WIDTH, ts), lambda s: (0, 0, s)),
            tok(GLA_WIDTH), tok(GLA_WIDTH), tok(SMALL_W),
            full(SMALL_W, GLA_KEY_WIDTH), full(1, GLA_KEY_WIDTH),
        ],
        out_specs=tok(GLA_WIDTH),
        out_shape=jax.ShapeDtypeStruct((B, S, GLA_WIDTH), BF16),
        scratch_shapes=[pltpu.VMEM((B, GLA_KEY_WIDTH, GLA_VAL_DIM), F32)],
        compiler_params=pltpu.CompilerParams(
            dimension_semantics=("arbitrary",), vmem_limit_bytes=VMEM_LIMIT),
        name="gla_attn",
    )(gq, gkT, gv, gate, small, w_a2_pad, b_a)


def _layer_norm(z, g, b):
    mu = jnp.mean(z, axis=-1, keepdims=True)
    d = z - mu
    var = jnp.mean(d * d, axis=-1, keepdims=True)
    return d * lax.rsqrt(var + LN_EPS) * g + b


def _tail_kernel(x_ref, fox_ref, gla_ref, mod_ref, wof_ref, wog_ref, ln1g_ref, ln1b_ref,
                 wg_ref, wu_ref, wd_ref, ln2g_ref, ln2b_ref, o_ref):
    m = mod_ref[0]
    tm = x_ref.shape[0]
    halves = [slice(0, tm // 2), slice(tm // 2, tm)]
    ys = [_dot(fox_ref[r, :], wof_ref[...]) + _dot(gla_ref[r, :], wog_ref[...]) for r in halves]
    x1s, gs, ups = [], [], []
    for r, y in zip(halves, ys):
        x1 = _layer_norm(DEEPNORM_ALPHA * x_ref[r, :] + (1.0 + m[2:3]) * y,
                         ln1g_ref[...], ln1b_ref[...])
        u2 = (x1 * (1.0 + m[4:5]) + m[3:4]).astype(BF16)
        x1s.append(x1)
        gs.append(_dot(u2, wg_ref[...]))
        ups.append(_dot(u2, wu_ref[...]))
    y2s = []
    for g, up in zip(gs, ups):
        h = (g * (1.0 / (1.0 + jnp.exp(-g))) * up).astype(BF16)
        y2s.append(_dot(h, wd_ref[...]))
    for r, x1, y2 in zip(halves, x1s, y2s):
        o_ref[r, :] = _layer_norm(DEEPNORM_ALPHA * x1 + (1.0 + m[5:6]) * y2,
                                  ln2g_ref[...], ln2b_ref[...])


def _tail(x2d, fox2d, gla2d, mod, wo, ln1g, ln1b, wg, wu, wd, ln2g, ln2b, steps_per_batch):
    T, D = x2d.shape
    tm = TAIL_TM
    const = lambda r, c: pl.BlockSpec((r, c), lambda i: (0, 0), pipeline_mode=pl.Buffered(1))
    tok = lambda w: pl.BlockSpec((tm, w), lambda i: (i, 0))
    return pl.pallas_call(
        _tail_kernel,
        grid=(T // tm,),
        in_specs=[
            tok(D), tok(FOX_WIDTH), tok(GLA_WIDTH),
            pl.BlockSpec((1, N_MOD, D), lambda i: (i // steps_per_batch, 0, 0)),
            pl.BlockSpec((FOX_WIDTH, D), lambda i: (0, 0), pipeline_mode=pl.Buffered(1)),
            pl.BlockSpec((GLA_WIDTH, D), lambda i: (1, 0), pipeline_mode=pl.Buffered(1)),
            const(1, D), const(1, D),
            const(D, D_FF), const(D, D_FF), const(D_FF, D), const(1, D), const(1, D),
        ],
        out_specs=tok(D),
        out_shape=jax.ShapeDtypeStruct((T, D), F32),
        compiler_params=pltpu.CompilerParams(
            dimension_semantics=("arbitrary",), vmem_limit_bytes=VMEM_LIMIT),
        name="tail",
    )(x2d, fox2d, gla2d, mod, wo, wo, ln1g, ln1b, wg, wu, wd, ln2g, ln2b)


def kernel(x, c, w_c, b_c, w_in, b_f, w_a2, b_a, g_gla, w_o, ln1_g, ln1_b,
           w_gate, w_up, w_down, ln2_g, ln2_b):
    B, S, D = x.shape
    assert D == D_MODEL and S % INPROJ_TM == 0 and INPROJ_TM == FOX_TK and FOX_TK % FOX_TQ == 0

    b_f_pad = jnp.pad(b_f.reshape(1, FOX_HEADS), ((0, 0), (0, SMALL_W - FOX_HEADS)))
    w_a2_pad = jnp.pad(w_a2, ((FOX_HEADS, SMALL_W - FOX_HEADS - GLA_GATE_RANK), (0, 0))).astype(BF16)

    mod, w_all = _modulation(c, w_c, b_c, w_in)

    fqT, kaug, fvT, gq, gkT, gv, gate, small, drel, stats, wg16, wu16, wd16, wo16 = _in_projection(
        x, mod, w_all, b_f_pad, g_gla.reshape(1, -1), w_gate, w_up, w_down, w_o)
    fox = _fox_attention(fqT, kaug, fvT, drel, stats)
    gla = _gla_attention(gq, gkT, gv, gate, small, w_a2_pad, b_a.reshape(1, -1))

    T = B * S
    out = _tail(x.reshape(T, D), fox.reshape(T, FOX_WIDTH), gla.reshape(T, GLA_WIDTH), mod,
                wo16,
                ln1_g.reshape(1, D), ln1_b.reshape(1, D),
                wg16, wu16, wd16,
                ln2_g.reshape(1, D), ln2_b.reshape(1, D), S // TAIL_TM)
    return out.reshape(B, S, D)
```

```python
import functools
import math

import numpy as np
import jax
import jax.numpy as jnp
from jax import lax
from jax.experimental import pallas as pl
from jax.experimental.pallas import tpu as pltpu

F32 = jnp.float32
BF16 = jnp.bfloat16

D_MODEL = 1024
FOX_WIDTH = 512
FOX_HEAD_DIM = 64
FOX_HEADS = 8
FOX_PAIRS = FOX_HEADS // 2
GLA_WIDTH = 512
GLA_HEADS = 4
GLA_VAL_DIM = 128
GLA_KEY_DIM = 64
GLA_KEY_WIDTH = 256
GLA_GATE_RANK = 16
GLA_GATE_TEMP = 16.0
GLA_CHUNK = 64
D_FF = 2816
LN_EPS = 1e-5
RMS_EPS = 1e-6
N_MOD = 6
DEEPNORM_ALPHA = 2.0 ** 0.25
IN_SPLITS = (FOX_WIDTH, FOX_WIDTH, FOX_WIDTH, FOX_HEADS, GLA_KEY_WIDTH, GLA_KEY_WIDTH,
             GLA_WIDTH, GLA_GATE_RANK, GLA_WIDTH)
LOG2E = math.log2(math.e)

LANES = 128
SUBLANES = 8
MXU_DIM = 256
SMALL_W = LANES
VMEM_LIMIT = 56 * 1024 * 1024

MOD_TN = 768
INPROJ_TM = 512
FOX_TQ = 256
FOX_GROUP_HEADS = 4
FOX_SUM_ROWS = 16
FOX_TK = 512
GLA_TS = 512
GLA_PAIR = 2 * GLA_CHUNK
TAIL_TM = 512

NEG_BIG = -1e30
PRUNE_LOG2 = -1080.0
NORM_SLACK = 1.02

AUG_ROWS = 32


def _log_sigmoid(z):
    return jnp.minimum(z, 0.0) - jnp.log(1.0 + jnp.exp(-jnp.abs(z)))


def _split3(a):
    hi = a.astype(BF16)
    r1 = a - hi.astype(F32)
    mid = r1.astype(BF16)
    lo = (r1 - mid.astype(F32)).astype(BF16)
    return hi, mid, lo


def _dot(a, b):
    return jnp.dot(a, b, preferred_element_type=F32)


def _tri_left(tri, a):
    hi, mid, lo = _split3(a)
    return _dot(tri, hi) + _dot(tri, mid) + _dot(tri, lo)


def _tri_right(a, tri):
    hi, mid, lo = _split3(a)
    return _dot(hi, tri) + _dot(mid, tri) + _dot(lo, tri)


def _regroup_blocks():
    blocks = []
    for dst, src, width in _W_GROUPS:
        blocks += [(dst + c0, src + c0) for c0 in range(0, width, MXU_DIM)]
    blocks.append((_C_SM, None))
    return blocks


def _mod_kernel(cT_ref, w_ref, b_ref, winT_ref, o_ref, wall_ref, *, n_rows):
    w = w_ref[...]
    outs = [jnp.sum(w * cT_ref[:, b:b + 1], axis=0, keepdims=True) for b in range(n_rows)]
    outs.append(jnp.zeros((o_ref.shape[0] - n_rows, w.shape[1]), F32))
    o_ref[...] = jnp.concatenate(outs, axis=0) + b_ref[...]

    j = pl.program_id(0)
    n_steps = pl.num_programs(0)
    for k, (dst, src) in enumerate(_regroup_blocks()):
        @pl.when(j == k % n_steps)
        def _():
            if src is None:
                blk = jnp.concatenate(
                    [winT_ref[_SRC_FF:_SRC_FF + FOX_HEADS, :], winT_ref[_SRC_GA:_SRC_GA + GLA_GATE_RANK, :],
                     jnp.zeros((MXU_DIM - FOX_HEADS - GLA_GATE_RANK, winT_ref.shape[1]), F32)], axis=0)
            else:
                blk = winT_ref[src:src + MXU_DIM, :]
            wall_ref[:, dst:dst + MXU_DIM] = blk.T.astype(BF16)


def _modulation(c, w_c, b_c, w_in):
    B, D = c.shape
    N = w_c.shape[1]
    rows = SUBLANES
    n_steps = N // MOD_TN
    cT_pad = jnp.pad(c.T, ((0, 0), (0, LANES - B)))
    w_inT = w_in.T
    out, w_all = pl.pallas_call(
        functools.partial(_mod_kernel, n_rows=B),
        grid=(n_steps,),
        in_specs=[
            pl.BlockSpec((D, LANES), lambda j: (0, 0)),
            pl.BlockSpec((D, MOD_TN), lambda j: (0, j)),
            pl.BlockSpec((1, MOD_TN), lambda j: (0, j)),
            pl.BlockSpec(w_inT.shape, lambda j: (0, 0), pipeline_mode=pl.Buffered(1)),
        ],
        out_specs=[pl.BlockSpec((rows, MOD_TN), lambda j: (0, j)),
                   pl.BlockSpec((D, _W_COLS), lambda j: (0, 0))],
        out_shape=[jax.ShapeDtypeStruct((rows, N), F32),
                   jax.ShapeDtypeStruct((D, _W_COLS), BF16)],
        compiler_params=pltpu.CompilerParams(
            dimension_semantics=("arbitrary",), vmem_limit_bytes=VMEM_LIMIT),
        name="adaln_mod",
    )(cT_pad, w_c, b_c.reshape(1, N), w_inT)
    return out[:B].reshape(B, N_MOD, D), w_all


_C_FQ, _C_FK, _C_FV = 0, 512, 1024
_C_GQ, _C_SM, _C_GK, _C_GV, _C_GR = 1536, 1792, 2048, 2304, 2816
_W_COLS = _C_GR + GLA_WIDTH
_SRC = [sum(IN_SPLITS[:i]) for i in range(len(IN_SPLITS))]
_SRC_FF, _SRC_GA = _SRC[3], _SRC[7]
_W_GROUPS = ((_C_FQ, _SRC[0], FOX_WIDTH), (_C_FK, _SRC[1], FOX_WIDTH), (_C_FV, _SRC[2], FOX_WIDTH),
             (_C_GQ, _SRC[4], GLA_KEY_WIDTH), (_C_GK, _SRC[5], GLA_KEY_WIDTH),
             (_C_GV, _SRC[6], GLA_WIDTH), (_C_GR, _SRC[8], GLA_WIDTH))
_KAUG_W = FOX_PAIRS * MXU_DIM


def _head_indicator():
    ind = np.zeros((FOX_WIDTH, MXU_DIM), np.float32)
    for h in range(FOX_HEADS):
        ind[h * FOX_HEAD_DIM:(h + 1) * FOX_HEAD_DIM, h] = 1.0
    return ind


def _inproj_kernel(x_ref, mod_ref, w_ref, bf_ref, ind_ref, gg_ref, wg32_ref, wu32_ref, wd32_ref, wo32_ref,
                   fqT_ref, kaug_ref, fvT_ref, gq_ref, gk_ref, gv_ref, gr_ref, sm_ref,
                   drel_ref, stats_ref, wg16_ref, wu16_ref, wd16_ref, wo16_ref, carry_ref):
    tm = x_ref.shape[1]

    @pl.when(pl.program_id(1) == 0)
    def _():
        carry_ref[...] = jnp.zeros_like(carry_ref)

    m = mod_ref[0]
    u = (x_ref[0] * (1.0 + m[1:2]) + m[0:1]).astype(BF16)

    def proj(lo, width):
        return _dot(u, w_ref[:, lo:lo + width])

    gq_small = proj(_C_GQ, 2 * GLA_KEY_WIDTH)
    gq_ref[0] = gq_small[:, :GLA_KEY_WIDTH] * (GLA_KEY_DIM ** -0.5)
    small = gq_small[:, GLA_KEY_WIDTH:GLA_KEY_WIDTH + SMALL_W]
    sm_ref[0] = small

    fq = proj(_C_FQ, FOX_WIDTH) * (FOX_HEAD_DIM ** -0.5 * LOG2E)
    fqT_ref[0] = fq.T.astype(BF16)
    fk32 = proj(_C_FK, FOX_WIDTH)
    fk = fk32.astype(BF16)
    qn2 = jnp.max(_dot((fq * fq).astype(BF16), ind_ref[...])[:, :SMALL_W], axis=0, keepdims=True)
    kn2 = jnp.max(_dot((fk32 * fk32).astype(BF16), ind_ref[...])[:, :SMALL_W], axis=0, keepdims=True)

    wg16_ref[...] = wg32_ref[...].astype(BF16)
    wu16_ref[...] = wu32_ref[...].astype(BF16)
    wd16_ref[...] = wd32_ref[...].astype(BF16)
    wo16_ref[...] = wo32_ref[...].astype(BF16)

    zT = (small + bf_ref[...]).T[:FOX_HEADS, :]
    lfT = _log_sigmoid(zT) * LOG2E
    r = lax.broadcasted_iota(jnp.int32, (tm, tm), 0)
    c = lax.broadcasted_iota(jnp.int32, (tm, tm), 1)
    triu = jnp.where(r <= c, 1.0, 0.0).astype(BF16)
    drelT = _tri_right(lfT, triu)
    drel_ref[0] = drelT

    c_before = carry_ref[...]
    srow = lax.broadcasted_iota(jnp.int32, (SUBLANES, SMALL_W), 0)
    slane = lax.broadcasted_iota(jnp.int32, (SUBLANES, SMALL_W), 1)
    total = jnp.sum(jnp.where(srow == slane, drelT[:, tm - 1:tm], 0.0), axis=0, keepdims=True)
    carry_ref[...] = c_before + total

    hi, mid, lo = _split3(drelT)
    ones3 = jnp.where(lax.broadcasted_iota(jnp.int32, (SUBLANES, tm), 0) < 3, 1.0, 0.0)
    slabT = jnp.concatenate(
        [-hi.astype(F32), -mid.astype(F32), -lo.astype(F32), ones3,
         jnp.zeros((LANES - 4 * SUBLANES, tm), F32)], axis=0)
    slab = slabT.T.astype(BF16)

    fvT_ref[0] = proj(_C_FV, FOX_WIDTH).T.astype(BF16)
    gr = proj(_C_GR, GLA_WIDTH)
    gr_ref[0] = gg_ref[...] * (gr * (1.0 / (1.0 + jnp.exp(-gr))))
    gk_ref[0] = proj(_C_GK, GLA_KEY_WIDTH).T
    gv_ref[0] = proj(_C_GV, GLA_WIDTH).astype(BF16)

    stats_ref[0, 0] = jnp.where(srow == 0, c_before,
                                jnp.where(srow == 1, qn2, jnp.where(srow == 2, kn2, 0.0)))
    for p in range(FOX_PAIRS):
        kaug_ref[0, :, p * MXU_DIM:p * MXU_DIM + LANES] = fk[:, p * LANES:(p + 1) * LANES]
        kaug_ref[0, :, p * MXU_DIM + LANES:(p + 1) * MXU_DIM] = slab


def _in_projection(x, mod, w_all, b_f_pad, g_gla, w_gate, w_up, w_down, w_o):
    B, S, D = x.shape
    tm = INPROJ_TM
    n_steps = B * (S // tm)
    steps_b = S // tm
    up_rows = D // n_steps
    down_rows = 2 * D_FF // n_steps
    assert D % n_steps == 0 and up_rows % 16 == 0 and (2 * D_FF) % n_steps == 0 and down_rows % 16 == 0
    up_spec = pl.BlockSpec((up_rows, D_FF), lambda b, s: (b * steps_b + s, 0))
    down_spec = pl.BlockSpec((down_rows, D), lambda b, s: ((b * steps_b + s) // 2, 0))
    wo_rows = w_o.shape[0] // n_steps
    assert w_o.shape[0] % n_steps == 0 and wo_rows % 16 == 0
    wo_spec = pl.BlockSpec((wo_rows, D), lambda b, s: (b * steps_b + s, 0))
    ind = _head_indicator()
    tok = lambda w: pl.BlockSpec((1, tm, w), lambda b, s: (b, s, 0))
    tokT = lambda h: pl.BlockSpec((1, h, tm), lambda b, s: (b, 0, s))
    outs = pl.pallas_call(
        _inproj_kernel,
        grid=(B, S // tm),
        in_specs=[
            tok(D),
            pl.BlockSpec((1, N_MOD, D), lambda b, s: (b, 0, 0)),
            pl.BlockSpec((D, _W_COLS), lambda b, s: (0, 0), pipeline_mode=pl.Buffered(1)),
            pl.BlockSpec((1, SMALL_W), lambda b, s: (0, 0)),
            pl.BlockSpec(ind.shape, lambda b, s: (0, 0)),
            pl.BlockSpec((1, GLA_WIDTH), lambda b, s: (0, 0)),
            up_spec, up_spec, down_spec, wo_spec,
        ],
        out_specs=[
            tokT(FOX_WIDTH), tok(_KAUG_W), tokT(FOX_WIDTH),
            tok(GLA_KEY_WIDTH), tokT(GLA_KEY_WIDTH), tok(GLA_WIDTH), tok(GLA_WIDTH),
            tok(SMALL_W),
            tokT(FOX_HEADS),
            pl.BlockSpec((1, 1, SUBLANES, SMALL_W), lambda b, s: (b, s, 0, 0)),
            up_spec, up_spec, down_spec, wo_spec,
        ],
        out_shape=[
            jax.ShapeDtypeStruct((B, FOX_WIDTH, S), BF16),
            jax.ShapeDtypeStruct((B, S, _KAUG_W), BF16),
            jax.ShapeDtypeStruct((B, FOX_WIDTH, S), BF16),
            jax.ShapeDtypeStruct((B, S, GLA_KEY_WIDTH), F32),
            jax.ShapeDtypeStruct((B, GLA_KEY_WIDTH, S), F32),
            jax.ShapeDtypeStruct((B, S, GLA_WIDTH), BF16),
            jax.ShapeDtypeStruct((B, S, GLA_WIDTH), F32),
            jax.ShapeDtypeStruct((B, S, SMALL_W), F32),
            jax.ShapeDtypeStruct((B, FOX_HEADS, S), F32),
            jax.ShapeDtypeStruct((B, S // tm, SUBLANES, SMALL_W), F32),
            jax.ShapeDtypeStruct(w_gate.shape, BF16),
            jax.ShapeDtypeStruct(w_up.shape, BF16),
            jax.ShapeDtypeStruct(w_down.shape, BF16),
            jax.ShapeDtypeStruct(w_o.shape, BF16),
        ],
        scratch_shapes=[pltpu.VMEM((1, SMALL_W), F32)],
        compiler_params=pltpu.CompilerParams(
            dimension_semantics=("arbitrary", "arbitrary"), vmem_limit_bytes=VMEM_LIMIT),
        name="in_proj",
    )(x, mod, w_all, b_f_pad, jnp.asarray(ind, BF16), g_gla, w_gate, w_up, w_down, w_o)
    return outs


def _fox_kernel(ntiles_ref, nitems_ref, qT_ref, kaug_ref, vT_ref, drel_ref, crep_ref, o_ref,
                qaug_ref, s_ref, p_ref, mx_ref, acc_ref, m_ref, mnext_ref, shift_ref, alpha_ref):
    tq, tk = FOX_TQ, FOX_TK
    half = FOX_HEAD_DIM
    S = kaug_ref.shape[1]
    nq = S // tq
    nh = FOX_GROUP_HEADS
    n_blk = crep_ref.shape[2] // nh
    g = pl.program_id(0) * pl.num_programs(1) + pl.program_id(1)

    def build_qaug(qi):
        q0 = pl.multiple_of(qi * tq, tq)
        dr_all = drel_ref[0, :, pl.ds(q0, tq)]
        dr = dr_all[0:nh]
        for grp in range(1, dr_all.shape[0] // nh):
            dr = jnp.where(pl.program_id(1) == grp, dr_all[grp * nh:(grp + 1) * nh], dr)
        row = lax.broadcasted_iota(jnp.int32, (2 * half, tq), 0)
        r32 = lax.broadcasted_iota(jnp.int32, (AUG_ROWS, tq), 0)
        for h in range(nh):
            buf = (qi % 2) * nh + h
            qT = qT_ref[0, (h // 2) * 2 * half:(h // 2 + 1) * 2 * half, pl.ds(q0, tq)]
            own = (row < half) if h % 2 == 0 else (row >= half)
            qaug_ref[buf, 0:2 * half, :] = jnp.where(own, qT, jnp.zeros_like(qT))
            hi, mid, lo = _split3(dr[h:h + 1])
            gh = nh * pl.program_id(1) + h
            pick = (r32 == gh) | (r32 == gh + SUBLANES) | (r32 == gh + 2 * SUBLANES)
            slab = jnp.where(r32 == 24, hi.astype(F32),
                             jnp.where(r32 == 25, mid.astype(F32),
                                       jnp.where(r32 == 26, lo.astype(F32),
                                                 jnp.where(pick, 1.0, 0.0))))
            qaug_ref[buf, 2 * half:2 * half + AUG_ROWS, :] = slab.astype(BF16)

    def reset_state():
        acc_ref[...] = jnp.zeros_like(acc_ref)

    sum_rows = jnp.where(lax.broadcasted_iota(jnp.int32, (FOX_SUM_ROWS, tk), 0) == 0, 1.0, 0.0).astype(BF16)

    per = tk // tq

    def col_max8(s):
        return jnp.max(s.reshape(s.shape[0] // SUBLANES, SUBLANES, tq), axis=0)

    def tile_of(qi, t):
        raw = (qi + 1) * tq - (t + 1) * tk
        return pl.multiple_of(jnp.maximum(raw, 0), tq), raw < 0

    def qk_dots(qi, k_off):
        kts = [kaug_ref[0, pl.ds(k_off, tk), p * MXU_DIM:(p + 1) * MXU_DIM] for p in range(nh // 2)]
        return [_dot(kts[h // 2], qaug_ref[(qi % 2) * nh + h]) for h in range(nh)]

    def store_scores(scores):
        for h in range(nh):
            s_ref[h] = scores[h]
            for g in range(per):
                mx_ref[per * h + g] = col_max8(scores[h][g * tq:(g + 1) * tq])

    def fill_groups(d):
        for g in range(1, per):
            @pl.when(d < g)
            def _():
                for h in range(nh):
                    s_ref[h, g * tq:(g + 1) * tq, :] = jnp.full((tq, tq), NEG_BIG, F32)
                    mx_ref[per * h + g] = jnp.full((SUBLANES, tq), NEG_BIG, F32)

    def mask_stage(d):
        d0 = pl.multiple_of(d * tq, tq)
        kr = lax.broadcasted_iota(jnp.int32, (tq, tq), 0)
        qc = lax.broadcasted_iota(jnp.int32, (tq, tq), 1)
        for h in range(nh):
            tri = jnp.where(kr <= qc, s_ref[h, pl.ds(d0, tq), :], NEG_BIG)
            s_ref[h, pl.ds(d0, tq), :] = tri
            mx_ref[per * h + d] = col_max8(tri)
        fill_groups(d)

    def prep_stage(qi, k_off, first):
        i_blk = qi * tq // INPROJ_TM
        for h in range(nh):
            c_i = crep_ref[0, 0, pl.ds(h * n_blk + i_blk, 1), :]
            big_c = [c_i - crep_ref[0, 0, pl.ds(h * n_blk + (k_off + g * tq) // INPROJ_TM, 1), :]
                     for g in range(per)]
            cand = jnp.max(mx_ref[per * h], axis=0, keepdims=True) + big_c[0]
            for g in range(1, per):
                cand = jnp.maximum(cand, jnp.max(mx_ref[per * h + g], axis=0, keepdims=True) + big_c[g])
            m_prev = jnp.where(first, NEG_BIG, m_ref[h])
            m_new = jnp.maximum(m_prev, cand)
            mnext_ref[h] = m_new
            for g in range(per):
                shift_ref[per * h + g] = m_new - big_c[g]
            alpha_ref[h] = jnp.exp2(m_prev - m_new)

    def exp_stage():
        for h in range(nh):
            for g in range(per):
                rows = slice(g * tq, (g + 1) * tq)
                p_ref[h, rows, :] = jnp.exp2(s_ref[h, rows, :] - shift_ref[per * h + g]).astype(BF16)

    def pv_stage(k_off):
        for h in range(nh):
            vT = jnp.concatenate([vT_ref[0, h * half:(h + 1) * half, pl.ds(k_off, tk)], sum_rows], axis=0)
            acc_ref[h] = alpha_ref[h] * acc_ref[h] + _dot(vT, p_ref[h])
            m_ref[h] = mnext_ref[h]

    def finalize(qi):
        outs = []
        for h in range(nh):
            acc = acc_ref[h]
            outs.append(acc[:half] / acc[half:half + 1])
        oT = jnp.concatenate(outs, axis=0)
        o_ref[0, pl.ds(pl.multiple_of(qi * tq, tq), tq), :] = oT.T.astype(o_ref.dtype)

    def fix_masks(t, clamped):
        @pl.when(t == 0)
        def _():
            mask_stage(jnp.where(clamped, 0, per - 1))

        @pl.when((t > 0) & clamped)
        def _():
            fill_groups(0)

    zero = jnp.int32(0)
    qaug_ref[...] = jnp.zeros_like(qaug_ref)
    build_qaug(zero)
    build_qaug(zero + 1)
    reset_state()
    k0, cl0 = tile_of(zero, zero)
    store_scores(qk_dots(zero, k0))
    fix_masks(zero, cl0)
    prep_stage(zero, k0, True)

    def body(it, carry):
        qi, t = carry
        last = t == 0
        qi2 = jnp.where(last, qi + 1, qi)
        t2 = jnp.where(last, ntiles_ref[g * nq + jnp.minimum(qi + 1, nq - 1)] - 1, t - 1)
        k_cur, _ = tile_of(qi, t)
        k_nxt, cl_nxt = tile_of(qi2, t2)
        exp_stage()
        nxt = qk_dots(qi2, k_nxt)
        pv_stage(k_cur)
        store_scores(nxt)
        prep_stage(qi2, k_nxt, last)

        @pl.when((t2 == 0) | cl_nxt)
        def _():
            fix_masks(t2, cl_nxt)
            prep_stage(qi2, k_nxt, last)

        @pl.when(last)
        def _():
            finalize(qi)
            reset_state()

            @pl.when(qi + 2 < nq)
            def _():
                build_qaug(qi + 2)

        return qi2, t2

    qi, t = lax.fori_loop(0, nitems_ref[g] - 1, body, (jnp.int32(0), jnp.int32(0)))
    exp_stage()
    pv_stage(tile_of(qi, t)[0])
    finalize(qi)


def _prune_plan(stats, drel, nq):
    B, n_blk = stats.shape[:2]
    per = FOX_TK // FOX_TQ
    c = stats[:, :, 0, :FOX_HEADS]
    qn = jnp.sqrt(stats[:, :, 1, :FOX_HEADS])
    kn = jnp.sqrt(stats[:, :, 2, :FOX_HEADS])
    d_in = jnp.transpose(drel[:, :, FOX_TQ - 1::FOX_TQ], (0, 2, 1))
    d_end = jnp.repeat(c, per, axis=1) + d_in
    d_before = jnp.concatenate([jnp.zeros_like(d_end[:, :1]), d_end[:, :-1]], axis=1)
    kn_q = jnp.repeat(kn, per, axis=1)
    qn_q = jnp.repeat(qn, per, axis=1)
    kn_prev = jnp.concatenate([jnp.zeros_like(kn_q[:, :1]), kn_q[:, :-1]], axis=1)
    kn_tile = jnp.maximum(kn_q, kn_prev)
    bound = (d_before[:, :, None, :] - d_end[:, None, :, :]
             + NORM_SLACK * qn_q[:, :, None, :] * (kn_tile[:, None, :, :] + kn_q[:, :, None, :]))
    back = np.arange(nq)[:, None] - np.arange(nq)[None, :]
    is_tile = (back > 0) & (back % per == 0)
    t_of = np.where(is_tile, back // per, 0).astype(np.int32)
    needed = is_tile[None, :, :, None] & ~(bound <= PRUNE_LOG2)
    nh = FOX_GROUP_HEADS
    needed_g = jnp.any(needed.reshape(B, nq, nq, FOX_HEADS // nh, nh), axis=-1)
    n_tiles = 1 + jnp.max(jnp.where(needed_g, t_of[None, :, :, None], 0), axis=2)
    n_tiles = jnp.transpose(n_tiles, (0, 2, 1))
    n_items = jnp.sum(n_tiles, axis=2)
    return n_tiles.reshape(-1).astype(jnp.int32), n_items.reshape(-1).astype(jnp.int32)


def _fox_attention(fqT, kaug, fvT, drel, stats):
    B, _, S = fqT.shape
    tq, tk = FOX_TQ, FOX_TK
    n_blk = S // INPROJ_TM
    nh = FOX_GROUP_HEADS
    n_groups = FOX_HEADS // nh
    c = stats[:, :, 0, :FOX_HEADS]
    c = jnp.transpose(c, (0, 2, 1)).reshape(B, n_groups, nh * n_blk, 1)
    crep = jnp.broadcast_to(c, (B, n_groups, nh * n_blk, tq))
    n_tiles, n_items = _prune_plan(stats, drel, S // tq)
    grid_spec = pltpu.PrefetchScalarGridSpec(
        num_scalar_prefetch=2,
        grid=(B, n_groups),
        in_specs=[
            pl.BlockSpec((1, nh * FOX_HEAD_DIM, S), lambda b, p, js, ni: (b, p, 0)),
            pl.BlockSpec((1, S, nh // 2 * MXU_DIM), lambda b, p, js, ni: (b, 0, p)),
            pl.BlockSpec((1, nh * FOX_HEAD_DIM, S), lambda b, p, js, ni: (b, p, 0)),
            pl.BlockSpec((1, FOX_HEADS, S), lambda b, p, js, ni: (b, 0, 0)),
            pl.BlockSpec((1, 1, nh * n_blk, tq), lambda b, p, js, ni: (b, p, 0, 0)),
        ],
        out_specs=pl.BlockSpec((1, S, nh * FOX_HEAD_DIM), lambda b, p, js, ni: (b, 0, p)),
        scratch_shapes=[
            pltpu.VMEM((2 * nh, MXU_DIM, tq), BF16),
            pltpu.VMEM((nh, tk, tq), F32),
            pltpu.VMEM((nh, tk, tq), BF16),
            pltpu.VMEM((nh * (tk // tq), SUBLANES, tq), F32),
            pltpu.VMEM((nh, FOX_HEAD_DIM + FOX_SUM_ROWS, tq), F32),
            pltpu.VMEM((nh, 1, tq), F32),
            pltpu.VMEM((nh, 1, tq), F32),
            pltpu.VMEM((nh * (tk // tq), 1, tq), F32),
            pltpu.VMEM((nh, 1, tq), F32),
        ])
    return pl.pallas_call(
        _fox_kernel,
        grid_spec=grid_spec,
        out_shape=jax.ShapeDtypeStruct((B, S, FOX_WIDTH), BF16),
        compiler_params=pltpu.CompilerParams(
            dimension_semantics=("arbitrary", "arbitrary"),
            vmem_limit_bytes=VMEM_LIMIT),
        name="fox_attn",
    )(n_tiles, n_items, fqT, kaug, fvT, drel, crep)


def _gla_kernel(q_ref, kT_ref, v_ref, gate_ref, sm_ref, wa_ref, ba_ref, o_ref, state_ref):
    nb, ts = q_ref.shape[0], q_ref.shape[1]
    C, P = GLA_CHUNK, GLA_PAIR
    dk, dv, H = GLA_KEY_DIM, GLA_VAL_DIM, GLA_HEADS
    KW = GLA_KEY_WIDTH

    @pl.when(pl.program_id(0) == 0)
    def _():
        state_ref[...] = jnp.zeros_like(state_ref)

    la_all = [_log_sigmoid(_dot(sm_ref[bb].astype(BF16), wa_ref[...]) + ba_ref[...])
              * (1.0 / GLA_GATE_TEMP) for bb in range(nb)]

    ri = lax.broadcasted_iota(jnp.int32, (P, P), 0)
    ci = lax.broadcasted_iota(jnp.int32, (P, P), 1)
    tril2 = jnp.where(((ri < C) == (ci < C)) & (ci <= ri), 1.0, 0.0).astype(BF16)

    rs = lax.broadcasted_iota(jnp.int32, (2 * H * C, P), 0)
    cs = lax.broadcasted_iota(jnp.int32, (2 * H * C, P), 1)
    causal = ((rs >= H * C) == (cs >= C)) & ((cs & (C - 1)) <= (rs & (C - 1)))

    lane_q = lax.broadcasted_iota(jnp.int32, (C, KW), 1)
    lane_t = lax.broadcasted_iota(jnp.int32, (KW, P), 1)
    first = lane_t < C

    per_b = ts // P
    n_slabs = nb * per_b
    slab_b = [k // per_b for k in range(n_slabs)]
    slab_rows = [slice((k % per_b) * P, (k % per_b + 1) * P) for k in range(n_slabs)]
    bs = [_tri_left(tril2, la_all[slab_b[k]][slab_rows[k]]) for k in range(n_slabs)]

    qms, scores, kv0s, kv1s, dec0s, dec1s, vss = [], [], [], [], [], [], []
    for pi in range(n_slabs):
        bb, rows = slab_b[pi], slab_rows[pi]
        b = bs[pi]
        bT = b.T
        bl0 = bT[:, C - 1:C]
        bl1 = bT[:, P - 1:P]
        q_dec = q_ref[bb, rows, :] * jnp.exp(b)
        kT = kT_ref[bb, :, rows]
        k_invT = (kT * jnp.exp(-bT)).astype(BF16)
        k_teT = kT * jnp.exp(jnp.where(first, bl0, bl1) - bT)
        zt = jnp.zeros_like(k_teT)
        k_te0 = jnp.where(first, k_teT, zt).astype(BF16)
        k_te1 = jnp.where(first, zt, k_teT).astype(BF16)
        dec0 = jnp.exp(bl0)
        dec1 = jnp.exp(bl1)

        vs = [v_ref[bb, rows, h * dv:(h + 1) * dv] for h in range(H)]
        kv = [_dot(jnp.concatenate([k_te0[h * dk:(h + 1) * dk, :], k_te1[h * dk:(h + 1) * dk, :]], axis=0),
                   vs[h]) for h in range(H)]
        kv0s.append(jnp.concatenate([kv[h][:dk] for h in range(H)], axis=0))
        kv1s.append(jnp.concatenate([kv[h][dk:] for h in range(H)], axis=0))
        dec0s.append(dec0)
        dec1s.append(dec1)
        vss.append(vs)

        zq = jnp.zeros((C, KW), F32)
        qm = [jnp.concatenate(
            [jnp.where((lane_q >= h * dk) & (lane_q < (h + 1) * dk), q_dec[c * C:(c + 1) * C], zq)
             for h in range(H)], axis=0).astype(BF16) for c in range(2)]
        a = _dot(jnp.concatenate(qm, axis=0), k_invT)
        scores.append(jnp.where(causal, a, 0.0).astype(BF16))
        qms.append(qm)

    o_inters = []
    for pi in range(n_slabs):
        if pi % per_b == 0:
            s = state_ref[slab_b[pi]]
        s1 = dec0s[pi] * s + kv0s[pi]
        o_inters.append([_dot(qms[pi][0], s.astype(BF16)), _dot(qms[pi][1], s1.astype(BF16))])
        s = dec1s[pi] * s1 + kv1s[pi]
        if pi % per_b == per_b - 1:
            state_ref[slab_b[pi]] = s

    for pi in range(n_slabs):
        bb, rows = slab_b[pi], slab_rows[pi]
        a, o_inter, vs = scores[pi], o_inters[pi], vss[pi]
        for h in range(H):
            hv = slice(h * dv, (h + 1) * dv)
            a_h = jnp.concatenate([a[(c * H + h) * C:(c * H + h + 1) * C] for c in range(2)], axis=0)
            o = _dot(a_h, vs[h]) + jnp.concatenate(
                [o_inter[c][h * C:(h + 1) * C] for c in range(2)], axis=0)
            o = o * lax.rsqrt(jnp.mean(o * o, axis=-1, keepdims=True) + RMS_EPS)
            o_ref[bb, rows, hv] = (o * gate_ref[bb, rows, hv]).astype(o_ref.dtype)


def _gla_attention(gq, gkT, gv, gate, small, w_a2_pad, b_a):
    B, S, _ = gq.shape
    ts = GLA_TS
    tok = lambda w: pl.BlockSpec((B, ts, w), lambda s: (0, s, 0))
    full = lambda r, c: pl.BlockSpec((r, c), lambda s: (0, 0))
    return pl.pallas_call(
        _gla_kernel,
        grid=(S // ts,),
        in_specs=[
            tok(GLA_KEY_WIDTH), pl.BlockSpec((B, GLA_KEY_WIDTH, ts), lambda s: (0, 0, s)),
            tok(GLA_WIDTH), tok(GLA_WIDTH), tok(SMALL_W),
            full(SMALL_W, GLA_KEY_WIDTH), full(1, GLA_KEY_WIDTH),
        ],
        out_specs=tok(GLA_WIDTH),
        out_shape=jax.ShapeDtypeStruct((B, S, GLA_WIDTH), BF16),
        scratch_shapes=[pltpu.VMEM((B, GLA_KEY_WIDTH, GLA_VAL_DIM), F32)],
        compiler_params=pltpu.CompilerParams(
            dimension_semantics=("arbitrary",), vmem_limit_bytes=VMEM_LIMIT),
        name="gla_attn",
    )(gq, gkT, gv, gate, small, w_a2_pad, b_a)


def _layer_norm(z, g, b):
    mu = jnp.mean(z, axis=-1, keepdims=True)
    d = z - mu
    var = jnp.mean(d * d, axis=-1, keepdims=True)
    return d * lax.rsqrt(var + LN_EPS) * g + b


def _tail_kernel(x_ref, fox_ref, gla_ref, mod_ref, wof_ref, wog_ref, ln1g_ref, ln1b_ref,
                 wg_ref, wu_ref, wd_ref, ln2g_ref, ln2b_ref, o_ref):
    m = mod_ref[0]
    tm = x_ref.shape[0]
    halves = [slice(0, tm // 2), slice(tm // 2, tm)]
    ys = [_dot(fox_ref[r, :], wof_ref[...]) + _dot(gla_ref[r, :], wog_ref[...]) for r in halves]
    x1s, gs, ups = [], [], []
    for r, y in zip(halves, ys):
        x1 = _layer_norm(DEEPNORM_ALPHA * x_ref[r, :] + (1.0 + m[2:3]) * y,
                         ln1g_ref[...], ln1b_ref[...])
        u2 = (x1 * (1.0 + m[4:5]) + m[3:4]).astype(BF16)
        x1s.append(x1)
        gs.append(_dot(u2, wg_ref[...]))
        ups.append(_dot(u2, wu_ref[...]))
    y2s = []
    for g, up in zip(gs, ups):
        h = (g * (1.0 / (1.0 + jnp.exp(-g))) * up).astype(BF16)
        y2s.append(_dot(h, wd_ref[...]))
    for r, x1, y2 in zip(halves, x1s, y2s):
        o_ref[r, :] = _layer_norm(DEEPNORM_ALPHA * x1 + (1.0 + m[5:6]) * y2,
                                  ln2g_ref[...], ln2b_ref[...])


def _tail(x2d, fox2d, gla2d, mod, wo, ln1g, ln1b, wg, wu, wd, ln2g, ln2b, steps_per_batch):
    T, D = x2d.shape
    tm = TAIL_TM
    const = lambda r, c: pl.BlockSpec((r, c), lambda i: (0, 0), pipeline_mode=pl.Buffered(1))
    tok = lambda w: pl.BlockSpec((tm, w), lambda i: (i, 0))
    return pl.pallas_call(
        _tail_kernel,
        grid=(T // tm,),
        in_specs=[
            tok(D), tok(FOX_WIDTH), tok(GLA_WIDTH),
            pl.BlockSpec((1, N_MOD, D), lambda i: (i // steps_per_batch, 0, 0)),
            pl.BlockSpec((FOX_WIDTH, D), lambda i: (0, 0), pipeline_mode=pl.Buffered(1)),
            pl.BlockSpec((GLA_WIDTH, D), lambda i: (1, 0), pipeline_mode=pl.Buffered(1)),
            const(1, D), const(1, D),
            const(D, D_FF), const(D, D_FF), const(D_FF, D), const(1, D), const(1, D),
        ],
        out_specs=tok(D),
        out_shape=jax.ShapeDtypeStruct((T, D), F32),
        compiler_params=pltpu.CompilerParams(
            dimension_semantics=("arbitrary",), vmem_limit_bytes=VMEM_LIMIT),
        name="tail",
    )(x2d, fox2d, gla2d, mod, wo, wo, ln1g, ln1b, wg, wu, wd, ln2g, ln2b)


def kernel(x, c, w_c, b_c, w_in, b_f, w_a2, b_a, g_gla, w_o, ln1_g, ln1_b,
           w_gate, w_up, w_down, ln2_g, ln2_b):
    B, S, D = x.shape
    assert D == D_MODEL and S % INPROJ_TM == 0 and INPROJ_TM == FOX_TK and FOX_TK % FOX_TQ == 0

    b_f_pad = jnp.pad(b_f.reshape(1, FOX_HEADS), ((0, 0), (0, SMALL_W - FOX_HEADS)))
    w_a2_pad = jnp.pad(w_a2, ((FOX_HEADS, SMALL_W - FOX_HEADS - GLA_GATE_RANK), (0, 0))).astype(BF16)

    mod, w_all = _modulation(c, w_c, b_c, w_in)

    fqT, kaug, fvT, gq, gkT, gv, gate, small, drel, stats, wg16, wu16, wd16, wo16 = _in_projection(
        x, mod, w_all, b_f_pad, g_gla.reshape(1, -1), w_gate, w_up, w_down, w_o)
    fox = _fox_attention(fqT, kaug, fvT, drel, stats)
    gla = _gla_attention(gq, gkT, gv, gate, small, w_a2_pad, b_a.reshape(1, -1))

    T = B * S
    out = _tail(x.reshape(T, D), fox.reshape(T, FOX_WIDTH), gla.reshape(T, GLA_WIDTH), mod,
                wo16,
                ln1_g.reshape(1, D), ln1_b.reshape(1, D),
                wg16, wu16, wd16,
                ln2_g.reshape(1, D), ln2_b.reshape(1, D), S // TAIL_TM)
    return out.reshape(B, S, D)
```

```python
import functools
import math

import numpy as np
import jax
import jax.numpy as jnp
from jax import lax
from jax.experimental import pallas as pl
from jax.experimental.pallas import tpu as pltpu

F32 = jnp.float32
BF16 = jnp.bfloat16

D_MODEL = 1024
FOX_WIDTH = 512
FOX_HEAD_DIM = 64
FOX_HEADS = 8
FOX_PAIRS = FOX_HEADS // 2
GLA_WIDTH = 512
GLA_HEADS = 4
GLA_VAL_DIM = 128
GLA_KEY_DIM = 64
GLA_KEY_WIDTH = 256
GLA_GATE_RANK = 16
GLA_GATE_TEMP = 16.0
GLA_CHUNK = 64
D_FF = 2816
LN_EPS = 1e-5
RMS_EPS = 1e-6
N_MOD = 6
DEEPNORM_ALPHA = 2.0 ** 0.25
IN_SPLITS = (FOX_WIDTH, FOX_WIDTH, FOX_WIDTH, FOX_HEADS, GLA_KEY_WIDTH, GLA_KEY_WIDTH,
             GLA_WIDTH, GLA_GATE_RANK, GLA_WIDTH)
LOG2E = math.log2(math.e)

LANES = 128
SUBLANES = 8
MXU_DIM = 256
SMALL_W = LANES
VMEM_LIMIT = 56 * 1024 * 1024

MOD_TN = 768
INPROJ_TM = 512
FOX_TQ = 256
FOX_GROUP_HEADS = 4
FOX_SUM_ROWS = 16
FOX_TK = 512
GLA_TS = 512
GLA_PAIR = 2 * GLA_CHUNK
TAIL_TM = 512

NEG_BIG = -1e30
PRUNE_LOG2 = -160.0
NORM_SLACK = 1.02

AUG_ROWS = 32


def _log_sigmoid(z):
    return jnp.minimum(z, 0.0) - jnp.log(1.0 + jnp.exp(-jnp.abs(z)))


def _split3(a):
    hi = a.astype(BF16)
    r1 = a - hi.astype(F32)
    mid = r1.astype(BF16)
    lo = (r1 - mid.astype(F32)).astype(BF16)
    return hi, mid, lo


def _dot(a, b):
    return jnp.dot(a, b, preferred_element_type=F32)


def _tri_left(tri, a):
    hi, mid, lo = _split3(a)
    return _dot(tri, hi) + _dot(tri, mid) + _dot(tri, lo)


def _tri_right(a, tri):
    hi, mid, lo = _split3(a)
    return _dot(hi, tri) + _dot(mid, tri) + _dot(lo, tri)


def _regroup_blocks():
    blocks = []
    for dst, src, width in _W_GROUPS:
        blocks += [(dst + c0, src + c0) for c0 in range(0, width, MXU_DIM)]
    blocks.append((_C_SM, None))
    return blocks


def _mod_kernel(cT_ref, w_ref, b_ref, winT_ref, o_ref, wall_ref, *, n_rows):
    w = w_ref[...]
    outs = [jnp.sum(w * cT_ref[:, b:b + 1], axis=0, keepdims=True) for b in range(n_rows)]
    outs.append(jnp.zeros((o_ref.shape[0] - n_rows, w.shape[1]), F32))
    o_ref[...] = jnp.concatenate(outs, axis=0) + b_ref[...]

    j = pl.program_id(0)
    n_steps = pl.num_programs(0)
    for k, (dst, src) in enumerate(_regroup_blocks()):
        @pl.when(j == k % n_steps)
        def _():
            if src is None:
                blk = jnp.concatenate(
                    [winT_ref[_SRC_FF:_SRC_FF + FOX_HEADS, :], winT_ref[_SRC_GA:_SRC_GA + GLA_GATE_RANK, :],
                     jnp.zeros((MXU_DIM - FOX_HEADS - GLA_GATE_RANK, winT_ref.shape[1]), F32)], axis=0)
            else:
                blk = winT_ref[src:src + MXU_DIM, :]
            wall_ref[:, dst:dst + MXU_DIM] = blk.T.astype(BF16)


def _modulation(c, w_c, b_c, w_in):
    B, D = c.shape
    N = w_c.shape[1]
    rows = SUBLANES
    n_steps = N // MOD_TN
    cT_pad = jnp.pad(c.T, ((0, 0), (0, LANES - B)))
    w_inT = w_in.T
    out, w_all = pl.pallas_call(
        functools.partial(_mod_kernel, n_rows=B),
        grid=(n_steps,),
        in_specs=[
            pl.BlockSpec((D, LANES), lambda j: (0, 0)),
            pl.BlockSpec((D, MOD_TN), lambda j: (0, j)),
            pl.BlockSpec((1, MOD_TN), lambda j: (0, j)),
            pl.BlockSpec(w_inT.shape, lambda j: (0, 0), pipeline_mode=pl.Buffered(1)),
        ],
        out_specs=[pl.BlockSpec((rows, MOD_TN), lambda j: (0, j)),
                   pl.BlockSpec((D, _W_COLS), lambda j: (0, 0))],
        out_shape=[jax.ShapeDtypeStruct((rows, N), F32),
                   jax.ShapeDtypeStruct((D, _W_COLS), BF16)],
        compiler_params=pltpu.CompilerParams(
            dimension_semantics=("arbitrary",), vmem_limit_bytes=VMEM_LIMIT),
        name="adaln_mod",
    )(cT_pad, w_c, b_c.reshape(1, N), w_inT)
    return out[:B].reshape(B, N_MOD, D), w_all


_C_FQ, _C_FK, _C_FV = 0, 512, 1024
_C_GQ, _C_SM, _C_GK, _C_GV, _C_GR = 1536, 1792, 2048, 2304, 2816
_W_COLS = _C_GR + GLA_WIDTH
_SRC = [sum(IN_SPLITS[:i]) for i in range(len(IN_SPLITS))]
_SRC_FF, _SRC_GA = _SRC[3], _SRC[7]
_W_GROUPS = ((_C_FQ, _SRC[0], FOX_WIDTH), (_C_FK, _SRC[1], FOX_WIDTH), (_C_FV, _SRC[2], FOX_WIDTH),
             (_C_GQ, _SRC[4], GLA_KEY_WIDTH), (_C_GK, _SRC[5], GLA_KEY_WIDTH),
             (_C_GV, _SRC[6], GLA_WIDTH), (_C_GR, _SRC[8], GLA_WIDTH))
_KAUG_W = FOX_PAIRS * MXU_DIM


def _head_indicator():
    ind = np.zeros((FOX_WIDTH, MXU_DIM), np.float32)
    for h in range(FOX_HEADS):
        ind[h * FOX_HEAD_DIM:(h + 1) * FOX_HEAD_DIM, h] = 1.0
    return ind


def _inproj_kernel(x_ref, mod_ref, w_ref, bf_ref, ind_ref, gg_ref, wg32_ref, wu32_ref, wd32_ref, wo32_ref,
                   fqT_ref, kaug_ref, fvT_ref, gq_ref, gk_ref, gv_ref, gr_ref, sm_ref,
                   drel_ref, stats_ref, wg16_ref, wu16_ref, wd16_ref, wo16_ref, carry_ref):
    tm = x_ref.shape[1]

    wg16_ref[...] = wg32_ref[...].astype(BF16)
    wu16_ref[...] = wu32_ref[...].astype(BF16)
    wd16_ref[...] = wd32_ref[...].astype(BF16)
    wo16_ref[...] = wo32_ref[...].astype(BF16)

    @pl.when(pl.program_id(1) == 0)
    def _():
        carry_ref[...] = jnp.zeros_like(carry_ref)

    m = mod_ref[0]
    u = (x_ref[0] * (1.0 + m[1:2]) + m[0:1]).astype(BF16)

    def proj(lo, width):
        return _dot(u, w_ref[:, lo:lo + width])

    gq_small = proj(_C_GQ, 2 * GLA_KEY_WIDTH)
    gq_ref[0] = gq_small[:, :GLA_KEY_WIDTH] * (GLA_KEY_DIM ** -0.5)
    small = gq_small[:, GLA_KEY_WIDTH:GLA_KEY_WIDTH + SMALL_W]
    sm_ref[0] = small

    fq = proj(_C_FQ, FOX_WIDTH) * (FOX_HEAD_DIM ** -0.5 * LOG2E)
    fqT_ref[0] = fq.T.astype(BF16)
    fk32 = proj(_C_FK, FOX_WIDTH)
    fk = fk32.astype(BF16)
    qn2 = jnp.max(_dot((fq * fq).astype(BF16), ind_ref[...])[:, :SMALL_W], axis=0, keepdims=True)
    kn2 = jnp.max(_dot((fk32 * fk32).astype(BF16), ind_ref[...])[:, :SMALL_W], axis=0, keepdims=True)

    zT = (small + bf_ref[...]).T[:FOX_HEADS, :]
    lfT = _log_sigmoid(zT) * LOG2E
    r = lax.broadcasted_iota(jnp.int32, (tm, tm), 0)
    c = lax.broadcasted_iota(jnp.int32, (tm, tm), 1)
    triu = jnp.where(r <= c, 1.0, 0.0).astype(BF16)
    drelT = _tri_right(lfT, triu)
    drel_ref[0] = drelT

    c_before = carry_ref[...]
    srow = lax.broadcasted_iota(jnp.int32, (SUBLANES, SMALL_W), 0)
    slane = lax.broadcasted_iota(jnp.int32, (SUBLANES, SMALL_W), 1)
    total = jnp.sum(jnp.where(srow == slane, drelT[:, tm - 1:tm], 0.0), axis=0, keepdims=True)
    carry_ref[...] = c_before + total

    hi, mid, lo = _split3(drelT)
    ones3 = jnp.where(lax.broadcasted_iota(jnp.int32, (SUBLANES, tm), 0) < 3, 1.0, 0.0)
    slabT = jnp.concatenate(
        [-hi.astype(F32), -mid.astype(F32), -lo.astype(F32), ones3,
         jnp.zeros((LANES - 4 * SUBLANES, tm), F32)], axis=0)
    slab = slabT.T.astype(BF16)

    fvT_ref[0] = proj(_C_FV, FOX_WIDTH).T.astype(BF16)
    gv_ref[0] = proj(_C_GV, GLA_WIDTH).astype(BF16)
    gr = proj(_C_GR, GLA_WIDTH)
    gr_ref[0] = gg_ref[...] * (gr * (1.0 / (1.0 + jnp.exp(-gr))))
    gk_ref[0] = proj(_C_GK, GLA_KEY_WIDTH).T

    stats_ref[0, 0] = jnp.where(srow == 0, c_before,
                                jnp.where(srow == 1, qn2, jnp.where(srow == 2, kn2, 0.0)))
    for p in range(FOX_PAIRS):
        kaug_ref[0, :, p * MXU_DIM:p * MXU_DIM + LANES] = fk[:, p * LANES:(p + 1) * LANES]
        kaug_ref[0, :, p * MXU_DIM + LANES:(p + 1) * MXU_DIM] = slab


def _in_projection(x, mod, w_all, b_f_pad, g_gla, w_gate, w_up, w_down, w_o):
    B, S, D = x.shape
    tm = INPROJ_TM
    n_steps = B * (S // tm)
    steps_b = S // tm
    up_rows = D // n_steps
    down_rows = 2 * D_FF // n_steps
    assert D % n_steps == 0 and up_rows % 16 == 0 and (2 * D_FF) % n_steps == 0 and down_rows % 16 == 0
    up_spec = pl.BlockSpec((up_rows, D_FF), lambda b, s: (b * steps_b + s, 0))
    down_spec = pl.BlockSpec((down_rows, D), lambda b, s: ((b * steps_b + s) // 2, 0))
    wo_rows = w_o.shape[0] // n_steps
    assert w_o.shape[0] % n_steps == 0 and wo_rows % 16 == 0
    wo_spec = pl.BlockSpec((wo_rows, D), lambda b, s: (b * steps_b + s, 0))
    ind = _head_indicator()
    tok = lambda w: pl.BlockSpec((1, tm, w), lambda b, s: (b, s, 0))
    tokT = lambda h: pl.BlockSpec((1, h, tm), lambda b, s: (b, 0, s))
    outs = pl.pallas_call(
        _inproj_kernel,
        grid=(B, S // tm),
        in_specs=[
            tok(D),
            pl.BlockSpec((1, N_MOD, D), lambda b, s: (b, 0, 0)),
            pl.BlockSpec((D, _W_COLS), lambda b, s: (0, 0), pipeline_mode=pl.Buffered(1)),
            pl.BlockSpec((1, SMALL_W), lambda b, s: (0, 0)),
            pl.BlockSpec(ind.shape, lambda b, s: (0, 0)),
            pl.BlockSpec((1, GLA_WIDTH), lambda b, s: (0, 0)),
            up_spec, up_spec, down_spec, wo_spec,
        ],
        out_specs=[
            tokT(FOX_WIDTH), tok(_KAUG_W), tokT(FOX_WIDTH),
            tok(GLA_KEY_WIDTH), tokT(GLA_KEY_WIDTH), tok(GLA_WIDTH), tok(GLA_WIDTH),
            tok(SMALL_W),
            tokT(FOX_HEADS),
            pl.BlockSpec((1, 1, SUBLANES, SMALL_W), lambda b, s: (b, s, 0, 0)),
            up_spec, up_spec, down_spec, wo_spec,
        ],
        out_shape=[
            jax.ShapeDtypeStruct((B, FOX_WIDTH, S), BF16),
            jax.ShapeDtypeStruct((B, S, _KAUG_W), BF16),
            jax.ShapeDtypeStruct((B, FOX_WIDTH, S), BF16),
            jax.ShapeDtypeStruct((B, S, GLA_KEY_WIDTH), F32),
            jax.ShapeDtypeStruct((B, GLA_KEY_WIDTH, S), F32),
            jax.ShapeDtypeStruct((B, S, GLA_WIDTH), BF16),
            jax.ShapeDtypeStruct((B, S, GLA_WIDTH), F32),
            jax.ShapeDtypeStruct((B, S, SMALL_W), F32),
            jax.ShapeDtypeStruct((B, FOX_HEADS, S), F32),
            jax.ShapeDtypeStruct((B, S // tm, SUBLANES, SMALL_W), F32),
            jax.ShapeDtypeStruct(w_gate.shape, BF16),
            jax.ShapeDtypeStruct(w_up.shape, BF16),
            jax.ShapeDtypeStruct(w_down.shape, BF16),
            jax.ShapeDtypeStruct(w_o.shape, BF16),
        ],
        scratch_shapes=[pltpu.VMEM((1, SMALL_W), F32)],
        compiler_params=pltpu.CompilerParams(
            dimension_semantics=("arbitrary", "arbitrary"), vmem_limit_bytes=VMEM_LIMIT),
        name="in_proj",
    )(x, mod, w_all, b_f_pad, jnp.asarray(ind, BF16), g_gla, w_gate, w_up, w_down, w_o)
    return outs


def _fox_kernel(ntiles_ref, nitems_ref, qT_ref, kaug_ref, vT_ref, drel_ref, crep_ref, o_ref,
                qaug_ref, s_ref, p_ref, mx_ref, acc_ref, m_ref, mnext_ref, shift_ref, alpha_ref):
    tq, tk = FOX_TQ, FOX_TK
    half = FOX_HEAD_DIM
    S = kaug_ref.shape[1]
    nq = S // tq
    nh = FOX_GROUP_HEADS
    n_blk = crep_ref.shape[2] // nh
    g = pl.program_id(0) * pl.num_programs(1) + pl.program_id(1)

    def build_qaug(qi):
        q0 = pl.multiple_of(qi * tq, tq)
        dr_all = drel_ref[0, :, pl.ds(q0, tq)]
        dr = dr_all[0:nh]
        for grp in range(1, dr_all.shape[0] // nh):
            dr = jnp.where(pl.program_id(1) == grp, dr_all[grp * nh:(grp + 1) * nh], dr)
        row = lax.broadcasted_iota(jnp.int32, (2 * half, tq), 0)
        r32 = lax.broadcasted_iota(jnp.int32, (AUG_ROWS, tq), 0)
        for h in range(nh):
            buf = (qi % 2) * nh + h
            qT = qT_ref[0, (h // 2) * 2 * half:(h // 2 + 1) * 2 * half, pl.ds(q0, tq)]
            own = (row < half) if h % 2 == 0 else (row >= half)
            qaug_ref[buf, 0:2 * half, :] = jnp.where(own, qT, jnp.zeros_like(qT))
            hi, mid, lo = _split3(dr[h:h + 1])
            gh = nh * pl.program_id(1) + h
            pick = (r32 == gh) | (r32 == gh + SUBLANES) | (r32 == gh + 2 * SUBLANES)
            slab = jnp.where(r32 == 24, hi.astype(F32),
                             jnp.where(r32 == 25, mid.astype(F32),
                                       jnp.where(r32 == 26, lo.astype(F32),
                                                 jnp.where(pick, 1.0, 0.0))))
            qaug_ref[buf, 2 * half:2 * half + AUG_ROWS, :] = slab.astype(BF16)

    def reset_state():
        acc_ref[...] = jnp.zeros_like(acc_ref)

    sum_rows = jnp.where(lax.broadcasted_iota(jnp.int32, (FOX_SUM_ROWS, tk), 0) == 0, 1.0, 0.0).astype(BF16)

    per = tk // tq

    def col_max8(s):
        return jnp.max(s.reshape(s.shape[0] // SUBLANES, SUBLANES, tq), axis=0)

    def tile_of(qi, t):
        raw = (qi + 1) * tq - (t + 1) * tk
        return pl.multiple_of(jnp.maximum(raw, 0), tq), raw < 0

    def qk_dots(qi, k_off):
        kts = [kaug_ref[0, pl.ds(k_off, tk), p * MXU_DIM:(p + 1) * MXU_DIM] for p in range(nh // 2)]
        return [_dot(kts[h // 2], qaug_ref[(qi % 2) * nh + h]) for h in range(nh)]

    def store_scores(scores):
        for h in range(nh):
            s_ref[h] = scores[h]
            for g in range(per):
                mx_ref[per * h + g] = col_max8(scores[h][g * tq:(g + 1) * tq])

    def fill_groups(d):
        for g in range(1, per):
            @pl.when(d < g)
            def _():
                for h in range(nh):
                    s_ref[h, g * tq:(g + 1) * tq, :] = jnp.full((tq, tq), NEG_BIG, F32)
                    mx_ref[per * h + g] = jnp.full((SUBLANES, tq), NEG_BIG, F32)

    def mask_stage(d):
        d0 = pl.multiple_of(d * tq, tq)
        kr = lax.broadcasted_iota(jnp.int32, (tq, tq), 0)
        qc = lax.broadcasted_iota(jnp.int32, (tq, tq), 1)
        for h in range(nh):
            tri = jnp.where(kr <= qc, s_ref[h, pl.ds(d0, tq), :], NEG_BIG)
            s_ref[h, pl.ds(d0, tq), :] = tri
            mx_ref[per * h + d] = col_max8(tri)
        fill_groups(d)

    def prep_stage(qi, k_off, first):
        i_blk = qi * tq // INPROJ_TM
        for h in range(nh):
            c_i = crep_ref[0, 0, pl.ds(h * n_blk + i_blk, 1), :]
            big_c = [c_i - crep_ref[0, 0, pl.ds(h * n_blk + (k_off + g * tq) // INPROJ_TM, 1), :]
                     for g in range(per)]
            cand = jnp.max(mx_ref[per * h], axis=0, keepdims=True) + big_c[0]
            for g in range(1, per):
                cand = jnp.maximum(cand, jnp.max(mx_ref[per * h + g], axis=0, keepdims=True) + big_c[g])
            m_prev = jnp.where(first, NEG_BIG, m_ref[h])
            m_new = jnp.maximum(m_prev, cand)
            mnext_ref[h] = m_new
            for g in range(per):
                shift_ref[per * h + g] = m_new - big_c[g]
            alpha_ref[h] = jnp.exp2(m_prev - m_new)

    def exp_stage():
        for h in range(nh):
            for g in range(per):
                rows = slice(g * tq, (g + 1) * tq)
                p_ref[h, rows, :] = jnp.exp2(s_ref[h, rows, :] - shift_ref[per * h + g]).astype(BF16)

    def pv_stage(k_off):
        for h in range(nh):
            vT = jnp.concatenate([vT_ref[0, h * half:(h + 1) * half, pl.ds(k_off, tk)], sum_rows], axis=0)
            acc_ref[h] = alpha_ref[h] * acc_ref[h] + _dot(vT, p_ref[h])
            m_ref[h] = mnext_ref[h]

    def finalize(qi):
        outs = []
        for h in range(nh):
            acc = acc_ref[h]
            outs.append(acc[:half] / acc[half:half + 1])
        oT = jnp.concatenate(outs, axis=0)
        o_ref[0, pl.ds(pl.multiple_of(qi * tq, tq), tq), :] = oT.T.astype(o_ref.dtype)

    def fix_masks(t, clamped):
        @pl.when(t == 0)
        def _():
            mask_stage(jnp.where(clamped, 0, per - 1))

        @pl.when((t > 0) & clamped)
        def _():
            fill_groups(0)

    zero = jnp.int32(0)
    qaug_ref[...] = jnp.zeros_like(qaug_ref)
    build_qaug(zero)
    build_qaug(zero + 1)
    reset_state()
    k0, cl0 = tile_of(zero, zero)
    store_scores(qk_dots(zero, k0))
    fix_masks(zero, cl0)
    prep_stage(zero, k0, True)

    def body(it, carry):
        qi, t = carry
        last = t == 0
        qi2 = jnp.where(last, qi + 1, qi)
        t2 = jnp.where(last, ntiles_ref[g * nq + jnp.minimum(qi + 1, nq - 1)] - 1, t - 1)
        k_cur, _ = tile_of(qi, t)
        k_nxt, cl_nxt = tile_of(qi2, t2)
        exp_stage()
        nxt = qk_dots(qi2, k_nxt)
        pv_stage(k_cur)
        store_scores(nxt)
        prep_stage(qi2, k_nxt, last)

        @pl.when((t2 == 0) | cl_nxt)
        def _():
            fix_masks(t2, cl_nxt)
            prep_stage(qi2, k_nxt, last)

        @pl.when(last)
        def _():
            finalize(qi)
            reset_state()

            @pl.when(qi + 2 < nq)
            def _():
                build_qaug(qi + 2)

        return qi2, t2

    qi, t = lax.fori_loop(0, nitems_ref[g] - 1, body, (jnp.int32(0), jnp.int32(0)))
    exp_stage()
    pv_stage(tile_of(qi, t)[0])
    finalize(qi)


def _prune_plan(stats, drel, nq):
    B, n_blk = stats.shape[:2]
    per = FOX_TK // FOX_TQ
    c = stats[:, :, 0, :FOX_HEADS]
    qn = jnp.sqrt(stats[:, :, 1, :FOX_HEADS])
    kn = jnp.sqrt(stats[:, :, 2, :FOX_HEADS])
    d_in = jnp.transpose(drel[:, :, FOX_TQ - 1::FOX_TQ], (0, 2, 1))
    d_end = jnp.repeat(c, per, axis=1) + d_in
    d_before = jnp.concatenate([jnp.zeros_like(d_end[:, :1]), d_end[:, :-1]], axis=1)
    kn_q = jnp.repeat(kn, per, axis=1)
    qn_q = jnp.repeat(qn, per, axis=1)
    kn_prev = jnp.concatenate([jnp.zeros_like(kn_q[:, :1]), kn_q[:, :-1]], axis=1)
    kn_tile = jnp.maximum(kn_q, kn_prev)
    bound = (d_before[:, :, None, :] - d_end[:, None, :, :]
             + NORM_SLACK * qn_q[:, :, None, :] * (kn_tile[:, None, :, :] + kn_q[:, :, None, :]))
    back = np.arange(nq)[:, None] - np.arange(nq)[None, :]
    is_tile = (back > 0) & (back % per == 0)
    t_of = np.where(is_tile, back // per, 0).astype(np.int32)
    needed = is_tile[None, :, :, None] & ~(bound <= PRUNE_LOG2)
    nh = FOX_GROUP_HEADS
    needed_g = jnp.any(needed.reshape(B, nq, nq, FOX_HEADS // nh, nh), axis=-1)
    n_tiles = 1 + jnp.max(jnp.where(needed_g, t_of[None, :, :, None], 0), axis=2)
    n_tiles = jnp.transpose(n_tiles, (0, 2, 1))
    n_items = jnp.sum(n_tiles, axis=2)
    return n_tiles.reshape(-1).astype(jnp.int32), n_items.reshape(-1).astype(jnp.int32)


def _fox_attention(fqT, kaug, fvT, drel, stats):
    B, _, S = fqT.shape
    tq, tk = FOX_TQ, FOX_TK
    n_blk = S // INPROJ_TM
    nh = FOX_GROUP_HEADS
    n_groups = FOX_HEADS // nh
    c = stats[:, :, 0, :FOX_HEADS]
    c = jnp.transpose(c, (0, 2, 1)).reshape(B, n_groups, nh * n_blk, 1)
    crep = jnp.broadcast_to(c, (B, n_groups, nh * n_blk, tq))
    n_tiles, n_items = _prune_plan(stats, drel, S // tq)
    grid_spec = pltpu.PrefetchScalarGridSpec(
        num_scalar_prefetch=2,
        grid=(B, n_groups),
        in_specs=[
            pl.BlockSpec((1, nh * FOX_HEAD_DIM, S), lambda b, p, js, ni: (b, p, 0)),
            pl.BlockSpec((1, S, nh // 2 * MXU_DIM), lambda b, p, js, ni: (b, 0, p)),
            pl.BlockSpec((1, nh * FOX_HEAD_DIM, S), lambda b, p, js, ni: (b, p, 0)),
            pl.BlockSpec((1, FOX_HEADS, S), lambda b, p, js, ni: (b, 0, 0)),
            pl.BlockSpec((1, 1, nh * n_blk, tq), lambda b, p, js, ni: (b, p, 0, 0)),
        ],
        out_specs=pl.BlockSpec((1, S, nh * FOX_HEAD_DIM), lambda b, p, js, ni: (b, 0, p)),
        scratch_shapes=[
            pltpu.VMEM((2 * nh, MXU_DIM, tq), BF16),
            pltpu.VMEM((nh, tk, tq), F32),
            pltpu.VMEM((nh, tk, tq), BF16),
            pltpu.VMEM((nh * (tk // tq), SUBLANES, tq), F32),
            pltpu.VMEM((nh, FOX_HEAD_DIM + FOX_SUM_ROWS, tq), F32),
            pltpu.VMEM((nh, 1, tq), F32),
            pltpu.VMEM((nh, 1, tq), F32),
            pltpu.VMEM((nh * (tk // tq), 1, tq), F32),
            pltpu.VMEM((nh, 1, tq), F32),
        ])
    return pl.pallas_call(
        _fox_kernel,
        grid_spec=grid_spec,
        out_shape=jax.ShapeDtypeStruct((B, S, FOX_WIDTH), BF16),
        compiler_params=pltpu.CompilerParams(
            dimension_semantics=("arbitrary", "arbitrary"),
            vmem_limit_bytes=VMEM_LIMIT),
        name="fox_attn",
    )(n_tiles, n_items, fqT, kaug, fvT, drel, crep)


def _gla_kernel(q_ref, kT_ref, v_ref, gate_ref, sm_ref, wa_ref, ba_ref, o_ref, state_ref):
    nb, ts = q_ref.shape[0], q_ref.shape[1]
    C, P = GLA_CHUNK, GLA_PAIR
    dk, dv, H = GLA_KEY_DIM, GLA_VAL_DIM, GLA_HEADS
    KW = GLA_KEY_WIDTH

    @pl.when(pl.program_id(0) == 0)
    def _():
        state_ref[...] = jnp.zeros_like(state_ref)

    la_all = [_log_sigmoid(_dot(sm_ref[bb].astype(BF16), wa_ref[...]) + ba_ref[...])
              * (1.0 / GLA_GATE_TEMP) for bb in range(nb)]

    ri = lax.broadcasted_iota(jnp.int32, (P, P), 0)
    ci = lax.broadcasted_iota(jnp.int32, (P, P), 1)
    tril2 = jnp.where(((ri < C) == (ci < C)) & (ci <= ri), 1.0, 0.0).astype(BF16)

    rs = lax.broadcasted_iota(jnp.int32, (2 * H * C, P), 0)
    cs = lax.broadcasted_iota(jnp.int32, (2 * H * C, P), 1)
    causal = ((rs >= H * C) == (cs >= C)) & ((cs & (C - 1)) <= (rs & (C - 1)))

    lane_q = lax.broadcasted_iota(jnp.int32, (C, KW), 1)
    lane_t = lax.broadcasted_iota(jnp.int32, (KW, P), 1)
    first = lane_t < C

    per_b = ts // P
    n_slabs = nb * per_b
    slab_b = [k // per_b for k in range(n_slabs)]
    slab_rows = [slice((k % per_b) * P, (k % per_b + 1) * P) for k in range(n_slabs)]
    bs = [_tri_left(tril2, la_all[slab_b[k]][slab_rows[k]]) for k in range(n_slabs)]

    qms, scores, kv0s, kv1s, dec0s, dec1s, vss = [], [], [], [], [], [], []
    for pi in range(n_slabs):
        bb, rows = slab_b[pi], slab_rows[pi]
        b = bs[pi]
        bT = b.T
        bl0 = bT[:, C - 1:C]
        bl1 = bT[:, P - 1:P]
        q_dec = q_ref[bb, rows, :] * jnp.exp(b)
        kT = kT_ref[bb, :, rows]
        k_invT = (kT * jnp.exp(-bT)).astype(BF16)
        k_teT = kT * jnp.exp(jnp.where(first, bl0, bl1) - bT)
        zt = jnp.zeros_like(k_teT)
        k_te0 = jnp.where(first, k_teT, zt).astype(BF16)
        k_te1 = jnp.where(first, zt, k_teT).astype(BF16)
        dec0 = jnp.exp(bl0)
        dec1 = jnp.exp(bl1)

        vs = [v_ref[bb, rows, h * dv:(h + 1) * dv] for h in range(H)]
        kv = [_dot(jnp.concatenate([k_te0[h * dk:(h + 1) * dk, :], k_te1[h * dk:(h + 1) * dk, :]], axis=0),
                   vs[h]) for h in range(H)]
        kv0s.append(jnp.concatenate([kv[h][:dk] for h in range(H)], axis=0))
        kv1s.append(jnp.concatenate([kv[h][dk:] for h in range(H)], axis=0))
        dec0s.append(dec0)
        dec1s.append(dec1)
        vss.append(vs)

        zq = jnp.zeros((C, KW), F32)
        qm = [jnp.concatenate(
            [jnp.where((lane_q >= h * dk) & (lane_q < (h + 1) * dk), q_dec[c * C:(c + 1) * C], zq)
             for h in range(H)], axis=0).astype(BF16) for c in range(2)]
        a = _dot(jnp.concatenate(qm, axis=0), k_invT)
        scores.append(jnp.where(causal, a, 0.0).astype(BF16))
        qms.append(qm)

    o_inters = []
    for pi in range(n_slabs):
        if pi % per_b == 0:
            s = state_ref[slab_b[pi]]
        s1 = dec0s[pi] * s + kv0s[pi]
        o_inters.append([_dot(qms[pi][0], s.astype(BF16)), _dot(qms[pi][1], s1.astype(BF16))])
        s = dec1s[pi] * s1 + kv1s[pi]
        if pi % per_b == per_b - 1:
            state_ref[slab_b[pi]] = s

    for pi in range(n_slabs):
        bb, rows = slab_b[pi], slab_rows[pi]
        a, o_inter, vs = scores[pi], o_inters[pi], vss[pi]
        for h in range(H):
            hv = slice(h * dv, (h + 1) * dv)
            a_h = jnp.concatenate([a[(c * H + h) * C:(c * H + h + 1) * C] for c in range(2)], axis=0)
            o = _dot(a_h, vs[h]) + jnp.concatenate(
                [o_inter[c][h * C:(h + 1) * C] for c in range(2)], axis=0)
            o = o * lax.rsqrt(jnp.mean(o * o, axis=-1, keepdims=True) + RMS_EPS)
            o_ref[bb, rows, hv] = (o * gate_ref[bb, rows, hv]).astype(o_ref.dtype)


def _gla_attention(gq, gkT, gv, gate, small, w_a2_pad, b_a):
    B, S, _ = gq.shape
    ts = GLA_TS
    tok = lambda w: pl.BlockSpec((B, ts, w), lambda s: (0, s, 0))
    full = lambda r, c: pl.BlockSpec((r, c), lambda s: (0, 0))
    return pl.pallas_call(
        _gla_kernel,
        grid=(S // ts,),
        in_specs=[
            tok(GLA_KEY_WIDTH), pl.BlockSpec((B, GLA_KEY_WIDTH, ts), lambda s: (0, 0, s)),
            tok(GLA_WIDTH), tok(GLA_WIDTH), tok(SMALL_W),
            full(SMALL_W, GLA_KEY_WIDTH), full(1, GLA_KEY_WIDTH),
        ],
        out_specs=tok(GLA_WIDTH),
        out_shape=jax.ShapeDtypeStruct((B, S, GLA_WIDTH), BF16),
        scratch_shapes=[pltpu.VMEM((B, GLA_KEY_WIDTH, GLA_VAL_DIM), F32)],
        compiler_params=pltpu.CompilerParams(
            dimension_semantics=("arbitrary",), vmem_limit_bytes=VMEM_LIMIT),
        name="gla_attn",
    )(gq, gkT, gv, gate, small, w_a2_pad, b_a)


def _layer_norm(z, g, b):
    mu = jnp.mean(z, axis=-1, keepdims=True)
    d = z - mu
    var = jnp.mean(d * d, axis=-1, keepdims=True)
    return d * lax.rsqrt(var + LN_EPS) * g + b


def _tail_kernel(x_ref, fox_ref, gla_ref, mod_ref, wof_ref, wog_ref, ln1g_ref, ln1b_ref,
                 wg_ref, wu_ref, wd_ref, ln2g_ref, ln2b_ref, o_ref):
    m = mod_ref[0]
    tm = x_ref.shape[0]
    halves = [slice(0, tm // 2), slice(tm // 2, tm)]
    ys = [_dot(fox_ref[r, :], wof_ref[...]) + _dot(gla_ref[r, :], wog_ref[...]) for r in halves]
    x1s, gs, ups = [], [], []
    for r, y in zip(halves, ys):
        x1 = _layer_norm(DEEPNORM_ALPHA * x_ref[r, :] + (1.0 + m[2:3]) * y,
                         ln1g_ref[...], ln1b_ref[...])
        u2 = (x1 * (1.0 + m[4:5]) + m[3:4]).astype(BF16)
        x1s.append(x1)
        gs.append(_dot(u2, wg_ref[...]))
        ups.append(_dot(u2, wu_ref[...]))
    y2s = []
    for g, up in zip(gs, ups):
        h = (g * (1.0 / (1.0 + jnp.exp(-g))) * up).astype(BF16)
        y2s.append(_dot(h, wd_ref[...]))
    for r, x1, y2 in zip(halves, x1s, y2s):
        o_ref[r, :] = _layer_norm(DEEPNORM_ALPHA * x1 + (1.0 + m[5:6]) * y2,
                                  ln2g_ref[...], ln2b_ref[...])


def _tail(x2d, fox2d, gla2d, mod, wo, ln1g, ln1b, wg, wu, wd, ln2g, ln2b, steps_per_batch):
    T, D = x2d.shape
    tm = TAIL_TM
    const = lambda r, c: pl.BlockSpec((r, c), lambda i: (0, 0), pipeline_mode=pl.Buffered(1))
    tok = lambda w: pl.BlockSpec((tm, w), lambda i: (i, 0))
    return pl.pallas_call(
        _tail_kernel,
        grid=(T // tm,),
        in_specs=[
            tok(D), tok(FOX_WIDTH), tok(GLA_WIDTH),
            pl.BlockSpec((1, N_MOD, D), lambda i: (i // steps_per_batch, 0, 0)),
            pl.BlockSpec((FOX_WIDTH, D), lambda i: (0, 0), pipeline_mode=pl.Buffered(1)),
            pl.BlockSpec((GLA_WIDTH, D), lambda i: (1, 0), pipeline_mode=pl.Buffered(1)),
            const(1, D), const(1, D),
            const(D, D_FF), const(D, D_FF), const(D_FF, D), const(1, D), const(1, D),
        ],
        out_specs=tok(D),
        out_shape=jax.ShapeDtypeStruct((T, D), F32),
        compiler_params=pltpu.CompilerParams(
            dimension_semantics=("arbitrary",), vmem_limit_bytes=VMEM_LIMIT),
        name="tail",
    )(x2d, fox2d, gla2d, mod, wo, wo, ln1g, ln1b, wg, wu, wd, ln2g, ln2b)


def kernel(x, c, w_c, b_c, w_in, b_f, w_a2, b_a, g_gla, w_o, ln1_g, ln1_b,
           w_gate, w_up, w_down, ln2_g, ln2_b):
    B, S, D = x.shape
    assert D == D_MODEL and S % INPROJ_TM == 0 and INPROJ_TM == FOX_TK and FOX_TK % FOX_TQ == 0

    b_f_pad = jnp.pad(b_f.reshape(1, FOX_HEADS), ((0, 0), (0, SMALL_W - FOX_HEADS)))
    w_a2_pad = jnp.pad(w_a2, ((FOX_HEADS, SMALL_W - FOX_HEADS - GLA_GATE_RANK), (0, 0))).astype(BF16)

    mod, w_all = _modulation(c, w_c, b_c, w_in)

    fqT, kaug, fvT, gq, gkT, gv, gate, small, drel, stats, wg16, wu16, wd16, wo16 = _in_projection(
        x, mod, w_all, b_f_pad, g_gla.reshape(1, -1), w_gate, w_up, w_down, w_o)
    fox = _fox_attention(fqT, kaug, fvT, drel, stats)
    gla = _gla_attention(gq, gkT, gv, gate, small, w_a2_pad, b_a.reshape(1, -1))

    T = B * S
    out = _tail(x.reshape(T, D), fox.reshape(T, FOX_WIDTH), gla.reshape(T, GLA_WIDTH), mod,
                wo16,
                ln1_g.reshape(1, D), ln1_b.reshape(1, D),
                wg16, wu16, wd16,
                ln2_g.reshape(1, D), ln2_b.reshape(1, D), S // TAIL_TM)
    return out.reshape(B, S, D)
```

```python
import functools
import math

import numpy as np
import jax
import jax.numpy as jnp
from jax import lax
from jax.experimental import pallas as pl
from jax.experimental.pallas import tpu as pltpu

F32 = jnp.float32
BF16 = jnp.bfloat16

D_MODEL = 1024
FOX_WIDTH = 512
FOX_HEAD_DIM = 64
FOX_HEADS = 8
FOX_PAIRS = FOX_HEADS // 2
GLA_WIDTH = 512
GLA_HEADS = 4
GLA_VAL_DIM = 128
GLA_KEY_DIM = 64
GLA_KEY_WIDTH = 256
GLA_GATE_RANK = 16
GLA_GATE_TEMP = 16.0
GLA_CHUNK = 64
D_FF = 2816
LN_EPS = 1e-5
RMS_EPS = 1e-6
N_MOD = 6
DEEPNORM_ALPHA = 2.0 ** 0.25
IN_SPLITS = (FOX_WIDTH, FOX_WIDTH, FOX_WIDTH, FOX_HEADS, GLA_KEY_WIDTH, GLA_KEY_WIDTH,
             GLA_WIDTH, GLA_GATE_RANK, GLA_WIDTH)
LOG2E = math.log2(math.e)

LANES = 128
SUBLANES = 8
MXU_DIM = 256
SMALL_W = LANES
VMEM_LIMIT = 56 * 1024 * 1024

MOD_TN = 768
INPROJ_TM = 512
FOX_TQ = 256
FOX_GROUP_HEADS = 4
FOX_SUM_ROWS = 16
FOX_TK = 512
GLA_TS = 512
GLA_PAIR = 2 * GLA_CHUNK
TAIL_TM = 512

NEG_BIG = -1e30
PRUNE_LOG2 = -160.0
NORM_SLACK = 1.02

AUG_ROWS = 32


def _log_sigmoid(z):
    return jnp.minimum(z, 0.0) - jnp.log(1.0 + jnp.exp(-jnp.abs(z)))


def _split3(a):
    hi = a.astype(BF16)
    r1 = a - hi.astype(F32)
    mid = r1.astype(BF16)
    lo = (r1 - mid.astype(F32)).astype(BF16)
    return hi, mid, lo


def _dot(a, b):
    return jnp.dot(a, b, preferred_element_type=F32)


def _tri_left(tri, a):
    hi, mid, lo = _split3(a)
    return _dot(tri, hi) + _dot(tri, mid) + _dot(tri, lo)


def _tri_right(a, tri):
    hi, mid, lo = _split3(a)
    return _dot(hi, tri) + _dot(mid, tri) + _dot(lo, tri)


def _regroup_blocks():
    blocks = []
    for dst, src, width in _W_GROUPS:
        blocks += [(dst + c0, src + c0) for c0 in range(0, width, MXU_DIM)]
    blocks.append((_C_SM, None))
    return blocks


def _mod_kernel(cT_ref, w_ref, b_ref, winT_ref, o_ref, wall_ref, *, n_rows):
    w = w_ref[...]
    outs = [jnp.sum(w * cT_ref[:, b:b + 1], axis=0, keepdims=True) for b in range(n_rows)]
    outs.append(jnp.zeros((o_ref.shape[0] - n_rows, w.shape[1]), F32))
    o_ref[...] = jnp.concatenate(outs, axis=0) + b_ref[...]

    j = pl.program_id(0)
    n_steps = pl.num_programs(0)
    for k, (dst, src) in enumerate(_regroup_blocks()):
        @pl.when(j == k % n_steps)
        def _():
            if src is None:
                blk = jnp.concatenate(
                    [winT_ref[_SRC_FF:_SRC_FF + FOX_HEADS, :], winT_ref[_SRC_GA:_SRC_GA + GLA_GATE_RANK, :],
                     jnp.zeros((MXU_DIM - FOX_HEADS - GLA_GATE_RANK, winT_ref.shape[1]), F32)], axis=0)
            else:
                blk = winT_ref[src:src + MXU_DIM, :]
            wall_ref[:, dst:dst + MXU_DIM] = blk.T.astype(BF16)


def _modulation(c, w_c, b_c, w_in):
    B, D = c.shape
    N = w_c.shape[1]
    rows = SUBLANES
    n_steps = N // MOD_TN
    cT_pad = jnp.pad(c.T, ((0, 0), (0, LANES - B)))
    w_inT = w_in.T
    out, w_all = pl.pallas_call(
        functools.partial(_mod_kernel, n_rows=B),
        grid=(n_steps,),
        in_specs=[
            pl.BlockSpec((D, LANES), lambda j: (0, 0)),
            pl.BlockSpec((D, MOD_TN), lambda j: (0, j)),
            pl.BlockSpec((1, MOD_TN), lambda j: (0, j)),
            pl.BlockSpec(w_inT.shape, lambda j: (0, 0), pipeline_mode=pl.Buffered(1)),
        ],
        out_specs=[pl.BlockSpec((rows, MOD_TN), lambda j: (0, j)),
                   pl.BlockSpec((D, _W_COLS), lambda j: (0, 0))],
        out_shape=[jax.ShapeDtypeStruct((rows, N), F32),
                   jax.ShapeDtypeStruct((D, _W_COLS), BF16)],
        compiler_params=pltpu.CompilerParams(
            dimension_semantics=("arbitrary",), vmem_limit_bytes=VMEM_LIMIT),
        name="adaln_mod",
    )(cT_pad, w_c, b_c.reshape(1, N), w_inT)
    return out[:B].reshape(B, N_MOD, D), w_all


_C_FQ, _C_FK, _C_FV = 0, 512, 1024
_C_GQ, _C_SM, _C_GK, _C_GV, _C_GR = 1536, 1792, 2048, 2304, 2816
_W_COLS = _C_GR + GLA_WIDTH
_SRC = [sum(IN_SPLITS[:i]) for i in range(len(IN_SPLITS))]
_SRC_FF, _SRC_GA = _SRC[3], _SRC[7]
_W_GROUPS = ((_C_FQ, _SRC[0], FOX_WIDTH), (_C_FK, _SRC[1], FOX_WIDTH), (_C_FV, _SRC[2], FOX_WIDTH),
             (_C_GQ, _SRC[4], GLA_KEY_WIDTH), (_C_GK, _SRC[5], GLA_KEY_WIDTH),
             (_C_GV, _SRC[6], GLA_WIDTH), (_C_GR, _SRC[8], GLA_WIDTH))
_KAUG_W = FOX_PAIRS * MXU_DIM


def _head_indicator():
    ind = np.zeros((FOX_WIDTH, MXU_DIM), np.float32)
    for h in range(FOX_HEADS):
        ind[h * FOX_HEAD_DIM:(h + 1) * FOX_HEAD_DIM, h] = 1.0
    return ind


def _inproj_kernel(x_ref, mod_ref, w_ref, bf_ref, ind_ref, gg_ref, wg32_ref, wu32_ref, wd32_ref, wo32_ref,
                   fqT_ref, kaug_ref, fvT_ref, gq_ref, gk_ref, gv_ref, gr_ref, sm_ref,
                   drel_ref, stats_ref, wg16_ref, wu16_ref, wd16_ref, wo16_ref, carry_ref):
    tm = x_ref.shape[1]

    wg16_ref[...] = wg32_ref[...].astype(BF16)
    wu16_ref[...] = wu32_ref[...].astype(BF16)
    wd16_ref[...] = wd32_ref[...].astype(BF16)
    wo16_ref[...] = wo32_ref[...].astype(BF16)

    @pl.when(pl.program_id(1) == 0)
    def _():
        carry_ref[...] = jnp.zeros_like(carry_ref)

    m = mod_ref[0]
    u = (x_ref[0] * (1.0 + m[1:2]) + m[0:1]).astype(BF16)

    def proj(lo, width):
        return _dot(u, w_ref[:, lo:lo + width])

    gq_small = proj(_C_GQ, 2 * GLA_KEY_WIDTH)
    gq_ref[0] = gq_small[:, :GLA_KEY_WIDTH] * (GLA_KEY_DIM ** -0.5)
    small = gq_small[:, GLA_KEY_WIDTH:GLA_KEY_WIDTH + SMALL_W]
    sm_ref[0] = small

    fq = proj(_C_FQ, FOX_WIDTH) * (FOX_HEAD_DIM ** -0.5 * LOG2E)
    fqT_ref[0] = fq.T.astype(BF16)
    fk32 = proj(_C_FK, FOX_WIDTH)
    fk = fk32.astype(BF16)
    qn2 = jnp.max(_dot((fq * fq).astype(BF16), ind_ref[...])[:, :SMALL_W], axis=0, keepdims=True)
    kn2 = jnp.max(_dot((fk32 * fk32).astype(BF16), ind_ref[...])[:, :SMALL_W], axis=0, keepdims=True)

    zT = (small + bf_ref[...]).T[:FOX_HEADS, :]
    lfT = _log_sigmoid(zT) * LOG2E
    r = lax.broadcasted_iota(jnp.int32, (tm, tm), 0)
    c = lax.broadcasted_iota(jnp.int32, (tm, tm), 1)
    triu = jnp.where(r <= c, 1.0, 0.0).astype(BF16)
    drelT = _tri_right(lfT, triu)
    drel_ref[0] = drelT

    c_before = carry_ref[...]
    srow = lax.broadcasted_iota(jnp.int32, (SUBLANES, SMALL_W), 0)
    slane = lax.broadcasted_iota(jnp.int32, (SUBLANES, SMALL_W), 1)
    total = jnp.sum(jnp.where(srow == slane, drelT[:, tm - 1:tm], 0.0), axis=0, keepdims=True)
    carry_ref[...] = c_before + total

    hi, mid, lo = _split3(drelT)
    ones3 = jnp.where(lax.broadcasted_iota(jnp.int32, (SUBLANES, tm), 0) < 3, 1.0, 0.0)
    slabT = jnp.concatenate(
        [-hi.astype(F32), -mid.astype(F32), -lo.astype(F32), ones3,
         jnp.zeros((LANES - 4 * SUBLANES, tm), F32)], axis=0)
    slab = slabT.T.astype(BF16)

    fvT_ref[0] = proj(_C_FV, FOX_WIDTH).T.astype(BF16)
    gv_ref[0] = proj(_C_GV, GLA_WIDTH).astype(BF16)
    gr = proj(_C_GR, GLA_WIDTH)
    gr_ref[0] = gg_ref[...] * (gr * (1.0 / (1.0 + jnp.exp(-gr))))
    gk_ref[0] = proj(_C_GK, GLA_KEY_WIDTH).T

    stats_ref[0, 0] = jnp.where(srow == 0, c_before,
                                jnp.where(srow == 1, qn2, jnp.where(srow == 2, kn2, 0.0)))
    for p in range(FOX_PAIRS):
        kaug_ref[0, :, p * MXU_DIM:p * MXU_DIM + LANES] = fk[:, p * LANES:(p + 1) * LANES]
        kaug_ref[0, :, p * MXU_DIM + LANES:(p + 1) * MXU_DIM] = slab


def _in_projection(x, mod, w_all, b_f_pad, g_gla, w_gate, w_up, w_down, w_o):
    B, S, D = x.shape
    tm = INPROJ_TM
    n_steps = B * (S // tm)
    steps_b = S // tm
    up_rows = D // n_steps
    down_rows = 2 * D_FF // n_steps
    assert D % n_steps == 0 and up_rows % 16 == 0 and (2 * D_FF) % n_steps == 0 and down_rows % 16 == 0
    up_spec = pl.BlockSpec((up_rows, D_FF), lambda b, s: (b * steps_b + s, 0))
    down_spec = pl.BlockSpec((down_rows, D), lambda b, s: ((b * steps_b + s) // 2, 0))
    wo_rows = w_o.shape[0] // n_steps
    assert w_o.shape[0] % n_steps == 0 and wo_rows % 16 == 0
    wo_spec = pl.BlockSpec((wo_rows, D), lambda b, s: (b * steps_b + s, 0))
    ind = _head_indicator()
    tok = lambda w: pl.BlockSpec((1, tm, w), lambda b, s: (b, s, 0))
    tokT = lambda h: pl.BlockSpec((1, h, tm), lambda b, s: (b, 0, s))
    outs = pl.pallas_call(
        _inproj_kernel,
        grid=(B, S // tm),
        in_specs=[
            tok(D),
            pl.BlockSpec((1, N_MOD, D), lambda b, s: (b, 0, 0)),
            pl.BlockSpec((D, _W_COLS), lambda b, s: (0, 0), pipeline_mode=pl.Buffered(1)),
            pl.BlockSpec((1, SMALL_W), lambda b, s: (0, 0)),
            pl.BlockSpec(ind.shape, lambda b, s: (0, 0)),
            pl.BlockSpec((1, GLA_WIDTH), lambda b, s: (0, 0)),
            up_spec, up_spec, down_spec, wo_spec,
        ],
        out_specs=[
            tokT(FOX_WIDTH), tok(_KAUG_W), tokT(FOX_WIDTH),
            tok(GLA_KEY_WIDTH), tokT(GLA_KEY_WIDTH), tok(GLA_WIDTH), tok(GLA_WIDTH),
            tok(SMALL_W),
            tokT(FOX_HEADS),
            pl.BlockSpec((1, 1, SUBLANES, SMALL_W), lambda b, s: (b, s, 0, 0)),
            up_spec, up_spec, down_spec, wo_spec,
        ],
        out_shape=[
            jax.ShapeDtypeStruct((B, FOX_WIDTH, S), BF16),
            jax.ShapeDtypeStruct((B, S, _KAUG_W), BF16),
            jax.ShapeDtypeStruct((B, FOX_WIDTH, S), BF16),
            jax.ShapeDtypeStruct((B, S, GLA_KEY_WIDTH), F32),
            jax.ShapeDtypeStruct((B, GLA_KEY_WIDTH, S), F32),
            jax.ShapeDtypeStruct((B, S, GLA_WIDTH), BF16),
            jax.ShapeDtypeStruct((B, S, GLA_WIDTH), F32),
            jax.ShapeDtypeStruct((B, S, SMALL_W), F32),
            jax.ShapeDtypeStruct((B, FOX_HEADS, S), F32),
            jax.ShapeDtypeStruct((B, S // tm, SUBLANES, SMALL_W), F32),
            jax.ShapeDtypeStruct(w_gate.shape, BF16),
            jax.ShapeDtypeStruct(w_up.shape, BF16),
            jax.ShapeDtypeStruct(w_down.shape, BF16),
            jax.ShapeDtypeStruct(w_o.shape, BF16),
        ],
        scratch_shapes=[pltpu.VMEM((1, SMALL_W), F32)],
        compiler_params=pltpu.CompilerParams(
            dimension_semantics=("arbitrary", "arbitrary"), vmem_limit_bytes=VMEM_LIMIT),
        name="in_proj",
    )(x, mod, w_all, b_f_pad, jnp.asarray(ind, BF16), g_gla, w_gate, w_up, w_down, w_o)
    return outs


def _fox_kernel(ntiles_ref, nitems_ref, qT_ref, kaug_ref, vT_ref, drel_ref, crep_ref, o_ref,
                qaug_ref, s_ref, p_ref, mx_ref, acc_ref, m_ref, mnext_ref, shift_ref, alpha_ref):
    tq, tk = FOX_TQ, FOX_TK
    half = FOX_HEAD_DIM
    S = kaug_ref.shape[1]
    nq = S // tq
    nh = FOX_GROUP_HEADS
    n_blk = crep_ref.shape[2] // nh
    g = pl.program_id(0) * pl.num_programs(1) + pl.program_id(1)

    def build_qaug(qi):
        q0 = pl.multiple_of(qi * tq, tq)
        dr_all = drel_ref[0, :, pl.ds(q0, tq)]
        dr = dr_all[0:nh]
        for grp in range(1, dr_all.shape[0] // nh):
            dr = jnp.where(pl.program_id(1) == grp, dr_all[grp * nh:(grp + 1) * nh], dr)
        row = lax.broadcasted_iota(jnp.int32, (2 * half, tq), 0)
        r32 = lax.broadcasted_iota(jnp.int32, (AUG_ROWS, tq), 0)
        for h in range(nh):
            buf = (qi % 2) * nh + h
            qT = qT_ref[0, (h // 2) * 2 * half:(h // 2 + 1) * 2 * half, pl.ds(q0, tq)]
            own = (row < half) if h % 2 == 0 else (row >= half)
            qaug_ref[buf, 0:2 * half, :] = jnp.where(own, qT, jnp.zeros_like(qT))
            hi, mid, lo = _split3(dr[h:h + 1])
            gh = nh * pl.program_id(1) + h
            pick = (r32 == gh) | (r32 == gh + SUBLANES) | (r32 == gh + 2 * SUBLANES)
            slab = jnp.where(r32 == 24, hi.astype(F32),
                             jnp.where(r32 == 25, mid.astype(F32),
                                       jnp.where(r32 == 26, lo.astype(F32),
                                                 jnp.where(pick, 1.0, 0.0))))
            qaug_ref[buf, 2 * half:2 * half + AUG_ROWS, :] = slab.astype(BF16)

    def reset_state():
        acc_ref[...] = jnp.zeros_like(acc_ref)

    sum_rows = jnp.where(lax.broadcasted_iota(jnp.int32, (FOX_SUM_ROWS, tk), 0) == 0, 1.0, 0.0).astype(BF16)

    per = tk // tq

    def col_max8(s):
        return jnp.max(s.reshape(s.shape[0] // SUBLANES, SUBLANES, tq), axis=0)

    def tile_of(qi, t):
        raw = (qi + 1) * tq - (t + 1) * tk
        return pl.multiple_of(jnp.maximum(raw, 0), tq), raw < 0

    def qk_dots(qi, k_off):
        kts = [kaug_ref[0, pl.ds(k_off, tk), p * MXU_DIM:(p + 1) * MXU_DIM] for p in range(nh // 2)]
        return [_dot(kts[h // 2], qaug_ref[(qi % 2) * nh + h]) for h in range(nh)]

    def store_scores(scores, diag):
        kr = lax.broadcasted_iota(jnp.int32, (tq, tq), 0)
        qc = lax.broadcasted_iota(jnp.int32, (tq, tq), 1)
        keep = (kr - qc) <= jnp.where(diag, 0, tq)
        for h in range(nh):
            for g in range(per):
                blk = scores[h][g * tq:(g + 1) * tq]
                if g == per - 1:
                    blk = jnp.where(keep, blk, NEG_BIG)
                s_ref[h, g * tq:(g + 1) * tq, :] = blk
                mx_ref[per * h + g] = col_max8(blk)

    def fill_upper_groups():
        for g in range(1, per):
            for h in range(nh):
                s_ref[h, g * tq:(g + 1) * tq, :] = jnp.full((tq, tq), NEG_BIG, F32)
                mx_ref[per * h + g] = jnp.full((SUBLANES, tq), NEG_BIG, F32)

    def mask_clamped_diagonal():
        kr = lax.broadcasted_iota(jnp.int32, (tq, tq), 0)
        qc = lax.broadcasted_iota(jnp.int32, (tq, tq), 1)
        for h in range(nh):
            tri = jnp.where(kr <= qc, s_ref[h, 0:tq, :], NEG_BIG)
            s_ref[h, 0:tq, :] = tri
            mx_ref[per * h] = col_max8(tri)
        fill_upper_groups()

    def prep_stage(qi, k_off, first):
        i_blk = qi * tq // INPROJ_TM
        for h in range(nh):
            c_i = crep_ref[0, 0, pl.ds(h * n_blk + i_blk, 1), :]
            big_c = [c_i - crep_ref[0, 0, pl.ds(h * n_blk + (k_off + g * tq) // INPROJ_TM, 1), :]
                     for g in range(per)]
            cand = jnp.max(mx_ref[per * h], axis=0, keepdims=True) + big_c[0]
            for g in range(1, per):
                cand = jnp.maximum(cand, jnp.max(mx_ref[per * h + g], axis=0, keepdims=True) + big_c[g])
            m_prev = jnp.where(first, NEG_BIG, m_ref[h])
            m_new = jnp.maximum(m_prev, cand)
            mnext_ref[h] = m_new
            for g in range(per):
                shift_ref[per * h + g] = m_new - big_c[g]
            alpha_ref[h] = jnp.exp2(m_prev - m_new)

    def exp_stage():
        for h in range(nh):
            for g in range(per):
                rows = slice(g * tq, (g + 1) * tq)
                p_ref[h, rows, :] = jnp.exp2(s_ref[h, rows, :] - shift_ref[per * h + g]).astype(BF16)

    def pv_stage(k_off):
        for h in range(nh):
            vT = jnp.concatenate([vT_ref[0, h * half:(h + 1) * half, pl.ds(k_off, tk)], sum_rows], axis=0)
            acc_ref[h] = alpha_ref[h] * acc_ref[h] + _dot(vT, p_ref[h])
            m_ref[h] = mnext_ref[h]

    def finalize(qi):
        outs = []
        for h in range(nh):
            acc = acc_ref[h]
            outs.append(acc[:half] / acc[half:half + 1])
        oT = jnp.concatenate(outs, axis=0)
        o_ref[0, pl.ds(pl.multiple_of(qi * tq, tq), tq), :] = oT.T.astype(o_ref.dtype)

    def fix_clamped(t):
        @pl.when(t == 0)
        def _():
            mask_clamped_diagonal()

        @pl.when(t > 0)
        def _():
            fill_upper_groups()

    zero = jnp.int32(0)
    qaug_ref[...] = jnp.zeros_like(qaug_ref)
    build_qaug(zero)
    build_qaug(zero + 1)
    reset_state()
    k0, _ = tile_of(zero, zero)
    store_scores(qk_dots(zero, k0), False)
    mask_clamped_diagonal()
    prep_stage(zero, k0, True)

    def body(it, carry):
        qi, t = carry
        last = t == 0
        qi2 = jnp.where(last, qi + 1, qi)
        t2 = jnp.where(last, ntiles_ref[g * nq + jnp.minimum(qi + 1, nq - 1)] - 1, t - 1)
        k_cur, _ = tile_of(qi, t)
        k_nxt, cl_nxt = tile_of(qi2, t2)
        exp_stage()
        nxt = qk_dots(qi2, k_nxt)
        pv_stage(k_cur)
        store_scores(nxt, (t2 == 0) & jnp.logical_not(cl_nxt))
        prep_stage(qi2, k_nxt, last)

        @pl.when(cl_nxt)
        def _():
            fix_clamped(t2)
            prep_stage(qi2, k_nxt, last)

        @pl.when(last)
        def _():
            finalize(qi)
            reset_state()

            @pl.when(qi + 2 < nq)
            def _():
                build_qaug(qi + 2)

        return qi2, t2

    qi, t = lax.fori_loop(0, nitems_ref[g] - 1, body, (jnp.int32(0), jnp.int32(0)))
    exp_stage()
    pv_stage(tile_of(qi, t)[0])
    finalize(qi)


def _prune_plan(stats, drel, nq):
    B, n_blk = stats.shape[:2]
    per = FOX_TK // FOX_TQ
    c = stats[:, :, 0, :FOX_HEADS]
    qn = jnp.sqrt(stats[:, :, 1, :FOX_HEADS])
    kn = jnp.sqrt(stats[:, :, 2, :FOX_HEADS])
    d_in = jnp.transpose(drel[:, :, FOX_TQ - 1::FOX_TQ], (0, 2, 1))
    d_end = jnp.repeat(c, per, axis=1) + d_in
    d_before = jnp.concatenate([jnp.zeros_like(d_end[:, :1]), d_end[:, :-1]], axis=1)
    kn_q = jnp.repeat(kn, per, axis=1)
    qn_q = jnp.repeat(qn, per, axis=1)
    kn_prev = jnp.concatenate([jnp.zeros_like(kn_q[:, :1]), kn_q[:, :-1]], axis=1)
    kn_tile = jnp.maximum(kn_q, kn_prev)
    bound = (d_before[:, :, None, :] - d_end[:, None, :, :]
             + NORM_SLACK * qn_q[:, :, None, :] * (kn_tile[:, None, :, :] + kn_q[:, :, None, :]))
    back = np.arange(nq)[:, None] - np.arange(nq)[None, :]
    is_tile = (back > 0) & (back % per == 0)
    t_of = np.where(is_tile, back // per, 0).astype(np.int32)
    needed = is_tile[None, :, :, None] & ~(bound <= PRUNE_LOG2)
    nh = FOX_GROUP_HEADS
    needed_g = jnp.any(needed.reshape(B, nq, nq, FOX_HEADS // nh, nh), axis=-1)
    n_tiles = 1 + jnp.max(jnp.where(needed_g, t_of[None, :, :, None], 0), axis=2)
    n_tiles = jnp.transpose(n_tiles, (0, 2, 1))
    n_items = jnp.sum(n_tiles, axis=2)
    return n_tiles.reshape(-1).astype(jnp.int32), n_items.reshape(-1).astype(jnp.int32)


def _fox_attention(fqT, kaug, fvT, drel, stats):
    B, _, S = fqT.shape
    tq, tk = FOX_TQ, FOX_TK
    n_blk = S // INPROJ_TM
    nh = FOX_GROUP_HEADS
    n_groups = FOX_HEADS // nh
    c = stats[:, :, 0, :FOX_HEADS]
    c = jnp.transpose(c, (0, 2, 1)).reshape(B, n_groups, nh * n_blk, 1)
    crep = jnp.broadcast_to(c, (B, n_groups, nh * n_blk, tq))
    n_tiles, n_items = _prune_plan(stats, drel, S // tq)
    grid_spec = pltpu.PrefetchScalarGridSpec(
        num_scalar_prefetch=2,
        grid=(B, n_groups),
        in_specs=[
            pl.BlockSpec((1, nh * FOX_HEAD_DIM, S), lambda b, p, js, ni: (b, p, 0)),
            pl.BlockSpec((1, S, nh // 2 * MXU_DIM), lambda b, p, js, ni: (b, 0, p)),
            pl.BlockSpec((1, nh * FOX_HEAD_DIM, S), lambda b, p, js, ni: (b, p, 0)),
            pl.BlockSpec((1, FOX_HEADS, S), lambda b, p, js, ni: (b, 0, 0)),
            pl.BlockSpec((1, 1, nh * n_blk, tq), lambda b, p, js, ni: (b, p, 0, 0)),
        ],
        out_specs=pl.BlockSpec((1, S, nh * FOX_HEAD_DIM), lambda b, p, js, ni: (b, 0, p)),
        scratch_shapes=[
            pltpu.VMEM((2 * nh, MXU_DIM, tq), BF16),
            pltpu.VMEM((nh, tk, tq), F32),
            pltpu.VMEM((nh, tk, tq), BF16),
            pltpu.VMEM((nh * (tk // tq), SUBLANES, tq), F32),
            pltpu.VMEM((nh, FOX_HEAD_DIM + FOX_SUM_ROWS, tq), F32),
            pltpu.VMEM((nh, 1, tq), F32),
            pltpu.VMEM((nh, 1, tq), F32),
            pltpu.VMEM((nh * (tk // tq), 1, tq), F32),
            pltpu.VMEM((nh, 1, tq), F32),
        ])
    return pl.pallas_call(
        _fox_kernel,
        grid_spec=grid_spec,
        out_shape=jax.ShapeDtypeStruct((B, S, FOX_WIDTH), BF16),
        compiler_params=pltpu.CompilerParams(
            dimension_semantics=("arbitrary", "arbitrary"),
            vmem_limit_bytes=VMEM_LIMIT),
        name="fox_attn",
    )(n_tiles, n_items, fqT, kaug, fvT, drel, crep)


def _gla_kernel(q_ref, kT_ref, v_ref, gate_ref, sm_ref, wa_ref, ba_ref, o_ref, state_ref):
    nb, ts = q_ref.shape[0], q_ref.shape[1]
    C, P = GLA_CHUNK, GLA_PAIR
    dk, dv, H = GLA_KEY_DIM, GLA_VAL_DIM, GLA_HEADS
    KW = GLA_KEY_WIDTH

    @pl.when(pl.program_id(0) == 0)
    def _():
        state_ref[...] = jnp.zeros_like(state_ref)

    la_all = [_log_sigmoid(_dot(sm_ref[bb].astype(BF16), wa_ref[...]) + ba_ref[...])
              * (1.0 / GLA_GATE_TEMP) for bb in range(nb)]

    ri = lax.broadcasted_iota(jnp.int32, (P, P), 0)
    ci = lax.broadcasted_iota(jnp.int32, (P, P), 1)
    tril2 = jnp.where(((ri < C) == (ci < C)) & (ci <= ri), 1.0, 0.0).astype(BF16)

    rs = lax.broadcasted_iota(jnp.int32, (2 * H * C, P), 0)
    cs = lax.broadcasted_iota(jnp.int32, (2 * H * C, P), 1)
    causal = ((rs >= H * C) == (cs >= C)) & ((cs & (C - 1)) <= (rs & (C - 1)))

    lane_q = lax.broadcasted_iota(jnp.int32, (C, KW), 1)
    lane_t = lax.broadcasted_iota(jnp.int32, (KW, P), 1)
    first = lane_t < C

    per_b = ts // P
    n_slabs = nb * per_b
    slab_b = [k // per_b for k in range(n_slabs)]
    slab_rows = [slice((k % per_b) * P, (k % per_b + 1) * P) for k in range(n_slabs)]
    bs = [_tri_left(tril2, la_all[slab_b[k]][slab_rows[k]]) for k in range(n_slabs)]

    qms, scores, kv0s, kv1s, dec0s, dec1s, vss = [], [], [], [], [], [], []
    for pi in range(n_slabs):
        bb, rows = slab_b[pi], slab_rows[pi]
        b = bs[pi]
        bT = b.T
        bl0 = bT[:, C - 1:C]
        bl1 = bT[:, P - 1:P]
        q_dec = q_ref[bb, rows, :] * jnp.exp(b)
        kT = kT_ref[bb, :, rows]
        k_invT = (kT * jnp.exp(-bT)).astype(BF16)
        k_teT = kT * jnp.exp(jnp.where(first, bl0, bl1) - bT)
        zt = jnp.zeros_like(k_teT)
        k_te0 = jnp.where(first, k_teT, zt).astype(BF16)
        k_te1 = jnp.where(first, zt, k_teT).astype(BF16)
        dec0 = jnp.exp(bl0)
        dec1 = jnp.exp(bl1)

        vs = [v_ref[bb, rows, h * dv:(h + 1) * dv] for h in range(H)]
        kv = [_dot(jnp.concatenate([k_te0[h * dk:(h + 1) * dk, :], k_te1[h * dk:(h + 1) * dk, :]], axis=0),
                   vs[h]) for h in range(H)]
        kv0s.append(jnp.concatenate([kv[h][:dk] for h in range(H)], axis=0))
        kv1s.append(jnp.concatenate([kv[h][dk:] for h in range(H)], axis=0))
        dec0s.append(dec0)
        dec1s.append(dec1)
        vss.append(vs)

        zq = jnp.zeros((C, KW), F32)
        qm = [jnp.concatenate(
            [jnp.where((lane_q >= h * dk) & (lane_q < (h + 1) * dk), q_dec[c * C:(c + 1) * C], zq)
             for h in range(H)], axis=0).astype(BF16) for c in range(2)]
        a = _dot(jnp.concatenate(qm, axis=0), k_invT)
        scores.append(jnp.where(causal, a, 0.0).astype(BF16))
        qms.append(qm)

    o_inters = []
    for pi in range(n_slabs):
        if pi % per_b == 0:
            s = state_ref[slab_b[pi]]
        s1 = dec0s[pi] * s + kv0s[pi]
        o_inters.append([_dot(qms[pi][0], s.astype(BF16)), _dot(qms[pi][1], s1.astype(BF16))])
        s = dec1s[pi] * s1 + kv1s[pi]
        if pi % per_b == per_b - 1:
            state_ref[slab_b[pi]] = s

    for pi in range(n_slabs):
        bb, rows = slab_b[pi], slab_rows[pi]
        a, o_inter, vs = scores[pi], o_inters[pi], vss[pi]
        for h in range(H):
            hv = slice(h * dv, (h + 1) * dv)
            a_h = jnp.concatenate([a[(c * H + h) * C:(c * H + h + 1) * C] for c in range(2)], axis=0)
            o = _dot(a_h, vs[h]) + jnp.concatenate(
                [o_inter[c][h * C:(h + 1) * C] for c in range(2)], axis=0)
            o = o * lax.rsqrt(jnp.mean(o * o, axis=-1, keepdims=True) + RMS_EPS)
            o_ref[bb, rows, hv] = (o * gate_ref[bb, rows, hv]).astype(o_ref.dtype)


def _gla_attention(gq, gkT, gv, gate, small, w_a2_pad, b_a):
    B, S, _ = gq.shape
    ts = GLA_TS
    tok = lambda w: pl.BlockSpec((B, ts, w), lambda s: (0, s, 0))
    full = lambda r, c: pl.BlockSpec((r, c), lambda s: (0, 0))
    return pl.pallas_call(
        _gla_kernel,
        grid=(S // ts,),
        in_specs=[
            tok(GLA_KEY_WIDTH), pl.BlockSpec((B, GLA_KEY_WIDTH, ts), lambda s: (0, 0, s)),
            tok(GLA_WIDTH), tok(GLA_WIDTH), tok(SMALL_W),
            full(SMALL_W, GLA_KEY_WIDTH), full(1, GLA_KEY_WIDTH),
        ],
        out_specs=tok(GLA_WIDTH),
        out_shape=jax.ShapeDtypeStruct((B, S, GLA_WIDTH), BF16),
        scratch_shapes=[pltpu.VMEM((B, GLA_KEY_WIDTH, GLA_VAL_DIM), F32)],
        compiler_params=pltpu.CompilerParams(
            dimension_semantics=("arbitrary",), vmem_limit_bytes=VMEM_LIMIT),
        name="gla_attn",
    )(gq, gkT, gv, gate, small, w_a2_pad, b_a)


def _layer_norm(z, g, b):
    mu = jnp.mean(z, axis=-1, keepdims=True)
    d = z - mu
    var = jnp.mean(d * d, axis=-1, keepdims=True)
    return d * lax.rsqrt(var + LN_EPS) * g + b


def _tail_kernel(x_ref, fox_ref, gla_ref, mod_ref, wof_ref, wog_ref, ln1g_ref, ln1b_ref,
                 wg_ref, wu_ref, wd_ref, ln2g_ref, ln2b_ref, o_ref):
    m = mod_ref[0]
    tm = x_ref.shape[0]
    halves = [slice(0, tm // 2), slice(tm // 2, tm)]
    ys = [_dot(fox_ref[r, :], wof_ref[...]) + _dot(gla_ref[r, :], wog_ref[...]) for r in halves]
    x1s, gs, ups = [], [], []
    for r, y in zip(halves, ys):
        x1 = _layer_norm(DEEPNORM_ALPHA * x_ref[r, :] + (1.0 + m[2:3]) * y,
                         ln1g_ref[...], ln1b_ref[...])
        u2 = (x1 * (1.0 + m[4:5]) + m[3:4]).astype(BF16)
        x1s.append(x1)
        gs.append(_dot(u2, wg_ref[...]))
        ups.append(_dot(u2, wu_ref[...]))
    y2s = []
    for g, up in zip(gs, ups):
        h = (g * (1.0 / (1.0 + jnp.exp(-g))) * up).astype(BF16)
        y2s.append(_dot(h, wd_ref[...]))
    for r, x1, y2 in zip(halves, x1s, y2s):
        o_ref[r, :] = _layer_norm(DEEPNORM_ALPHA * x1 + (1.0 + m[5:6]) * y2,
                                  ln2g_ref[...], ln2b_ref[...])


def _tail(x2d, fox2d, gla2d, mod, wo, ln1g, ln1b, wg, wu, wd, ln2g, ln2b, steps_per_batch):
    T, D = x2d.shape
    tm = TAIL_TM
    const = lambda r, c: pl.BlockSpec((r, c), lambda i: (0, 0), pipeline_mode=pl.Buffered(1))
    tok = lambda w: pl.BlockSpec((tm, w), lambda i: (i, 0))
    return pl.pallas_call(
        _tail_kernel,
        grid=(T // tm,),
        in_specs=[
            tok(D), tok(FOX_WIDTH), tok(GLA_WIDTH),
            pl.BlockSpec((1, N_MOD, D), lambda i: (i // steps_per_batch, 0, 0)),
            pl.BlockSpec((FOX_WIDTH, D), lambda i: (0, 0), pipeline_mode=pl.Buffered(1)),
            pl.BlockSpec((GLA_WIDTH, D), lambda i: (1, 0), pipeline_mode=pl.Buffered(1)),
            const(1, D), const(1, D),
            const(D, D_FF), const(D, D_FF), const(D_FF, D), const(1, D), const(1, D),
        ],
        out_specs=tok(D),
        out_shape=jax.ShapeDtypeStruct((T, D), F32),
        compiler_params=pltpu.CompilerParams(
            dimension_semantics=("arbitrary",), vmem_limit_bytes=VMEM_LIMIT),
        name="tail",
    )(x2d, fox2d, gla2d, mod, wo, wo, ln1g, ln1b, wg, wu, wd, ln2g, ln2b)


def kernel(x, c, w_c, b_c, w_in, b_f, w_a2, b_a, g_gla, w_o, ln1_g, ln1_b,
           w_gate, w_up, w_down, ln2_g, ln2_b):
    B, S, D = x.shape
    assert D == D_MODEL and S % INPROJ_TM == 0 and INPROJ_TM == FOX_TK and FOX_TK % FOX_TQ == 0

    b_f_pad = jnp.pad(b_f.reshape(1, FOX_HEADS), ((0, 0), (0, SMALL_W - FOX_HEADS)))
    w_a2_pad = jnp.pad(w_a2, ((FOX_HEADS, SMALL_W - FOX_HEADS - GLA_GATE_RANK), (0, 0))).astype(BF16)

    mod, w_all = _modulation(c, w_c, b_c, w_in)

    fqT, kaug, fvT, gq, gkT, gv, gate, small, drel, stats, wg16, wu16, wd16, wo16 = _in_projection(
        x, mod, w_all, b_f_pad, g_gla.reshape(1, -1), w_gate, w_up, w_down, w_o)
    fox = _fox_attention(fqT, kaug, fvT, drel, stats)
    gla = _gla_attention(gq, gkT, gv, gate, small, w_a2_pad, b_a.reshape(1, -1))

    T = B * S
    out = _tail(x.reshape(T, D), fox.reshape(T, FOX_WIDTH), gla.reshape(T, GLA_WIDTH), mod,
                wo16,
                ln1_g.reshape(1, D), ln1_b.reshape(1, D),
                wg16, wu16, wd16,
                ln2_g.reshape(1, D), ln2_b.reshape(1, D), S // TAIL_TM)
    return out.reshape(B, S, D)
```

```python
import functools
import math

import numpy as np
import jax
import jax.numpy as jnp
from jax import lax
from jax.experimental import pallas as pl
from jax.experimental.pallas import tpu as pltpu

F32 = jnp.float32
BF16 = jnp.bfloat16

D_MODEL = 1024
FOX_WIDTH = 512
FOX_HEAD_DIM = 64
FOX_HEADS = 8
FOX_PAIRS = FOX_HEADS // 2
GLA_WIDTH = 512
GLA_HEADS = 4
GLA_VAL_DIM = 128
GLA_KEY_DIM = 64
GLA_KEY_WIDTH = 256
GLA_GATE_RANK = 16
GLA_GATE_TEMP = 16.0
GLA_CHUNK = 64
D_FF = 2816
LN_EPS = 1e-5
RMS_EPS = 1e-6
N_MOD = 6
DEEPNORM_ALPHA = 2.0 ** 0.25
IN_SPLITS = (FOX_WIDTH, FOX_WIDTH, FOX_WIDTH, FOX_HEADS, GLA_KEY_WIDTH, GLA_KEY_WIDTH,
             GLA_WIDTH, GLA_GATE_RANK, GLA_WIDTH)
LOG2E = math.log2(math.e)

LANES = 128
SUBLANES = 8
MXU_DIM = 256
SMALL_W = LANES
VMEM_LIMIT = 56 * 1024 * 1024

MOD_TN = 768
INPROJ_TM = 512
FOX_TQ = 256
FOX_GROUP_HEADS = 4
FOX_SUM_ROWS = 16
FOX_TK = 512
GLA_TS = 512
GLA_PAIR = 2 * GLA_CHUNK
TAIL_TM = 512

NEG_BIG = -1e30
PRUNE_LOG2 = -160.0
NORM_SLACK = 1.02

AUG_ROWS = 32


def _log_sigmoid(z):
    return jnp.minimum(z, 0.0) - jnp.log(1.0 + jnp.exp(-jnp.abs(z)))


def _split3(a):
    hi = a.astype(BF16)
    r1 = a - hi.astype(F32)
    mid = r1.astype(BF16)
    lo = (r1 - mid.astype(F32)).astype(BF16)
    return hi, mid, lo


def _dot(a, b):
    return jnp.dot(a, b, preferred_element_type=F32)


def _tri_left(tri, a):
    hi, mid, lo = _split3(a)
    return _dot(tri, hi) + _dot(tri, mid) + _dot(tri, lo)


def _tri_right(a, tri):
    hi, mid, lo = _split3(a)
    return _dot(hi, tri) + _dot(mid, tri) + _dot(lo, tri)


def _regroup_blocks():
    blocks = []
    for dst, src, width in _W_GROUPS:
        blocks += [(dst + c0, src + c0) for c0 in range(0, width, MXU_DIM)]
    blocks.append((_C_SM, None))
    return blocks


def _mod_kernel(cT_ref, w_ref, b_ref, winT_ref, o_ref, wall_ref, *, n_rows):
    w = w_ref[...]
    outs = [jnp.sum(w * cT_ref[:, b:b + 1], axis=0, keepdims=True) for b in range(n_rows)]
    outs.append(jnp.zeros((o_ref.shape[0] - n_rows, w.shape[1]), F32))
    o_ref[...] = jnp.concatenate(outs, axis=0) + b_ref[...]

    j = pl.program_id(0)
    n_steps = pl.num_programs(0)
    for k, (dst, src) in enumerate(_regroup_blocks()):
        @pl.when(j == k % n_steps)
        def _():
            if src is None:
                blk = jnp.concatenate(
                    [winT_ref[_SRC_FF:_SRC_FF + FOX_HEADS, :], winT_ref[_SRC_GA:_SRC_GA + GLA_GATE_RANK, :],
                     jnp.zeros((MXU_DIM - FOX_HEADS - GLA_GATE_RANK, winT_ref.shape[1]), F32)], axis=0)
            else:
                blk = winT_ref[src:src + MXU_DIM, :]
            wall_ref[:, dst:dst + MXU_DIM] = blk.T.astype(BF16)


def _modulation(c, w_c, b_c, w_in):
    B, D = c.shape
    N = w_c.shape[1]
    rows = SUBLANES
    n_steps = N // MOD_TN
    cT_pad = jnp.pad(c.T, ((0, 0), (0, LANES - B)))
    w_inT = w_in.T
    out, w_all = pl.pallas_call(
        functools.partial(_mod_kernel, n_rows=B),
        grid=(n_steps,),
        in_specs=[
            pl.BlockSpec((D, LANES), lambda j: (0, 0)),
            pl.BlockSpec((D, MOD_TN), lambda j: (0, j)),
            pl.BlockSpec((1, MOD_TN), lambda j: (0, j)),
            pl.BlockSpec(w_inT.shape, lambda j: (0, 0), pipeline_mode=pl.Buffered(1)),
        ],
        out_specs=[pl.BlockSpec((rows, MOD_TN), lambda j: (0, j)),
                   pl.BlockSpec((D, _W_COLS), lambda j: (0, 0))],
        out_shape=[jax.ShapeDtypeStruct((rows, N), F32),
                   jax.ShapeDtypeStruct((D, _W_COLS), BF16)],
        compiler_params=pltpu.CompilerParams(
            dimension_semantics=("arbitrary",), vmem_limit_bytes=VMEM_LIMIT),
        name="adaln_mod",
    )(cT_pad, w_c, b_c.reshape(1, N), w_inT)
    return out[:B].reshape(B, N_MOD, D), w_all


_C_FQ, _C_FK, _C_FV = 0, 512, 1024
_C_GQ, _C_SM, _C_GK, _C_GV, _C_GR = 1536, 1792, 2048, 2304, 2816
_W_COLS = _C_GR + GLA_WIDTH
_SRC = [sum(IN_SPLITS[:i]) for i in range(len(IN_SPLITS))]
_SRC_FF, _SRC_GA = _SRC[3], _SRC[7]
_W_GROUPS = ((_C_FQ, _SRC[0], FOX_WIDTH), (_C_FK, _SRC[1], FOX_WIDTH), (_C_FV, _SRC[2], FOX_WIDTH),
             (_C_GQ, _SRC[4], GLA_KEY_WIDTH), (_C_GK, _SRC[5], GLA_KEY_WIDTH),
             (_C_GV, _SRC[6], GLA_WIDTH), (_C_GR, _SRC[8], GLA_WIDTH))
_KAUG_W = FOX_PAIRS * MXU_DIM


def _head_indicator():
    ind = np.zeros((FOX_WIDTH, MXU_DIM), np.float32)
    for h in range(FOX_HEADS):
        ind[h * FOX_HEAD_DIM:(h + 1) * FOX_HEAD_DIM, h] = 1.0
    return ind


def _inproj_kernel(x_ref, mod_ref, w_ref, bf_ref, ind_ref, gg_ref, wg32_ref, wu32_ref, wd32_ref, wo32_ref,
                   fqT_ref, kaug_ref, fvT_ref, gq_ref, gk_ref, gv_ref, gr_ref, sm_ref,
                   drel_ref, stats_ref, wg16_ref, wu16_ref, wd16_ref, wo16_ref, carry_ref):
    tm = x_ref.shape[1]

    wg16_ref[...] = wg32_ref[...].astype(BF16)
    wu16_ref[...] = wu32_ref[...].astype(BF16)
    wd16_ref[...] = wd32_ref[...].astype(BF16)
    wo16_ref[...] = wo32_ref[...].astype(BF16)

    @pl.when(pl.program_id(1) == 0)
    def _():
        carry_ref[...] = jnp.zeros_like(carry_ref)

    m = mod_ref[0]
    u = (x_ref[0] * (1.0 + m[1:2]) + m[0:1]).astype(BF16)

    def proj(lo, width):
        return _dot(u, w_ref[:, lo:lo + width])

    gq_small = proj(_C_GQ, 2 * GLA_KEY_WIDTH)
    gq_ref[0] = gq_small[:, :GLA_KEY_WIDTH] * (GLA_KEY_DIM ** -0.5)
    small = gq_small[:, GLA_KEY_WIDTH:GLA_KEY_WIDTH + SMALL_W]
    sm_ref[0] = small

    fq = proj(_C_FQ, FOX_WIDTH) * (FOX_HEAD_DIM ** -0.5 * LOG2E)
    fqT_ref[0] = fq.T.astype(BF16)
    fk32 = proj(_C_FK, FOX_WIDTH)
    fk = fk32.astype(BF16)
    qn2 = jnp.max(_dot((fq * fq).astype(BF16), ind_ref[...])[:, :SMALL_W], axis=0, keepdims=True)
    kn2 = jnp.max(_dot((fk32 * fk32).astype(BF16), ind_ref[...])[:, :SMALL_W], axis=0, keepdims=True)

    zT = (small + bf_ref[...]).T[:FOX_HEADS, :]
    lfT = _log_sigmoid(zT) * LOG2E
    r = lax.broadcasted_iota(jnp.int32, (tm, tm), 0)
    c = lax.broadcasted_iota(jnp.int32, (tm, tm), 1)
    triu = jnp.where(r <= c, 1.0, 0.0).astype(BF16)
    drelT = _tri_right(lfT, triu)
    drel_ref[0] = drelT

    c_before = carry_ref[...]
    srow = lax.broadcasted_iota(jnp.int32, (SUBLANES, SMALL_W), 0)
    slane = lax.broadcasted_iota(jnp.int32, (SUBLANES, SMALL_W), 1)
    total = jnp.sum(jnp.where(srow == slane, drelT[:, tm - 1:tm], 0.0), axis=0, keepdims=True)
    carry_ref[...] = c_before + total

    hi, mid, lo = _split3(drelT)
    ones3 = jnp.where(lax.broadcasted_iota(jnp.int32, (SUBLANES, tm), 0) < 3, 1.0, 0.0)
    slabT = jnp.concatenate(
        [-hi.astype(F32), -mid.astype(F32), -lo.astype(F32), ones3,
         jnp.zeros((LANES - 4 * SUBLANES, tm), F32)], axis=0)
    slab = slabT.T.astype(BF16)

    fvT_ref[0] = proj(_C_FV, FOX_WIDTH).T.astype(BF16)
    gv_ref[0] = proj(_C_GV, GLA_WIDTH).astype(BF16)
    gr = proj(_C_GR, GLA_WIDTH)
    gr_ref[0] = gg_ref[...] * (gr * (1.0 / (1.0 + jnp.exp(-gr))))
    gk_ref[0] = proj(_C_GK, GLA_KEY_WIDTH).T

    stats_ref[0, 0] = jnp.where(srow == 0, c_before,
                                jnp.where(srow == 1, qn2, jnp.where(srow == 2, kn2, 0.0)))
    for p in range(FOX_PAIRS):
        kaug_ref[0, :, p * MXU_DIM:p * MXU_DIM + LANES] = fk[:, p * LANES:(p + 1) * LANES]
        kaug_ref[0, :, p * MXU_DIM + LANES:(p + 1) * MXU_DIM] = slab


def _in_projection(x, mod, w_all, b_f_pad, g_gla, w_gate, w_up, w_down, w_o):
    B, S, D = x.shape
    tm = INPROJ_TM
    n_steps = B * (S // tm)
    steps_b = S // tm
    up_rows = D // n_steps
    down_rows = 2 * D_FF // n_steps
    assert D % n_steps == 0 and up_rows % 16 == 0 and (2 * D_FF) % n_steps == 0 and down_rows % 16 == 0
    up_spec = pl.BlockSpec((up_rows, D_FF), lambda b, s: (b * steps_b + s, 0))
    down_spec = pl.BlockSpec((down_rows, D), lambda b, s: ((b * steps_b + s) // 2, 0))
    wo_rows = w_o.shape[0] // n_steps
    assert w_o.shape[0] % n_steps == 0 and wo_rows % 16 == 0
    wo_spec = pl.BlockSpec((wo_rows, D), lambda b, s: (b * steps_b + s, 0))
    ind = _head_indicator()
    tok = lambda w: pl.BlockSpec((1, tm, w), lambda b, s: (b, s, 0))
    tokT = lambda h: pl.BlockSpec((1, h, tm), lambda b, s: (b, 0, s))
    outs = pl.pallas_call(
        _inproj_kernel,
        grid=(B, S // tm),
        in_specs=[
            tok(D),
            pl.BlockSpec((1, N_MOD, D), lambda b, s: (b, 0, 0)),
            pl.BlockSpec((D, _W_COLS), lambda b, s: (0, 0), pipeline_mode=pl.Buffered(1)),
            pl.BlockSpec((1, SMALL_W), lambda b, s: (0, 0)),
            pl.BlockSpec(ind.shape, lambda b, s: (0, 0)),
            pl.BlockSpec((1, GLA_WIDTH), lambda b, s: (0, 0)),
            up_spec, up_spec, down_spec, wo_spec,
        ],
        out_specs=[
            tokT(FOX_WIDTH), tok(_KAUG_W), tokT(FOX_WIDTH),
            tok(GLA_KEY_WIDTH), tokT(GLA_KEY_WIDTH), tok(GLA_WIDTH), tok(GLA_WIDTH),
            tok(SMALL_W),
            tokT(FOX_HEADS),
            pl.BlockSpec((1, 1, SUBLANES, SMALL_W), lambda b, s: (b, s, 0, 0)),
            up_spec, up_spec, down_spec, wo_spec,
        ],
        out_shape=[
            jax.ShapeDtypeStruct((B, FOX_WIDTH, S), BF16),
            jax.ShapeDtypeStruct((B, S, _KAUG_W), BF16),
            jax.ShapeDtypeStruct((B, FOX_WIDTH, S), BF16),
            jax.ShapeDtypeStruct((B, S, GLA_KEY_WIDTH), F32),
            jax.ShapeDtypeStruct((B, GLA_KEY_WIDTH, S), F32),
            jax.ShapeDtypeStruct((B, S, GLA_WIDTH), BF16),
            jax.ShapeDtypeStruct((B, S, GLA_WIDTH), F32),
            jax.ShapeDtypeStruct((B, S, SMALL_W), F32),
            jax.ShapeDtypeStruct((B, FOX_HEADS, S), F32),
            jax.ShapeDtypeStruct((B, S // tm, SUBLANES, SMALL_W), F32),
            jax.ShapeDtypeStruct(w_gate.shape, BF16),
            jax.ShapeDtypeStruct(w_up.shape, BF16),
            jax.ShapeDtypeStruct(w_down.shape, BF16),
            jax.ShapeDtypeStruct(w_o.shape, BF16),
        ],
        scratch_shapes=[pltpu.VMEM((1, SMALL_W), F32)],
        compiler_params=pltpu.CompilerParams(
            dimension_semantics=("arbitrary", "arbitrary"), vmem_limit_bytes=VMEM_LIMIT),
        name="in_proj",
    )(x, mod, w_all, b_f_pad, jnp.asarray(ind, BF16), g_gla, w_gate, w_up, w_down, w_o)
    return outs


def _fox_kernel(ntiles_ref, nitems_ref, qT_ref, kaug_ref, vT_ref, drel_ref, crep_ref, o_ref,
                qaug_ref, s_ref, p_ref, mx_ref, acc_ref, m_ref, mnext_ref, shift_ref, alpha_ref, keep_ref):
    tq, tk = FOX_TQ, FOX_TK
    half = FOX_HEAD_DIM
    S = kaug_ref.shape[1]
    nq = S // tq
    nh = FOX_GROUP_HEADS
    n_blk = crep_ref.shape[2] // nh
    g = pl.program_id(0) * pl.num_programs(1) + pl.program_id(1)

    def build_qaug(qi):
        q0 = pl.multiple_of(qi * tq, tq)
        dr_all = drel_ref[0, :, pl.ds(q0, tq)]
        dr = dr_all[0:nh]
        for grp in range(1, dr_all.shape[0] // nh):
            dr = jnp.where(pl.program_id(1) == grp, dr_all[grp * nh:(grp + 1) * nh], dr)
        row = lax.broadcasted_iota(jnp.int32, (2 * half, tq), 0)
        r32 = lax.broadcasted_iota(jnp.int32, (AUG_ROWS, tq), 0)
        for h in range(nh):
            buf = (qi % 2) * nh + h
            qT = qT_ref[0, (h // 2) * 2 * half:(h // 2 + 1) * 2 * half, pl.ds(q0, tq)]
            own = (row < half) if h % 2 == 0 else (row >= half)
            qaug_ref[buf, 0:2 * half, :] = jnp.where(own, qT, jnp.zeros_like(qT))
            hi, mid, lo = _split3(dr[h:h + 1])
            gh = nh * pl.program_id(1) + h
            pick = (r32 == gh) | (r32 == gh + SUBLANES) | (r32 == gh + 2 * SUBLANES)
            slab = jnp.where(r32 == 24, hi.astype(F32),
                             jnp.where(r32 == 25, mid.astype(F32),
                                       jnp.where(r32 == 26, lo.astype(F32),
                                                 jnp.where(pick, 1.0, 0.0))))
            qaug_ref[buf, 2 * half:2 * half + AUG_ROWS, :] = slab.astype(BF16)

    def reset_state():
        acc_ref[...] = jnp.zeros_like(acc_ref)

    sum_rows = jnp.where(lax.broadcasted_iota(jnp.int32, (FOX_SUM_ROWS, tk), 0) == 0, 1.0, 0.0).astype(BF16)

    per = tk // tq

    def col_max8(s):
        return jnp.max(s.reshape(s.shape[0] // SUBLANES, SUBLANES, tq), axis=0)

    def tile_of(qi, t):
        raw = (qi + 1) * tq - (t + 1) * tk
        return pl.multiple_of(jnp.maximum(raw, 0), tq), raw < 0

    def qk_dots(qi, k_off):
        kts = [kaug_ref[0, pl.ds(k_off, tk), p * MXU_DIM:(p + 1) * MXU_DIM] for p in range(nh // 2)]
        return [_dot(kts[h // 2], qaug_ref[(qi % 2) * nh + h]) for h in range(nh)]

    def store_scores(scores, diag):
        which = jnp.where(diag, 1, 0)
        for h in range(nh):
            for g in range(per):
                blk = scores[h][g * tq:(g + 1) * tq]
                if g == per - 1:
                    blk = jnp.where(keep_ref[which] > 0.5, blk, NEG_BIG)
                s_ref[h, g * tq:(g + 1) * tq, :] = blk
                mx_ref[per * h + g] = col_max8(blk)

    def fill_upper_groups():
        for g in range(1, per):
            for h in range(nh):
                s_ref[h, g * tq:(g + 1) * tq, :] = jnp.full((tq, tq), NEG_BIG, F32)
                mx_ref[per * h + g] = jnp.full((SUBLANES, tq), NEG_BIG, F32)

    def mask_clamped_diagonal():
        kr = lax.broadcasted_iota(jnp.int32, (tq, tq), 0)
        qc = lax.broadcasted_iota(jnp.int32, (tq, tq), 1)
        for h in range(nh):
            tri = jnp.where(kr <= qc, s_ref[h, 0:tq, :], NEG_BIG)
            s_ref[h, 0:tq, :] = tri
            mx_ref[per * h] = col_max8(tri)
        fill_upper_groups()

    def prep_stage(qi, k_off, first):
        i_blk = qi * tq // INPROJ_TM
        for h in range(nh):
            c_i = crep_ref[0, 0, pl.ds(h * n_blk + i_blk, 1), :]
            big_c = [c_i - crep_ref[0, 0, pl.ds(h * n_blk + (k_off + g * tq) // INPROJ_TM, 1), :]
                     for g in range(per)]
            cand = jnp.max(mx_ref[per * h], axis=0, keepdims=True) + big_c[0]
            for g in range(1, per):
                cand = jnp.maximum(cand, jnp.max(mx_ref[per * h + g], axis=0, keepdims=True) + big_c[g])
            m_prev = jnp.where(first, NEG_BIG, m_ref[h])
            m_new = jnp.maximum(m_prev, cand)
            mnext_ref[h] = m_new
            for g in range(per):
                shift_ref[per * h + g] = m_new - big_c[g]
            alpha_ref[h] = jnp.exp2(m_prev - m_new)

    def exp_stage():
        for h in range(nh):
            for g in range(per):
                rows = slice(g * tq, (g + 1) * tq)
                p_ref[h, rows, :] = jnp.exp2(s_ref[h, rows, :] - shift_ref[per * h + g]).astype(BF16)

    def pv_stage(k_off):
        for h in range(nh):
            vT = jnp.concatenate([vT_ref[0, h * half:(h + 1) * half, pl.ds(k_off, tk)], sum_rows], axis=0)
            acc_ref[h] = alpha_ref[h] * acc_ref[h] + _dot(vT, p_ref[h])
            m_ref[h] = mnext_ref[h]

    def finalize(qi):
        outs = []
        for h in range(nh):
            acc = acc_ref[h]
            outs.append(acc[:half] / acc[half:half + 1])
        oT = jnp.concatenate(outs, axis=0)
        o_ref[0, pl.ds(pl.multiple_of(qi * tq, tq), tq), :] = oT.T.astype(o_ref.dtype)

    def fix_clamped(t):
        @pl.when(t == 0)
        def _():
            mask_clamped_diagonal()

        @pl.when(t > 0)
        def _():
            fill_upper_groups()

    zero = jnp.int32(0)
    qaug_ref[...] = jnp.zeros_like(qaug_ref)
    keep_ref[0] = jnp.ones((tq, tq), F32)
    keep_ref[1] = jnp.where(lax.broadcasted_iota(jnp.int32, (tq, tq), 0)
                            <= lax.broadcasted_iota(jnp.int32, (tq, tq), 1), 1.0, 0.0)
    build_qaug(zero)
    build_qaug(zero + 1)
    reset_state()
    k0, _ = tile_of(zero, zero)
    store_scores(qk_dots(zero, k0), False)
    mask_clamped_diagonal()
    prep_stage(zero, k0, True)

    def body(it, carry):
        qi, t = carry
        last = t == 0
        qi2 = jnp.where(last, qi + 1, qi)
        t2 = jnp.where(last, ntiles_ref[g * nq + jnp.minimum(qi + 1, nq - 1)] - 1, t - 1)
        k_cur, _ = tile_of(qi, t)
        k_nxt, cl_nxt = tile_of(qi2, t2)
        exp_stage()
        nxt = qk_dots(qi2, k_nxt)
        pv_stage(k_cur)
        store_scores(nxt, (t2 == 0) & jnp.logical_not(cl_nxt))
        prep_stage(qi2, k_nxt, last)

        @pl.when(cl_nxt)
        def _():
            fix_clamped(t2)
            prep_stage(qi2, k_nxt, last)

        @pl.when(last)
        def _():
            finalize(qi)
            reset_state()

            @pl.when(qi + 2 < nq)
            def _():
                build_qaug(qi + 2)

        return qi2, t2

    qi, t = lax.fori_loop(0, nitems_ref[g] - 1, body, (jnp.int32(0), jnp.int32(0)))
    exp_stage()
    pv_stage(tile_of(qi, t)[0])
    finalize(qi)


def _prune_plan(stats, drel, nq):
    B, n_blk = stats.shape[:2]
    per = FOX_TK // FOX_TQ
    c = stats[:, :, 0, :FOX_HEADS]
    qn = jnp.sqrt(stats[:, :, 1, :FOX_HEADS])
    kn = jnp.sqrt(stats[:, :, 2, :FOX_HEADS])
    d_in = jnp.transpose(drel[:, :, FOX_TQ - 1::FOX_TQ], (0, 2, 1))
    d_end = jnp.repeat(c, per, axis=1) + d_in
    d_before = jnp.concatenate([jnp.zeros_like(d_end[:, :1]), d_end[:, :-1]], axis=1)
    kn_q = jnp.repeat(kn, per, axis=1)
    qn_q = jnp.repeat(qn, per, axis=1)
    kn_prev = jnp.concatenate([jnp.zeros_like(kn_q[:, :1]), kn_q[:, :-1]], axis=1)
    kn_tile = jnp.maximum(kn_q, kn_prev)
    bound = (d_before[:, :, None, :] - d_end[:, None, :, :]
             + NORM_SLACK * qn_q[:, :, None, :] * (kn_tile[:, None, :, :] + kn_q[:, :, None, :]))
    back = np.arange(nq)[:, None] - np.arange(nq)[None, :]
    is_tile = (back > 0) & (back % per == 0)
    t_of = np.where(is_tile, back // per, 0).astype(np.int32)
    needed = is_tile[None, :, :, None] & ~(bound <= PRUNE_LOG2)
    nh = FOX_GROUP_HEADS
    needed_g = jnp.any(needed.reshape(B, nq, nq, FOX_HEADS // nh, nh), axis=-1)
    n_tiles = 1 + jnp.max(jnp.where(needed_g, t_of[None, :, :, None], 0), axis=2)
    n_tiles = jnp.transpose(n_tiles, (0, 2, 1))
    n_items = jnp.sum(n_tiles, axis=2)
    return n_tiles.reshape(-1).astype(jnp.int32), n_items.reshape(-1).astype(jnp.int32)


def _fox_attention(fqT, kaug, fvT, drel, stats):
    B, _, S = fqT.shape
    tq, tk = FOX_TQ, FOX_TK
    n_blk = S // INPROJ_TM
    nh = FOX_GROUP_HEADS
    n_groups = FOX_HEADS // nh
    c = stats[:, :, 0, :FOX_HEADS]
    c = jnp.transpose(c, (0, 2, 1)).reshape(B, n_groups, nh * n_blk, 1)
    crep = jnp.broadcast_to(c, (B, n_groups, nh * n_blk, tq))
    n_tiles, n_items = _prune_plan(stats, drel, S // tq)
    grid_spec = pltpu.PrefetchScalarGridSpec(
        num_scalar_prefetch=2,
        grid=(B, n_groups),
        in_specs=[
            pl.BlockSpec((1, nh * FOX_HEAD_DIM, S), lambda b, p, js, ni: (b, p, 0)),
            pl.BlockSpec((1, S, nh // 2 * MXU_DIM), lambda b, p, js, ni: (b, 0, p)),
            pl.BlockSpec((1, nh * FOX_HEAD_DIM, S), lambda b, p, js, ni: (b, p, 0)),
            pl.BlockSpec((1, FOX_HEADS, S), lambda b, p, js, ni: (b, 0, 0)),
            pl.BlockSpec((1, 1, nh * n_blk, tq), lambda b, p, js, ni: (b, p, 0, 0)),
        ],
        out_specs=pl.BlockSpec((1, S, nh * FOX_HEAD_DIM), lambda b, p, js, ni: (b, 0, p)),
        scratch_shapes=[
            pltpu.VMEM((2 * nh, MXU_DIM, tq), BF16),
            pltpu.VMEM((nh, tk, tq), F32),
            pltpu.VMEM((nh, tk, tq), BF16),
            pltpu.VMEM((nh * (tk // tq), SUBLANES, tq), F32),
            pltpu.VMEM((nh, FOX_HEAD_DIM + FOX_SUM_ROWS, tq), F32),
            pltpu.VMEM((nh, 1, tq), F32),
            pltpu.VMEM((nh, 1, tq), F32),
            pltpu.VMEM((nh * (tk // tq), 1, tq), F32),
            pltpu.VMEM((nh, 1, tq), F32),
            pltpu.VMEM((2, tq, tq), F32),
        ])
    return pl.pallas_call(
        _fox_kernel,
        grid_spec=grid_spec,
        out_shape=jax.ShapeDtypeStruct((B, S, FOX_WIDTH), BF16),
        compiler_params=pltpu.CompilerParams(
            dimension_semantics=("arbitrary", "arbitrary"),
            vmem_limit_bytes=VMEM_LIMIT),
        name="fox_attn",
    )(n_tiles, n_items, fqT, kaug, fvT, drel, crep)


def _gla_kernel(q_ref, kT_ref, v_ref, gate_ref, sm_ref, wa_ref, ba_ref, o_ref, state_ref):
    nb, ts = q_ref.shape[0], q_ref.shape[1]
    C, P = GLA_CHUNK, GLA_PAIR
    dk, dv, H = GLA_KEY_DIM, GLA_VAL_DIM, GLA_HEADS
    KW = GLA_KEY_WIDTH

    @pl.when(pl.program_id(0) == 0)
    def _():
        state_ref[...] = jnp.zeros_like(state_ref)

    la_all = [_log_sigmoid(_dot(sm_ref[bb].astype(BF16), wa_ref[...]) + ba_ref[...])
              * (1.0 / GLA_GATE_TEMP) for bb in range(nb)]

    ri = lax.broadcasted_iota(jnp.int32, (P, P), 0)
    ci = lax.broadcasted_iota(jnp.int32, (P, P), 1)
    tril2 = jnp.where(((ri < C) == (ci < C)) & (ci <= ri), 1.0, 0.0).astype(BF16)

    rs = lax.broadcasted_iota(jnp.int32, (2 * H * C, P), 0)
    cs = lax.broadcasted_iota(jnp.int32, (2 * H * C, P), 1)
    causal = ((rs >= H * C) == (cs >= C)) & ((cs & (C - 1)) <= (rs & (C - 1)))

    lane_q = lax.broadcasted_iota(jnp.int32, (C, KW), 1)
    lane_t = lax.broadcasted_iota(jnp.int32, (KW, P), 1)
    first = lane_t < C

    per_b = ts // P
    n_slabs = nb * per_b
    slab_b = [k // per_b for k in range(n_slabs)]
    slab_rows = [slice((k % per_b) * P, (k % per_b + 1) * P) for k in range(n_slabs)]
    bs = [_tri_left(tril2, la_all[slab_b[k]][slab_rows[k]]) for k in range(n_slabs)]

    qms, scores, kv0s, kv1s, dec0s, dec1s, vss = [], [], [], [], [], [], []
    for pi in range(n_slabs):
        bb, rows = slab_b[pi], slab_rows[pi]
        b = bs[pi]
        bT = b.T
        bl0 = bT[:, C - 1:C]
        bl1 = bT[:, P - 1:P]
        q_dec = q_ref[bb, rows, :] * jnp.exp(b)
        kT = kT_ref[bb, :, rows]
        k_invT = (kT * jnp.exp(-bT)).astype(BF16)
        k_teT = kT * jnp.exp(jnp.where(first, bl0, bl1) - bT)
        zt = jnp.zeros_like(k_teT)
        k_te0 = jnp.where(first, k_teT, zt).astype(BF16)
        k_te1 = jnp.where(first, zt, k_teT).astype(BF16)
        dec0 = jnp.exp(bl0)
        dec1 = jnp.exp(bl1)

        vs = [v_ref[bb, rows, h * dv:(h + 1) * dv] for h in range(H)]
        kv = [_dot(jnp.concatenate([k_te0[h * dk:(h + 1) * dk, :], k_te1[h * dk:(h + 1) * dk, :]], axis=0),
                   vs[h]) for h in range(H)]
        kv0s.append(jnp.concatenate([kv[h][:dk] for h in range(H)], axis=0))
        kv1s.append(jnp.concatenate([kv[h][dk:] for h in range(H)], axis=0))
        dec0s.append(dec0)
        dec1s.append(dec1)
        vss.append(vs)

        zq = jnp.zeros((C, KW), F32)
        qm = [jnp.concatenate(
            [jnp.where((lane_q >= h * dk) & (lane_q < (h + 1) * dk), q_dec[c * C:(c + 1) * C], zq)
             for h in range(H)], axis=0).astype(BF16) for c in range(2)]
        a = _dot(jnp.concatenate(qm, axis=0), k_invT)
        scores.append(jnp.where(causal, a, 0.0).astype(BF16))
        qms.append(qm)

    o_inters = []
    for pi in range(n_slabs):
        if pi % per_b == 0:
            s = state_ref[slab_b[pi]]
        s1 = dec0s[pi] * s + kv0s[pi]
        o_inters.append([_dot(qms[pi][0], s.astype(BF16)), _dot(qms[pi][1], s1.astype(BF16))])
        s = dec1s[pi] * s1 + kv1s[pi]
        if pi % per_b == per_b - 1:
            state_ref[slab_b[pi]] = s

    for pi in range(n_slabs):
        bb, rows = slab_b[pi], slab_rows[pi]
        a, o_inter, vs = scores[pi], o_inters[pi], vss[pi]
        for h in range(H):
            hv = slice(h * dv, (h + 1) * dv)
            a_h = jnp.concatenate([a[(c * H + h) * C:(c * H + h + 1) * C] for c in range(2)], axis=0)
            o = _dot(a_h, vs[h]) + jnp.concatenate(
                [o_inter[c][h * C:(h + 1) * C] for c in range(2)], axis=0)
            o = o * lax.rsqrt(jnp.mean(o * o, axis=-1, keepdims=True) + RMS_EPS)
            o_ref[bb, rows, hv] = (o * gate_ref[bb, rows, hv]).astype(o_ref.dtype)


def _gla_attention(gq, gkT, gv, gate, small, w_a2_pad, b_a):
    B, S, _ = gq.shape
    ts = GLA_TS
    tok = lambda w: pl.BlockSpec((B, ts, w), lambda s: (0, s, 0))
    full = lambda r, c: pl.BlockSpec((r, c), lambda s: (0, 0))
    return pl.pallas_call(
        _gla_kernel,
        grid=(S // ts,),
        in_specs=[
            tok(GLA_KEY_WIDTH), pl.BlockSpec((B, GLA_KEY_WIDTH, ts), lambda s: (0, 0, s)),
            tok(GLA_WIDTH), tok(GLA_WIDTH), tok(SMALL_W),
            full(SMALL_W, GLA_KEY_WIDTH), full(1, GLA_KEY_WIDTH),
        ],
        out_specs=tok(GLA_WIDTH),
        out_shape=jax.ShapeDtypeStruct((B, S, GLA_WIDTH), BF16),
        scratch_shapes=[pltpu.VMEM((B, GLA_KEY_WIDTH, GLA_VAL_DIM), F32)],
        compiler_params=pltpu.CompilerParams(
            dimension_semantics=("arbitrary",), vmem_limit_bytes=VMEM_LIMIT),
        name="gla_attn",
    )(gq, gkT, gv, gate, small, w_a2_pad, b_a)


def _layer_norm(z, g, b):
    mu = jnp.mean(z, axis=-1, keepdims=True)
    d = z - mu
    var = jnp.mean(d * d, axis=-1, keepdims=True)
    return d * lax.rsqrt(var + LN_EPS) * g + b


def _tail_kernel(x_ref, fox_ref, gla_ref, mod_ref, wof_ref, wog_ref, ln1g_ref, ln1b_ref,
                 wg_ref, wu_ref, wd_ref, ln2g_ref, ln2b_ref, o_ref):
    m = mod_ref[0]
    tm = x_ref.shape[0]
    halves = [slice(0, tm // 2), slice(tm // 2, tm)]
    ys = [_dot(fox_ref[r, :], wof_ref[...]) + _dot(gla_ref[r, :], wog_ref[...]) for r in halves]
    x1s, gs, ups = [], [], []
    for r, y in zip(halves, ys):
        x1 = _layer_norm(DEEPNORM_ALPHA * x_ref[r, :] + (1.0 + m[2:3]) * y,
                         ln1g_ref[...], ln1b_ref[...])
        u2 = (x1 * (1.0 + m[4:5]) + m[3:4]).astype(BF16)
        x1s.append(x1)
        gs.append(_dot(u2, wg_ref[...]))
        ups.append(_dot(u2, wu_ref[...]))
    y2s = []
    for g, up in zip(gs, ups):
        h = (g * (1.0 / (1.0 + jnp.exp(-g))) * up).astype(BF16)
        y2s.append(_dot(h, wd_ref[...]))
    for r, x1, y2 in zip(halves, x1s, y2s):
        o_ref[r, :] = _layer_norm(DEEPNORM_ALPHA * x1 + (1.0 + m[5:6]) * y2,
                                  ln2g_ref[...], ln2b_ref[...])


def _tail(x2d, fox2d, gla2d, mod, wo, ln1g, ln1b, wg, wu, wd, ln2g, ln2b, steps_per_batch):
    T, D = x2d.shape
    tm = TAIL_TM
    const = lambda r, c: pl.BlockSpec((r, c), lambda i: (0, 0), pipeline_mode=pl.Buffered(1))
    tok = lambda w: pl.BlockSpec((tm, w), lambda i: (i, 0))
    return pl.pallas_call(
        _tail_kernel,
        grid=(T // tm,),
        in_specs=[
            tok(D), tok(FOX_WIDTH), tok(GLA_WIDTH),
            pl.BlockSpec((1, N_MOD, D), lambda i: (i // steps_per_batch, 0, 0)),
            pl.BlockSpec((FOX_WIDTH, D), lambda i: (0, 0), pipeline_mode=pl.Buffered(1)),
            pl.BlockSpec((GLA_WIDTH, D), lambda i: (1, 0), pipeline_mode=pl.Buffered(1)),
            const(1, D), const(1, D),
            const(D, D_FF), const(D, D_FF), const(D_FF, D), const(1, D), const(1, D),
        ],
        out_specs=tok(D),
        out_shape=jax.ShapeDtypeStruct((T, D), F32),
        compiler_params=pltpu.CompilerParams(
            dimension_semantics=("arbitrary",), vmem_limit_bytes=VMEM_LIMIT),
        name="tail",
    )(x2d, fox2d, gla2d, mod, wo, wo, ln1g, ln1b, wg, wu, wd, ln2g, ln2b)


def kernel(x, c, w_c, b_c, w_in, b_f, w_a2, b_a, g_gla, w_o, ln1_g, ln1_b,
           w_gate, w_up, w_down, ln2_g, ln2_b):
    B, S, D = x.shape
    assert D == D_MODEL and S % INPROJ_TM == 0 and INPROJ_TM == FOX_TK and FOX_TK % FOX_TQ == 0

    b_f_pad = jnp.pad(b_f.reshape(1, FOX_HEADS), ((0, 0), (0, SMALL_W - FOX_HEADS)))
    w_a2_pad = jnp.pad(w_a2, ((FOX_HEADS, SMALL_W - FOX_HEADS - GLA_GATE_RANK), (0, 0))).astype(BF16)

    mod, w_all = _modulation(c, w_c, b_c, w_in)

    fqT, kaug, fvT, gq, gkT, gv, gate, small, drel, stats, wg16, wu16, wd16, wo16 = _in_projection(
        x, mod, w_all, b_f_pad, g_gla.reshape(1, -1), w_gate, w_up, w_down, w_o)
    fox = _fox_attention(fqT, kaug, fvT, drel, stats)
    gla = _gla_attention(gq, gkT, gv, gate, small, w_a2_pad, b_a.reshape(1, -1))

    T = B * S
    out = _tail(x.reshape(T, D), fox.reshape(T, FOX_WIDTH), gla.reshape(T, GLA_WIDTH), mod,
                wo16,
                ln1_g.reshape(1, D), ln1_b.reshape(1, D),
                wg16, wu16, wd16,
                ln2_g.reshape(1, D), ln2_b.reshape(1, D), S // TAIL_TM)
    return out.reshape(B, S, D)
```

```python
import functools
import math

import numpy as np
import jax
import jax.numpy as jnp
from jax import lax
from jax.experimental import pallas as pl
from jax.experimental.pallas import tpu as pltpu

F32 = jnp.float32
BF16 = jnp.bfloat16

D_MODEL = 1024
FOX_WIDTH = 512
FOX_HEAD_DIM = 64
FOX_HEADS = 8
FOX_PAIRS = FOX_HEADS // 2
GLA_WIDTH = 512
GLA_HEADS = 4
GLA_VAL_DIM = 128
GLA_KEY_DIM = 64
GLA_KEY_WIDTH = 256
GLA_GATE_RANK = 16
GLA_GATE_TEMP = 16.0
GLA_CHUNK = 64
D_FF = 2816
LN_EPS = 1e-5
RMS_EPS = 1e-6
N_MOD = 6
DEEPNORM_ALPHA = 2.0 ** 0.25
IN_SPLITS = (FOX_WIDTH, FOX_WIDTH, FOX_WIDTH, FOX_HEADS, GLA_KEY_WIDTH, GLA_KEY_WIDTH,
             GLA_WIDTH, GLA_GATE_RANK, GLA_WIDTH)
LOG2E = math.log2(math.e)

LANES = 128
SUBLANES = 8
MXU_DIM = 256
SMALL_W = LANES
VMEM_LIMIT = 56 * 1024 * 1024

MOD_TN = 768
INPROJ_TM = 512
FOX_TQ = 256
FOX_GROUP_HEADS = 4
FOX_SUM_ROWS = 16
FOX_TK = 512
GLA_TS = 512
GLA_PAIR = 2 * GLA_CHUNK
TAIL_TM = 512

NEG_BIG = -1e30
PRUNE_LOG2 = -150.0
NORM_SLACK = 1.02

AUG_ROWS = 32


def _log_sigmoid(z):
    return jnp.minimum(z, 0.0) - jnp.log(1.0 + jnp.exp(-jnp.abs(z)))


def _split3(a):
    hi = a.astype(BF16)
    r1 = a - hi.astype(F32)
    mid = r1.astype(BF16)
    lo = (r1 - mid.astype(F32)).astype(BF16)
    return hi, mid, lo


def _dot(a, b):
    return jnp.dot(a, b, preferred_element_type=F32)


def _tri_left(tri, a):
    hi, mid, lo = _split3(a)
    return _dot(tri, hi) + _dot(tri, mid) + _dot(tri, lo)


def _tri_right(a, tri):
    hi, mid, lo = _split3(a)
    return _dot(hi, tri) + _dot(mid, tri) + _dot(lo, tri)


def _regroup_blocks():
    blocks = []
    for dst, src, width in _W_GROUPS:
        blocks += [(dst + c0, src + c0) for c0 in range(0, width, MXU_DIM)]
    blocks.append((_C_SM, None))
    return blocks


def _mod_kernel(cT_ref, w_ref, b_ref, winT_ref, o_ref, wall_ref, *, n_rows):
    w = w_ref[...]
    outs = [jnp.sum(w * cT_ref[:, b:b + 1], axis=0, keepdims=True) for b in range(n_rows)]
    outs.append(jnp.zeros((o_ref.shape[0] - n_rows, w.shape[1]), F32))
    o_ref[...] = jnp.concatenate(outs, axis=0) + b_ref[...]

    j = pl.program_id(0)
    n_steps = pl.num_programs(0)
    for k, (dst, src) in enumerate(_regroup_blocks()):
        @pl.when(j == k % n_steps)
        def _():
            if src is None:
                blk = jnp.concatenate(
                    [winT_ref[_SRC_FF:_SRC_FF + FOX_HEADS, :], winT_ref[_SRC_GA:_SRC_GA + GLA_GATE_RANK, :],
                     jnp.zeros((MXU_DIM - FOX_HEADS - GLA_GATE_RANK, winT_ref.shape[1]), F32)], axis=0)
            else:
                blk = winT_ref[src:src + MXU_DIM, :]
            wall_ref[:, dst:dst + MXU_DIM] = blk.T.astype(BF16)


def _modulation(c, w_c, b_c, w_in):
    B, D = c.shape
    N = w_c.shape[1]
    rows = SUBLANES
    n_steps = N // MOD_TN
    cT_pad = jnp.pad(c.T, ((0, 0), (0, LANES - B)))
    w_inT = w_in.T
    out, w_all = pl.pallas_call(
        functools.partial(_mod_kernel, n_rows=B),
        grid=(n_steps,),
        in_specs=[
            pl.BlockSpec((D, LANES), lambda j: (0, 0)),
            pl.BlockSpec((D, MOD_TN), lambda j: (0, j)),
            pl.BlockSpec((1, MOD_TN), lambda j: (0, j)),
            pl.BlockSpec(w_inT.shape, lambda j: (0, 0), pipeline_mode=pl.Buffered(1)),
        ],
        out_specs=[pl.BlockSpec((rows, MOD_TN), lambda j: (0, j)),
                   pl.BlockSpec((D, _W_COLS), lambda j: (0, 0))],
        out_shape=[jax.ShapeDtypeStruct((rows, N), F32),
                   jax.ShapeDtypeStruct((D, _W_COLS), BF16)],
        compiler_params=pltpu.CompilerParams(
            dimension_semantics=("arbitrary",), vmem_limit_bytes=VMEM_LIMIT),
        name="adaln_mod",
    )(cT_pad, w_c, b_c.reshape(1, N), w_inT)
    return out[:B].reshape(B, N_MOD, D), w_all


_C_FQ, _C_FK, _C_FV = 0, 512, 1024
_C_GQ, _C_SM, _C_GK, _C_GV, _C_GR = 1536, 1792, 2048, 2304, 2816
_W_COLS = _C_GR + GLA_WIDTH
_SRC = [sum(IN_SPLITS[:i]) for i in range(len(IN_SPLITS))]
_SRC_FF, _SRC_GA = _SRC[3], _SRC[7]
_W_GROUPS = ((_C_FQ, _SRC[0], FOX_WIDTH), (_C_FK, _SRC[1], FOX_WIDTH), (_C_FV, _SRC[2], FOX_WIDTH),
             (_C_GQ, _SRC[4], GLA_KEY_WIDTH), (_C_GK, _SRC[5], GLA_KEY_WIDTH),
             (_C_GV, _SRC[6], GLA_WIDTH), (_C_GR, _SRC[8], GLA_WIDTH))
_KAUG_W = FOX_PAIRS * MXU_DIM


def _head_indicator():
    ind = np.zeros((FOX_WIDTH, MXU_DIM), np.float32)
    for h in range(FOX_HEADS):
        ind[h * FOX_HEAD_DIM:(h + 1) * FOX_HEAD_DIM, h] = 1.0
    return ind


def _inproj_kernel(x_ref, mod_ref, w_ref, bf_ref, ind_ref, gg_ref, wg32_ref, wu32_ref, wd32_ref, wo32_ref,
                   fqT_ref, kaug_ref, fvT_ref, gq_ref, gk_ref, gv_ref, gr_ref, sm_ref,
                   drel_ref, stats_ref, wg16_ref, wu16_ref, wd16_ref, wo16_ref, carry_ref):
    tm = x_ref.shape[1]

    wg16_ref[...] = wg32_ref[...].astype(BF16)
    wu16_ref[...] = wu32_ref[...].astype(BF16)
    wd16_ref[...] = wd32_ref[...].astype(BF16)
    wo16_ref[...] = wo32_ref[...].astype(BF16)

    @pl.when(pl.program_id(1) == 0)
    def _():
        carry_ref[...] = jnp.zeros_like(carry_ref)

    m = mod_ref[0]
    u = (x_ref[0] * (1.0 + m[1:2]) + m[0:1]).astype(BF16)

    def proj(lo, width):
        return _dot(u, w_ref[:, lo:lo + width])

    gq_small = proj(_C_GQ, 2 * GLA_KEY_WIDTH)
    gq_ref[0] = gq_small[:, :GLA_KEY_WIDTH] * (GLA_KEY_DIM ** -0.5)
    small = gq_small[:, GLA_KEY_WIDTH:GLA_KEY_WIDTH + SMALL_W]
    sm_ref[0] = small

    fq = proj(_C_FQ, FOX_WIDTH) * (FOX_HEAD_DIM ** -0.5 * LOG2E)
    fqT_ref[0] = fq.T.astype(BF16)
    fk32 = proj(_C_FK, FOX_WIDTH)
    fk = fk32.astype(BF16)
    qn2 = jnp.max(_dot((fq * fq).astype(BF16), ind_ref[...])[:, :SMALL_W], axis=0, keepdims=True)
    kn2 = jnp.max(_dot((fk32 * fk32).astype(BF16), ind_ref[...])[:, :SMALL_W], axis=0, keepdims=True)

    zT = (small + bf_ref[...]).T[:FOX_HEADS, :]
    lfT = _log_sigmoid(zT) * LOG2E
    r = lax.broadcasted_iota(jnp.int32, (tm, tm), 0)
    c = lax.broadcasted_iota(jnp.int32, (tm, tm), 1)
    triu = jnp.where(r <= c, 1.0, 0.0).astype(BF16)
    drelT = _tri_right(lfT, triu)
    drel_ref[0] = drelT

    c_before = carry_ref[...]
    srow = lax.broadcasted_iota(jnp.int32, (SUBLANES, SMALL_W), 0)
    slane = lax.broadcasted_iota(jnp.int32, (SUBLANES, SMALL_W), 1)
    total = jnp.sum(jnp.where(srow == slane, drelT[:, tm - 1:tm], 0.0), axis=0, keepdims=True)
    carry_ref[...] = c_before + total

    hi, mid, lo = _split3(drelT)
    ones3 = jnp.where(lax.broadcasted_iota(jnp.int32, (SUBLANES, tm), 0) < 3, 1.0, 0.0)
    slabT = jnp.concatenate(
        [-hi.astype(F32), -mid.astype(F32), -lo.astype(F32), ones3,
         jnp.zeros((LANES - 4 * SUBLANES, tm), F32)], axis=0)
    slab = slabT.T.astype(BF16)

    fvT_ref[0] = proj(_C_FV, FOX_WIDTH).T.astype(BF16)
    gv_ref[0] = proj(_C_GV, GLA_WIDTH).astype(BF16)
    gr = proj(_C_GR, GLA_WIDTH)
    gr_ref[0] = gg_ref[...] * (gr * (1.0 / (1.0 + jnp.exp(-gr))))
    gk_ref[0] = proj(_C_GK, GLA_KEY_WIDTH).T

    stats_ref[0, 0] = jnp.where(srow == 0, c_before,
                                jnp.where(srow == 1, qn2, jnp.where(srow == 2, kn2, 0.0)))
    for p in range(FOX_PAIRS):
        kaug_ref[0, :, p * MXU_DIM:p * MXU_DIM + LANES] = fk[:, p * LANES:(p + 1) * LANES]
        kaug_ref[0, :, p * MXU_DIM + LANES:(p + 1) * MXU_DIM] = slab


def _in_projection(x, mod, w_all, b_f_pad, g_gla, w_gate, w_up, w_down, w_o):
    B, S, D = x.shape
    tm = INPROJ_TM
    n_steps = B * (S // tm)
    steps_b = S // tm
    up_rows = D // n_steps
    down_rows = 2 * D_FF // n_steps
    assert D % n_steps == 0 and up_rows % 16 == 0 and (2 * D_FF) % n_steps == 0 and down_rows % 16 == 0
    up_spec = pl.BlockSpec((up_rows, D_FF), lambda b, s: (b * steps_b + s, 0))
    down_spec = pl.BlockSpec((down_rows, D), lambda b, s: ((b * steps_b + s) // 2, 0))
    wo_rows = w_o.shape[0] // n_steps
    assert w_o.shape[0] % n_steps == 0 and wo_rows % 16 == 0
    wo_spec = pl.BlockSpec((wo_rows, D), lambda b, s: (b * steps_b + s, 0))
    ind = _head_indicator()
    tok = lambda w: pl.BlockSpec((1, tm, w), lambda b, s: (b, s, 0))
    tokT = lambda h: pl.BlockSpec((1, h, tm), lambda b, s: (b, 0, s))
    outs = pl.pallas_call(
        _inproj_kernel,
        grid=(B, S // tm),
        in_specs=[
            tok(D),
            pl.BlockSpec((1, N_MOD, D), lambda b, s: (b, 0, 0)),
            pl.BlockSpec((D, _W_COLS), lambda b, s: (0, 0), pipeline_mode=pl.Buffered(1)),
            pl.BlockSpec((1, SMALL_W), lambda b, s: (0, 0)),
            pl.BlockSpec(ind.shape, lambda b, s: (0, 0)),
            pl.BlockSpec((1, GLA_WIDTH), lambda b, s: (0, 0)),
            up_spec, up_spec, down_spec, wo_spec,
        ],
        out_specs=[
            tokT(FOX_WIDTH), tok(_KAUG_W), tokT(FOX_WIDTH),
            tok(GLA_KEY_WIDTH), tokT(GLA_KEY_WIDTH), tok(GLA_WIDTH), tok(GLA_WIDTH),
            tok(SMALL_W),
            tokT(FOX_HEADS),
            pl.BlockSpec((1, 1, SUBLANES, SMALL_W), lambda b, s: (b, s, 0, 0)),
            up_spec, up_spec, down_spec, wo_spec,
        ],
        out_shape=[
            jax.ShapeDtypeStruct((B, FOX_WIDTH, S), BF16),
            jax.ShapeDtypeStruct((B, S, _KAUG_W), BF16),
            jax.ShapeDtypeStruct((B, FOX_WIDTH, S), BF16),
            jax.ShapeDtypeStruct((B, S, GLA_KEY_WIDTH), F32),
            jax.ShapeDtypeStruct((B, GLA_KEY_WIDTH, S), F32),
            jax.ShapeDtypeStruct((B, S, GLA_WIDTH), BF16),
            jax.ShapeDtypeStruct((B, S, GLA_WIDTH), F32),
            jax.ShapeDtypeStruct((B, S, SMALL_W), F32),
            jax.ShapeDtypeStruct((B, FOX_HEADS, S), F32),
            jax.ShapeDtypeStruct((B, S // tm, SUBLANES, SMALL_W), F32),
            jax.ShapeDtypeStruct(w_gate.shape, BF16),
            jax.ShapeDtypeStruct(w_up.shape, BF16),
            jax.ShapeDtypeStruct(w_down.shape, BF16),
            jax.ShapeDtypeStruct(w_o.shape, BF16),
        ],
        scratch_shapes=[pltpu.VMEM((1, SMALL_W), F32)],
        compiler_params=pltpu.CompilerParams(
            dimension_semantics=("arbitrary", "arbitrary"), vmem_limit_bytes=VMEM_LIMIT),
        name="in_proj",
    )(x, mod, w_all, b_f_pad, jnp.asarray(ind, BF16), g_gla, w_gate, w_up, w_down, w_o)
    return outs


def _fox_kernel(ntiles_ref, nitems_ref, qT_ref, kaug_ref, vT_ref, drel_ref, crep_ref, o_ref,
                qaug_ref, s_ref, p_ref, mx_ref, acc_ref, m_ref, mnext_ref, shift_ref, alpha_ref, keep_ref):
    tq, tk = FOX_TQ, FOX_TK
    half = FOX_HEAD_DIM
    S = kaug_ref.shape[1]
    nq = S // tq
    nh = FOX_GROUP_HEADS
    n_blk = crep_ref.shape[2] // nh
    g = pl.program_id(0) * pl.num_programs(1) + pl.program_id(1)

    def build_qaug(qi):
        q0 = pl.multiple_of(qi * tq, tq)
        dr_all = drel_ref[0, :, pl.ds(q0, tq)]
        dr = dr_all[0:nh]
        for grp in range(1, dr_all.shape[0] // nh):
            dr = jnp.where(pl.program_id(1) == grp, dr_all[grp * nh:(grp + 1) * nh], dr)
        row = lax.broadcasted_iota(jnp.int32, (2 * half, tq), 0)
        r32 = lax.broadcasted_iota(jnp.int32, (AUG_ROWS, tq), 0)
        for h in range(nh):
            buf = (qi % 2) * nh + h
            qT = qT_ref[0, (h // 2) * 2 * half:(h // 2 + 1) * 2 * half, pl.ds(q0, tq)]
            own = (row < half) if h % 2 == 0 else (row >= half)
            qaug_ref[buf, 0:2 * half, :] = jnp.where(own, qT, jnp.zeros_like(qT))
            hi, mid, lo = _split3(dr[h:h + 1])
            gh = nh * pl.program_id(1) + h
            pick = (r32 == gh) | (r32 == gh + SUBLANES) | (r32 == gh + 2 * SUBLANES)
            slab = jnp.where(r32 == 24, hi.astype(F32),
                             jnp.where(r32 == 25, mid.astype(F32),
                                       jnp.where(r32 == 26, lo.astype(F32),
                                                 jnp.where(pick, 1.0, 0.0))))
            qaug_ref[buf, 2 * half:2 * half + AUG_ROWS, :] = slab.astype(BF16)

    def reset_state():
        acc_ref[...] = jnp.zeros_like(acc_ref)

    sum_rows = jnp.where(lax.broadcasted_iota(jnp.int32, (FOX_SUM_ROWS, tk), 0) == 0, 1.0, 0.0).astype(BF16)

    per = tk // tq

    def col_max8(s):
        return jnp.max(s.reshape(s.shape[0] // SUBLANES, SUBLANES, tq), axis=0)

    def tile_of(qi, t):
        raw = (qi + 1) * tq - (t + 1) * tk
        return pl.multiple_of(jnp.maximum(raw, 0), tq), raw < 0

    def qk_dots(qi, k_off):
        kts = [kaug_ref[0, pl.ds(k_off, tk), p * MXU_DIM:(p + 1) * MXU_DIM] for p in range(nh // 2)]
        return [_dot(kts[h // 2], qaug_ref[(qi % 2) * nh + h]) for h in range(nh)]

    def store_scores(scores, diag):
        which = jnp.where(diag, 1, 0)
        for h in range(nh):
            for g in range(per):
                blk = scores[h][g * tq:(g + 1) * tq]
                if g == per - 1:
                    blk = jnp.where(keep_ref[which] > 0.5, blk, NEG_BIG)
                s_ref[h, g * tq:(g + 1) * tq, :] = blk
                mx_ref[per * h + g] = col_max8(blk)

    def fill_upper_groups():
        for g in range(1, per):
            for h in range(nh):
                s_ref[h, g * tq:(g + 1) * tq, :] = jnp.full((tq, tq), NEG_BIG, F32)
                mx_ref[per * h + g] = jnp.full((SUBLANES, tq), NEG_BIG, F32)

    def mask_clamped_diagonal():
        kr = lax.broadcasted_iota(jnp.int32, (tq, tq), 0)
        qc = lax.broadcasted_iota(jnp.int32, (tq, tq), 1)
        for h in range(nh):
            tri = jnp.where(kr <= qc, s_ref[h, 0:tq, :], NEG_BIG)
            s_ref[h, 0:tq, :] = tri
            mx_ref[per * h] = col_max8(tri)
        fill_upper_groups()

    def prep_stage(qi, k_off, first):
        i_blk = qi * tq // INPROJ_TM
        for h in range(nh):
            c_i = crep_ref[0, 0, pl.ds(h * n_blk + i_blk, 1), :]
            big_c = [c_i - crep_ref[0, 0, pl.ds(h * n_blk + (k_off + g * tq) // INPROJ_TM, 1), :]
                     for g in range(per)]
            cand = jnp.max(mx_ref[per * h], axis=0, keepdims=True) + big_c[0]
            for g in range(1, per):
                cand = jnp.maximum(cand, jnp.max(mx_ref[per * h + g], axis=0, keepdims=True) + big_c[g])
            m_prev = jnp.where(first, NEG_BIG, m_ref[h])
            m_new = jnp.maximum(m_prev, cand)
            mnext_ref[h] = m_new
            for g in range(per):
                shift_ref[per * h + g] = m_new - big_c[g]
            alpha_ref[h] = jnp.exp2(m_prev - m_new)

    def exp_stage():
        for h in range(nh):
            for g in range(per):
                rows = slice(g * tq, (g + 1) * tq)
                p_ref[h, rows, :] = jnp.exp2(s_ref[h, rows, :] - shift_ref[per * h + g]).astype(BF16)

    def pv_stage(k_off):
        for h in range(nh):
            vT = jnp.concatenate([vT_ref[0, h * half:(h + 1) * half, pl.ds(k_off, tk)], sum_rows], axis=0)
            acc_ref[h] = alpha_ref[h] * acc_ref[h] + _dot(vT, p_ref[h])
            m_ref[h] = mnext_ref[h]

    def finalize(qi):
        outs = []
        for h in range(nh):
            acc = acc_ref[h]
            outs.append(acc[:half] / acc[half:half + 1])
        oT = jnp.concatenate(outs, axis=0)
        o_ref[0, pl.ds(pl.multiple_of(qi * tq, tq), tq), :] = oT.T.astype(o_ref.dtype)

    def fix_clamped(t):
        @pl.when(t == 0)
        def _():
            mask_clamped_diagonal()

        @pl.when(t > 0)
        def _():
            fill_upper_groups()

    zero = jnp.int32(0)
    qaug_ref[...] = jnp.zeros_like(qaug_ref)
    keep_ref[0] = jnp.ones((tq, tq), F32)
    keep_ref[1] = jnp.where(lax.broadcasted_iota(jnp.int32, (tq, tq), 0)
                            <= lax.broadcasted_iota(jnp.int32, (tq, tq), 1), 1.0, 0.0)
    build_qaug(zero)
    build_qaug(zero + 1)
    reset_state()
    k0, _ = tile_of(zero, zero)
    store_scores(qk_dots(zero, k0), False)
    mask_clamped_diagonal()
    prep_stage(zero, k0, True)

    def body(it, carry):
        qi, t = carry
        last = t == 0
        qi2 = jnp.where(last, qi + 1, qi)
        t2 = jnp.where(last, ntiles_ref[g * nq + jnp.minimum(qi + 1, nq - 1)] - 1, t - 1)
        k_cur, _ = tile_of(qi, t)
        k_nxt, cl_nxt = tile_of(qi2, t2)
        exp_stage()
        nxt = qk_dots(qi2, k_nxt)
        pv_stage(k_cur)
        store_scores(nxt, (t2 == 0) & jnp.logical_not(cl_nxt))
        prep_stage(qi2, k_nxt, last)

        @pl.when(cl_nxt)
        def _():
            fix_clamped(t2)
            prep_stage(qi2, k_nxt, last)

        @pl.when(last)
        def _():
            finalize(qi)
            reset_state()

            @pl.when(qi + 2 < nq)
            def _():
                build_qaug(qi + 2)

        return qi2, t2

    qi, t = lax.fori_loop(0, nitems_ref[g] - 1, body, (jnp.int32(0), jnp.int32(0)))
    exp_stage()
    pv_stage(tile_of(qi, t)[0])
    finalize(qi)


def _prune_plan(stats, drel, nq):
    B, n_blk = stats.shape[:2]
    per = FOX_TK // FOX_TQ
    c = stats[:, :, 0, :FOX_HEADS]
    qn = jnp.sqrt(stats[:, :, 1, :FOX_HEADS])
    kn = jnp.sqrt(stats[:, :, 2, :FOX_HEADS])
    d_in = jnp.transpose(drel[:, :, FOX_TQ - 1::FOX_TQ], (0, 2, 1))
    d_end = jnp.repeat(c, per, axis=1) + d_in
    d_before = jnp.concatenate([jnp.zeros_like(d_end[:, :1]), d_end[:, :-1]], axis=1)
    kn_q = jnp.repeat(kn, per, axis=1)
    qn_q = jnp.repeat(qn, per, axis=1)
    kn_prev = jnp.concatenate([jnp.zeros_like(kn_q[:, :1]), kn_q[:, :-1]], axis=1)
    kn_tile = jnp.maximum(kn_q, kn_prev)
    bound = (d_before[:, :, None, :] - d_end[:, None, :, :]
             + NORM_SLACK * qn_q[:, :, None, :] * (kn_tile[:, None, :, :] + kn_q[:, :, None, :]))
    back = np.arange(nq)[:, None] - np.arange(nq)[None, :]
    is_tile = (back > 0) & (back % per == 0)
    t_of = np.where(is_tile, back // per, 0).astype(np.int32)
    needed = is_tile[None, :, :, None] & ~(bound <= PRUNE_LOG2)
    nh = FOX_GROUP_HEADS
    needed_g = jnp.any(needed.reshape(B, nq, nq, FOX_HEADS // nh, nh), axis=-1)
    n_tiles = 1 + jnp.max(jnp.where(needed_g, t_of[None, :, :, None], 0), axis=2)
    n_tiles = jnp.transpose(n_tiles, (0, 2, 1))
    n_items = jnp.sum(n_tiles, axis=2)
    return n_tiles.reshape(-1).astype(jnp.int32), n_items.reshape(-1).astype(jnp.int32)


def _fox_attention(fqT, kaug, fvT, drel, stats):
    B, _, S = fqT.shape
    tq, tk = FOX_TQ, FOX_TK
    n_blk = S // INPROJ_TM
    nh = FOX_GROUP_HEADS
    n_groups = FOX_HEADS // nh
    c = stats[:, :, 0, :FOX_HEADS]
    c = jnp.transpose(c, (0, 2, 1)).reshape(B, n_groups, nh * n_blk, 1)
    crep = jnp.broadcast_to(c, (B, n_groups, nh * n_blk, tq))
    n_tiles, n_items = _prune_plan(stats, drel, S // tq)
    grid_spec = pltpu.PrefetchScalarGridSpec(
        num_scalar_prefetch=2,
        grid=(B, n_groups),
        in_specs=[
            pl.BlockSpec((1, nh * FOX_HEAD_DIM, S), lambda b, p, js, ni: (b, p, 0)),
            pl.BlockSpec((1, S, nh // 2 * MXU_DIM), lambda b, p, js, ni: (b, 0, p)),
            pl.BlockSpec((1, nh * FOX_HEAD_DIM, S), lambda b, p, js, ni: (b, p, 0)),
            pl.BlockSpec((1, FOX_HEADS, S), lambda b, p, js, ni: (b, 0, 0)),
            pl.BlockSpec((1, 1, nh * n_blk, tq), lambda b, p, js, ni: (b, p, 0, 0)),
        ],
        out_specs=pl.BlockSpec((1, S, nh * FOX_HEAD_DIM), lambda b, p, js, ni: (b, 0, p)),
        scratch_shapes=[
            pltpu.VMEM((2 * nh, MXU_DIM, tq), BF16),
            pltpu.VMEM((nh, tk, tq), F32),
            pltpu.VMEM((nh, tk, tq), BF16),
            pltpu.VMEM((nh * (tk // tq), SUBLANES, tq), F32),
            pltpu.VMEM((nh, FOX_HEAD_DIM + FOX_SUM_ROWS, tq), F32),
            pltpu.VMEM((nh, 1, tq), F32),
            pltpu.VMEM((nh, 1, tq), F32),
            pltpu.VMEM((nh * (tk // tq), 1, tq), F32),
            pltpu.VMEM((nh, 1, tq), F32),
            pltpu.VMEM((2, tq, tq), F32),
        ])
    return pl.pallas_call(
        _fox_kernel,
        grid_spec=grid_spec,
        out_shape=jax.ShapeDtypeStruct((B, S, FOX_WIDTH), BF16),
        compiler_params=pltpu.CompilerParams(
            dimension_semantics=("arbitrary", "arbitrary"),
            vmem_limit_bytes=VMEM_LIMIT),
        name="fox_attn",
    )(n_tiles, n_items, fqT, kaug, fvT, drel, crep)


def _gla_kernel(q_ref, kT_ref, v_ref, gate_ref, sm_ref, wa_ref, ba_ref, o_ref, state_ref):
    nb, ts = q_ref.shape[0], q_ref.shape[1]
    C, P = GLA_CHUNK, GLA_PAIR
    dk, dv, H = GLA_KEY_DIM, GLA_VAL_DIM, GLA_HEADS
    KW = GLA_KEY_WIDTH

    @pl.when(pl.program_id(0) == 0)
    def _():
        state_ref[...] = jnp.zeros_like(state_ref)

    la_all = [_log_sigmoid(_dot(sm_ref[bb].astype(BF16), wa_ref[...]) + ba_ref[...])
              * (1.0 / GLA_GATE_TEMP) for bb in range(nb)]

    ri = lax.broadcasted_iota(jnp.int32, (P, P), 0)
    ci = lax.broadcasted_iota(jnp.int32, (P, P), 1)
    tril2 = jnp.where(((ri < C) == (ci < C)) & (ci <= ri), 1.0, 0.0).astype(BF16)

    rs = lax.broadcasted_iota(jnp.int32, (2 * H * C, P), 0)
    cs = lax.broadcasted_iota(jnp.int32, (2 * H * C, P), 1)
    causal = ((rs >= H * C) == (cs >= C)) & ((cs & (C - 1)) <= (rs & (C - 1)))

    lane_q = lax.broadcasted_iota(jnp.int32, (C, KW), 1)
    lane_t = lax.broadcasted_iota(jnp.int32, (KW, P), 1)
    first = lane_t < C

    per_b = ts // P
    n_slabs = nb * per_b
    slab_b = [k // per_b for k in range(n_slabs)]
    slab_rows = [slice((k % per_b) * P, (k % per_b + 1) * P) for k in range(n_slabs)]
    bs = [_tri_left(tril2, la_all[slab_b[k]][slab_rows[k]]) for k in range(n_slabs)]

    qms, scores, kv0s, kv1s, dec0s, dec1s, vss = [], [], [], [], [], [], []
    for pi in range(n_slabs):
        bb, rows = slab_b[pi], slab_rows[pi]
        b = bs[pi]
        bT = b.T
        bl0 = bT[:, C - 1:C]
        bl1 = bT[:, P - 1:P]
        q_dec = q_ref[bb, rows, :] * jnp.exp(b)
        kT = kT_ref[bb, :, rows]
        k_invT = (kT * jnp.exp(-bT)).astype(BF16)
        k_teT = kT * jnp.exp(jnp.where(first, bl0, bl1) - bT)
        zt = jnp.zeros_like(k_teT)
        k_te0 = jnp.where(first, k_teT, zt).astype(BF16)
        k_te1 = jnp.where(first, zt, k_teT).astype(BF16)
        dec0 = jnp.exp(bl0)
        dec1 = jnp.exp(bl1)

        vs = [v_ref[bb, rows, h * dv:(h + 1) * dv] for h in range(H)]
        kv = [_dot(jnp.concatenate([k_te0[h * dk:(h + 1) * dk, :], k_te1[h * dk:(h + 1) * dk, :]], axis=0),
                   vs[h]) for h in range(H)]
        kv0s.append(jnp.concatenate([kv[h][:dk] for h in range(H)], axis=0))
        kv1s.append(jnp.concatenate([kv[h][dk:] for h in range(H)], axis=0))
        dec0s.append(dec0)
        dec1s.append(dec1)
        vss.append(vs)

        zq = jnp.zeros((C, KW), F32)
        qm = [jnp.concatenate(
            [jnp.where((lane_q >= h * dk) & (lane_q < (h + 1) * dk), q_dec[c * C:(c + 1) * C], zq)
             for h in range(H)], axis=0).astype(BF16) for c in range(2)]
        a = _dot(jnp.concatenate(qm, axis=0), k_invT)
        scores.append(jnp.where(causal, a, 0.0).astype(BF16))
        qms.append(qm)

    o_inters = []
    for pi in range(n_slabs):
        if pi % per_b == 0:
            s = state_ref[slab_b[pi]]
        s1 = dec0s[pi] * s + kv0s[pi]
        o_inters.append([_dot(qms[pi][0], s.astype(BF16)), _dot(qms[pi][1], s1.astype(BF16))])
        s = dec1s[pi] * s1 + kv1s[pi]
        if pi % per_b == per_b - 1:
            state_ref[slab_b[pi]] = s

    for pi in range(n_slabs):
        bb, rows = slab_b[pi], slab_rows[pi]
        a, o_inter, vs = scores[pi], o_inters[pi], vss[pi]
        for h in range(H):
            hv = slice(h * dv, (h + 1) * dv)
            a_h = jnp.concatenate([a[(c * H + h) * C:(c * H + h + 1) * C] for c in range(2)], axis=0)
            o = _dot(a_h, vs[h]) + jnp.concatenate(
                [o_inter[c][h * C:(h + 1) * C] for c in range(2)], axis=0)
            o = o * lax.rsqrt(jnp.mean(o * o, axis=-1, keepdims=True) + RMS_EPS)
            o_ref[bb, rows, hv] = (o * gate_ref[bb, rows, hv]).astype(o_ref.dtype)


def _gla_attention(gq, gkT, gv, gate, small, w_a2_pad, b_a):
    B, S, _ = gq.shape
    ts = GLA_TS
    tok = lambda w: pl.BlockSpec((B, ts, w), lambda s: (0, s, 0))
    full = lambda r, c: pl.BlockSpec((r, c), lambda s: (0, 0))
    return pl.pallas_call(
        _gla_kernel,
        grid=(S // ts,),
        in_specs=[
            tok(GLA_KEY_WIDTH), pl.BlockSpec((B, GLA_KEY_WIDTH, ts), lambda s: (0, 0, s)),
            tok(GLA_WIDTH), tok(GLA_WIDTH), tok(SMALL_W),
            full(SMALL_W, GLA_KEY_WIDTH), full(1, GLA_KEY_WIDTH),
        ],
        out_specs=tok(GLA_WIDTH),
        out_shape=jax.ShapeDtypeStruct((B, S, GLA_WIDTH), BF16),
        scratch_shapes=[pltpu.VMEM((B, GLA_KEY_WIDTH, GLA_VAL_DIM), F32)],
        compiler_params=pltpu.CompilerParams(
            dimension_semantics=("arbitrary",), vmem_limit_bytes=VMEM_LIMIT),
        name="gla_attn",
    )(gq, gkT, gv, gate, small, w_a2_pad, b_a)


def _layer_norm(z, g, b):
    mu = jnp.mean(z, axis=-1, keepdims=True)
    d = z - mu
    var = jnp.mean(d * d, axis=-1, keepdims=True)
    return d * lax.rsqrt(var + LN_EPS) * g + b


def _tail_kernel(x_ref, fox_ref, gla_ref, mod_ref, wof_ref, wog_ref, ln1g_ref, ln1b_ref,
                 wg_ref, wu_ref, wd_ref, ln2g_ref, ln2b_ref, o_ref):
    m = mod_ref[0]
    tm = x_ref.shape[0]
    halves = [slice(0, tm // 2), slice(tm // 2, tm)]
    ys = [_dot(fox_ref[r, :], wof_ref[...]) + _dot(gla_ref[r, :], wog_ref[...]) for r in halves]
    x1s, gs, ups = [], [], []
    for r, y in zip(halves, ys):
        x1 = _layer_norm(DEEPNORM_ALPHA * x_ref[r, :] + (1.0 + m[2:3]) * y,
                         ln1g_ref[...], ln1b_ref[...])
        u2 = (x1 * (1.0 + m[4:5]) + m[3:4]).astype(BF16)
        x1s.append(x1)
        gs.append(_dot(u2, wg_ref[...]))
        ups.append(_dot(u2, wu_ref[...]))
    y2s = []
    for g, up in zip(gs, ups):
        h = (g * (1.0 / (1.0 + jnp.exp(-g))) * up).astype(BF16)
        y2s.append(_dot(h, wd_ref[...]))
    for r, x1, y2 in zip(halves, x1s, y2s):
        o_ref[r, :] = _layer_norm(DEEPNORM_ALPHA * x1 + (1.0 + m[5:6]) * y2,
                                  ln2g_ref[...], ln2b_ref[...])


def _tail(x2d, fox2d, gla2d, mod, wo, ln1g, ln1b, wg, wu, wd, ln2g, ln2b, steps_per_batch):
    T, D = x2d.shape
    tm = TAIL_TM
    const = lambda r, c: pl.BlockSpec((r, c), lambda i: (0, 0), pipeline_mode=pl.Buffered(1))
    tok = lambda w: pl.BlockSpec((tm, w), lambda i: (i, 0))
    return pl.pallas_call(
        _tail_kernel,
        grid=(T // tm,),
        in_specs=[
            tok(D), tok(FOX_WIDTH), tok(GLA_WIDTH),
            pl.BlockSpec((1, N_MOD, D), lambda i: (i // steps_per_batch, 0, 0)),
            pl.BlockSpec((FOX_WIDTH, D), lambda i: (0, 0), pipeline_mode=pl.Buffered(1)),
            pl.BlockSpec((GLA_WIDTH, D), lambda i: (1, 0), pipeline_mode=pl.Buffered(1)),
            const(1, D), const(1, D),
            const(D, D_FF), const(D, D_FF), const(D_FF, D), const(1, D), const(1, D),
        ],
        out_specs=tok(D),
        out_shape=jax.ShapeDtypeStruct((T, D), F32),
        compiler_params=pltpu.CompilerParams(
            dimension_semantics=("arbitrary",), vmem_limit_bytes=VMEM_LIMIT),
        name="tail",
    )(x2d, fox2d, gla2d, mod, wo, wo, ln1g, ln1b, wg, wu, wd, ln2g, ln2b)


def kernel(x, c, w_c, b_c, w_in, b_f, w_a2, b_a, g_gla, w_o, ln1_g, ln1_b,
           w_gate, w_up, w_down, ln2_g, ln2_b):
    B, S, D = x.shape
    assert D == D_MODEL and S % INPROJ_TM == 0 and INPROJ_TM == FOX_TK and FOX_TK % FOX_TQ == 0

    b_f_pad = jnp.pad(b_f.reshape(1, FOX_HEADS), ((0, 0), (0, SMALL_W - FOX_HEADS)))
    w_a2_pad = jnp.pad(w_a2, ((FOX_HEADS, SMALL_W - FOX_HEADS - GLA_GATE_RANK), (0, 0))).astype(BF16)

    mod, w_all = _modulation(c, w_c, b_c, w_in)

    fqT, kaug, fvT, gq, gkT, gv, gate, small, drel, stats, wg16, wu16, wd16, wo16 = _in_projection(
        x, mod, w_all, b_f_pad, g_gla.reshape(1, -1), w_gate, w_up, w_down, w_o)
    fox = _fox_attention(fqT, kaug, fvT, drel, stats)
    gla = _gla_attention(gq, gkT, gv, gate, small, w_a2_pad, b_a.reshape(1, -1))

    T = B * S
    out = _tail(x.reshape(T, D), fox.reshape(T, FOX_WIDTH), gla.reshape(T, GLA_WIDTH), mod,
                wo16,
                ln1_g.reshape(1, D), ln1_b.reshape(1, D),
                wg16, wu16, wd16,
                ln2_g.reshape(1, D), ln2_b.reshape(1, D), S // TAIL_TM)
    return out.reshape(B, S, D)
```

```python
import functools
import math

import numpy as np
import jax
import jax.numpy as jnp
from jax import lax
from jax.experimental import pallas as pl
from jax.experimental.pallas import tpu as pltpu

F32 = jnp.float32
BF16 = jnp.bfloat16

D_MODEL = 1024
FOX_WIDTH = 512
FOX_HEAD_DIM = 64
FOX_HEADS = 8
FOX_PAIRS = FOX_HEADS // 2
GLA_WIDTH = 512
GLA_HEADS = 4
GLA_VAL_DIM = 128
GLA_KEY_DIM = 64
GLA_KEY_WIDTH = 256
GLA_GATE_RANK = 16
GLA_GATE_TEMP = 16.0
GLA_CHUNK = 64
D_FF = 2816
LN_EPS = 1e-5
RMS_EPS = 1e-6
N_MOD = 6
DEEPNORM_ALPHA = 2.0 ** 0.25
IN_SPLITS = (FOX_WIDTH, FOX_WIDTH, FOX_WIDTH, FOX_HEADS, GLA_KEY_WIDTH, GLA_KEY_WIDTH,
             GLA_WIDTH, GLA_GATE_RANK, GLA_WIDTH)
LOG2E = math.log2(math.e)

LANES = 128
SUBLANES = 8
MXU_DIM = 256
SMALL_W = LANES
VMEM_LIMIT = 56 * 1024 * 1024

MOD_TN = 768
INPROJ_TM = 512
FOX_TQ = 256
FOX_GROUP_HEADS = 4
FOX_SUM_ROWS = 16
FOX_TK = 512
GLA_TS = 512
GLA_PAIR = 2 * GLA_CHUNK
TAIL_TM = 512

NEG_BIG = -1e30
PRUNE_LOG2 = -150.0
NORM_SLACK = 1.02

AUG_ROWS = 32


def _log_sigmoid(z):
    return jnp.minimum(z, 0.0) - jnp.log(1.0 + jnp.exp(-jnp.abs(z)))


def _split3(a):
    hi = a.astype(BF16)
    r1 = a - hi.astype(F32)
    mid = r1.astype(BF16)
    lo = (r1 - mid.astype(F32)).astype(BF16)
    return hi, mid, lo


def _dot(a, b):
    return jnp.dot(a, b, preferred_element_type=F32)


def _tri_left(tri, a):
    hi, mid, lo = _split3(a)
    return _dot(tri, hi) + _dot(tri, mid) + _dot(tri, lo)


def _tri_right(a, tri):
    hi, mid, lo = _split3(a)
    return _dot(hi, tri) + _dot(mid, tri) + _dot(lo, tri)


def _regroup_blocks():
    blocks = []
    for dst, src, width in _W_GROUPS:
        blocks += [(dst + c0, src + c0) for c0 in range(0, width, MXU_DIM)]
    blocks.append((_C_SM, None))
    return blocks


def _mod_kernel(cT_ref, w_ref, b_ref, winT_ref, o_ref, wall_ref, *, n_rows):
    w = w_ref[...]
    outs = [jnp.sum(w * cT_ref[:, b:b + 1], axis=0, keepdims=True) for b in range(n_rows)]
    outs.append(jnp.zeros((o_ref.shape[0] - n_rows, w.shape[1]), F32))
    o_ref[...] = jnp.concatenate(outs, axis=0) + b_ref[...]

    j = pl.program_id(0)
    n_steps = pl.num_programs(0)
    for k, (dst, src) in enumerate(_regroup_blocks()):
        @pl.when(j == k % n_steps)
        def _():
            if src is None:
                blk = jnp.concatenate(
                    [winT_ref[_SRC_FF:_SRC_FF + FOX_HEADS, :], winT_ref[_SRC_GA:_SRC_GA + GLA_GATE_RANK, :],
                     jnp.zeros((MXU_DIM - FOX_HEADS - GLA_GATE_RANK, winT_ref.shape[1]), F32)], axis=0)
            else:
                blk = winT_ref[src:src + MXU_DIM, :]
            wall_ref[:, dst:dst + MXU_DIM] = blk.T.astype(BF16)


def _modulation(c, w_c, b_c, w_in):
    B, D = c.shape
    N = w_c.shape[1]
    rows = SUBLANES
    n_steps = N // MOD_TN
    cT_pad = jnp.pad(c.T, ((0, 0), (0, LANES - B)))
    w_inT = w_in.T
    out, w_all = pl.pallas_call(
        functools.partial(_mod_kernel, n_rows=B),
        grid=(n_steps,),
        in_specs=[
            pl.BlockSpec((D, LANES), lambda j: (0, 0)),
            pl.BlockSpec((D, MOD_TN), lambda j: (0, j)),
            pl.BlockSpec((1, MOD_TN), lambda j: (0, j)),
            pl.BlockSpec(w_inT.shape, lambda j: (0, 0), pipeline_mode=pl.Buffered(1)),
        ],
        out_specs=[pl.BlockSpec((rows, MOD_TN), lambda j: (0, j)),
                   pl.BlockSpec((D, _W_COLS), lambda j: (0, 0))],
        out_shape=[jax.ShapeDtypeStruct((rows, N), F32),
                   jax.ShapeDtypeStruct((D, _W_COLS), BF16)],
        compiler_params=pltpu.CompilerParams(
            dimension_semantics=("arbitrary",), vmem_limit_bytes=VMEM_LIMIT),
        name="adaln_mod",
    )(cT_pad, w_c, b_c.reshape(1, N), w_inT)
    return out[:B].reshape(B, N_MOD, D), w_all


_C_FQ, _C_FK, _C_FV = 0, 512, 1024
_C_GQ, _C_SM, _C_GK, _C_GV, _C_GR = 1536, 1792, 2048, 2304, 2816
_W_COLS = _C_GR + GLA_WIDTH
_SRC = [sum(IN_SPLITS[:i]) for i in range(len(IN_SPLITS))]
_SRC_FF, _SRC_GA = _SRC[3], _SRC[7]
_W_GROUPS = ((_C_FQ, _SRC[0], FOX_WIDTH), (_C_FK, _SRC[1], FOX_WIDTH), (_C_FV, _SRC[2], FOX_WIDTH),
             (_C_GQ, _SRC[4], GLA_KEY_WIDTH), (_C_GK, _SRC[5], GLA_KEY_WIDTH),
             (_C_GV, _SRC[6], GLA_WIDTH), (_C_GR, _SRC[8], GLA_WIDTH))
_KAUG_W = FOX_PAIRS * MXU_DIM


def _head_indicator():
    ind = np.zeros((FOX_WIDTH, MXU_DIM), np.float32)
    for h in range(FOX_HEADS):
        ind[h * FOX_HEAD_DIM:(h + 1) * FOX_HEAD_DIM, h] = 1.0
    return ind


def _inproj_kernel(x_ref, mod_ref, w_ref, bf_ref, ind_ref, gg_ref, wg32_ref, wu32_ref, wd32_ref, wo32_ref,
                   fqT_ref, kaug_ref, fvT_ref, gq_ref, gk_ref, gv_ref, gr_ref, sm_ref,
                   drel_ref, stats_ref, wg16_ref, wu16_ref, wd16_ref, wo16_ref, carry_ref):
    tm = x_ref.shape[1]

    wg16_ref[...] = wg32_ref[...].astype(BF16)
    wu16_ref[...] = wu32_ref[...].astype(BF16)
    wd16_ref[...] = wd32_ref[...].astype(BF16)
    wo16_ref[...] = wo32_ref[...].astype(BF16)

    @pl.when(pl.program_id(1) == 0)
    def _():
        carry_ref[...] = jnp.zeros_like(carry_ref)

    m = mod_ref[0]
    u = (x_ref[0] * (1.0 + m[1:2]) + m[0:1]).astype(BF16)

    def proj(lo, width):
        return _dot(u, w_ref[:, lo:lo + width])

    gq_small = proj(_C_GQ, 2 * GLA_KEY_WIDTH)
    gq_ref[0] = gq_small[:, :GLA_KEY_WIDTH] * (GLA_KEY_DIM ** -0.5)
    small = gq_small[:, GLA_KEY_WIDTH:GLA_KEY_WIDTH + SMALL_W]
    sm_ref[0] = small

    fq = proj(_C_FQ, FOX_WIDTH) * (FOX_HEAD_DIM ** -0.5 * LOG2E)
    fqT_ref[0] = fq.T.astype(BF16)
    fk32 = proj(_C_FK, FOX_WIDTH)
    fk = fk32.astype(BF16)
    qn2 = jnp.max(_dot((fq * fq).astype(BF16), ind_ref[...])[:, :SMALL_W], axis=0, keepdims=True)
    kn2 = jnp.max(_dot((fk32 * fk32).astype(BF16), ind_ref[...])[:, :SMALL_W], axis=0, keepdims=True)

    zT = (small + bf_ref[...]).T[:FOX_HEADS, :]
    lfT = _log_sigmoid(zT) * LOG2E
    r = lax.broadcasted_iota(jnp.int32, (tm, tm), 0)
    c = lax.broadcasted_iota(jnp.int32, (tm, tm), 1)
    triu = jnp.where(r <= c, 1.0, 0.0).astype(BF16)
    drelT = _tri_right(lfT, triu)
    drel_ref[0] = drelT

    c_before = carry_ref[...]
    srow = lax.broadcasted_iota(jnp.int32, (SUBLANES, SMALL_W), 0)
    slane = lax.broadcasted_iota(jnp.int32, (SUBLANES, SMALL_W), 1)
    total = jnp.sum(jnp.where(srow == slane, drelT[:, tm - 1:tm], 0.0), axis=0, keepdims=True)
    carry_ref[...] = c_before + total

    hi, mid, lo = _split3(drelT)
    ones3 = jnp.where(lax.broadcasted_iota(jnp.int32, (SUBLANES, tm), 0) < 3, 1.0, 0.0)
    slabT = jnp.concatenate(
        [-hi.astype(F32), -mid.astype(F32), -lo.astype(F32), ones3,
         jnp.zeros((LANES - 4 * SUBLANES, tm), F32)], axis=0)
    slab = slabT.T.astype(BF16)

    fvT_ref[0] = proj(_C_FV, FOX_WIDTH).T.astype(BF16)
    gv_ref[0] = proj(_C_GV, GLA_WIDTH).astype(BF16)
    gr = proj(_C_GR, GLA_WIDTH)
    gr_ref[0] = gg_ref[...] * (gr * (1.0 / (1.0 + jnp.exp(-gr))))
    gk_ref[0] = proj(_C_GK, GLA_KEY_WIDTH).T

    stats_ref[0, 0] = jnp.where(srow == 0, c_before,
                                jnp.where(srow == 1, qn2, jnp.where(srow == 2, kn2, 0.0)))
    for p in range(FOX_PAIRS):
        kaug_ref[0, :, p * MXU_DIM:p * MXU_DIM + LANES] = fk[:, p * LANES:(p + 1) * LANES]
        kaug_ref[0, :, p * MXU_DIM + LANES:(p + 1) * MXU_DIM] = slab


def _in_projection(x, mod, w_all, b_f_pad, g_gla, w_gate, w_up, w_down, w_o):
    B, S, D = x.shape
    tm = INPROJ_TM
    n_steps = B * (S // tm)
    steps_b = S // tm
    up_rows = D // n_steps
    down_rows = 2 * D_FF // n_steps
    assert D % n_steps == 0 and up_rows % 16 == 0 and (2 * D_FF) % n_steps == 0 and down_rows % 16 == 0
    up_spec = pl.BlockSpec((up_rows, D_FF), lambda b, s: (b * steps_b + s, 0))
    down_spec = pl.BlockSpec((down_rows, D), lambda b, s: ((b * steps_b + s) // 2, 0))
    wo_rows = w_o.shape[0] // n_steps
    assert w_o.shape[0] % n_steps == 0 and wo_rows % 16 == 0
    wo_spec = pl.BlockSpec((wo_rows, D), lambda b, s: (b * steps_b + s, 0))
    ind = _head_indicator()
    tok = lambda w: pl.BlockSpec((1, tm, w), lambda b, s: (b, s, 0))
    tokT = lambda h: pl.BlockSpec((1, h, tm), lambda b, s: (b, 0, s))
    outs = pl.pallas_call(
        _inproj_kernel,
        grid=(B, S // tm),
        in_specs=[
            tok(D),
            pl.BlockSpec((1, N_MOD, D), lambda b, s: (b, 0, 0)),
            pl.BlockSpec((D, _W_COLS), lambda b, s: (0, 0), pipeline_mode=pl.Buffered(1)),
            pl.BlockSpec((1, SMALL_W), lambda b, s: (0, 0)),
            pl.BlockSpec(ind.shape, lambda b, s: (0, 0)),
            pl.BlockSpec((1, GLA_WIDTH), lambda b, s: (0, 0)),
            up_spec, up_spec, down_spec, wo_spec,
        ],
        out_specs=[
            tokT(FOX_WIDTH), tok(_KAUG_W), tokT(FOX_WIDTH),
            tok(GLA_KEY_WIDTH), tokT(GLA_KEY_WIDTH), tok(GLA_WIDTH), tok(GLA_WIDTH),
            tok(SMALL_W),
            tokT(FOX_HEADS),
            pl.BlockSpec((1, 1, SUBLANES, SMALL_W), lambda b, s: (b, s, 0, 0)),
            up_spec, up_spec, down_spec, wo_spec,
        ],
        out_shape=[
            jax.ShapeDtypeStruct((B, FOX_WIDTH, S), BF16),
            jax.ShapeDtypeStruct((B, S, _KAUG_W), BF16),
            jax.ShapeDtypeStruct((B, FOX_WIDTH, S), BF16),
            jax.ShapeDtypeStruct((B, S, GLA_KEY_WIDTH), F32),
            jax.ShapeDtypeStruct((B, GLA_KEY_WIDTH, S), F32),
            jax.ShapeDtypeStruct((B, S, GLA_WIDTH), BF16),
            jax.ShapeDtypeStruct((B, S, GLA_WIDTH), F32),
            jax.ShapeDtypeStruct((B, S, SMALL_W), F32),
            jax.ShapeDtypeStruct((B, FOX_HEADS, S), F32),
            jax.ShapeDtypeStruct((B, S // tm, SUBLANES, SMALL_W), F32),
            jax.ShapeDtypeStruct(w_gate.shape, BF16),
            jax.ShapeDtypeStruct(w_up.shape, BF16),
            jax.ShapeDtypeStruct(w_down.shape, BF16),
            jax.ShapeDtypeStruct(w_o.shape, BF16),
        ],
        scratch_shapes=[pltpu.VMEM((1, SMALL_W), F32)],
        compiler_params=pltpu.CompilerParams(
            dimension_semantics=("arbitrary", "arbitrary"), vmem_limit_bytes=VMEM_LIMIT),
        name="in_proj",
    )(x, mod, w_all, b_f_pad, jnp.asarray(ind, BF16), g_gla, w_gate, w_up, w_down, w_o)
    return outs


def _fox_kernel(ntiles_ref, nitems_ref, qT_ref, kaug_ref, vT_ref, drel_ref, crep_ref, o_ref,
                qaug_ref, s_ref, p_ref, mx_ref, acc_ref, m_ref, mnext_ref, shift_ref, alpha_ref, keep_ref, done_ref):
    tq, tk = FOX_TQ, FOX_TK
    half = FOX_HEAD_DIM
    S = kaug_ref.shape[1]
    nq = S // tq
    nh = FOX_GROUP_HEADS
    n_blk = crep_ref.shape[2] // nh
    g = pl.program_id(0) * pl.num_programs(1) + pl.program_id(1)

    def build_qaug(qi):
        q0 = pl.multiple_of(qi * tq, tq)
        dr_all = drel_ref[0, :, pl.ds(q0, tq)]
        dr = dr_all[0:nh]
        for grp in range(1, dr_all.shape[0] // nh):
            dr = jnp.where(pl.program_id(1) == grp, dr_all[grp * nh:(grp + 1) * nh], dr)
        row = lax.broadcasted_iota(jnp.int32, (2 * half, tq), 0)
        r32 = lax.broadcasted_iota(jnp.int32, (AUG_ROWS, tq), 0)
        for h in range(nh):
            buf = (qi % 2) * nh + h
            qT = qT_ref[0, (h // 2) * 2 * half:(h // 2 + 1) * 2 * half, pl.ds(q0, tq)]
            own = (row < half) if h % 2 == 0 else (row >= half)
            qaug_ref[buf, 0:2 * half, :] = jnp.where(own, qT, jnp.zeros_like(qT))
            hi, mid, lo = _split3(dr[h:h + 1])
            gh = nh * pl.program_id(1) + h
            pick = (r32 == gh) | (r32 == gh + SUBLANES) | (r32 == gh + 2 * SUBLANES)
            slab = jnp.where(r32 == 24, hi.astype(F32),
                             jnp.where(r32 == 25, mid.astype(F32),
                                       jnp.where(r32 == 26, lo.astype(F32),
                                                 jnp.where(pick, 1.0, 0.0))))
            qaug_ref[buf, 2 * half:2 * half + AUG_ROWS, :] = slab.astype(BF16)

    def reset_state():
        acc_ref[...] = jnp.zeros_like(acc_ref)

    sum_rows = jnp.where(lax.broadcasted_iota(jnp.int32, (FOX_SUM_ROWS, tk), 0) == 0, 1.0, 0.0).astype(BF16)

    per = tk // tq

    def col_max8(s):
        return jnp.max(s.reshape(s.shape[0] // SUBLANES, SUBLANES, tq), axis=0)

    def tile_of(qi, t):
        raw = (qi + 1) * tq - (t + 1) * tk
        return pl.multiple_of(jnp.maximum(raw, 0), tq), raw < 0

    def qk_dots(qi, k_off):
        kts = [kaug_ref[0, pl.ds(k_off, tk), p * MXU_DIM:(p + 1) * MXU_DIM] for p in range(nh // 2)]
        return [_dot(kts[h // 2], qaug_ref[(qi % 2) * nh + h]) for h in range(nh)]

    def store_scores(scores, diag):
        which = jnp.where(diag, 1, 0)
        for h in range(nh):
            for g in range(per):
                blk = scores[h][g * tq:(g + 1) * tq]
                if g == per - 1:
                    blk = jnp.where(keep_ref[which] > 0.5, blk, NEG_BIG)
                s_ref[h, g * tq:(g + 1) * tq, :] = blk
                mx_ref[per * h + g] = col_max8(blk)

    def fill_upper_groups():
        for g in range(1, per):
            for h in range(nh):
                s_ref[h, g * tq:(g + 1) * tq, :] = jnp.full((tq, tq), NEG_BIG, F32)
                mx_ref[per * h + g] = jnp.full((SUBLANES, tq), NEG_BIG, F32)

    def mask_clamped_diagonal():
        kr = lax.broadcasted_iota(jnp.int32, (tq, tq), 0)
        qc = lax.broadcasted_iota(jnp.int32, (tq, tq), 1)
        for h in range(nh):
            tri = jnp.where(kr <= qc, s_ref[h, 0:tq, :], NEG_BIG)
            s_ref[h, 0:tq, :] = tri
            mx_ref[per * h] = col_max8(tri)
        fill_upper_groups()

    def prep_stage(qi, k_off, first):
        i_blk = qi * tq // INPROJ_TM
        for h in range(nh):
            c_i = crep_ref[0, 0, pl.ds(h * n_blk + i_blk, 1), :]
            big_c = [c_i - crep_ref[0, 0, pl.ds(h * n_blk + (k_off + g * tq) // INPROJ_TM, 1), :]
                     for g in range(per)]
            cand = jnp.max(mx_ref[per * h], axis=0, keepdims=True) + big_c[0]
            for g in range(1, per):
                cand = jnp.maximum(cand, jnp.max(mx_ref[per * h + g], axis=0, keepdims=True) + big_c[g])
            m_prev = jnp.where(first, NEG_BIG, m_ref[h])
            m_new = jnp.maximum(m_prev, cand)
            mnext_ref[h] = m_new
            for g in range(per):
                shift_ref[per * h + g] = m_new - big_c[g]
            alpha_ref[h] = jnp.exp2(m_prev - m_new)

    def exp_stage():
        for h in range(nh):
            for g in range(per):
                rows = slice(g * tq, (g + 1) * tq)
                p_ref[h, rows, :] = jnp.exp2(s_ref[h, rows, :] - shift_ref[per * h + g]).astype(BF16)

    def pv_stage(k_off):
        for h in range(nh):
            vT = jnp.concatenate([vT_ref[0, h * half:(h + 1) * half, pl.ds(k_off, tk)], sum_rows], axis=0)
            acc_ref[h] = alpha_ref[h] * acc_ref[h] + _dot(vT, p_ref[h])
            m_ref[h] = mnext_ref[h]

    def finalize(src_ref, qi):
        outs = []
        for h in range(nh):
            acc = src_ref[h]
            outs.append(acc[:half] / acc[half:half + 1])
        oT = jnp.concatenate(outs, axis=0)
        o_ref[0, pl.ds(pl.multiple_of(qi * tq, tq), tq), :] = oT.T.astype(o_ref.dtype)

    def fix_clamped(t):
        @pl.when(t == 0)
        def _():
            mask_clamped_diagonal()

        @pl.when(t > 0)
        def _():
            fill_upper_groups()

    zero = jnp.int32(0)
    qaug_ref[...] = jnp.zeros_like(qaug_ref)
    keep_ref[0] = jnp.ones((tq, tq), F32)
    keep_ref[1] = jnp.where(lax.broadcasted_iota(jnp.int32, (tq, tq), 0)
                            <= lax.broadcasted_iota(jnp.int32, (tq, tq), 1), 1.0, 0.0)
    build_qaug(zero)
    build_qaug(zero + 1)
    reset_state()
    done_ref[...] = jnp.ones_like(done_ref)
    k0, _ = tile_of(zero, zero)
    store_scores(qk_dots(zero, k0), False)
    mask_clamped_diagonal()
    prep_stage(zero, k0, True)

    def body(it, carry):
        qi, t, qd = carry
        last = t == 0
        finalize(done_ref, qd)
        qi2 = jnp.where(last, qi + 1, qi)
        t2 = jnp.where(last, ntiles_ref[g * nq + jnp.minimum(qi + 1, nq - 1)] - 1, t - 1)
        k_cur, _ = tile_of(qi, t)
        k_nxt, cl_nxt = tile_of(qi2, t2)
        exp_stage()
        nxt = qk_dots(qi2, k_nxt)
        pv_stage(k_cur)
        store_scores(nxt, (t2 == 0) & jnp.logical_not(cl_nxt))
        prep_stage(qi2, k_nxt, last)

        @pl.when(cl_nxt)
        def _():
            fix_clamped(t2)
            prep_stage(qi2, k_nxt, last)

        @pl.when(last)
        def _():
            done_ref[...] = acc_ref[...]
            reset_state()

            @pl.when(qi + 2 < nq)
            def _():
                build_qaug(qi + 2)

        return qi2, t2, jnp.where(last, qi, qd)

    qi, t, qd = lax.fori_loop(0, nitems_ref[g] - 1, body, (jnp.int32(0), jnp.int32(0), jnp.int32(0)))
    finalize(done_ref, qd)
    exp_stage()
    pv_stage(tile_of(qi, t)[0])
    finalize(acc_ref, qi)


def _prune_plan(stats, drel, nq):
    B, n_blk = stats.shape[:2]
    per = FOX_TK // FOX_TQ
    c = stats[:, :, 0, :FOX_HEADS]
    qn = jnp.sqrt(stats[:, :, 1, :FOX_HEADS])
    kn = jnp.sqrt(stats[:, :, 2, :FOX_HEADS])
    d_in = jnp.transpose(drel[:, :, FOX_TQ - 1::FOX_TQ], (0, 2, 1))
    d_end = jnp.repeat(c, per, axis=1) + d_in
    d_before = jnp.concatenate([jnp.zeros_like(d_end[:, :1]), d_end[:, :-1]], axis=1)
    kn_q = jnp.repeat(kn, per, axis=1)
    qn_q = jnp.repeat(qn, per, axis=1)
    kn_prev = jnp.concatenate([jnp.zeros_like(kn_q[:, :1]), kn_q[:, :-1]], axis=1)
    kn_tile = jnp.maximum(kn_q, kn_prev)
    bound = (d_before[:, :, None, :] - d_end[:, None, :, :]
             + NORM_SLACK * qn_q[:, :, None, :] * (kn_tile[:, None, :, :] + kn_q[:, :, None, :]))
    back = np.arange(nq)[:, None] - np.arange(nq)[None, :]
    is_tile = (back > 0) & (back % per == 0)
    t_of = np.where(is_tile, back // per, 0).astype(np.int32)
    needed = is_tile[None, :, :, None] & ~(bound <= PRUNE_LOG2)
    nh = FOX_GROUP_HEADS
    needed_g = jnp.any(needed.reshape(B, nq, nq, FOX_HEADS // nh, nh), axis=-1)
    n_tiles = 1 + jnp.max(jnp.where(needed_g, t_of[None, :, :, None], 0), axis=2)
    n_tiles = jnp.transpose(n_tiles, (0, 2, 1))
    n_items = jnp.sum(n_tiles, axis=2)
    return n_tiles.reshape(-1).astype(jnp.int32), n_items.reshape(-1).astype(jnp.int32)


def _fox_attention(fqT, kaug, fvT, drel, stats):
    B, _, S = fqT.shape
    tq, tk = FOX_TQ, FOX_TK
    n_blk = S // INPROJ_TM
    nh = FOX_GROUP_HEADS
    n_groups = FOX_HEADS // nh
    c = stats[:, :, 0, :FOX_HEADS]
    c = jnp.transpose(c, (0, 2, 1)).reshape(B, n_groups, nh * n_blk, 1)
    crep = jnp.broadcast_to(c, (B, n_groups, nh * n_blk, tq))
    n_tiles, n_items = _prune_plan(stats, drel, S // tq)
    grid_spec = pltpu.PrefetchScalarGridSpec(
        num_scalar_prefetch=2,
        grid=(B, n_groups),
        in_specs=[
            pl.BlockSpec((1, nh * FOX_HEAD_DIM, S), lambda b, p, js, ni: (b, p, 0)),
            pl.BlockSpec((1, S, nh // 2 * MXU_DIM), lambda b, p, js, ni: (b, 0, p)),
            pl.BlockSpec((1, nh * FOX_HEAD_DIM, S), lambda b, p, js, ni: (b, p, 0)),
            pl.BlockSpec((1, FOX_HEADS, S), lambda b, p, js, ni: (b, 0, 0)),
            pl.BlockSpec((1, 1, nh * n_blk, tq), lambda b, p, js, ni: (b, p, 0, 0)),
        ],
        out_specs=pl.BlockSpec((1, S, nh * FOX_HEAD_DIM), lambda b, p, js, ni: (b, 0, p)),
        scratch_shapes=[
            pltpu.VMEM((2 * nh, MXU_DIM, tq), BF16),
            pltpu.VMEM((nh, tk, tq), F32),
            pltpu.VMEM((nh, tk, tq), BF16),
            pltpu.VMEM((nh * (tk // tq), SUBLANES, tq), F32),
            pltpu.VMEM((nh, FOX_HEAD_DIM + FOX_SUM_ROWS, tq), F32),
            pltpu.VMEM((nh, 1, tq), F32),
            pltpu.VMEM((nh, 1, tq), F32),
            pltpu.VMEM((nh * (tk // tq), 1, tq), F32),
            pltpu.VMEM((nh, 1, tq), F32),
            pltpu.VMEM((2, tq, tq), F32),
            pltpu.VMEM((nh, FOX_HEAD_DIM + FOX_SUM_ROWS, tq), F32),
        ])
    return pl.pallas_call(
        _fox_kernel,
        grid_spec=grid_spec,
        out_shape=jax.ShapeDtypeStruct((B, S, FOX_WIDTH), BF16),
        compiler_params=pltpu.CompilerParams(
            dimension_semantics=("arbitrary", "arbitrary"),
            vmem_limit_bytes=VMEM_LIMIT),
        name="fox_attn",
    )(n_tiles, n_items, fqT, kaug, fvT, drel, crep)


def _gla_kernel(q_ref, kT_ref, v_ref, gate_ref, sm_ref, wa_ref, ba_ref, o_ref, state_ref):
    nb, ts = q_ref.shape[0], q_ref.shape[1]
    C, P = GLA_CHUNK, GLA_PAIR
    dk, dv, H = GLA_KEY_DIM, GLA_VAL_DIM, GLA_HEADS
    KW = GLA_KEY_WIDTH

    @pl.when(pl.program_id(0) == 0)
    def _():
        state_ref[...] = jnp.zeros_like(state_ref)

    la_all = [_log_sigmoid(_dot(sm_ref[bb].astype(BF16), wa_ref[...]) + ba_ref[...])
              * (1.0 / GLA_GATE_TEMP) for bb in range(nb)]

    ri = lax.broadcasted_iota(jnp.int32, (P, P), 0)
    ci = lax.broadcasted_iota(jnp.int32, (P, P), 1)
    tril2 = jnp.where(((ri < C) == (ci < C)) & (ci <= ri), 1.0, 0.0).astype(BF16)

    rs = lax.broadcasted_iota(jnp.int32, (2 * H * C, P), 0)
    cs = lax.broadcasted_iota(jnp.int32, (2 * H * C, P), 1)
    causal = ((rs >= H * C) == (cs >= C)) & ((cs & (C - 1)) <= (rs & (C - 1)))

    lane_q = lax.broadcasted_iota(jnp.int32, (C, KW), 1)
    lane_t = lax.broadcasted_iota(jnp.int32, (KW, P), 1)
    first = lane_t < C

    per_b = ts // P
    n_slabs = nb * per_b
    slab_b = [k // per_b for k in range(n_slabs)]
    slab_rows = [slice((k % per_b) * P, (k % per_b + 1) * P) for k in range(n_slabs)]
    bs = [_tri_left(tril2, la_all[slab_b[k]][slab_rows[k]]) for k in range(n_slabs)]

    qms, scores, kv0s, kv1s, dec0s, dec1s, vss = [], [], [], [], [], [], []
    for pi in range(n_slabs):
        bb, rows = slab_b[pi], slab_rows[pi]
        b = bs[pi]
        bT = b.T
        bl0 = bT[:, C - 1:C]
        bl1 = bT[:, P - 1:P]
        q_dec = q_ref[bb, rows, :] * jnp.exp(b)
        kT = kT_ref[bb, :, rows]
        k_invT = (kT * jnp.exp(-bT)).astype(BF16)
        k_teT = kT * jnp.exp(jnp.where(first, bl0, bl1) - bT)
        zt = jnp.zeros_like(k_teT)
        k_te0 = jnp.where(first, k_teT, zt).astype(BF16)
        k_te1 = jnp.where(first, zt, k_teT).astype(BF16)
        dec0 = jnp.exp(bl0)
        dec1 = jnp.exp(bl1)

        vs = [v_ref[bb, rows, h * dv:(h + 1) * dv] for h in range(H)]
        kv = [_dot(jnp.concatenate([k_te0[h * dk:(h + 1) * dk, :], k_te1[h * dk:(h + 1) * dk, :]], axis=0),
                   vs[h]) for h in range(H)]
        kv0s.append(jnp.concatenate([kv[h][:dk] for h in range(H)], axis=0))
        kv1s.append(jnp.concatenate([kv[h][dk:] for h in range(H)], axis=0))
        dec0s.append(dec0)
        dec1s.append(dec1)
        vss.append(vs)

        zq = jnp.zeros((C, KW), F32)
        qm = [jnp.concatenate(
            [jnp.where((lane_q >= h * dk) & (lane_q < (h + 1) * dk), q_dec[c * C:(c + 1) * C], zq)
             for h in range(H)], axis=0).astype(BF16) for c in range(2)]
        a = _dot(jnp.concatenate(qm, axis=0), k_invT)
        scores.append(jnp.where(causal, a, 0.0).astype(BF16))
        qms.append(qm)

    o_inters = []
    for pi in range(n_slabs):
        if pi % per_b == 0:
            s = state_ref[slab_b[pi]]
        s1 = dec0s[pi] * s + kv0s[pi]
        o_inters.append([_dot(qms[pi][0], s.astype(BF16)), _dot(qms[pi][1], s1.astype(BF16))])
        s = dec1s[pi] * s1 + kv1s[pi]
        if pi % per_b == per_b - 1:
            state_ref[slab_b[pi]] = s

    for pi in range(n_slabs):
        bb, rows = slab_b[pi], slab_rows[pi]
        a, o_inter, vs = scores[pi], o_inters[pi], vss[pi]
        for h in range(H):
            hv = slice(h * dv, (h + 1) * dv)
            a_h = jnp.concatenate([a[(c * H + h) * C:(c * H + h + 1) * C] for c in range(2)], axis=0)
            o = _dot(a_h, vs[h]) + jnp.concatenate(
                [o_inter[c][h * C:(h + 1) * C] for c in range(2)], axis=0)
            o = o * lax.rsqrt(jnp.mean(o * o, axis=-1, keepdims=True) + RMS_EPS)
            o_ref[bb, rows, hv] = (o * gate_ref[bb, rows, hv]).astype(o_ref.dtype)


def _gla_attention(gq, gkT, gv, gate, small, w_a2_pad, b_a):
    B, S, _ = gq.shape
    ts = GLA_TS
    tok = lambda w: pl.BlockSpec((B, ts, w), lambda s: (0, s, 0))
    full = lambda r, c: pl.BlockSpec((r, c), lambda s: (0, 0))
    return pl.pallas_call(
        _gla_kernel,
        grid=(S // ts,),
        in_specs=[
            tok(GLA_KEY_WIDTH), pl.BlockSpec((B, GLA_KEY_WIDTH, ts), lambda s: (0, 0, s)),
            tok(GLA_WIDTH), tok(GLA_WIDTH), tok(SMALL_W),
            full(SMALL_W, GLA_KEY_WIDTH), full(1, GLA_KEY_WIDTH),
        ],
        out_specs=tok(GLA_WIDTH),
        out_shape=jax.ShapeDtypeStruct((B, S, GLA_WIDTH), BF16),
        scratch_shapes=[pltpu.VMEM((B, GLA_KEY_WIDTH, GLA_VAL_DIM), F32)],
        compiler_params=pltpu.CompilerParams(
            dimension_semantics=("arbitrary",), vmem_limit_bytes=VMEM_LIMIT),
        name="gla_attn",
    )(gq, gkT, gv, gate, small, w_a2_pad, b_a)


def _layer_norm(z, g, b):
    mu = jnp.mean(z, axis=-1, keepdims=True)
    d = z - mu
    var = jnp.mean(d * d, axis=-1, keepdims=True)
    return d * lax.rsqrt(var + LN_EPS) * g + b


def _tail_kernel(x_ref, fox_ref, gla_ref, mod_ref, wof_ref, wog_ref, ln1g_ref, ln1b_ref,
                 wg_ref, wu_ref, wd_ref, ln2g_ref, ln2b_ref, o_ref):
    m = mod_ref[0]
    tm = x_ref.shape[0]
    halves = [slice(0, tm // 2), slice(tm // 2, tm)]
    ys = [_dot(fox_ref[r, :], wof_ref[...]) + _dot(gla_ref[r, :], wog_ref[...]) for r in halves]
    x1s, gs, ups = [], [], []
    for r, y in zip(halves, ys):
        x1 = _layer_norm(DEEPNORM_ALPHA * x_ref[r, :] + (1.0 + m[2:3]) * y,
                         ln1g_ref[...], ln1b_ref[...])
        u2 = (x1 * (1.0 + m[4:5]) + m[3:4]).astype(BF16)
        x1s.append(x1)
        gs.append(_dot(u2, wg_ref[...]))
        ups.append(_dot(u2, wu_ref[...]))
    y2s = []
    for g, up in zip(gs, ups):
        h = (g * (1.0 / (1.0 + jnp.exp(-g))) * up).astype(BF16)
        y2s.append(_dot(h, wd_ref[...]))
    for r, x1, y2 in zip(halves, x1s, y2s):
        o_ref[r, :] = _layer_norm(DEEPNORM_ALPHA * x1 + (1.0 + m[5:6]) * y2,
                                  ln2g_ref[...], ln2b_ref[...])


def _tail(x2d, fox2d, gla2d, mod, wo, ln1g, ln1b, wg, wu, wd, ln2g, ln2b, steps_per_batch):
    T, D = x2d.shape
    tm = TAIL_TM
    const = lambda r, c: pl.BlockSpec((r, c), lambda i: (0, 0), pipeline_mode=pl.Buffered(1))
    tok = lambda w: pl.BlockSpec((tm, w), lambda i: (i, 0))
    return pl.pallas_call(
        _tail_kernel,
        grid=(T // tm,),
        in_specs=[
            tok(D), tok(FOX_WIDTH), tok(GLA_WIDTH),
            pl.BlockSpec((1, N_MOD, D), lambda i: (i // steps_per_batch, 0, 0)),
            pl.BlockSpec((FOX_WIDTH, D), lambda i: (0, 0), pipeline_mode=pl.Buffered(1)),
            pl.BlockSpec((GLA_WIDTH, D), lambda i: (1, 0), pipeline_mode=pl.Buffered(1)),
            const(1, D), const(1, D),
            const(D, D_FF), const(D, D_FF), const(D_FF, D), const(1, D), const(1, D),
        ],
        out_specs=tok(D),
        out_shape=jax.ShapeDtypeStruct((T, D), F32),
        compiler_params=pltpu.CompilerParams(
            dimension_semantics=("arbitrary",), vmem_limit_bytes=VMEM_LIMIT),
        name="tail",
    )(x2d, fox2d, gla2d, mod, wo, wo, ln1g, ln1b, wg, wu, wd, ln2g, ln2b)


def kernel(x, c, w_c, b_c, w_in, b_f, w_a2, b_a, g_gla, w_o, ln1_g, ln1_b,
           w_gate, w_up, w_down, ln2_g, ln2_b):
    B, S, D = x.shape
    assert D == D_MODEL and S % INPROJ_TM == 0 and INPROJ_TM == FOX_TK and FOX_TK % FOX_TQ == 0

    b_f_pad = jnp.pad(b_f.reshape(1, FOX_HEADS), ((0, 0), (0, SMALL_W - FOX_HEADS)))
    w_a2_pad = jnp.pad(w_a2, ((FOX_HEADS, SMALL_W - FOX_HEADS - GLA_GATE_RANK), (0, 0))).astype(BF16)

    mod, w_all = _modulation(c, w_c, b_c, w_in)

    fqT, kaug, fvT, gq, gkT, gv, gate, small, drel, stats, wg16, wu16, wd16, wo16 = _in_projection(
        x, mod, w_all, b_f_pad, g_gla.reshape(1, -1), w_gate, w_up, w_down, w_o)
    fox = _fox_attention(fqT, kaug, fvT, drel, stats)
    gla = _gla_attention(gq, gkT, gv, gate, small, w_a2_pad, b_a.reshape(1, -1))

    T = B * S
    out = _tail(x.reshape(T, D), fox.reshape(T, FOX_WIDTH), gla.reshape(T, GLA_WIDTH), mod,
                wo16,
                ln1_g.reshape(1, D), ln1_b.reshape(1, D),
                wg16, wu16, wd16,
                ln2_g.reshape(1, D), ln2_b.reshape(1, D), S // TAIL_TM)
    return out.reshape(B, S, D)
```

```python
import functools
import math

import numpy as np
import jax
import jax.numpy as jnp
from jax import lax
from jax.experimental import pallas as pl
from jax.experimental.pallas import tpu as pltpu

F32 = jnp.float32
BF16 = jnp.bfloat16

D_MODEL = 1024
FOX_WIDTH = 512
FOX_HEAD_DIM = 64
FOX_HEADS = 8
FOX_PAIRS = FOX_HEADS // 2
GLA_WIDTH = 512
GLA_HEADS = 4
GLA_VAL_DIM = 128
GLA_KEY_DIM = 64
GLA_KEY_WIDTH = 256
GLA_GATE_RANK = 16
GLA_GATE_TEMP = 16.0
GLA_CHUNK = 64
D_FF = 2816
LN_EPS = 1e-5
RMS_EPS = 1e-6
N_MOD = 6
DEEPNORM_ALPHA = 2.0 ** 0.25
IN_SPLITS = (FOX_WIDTH, FOX_WIDTH, FOX_WIDTH, FOX_HEADS, GLA_KEY_WIDTH, GLA_KEY_WIDTH,
             GLA_WIDTH, GLA_GATE_RANK, GLA_WIDTH)
LOG2E = math.log2(math.e)

LANES = 128
SUBLANES = 8
MXU_DIM = 256
SMALL_W = LANES
VMEM_LIMIT = 56 * 1024 * 1024

MOD_TN = 768
INPROJ_X_SLOTS = 3
INPROJ_TM = 512
FOX_TQ = 256
FOX_GROUP_HEADS = 4
FOX_SUM_ROWS = 16
FOX_TK = 512
GLA_TS = 512
GLA_PAIR = 2 * GLA_CHUNK
TAIL_TM = 512

NEG_BIG = -1e30
PRUNE_LOG2 = -150.0
NORM_SLACK = 1.02

AUG_ROWS = 32


def _log_sigmoid(z):
    return jnp.minimum(z, 0.0) - jnp.log(1.0 + jnp.exp(-jnp.abs(z)))


def _split3(a):
    hi = a.astype(BF16)
    r1 = a - hi.astype(F32)
    mid = r1.astype(BF16)
    lo = (r1 - mid.astype(F32)).astype(BF16)
    return hi, mid, lo


def _dot(a, b):
    return jnp.dot(a, b, preferred_element_type=F32)


def _tri_left(tri, a):
    hi, mid, lo = _split3(a)
    return _dot(tri, hi) + _dot(tri, mid) + _dot(tri, lo)


def _tri_right(a, tri):
    hi, mid, lo = _split3(a)
    return _dot(hi, tri) + _dot(mid, tri) + _dot(lo, tri)


def _regroup_blocks():
    blocks = []
    for dst, src, width in _W_GROUPS:
        blocks += [(dst + c0, src + c0) for c0 in range(0, width, MXU_DIM)]
    blocks.append((_C_SM, None))
    return blocks


def _mod_kernel(cT_ref, w_ref, b_ref, winT_ref, o_ref, wall_ref, *, n_rows):
    w = w_ref[...]
    outs = [jnp.sum(w * cT_ref[:, b:b + 1], axis=0, keepdims=True) for b in range(n_rows)]
    outs.append(jnp.zeros((o_ref.shape[0] - n_rows, w.shape[1]), F32))
    o_ref[...] = jnp.concatenate(outs, axis=0) + b_ref[...]

    j = pl.program_id(0)
    n_steps = pl.num_programs(0)
    for k, (dst, src) in enumerate(_regroup_blocks()):
        @pl.when(j == k % n_steps)
        def _():
            if src is None:
                blk = jnp.concatenate(
                    [winT_ref[_SRC_FF:_SRC_FF + FOX_HEADS, :], winT_ref[_SRC_GA:_SRC_GA + GLA_GATE_RANK, :],
                     jnp.zeros((MXU_DIM - FOX_HEADS - GLA_GATE_RANK, winT_ref.shape[1]), F32)], axis=0)
            else:
                blk = winT_ref[src:src + MXU_DIM, :]
            wall_ref[:, dst:dst + MXU_DIM] = blk.T.astype(BF16)


def _modulation(c, w_c, b_c, w_in):
    B, D = c.shape
    N = w_c.shape[1]
    rows = SUBLANES
    n_steps = N // MOD_TN
    cT_pad = jnp.pad(c.T, ((0, 0), (0, LANES - B)))
    w_inT = w_in.T
    out, w_all = pl.pallas_call(
        functools.partial(_mod_kernel, n_rows=B),
        grid=(n_steps,),
        in_specs=[
            pl.BlockSpec((D, LANES), lambda j: (0, 0)),
            pl.BlockSpec((D, MOD_TN), lambda j: (0, j)),
            pl.BlockSpec((1, MOD_TN), lambda j: (0, j)),
            pl.BlockSpec(w_inT.shape, lambda j: (0, 0), pipeline_mode=pl.Buffered(1)),
        ],
        out_specs=[pl.BlockSpec((rows, MOD_TN), lambda j: (0, j)),
                   pl.BlockSpec((D, _W_COLS), lambda j: (0, 0))],
        out_shape=[jax.ShapeDtypeStruct((rows, N), F32),
                   jax.ShapeDtypeStruct((D, _W_COLS), BF16)],
        compiler_params=pltpu.CompilerParams(
            dimension_semantics=("arbitrary",), vmem_limit_bytes=VMEM_LIMIT),
        name="adaln_mod",
    )(cT_pad, w_c, b_c.reshape(1, N), w_inT)
    return out[:B].reshape(B, N_MOD, D), w_all


_C_FQ, _C_FK, _C_FV = 0, 512, 1024
_C_GQ, _C_SM, _C_GK, _C_GV, _C_GR = 1536, 1792, 2048, 2304, 2816
_W_COLS = _C_GR + GLA_WIDTH
_SRC = [sum(IN_SPLITS[:i]) for i in range(len(IN_SPLITS))]
_SRC_FF, _SRC_GA = _SRC[3], _SRC[7]
_W_GROUPS = ((_C_FQ, _SRC[0], FOX_WIDTH), (_C_FK, _SRC[1], FOX_WIDTH), (_C_FV, _SRC[2], FOX_WIDTH),
             (_C_GQ, _SRC[4], GLA_KEY_WIDTH), (_C_GK, _SRC[5], GLA_KEY_WIDTH),
             (_C_GV, _SRC[6], GLA_WIDTH), (_C_GR, _SRC[8], GLA_WIDTH))
_KAUG_W = FOX_PAIRS * MXU_DIM


def _head_indicator():
    ind = np.zeros((FOX_WIDTH, MXU_DIM), np.float32)
    for h in range(FOX_HEADS):
        ind[h * FOX_HEAD_DIM:(h + 1) * FOX_HEAD_DIM, h] = 1.0
    return ind


def _inproj_kernel(x_ref, mod_ref, w_ref, bf_ref, ind_ref, gg_ref, wg32_ref, wu32_ref, wd32_ref, wo32_ref,
                   fqT_ref, kaug_ref, fvT_ref, gq_ref, gk_ref, gv_ref, gr_ref, sm_ref,
                   drel_ref, stats_ref, wg16_ref, wu16_ref, wd16_ref, wo16_ref, carry_ref, xbuf_ref, xsem_ref):
    tm = xbuf_ref.shape[1]

    step = pl.program_id(0) * pl.num_programs(1) + pl.program_id(1)
    n_steps = pl.num_programs(0) * pl.num_programs(1)
    ahead = INPROJ_X_SLOTS - 1

    def x_copy(k):
        slot = lax.rem(k, INPROJ_X_SLOTS)
        return pltpu.make_async_copy(x_ref.at[k], xbuf_ref.at[slot], xsem_ref.at[slot])

    @pl.when(step == 0)
    def _():
        for k in range(ahead):
            x_copy(jnp.int32(k)).start()

    @pl.when(step + ahead < n_steps)
    def _():
        x_copy(step + ahead).start()

    x_copy(step).wait()
    x_tile = xbuf_ref[lax.rem(step, INPROJ_X_SLOTS)]

    wg16_ref[...] = wg32_ref[...].astype(BF16)
    wu16_ref[...] = wu32_ref[...].astype(BF16)
    wd16_ref[...] = wd32_ref[...].astype(BF16)
    wo16_ref[...] = wo32_ref[...].astype(BF16)

    @pl.when(pl.program_id(1) == 0)
    def _():
        carry_ref[...] = jnp.zeros_like(carry_ref)

    m = mod_ref[0]
    u = (x_tile * (1.0 + m[1:2]) + m[0:1]).astype(BF16)

    def proj(lo, width):
        return _dot(u, w_ref[:, lo:lo + width])

    gq_small = proj(_C_GQ, 2 * GLA_KEY_WIDTH)
    gq_ref[0] = gq_small[:, :GLA_KEY_WIDTH] * (GLA_KEY_DIM ** -0.5)
    small = gq_small[:, GLA_KEY_WIDTH:GLA_KEY_WIDTH + SMALL_W]
    sm_ref[0] = small

    fq = proj(_C_FQ, FOX_WIDTH) * (FOX_HEAD_DIM ** -0.5 * LOG2E)
    fqT_ref[0] = fq.T.astype(BF16)
    fk32 = proj(_C_FK, FOX_WIDTH)
    fk = fk32.astype(BF16)
    qn2 = jnp.max(_dot((fq * fq).astype(BF16), ind_ref[...])[:, :SMALL_W], axis=0, keepdims=True)
    kn2 = jnp.max(_dot((fk32 * fk32).astype(BF16), ind_ref[...])[:, :SMALL_W], axis=0, keepdims=True)

    zT = (small + bf_ref[...]).T[:FOX_HEADS, :]
    lfT = _log_sigmoid(zT) * LOG2E
    r = lax.broadcasted_iota(jnp.int32, (tm, tm), 0)
    c = lax.broadcasted_iota(jnp.int32, (tm, tm), 1)
    triu = jnp.where(r <= c, 1.0, 0.0).astype(BF16)
    drelT = _tri_right(lfT, triu)
    drel_ref[0] = drelT

    c_before = carry_ref[...]
    srow = lax.broadcasted_iota(jnp.int32, (SUBLANES, SMALL_W), 0)
    slane = lax.broadcasted_iota(jnp.int32, (SUBLANES, SMALL_W), 1)
    total = jnp.sum(jnp.where(srow == slane, drelT[:, tm - 1:tm], 0.0), axis=0, keepdims=True)
    carry_ref[...] = c_before + total

    hi, mid, lo = _split3(drelT)
    ones3 = jnp.where(lax.broadcasted_iota(jnp.int32, (SUBLANES, tm), 0) < 3, 1.0, 0.0)
    slabT = jnp.concatenate(
        [-hi.astype(F32), -mid.astype(F32), -lo.astype(F32), ones3,
         jnp.zeros((LANES - 4 * SUBLANES, tm), F32)], axis=0)
    slab = slabT.T.astype(BF16)

    fvT_ref[0] = proj(_C_FV, FOX_WIDTH).T.astype(BF16)
    gv_ref[0] = proj(_C_GV, GLA_WIDTH).astype(BF16)
    gr = proj(_C_GR, GLA_WIDTH)
    gr_ref[0] = gg_ref[...] * (gr * (1.0 / (1.0 + jnp.exp(-gr))))
    gk_ref[0] = proj(_C_GK, GLA_KEY_WIDTH).T

    stats_ref[0, 0] = jnp.where(srow == 0, c_before,
                                jnp.where(srow == 1, qn2, jnp.where(srow == 2, kn2, 0.0)))
    for p in range(FOX_PAIRS):
        kaug_ref[0, :, p * MXU_DIM:p * MXU_DIM + LANES] = fk[:, p * LANES:(p + 1) * LANES]
        kaug_ref[0, :, p * MXU_DIM + LANES:(p + 1) * MXU_DIM] = slab


def _in_projection(x, mod, w_all, b_f_pad, g_gla, w_gate, w_up, w_down, w_o):
    B, S, D = x.shape
    tm = INPROJ_TM
    n_steps = B * (S // tm)
    steps_b = S // tm
    up_rows = D // n_steps
    down_rows = 2 * D_FF // n_steps
    assert D % n_steps == 0 and up_rows % 16 == 0 and (2 * D_FF) % n_steps == 0 and down_rows % 16 == 0
    up_spec = pl.BlockSpec((up_rows, D_FF), lambda b, s: (b * steps_b + s, 0))
    down_spec = pl.BlockSpec((down_rows, D), lambda b, s: ((b * steps_b + s) // 2, 0))
    wo_rows = w_o.shape[0] // n_steps
    assert w_o.shape[0] % n_steps == 0 and wo_rows % 16 == 0
    wo_spec = pl.BlockSpec((wo_rows, D), lambda b, s: (b * steps_b + s, 0))
    ind = _head_indicator()
    tok = lambda w: pl.BlockSpec((1, tm, w), lambda b, s: (b, s, 0))
    tokT = lambda h: pl.BlockSpec((1, h, tm), lambda b, s: (b, 0, s))
    outs = pl.pallas_call(
        _inproj_kernel,
        grid=(B, S // tm),
        in_specs=[
            pl.BlockSpec(memory_space=pl.ANY),
            pl.BlockSpec((1, N_MOD, D), lambda b, s: (b, 0, 0)),
            pl.BlockSpec((D, _W_COLS), lambda b, s: (0, 0), pipeline_mode=pl.Buffered(1)),
            pl.BlockSpec((1, SMALL_W), lambda b, s: (0, 0)),
            pl.BlockSpec(ind.shape, lambda b, s: (0, 0)),
            pl.BlockSpec((1, GLA_WIDTH), lambda b, s: (0, 0)),
            up_spec, up_spec, down_spec, wo_spec,
        ],
        out_specs=[
            tokT(FOX_WIDTH), tok(_KAUG_W), tokT(FOX_WIDTH),
            tok(GLA_KEY_WIDTH), tokT(GLA_KEY_WIDTH), tok(GLA_WIDTH), tok(GLA_WIDTH),
            tok(SMALL_W),
            tokT(FOX_HEADS),
            pl.BlockSpec((1, 1, SUBLANES, SMALL_W), lambda b, s: (b, s, 0, 0)),
            up_spec, up_spec, down_spec, wo_spec,
        ],
        out_shape=[
            jax.ShapeDtypeStruct((B, FOX_WIDTH, S), BF16),
            jax.ShapeDtypeStruct((B, S, _KAUG_W), BF16),
            jax.ShapeDtypeStruct((B, FOX_WIDTH, S), BF16),
            jax.ShapeDtypeStruct((B, S, GLA_KEY_WIDTH), F32),
            jax.ShapeDtypeStruct((B, GLA_KEY_WIDTH, S), F32),
            jax.ShapeDtypeStruct((B, S, GLA_WIDTH), BF16),
            jax.ShapeDtypeStruct((B, S, GLA_WIDTH), F32),
            jax.ShapeDtypeStruct((B, S, SMALL_W), F32),
            jax.ShapeDtypeStruct((B, FOX_HEADS, S), F32),
            jax.ShapeDtypeStruct((B, S // tm, SUBLANES, SMALL_W), F32),
            jax.ShapeDtypeStruct(w_gate.shape, BF16),
            jax.ShapeDtypeStruct(w_up.shape, BF16),
            jax.ShapeDtypeStruct(w_down.shape, BF16),
            jax.ShapeDtypeStruct(w_o.shape, BF16),
        ],
        scratch_shapes=[pltpu.VMEM((1, SMALL_W), F32),
                        pltpu.VMEM((INPROJ_X_SLOTS, tm, D), F32),
                        pltpu.SemaphoreType.DMA((INPROJ_X_SLOTS,))],
        compiler_params=pltpu.CompilerParams(
            dimension_semantics=("arbitrary", "arbitrary"), vmem_limit_bytes=VMEM_LIMIT),
        name="in_proj",
    )(x.reshape(n_steps, tm, D), mod, w_all, b_f_pad, jnp.asarray(ind, BF16), g_gla, w_gate, w_up, w_down, w_o)
    return outs


def _fox_kernel(ntiles_ref, nitems_ref, qT_ref, kaug_ref, vT_ref, drel_ref, crep_ref, o_ref,
                qaug_ref, s_ref, p_ref, mx_ref, acc_ref, m_ref, mnext_ref, shift_ref, alpha_ref, keep_ref, done_ref):
    tq, tk = FOX_TQ, FOX_TK
    half = FOX_HEAD_DIM
    S = kaug_ref.shape[1]
    nq = S // tq
    nh = FOX_GROUP_HEADS
    n_blk = crep_ref.shape[2] // nh
    g = pl.program_id(0) * pl.num_programs(1) + pl.program_id(1)

    def build_qaug(qi):
        q0 = pl.multiple_of(qi * tq, tq)
        dr_all = drel_ref[0, :, pl.ds(q0, tq)]
        dr = dr_all[0:nh]
        for grp in range(1, dr_all.shape[0] // nh):
            dr = jnp.where(pl.program_id(1) == grp, dr_all[grp * nh:(grp + 1) * nh], dr)
        row = lax.broadcasted_iota(jnp.int32, (2 * half, tq), 0)
        r32 = lax.broadcasted_iota(jnp.int32, (AUG_ROWS, tq), 0)
        for h in range(nh):
            buf = (qi % 2) * nh + h
            qT = qT_ref[0, (h // 2) * 2 * half:(h // 2 + 1) * 2 * half, pl.ds(q0, tq)]
            own = (row < half) if h % 2 == 0 else (row >= half)
            qaug_ref[buf, 0:2 * half, :] = jnp.where(own, qT, jnp.zeros_like(qT))
            hi, mid, lo = _split3(dr[h:h + 1])
            gh = nh * pl.program_id(1) + h
            pick = (r32 == gh) | (r32 == gh + SUBLANES) | (r32 == gh + 2 * SUBLANES)
            slab = jnp.where(r32 == 24, hi.astype(F32),
                             jnp.where(r32 == 25, mid.astype(F32),
                                       jnp.where(r32 == 26, lo.astype(F32),
                                                 jnp.where(pick, 1.0, 0.0))))
            qaug_ref[buf, 2 * half:2 * half + AUG_ROWS, :] = slab.astype(BF16)

    def reset_state():
        acc_ref[...] = jnp.zeros_like(acc_ref)

    sum_rows = jnp.where(lax.broadcasted_iota(jnp.int32, (FOX_SUM_ROWS, tk), 0) == 0, 1.0, 0.0).astype(BF16)

    per = tk // tq

    def col_max8(s):
        return jnp.max(s.reshape(s.shape[0] // SUBLANES, SUBLANES, tq), axis=0)

    def tile_of(qi, t):
        raw = (qi + 1) * tq - (t + 1) * tk
        return pl.multiple_of(jnp.maximum(raw, 0), tq), raw < 0

    def qk_dots(qi, k_off):
        kts = [kaug_ref[0, pl.ds(k_off, tk), p * MXU_DIM:(p + 1) * MXU_DIM] for p in range(nh // 2)]
        return [_dot(kts[h // 2], qaug_ref[(qi % 2) * nh + h]) for h in range(nh)]

    def store_scores(scores, diag):
        which = jnp.where(diag, 1, 0)
        for h in range(nh):
            for g in range(per):
                blk = scores[h][g * tq:(g + 1) * tq]
                if g == per - 1:
                    blk = jnp.where(keep_ref[which] > 0.5, blk, NEG_BIG)
                s_ref[h, g * tq:(g + 1) * tq, :] = blk
                mx_ref[per * h + g] = col_max8(blk)

    def fill_upper_groups():
        for g in range(1, per):
            for h in range(nh):
                s_ref[h, g * tq:(g + 1) * tq, :] = jnp.full((tq, tq), NEG_BIG, F32)
                mx_ref[per * h + g] = jnp.full((SUBLANES, tq), NEG_BIG, F32)

    def mask_clamped_diagonal():
        kr = lax.broadcasted_iota(jnp.int32, (tq, tq), 0)
        qc = lax.broadcasted_iota(jnp.int32, (tq, tq), 1)
        for h in range(nh):
            tri = jnp.where(kr <= qc, s_ref[h, 0:tq, :], NEG_BIG)
            s_ref[h, 0:tq, :] = tri
            mx_ref[per * h] = col_max8(tri)
        fill_upper_groups()

    def prep_stage(qi, k_off, first):
        i_blk = qi * tq // INPROJ_TM
        for h in range(nh):
            c_i = crep_ref[0, 0, pl.ds(h * n_blk + i_blk, 1), :]
            big_c = [c_i - crep_ref[0, 0, pl.ds(h * n_blk + (k_off + g * tq) // INPROJ_TM, 1), :]
                     for g in range(per)]
            cand = jnp.max(mx_ref[per * h], axis=0, keepdims=True) + big_c[0]
            for g in range(1, per):
                cand = jnp.maximum(cand, jnp.max(mx_ref[per * h + g], axis=0, keepdims=True) + big_c[g])
            m_prev = jnp.where(first, NEG_BIG, m_ref[h])
            m_new = jnp.maximum(m_prev, cand)
            mnext_ref[h] = m_new
            for g in range(per):
                shift_ref[per * h + g] = m_new - big_c[g]
            alpha_ref[h] = jnp.exp2(m_prev - m_new)

    def exp_stage():
        for h in range(nh):
            for g in range(per):
                rows = slice(g * tq, (g + 1) * tq)
                p_ref[h, rows, :] = jnp.exp2(s_ref[h, rows, :] - shift_ref[per * h + g]).astype(BF16)

    def pv_stage(k_off):
        for h in range(nh):
            vT = jnp.concatenate([vT_ref[0, h * half:(h + 1) * half, pl.ds(k_off, tk)], sum_rows], axis=0)
            acc_ref[h] = alpha_ref[h] * acc_ref[h] + _dot(vT, p_ref[h])
            m_ref[h] = mnext_ref[h]

    def finalize(src_ref, qi):
        outs = []
        for h in range(nh):
            acc = src_ref[h]
            outs.append(acc[:half] / acc[half:half + 1])
        oT = jnp.concatenate(outs, axis=0)
        o_ref[0, pl.ds(pl.multiple_of(qi * tq, tq), tq), :] = oT.T.astype(o_ref.dtype)

    def fix_clamped(t):
        @pl.when(t == 0)
        def _():
            mask_clamped_diagonal()

        @pl.when(t > 0)
        def _():
            fill_upper_groups()

    zero = jnp.int32(0)
    qaug_ref[...] = jnp.zeros_like(qaug_ref)
    keep_ref[0] = jnp.ones((tq, tq), F32)
    keep_ref[1] = jnp.where(lax.broadcasted_iota(jnp.int32, (tq, tq), 0)
                            <= lax.broadcasted_iota(jnp.int32, (tq, tq), 1), 1.0, 0.0)
    build_qaug(zero)
    build_qaug(zero + 1)
    reset_state()
    done_ref[...] = jnp.ones_like(done_ref)
    k0, _ = tile_of(zero, zero)
    store_scores(qk_dots(zero, k0), False)
    mask_clamped_diagonal()
    prep_stage(zero, k0, True)

    def body(it, carry):
        qi, t, qd = carry
        last = t == 0
        finalize(done_ref, qd)
        qi2 = jnp.where(last, qi + 1, qi)
        t2 = jnp.where(last, ntiles_ref[g * nq + jnp.minimum(qi + 1, nq - 1)] - 1, t - 1)
        k_cur, _ = tile_of(qi, t)
        k_nxt, cl_nxt = tile_of(qi2, t2)
        exp_stage()
        nxt = qk_dots(qi2, k_nxt)
        pv_stage(k_cur)
        store_scores(nxt, (t2 == 0) & jnp.logical_not(cl_nxt))
        prep_stage(qi2, k_nxt, last)

        @pl.when(cl_nxt)
        def _():
            fix_clamped(t2)
            prep_stage(qi2, k_nxt, last)

        @pl.when(last)
        def _():
            done_ref[...] = acc_ref[...]
            reset_state()

            @pl.when(qi + 2 < nq)
            def _():
                build_qaug(qi + 2)

        return qi2, t2, jnp.where(last, qi, qd)

    qi, t, qd = lax.fori_loop(0, nitems_ref[g] - 1, body, (jnp.int32(0), jnp.int32(0), jnp.int32(0)))
    finalize(done_ref, qd)
    exp_stage()
    pv_stage(tile_of(qi, t)[0])
    finalize(acc_ref, qi)


def _prune_plan(stats, drel, nq):
    B, n_blk = stats.shape[:2]
    per = FOX_TK // FOX_TQ
    c = stats[:, :, 0, :FOX_HEADS]
    qn = jnp.sqrt(stats[:, :, 1, :FOX_HEADS])
    kn = jnp.sqrt(stats[:, :, 2, :FOX_HEADS])
    d_in = jnp.transpose(drel[:, :, FOX_TQ - 1::FOX_TQ], (0, 2, 1))
    d_end = jnp.repeat(c, per, axis=1) + d_in
    d_before = jnp.concatenate([jnp.zeros_like(d_end[:, :1]), d_end[:, :-1]], axis=1)
    kn_q = jnp.repeat(kn, per, axis=1)
    qn_q = jnp.repeat(qn, per, axis=1)
    kn_prev = jnp.concatenate([jnp.zeros_like(kn_q[:, :1]), kn_q[:, :-1]], axis=1)
    kn_tile = jnp.maximum(kn_q, kn_prev)
    bound = (d_before[:, :, None, :] - d_end[:, None, :, :]
             + NORM_SLACK * qn_q[:, :, None, :] * (kn_tile[:, None, :, :] + kn_q[:, :, None, :]))
    back = np.arange(nq)[:, None] - np.arange(nq)[None, :]
    is_tile = (back > 0) & (back % per == 0)
    t_of = np.where(is_tile, back // per, 0).astype(np.int32)
    needed = is_tile[None, :, :, None] & ~(bound <= PRUNE_LOG2)
    nh = FOX_GROUP_HEADS
    needed_g = jnp.any(needed.reshape(B, nq, nq, FOX_HEADS // nh, nh), axis=-1)
    n_tiles = 1 + jnp.max(jnp.where(needed_g, t_of[None, :, :, None], 0), axis=2)
    n_tiles = jnp.transpose(n_tiles, (0, 2, 1))
    n_items = jnp.sum(n_tiles, axis=2)
    return n_tiles.reshape(-1).astype(jnp.int32), n_items.reshape(-1).astype(jnp.int32)


def _fox_attention(fqT, kaug, fvT, drel, stats):
    B, _, S = fqT.shape
    tq, tk = FOX_TQ, FOX_TK
    n_blk = S // INPROJ_TM
    nh = FOX_GROUP_HEADS
    n_groups = FOX_HEADS // nh
    c = stats[:, :, 0, :FOX_HEADS]
    c = jnp.transpose(c, (0, 2, 1)).reshape(B, n_groups, nh * n_blk, 1)
    crep = jnp.broadcast_to(c, (B, n_groups, nh * n_blk, tq))
    n_tiles, n_items = _prune_plan(stats, drel, S // tq)
    grid_spec = pltpu.PrefetchScalarGridSpec(
        num_scalar_prefetch=2,
        grid=(B, n_groups),
        in_specs=[
            pl.BlockSpec((1, nh * FOX_HEAD_DIM, S), lambda b, p, js, ni: (b, p, 0)),
            pl.BlockSpec((1, S, nh // 2 * MXU_DIM), lambda b, p, js, ni: (b, 0, p)),
            pl.BlockSpec((1, nh * FOX_HEAD_DIM, S), lambda b, p, js, ni: (b, p, 0)),
            pl.BlockSpec((1, FOX_HEADS, S), lambda b, p, js, ni: (b, 0, 0)),
            pl.BlockSpec((1, 1, nh * n_blk, tq), lambda b, p, js, ni: (b, p, 0, 0)),
        ],
        out_specs=pl.BlockSpec((1, S, nh * FOX_HEAD_DIM), lambda b, p, js, ni: (b, 0, p)),
        scratch_shapes=[
            pltpu.VMEM((2 * nh, MXU_DIM, tq), BF16),
            pltpu.VMEM((nh, tk, tq), F32),
            pltpu.VMEM((nh, tk, tq), BF16),
            pltpu.VMEM((nh * (tk // tq), SUBLANES, tq), F32),
            pltpu.VMEM((nh, FOX_HEAD_DIM + FOX_SUM_ROWS, tq), F32),
            pltpu.VMEM((nh, 1, tq), F32),
            pltpu.VMEM((nh, 1, tq), F32),
            pltpu.VMEM((nh * (tk // tq), 1, tq), F32),
            pltpu.VMEM((nh, 1, tq), F32),
            pltpu.VMEM((2, tq, tq), F32),
            pltpu.VMEM((nh, FOX_HEAD_DIM + FOX_SUM_ROWS, tq), F32),
        ])
    return pl.pallas_call(
        _fox_kernel,
        grid_spec=grid_spec,
        out_shape=jax.ShapeDtypeStruct((B, S, FOX_WIDTH), BF16),
        compiler_params=pltpu.CompilerParams(
            dimension_semantics=("arbitrary", "arbitrary"),
            vmem_limit_bytes=VMEM_LIMIT),
        name="fox_attn",
    )(n_tiles, n_items, fqT, kaug, fvT, drel, crep)


def _gla_kernel(q_ref, kT_ref, v_ref, gate_ref, sm_ref, wa_ref, ba_ref, o_ref, state_ref):
    nb, ts = q_ref.shape[0], q_ref.shape[1]
    C, P = GLA_CHUNK, GLA_PAIR
    dk, dv, H = GLA_KEY_DIM, GLA_VAL_DIM, GLA_HEADS
    KW = GLA_KEY_WIDTH

    @pl.when(pl.program_id(0) == 0)
    def _():
        state_ref[...] = jnp.zeros_like(state_ref)

    la_all = [_log_sigmoid(_dot(sm_ref[bb].astype(BF16), wa_ref[...]) + ba_ref[...])
              * (1.0 / GLA_GATE_TEMP) for bb in range(nb)]

    ri = lax.broadcasted_iota(jnp.int32, (P, P), 0)
    ci = lax.broadcasted_iota(jnp.int32, (P, P), 1)
    tril2 = jnp.where(((ri < C) == (ci < C)) & (ci <= ri), 1.0, 0.0).astype(BF16)

    rs = lax.broadcasted_iota(jnp.int32, (2 * H * C, P), 0)
    cs = lax.broadcasted_iota(jnp.int32, (2 * H * C, P), 1)
    causal = ((rs >= H * C) == (cs >= C)) & ((cs & (C - 1)) <= (rs & (C - 1)))

    lane_q = lax.broadcasted_iota(jnp.int32, (C, KW), 1)
    lane_t = lax.broadcasted_iota(jnp.int32, (KW, P), 1)
    first = lane_t < C

    per_b = ts // P
    n_slabs = nb * per_b
    slab_b = [k // per_b for k in range(n_slabs)]
    slab_rows = [slice((k % per_b) * P, (k % per_b + 1) * P) for k in range(n_slabs)]
    bs = [_tri_left(tril2, la_all[slab_b[k]][slab_rows[k]]) for k in range(n_slabs)]

    qms, scores, kv0s, kv1s, dec0s, dec1s, vss = [], [], [], [], [], [], []
    for pi in range(n_slabs):
        bb, rows = slab_b[pi], slab_rows[pi]
        b = bs[pi]
        bT = b.T
        bl0 = bT[:, C - 1:C]
        bl1 = bT[:, P - 1:P]
        q_dec = q_ref[bb, rows, :] * jnp.exp(b)
        kT = kT_ref[bb, :, rows]
        k_invT = (kT * jnp.exp(-bT)).astype(BF16)
        k_teT = kT * jnp.exp(jnp.where(first, bl0, bl1) - bT)
        zt = jnp.zeros_like(k_teT)
        k_te0 = jnp.where(first, k_teT, zt).astype(BF16)
        k_te1 = jnp.where(first, zt, k_teT).astype(BF16)
        dec0 = jnp.exp(bl0)
        dec1 = jnp.exp(bl1)

        vs = [v_ref[bb, rows, h * dv:(h + 1) * dv] for h in range(H)]
        kv = [_dot(jnp.concatenate([k_te0[h * dk:(h + 1) * dk, :], k_te1[h * dk:(h + 1) * dk, :]], axis=0),
                   vs[h]) for h in range(H)]
        kv0s.append(jnp.concatenate([kv[h][:dk] for h in range(H)], axis=0))
        kv1s.append(jnp.concatenate([kv[h][dk:] for h in range(H)], axis=0))
        dec0s.append(dec0)
        dec1s.append(dec1)
        vss.append(vs)

        zq = jnp.zeros((C, KW), F32)
        qm = [jnp.concatenate(
            [jnp.where((lane_q >= h * dk) & (lane_q < (h + 1) * dk), q_dec[c * C:(c + 1) * C], zq)
             for h in range(H)], axis=0).astype(BF16) for c in range(2)]
        a = _dot(jnp.concatenate(qm, axis=0), k_invT)
        scores.append(jnp.where(causal, a, 0.0).astype(BF16))
        qms.append(qm)

    o_inters = []
    for pi in range(n_slabs):
        if pi % per_b == 0:
            s = state_ref[slab_b[pi]]
        s1 = dec0s[pi] * s + kv0s[pi]
        o_inters.append([_dot(qms[pi][0], s.astype(BF16)), _dot(qms[pi][1], s1.astype(BF16))])
        s = dec1s[pi] * s1 + kv1s[pi]
        if pi % per_b == per_b - 1:
            state_ref[slab_b[pi]] = s

    for pi in range(n_slabs):
        bb, rows = slab_b[pi], slab_rows[pi]
        a, o_inter, vs = scores[pi], o_inters[pi], vss[pi]
        for h in range(H):
            hv = slice(h * dv, (h + 1) * dv)
            a_h = jnp.concatenate([a[(c * H + h) * C:(c * H + h + 1) * C] for c in range(2)], axis=0)
            o = _dot(a_h, vs[h]) + jnp.concatenate(
                [o_inter[c][h * C:(h + 1) * C] for c in range(2)], axis=0)
            o = o * lax.rsqrt(jnp.mean(o * o, axis=-1, keepdims=True) + RMS_EPS)
            o_ref[bb, rows, hv] = (o * gate_ref[bb, rows, hv]).astype(o_ref.dtype)


def _gla_attention(gq, gkT, gv, gate, small, w_a2_pad, b_a):
    B, S, _ = gq.shape
    ts = GLA_TS
    tok = lambda w: pl.BlockSpec((B, ts, w), lambda s: (0, s, 0))
    full = lambda r, c: pl.BlockSpec((r, c), lambda s: (0, 0))
    return pl.pallas_call(
        _gla_kernel,
        grid=(S // ts,),
        in_specs=[
            tok(GLA_KEY_WIDTH), pl.BlockSpec((B, GLA_KEY_WIDTH, ts), lambda s: (0, 0, s)),
            tok(GLA_WIDTH), tok(GLA_WIDTH), tok(SMALL_W),
            full(SMALL_W, GLA_KEY_WIDTH), full(1, GLA_KEY_WIDTH),
        ],
        out_specs=tok(GLA_WIDTH),
        out_shape=jax.ShapeDtypeStruct((B, S, GLA_WIDTH), BF16),
        scratch_shapes=[pltpu.VMEM((B, GLA_KEY_WIDTH, GLA_VAL_DIM), F32)],
        compiler_params=pltpu.CompilerParams(
            dimension_semantics=("arbitrary",), vmem_limit_bytes=VMEM_LIMIT),
        name="gla_attn",
    )(gq, gkT, gv, gate, small, w_a2_pad, b_a)


def _layer_norm(z, g, b):
    mu = jnp.mean(z, axis=-1, keepdims=True)
    d = z - mu
    var = jnp.mean(d * d, axis=-1, keepdims=True)
    return d * lax.rsqrt(var + LN_EPS) * g + b


def _tail_kernel(x_ref, fox_ref, gla_ref, mod_ref, wof_ref, wog_ref, ln1g_ref, ln1b_ref,
                 wg_ref, wu_ref, wd_ref, ln2g_ref, ln2b_ref, o_ref):
    m = mod_ref[0]
    tm = x_ref.shape[0]
    halves = [slice(0, tm // 2), slice(tm // 2, tm)]
    ys = [_dot(fox_ref[r, :], wof_ref[...]) + _dot(gla_ref[r, :], wog_ref[...]) for r in halves]
    x1s, gs, ups = [], [], []
    for r, y in zip(halves, ys):
        x1 = _layer_norm(DEEPNORM_ALPHA * x_ref[r, :] + (1.0 + m[2:3]) * y,
                         ln1g_ref[...], ln1b_ref[...])
        u2 = (x1 * (1.0 + m[4:5]) + m[3:4]).astype(BF16)
        x1s.append(x1)
        gs.append(_dot(u2, wg_ref[...]))
        ups.append(_dot(u2, wu_ref[...]))
    y2s = []
    for g, up in zip(gs, ups):
        h = (g * (1.0 / (1.0 + jnp.exp(-g))) * up).astype(BF16)
        y2s.append(_dot(h, wd_ref[...]))
    for r, x1, y2 in zip(halves, x1s, y2s):
        o_ref[r, :] = _layer_norm(DEEPNORM_ALPHA * x1 + (1.0 + m[5:6]) * y2,
                                  ln2g_ref[...], ln2b_ref[...])


def _tail(x2d, fox2d, gla2d, mod, wo, ln1g, ln1b, wg, wu, wd, ln2g, ln2b, steps_per_batch):
    T, D = x2d.shape
    tm = TAIL_TM
    const = lambda r, c: pl.BlockSpec((r, c), lambda i: (0, 0), pipeline_mode=pl.Buffered(1))
    tok = lambda w: pl.BlockSpec((tm, w), lambda i: (i, 0))
    return pl.pallas_call(
        _tail_kernel,
        grid=(T // tm,),
        in_specs=[
            tok(D), tok(FOX_WIDTH), tok(GLA_WIDTH),
            pl.BlockSpec((1, N_MOD, D), lambda i: (i // steps_per_batch, 0, 0)),
            pl.BlockSpec((FOX_WIDTH, D), lambda i: (0, 0), pipeline_mode=pl.Buffered(1)),
            pl.BlockSpec((GLA_WIDTH, D), lambda i: (1, 0), pipeline_mode=pl.Buffered(1)),
            const(1, D), const(1, D),
            const(D, D_FF), const(D, D_FF), const(D_FF, D), const(1, D), const(1, D),
        ],
        out_specs=tok(D),
        out_shape=jax.ShapeDtypeStruct((T, D), F32),
        compiler_params=pltpu.CompilerParams(
            dimension_semantics=("arbitrary",), vmem_limit_bytes=VMEM_LIMIT),
        name="tail",
    )(x2d, fox2d, gla2d, mod, wo, wo, ln1g, ln1b, wg, wu, wd, ln2g, ln2b)


def kernel(x, c, w_c, b_c, w_in, b_f, w_a2, b_a, g_gla, w_o, ln1_g, ln1_b,
           w_gate, w_up, w_down, ln2_g, ln2_b):
    B, S, D = x.shape
    assert D == D_MODEL and S % INPROJ_TM == 0 and INPROJ_TM == FOX_TK and FOX_TK % FOX_TQ == 0

    b_f_pad = jnp.pad(b_f.reshape(1, FOX_HEADS), ((0, 0), (0, SMALL_W - FOX_HEADS)))
    w_a2_pad = jnp.pad(w_a2, ((FOX_HEADS, SMALL_W - FOX_HEADS - GLA_GATE_RANK), (0, 0))).astype(BF16)

    mod, w_all = _modulation(c, w_c, b_c, w_in)

    fqT, kaug, fvT, gq, gkT, gv, gate, small, drel, stats, wg16, wu16, wd16, wo16 = _in_projection(
        x, mod, w_all, b_f_pad, g_gla.reshape(1, -1), w_gate, w_up, w_down, w_o)
    fox = _fox_attention(fqT, kaug, fvT, drel, stats)
    gla = _gla_attention(gq, gkT, gv, gate, small, w_a2_pad, b_a.reshape(1, -1))

    T = B * S
    out = _tail(x.reshape(T, D), fox.reshape(T, FOX_WIDTH), gla.reshape(T, GLA_WIDTH), mod,
                wo16,
                ln1_g.reshape(1, D), ln1_b.reshape(1, D),
                wg16, wu16, wd16,
                ln2_g.reshape(1, D), ln2_b.reshape(1, D), S // TAIL_TM)
    return out.reshape(B, S, D)
```

```python
import functools
import math

import numpy as np
import jax
import jax.numpy as jnp
from jax import lax
from jax.experimental import pallas as pl
from jax.experimental.pallas import tpu as pltpu

F32 = jnp.float32
BF16 = jnp.bfloat16

D_MODEL = 1024
FOX_WIDTH = 512
FOX_HEAD_DIM = 64
FOX_HEADS = 8
FOX_PAIRS = FOX_HEADS // 2
GLA_WIDTH = 512
GLA_HEADS = 4
GLA_VAL_DIM = 128
GLA_KEY_DIM = 64
GLA_KEY_WIDTH = 256
GLA_GATE_RANK = 16
GLA_GATE_TEMP = 16.0
GLA_CHUNK = 64
D_FF = 2816
LN_EPS = 1e-5
RMS_EPS = 1e-6
N_MOD = 6
DEEPNORM_ALPHA = 2.0 ** 0.25
IN_SPLITS = (FOX_WIDTH, FOX_WIDTH, FOX_WIDTH, FOX_HEADS, GLA_KEY_WIDTH, GLA_KEY_WIDTH,
             GLA_WIDTH, GLA_GATE_RANK, GLA_WIDTH)
LOG2E = math.log2(math.e)

LANES = 128
SUBLANES = 8
MXU_DIM = 256
SMALL_W = LANES
VMEM_LIMIT = 56 * 1024 * 1024

MOD_TN = 768
INPROJ_TM = 512
FOX_TQ = 256
FOX_GROUP_HEADS = 4
FOX_SUM_ROWS = 16
FOX_TK = 512
GLA_TS = 512
GLA_PAIR = 2 * GLA_CHUNK
TAIL_TM = 512

NEG_BIG = -1e30
PRUNE_LOG2 = -150.0
NORM_SLACK = 1.02

AUG_ROWS = 32


def _log_sigmoid(z):
    return jnp.minimum(z, 0.0) - jnp.log(1.0 + jnp.exp(-jnp.abs(z)))


def _split3(a):
    hi = a.astype(BF16)
    r1 = a - hi.astype(F32)
    mid = r1.astype(BF16)
    lo = (r1 - mid.astype(F32)).astype(BF16)
    return hi, mid, lo


def _dot(a, b):
    return jnp.dot(a, b, preferred_element_type=F32)


def _tri_left(tri, a):
    hi, mid, lo = _split3(a)
    return _dot(tri, hi) + _dot(tri, mid) + _dot(tri, lo)


def _tri_right(a, tri):
    hi, mid, lo = _split3(a)
    return _dot(hi, tri) + _dot(mid, tri) + _dot(lo, tri)


def _regroup_blocks():
    blocks = []
    for dst, src, width in _W_GROUPS:
        blocks += [(dst + c0, src + c0) for c0 in range(0, width, MXU_DIM)]
    blocks.append((_C_SM, None))
    return blocks


def _mod_kernel(cT_ref, w_ref, b_ref, winT_ref, o_ref, wall_ref, *, n_rows):
    w = w_ref[...]
    outs = [jnp.sum(w * cT_ref[:, b:b + 1], axis=0, keepdims=True) for b in range(n_rows)]
    outs.append(jnp.zeros((o_ref.shape[0] - n_rows, w.shape[1]), F32))
    o_ref[...] = jnp.concatenate(outs, axis=0) + b_ref[...]

    j = pl.program_id(0)
    n_steps = pl.num_programs(0)
    for k, (dst, src) in enumerate(_regroup_blocks()):
        @pl.when(j == k % n_steps)
        def _():
            if src is None:
                blk = jnp.concatenate(
                    [winT_ref[_SRC_FF:_SRC_FF + FOX_HEADS, :], winT_ref[_SRC_GA:_SRC_GA + GLA_GATE_RANK, :],
                     jnp.zeros((MXU_DIM - FOX_HEADS - GLA_GATE_RANK, winT_ref.shape[1]), F32)], axis=0)
            else:
                blk = winT_ref[src:src + MXU_DIM, :]
            wall_ref[:, dst:dst + MXU_DIM] = blk.T.astype(BF16)


def _modulation(c, w_c, b_c, w_in):
    B, D = c.shape
    N = w_c.shape[1]
    rows = SUBLANES
    n_steps = N // MOD_TN
    cT_pad = jnp.pad(c.T, ((0, 0), (0, LANES - B)))
    w_inT = w_in.T
    out, w_all = pl.pallas_call(
        functools.partial(_mod_kernel, n_rows=B),
        grid=(n_steps,),
        in_specs=[
            pl.BlockSpec((D, LANES), lambda j: (0, 0)),
            pl.BlockSpec((D, MOD_TN), lambda j: (0, j)),
            pl.BlockSpec((1, MOD_TN), lambda j: (0, j)),
            pl.BlockSpec(w_inT.shape, lambda j: (0, 0), pipeline_mode=pl.Buffered(1)),
        ],
        out_specs=[pl.BlockSpec((rows, MOD_TN), lambda j: (0, j)),
                   pl.BlockSpec((D, _W_COLS), lambda j: (0, 0))],
        out_shape=[jax.ShapeDtypeStruct((rows, N), F32),
                   jax.ShapeDtypeStruct((D, _W_COLS), BF16)],
        compiler_params=pltpu.CompilerParams(
            dimension_semantics=("arbitrary",), vmem_limit_bytes=VMEM_LIMIT),
        name="adaln_mod",
    )(cT_pad, w_c, b_c.reshape(1, N), w_inT)
    return out[:B].reshape(B, N_MOD, D), w_all


_C_FQ, _C_FK, _C_FV = 0, 512, 1024
_C_GQ, _C_SM, _C_GK, _C_GV, _C_GR = 1536, 1792, 2048, 2304, 2816
_W_COLS = _C_GR + GLA_WIDTH
_SRC = [sum(IN_SPLITS[:i]) for i in range(len(IN_SPLITS))]
_SRC_FF, _SRC_GA = _SRC[3], _SRC[7]
_W_GROUPS = ((_C_FQ, _SRC[0], FOX_WIDTH), (_C_FK, _SRC[1], FOX_WIDTH), (_C_FV, _SRC[2], FOX_WIDTH),
             (_C_GQ, _SRC[4], GLA_KEY_WIDTH), (_C_GK, _SRC[5], GLA_KEY_WIDTH),
             (_C_GV, _SRC[6], GLA_WIDTH), (_C_GR, _SRC[8], GLA_WIDTH))
_KAUG_W = FOX_PAIRS * MXU_DIM


def _head_indicator():
    ind = np.zeros((FOX_WIDTH, MXU_DIM), np.float32)
    for h in range(FOX_HEADS):
        ind[h * FOX_HEAD_DIM:(h + 1) * FOX_HEAD_DIM, h] = 1.0
    return ind


def _inproj_kernel(x_ref, mod_ref, w_ref, bf_ref, ind_ref, gg_ref, wg32_ref, wu32_ref, wd32_ref, wo32_ref,
                   fqT_ref, kaug_ref, fvT_ref, gq_ref, gk_ref, gv_ref, gr_ref, sm_ref,
                   drel_ref, stats_ref, wg16_ref, wu16_ref, wd16_ref, wo16_ref, carry_ref):
    tm = x_ref.shape[1]

    wg16_ref[...] = wg32_ref[...].astype(BF16)
    wu16_ref[...] = wu32_ref[...].astype(BF16)
    wd16_ref[...] = wd32_ref[...].astype(BF16)
    wo16_ref[...] = wo32_ref[...].astype(BF16)

    @pl.when(pl.program_id(1) == 0)
    def _():
        carry_ref[...] = jnp.zeros_like(carry_ref)

    m = mod_ref[0]
    u = (x_ref[0] * (1.0 + m[1:2]) + m[0:1]).astype(BF16)

    def proj(lo, width):
        return _dot(u, w_ref[:, lo:lo + width])

    gq_small = proj(_C_GQ, 2 * GLA_KEY_WIDTH)
    gq_ref[0] = gq_small[:, :GLA_KEY_WIDTH] * (GLA_KEY_DIM ** -0.5)
    small = gq_small[:, GLA_KEY_WIDTH:GLA_KEY_WIDTH + SMALL_W]
    sm_ref[0] = small

    fq = proj(_C_FQ, FOX_WIDTH) * (FOX_HEAD_DIM ** -0.5 * LOG2E)
    fqT_ref[0] = fq.T.astype(BF16)
    fk32 = proj(_C_FK, FOX_WIDTH)
    fk = fk32.astype(BF16)
    qn2 = jnp.max(_dot((fq * fq).astype(BF16), ind_ref[...])[:, :SMALL_W], axis=0, keepdims=True)
    kn2 = jnp.max(_dot((fk32 * fk32).astype(BF16), ind_ref[...])[:, :SMALL_W], axis=0, keepdims=True)

    zT = (small + bf_ref[...]).T[:FOX_HEADS, :]
    lfT = _log_sigmoid(zT) * LOG2E
    r = lax.broadcasted_iota(jnp.int32, (tm, tm), 0)
    c = lax.broadcasted_iota(jnp.int32, (tm, tm), 1)
    triu = jnp.where(r <= c, 1.0, 0.0).astype(BF16)
    drelT = _tri_right(lfT, triu)
    drel_ref[0] = drelT

    c_before = carry_ref[...]
    srow = lax.broadcasted_iota(jnp.int32, (SUBLANES, SMALL_W), 0)
    slane = lax.broadcasted_iota(jnp.int32, (SUBLANES, SMALL_W), 1)
    total = jnp.sum(jnp.where(srow == slane, drelT[:, tm - 1:tm], 0.0), axis=0, keepdims=True)
    carry_ref[...] = c_before + total

    hi, mid, lo = _split3(drelT)
    ones3 = jnp.where(lax.broadcasted_iota(jnp.int32, (SUBLANES, tm), 0) < 3, 1.0, 0.0)
    slabT = jnp.concatenate(
        [-hi.astype(F32), -mid.astype(F32), -lo.astype(F32), ones3,
         jnp.zeros((LANES - 4 * SUBLANES, tm), F32)], axis=0)
    slab = slabT.T.astype(BF16)

    fvT_ref[0] = proj(_C_FV, FOX_WIDTH).T.astype(BF16)
    gv_ref[0] = proj(_C_GV, GLA_WIDTH).astype(BF16)
    gr = proj(_C_GR, GLA_WIDTH)
    gr_ref[0] = gg_ref[...] * (gr * (1.0 / (1.0 + jnp.exp(-gr))))
    gk_ref[0] = proj(_C_GK, GLA_KEY_WIDTH).T

    stats_ref[0, 0] = jnp.where(srow == 0, c_before,
                                jnp.where(srow == 1, qn2, jnp.where(srow == 2, kn2, 0.0)))
    for p in range(FOX_PAIRS):
        kaug_ref[0, :, p * MXU_DIM:p * MXU_DIM + LANES] = fk[:, p * LANES:(p + 1) * LANES]
        kaug_ref[0, :, p * MXU_DIM + LANES:(p + 1) * MXU_DIM] = slab


def _in_projection(x, mod, w_all, b_f_pad, g_gla, w_gate, w_up, w_down, w_o):
    B, S, D = x.shape
    tm = INPROJ_TM
    n_steps = B * (S // tm)
    steps_b = S // tm
    up_rows = D // n_steps
    down_rows = 2 * D_FF // n_steps
    assert D % n_steps == 0 and up_rows % 16 == 0 and (2 * D_FF) % n_steps == 0 and down_rows % 16 == 0
    up_spec = pl.BlockSpec((up_rows, D_FF), lambda b, s: (b * steps_b + s, 0))
    down_spec = pl.BlockSpec((down_rows, D), lambda b, s: ((b * steps_b + s) // 2, 0))
    wo_rows = w_o.shape[0] // n_steps
    assert w_o.shape[0] % n_steps == 0 and wo_rows % 16 == 0
    wo_spec = pl.BlockSpec((wo_rows, D), lambda b, s: (b * steps_b + s, 0))
    ind = _head_indicator()
    tok = lambda w: pl.BlockSpec((1, tm, w), lambda b, s: (b, s, 0))
    tokT = lambda h: pl.BlockSpec((1, h, tm), lambda b, s: (b, 0, s))
    outs = pl.pallas_call(
        _inproj_kernel,
        grid=(B, S // tm),
        in_specs=[
            tok(D),
            pl.BlockSpec((1, N_MOD, D), lambda b, s: (b, 0, 0)),
            pl.BlockSpec((D, _W_COLS), lambda b, s: (0, 0), pipeline_mode=pl.Buffered(1)),
            pl.BlockSpec((1, SMALL_W), lambda b, s: (0, 0)),
            pl.BlockSpec(ind.shape, lambda b, s: (0, 0)),
            pl.BlockSpec((1, GLA_WIDTH), lambda b, s: (0, 0)),
            up_spec, up_spec, down_spec, wo_spec,
        ],
        out_specs=[
            tokT(FOX_WIDTH), tok(_KAUG_W), tokT(FOX_WIDTH),
            tok(GLA_KEY_WIDTH), tokT(GLA_KEY_WIDTH), tok(GLA_WIDTH), tok(GLA_WIDTH),
            tok(SMALL_W),
            tokT(FOX_HEADS),
            pl.BlockSpec((1, 1, SUBLANES, SMALL_W), lambda b, s: (b, s, 0, 0)),
            up_spec, up_spec, down_spec, wo_spec,
        ],
        out_shape=[
            jax.ShapeDtypeStruct((B, FOX_WIDTH, S), BF16),
            jax.ShapeDtypeStruct((B, S, _KAUG_W), BF16),
            jax.ShapeDtypeStruct((B, FOX_WIDTH, S), BF16),
            jax.ShapeDtypeStruct((B, S, GLA_KEY_WIDTH), F32),
            jax.ShapeDtypeStruct((B, GLA_KEY_WIDTH, S), F32),
            jax.ShapeDtypeStruct((B, S, GLA_WIDTH), BF16),
            jax.ShapeDtypeStruct((B, S, GLA_WIDTH), F32),
            jax.ShapeDtypeStruct((B, S, SMALL_W), F32),
            jax.ShapeDtypeStruct((B, FOX_HEADS, S), F32),
            jax.ShapeDtypeStruct((B, S // tm, SUBLANES, SMALL_W), F32),
            jax.ShapeDtypeStruct(w_gate.shape, BF16),
            jax.ShapeDtypeStruct(w_up.shape, BF16),
            jax.ShapeDtypeStruct(w_down.shape, BF16),
            jax.ShapeDtypeStruct(w_o.shape, BF16),
        ],
        scratch_shapes=[pltpu.VMEM((1, SMALL_W), F32)],
        compiler_params=pltpu.CompilerParams(
            dimension_semantics=("arbitrary", "arbitrary"), vmem_limit_bytes=VMEM_LIMIT),
        name="in_proj",
    )(x, mod, w_all, b_f_pad, jnp.asarray(ind, BF16), g_gla, w_gate, w_up, w_down, w_o)
    return outs


def _fox_kernel(ntiles_ref, nitems_ref, qT_ref, kaug_ref, vT_ref, drel_ref, crep_ref, o_ref,
                qaug_ref, s_ref, p_ref, mx_ref, acc_ref, m_ref, mnext_ref, shift_ref, alpha_ref, keep_ref, done_ref):
    tq, tk = FOX_TQ, FOX_TK
    half = FOX_HEAD_DIM
    S = kaug_ref.shape[1]
    nq = S // tq
    nh = FOX_GROUP_HEADS
    n_blk = crep_ref.shape[2] // nh
    g = pl.program_id(0) * pl.num_programs(1) + pl.program_id(1)

    def build_qaug(qi, slot):
        q0 = pl.multiple_of(qi * tq, tq)
        dr_all = drel_ref[0, :, pl.ds(q0, tq)]
        dr = dr_all[0:nh]
        for grp in range(1, dr_all.shape[0] // nh):
            dr = jnp.where(pl.program_id(1) == grp, dr_all[grp * nh:(grp + 1) * nh], dr)
        row = lax.broadcasted_iota(jnp.int32, (2 * half, tq), 0)
        r32 = lax.broadcasted_iota(jnp.int32, (AUG_ROWS, tq), 0)
        for h in range(nh):
            buf = slot * nh + h
            qT = qT_ref[0, (h // 2) * 2 * half:(h // 2 + 1) * 2 * half, pl.ds(q0, tq)]
            own = (row < half) if h % 2 == 0 else (row >= half)
            qaug_ref[buf, 0:2 * half, :] = jnp.where(own, qT, jnp.zeros_like(qT))
            hi, mid, lo = _split3(dr[h:h + 1])
            gh = nh * pl.program_id(1) + h
            pick = (r32 == gh) | (r32 == gh + SUBLANES) | (r32 == gh + 2 * SUBLANES)
            slab = jnp.where(r32 == 24, hi.astype(F32),
                             jnp.where(r32 == 25, mid.astype(F32),
                                       jnp.where(r32 == 26, lo.astype(F32),
                                                 jnp.where(pick, 1.0, 0.0))))
            qaug_ref[buf, 2 * half:2 * half + AUG_ROWS, :] = slab.astype(BF16)

    def reset_state():
        acc_ref[...] = jnp.zeros_like(acc_ref)

    sum_rows = jnp.where(lax.broadcasted_iota(jnp.int32, (FOX_SUM_ROWS, tk), 0) == 0, 1.0, 0.0).astype(BF16)

    per = tk // tq

    def col_max8(s):
        return jnp.max(s.reshape(s.shape[0] // SUBLANES, SUBLANES, tq), axis=0)

    def tile_of(qi, t):
        raw = (qi + 1) * tq - (t + 1) * tk
        return pl.multiple_of(jnp.maximum(raw, 0), tq), raw < 0

    def qk_dots(slot, k_off):
        kts = [kaug_ref[0, pl.ds(k_off, tk), p * MXU_DIM:(p + 1) * MXU_DIM] for p in range(nh // 2)]
        return [_dot(kts[h // 2], qaug_ref[slot * nh + h]) for h in range(nh)]

    def store_scores(scores, diag):
        which = jnp.where(diag, 1, 0)
        for h in range(nh):
            for g in range(per):
                blk = scores[h][g * tq:(g + 1) * tq]
                if g == per - 1:
                    blk = jnp.where(keep_ref[which] > 0.5, blk, NEG_BIG)
                s_ref[h, g * tq:(g + 1) * tq, :] = blk
                mx_ref[per * h + g] = col_max8(blk)

    def fill_upper_groups():
        for g in range(1, per):
            for h in range(nh):
                s_ref[h, g * tq:(g + 1) * tq, :] = jnp.full((tq, tq), NEG_BIG, F32)
                mx_ref[per * h + g] = jnp.full((SUBLANES, tq), NEG_BIG, F32)

    def mask_clamped_diagonal():
        kr = lax.broadcasted_iota(jnp.int32, (tq, tq), 0)
        qc = lax.broadcasted_iota(jnp.int32, (tq, tq), 1)
        for h in range(nh):
            tri = jnp.where(kr <= qc, s_ref[h, 0:tq, :], NEG_BIG)
            s_ref[h, 0:tq, :] = tri
            mx_ref[per * h] = col_max8(tri)
        fill_upper_groups()

    def prep_stage(qi, k_off, first):
        i_blk = qi * tq // INPROJ_TM
        for h in range(nh):
            c_i = crep_ref[0, 0, pl.ds(h * n_blk + i_blk, 1), :]
            big_c = [c_i - crep_ref[0, 0, pl.ds(h * n_blk + (k_off + g * tq) // INPROJ_TM, 1), :]
                     for g in range(per)]
            cand = jnp.max(mx_ref[per * h], axis=0, keepdims=True) + big_c[0]
            for g in range(1, per):
                cand = jnp.maximum(cand, jnp.max(mx_ref[per * h + g], axis=0, keepdims=True) + big_c[g])
            m_prev = jnp.where(first, NEG_BIG, m_ref[h])
            m_new = jnp.maximum(m_prev, cand)
            mnext_ref[h] = m_new
            for g in range(per):
                shift_ref[per * h + g] = m_new - big_c[g]
            alpha_ref[h] = jnp.exp2(m_prev - m_new)

    def exp_stage():
        for h in range(nh):
            for g in range(per):
                rows = slice(g * tq, (g + 1) * tq)
                p_ref[h, rows, :] = jnp.exp2(s_ref[h, rows, :] - shift_ref[per * h + g]).astype(BF16)

    def pv_stage(k_off):
        for h in range(nh):
            vT = jnp.concatenate([vT_ref[0, h * half:(h + 1) * half, pl.ds(k_off, tk)], sum_rows], axis=0)
            acc_ref[h] = alpha_ref[h] * acc_ref[h] + _dot(vT, p_ref[h])
            m_ref[h] = mnext_ref[h]

    def finalize(src_ref, qi):
        outs = []
        for h in range(nh):
            acc = src_ref[h]
            outs.append(acc[:half] / acc[half:half + 1])
        oT = jnp.concatenate(outs, axis=0)
        o_ref[0, pl.ds(pl.multiple_of(qi * tq, tq), tq), :] = oT.T.astype(o_ref.dtype)

    def fix_clamped(t):
        @pl.when(t == 0)
        def _():
            mask_clamped_diagonal()

        @pl.when(t > 0)
        def _():
            fill_upper_groups()

    zero = jnp.int32(0)
    qaug_ref[...] = jnp.zeros_like(qaug_ref)
    keep_ref[0] = jnp.ones((tq, tq), F32)
    keep_ref[1] = jnp.where(lax.broadcasted_iota(jnp.int32, (tq, tq), 0)
                            <= lax.broadcasted_iota(jnp.int32, (tq, tq), 1), 1.0, 0.0)
    build_qaug(zero, zero)
    build_qaug(zero + 1, zero + 1)
    reset_state()
    done_ref[...] = jnp.ones_like(done_ref)
    k0, _ = tile_of(zero, zero)
    store_scores(qk_dots(zero, k0), False)
    mask_clamped_diagonal()
    prep_stage(zero, k0, True)

    def body(it, carry):
        qi, t, qd, sb = carry
        last = t == 0
        nxt_set = lambda s: jnp.where(s == 2, 0, s + 1)
        build_qaug(jnp.minimum(qi + 2, nq - 1), nxt_set(nxt_set(sb)))
        sb2 = jnp.where(last, nxt_set(sb), sb)
        finalize(done_ref, qd)
        qi2 = jnp.where(last, qi + 1, qi)
        t2 = jnp.where(last, ntiles_ref[g * nq + jnp.minimum(qi + 1, nq - 1)] - 1, t - 1)
        k_cur, _ = tile_of(qi, t)
        k_nxt, cl_nxt = tile_of(qi2, t2)
        exp_stage()
        nxt = qk_dots(sb2, k_nxt)
        pv_stage(k_cur)
        store_scores(nxt, (t2 == 0) & jnp.logical_not(cl_nxt))
        prep_stage(qi2, k_nxt, last)

        @pl.when(cl_nxt)
        def _():
            fix_clamped(t2)
            prep_stage(qi2, k_nxt, last)

        @pl.when(last)
        def _():
            done_ref[...] = acc_ref[...]
            reset_state()

        return qi2, t2, jnp.where(last, qi, qd), sb2

    qi, t, qd, _ = lax.fori_loop(0, nitems_ref[g] - 1, body,
                                 (jnp.int32(0), jnp.int32(0), jnp.int32(0), jnp.int32(0)))
    finalize(done_ref, qd)
    exp_stage()
    pv_stage(tile_of(qi, t)[0])
    finalize(acc_ref, qi)


def _prune_plan(stats, drel, nq):
    B, n_blk = stats.shape[:2]
    per = FOX_TK // FOX_TQ
    c = stats[:, :, 0, :FOX_HEADS]
    qn = jnp.sqrt(stats[:, :, 1, :FOX_HEADS])
    kn = jnp.sqrt(stats[:, :, 2, :FOX_HEADS])
    d_in = jnp.transpose(drel[:, :, FOX_TQ - 1::FOX_TQ], (0, 2, 1))
    d_end = jnp.repeat(c, per, axis=1) + d_in
    d_before = jnp.concatenate([jnp.zeros_like(d_end[:, :1]), d_end[:, :-1]], axis=1)
    kn_q = jnp.repeat(kn, per, axis=1)
    qn_q = jnp.repeat(qn, per, axis=1)
    kn_prev = jnp.concatenate([jnp.zeros_like(kn_q[:, :1]), kn_q[:, :-1]], axis=1)
    kn_tile = jnp.maximum(kn_q, kn_prev)
    bound = (d_before[:, :, None, :] - d_end[:, None, :, :]
             + NORM_SLACK * qn_q[:, :, None, :] * (kn_tile[:, None, :, :] + kn_q[:, :, None, :]))
    back = np.arange(nq)[:, None] - np.arange(nq)[None, :]
    is_tile = (back > 0) & (back % per == 0)
    t_of = np.where(is_tile, back // per, 0).astype(np.int32)
    needed = is_tile[None, :, :, None] & ~(bound <= PRUNE_LOG2)
    nh = FOX_GROUP_HEADS
    needed_g = jnp.any(needed.reshape(B, nq, nq, FOX_HEADS // nh, nh), axis=-1)
    n_tiles = 1 + jnp.max(jnp.where(needed_g, t_of[None, :, :, None], 0), axis=2)
    n_tiles = jnp.transpose(n_tiles, (0, 2, 1))
    n_items = jnp.sum(n_tiles, axis=2)
    return n_tiles.reshape(-1).astype(jnp.int32), n_items.reshape(-1).astype(jnp.int32)


def _fox_attention(fqT, kaug, fvT, drel, stats):
    B, _, S = fqT.shape
    tq, tk = FOX_TQ, FOX_TK
    n_blk = S // INPROJ_TM
    nh = FOX_GROUP_HEADS
    n_groups = FOX_HEADS // nh
    c = stats[:, :, 0, :FOX_HEADS]
    c = jnp.transpose(c, (0, 2, 1)).reshape(B, n_groups, nh * n_blk, 1)
    crep = jnp.broadcast_to(c, (B, n_groups, nh * n_blk, tq))
    n_tiles, n_items = _prune_plan(stats, drel, S // tq)
    grid_spec = pltpu.PrefetchScalarGridSpec(
        num_scalar_prefetch=2,
        grid=(B, n_groups),
        in_specs=[
            pl.BlockSpec((1, nh * FOX_HEAD_DIM, S), lambda b, p, js, ni: (b, p, 0)),
            pl.BlockSpec((1, S, nh // 2 * MXU_DIM), lambda b, p, js, ni: (b, 0, p)),
            pl.BlockSpec((1, nh * FOX_HEAD_DIM, S), lambda b, p, js, ni: (b, p, 0)),
            pl.BlockSpec((1, FOX_HEADS, S), lambda b, p, js, ni: (b, 0, 0)),
            pl.BlockSpec((1, 1, nh * n_blk, tq), lambda b, p, js, ni: (b, p, 0, 0)),
        ],
        out_specs=pl.BlockSpec((1, S, nh * FOX_HEAD_DIM), lambda b, p, js, ni: (b, 0, p)),
        scratch_shapes=[
            pltpu.VMEM((3 * nh, MXU_DIM, tq), BF16),
            pltpu.VMEM((nh, tk, tq), F32),
            pltpu.VMEM((nh, tk, tq), BF16),
            pltpu.VMEM((nh * (tk // tq), SUBLANES, tq), F32),
            pltpu.VMEM((nh, FOX_HEAD_DIM + FOX_SUM_ROWS, tq), F32),
            pltpu.VMEM((nh, 1, tq), F32),
            pltpu.VMEM((nh, 1, tq), F32),
            pltpu.VMEM((nh * (tk // tq), 1, tq), F32),
            pltpu.VMEM((nh, 1, tq), F32),
            pltpu.VMEM((2, tq, tq), F32),
            pltpu.VMEM((nh, FOX_HEAD_DIM + FOX_SUM_ROWS, tq), F32),
        ])
    return pl.pallas_call(
        _fox_kernel,
        grid_spec=grid_spec,
        out_shape=jax.ShapeDtypeStruct((B, S, FOX_WIDTH), BF16),
        compiler_params=pltpu.CompilerParams(
            dimension_semantics=("arbitrary", "arbitrary"),
            vmem_limit_bytes=VMEM_LIMIT),
        name="fox_attn",
    )(n_tiles, n_items, fqT, kaug, fvT, drel, crep)


def _gla_kernel(q_ref, kT_ref, v_ref, gate_ref, sm_ref, wa_ref, ba_ref, o_ref, state_ref):
    nb, ts = q_ref.shape[0], q_ref.shape[1]
    C, P = GLA_CHUNK, GLA_PAIR
    dk, dv, H = GLA_KEY_DIM, GLA_VAL_DIM, GLA_HEADS
    KW = GLA_KEY_WIDTH

    @pl.when(pl.program_id(0) == 0)
    def _():
        state_ref[...] = jnp.zeros_like(state_ref)

    la_all = [_log_sigmoid(_dot(sm_ref[bb].astype(BF16), wa_ref[...]) + ba_ref[...])
              * (1.0 / GLA_GATE_TEMP) for bb in range(nb)]

    ri = lax.broadcasted_iota(jnp.int32, (P, P), 0)
    ci = lax.broadcasted_iota(jnp.int32, (P, P), 1)
    tril2 = jnp.where(((ri < C) == (ci < C)) & (ci <= ri), 1.0, 0.0).astype(BF16)

    rs = lax.broadcasted_iota(jnp.int32, (2 * H * C, P), 0)
    cs = lax.broadcasted_iota(jnp.int32, (2 * H * C, P), 1)
    causal = ((rs >= H * C) == (cs >= C)) & ((cs & (C - 1)) <= (rs & (C - 1)))

    lane_q = lax.broadcasted_iota(jnp.int32, (C, KW), 1)
    lane_t = lax.broadcasted_iota(jnp.int32, (KW, P), 1)
    first = lane_t < C

    per_b = ts // P
    n_slabs = nb * per_b
    slab_b = [k // per_b for k in range(n_slabs)]
    slab_rows = [slice((k % per_b) * P, (k % per_b + 1) * P) for k in range(n_slabs)]
    bs = [_tri_left(tril2, la_all[slab_b[k]][slab_rows[k]]) for k in range(n_slabs)]

    qms, scores, kv0s, kv1s, dec0s, dec1s, vss = [], [], [], [], [], [], []
    for pi in range(n_slabs):
        bb, rows = slab_b[pi], slab_rows[pi]
        b = bs[pi]
        bT = b.T
        bl0 = bT[:, C - 1:C]
        bl1 = bT[:, P - 1:P]
        q_dec = q_ref[bb, rows, :] * jnp.exp(b)
        kT = kT_ref[bb, :, rows]
        k_invT = (kT * jnp.exp(-bT)).astype(BF16)
        k_teT = kT * jnp.exp(jnp.where(first, bl0, bl1) - bT)
        zt = jnp.zeros_like(k_teT)
        k_te0 = jnp.where(first, k_teT, zt).astype(BF16)
        k_te1 = jnp.where(first, zt, k_teT).astype(BF16)
        dec0 = jnp.exp(bl0)
        dec1 = jnp.exp(bl1)

        vs = [v_ref[bb, rows, h * dv:(h + 1) * dv] for h in range(H)]
        kv = [_dot(jnp.concatenate([k_te0[h * dk:(h + 1) * dk, :], k_te1[h * dk:(h + 1) * dk, :]], axis=0),
                   vs[h]) for h in range(H)]
        kv0s.append(jnp.concatenate([kv[h][:dk] for h in range(H)], axis=0))
        kv1s.append(jnp.concatenate([kv[h][dk:] for h in range(H)], axis=0))
        dec0s.append(dec0)
        dec1s.append(dec1)
        vss.append(vs)

        zq = jnp.zeros((C, KW), F32)
        qm = [jnp.concatenate(
            [jnp.where((lane_q >= h * dk) & (lane_q < (h + 1) * dk), q_dec[c * C:(c + 1) * C], zq)
             for h in range(H)], axis=0).astype(BF16) for c in range(2)]
        a = _dot(jnp.concatenate(qm, axis=0), k_invT)
        scores.append(jnp.where(causal, a, 0.0).astype(BF16))
        qms.append(qm)

    o_inters = []
    for pi in range(n_slabs):
        if pi % per_b == 0:
            s = state_ref[slab_b[pi]]
        s1 = dec0s[pi] * s + kv0s[pi]
        o_inters.append([_dot(qms[pi][0], s.astype(BF16)), _dot(qms[pi][1], s1.astype(BF16))])
        s = dec1s[pi] * s1 + kv1s[pi]
        if pi % per_b == per_b - 1:
            state_ref[slab_b[pi]] = s

    for pi in range(n_slabs):
        bb, rows = slab_b[pi], slab_rows[pi]
        a, o_inter, vs = scores[pi], o_inters[pi], vss[pi]
        for h in range(H):
            hv = slice(h * dv, (h + 1) * dv)
            a_h = jnp.concatenate([a[(c * H + h) * C:(c * H + h + 1) * C] for c in range(2)], axis=0)
            o = _dot(a_h, vs[h]) + jnp.concatenate(
                [o_inter[c][h * C:(h + 1) * C] for c in range(2)], axis=0)
            o = o * lax.rsqrt(jnp.mean(o * o, axis=-1, keepdims=True) + RMS_EPS)
            o_ref[bb, rows, hv] = (o * gate_ref[bb, rows, hv]).astype(o_ref.dtype)


def _gla_attention(gq, gkT, gv, gate, small, w_a2_pad, b_a):
    B, S, _ = gq.shape
    ts = GLA_TS
    tok = lambda w: pl.BlockSpec((B, ts, w), lambda s: (0, s, 0))
    full = lambda r, c: pl.BlockSpec((r, c), lambda s: (0, 0))
    return pl.pallas_call(
        _gla_kernel,
        grid=(S // ts,),
        in_specs=[
            tok(GLA_KEY_WIDTH), pl.BlockSpec((B, GLA_KEY_WIDTH, ts), lambda s: (0, 0, s)),
            tok(GLA_WIDTH), tok(GLA_WIDTH), tok(SMALL_W),
            full(SMALL_W, GLA_KEY_WIDTH), full(1, GLA_KEY_WIDTH),
        ],
        out_specs=tok(GLA_WIDTH),
        out_shape=jax.ShapeDtypeStruct((B, S, GLA_WIDTH), BF16),
        scratch_shapes=[pltpu.VMEM((B, GLA_KEY_WIDTH, GLA_VAL_DIM), F32)],
        compiler_params=pltpu.CompilerParams(
            dimension_semantics=("arbitrary",), vmem_limit_bytes=VMEM_LIMIT),
        name="gla_attn",
    )(gq, gkT, gv, gate, small, w_a2_pad, b_a)


def _layer_norm(z, g, b):
    mu = jnp.mean(z, axis=-1, keepdims=True)
    d = z - mu
    var = jnp.mean(d * d, axis=-1, keepdims=True)
    return d * lax.rsqrt(var + LN_EPS) * g + b


def _tail_kernel(x_ref, fox_ref, gla_ref, mod_ref, wof_ref, wog_ref, ln1g_ref, ln1b_ref,
                 wg_ref, wu_ref, wd_ref, ln2g_ref, ln2b_ref, o_ref):
    m = mod_ref[0]
    tm = x_ref.shape[0]
    halves = [slice(0, tm // 2), slice(tm // 2, tm)]
    ys = [_dot(fox_ref[r, :], wof_ref[...]) + _dot(gla_ref[r, :], wog_ref[...]) for r in halves]
    x1s, gs, ups = [], [], []
    for r, y in zip(halves, ys):
        x1 = _layer_norm(DEEPNORM_ALPHA * x_ref[r, :] + (1.0 + m[2:3]) * y,
                         ln1g_ref[...], ln1b_ref[...])
        u2 = (x1 * (1.0 + m[4:5]) + m[3:4]).astype(BF16)
        x1s.append(x1)
        gs.append(_dot(u2, wg_ref[...]))
        ups.append(_dot(u2, wu_ref[...]))
    y2s = []
    for g, up in zip(gs, ups):
        h = (g * (1.0 / (1.0 + jnp.exp(-g))) * up).astype(BF16)
        y2s.append(_dot(h, wd_ref[...]))
    for r, x1, y2 in zip(halves, x1s, y2s):
        o_ref[r, :] = _layer_norm(DEEPNORM_ALPHA * x1 + (1.0 + m[5:6]) * y2,
                                  ln2g_ref[...], ln2b_ref[...])


def _tail(x2d, fox2d, gla2d, mod, wo, ln1g, ln1b, wg, wu, wd, ln2g, ln2b, steps_per_batch):
    T, D = x2d.shape
    tm = TAIL_TM
    const = lambda r, c: pl.BlockSpec((r, c), lambda i: (0, 0), pipeline_mode=pl.Buffered(1))
    tok = lambda w: pl.BlockSpec((tm, w), lambda i: (i, 0))
    return pl.pallas_call(
        _tail_kernel,
        grid=(T // tm,),
        in_specs=[
            tok(D), tok(FOX_WIDTH), tok(GLA_WIDTH),
            pl.BlockSpec((1, N_MOD, D), lambda i: (i // steps_per_batch, 0, 0)),
            pl.BlockSpec((FOX_WIDTH, D), lambda i: (0, 0), pipeline_mode=pl.Buffered(1)),
            pl.BlockSpec((GLA_WIDTH, D), lambda i: (1, 0), pipeline_mode=pl.Buffered(1)),
            const(1, D), const(1, D),
            const(D, D_FF), const(D, D_FF), const(D_FF, D), const(1, D), const(1, D),
        ],
        out_specs=tok(D),
        out_shape=jax.ShapeDtypeStruct((T, D), F32),
        compiler_params=pltpu.CompilerParams(
            dimension_semantics=("arbitrary",), vmem_limit_bytes=VMEM_LIMIT),
        name="tail",
    )(x2d, fox2d, gla2d, mod, wo, wo, ln1g, ln1b, wg, wu, wd, ln2g, ln2b)


def kernel(x, c, w_c, b_c, w_in, b_f, w_a2, b_a, g_gla, w_o, ln1_g, ln1_b,
           w_gate, w_up, w_down, ln2_g, ln2_b):
    B, S, D = x.shape
    assert D == D_MODEL and S % INPROJ_TM == 0 and INPROJ_TM == FOX_TK and FOX_TK % FOX_TQ == 0

    b_f_pad = jnp.pad(b_f.reshape(1, FOX_HEADS), ((0, 0), (0, SMALL_W - FOX_HEADS)))
    w_a2_pad = jnp.pad(w_a2, ((FOX_HEADS, SMALL_W - FOX_HEADS - GLA_GATE_RANK), (0, 0))).astype(BF16)

    mod, w_all = _modulation(c, w_c, b_c, w_in)

    fqT, kaug, fvT, gq, gkT, gv, gate, small, drel, stats, wg16, wu16, wd16, wo16 = _in_projection(
        x, mod, w_all, b_f_pad, g_gla.reshape(1, -1), w_gate, w_up, w_down, w_o)
    fox = _fox_attention(fqT, kaug, fvT, drel, stats)
    gla = _gla_attention(gq, gkT, gv, gate, small, w_a2_pad, b_a.reshape(1, -1))

    T = B * S
    out = _tail(x.reshape(T, D), fox.reshape(T, FOX_WIDTH), gla.reshape(T, GLA_WIDTH), mod,
                wo16,
                ln1_g.reshape(1, D), ln1_b.reshape(1, D),
                wg16, wu16, wd16,
                ln2_g.reshape(1, D), ln2_b.reshape(1, D), S // TAIL_TM)
    return out.reshape(B, S, D)
```
